```python
import math
import jax
import jax.numpy as jnp
from jax import lax
import numpy as np

D_MODEL = 1024
BATCH = 4
SEQ = 4096
DEPTH = 2

S5_WIDTH = D_MODEL // 2
S5_GROUP = 16
S5_GROUPS = S5_WIDTH // S5_GROUP
S5_STATE = 64
ATTN_HEADS = 8
HEAD_DIM = 64
ATTN_WIDTH = ATTN_HEADS * HEAD_DIM
IDX_HEADS = 8
IDX_DIM = 64
TOPK_MAX = 256
Q_BLOCK = 128
ROPE_THETA = 500000.0
ROPE_DIM = HEAD_DIM // 4
MIX_WIDTH = S5_WIDTH + ATTN_WIDTH
IN_SPLITS = (S5_WIDTH, ATTN_WIDTH, HEAD_DIM, HEAD_DIM, IDX_HEADS * IDX_DIM, IDX_DIM, IDX_HEADS)
IN_WIDTH = sum(IN_SPLITS)
POOL_WINDOWS = (2, 4, 8, 16)
POOL_GROUP = D_MODEL // len(POOL_WINDOWS)
N_EXPERTS = 64
TOP_K = 8
N_EXPERT_GROUPS = 8
TOPK_GROUPS = 4
EXPERT_DIM = 256
SHARED_DIM = 256
ROUTED_SCALE = 2.5
ROW_BLOCK = 128
N_EVEN = (DEPTH + 1) // 2
N_ODD = DEPTH // 2
EPS = 1e-6

kernel_name = 'hybrid_s5_dsa_pool_moe_trunk'


def rms_norm(x, g):
    xf = x.astype(jnp.float32)
    y = xf * lax.rsqrt(jnp.mean(xf * xf, axis=-1, keepdims=True) + EPS)
    return (y * g.astype(jnp.float32)).astype(x.dtype)


def ada_mod(cs, w, b):
    return jnp.split(cs @ w + b, 3, axis=-1)


def modulate(n, shift, scale):
    return n * (1.0 + scale[:, None, :]) + shift[:, None, :]


def rope_tables(positions, rot_dim):
    inv = jnp.power(jnp.float32(ROPE_THETA), -jnp.arange(0, rot_dim, 2, dtype=jnp.float32) / rot_dim)
    ang = positions.astype(jnp.float32)[..., None] * inv
    return jnp.cos(ang)[:, :, None, :], jnp.sin(ang)[:, :, None, :]


def partial_rope(x, cos, sin):
    half = cos.shape[-1]
    xr = x[..., :2 * half].astype(jnp.float32)
    x1, x2 = xr[..., :half], xr[..., half:]
    rot = jnp.concatenate([x1 * cos - x2 * sin, x1 * sin + x2 * cos], axis=-1)
    return jnp.concatenate([rot.astype(x.dtype), x[..., 2 * half:]], axis=-1)


def _ssm_combine(e1, e2):
    a1r, a1i, b1r, b1i = e1
    a2r, a2i, b2r, b2i = e2
    return (a2r * a1r - a2i * a1i, a2r * a1i + a2i * a1r,
            a2r * b1r - a2i * b1i + b2r, a2r * b1i + a2i * b1r + b2i)


def s5_mixer(u, lam_re, lam_im, log_dt, b_re, b_im, c_re, c_im, d_skip, glu_w, glu_b):
    bsz, seq, _ = u.shape
    uf = u.astype(jnp.float32).reshape(bsz, seq, S5_GROUPS, S5_GROUP)
    lr = lam_re.astype(jnp.float32)
    li = lam_im.astype(jnp.float32)
    dt = jnp.exp(log_dt.astype(jnp.float32))[:, None]
    mag = jnp.exp(lr * dt)
    ar, ai = mag * jnp.cos(li * dt), mag * jnp.sin(li * dt)
    nr, ni = ar - 1.0, ai
    den = lr * lr + li * li
    cr, ci = (nr * lr + ni * li) / den, (ni * lr - nr * li) / den
    bre, bim = b_re.astype(jnp.float32), b_im.astype(jnp.float32)
    bbar_r = cr[..., None] * bre - ci[..., None] * bim
    bbar_i = cr[..., None] * bim + ci[..., None] * bre
    bu_r = jnp.einsum('gph,bsgh->bsgp', bbar_r, uf)
    bu_i = jnp.einsum('gph,bsgh->bsgp', bbar_i, uf)
    a_r = jnp.broadcast_to(ar, bu_r.shape)
    a_i = jnp.broadcast_to(ai, bu_i.shape)
    _, _, st_r, st_i = lax.associative_scan(_ssm_combine, (a_r, a_i, bu_r, bu_i), axis=1)
    y = (jnp.einsum('ghp,bsgp->bsgh', c_re.astype(jnp.float32), st_r)
         - jnp.einsum('ghp,bsgp->bsgh', c_im.astype(jnp.float32), st_i)
         + d_skip.astype(jnp.float32) * uf)
    y = jax.nn.gelu(y)
    y = y * jax.nn.sigmoid(jnp.einsum('bsgh,ghk->bsgk', y, glu_w.astype(jnp.float32)) + glu_b.astype(jnp.float32))
    return y.reshape(bsz, seq, S5_WIDTH).astype(u.dtype)


def dsa_mixer(q, k, v, qi, ki, wi, q_norm_g, k_norm_g, positions):
    bsz, seq, _ = q.shape
    cos, sin = rope_tables(positions, ROPE_DIM)
    q = partial_rope(rms_norm(q.reshape(bsz, seq, ATTN_HEADS, HEAD_DIM), q_norm_g), cos, sin)
    k = partial_rope(rms_norm(k, k_norm_g)[:, :, None, :], cos, sin)[:, :, 0]
    qi = partial_rope(qi.reshape(bsz, seq, IDX_HEADS, IDX_DIM), cos, sin).astype(jnp.float32)
    ki = partial_rope(ki[:, :, None, :], cos, sin)[:, :, 0].astype(jnp.float32)
    wi = wi.astype(jnp.float32) * IDX_HEADS ** -0.5
    k_sel = min(TOPK_MAX, seq // 4)
    n_blk = seq // Q_BLOCK
    key_pos = jnp.arange(seq, dtype=jnp.int32)

    def to_blocks(a):
        return jnp.moveaxis(a.reshape(bsz, n_blk, Q_BLOCK, *a.shape[2:]), 1, 0)

    def block(args):
        qb, qib, wb, tb = args
        iscore = jnp.einsum('bqhd,bsd->bqhs', qib, ki) * IDX_DIM ** -0.5
        iscore = jnp.einsum('bqh,bqhs->bqs', wb, jax.nn.relu(iscore))
        iscore = jnp.where(key_pos[None, None, :] <= tb[None, :, None], iscore, -jnp.inf)
        _, sel = lax.top_k(iscore, k_sel)
        valid = sel <= tb[None, :, None]
        kg = jax.vmap(lambda kk, ii: kk[ii])(k, sel)
        vg = jax.vmap(lambda vv, ii: vv[ii])(v, sel)
        sc = jnp.einsum('bqhd,bqkd->bqhk', qb, kg).astype(jnp.float32) * HEAD_DIM ** -0.5
        sc = jnp.where(valid[:, :, None, :], sc, -jnp.inf)
        p = jax.nn.softmax(sc, axis=-1).astype(vg.dtype)
        return jnp.einsum('bqhk,bqkd->bqhd', p, vg)

    q_pos = jnp.arange(seq, dtype=jnp.int32).reshape(n_blk, Q_BLOCK)
    out = lax.map(block, (to_blocks(q), to_blocks(qi), to_blocks(wi), q_pos))
    return jnp.moveaxis(out, 0, 1).reshape(bsz, seq, ATTN_WIDTH)


def hybrid_mixer(h, positions, w_in, w_out, lam_re, lam_im, log_dt, b_re, b_im, c_re, c_im,
                 d_skip, glu_w, glu_b, q_norm_g, k_norm_g):
    proj = h @ w_in
    cuts = np.cumsum(IN_SPLITS)[:-1].tolist()
    u, q, k, v, qi, ki, wi = jnp.split(proj, cuts, axis=-1)
    y_ssm = s5_mixer(u, lam_re, lam_im, log_dt, b_re, b_im, c_re, c_im, d_skip, glu_w, glu_b)
    y_att = dsa_mixer(q, k, v, qi, ki, wi, q_norm_g, k_norm_g, positions)
    return jnp.concatenate([y_ssm, y_att], axis=-1) @ w_out


def pool_mixer(h, pool_w, pool_scale):
    bsz, seq, d = h.shape
    hf = h.astype(jnp.float32)
    cs = jnp.concatenate([jnp.zeros((bsz, 1, d), jnp.float32), jnp.cumsum(hf, axis=1)], axis=1)
    hi = np.arange(1, seq + 1)
    outs = []
    for g, win in enumerate(POOL_WINDOWS):
        lo = np.maximum(hi - win, 0)
        cnt = (hi - lo).astype(np.float32)
        sl = slice(g * POOL_GROUP, (g + 1) * POOL_GROUP)
        csg = cs[:, :, sl]
        pooled = (csg[:, hi] - csg[:, lo]) / cnt[None, :, None] - hf[:, :, sl]
        outs.append(jnp.einsum('bsc,cd->bsd', pooled.astype(h.dtype), pool_w[g]))
    return jnp.concatenate(outs, axis=-1) * pool_scale


def moe_ffn(h, router_w, router_bias, w_gate, w_up, w_down, sh_gate, sh_up, sh_down):
    bsz, seq, d = h.shape
    n_tok = bsz * seq
    xt = h.reshape(n_tok, d)
    scores = jax.nn.sigmoid((xt @ router_w).astype(jnp.float32))
    sel = (scores + router_bias.astype(jnp.float32)).reshape(n_tok, N_EXPERT_GROUPS, -1)
    group_score = lax.top_k(sel, 2)[0].sum(-1)
    _, top_groups = lax.top_k(group_score, TOPK_GROUPS)
    group_mask = jax.nn.one_hot(top_groups, N_EXPERT_GROUPS, dtype=jnp.float32).sum(1) > 0
    sel = jnp.where(group_mask[:, :, None], sel, -jnp.inf).reshape(n_tok, N_EXPERTS)
    _, expert_idx = lax.top_k(sel, TOP_K)
    gate_w = jnp.take_along_axis(scores, expert_idx, axis=-1)
    gate_w = gate_w / jnp.sum(gate_w, axis=-1, keepdims=True) * ROUTED_SCALE
    n_assign = n_tok * TOP_K
    flat_e = expert_idx.reshape(n_assign)
    flat_tok = jnp.repeat(jnp.arange(n_tok, dtype=jnp.int32), TOP_K)
    flat_w = gate_w.reshape(n_assign)
    order = jnp.argsort(flat_e)
    se, stok, sw = flat_e[order], flat_tok[order], flat_w[order]
    counts = jnp.bincount(flat_e, length=N_EXPERTS)
    padded = (counts + ROW_BLOCK - 1) // ROW_BLOCK * ROW_BLOCK
    pad_end = jnp.cumsum(padded)
    pad_start = pad_end - padded
    start = jnp.cumsum(counts) - counts
    dest = pad_start[se] + jnp.arange(n_assign, dtype=jnp.int32) - start[se]
    n_blocks = -(-n_assign // ROW_BLOCK) + N_EXPERTS
    n_rows = n_blocks * ROW_BLOCK
    row_tok = jnp.full((n_rows,), n_tok, jnp.int32).at[dest].set(stok)
    row_w = jnp.zeros((n_rows,), jnp.float32).at[dest].set(sw)
    block_e = jnp.minimum(jnp.searchsorted(pad_end, jnp.arange(n_blocks, dtype=jnp.int32) * ROW_BLOCK, side='right'), N_EXPERTS - 1)
    x_pad = jnp.concatenate([xt, jnp.zeros((1, d), xt.dtype)], axis=0)

    def expert_block(args):
        rows, wr, e = args
        xb = x_pad[rows]
        hb = jax.nn.silu(xb @ w_gate[e]) * (xb @ w_up[e])
        return (hb @ w_down[e]) * wr[:, None].astype(xb.dtype)

    y = lax.map(expert_block, (row_tok.reshape(n_blocks, ROW_BLOCK), row_w.reshape(n_blocks, ROW_BLOCK), block_e))
    routed = jax.ops.segment_sum(y.reshape(n_rows, d), row_tok, num_segments=n_tok + 1)[:n_tok]
    shared = (jax.nn.silu(xt @ sh_gate) * (xt @ sh_up)) @ sh_down
    return (routed + shared).reshape(bsz, seq, d)


def setup_inputs(seed: int = 0) -> dict:
    key = jax.random.key(seed)
    ks = iter(jax.random.split(key, 48))
    f32 = jnp.float32

    def nrm(shape, std):
        return jax.random.normal(next(ks), shape, f32) * std

    D, G, P, H, E = D_MODEL, S5_GROUPS, S5_STATE, S5_GROUP, N_EXPERTS
    x = nrm((BATCH, SEQ, D), 1.0)
    c = nrm((BATCH, D), 1.0)
    positions = jnp.broadcast_to(jnp.arange(SEQ, dtype=jnp.int32)[None, :], (BATCH, SEQ))
    mix_norm_g = 1.0 + nrm((DEPTH, D), 0.02)
    mix_mod_w = nrm((DEPTH, D, 3 * D), 0.5 * D ** -0.5)
    mix_mod_b = nrm((DEPTH, 3 * D), 0.02)
    ffn_norm_g = 1.0 + nrm((DEPTH, D), 0.02)
    ffn_mod_w = nrm((DEPTH, D, 3 * D), 0.5 * D ** -0.5)
    ffn_mod_b = nrm((DEPTH, 3 * D), 0.02)
    hyb_w_in = nrm((N_EVEN, D, IN_WIDTH), D ** -0.5)
    hyb_w_out = nrm((N_EVEN, MIX_WIDTH, D), MIX_WIDTH ** -0.5)
    s5_lambda_re = -0.5 + nrm((N_EVEN, G, P), 0.01)
    s5_lambda_im = math.pi * jnp.arange(P, dtype=f32) + nrm((N_EVEN, G, P), 0.01)
    s5_log_dt = jax.random.uniform(next(ks), (N_EVEN, G), f32, math.log(1e-3), math.log(1e-1))
    s5_b_re = nrm((N_EVEN, G, P, H), (2 * H) ** -0.5)
    s5_b_im = nrm((N_EVEN, G, P, H), (2 * H) ** -0.5)
    s5_c_re = nrm((N_EVEN, G, H, P), 4.0 * P ** -0.5)
    s5_c_im = nrm((N_EVEN, G, H, P), 4.0 * P ** -0.5)
    s5_d = nrm((N_EVEN, G, H), 1.0)
    s5_glu_w = nrm((N_EVEN, G, H, H), H ** -0.5)
    s5_glu_b = nrm((N_EVEN, G, H), 0.02)
    attn_q_norm_g = 1.0 + nrm((N_EVEN, HEAD_DIM), 0.02)
    attn_k_norm_g = 1.0 + nrm((N_EVEN, HEAD_DIM), 0.02)
    pool_w = nrm((N_ODD, len(POOL_WINDOWS), POOL_GROUP, POOL_GROUP), POOL_GROUP ** -0.5)
    pool_scale = 1.0 + nrm((N_ODD, D), 0.1)
    router_w = nrm((DEPTH, D, E), D ** -0.5)
    router_bias = nrm((DEPTH, E), 0.01)
    exp_w_gate = nrm((DEPTH, E, D, EXPERT_DIM), D ** -0.5)
    exp_w_up = nrm((DEPTH, E, D, EXPERT_DIM), D ** -0.5)
    exp_w_down = nrm((DEPTH, E, EXPERT_DIM, D), EXPERT_DIM ** -0.5)
    sh_w_gate = nrm((DEPTH, D, SHARED_DIM), D ** -0.5)
    sh_w_up = nrm((DEPTH, D, SHARED_DIM), D ** -0.5)
    sh_w_down = nrm((DEPTH, SHARED_DIM, D), SHARED_DIM ** -0.5)
    return {'x': x, 'c': c, 'positions': positions,
            'mix_norm_g': mix_norm_g, 'mix_mod_w': mix_mod_w, 'mix_mod_b': mix_mod_b,
            'ffn_norm_g': ffn_norm_g, 'ffn_mod_w': ffn_mod_w, 'ffn_mod_b': ffn_mod_b,
            'hyb_w_in': hyb_w_in, 'hyb_w_out': hyb_w_out,
            's5_lambda_re': s5_lambda_re, 's5_lambda_im': s5_lambda_im, 's5_log_dt': s5_log_dt,
            's5_b_re': s5_b_re, 's5_b_im': s5_b_im, 's5_c_re': s5_c_re, 's5_c_im': s5_c_im,
            's5_d': s5_d, 's5_glu_w': s5_glu_w, 's5_glu_b': s5_glu_b,
            'attn_q_norm_g': attn_q_norm_g, 'attn_k_norm_g': attn_k_norm_g,
            'pool_w': pool_w, 'pool_scale': pool_scale,
            'router_w': router_w, 'router_bias': router_bias,
            'exp_w_gate': exp_w_gate, 'exp_w_up': exp_w_up, 'exp_w_down': exp_w_down,
            'sh_w_gate': sh_w_gate, 'sh_w_up': sh_w_up, 'sh_w_down': sh_w_down}


def reference(x, c, positions, mix_norm_g, mix_mod_w, mix_mod_b, ffn_norm_g, ffn_mod_w, ffn_mod_b,
              hyb_w_in, hyb_w_out, s5_lambda_re, s5_lambda_im, s5_log_dt, s5_b_re, s5_b_im,
              s5_c_re, s5_c_im, s5_d, s5_glu_w, s5_glu_b, attn_q_norm_g, attn_k_norm_g,
              pool_w, pool_scale, router_w, router_bias, exp_w_gate, exp_w_up, exp_w_down,
              sh_w_gate, sh_w_up, sh_w_down):
    cs = jax.nn.silu(c)
    for i in range(DEPTH):
        j = i // 2
        shift, scale, gate = ada_mod(cs, mix_mod_w[i], mix_mod_b[i])
        h = modulate(rms_norm(x, mix_norm_g[i]), shift, scale)
        if i % 2 == 0:
            y = hybrid_mixer(h, positions, hyb_w_in[j], hyb_w_out[j], s5_lambda_re[j], s5_lambda_im[j],
                             s5_log_dt[j], s5_b_re[j], s5_b_im[j], s5_c_re[j], s5_c_im[j], s5_d[j],
                             s5_glu_w[j], s5_glu_b[j], attn_q_norm_g[j], attn_k_norm_g[j])
        else:
            y = pool_mixer(h, pool_w[j], pool_scale[j])
        x = x + gate[:, None, :] * y
        shift, scale, gate = ada_mod(cs, ffn_mod_w[i], ffn_mod_b[i])
        h = modulate(rms_norm(x, ffn_norm_g[i]), shift, scale)
        y = moe_ffn(h, router_w[i], router_bias[i], exp_w_gate[i], exp_w_up[i], exp_w_down[i],
                    sh_w_gate[i], sh_w_up[i], sh_w_down[i])
        x = x + gate[:, None, :] * y
    return x
```

```python
import functools
import math

import numpy as np
import jax
import jax.numpy as jnp
from jax import lax
from jax.experimental import pallas as pl
from jax.experimental.pallas import tpu as pltpu

F32 = jnp.float32
BF16 = jnp.bfloat16

EPS = 1e-6
S5_GROUP = 16
S5_STATE = 64
HEAD_DIM = 64
ATTN_HEADS = 8
IDX_HEADS = 8
ROPE_HALF = 8
ROPE_THETA = 500000.0
TOPK_MAX = 256
POOL_WINDOWS = (2, 4, 8, 16)
POOL_HALO = 16
N_EXPERT_GROUPS = 8
TOPK_GROUPS = 4
TOP_K = 8
ROUTED_SCALE = 2.5
S5_CHUNK = 16
NEG_BIG = -1e30
INT_MIN = -2147483648
VMEM_LIMIT = 56 * 1024 * 1024


def _cparams(sem, vmem=None):
    return pltpu.CompilerParams(dimension_semantics=sem, vmem_limit_bytes=vmem)


def _dot(a, b):
    return jnp.dot(a, b, preferred_element_type=F32)


def _dot_nt(a, b):
    return lax.dot_general(a, b, (((1,), (1,)), ((), ())), preferred_element_type=F32)


def _split_bf16(a):
    hi = a.astype(BF16)
    lo = (a - hi.astype(F32)).astype(BF16)
    return hi, lo


def _dot3(a, b):
    ah, al = _split_bf16(a)
    bh, bl = _split_bf16(b)
    return _dot(ah, bh) + (_dot(ah, bl) + _dot(al, bh))


def _norm_mod(x, g, shift, scale):
    y = x * lax.rsqrt(jnp.mean(x * x, axis=-1, keepdims=True) + EPS)
    return (y * g) * (1.0 + scale) + shift


def _mod_kernel(ct_ref, w_ref, b_ref, o_ref):
    ct = ct_ref[...]
    cs = ct * jax.nn.sigmoid(ct)
    w = w_ref[...]
    rows = [jnp.sum(w * cs[:, b:b + 1], axis=0, keepdims=True) for b in range(ct.shape[1])]
    o_ref[...] = jnp.concatenate(rows, axis=0) + b_ref[...]


def _mod_vectors(c, w, b):
    nl, d, n3 = w.shape
    bsz = c.shape[0]
    tn = 512
    return pl.pallas_call(
        _mod_kernel,
        grid=(nl, n3 // tn),
        in_specs=[pl.BlockSpec((d, bsz), lambda l, j: (0, 0)),
                  pl.BlockSpec((None, d, tn), lambda l, j: (l, 0, j)),
                  pl.BlockSpec((None, 1, tn), lambda l, j: (l, 0, j))],
        out_specs=pl.BlockSpec((None, bsz, tn), lambda l, j: (l, 0, j)),
        out_shape=jax.ShapeDtypeStruct((nl, bsz, n3), F32),
        compiler_params=_cparams(("arbitrary", "arbitrary")),
        name="mod_vectors",
    )(c.T, w, b.reshape(nl, 1, n3))


def _inproj_kernel(x_ref, g_ref, mod_ref, w_ref, pos_ref, inv_ref, qg_ref, kg_ref,
                   u_ref, q_ref, qi_ref, k_ref, v_ref, ki_ref, wi_ref, *, d):
    x = x_ref[...]
    mod = mod_ref[...]
    h = _norm_mod(x, g_ref[...], mod[:, :d], mod[:, d:2 * d])
    proj = _dot(h.astype(BF16), w_ref[...])
    tm = x.shape[0]
    width = ATTN_HEADS * HEAD_DIM

    lane = lax.broadcasted_iota(jnp.int32, (1, 128), 1) & (HEAD_DIM - 1)
    ang = pos_ref[...].astype(F32) * inv_ref[...]
    cos, sin = jnp.cos(ang), jnp.sin(ang)
    t_a = jnp.where(lane < 2 * ROPE_HALF, cos, 1.0)
    t_b = jnp.where(lane < ROPE_HALF, -sin, 0.0)
    t_c = jnp.where((lane >= ROPE_HALF) & (lane < 2 * ROPE_HALF), sin, 0.0)

    def rope(a):
        return a * t_a + pltpu.roll(a, 128 - ROPE_HALF, 1) * t_b + pltpu.roll(a, ROPE_HALF, 1) * t_c

    u_ref[...] = proj[:, :width].astype(u_ref.dtype)
    q = proj[:, width:2 * width]
    qi = proj[:, 2 * width:3 * width]
    small = proj[:, 3 * width:3 * width + 256]
    qsq = q * q
    qg = qg_ref[...]
    for pair in range(ATTN_HEADS // 2):
        sl = slice(pair * 128, (pair + 1) * 128)
        qr = rope(q[:, sl] * qg)
        qir = rope(qi[:, sl])
        for half in range(2):
            hh = 2 * pair + half
            hs = slice(half * HEAD_DIM, (half + 1) * HEAD_DIM)
            ss = jnp.sum(qsq[:, hh * HEAD_DIM:(hh + 1) * HEAD_DIM], axis=-1, keepdims=True)
            r = lax.rsqrt(ss * (1.0 / HEAD_DIM) + EPS) * (HEAD_DIM ** -0.5)
            q_ref[hh] = (qr[:, hs] * r).astype(q_ref.dtype)
            qi_ref[hh] = qir[:, hs].astype(qi_ref.dtype)
    kv = small[:, :128]
    k = kv[:, :HEAD_DIM]
    rk = lax.rsqrt(jnp.mean(k * k, axis=-1, keepdims=True) + EPS)
    kr = rope(kv * kg_ref[...])
    k_ref[...] = (kr[:, :HEAD_DIM] * rk).astype(k_ref.dtype)
    v_ref[...] = kv[:, HEAD_DIM:].astype(v_ref.dtype)
    kiw = small[:, 128:256]
    ki_ref[...] = rope(kiw)[:, :HEAD_DIM].astype(ki_ref.dtype)
    wscale = (IDX_HEADS ** -0.5) * (HEAD_DIM ** -0.5)
    wi_ref[...] = pltpu.roll(kiw, HEAD_DIM, 1) * wscale


def _in_projection(x2, g, mod, layer, w_in, positions, q_norm_g, k_norm_g, bsz, seq):
    t, d = x2.shape
    tm = 512
    width = ATTN_HEADS * HEAD_DIM
    cuts = np.cumsum([width, width, HEAD_DIM, HEAD_DIM, width, HEAD_DIM, IDX_HEADS])
    u_w, q_w, k_w, v_w, qi_w, ki_w, wi_w = jnp.split(w_in, cuts[:-1].tolist(), axis=1)
    pad = jnp.zeros((d, 128 - HEAD_DIM - IDX_HEADS), w_in.dtype)
    w = jnp.concatenate([u_w, q_w, qi_w, k_w, v_w, ki_w, wi_w, pad], axis=1).astype(BF16)
    nw = w.shape[1]
    lane = np.arange(128) % HEAD_DIM
    inv = np.where(lane < 2 * ROPE_HALF,
                   np.power(ROPE_THETA, -2.0 * (lane % ROPE_HALF) / (2 * ROPE_HALF)), 0.0)
    inv = jnp.asarray(inv.reshape(1, 128), F32)
    qg2 = jnp.tile(q_norm_g.reshape(1, HEAD_DIM), (1, 2))
    kg2 = jnp.concatenate([k_norm_g.reshape(1, HEAD_DIM), jnp.ones((1, HEAD_DIM), F32)], axis=1)
    tiles_per_seq = seq // tm
    row = lambda i: (i, 0)
    const = lambda i: (0, 0)
    hm = lambda i: (i // tiles_per_seq, 0, i % tiles_per_seq, 0)
    outs = pl.pallas_call(
        functools.partial(_inproj_kernel, d=d),
        grid=(t // tm,),
        in_specs=[pl.BlockSpec((tm, d), row),
                  pl.BlockSpec((1, d), const),
                  pl.BlockSpec((None, None, 1, 3 * d), lambda i: (layer, i // tiles_per_seq, 0, 0)),
                  pl.BlockSpec((d, nw), const),
                  pl.BlockSpec((tm, 1), row),
                  pl.BlockSpec((1, 128), const),
                  pl.BlockSpec((1, 128), const),
                  pl.BlockSpec((1, 128), const)],
        out_specs=[pl.BlockSpec((tm, width), row),
                   pl.BlockSpec((None, ATTN_HEADS, tm, HEAD_DIM), hm),
                   pl.BlockSpec((None, IDX_HEADS, tm, HEAD_DIM), hm),
                   pl.BlockSpec((tm, HEAD_DIM), row),
                   pl.BlockSpec((tm, HEAD_DIM), row),
                   pl.BlockSpec((tm, HEAD_DIM), row),
                   pl.BlockSpec((tm, 128), row)],
        out_shape=[jax.ShapeDtypeStruct((t, width), BF16),
                   jax.ShapeDtypeStruct((bsz, ATTN_HEADS, seq, HEAD_DIM), BF16),
                   jax.ShapeDtypeStruct((bsz, IDX_HEADS, seq, HEAD_DIM), BF16),
                   jax.ShapeDtypeStruct((t, HEAD_DIM), BF16),
                   jax.ShapeDtypeStruct((t, HEAD_DIM), BF16),
                   jax.ShapeDtypeStruct((t, HEAD_DIM), BF16),
                   jax.ShapeDtypeStruct((t, 128), F32)],
        compiler_params=_cparams(("arbitrary",), VMEM_LIMIT),
        name="in_projection",
    )(x2, g.reshape(1, d), mod, w, positions.reshape(t, 1), inv, qg2, kg2)
    return outs


def _s5_weights(lam_re, lam_im, log_dt, b_re, b_im, c_re, c_im, d_skip, glu_w, glu_b):
    L = S5_CHUNK
    g, p = lam_re.shape
    hch = S5_GROUP
    dt = jnp.exp(log_dt)[:, None]
    lr, li = lam_re, lam_im
    tau = jnp.arange(L + 1, dtype=F32)[:, None, None]
    mag = jnp.exp(lr * dt * tau)
    pw_r, pw_i = mag * jnp.cos(li * dt * tau), mag * jnp.sin(li * dt * tau)
    ar, ai = pw_r[1], pw_i[1]
    nr, ni = ar - 1.0, ai
    den = lr * lr + li * li
    cr, ci = (nr * lr + ni * li) / den, (ni * lr - nr * li) / den
    bb_r = cr[..., None] * b_re - ci[..., None] * b_im
    bb_i = cr[..., None] * b_im + ci[..., None] * b_re
    cl_r = c_re[None] * pw_r[:L, :, None, :] - c_im[None] * pw_i[:L, :, None, :]
    cl_i = c_re[None] * pw_i[:L, :, None, :] + c_im[None] * pw_r[:L, :, None, :]
    taps = (jnp.einsum('tghp,gpk->tghk', cl_r, bb_r) - jnp.einsum('tghp,gpk->tghk', cl_i, bb_i))
    ii = jnp.arange(L)
    lag = ii[None, :] - ii[:, None]
    tap_ji = taps[jnp.clip(lag, 0, L - 1)]
    tap_ji = jnp.where((lag >= 0)[:, :, None, None, None], tap_ji, 0.0)
    m_intra = jnp.transpose(tap_ji, (2, 0, 4, 1, 3)).reshape(g, L * hch, L * hch)
    rev_r, rev_i = pw_r[L - 1 - ii], pw_i[L - 1 - ii]
    z_r = rev_r[..., None] * bb_r[None] - rev_i[..., None] * bb_i[None]
    z_i = rev_r[..., None] * bb_i[None] + rev_i[..., None] * bb_r[None]
    w_z = jnp.concatenate([jnp.transpose(z_r, (1, 0, 3, 2)).reshape(g, L * hch, p),
                           jnp.transpose(z_i, (1, 0, 3, 2)).reshape(g, L * hch, p)], axis=-1)
    w_a = jnp.concatenate([m_intra, w_z], axis=-1).astype(BF16)
    co_r = c_re[None] * pw_r[1:, :, None, :] - c_im[None] * pw_i[1:, :, None, :]
    co_i = c_re[None] * pw_i[1:, :, None, :] + c_im[None] * pw_r[1:, :, None, :]
    w_c = jnp.concatenate([jnp.transpose(co_r, (1, 3, 0, 2)).reshape(g, p, L * hch),
                           -jnp.transpose(co_i, (1, 3, 0, 2)).reshape(g, p, L * hch)], axis=1)
    al_r, al_i = pw_r[L], pw_i[L]
    dec_a = jnp.concatenate([al_r, al_r], axis=-1)[:, None, :]
    dec_b = jnp.concatenate([-al_i, al_i], axis=-1)[:, None, :]
    d_t = jnp.tile(d_skip[:, None, :], (1, L, 1)).reshape(g, 1, L * hch)
    glu_k = jnp.einsum('ij,ghk->gihjk', jnp.eye(L, dtype=F32), glu_w).reshape(g, L * hch, L * hch).astype(BF16)
    glu_bt = jnp.tile(glu_b[:, None, :], (1, L, 1)).reshape(g, 1, L * hch)
    return w_a, w_c, dec_a, dec_b, d_t, glu_k, glu_bt


def _s5_kernel(u_ref, wa_ref, wc_ref, da_ref, db_ref, d_ref, gk_ref, gb_ref, o_ref,
               zp_ref, zq_ref, sp_ref, *, bsz, nchunk):
    u = u_ref[...]
    cols = u.shape[1]
    p2 = zp_ref.shape[1]
    res = _dot(u, wa_ref[...])
    y_intra = res[:, :cols]
    z = res[:, cols:]
    zp_ref[...] = z
    zq_ref[...] = pltpu.roll(z, p2 // 2, 1)
    da = da_ref[...]
    db = db_ref[...]
    dbq = -db

    def step(c, carry):
        new = []
        for b in range(bsz):
            sp, sq = carry[2 * b], carry[2 * b + 1]
            r = b * nchunk + c
            sp_ref[pl.ds(r, 1), :] = sp
            zp = zp_ref[pl.ds(r, 1), :]
            zq = zq_ref[pl.ds(r, 1), :]
            new.append(da * sp + db * sq + zp)
            new.append(da * sq + dbq * sp + zq)
        return tuple(new)

    zero = jnp.zeros((1, p2), F32)
    lax.fori_loop(0, nchunk, step, (zero,) * (2 * bsz))
    s_prev = sp_ref[...]
    sh, sl = _split_bf16(s_prev)
    wc = wc_ref[...]
    wch, wcl = _split_bf16(wc)
    y = y_intra + (_dot(sh, wch) + (_dot(sh, wcl) + _dot(sl, wch))) + d_ref[...] * u.astype(F32)
    y = jax.nn.gelu(y)
    gate = jax.nn.sigmoid(_dot(y.astype(BF16), gk_ref[...]) + gb_ref[...])
    o_ref[...] = (y * gate).astype(o_ref.dtype)


def _s5_mixer(u2, weights, bsz, seq):
    w_a, w_c, dec_a, dec_b, d_t, glu_k, glu_bt = weights
    g = w_a.shape[0]
    r, _ = u2.shape
    cols = S5_CHUNK * S5_GROUP
    p2 = 2 * S5_STATE
    nchunk = seq // S5_CHUNK
    grp = lambda i: (i, 0, 0)
    return pl.pallas_call(
        functools.partial(_s5_kernel, bsz=bsz, nchunk=nchunk),
        grid=(g,),
        in_specs=[pl.BlockSpec((r, cols), lambda i: (0, i)),
                  pl.BlockSpec((None, cols, cols + p2), grp),
                  pl.BlockSpec((None, p2, cols), grp),
                  pl.BlockSpec((None, 1, p2), grp),
                  pl.BlockSpec((None, 1, p2), grp),
                  pl.BlockSpec((None, 1, cols), grp),
                  pl.BlockSpec((None, cols, cols), grp),
                  pl.BlockSpec((None, 1, cols), grp)],
        out_specs=pl.BlockSpec((r, cols), lambda i: (0, i)),
        out_shape=jax.ShapeDtypeStruct(u2.shape, BF16),
        scratch_shapes=[pltpu.VMEM((r, p2), F32), pltpu.VMEM((r, p2), F32), pltpu.VMEM((r, p2), F32)],
        compiler_params=_cparams(("arbitrary",)),
        name="s5_mixer",
    )(u2, w_a, w_c, dec_a, dec_b, d_t, glu_k, glu_bt)


def _dsa_kernel(q_ref, qi_ref, wi_ref, kt_ref, v_ref, kit_ref, o_ref,
                sc_ref, s_ref, wb_ref, m_ref, l_ref, acc_ref, *, tq, ck, k_sel, idx_bits, max_iter):
    i = pl.program_id(1)
    nheads = q_ref.shape[0]
    nj = ck // 128
    n_ck = (i * tq + tq + ck - 1) // ck
    shape = (tq, 128)
    t_pos = i * tq + lax.broadcasted_iota(jnp.int32, shape, 0)
    lane = lax.broadcasted_iota(jnp.int32, shape, 1)
    kf = float(k_sel)
    inf = jnp.inf

    def rep(col):
        return jnp.broadcast_to(col, shape)

    def cols(x):
        return [x[:, j * 128:(j + 1) * 128] for j in range(nj)]

    w = wi_ref[...]
    for h in range(nheads):
        wb_ref[h] = rep(w[:, h:h + 1])

    def score_chunk(c, carry):
        lo, hi = carry
        kit = kit_ref[c]
        accs = [jnp.zeros(shape, F32) for _ in range(nj)]
        for h in range(nheads):
            s = jnp.maximum(_dot(qi_ref[h], kit), 0.0)
            wb = wb_ref[h]
            accs = [a + wb * sj for a, sj in zip(accs, cols(s))]
        for j in range(nj):
            a = accs[j] + 0.0
            vis = (c * ck + j * 128) + lane <= t_pos
            sc_ref[c, :, j * 128:(j + 1) * 128] = jnp.where(vis, a, -inf)
            hi = jnp.maximum(hi, jnp.where(vis, a, -inf))
            lo = jnp.minimum(lo, jnp.where(vis, a, inf))
        return lo, hi

    lo, hi = lax.fori_loop(0, n_ck, score_chunk, (jnp.full(shape, inf, F32), jnp.full(shape, -inf, F32)))
    lo = rep(jnp.min(lo, axis=1, keepdims=True))
    hi = rep(jnp.max(hi, axis=1, keepdims=True))
    few = (t_pos + 1).astype(F32) < kf

    def reduce_rows(fn, init, combine, final):
        def body(c, part):
            for xj in cols(sc_ref[c]):
                part = combine(part, fn(xj))
            return part
        return rep(final(lax.fori_loop(0, n_ck, body, jnp.full(shape, init, F32)), axis=1, keepdims=True))

    def count(pred):
        return reduce_rows(lambda xj: jnp.where(pred(xj), 1.0, 0.0), 0.0, jnp.add, jnp.sum)

    def pivot(it, lo_, hi_, c_lo, c_hi):
        frac = jnp.clip((c_lo - (kf + 0.5)) / (c_lo - c_hi), 0.0625, 0.9375)
        frac = jnp.where(it % 2 == 0, 0.5, frac)
        mid = lo_ + (hi_ - lo_) * frac
        half = lo_ + (hi_ - lo_) * 0.5
        return jnp.where(mid > lo_, jnp.where(mid < hi_, mid, half), half)

    def pending(lo_, hi_, mid, c_lo):
        unresolved = jnp.where(few, 0.0, jnp.where(c_lo == kf, 0.0, 1.0))
        movable = jnp.where(mid > lo_, jnp.where(mid < hi_, 1.0, 0.0), 0.0)
        return jnp.max(unresolved), jnp.max(unresolved * movable)

    def search_cond(st):
        return jnp.logical_and(st[0] < max_iter, st[-1] > 0.5)

    def search_body(st):
        it, lo_, hi_, c_lo, c_hi, mid, _, _ = st
        cnt = count(lambda xj: xj >= mid)
        up = cnt >= kf
        lo2 = jnp.where(up, mid, lo_)
        hi2 = jnp.where(up, hi_, mid)
        c_lo2 = jnp.where(up, cnt, c_lo)
        c_hi2 = jnp.where(up, c_hi, cnt)
        mid2 = pivot(it + 1, lo2, hi2, c_lo2, c_hi2)
        open_, go = pending(lo2, hi2, mid2, c_lo2)
        return it + 1, lo2, hi2, c_lo2, c_hi2, mid2, open_, go

    c_lo0 = (t_pos + 1).astype(F32)
    c_hi0 = jnp.zeros(shape, F32)
    mid0 = pivot(jnp.int32(0), lo, hi, c_lo0, c_hi0)
    open0, go0 = pending(lo, hi, mid0, c_lo0)
    st = lax.while_loop(search_cond, search_body, (jnp.int32(0), lo, hi, c_lo0, c_hi0, mid0, open0, go0))
    lo, open_ = st[1], st[6]

    def resolve():
        def walk(thr_, strict):
            keep = (lambda xj: xj > thr_) if strict else (lambda xj: xj >= thr_)
            return reduce_rows(lambda xj: jnp.where(keep(xj), xj, inf), inf, jnp.minimum, jnp.min)

        def above_equal(thr_):
            return count(lambda xj: xj > thr_), count(lambda xj: xj == thr_)

        def too_low(n_gt):
            return jnp.where(few, 0.0, jnp.where(n_gt >= kf, 1.0, 0.0))

        thr0 = walk(lo, False)
        gt0, eq0 = above_equal(thr0)

        def walk_body(st):
            thr_, n_gt, _ = st
            thr2 = jnp.where(too_low(n_gt) > 0.5, walk(thr_, True), thr_)
            return (thr2,) + above_equal(thr2)

        thr_, n_gt, n_eq = lax.while_loop(lambda st: jnp.max(too_low(st[1])) > 0.5, walk_body, (thr0, gt0, eq0))
        need = kf - n_gt

        def idx_step(b, cur):
            trial = cur | (jnp.int32(1) << (idx_bits - 1 - b))

            def body(c, part):
                for j, xj in enumerate(cols(sc_ref[c])):
                    idx = (c * ck + j * 128) + lane
                    part = part + jnp.where(xj == thr_, jnp.where(idx < trial, 1.0, 0.0), 0.0)
                return part
            cnt = rep(jnp.sum(lax.fori_loop(0, n_ck, body, jnp.zeros(shape, F32)), axis=1, keepdims=True))
            return jnp.where(cnt < need, trial, cur)

        cut_ = lax.fori_loop(0, idx_bits, idx_step, jnp.zeros(shape, jnp.int32))
        return thr_, cut_

    thr, cut = lax.cond(open_ > 0.5, resolve, lambda: (lo, jnp.full(shape, 0x7FFFFFFF, jnp.int32)))

    def bias_chunk(c, _):
        x = sc_ref[c]
        for j, xj in enumerate(cols(x)):
            idx = (c * ck + j * 128) + lane
            at_thr = jnp.where(xj == thr, jnp.where(idx <= cut, 0.0, NEG_BIG), NEG_BIG)
            sc_ref[c, :, j * 128:(j + 1) * 128] = jnp.where(xj > thr, 0.0, at_thr)
        return 0

    lax.fori_loop(0, n_ck, bias_chunk, 0)

    for h in range(nheads):
        m_ref[h] = jnp.full(shape, NEG_BIG, F32)
        l_ref[h] = jnp.zeros(shape, F32)
        acc_ref[h] = jnp.zeros((tq, HEAD_DIM), F32)

    def attn_scores(c, _):
        kt = kt_ref[c]
        bias = cols(sc_ref[c])
        for h in range(nheads):
            s = cols(_dot(q_ref[h], kt))
            part = m_ref[h]
            for j in range(nj):
                sj = s[j] + bias[j]
                s_ref[h, c, :, j * 128:(j + 1) * 128] = sj
                part = jnp.maximum(part, sj)
            m_ref[h] = part
        return 0

    lax.fori_loop(0, n_ck, attn_scores, 0)
    for h in range(nheads):
        m_ref[h] = rep(jnp.max(m_ref[h], axis=1, keepdims=True))

    def attn_values(c, _):
        off = pl.multiple_of(c * ck, ck)
        vc = v_ref[pl.ds(off, ck), :]
        for h in range(nheads):
            m = m_ref[h]
            p = [jnp.exp(sj - m) for sj in cols(s_ref[h, c])]
            l_ref[h] += functools.reduce(jnp.add, p)
            acc_ref[h] += _dot(jnp.concatenate(p, axis=1).astype(BF16), vc)
        return 0

    lax.fori_loop(0, n_ck, attn_values, 0)
    for h in range(nheads):
        l_row = jnp.sum(l_ref[h], axis=1, keepdims=True)
        o_ref[:, h * HEAD_DIM:(h + 1) * HEAD_DIM] = (acc_ref[h] / l_row).astype(o_ref.dtype)


def _dsa_attention(q_hm, qi_hm, wi, k, v, ki, bsz, seq):
    tq = 128
    ck = min(512, seq)
    n_ck = seq // ck
    k_sel = min(TOPK_MAX, seq // 4)
    nq = seq // tq
    width = ATTN_HEADS * HEAD_DIM
    kt = k.reshape(bsz, n_ck, ck, HEAD_DIM).transpose(0, 1, 3, 2)
    kit = ki.reshape(bsz, n_ck, ck, HEAD_DIM).transpose(0, 1, 3, 2)
    hm = lambda b, i: (b, 0, i, 0)
    per_batch = lambda b, i: (b, 0, 0, 0)
    nh = ATTN_HEADS
    return pl.pallas_call(
        functools.partial(_dsa_kernel, tq=tq, ck=ck, k_sel=k_sel, idx_bits=(seq - 1).bit_length(), max_iter=48),
        grid=(bsz, nq),
        in_specs=[pl.BlockSpec((None, ATTN_HEADS, tq, HEAD_DIM), hm),
                  pl.BlockSpec((None, IDX_HEADS, tq, HEAD_DIM), hm),
                  pl.BlockSpec((tq, 128), lambda b, i: (b * nq + i, 0)),
                  pl.BlockSpec((None, n_ck, HEAD_DIM, ck), per_batch),
                  pl.BlockSpec((seq, HEAD_DIM), lambda b, i: (b, 0)),
                  pl.BlockSpec((None, n_ck, HEAD_DIM, ck), per_batch)],
        out_specs=pl.BlockSpec((tq, width), lambda b, i: (b * nq + i, 0)),
        out_shape=jax.ShapeDtypeStruct((bsz * seq, width), BF16),
        scratch_shapes=[pltpu.VMEM((n_ck, tq, ck), F32),
                        pltpu.VMEM((nh, n_ck, tq, ck), F32),
                        pltpu.VMEM((IDX_HEADS, tq, 128), F32),
                        pltpu.VMEM((nh, tq, 128), F32),
                        pltpu.VMEM((nh, tq, 128), F32),
                        pltpu.VMEM((nh, tq, HEAD_DIM), F32)],
        compiler_params=_cparams(("arbitrary", "arbitrary"), VMEM_LIMIT),
        name="dsa_attention",
    )(q_hm, qi_hm, wi, kt, v, kit)


def _outproj_kernel(x_ref, ys_ref, ya_ref, w_ref, mod_ref, o_ref, *, d):
    width = ys_ref.shape[1]
    y = _dot(ys_ref[...], w_ref[:width, :]) + _dot(ya_ref[...], w_ref[width:, :])
    o_ref[...] = x_ref[...] + mod_ref[...][:, 2 * d:] * y


def _out_projection(x2, y_ssm, y_att, w_out, mod, layer, seq):
    t, d = x2.shape
    tm = 512
    width = y_ssm.shape[1]
    tiles_per_seq = seq // tm
    row = lambda i: (i, 0)
    return pl.pallas_call(
        functools.partial(_outproj_kernel, d=d),
        grid=(t // tm,),
        in_specs=[pl.BlockSpec((tm, d), row),
                  pl.BlockSpec((tm, width), row),
                  pl.BlockSpec((tm, width), row),
                  pl.BlockSpec((2 * width, d), lambda i: (0, 0)),
                  pl.BlockSpec((None, None, 1, 3 * d), lambda i: (layer, i // tiles_per_seq, 0, 0))],
        out_specs=pl.BlockSpec((tm, d), row),
        out_shape=jax.ShapeDtypeStruct((t, d), F32),
        compiler_params=_cparams(("arbitrary",)),
        name="out_projection",
    )(x2, y_ssm, y_att, w_out.astype(BF16), mod)


def _pool_kernel(x_ref, halo_ref, g_ref, mod_ref, pw_ref, ps_ref, o_ref, h_ref, *, d, tiles_per_seq):
    i = pl.program_id(0)
    tm = x_ref.shape[0]
    mod = mod_ref[...]
    shift, scale, gate = mod[:, :d], mod[:, d:2 * d], mod[:, 2 * d:]
    x = x_ref[...]
    first = (i % tiles_per_seq) == 0
    h_ref[POOL_HALO:, :] = _norm_mod(x, g_ref[...], shift, scale)
    halo = _norm_mod(halo_ref[...], g_ref[...], shift, scale)
    h_ref[:POOL_HALO, :] = jnp.where(first, 0.0, halo)
    pos = (i % tiles_per_seq) * tm + lax.broadcasted_iota(jnp.int32, (tm, 1), 0)
    gw = d // len(POOL_WINDOWS)
    ys = []
    for gi, win in enumerate(POOL_WINDOWS):
        cs = slice(gi * gw, (gi + 1) * gw)
        cur = h_ref[POOL_HALO:, cs]
        tot = cur
        for k in range(1, win):
            tot = tot + h_ref[POOL_HALO - k:POOL_HALO - k + tm, cs]
        cnt = jnp.minimum(pos + 1, win).astype(F32)
        pooled = tot / cnt - cur
        ys.append(_dot(pooled.astype(BF16), pw_ref[gi]))
    y = jnp.concatenate(ys, axis=1) * ps_ref[...]
    o_ref[...] = x + gate * y


def _pool_layer(x2, g, mod, layer, pool_w, pool_scale, seq):
    t, d = x2.shape
    tm = 512
    tiles_per_seq = seq // tm
    gw = d // len(POOL_WINDOWS)
    hb = tm // POOL_HALO
    return pl.pallas_call(
        functools.partial(_pool_kernel, d=d, tiles_per_seq=tiles_per_seq),
        grid=(t // tm,),
        in_specs=[pl.BlockSpec((tm, d), lambda i: (i, 0)),
                  pl.BlockSpec((POOL_HALO, d), lambda i: (jnp.maximum(i * hb - 1, 0), 0)),
                  pl.BlockSpec((1, d), lambda i: (0, 0)),
                  pl.BlockSpec((None, None, 1, 3 * d), lambda i: (layer, i // tiles_per_seq, 0, 0)),
                  pl.BlockSpec((len(POOL_WINDOWS), gw, gw), lambda i: (0, 0, 0)),
                  pl.BlockSpec((1, d), lambda i: (0, 0))],
        out_specs=pl.BlockSpec((tm, d), lambda i: (i, 0)),
        out_shape=jax.ShapeDtypeStruct((t, d), F32),
        scratch_shapes=[pltpu.VMEM((tm + POOL_HALO, d), F32)],
        compiler_params=_cparams(("arbitrary",)),
        name="pool_mixer",
    )(x2, x2, g.reshape(1, d), mod, pool_w.astype(BF16), pool_scale.reshape(1, d))


def _first_max(vals, idx, big):
    m = jnp.max(vals, axis=0, keepdims=True)
    first = jnp.min(jnp.where(vals == m, idx, big), axis=0, keepdims=True)
    return m, first


def _router_kernel(x_ref, g_ref, mod_ref, rw_ref, rb_ref, h_ref, gt_ref, *, d, n_exp):
    mod = mod_ref[...]
    h = _norm_mod(x_ref[...], g_ref[...], mod[:, :d], mod[:, d:2 * d])
    h_ref[...] = h.astype(h_ref.dtype)
    tm = h.shape[0]
    hh, hl = _split_bf16(h)
    rw = rw_ref[...]
    rh, rl = _split_bf16(rw)
    logits = _dot_nt(rh, hh) + (_dot_nt(rh, hl) + _dot_nt(rl, hh))
    scores = jax.nn.sigmoid(logits)
    sel = scores + rb_ref[...]
    gsz = n_exp // N_EXPERT_GROUPS
    neg = -jnp.inf
    shape3 = (N_EXPERT_GROUPS, gsz, tm)
    sel3 = sel.reshape(shape3)
    sub = lax.broadcasted_iota(jnp.int32, shape3, 1)
    m1 = jnp.max(sel3, axis=1, keepdims=True)
    f1 = jnp.min(jnp.where(sel3 == m1, sub, gsz), axis=1, keepdims=True)
    m2 = jnp.max(jnp.where(sub == f1, neg, sel3), axis=1, keepdims=True)
    gscore = jnp.broadcast_to(m1 + m2, shape3).reshape(n_exp, tm)
    eidx = lax.broadcasted_iota(jnp.int32, (n_exp, tm), 0)
    gidx = eidx // gsz
    keep = jnp.zeros((n_exp, tm), F32)
    work = gscore
    for _ in range(TOPK_GROUPS):
        _, first = _first_max(work, gidx, N_EXPERT_GROUPS)
        hit = gidx == first
        keep = jnp.where(hit, 1.0, keep)
        work = jnp.where(hit, neg, work)
    work = jnp.where(keep > 0.0, sel, neg)
    chosen = jnp.zeros((n_exp, tm), F32)
    for _ in range(TOP_K):
        _, first = _first_max(work, eidx, n_exp)
        hit = eidx == first
        chosen = jnp.where(hit, 1.0, chosen)
        work = jnp.where(hit, neg, work)
    picked = chosen * scores
    gates = picked / jnp.sum(picked, axis=0, keepdims=True) * ROUTED_SCALE
    gpad = jnp.concatenate([gates, jnp.zeros((128 - n_exp, tm), F32)], axis=0)
    gt_ref[...] = gpad.T


def _ffn_router(x2, g, mod, layer, router_w, router_bias, seq):
    t, d = x2.shape
    n_exp = router_w.shape[1]
    tm = 512
    tiles_per_seq = seq // tm
    return pl.pallas_call(
        functools.partial(_router_kernel, d=d, n_exp=n_exp),
        grid=(t // tm,),
        in_specs=[pl.BlockSpec((tm, d), lambda i: (i, 0)),
                  pl.BlockSpec((1, d), lambda i: (0, 0)),
                  pl.BlockSpec((None, None, 1, 3 * d), lambda i: (layer, i // tiles_per_seq, 0, 0)),
                  pl.BlockSpec((n_exp, d), lambda i: (0, 0)),
                  pl.BlockSpec((n_exp, 1), lambda i: (0, 0))],
        out_specs=[pl.BlockSpec((tm, d), lambda i: (i, 0)),
                   pl.BlockSpec((tm, 128), lambda i: (i, 0))],
        out_shape=[jax.ShapeDtypeStruct((t, d), BF16), jax.ShapeDtypeStruct((t, 128), F32)],
        compiler_params=_cparams(("arbitrary",)),
        name="ffn_router",
    )(x2, g.reshape(1, d), mod, router_w.T, router_bias.reshape(n_exp, 1))


def _moe_kernel(h_ref, gt_ref, x_ref, mod_ref, wg_ref, wu_ref, wd_ref, sg_ref, su_ref, sd_ref,
                o_ref, acc_ref, *, d):
    e = pl.program_id(1)
    h = h_ref[...]

    def ffn(wg, wu, wd, gate_col):
        a = _dot(h, wg.astype(BF16))
        b = _dot(h, wu.astype(BF16))
        mid = (a * jax.nn.sigmoid(a)) * b
        if gate_col is not None:
            mid = mid * gate_col
        return _dot(mid.astype(BF16), wd.astype(BF16))

    @pl.when(e == 0)
    def _():
        acc_ref[...] = ffn(sg_ref[...], su_ref[...], sd_ref[...], None)

    lane = lax.broadcasted_iota(jnp.int32, (1, 128), 1)
    gate_col = jnp.sum(jnp.where(lane == e, gt_ref[...], 0.0), axis=1, keepdims=True)
    acc_ref[...] += ffn(wg_ref[...], wu_ref[...], wd_ref[...], gate_col)

    @pl.when(e == pl.num_programs(1) - 1)
    def _():
        o_ref[...] = x_ref[...] + mod_ref[...][:, 2 * d:] * acc_ref[...]


def _moe_layer(x2, h, gates, mod, layer, w_gate, w_up, w_down, sh_gate, sh_up, sh_down, seq):
    t, d = x2.shape
    n_exp, _, de = w_gate.shape
    ds = sh_gate.shape[1]
    tm = min(1024, seq)
    tiles_per_seq = seq // tm
    row = lambda i, e: (i, 0)
    const = lambda i, e: (0, 0)
    return pl.pallas_call(
        functools.partial(_moe_kernel, d=d),
        grid=(t // tm, n_exp),
        in_specs=[pl.BlockSpec((tm, d), row),
                  pl.BlockSpec((tm, 128), row),
                  pl.BlockSpec((tm, d), row),
                  pl.BlockSpec((None, None, 1, 3 * d), lambda i, e: (layer, i // tiles_per_seq, 0, 0)),
                  pl.BlockSpec((None, d, de), lambda i, e: (e, 0, 0)),
                  pl.BlockSpec((None, d, de), lambda i, e: (e, 0, 0)),
                  pl.BlockSpec((None, de, d), lambda i, e: (e, 0, 0)),
                  pl.BlockSpec((d, ds), const),
                  pl.BlockSpec((d, ds), const),
                  pl.BlockSpec((ds, d), const)],
        out_specs=pl.BlockSpec((tm, d), row),
        out_shape=jax.ShapeDtypeStruct((t, d), F32),
        scratch_shapes=[pltpu.VMEM((tm, d), F32)],
        compiler_params=_cparams(("arbitrary", "arbitrary"), VMEM_LIMIT),
        name="moe_experts",
    )(h, gates, x2, mod, w_gate, w_up, w_down, sh_gate, sh_up, sh_down)


def kernel(x, c, positions, mix_norm_g, mix_mod_w, mix_mod_b, ffn_norm_g, ffn_mod_w, ffn_mod_b,
           hyb_w_in, hyb_w_out, s5_lambda_re, s5_lambda_im, s5_log_dt, s5_b_re, s5_b_im,
           s5_c_re, s5_c_im, s5_d, s5_glu_w, s5_glu_b, attn_q_norm_g, attn_k_norm_g,
           pool_w, pool_scale, router_w, router_bias, exp_w_gate, exp_w_up, exp_w_down,
           sh_w_gate, sh_w_up, sh_w_down):
    bsz, seq, d = x.shape
    t = bsz * seq
    depth = mix_norm_g.shape[0]
    x2 = x.reshape(t, d)
    mix_mod = _mod_vectors(c, mix_mod_w, mix_mod_b).reshape(depth, bsz, 1, 3 * d)
    ffn_mod = _mod_vectors(c, ffn_mod_w, ffn_mod_b).reshape(depth, bsz, 1, 3 * d)
    for i in range(depth):
        j = i // 2
        if i % 2 == 0:
            u, q_hm, qi_hm, k, v, ki, wi = _in_projection(
                x2, mix_norm_g[i], mix_mod, i, hyb_w_in[j], positions, attn_q_norm_g[j], attn_k_norm_g[j], bsz, seq)
            nchunk = seq // S5_CHUNK
            groups = u.shape[1] // S5_GROUP
            u2 = u.reshape(bsz * nchunk, S5_CHUNK, groups, S5_GROUP).transpose(0, 2, 1, 3)
            u2 = u2.reshape(bsz * nchunk, groups * S5_CHUNK * S5_GROUP)
            weights = _s5_weights(s5_lambda_re[j], s5_lambda_im[j], s5_log_dt[j], s5_b_re[j], s5_b_im[j],
                                  s5_c_re[j], s5_c_im[j], s5_d[j], s5_glu_w[j], s5_glu_b[j])
            y2 = _s5_mixer(u2, weights, bsz, seq)
            y_ssm = y2.reshape(bsz * nchunk, groups, S5_CHUNK, S5_GROUP).transpose(0, 2, 1, 3).reshape(t, -1)
            y_att = _dsa_attention(q_hm, qi_hm, wi, k, v, ki, bsz, seq)
            x2 = _out_projection(x2, y_ssm, y_att, hyb_w_out[j], mix_mod, i, seq)
        else:
            x2 = _pool_layer(x2, mix_norm_g[i], mix_mod, i, pool_w[j], pool_scale[j], seq)
        h, gates = _ffn_router(x2, ffn_norm_g[i], ffn_mod, i, router_w[i], router_bias[i], seq)
        x2 = _moe_layer(x2, h, gates, ffn_mod, i, exp_w_gate[i], exp_w_up[i], exp_w_down[i],
                        sh_w_gate[i], sh_w_up[i], sh_w_down[i], seq)
    return x2.reshape(bsz, seq, d)
```

```python
import functools
import math

import numpy as np
import jax
import jax.numpy as jnp
from jax import lax
from jax.experimental import pallas as pl
from jax.experimental.pallas import tpu as pltpu
from jax.experimental.pallas import tpu_sc as plsc

F32 = jnp.float32
BF16 = jnp.bfloat16

EPS = 1e-6
S5_GROUP = 16
S5_STATE = 64
HEAD_DIM = 64
ATTN_HEADS = 8
IDX_HEADS = 8
ROPE_HALF = 8
ROPE_THETA = 500000.0
TOPK_MAX = 256
POOL_WINDOWS = (2, 4, 8, 16)
POOL_HALO = 16
N_EXPERT_GROUPS = 8
TOPK_GROUPS = 4
TOP_K = 8
ROUTED_SCALE = 2.5
S5_CHUNK = 16
NEG_BIG = -1e30
INT_MIN = -2147483648
VMEM_LIMIT = 56 * 1024 * 1024


def _cparams(sem, vmem=None):
    return pltpu.CompilerParams(dimension_semantics=sem, vmem_limit_bytes=vmem)


def _dot(a, b):
    return jnp.dot(a, b, preferred_element_type=F32)


def _dot_nt(a, b):
    return lax.dot_general(a, b, (((1,), (1,)), ((), ())), preferred_element_type=F32)


def _split_bf16(a):
    hi = a.astype(BF16)
    lo = (a - hi.astype(F32)).astype(BF16)
    return hi, lo


def _dot3(a, b):
    ah, al = _split_bf16(a)
    bh, bl = _split_bf16(b)
    return _dot(ah, bh) + (_dot(ah, bl) + _dot(al, bh))


def _norm_mod(x, g, shift, scale):
    y = x * lax.rsqrt(jnp.mean(x * x, axis=-1, keepdims=True) + EPS)
    return (y * g) * (1.0 + scale) + shift


def _mod_kernel(ct_ref, w_ref, b_ref, o_ref):
    ct = ct_ref[...]
    cs = ct * jax.nn.sigmoid(ct)
    w = w_ref[...]
    rows = [jnp.sum(w * cs[:, b:b + 1], axis=0, keepdims=True) for b in range(ct.shape[1])]
    o_ref[...] = jnp.concatenate(rows, axis=0) + b_ref[...]


def _mod_vectors(c, w, b):
    nl, d, n3 = w.shape
    bsz = c.shape[0]
    tn = 512
    return pl.pallas_call(
        _mod_kernel,
        grid=(nl, n3 // tn),
        in_specs=[pl.BlockSpec((d, bsz), lambda l, j: (0, 0)),
                  pl.BlockSpec((None, d, tn), lambda l, j: (l, 0, j)),
                  pl.BlockSpec((None, 1, tn), lambda l, j: (l, 0, j))],
        out_specs=pl.BlockSpec((None, bsz, tn), lambda l, j: (l, 0, j)),
        out_shape=jax.ShapeDtypeStruct((nl, bsz, n3), F32),
        compiler_params=_cparams(("arbitrary", "arbitrary")),
        name="mod_vectors",
    )(c.T, w, b.reshape(nl, 1, n3))


def _inproj_kernel(x_ref, g_ref, mod_ref, w_ref, pos_ref, inv_ref, qg_ref, kg_ref,
                   u_ref, q_ref, qi_ref, k_ref, v_ref, ki_ref, wi_ref, *, d):
    x = x_ref[...]
    mod = mod_ref[...]
    h = _norm_mod(x, g_ref[...], mod[:, :d], mod[:, d:2 * d])
    proj = _dot(h.astype(BF16), w_ref[...])
    tm = x.shape[0]
    width = ATTN_HEADS * HEAD_DIM

    lane = lax.broadcasted_iota(jnp.int32, (1, 128), 1) & (HEAD_DIM - 1)
    ang = pos_ref[...].astype(F32) * inv_ref[...]
    cos, sin = jnp.cos(ang), jnp.sin(ang)
    t_a = jnp.where(lane < 2 * ROPE_HALF, cos, 1.0)
    t_b = jnp.where(lane < ROPE_HALF, -sin, 0.0)
    t_c = jnp.where((lane >= ROPE_HALF) & (lane < 2 * ROPE_HALF), sin, 0.0)

    def rope(a):
        return a * t_a + pltpu.roll(a, 128 - ROPE_HALF, 1) * t_b + pltpu.roll(a, ROPE_HALF, 1) * t_c

    u_ref[...] = proj[:, :width].astype(u_ref.dtype)
    q = proj[:, width:2 * width]
    qi = proj[:, 2 * width:3 * width]
    small = proj[:, 3 * width:3 * width + 256]
    qsq = q * q
    qg = qg_ref[...]
    for pair in range(ATTN_HEADS // 2):
        sl = slice(pair * 128, (pair + 1) * 128)
        qr = rope(q[:, sl] * qg)
        qir = rope(qi[:, sl])
        for half in range(2):
            hh = 2 * pair + half
            hs = slice(half * HEAD_DIM, (half + 1) * HEAD_DIM)
            ss = jnp.sum(qsq[:, hh * HEAD_DIM:(hh + 1) * HEAD_DIM], axis=-1, keepdims=True)
            r = lax.rsqrt(ss * (1.0 / HEAD_DIM) + EPS) * (HEAD_DIM ** -0.5)
            q_ref[hh] = (qr[:, hs] * r).astype(q_ref.dtype)
            qi_ref[hh] = qir[:, hs].astype(qi_ref.dtype)
    kv = small[:, :128]
    k = kv[:, :HEAD_DIM]
    rk = lax.rsqrt(jnp.mean(k * k, axis=-1, keepdims=True) + EPS)
    kr = rope(kv * kg_ref[...])
    k_ref[...] = (kr[:, :HEAD_DIM] * rk).astype(k_ref.dtype)
    v_ref[...] = kv[:, HEAD_DIM:].astype(v_ref.dtype)
    kiw = small[:, 128:256]
    ki_ref[...] = rope(kiw)[:, :HEAD_DIM].astype(ki_ref.dtype)
    wscale = (IDX_HEADS ** -0.5) * (HEAD_DIM ** -0.5)
    wi_ref[...] = pltpu.roll(kiw, HEAD_DIM, 1) * wscale


def _in_projection(x2, g, mod, layer, w_in, positions, q_norm_g, k_norm_g, bsz, seq):
    t, d = x2.shape
    tm = 512
    width = ATTN_HEADS * HEAD_DIM
    cuts = np.cumsum([width, width, HEAD_DIM, HEAD_DIM, width, HEAD_DIM, IDX_HEADS])
    u_w, q_w, k_w, v_w, qi_w, ki_w, wi_w = jnp.split(w_in, cuts[:-1].tolist(), axis=1)
    pad = jnp.zeros((d, 128 - HEAD_DIM - IDX_HEADS), w_in.dtype)
    w = jnp.concatenate([u_w, q_w, qi_w, k_w, v_w, ki_w, wi_w, pad], axis=1).astype(BF16)
    nw = w.shape[1]
    lane = np.arange(128) % HEAD_DIM
    inv = np.where(lane < 2 * ROPE_HALF,
                   np.power(ROPE_THETA, -2.0 * (lane % ROPE_HALF) / (2 * ROPE_HALF)), 0.0)
    inv = jnp.asarray(inv.reshape(1, 128), F32)
    qg2 = jnp.tile(q_norm_g.reshape(1, HEAD_DIM), (1, 2))
    kg2 = jnp.concatenate([k_norm_g.reshape(1, HEAD_DIM), jnp.ones((1, HEAD_DIM), F32)], axis=1)
    tiles_per_seq = seq // tm
    row = lambda i: (i, 0)
    const = lambda i: (0, 0)
    hm = lambda i: (i // tiles_per_seq, 0, i % tiles_per_seq, 0)
    outs = pl.pallas_call(
        functools.partial(_inproj_kernel, d=d),
        grid=(t // tm,),
        in_specs=[pl.BlockSpec((tm, d), row),
                  pl.BlockSpec((1, d), const),
                  pl.BlockSpec((None, None, 1, 3 * d), lambda i: (layer, i // tiles_per_seq, 0, 0)),
                  pl.BlockSpec((d, nw), const),
                  pl.BlockSpec((tm, 1), row),
                  pl.BlockSpec((1, 128), const),
                  pl.BlockSpec((1, 128), const),
                  pl.BlockSpec((1, 128), const)],
        out_specs=[pl.BlockSpec((tm, width), row),
                   pl.BlockSpec((None, ATTN_HEADS, tm, HEAD_DIM), hm),
                   pl.BlockSpec((None, IDX_HEADS, tm, HEAD_DIM), hm),
                   pl.BlockSpec((tm, HEAD_DIM), row),
                   pl.BlockSpec((tm, HEAD_DIM), row),
                   pl.BlockSpec((tm, HEAD_DIM), row),
                   pl.BlockSpec((tm, 128), row)],
        out_shape=[jax.ShapeDtypeStruct((t, width), BF16),
                   jax.ShapeDtypeStruct((bsz, ATTN_HEADS, seq, HEAD_DIM), BF16),
                   jax.ShapeDtypeStruct((bsz, IDX_HEADS, seq, HEAD_DIM), BF16),
                   jax.ShapeDtypeStruct((t, HEAD_DIM), BF16),
                   jax.ShapeDtypeStruct((t, HEAD_DIM), BF16),
                   jax.ShapeDtypeStruct((t, HEAD_DIM), BF16),
                   jax.ShapeDtypeStruct((t, 128), F32)],
        compiler_params=_cparams(("arbitrary",), VMEM_LIMIT),
        name="in_projection",
    )(x2, g.reshape(1, d), mod, w, positions.reshape(t, 1), inv, qg2, kg2)
    return outs


def _s5_weights(lam_re, lam_im, log_dt, b_re, b_im, c_re, c_im, d_skip, glu_w, glu_b):
    L = S5_CHUNK
    g, p = lam_re.shape
    hch = S5_GROUP
    dt = jnp.exp(log_dt)[:, None]
    lr, li = lam_re, lam_im
    tau = jnp.arange(L + 1, dtype=F32)[:, None, None]
    mag = jnp.exp(lr * dt * tau)
    pw_r, pw_i = mag * jnp.cos(li * dt * tau), mag * jnp.sin(li * dt * tau)
    ar, ai = pw_r[1], pw_i[1]
    nr, ni = ar - 1.0, ai
    den = lr * lr + li * li
    cr, ci = (nr * lr + ni * li) / den, (ni * lr - nr * li) / den
    bb_r = cr[..., None] * b_re - ci[..., None] * b_im
    bb_i = cr[..., None] * b_im + ci[..., None] * b_re
    cl_r = c_re[None] * pw_r[:L, :, None, :] - c_im[None] * pw_i[:L, :, None, :]
    cl_i = c_re[None] * pw_i[:L, :, None, :] + c_im[None] * pw_r[:L, :, None, :]
    taps = (jnp.einsum('tghp,gpk->tghk', cl_r, bb_r) - jnp.einsum('tghp,gpk->tghk', cl_i, bb_i))
    ii = jnp.arange(L)
    lag = ii[None, :] - ii[:, None]
    tap_ji = taps[jnp.clip(lag, 0, L - 1)]
    tap_ji = jnp.where((lag >= 0)[:, :, None, None, None], tap_ji, 0.0)
    m_intra = jnp.transpose(tap_ji, (2, 0, 4, 1, 3)).reshape(g, L * hch, L * hch)
    rev_r, rev_i = pw_r[L - 1 - ii], pw_i[L - 1 - ii]
    z_r = rev_r[..., None] * bb_r[None] - rev_i[..., None] * bb_i[None]
    z_i = rev_r[..., None] * bb_i[None] + rev_i[..., None] * bb_r[None]
    w_z = jnp.concatenate([jnp.transpose(z_r, (1, 0, 3, 2)).reshape(g, L * hch, p),
                           jnp.transpose(z_i, (1, 0, 3, 2)).reshape(g, L * hch, p)], axis=-1)
    w_a = jnp.concatenate([m_intra, w_z], axis=-1).astype(BF16)
    co_r = c_re[None] * pw_r[1:, :, None, :] - c_im[None] * pw_i[1:, :, None, :]
    co_i = c_re[None] * pw_i[1:, :, None, :] + c_im[None] * pw_r[1:, :, None, :]
    w_c = jnp.concatenate([jnp.transpose(co_r, (1, 3, 0, 2)).reshape(g, p, L * hch),
                           -jnp.transpose(co_i, (1, 3, 0, 2)).reshape(g, p, L * hch)], axis=1)
    al_r, al_i = pw_r[L], pw_i[L]
    dec_a = jnp.concatenate([al_r, al_r], axis=-1)[:, None, :]
    dec_b = jnp.concatenate([-al_i, al_i], axis=-1)[:, None, :]
    d_t = jnp.tile(d_skip[:, None, :], (1, L, 1)).reshape(g, 1, L * hch)
    glu_k = jnp.einsum('ij,ghk->gihjk', jnp.eye(L, dtype=F32), glu_w).reshape(g, L * hch, L * hch).astype(BF16)
    glu_bt = jnp.tile(glu_b[:, None, :], (1, L, 1)).reshape(g, 1, L * hch)
    return w_a, w_c, dec_a, dec_b, d_t, glu_k, glu_bt


def _s5_kernel(u_ref, wa_ref, wc_ref, da_ref, db_ref, d_ref, gk_ref, gb_ref, o_ref,
               zp_ref, zq_ref, sp_ref, *, bsz, nchunk):
    u = u_ref[...]
    cols = u.shape[1]
    p2 = zp_ref.shape[1]
    res = _dot(u, wa_ref[...])
    y_intra = res[:, :cols]
    z = res[:, cols:]
    zp_ref[...] = z
    zq_ref[...] = pltpu.roll(z, p2 // 2, 1)
    da = da_ref[...]
    db = db_ref[...]
    dbq = -db

    def step(c, carry):
        new = []
        for b in range(bsz):
            sp, sq = carry[2 * b], carry[2 * b + 1]
            r = b * nchunk + c
            sp_ref[pl.ds(r, 1), :] = sp
            zp = zp_ref[pl.ds(r, 1), :]
            zq = zq_ref[pl.ds(r, 1), :]
            new.append(da * sp + db * sq + zp)
            new.append(da * sq + dbq * sp + zq)
        return tuple(new)

    zero = jnp.zeros((1, p2), F32)
    lax.fori_loop(0, nchunk, step, (zero,) * (2 * bsz))
    s_prev = sp_ref[...]
    sh, sl = _split_bf16(s_prev)
    wc = wc_ref[...]
    wch, wcl = _split_bf16(wc)
    y = y_intra + (_dot(sh, wch) + (_dot(sh, wcl) + _dot(sl, wch))) + d_ref[...] * u.astype(F32)
    y = jax.nn.gelu(y)
    gate = jax.nn.sigmoid(_dot(y.astype(BF16), gk_ref[...]) + gb_ref[...])
    o_ref[...] = (y * gate).astype(o_ref.dtype)


def _s5_mixer(u2, weights, bsz, seq):
    w_a, w_c, dec_a, dec_b, d_t, glu_k, glu_bt = weights
    g = w_a.shape[0]
    r, _ = u2.shape
    cols = S5_CHUNK * S5_GROUP
    p2 = 2 * S5_STATE
    nchunk = seq // S5_CHUNK
    grp = lambda i: (i, 0, 0)
    return pl.pallas_call(
        functools.partial(_s5_kernel, bsz=bsz, nchunk=nchunk),
        grid=(g,),
        in_specs=[pl.BlockSpec((r, cols), lambda i: (0, i)),
                  pl.BlockSpec((None, cols, cols + p2), grp),
                  pl.BlockSpec((None, p2, cols), grp),
                  pl.BlockSpec((None, 1, p2), grp),
                  pl.BlockSpec((None, 1, p2), grp),
                  pl.BlockSpec((None, 1, cols), grp),
                  pl.BlockSpec((None, cols, cols), grp),
                  pl.BlockSpec((None, 1, cols), grp)],
        out_specs=pl.BlockSpec((r, cols), lambda i: (0, i)),
        out_shape=jax.ShapeDtypeStruct(u2.shape, BF16),
        scratch_shapes=[pltpu.VMEM((r, p2), F32), pltpu.VMEM((r, p2), F32), pltpu.VMEM((r, p2), F32)],
        compiler_params=_cparams(("arbitrary",)),
        name="s5_mixer",
    )(u2, w_a, w_c, dec_a, dec_b, d_t, glu_k, glu_bt)


def _dsa_kernel(q_ref, qi_ref, wi_ref, kt_ref, v_ref, kit_ref, o_ref,
                sc_ref, s_ref, wb_ref, m_ref, l_ref, acc_ref, *, tq, ck, k_sel, idx_bits, max_iter):
    i = pl.program_id(1)
    nheads = q_ref.shape[0]
    nj = ck // 128
    n_ck = (i * tq + tq + ck - 1) // ck
    shape = (tq, 128)
    t_pos = i * tq + lax.broadcasted_iota(jnp.int32, shape, 0)
    lane = lax.broadcasted_iota(jnp.int32, shape, 1)
    kf = float(k_sel)
    inf = jnp.inf

    def rep(col):
        return jnp.broadcast_to(col, shape)

    def cols(x):
        return [x[:, j * 128:(j + 1) * 128] for j in range(nj)]

    w = wi_ref[...]
    for h in range(nheads):
        wb_ref[h] = rep(w[:, h:h + 1])

    def score_chunk(c, carry):
        lo, hi = carry
        kit = kit_ref[c]
        accs = [jnp.zeros(shape, F32) for _ in range(nj)]
        for h in range(nheads):
            s = jnp.maximum(_dot(qi_ref[h], kit), 0.0)
            wb = wb_ref[h]
            accs = [a + wb * sj for a, sj in zip(accs, cols(s))]
        for j in range(nj):
            a = accs[j] + 0.0
            vis = (c * ck + j * 128) + lane <= t_pos
            sc_ref[c, :, j * 128:(j + 1) * 128] = jnp.where(vis, a, -inf)
            hi = jnp.maximum(hi, jnp.where(vis, a, -inf))
            lo = jnp.minimum(lo, jnp.where(vis, a, inf))
        return lo, hi

    lo, hi = lax.fori_loop(0, n_ck, score_chunk, (jnp.full(shape, inf, F32), jnp.full(shape, -inf, F32)))
    lo = rep(jnp.min(lo, axis=1, keepdims=True))
    hi = rep(jnp.max(hi, axis=1, keepdims=True))
    few = (t_pos + 1).astype(F32) < kf

    def reduce_rows(fn, init, combine, final):
        def body(c, part):
            for xj in cols(sc_ref[c]):
                part = combine(part, fn(xj))
            return part
        return rep(final(lax.fori_loop(0, n_ck, body, jnp.full(shape, init, F32)), axis=1, keepdims=True))

    def count(pred):
        return reduce_rows(lambda xj: jnp.where(pred(xj), 1.0, 0.0), 0.0, jnp.add, jnp.sum)

    def pivot(it, lo_, hi_, c_lo, c_hi):
        frac = jnp.clip((c_lo - (kf + 0.5)) / (c_lo - c_hi), 0.0625, 0.9375)
        frac = jnp.where(it % 2 == 0, 0.5, frac)
        mid = lo_ + (hi_ - lo_) * frac
        half = lo_ + (hi_ - lo_) * 0.5
        return jnp.where(mid > lo_, jnp.where(mid < hi_, mid, half), half)

    def pending(lo_, hi_, mid, c_lo):
        unresolved = jnp.where(few, 0.0, jnp.where(c_lo == kf, 0.0, 1.0))
        movable = jnp.where(mid > lo_, jnp.where(mid < hi_, 1.0, 0.0), 0.0)
        return jnp.max(unresolved), jnp.max(unresolved * movable)

    def search_cond(st):
        return jnp.logical_and(st[0] < max_iter, st[-1] > 0.5)

    def search_body(st):
        it, lo_, hi_, c_lo, c_hi, mid, _, _ = st
        cnt = count(lambda xj: xj >= mid)
        up = cnt >= kf
        lo2 = jnp.where(up, mid, lo_)
        hi2 = jnp.where(up, hi_, mid)
        c_lo2 = jnp.where(up, cnt, c_lo)
        c_hi2 = jnp.where(up, c_hi, cnt)
        mid2 = pivot(it + 1, lo2, hi2, c_lo2, c_hi2)
        open_, go = pending(lo2, hi2, mid2, c_lo2)
        return it + 1, lo2, hi2, c_lo2, c_hi2, mid2, open_, go

    c_lo0 = (t_pos + 1).astype(F32)
    c_hi0 = jnp.zeros(shape, F32)
    mid0 = pivot(jnp.int32(0), lo, hi, c_lo0, c_hi0)
    open0, go0 = pending(lo, hi, mid0, c_lo0)
    st = lax.while_loop(search_cond, search_body, (jnp.int32(0), lo, hi, c_lo0, c_hi0, mid0, open0, go0))
    lo, open_ = st[1], st[6]

    def resolve():
        def walk(thr_, strict):
            keep = (lambda xj: xj > thr_) if strict else (lambda xj: xj >= thr_)
            return reduce_rows(lambda xj: jnp.where(keep(xj), xj, inf), inf, jnp.minimum, jnp.min)

        def above_equal(thr_):
            return count(lambda xj: xj > thr_), count(lambda xj: xj == thr_)

        def too_low(n_gt):
            return jnp.where(few, 0.0, jnp.where(n_gt >= kf, 1.0, 0.0))

        thr0 = walk(lo, False)
        gt0, eq0 = above_equal(thr0)

        def walk_body(st):
            thr_, n_gt, _ = st
            thr2 = jnp.where(too_low(n_gt) > 0.5, walk(thr_, True), thr_)
            return (thr2,) + above_equal(thr2)

        thr_, n_gt, n_eq = lax.while_loop(lambda st: jnp.max(too_low(st[1])) > 0.5, walk_body, (thr0, gt0, eq0))
        need = kf - n_gt

        def idx_step(b, cur):
            trial = cur | (jnp.int32(1) << (idx_bits - 1 - b))

            def body(c, part):
                for j, xj in enumerate(cols(sc_ref[c])):
                    idx = (c * ck + j * 128) + lane
                    part = part + jnp.where(xj == thr_, jnp.where(idx < trial, 1.0, 0.0), 0.0)
                return part
            cnt = rep(jnp.sum(lax.fori_loop(0, n_ck, body, jnp.zeros(shape, F32)), axis=1, keepdims=True))
            return jnp.where(cnt < need, trial, cur)

        cut_ = lax.fori_loop(0, idx_bits, idx_step, jnp.zeros(shape, jnp.int32))
        return thr_, cut_

    thr, cut = lax.cond(open_ > 0.5, resolve, lambda: (lo, jnp.full(shape, 0x7FFFFFFF, jnp.int32)))

    def bias_chunk(c, _):
        x = sc_ref[c]
        for j, xj in enumerate(cols(x)):
            idx = (c * ck + j * 128) + lane
            at_thr = jnp.where(xj == thr, jnp.where(idx <= cut, 0.0, NEG_BIG), NEG_BIG)
            sc_ref[c, :, j * 128:(j + 1) * 128] = jnp.where(xj > thr, 0.0, at_thr)
        return 0

    lax.fori_loop(0, n_ck, bias_chunk, 0)

    for h in range(nheads):
        m_ref[h] = jnp.full(shape, NEG_BIG, F32)
        l_ref[h] = jnp.zeros(shape, F32)
        acc_ref[h] = jnp.zeros((tq, HEAD_DIM), F32)

    def attn_scores(c, _):
        kt = kt_ref[c]
        bias = cols(sc_ref[c])
        for h in range(nheads):
            s = cols(_dot(q_ref[h], kt))
            part = m_ref[h]
            for j in range(nj):
                sj = s[j] + bias[j]
                s_ref[h, c, :, j * 128:(j + 1) * 128] = sj
                part = jnp.maximum(part, sj)
            m_ref[h] = part
        return 0

    lax.fori_loop(0, n_ck, attn_scores, 0)
    for h in range(nheads):
        m_ref[h] = rep(jnp.max(m_ref[h], axis=1, keepdims=True))

    def attn_values(c, _):
        off = pl.multiple_of(c * ck, ck)
        vc = v_ref[pl.ds(off, ck), :]
        for h in range(nheads):
            m = m_ref[h]
            p = [jnp.exp(sj - m) for sj in cols(s_ref[h, c])]
            l_ref[h] += functools.reduce(jnp.add, p)
            acc_ref[h] += _dot(jnp.concatenate(p, axis=1).astype(BF16), vc)
        return 0

    lax.fori_loop(0, n_ck, attn_values, 0)
    for h in range(nheads):
        l_row = jnp.sum(l_ref[h], axis=1, keepdims=True)
        o_ref[:, h * HEAD_DIM:(h + 1) * HEAD_DIM] = (acc_ref[h] / l_row).astype(o_ref.dtype)


def _dsa_attention(q_hm, qi_hm, wi, k, v, ki, bsz, seq):
    tq = 128
    ck = min(512, seq)
    n_ck = seq // ck
    k_sel = min(TOPK_MAX, seq // 4)
    nq = seq // tq
    width = ATTN_HEADS * HEAD_DIM
    kt = k.reshape(bsz, n_ck, ck, HEAD_DIM).transpose(0, 1, 3, 2)
    kit = ki.reshape(bsz, n_ck, ck, HEAD_DIM).transpose(0, 1, 3, 2)
    hm = lambda b, i: (b, 0, i, 0)
    per_batch = lambda b, i: (b, 0, 0, 0)
    nh = ATTN_HEADS
    return pl.pallas_call(
        functools.partial(_dsa_kernel, tq=tq, ck=ck, k_sel=k_sel, idx_bits=(seq - 1).bit_length(), max_iter=48),
        grid=(bsz, nq),
        in_specs=[pl.BlockSpec((None, ATTN_HEADS, tq, HEAD_DIM), hm),
                  pl.BlockSpec((None, IDX_HEADS, tq, HEAD_DIM), hm),
                  pl.BlockSpec((tq, 128), lambda b, i: (b * nq + i, 0)),
                  pl.BlockSpec((None, n_ck, HEAD_DIM, ck), per_batch),
                  pl.BlockSpec((seq, HEAD_DIM), lambda b, i: (b, 0)),
                  pl.BlockSpec((None, n_ck, HEAD_DIM, ck), per_batch)],
        out_specs=pl.BlockSpec((tq, width), lambda b, i: (b * nq + i, 0)),
        out_shape=jax.ShapeDtypeStruct((bsz * seq, width), BF16),
        scratch_shapes=[pltpu.VMEM((n_ck, tq, ck), F32),
                        pltpu.VMEM((nh, n_ck, tq, ck), F32),
                        pltpu.VMEM((IDX_HEADS, tq, 128), F32),
                        pltpu.VMEM((nh, tq, 128), F32),
                        pltpu.VMEM((nh, tq, 128), F32),
                        pltpu.VMEM((nh, tq, HEAD_DIM), F32)],
        compiler_params=_cparams(("arbitrary", "arbitrary"), VMEM_LIMIT),
        name="dsa_attention",
    )(q_hm, qi_hm, wi, kt, v, kit)


def _outproj_kernel(x_ref, ys_ref, ya_ref, w_ref, mod_ref, o_ref, *, d):
    width = ys_ref.shape[1]
    y = _dot(ys_ref[...], w_ref[:width, :]) + _dot(ya_ref[...], w_ref[width:, :])
    o_ref[...] = x_ref[...] + mod_ref[...][:, 2 * d:] * y


def _out_projection(x2, y_ssm, y_att, w_out, mod, layer, seq):
    t, d = x2.shape
    tm = 512
    width = y_ssm.shape[1]
    tiles_per_seq = seq // tm
    row = lambda i: (i, 0)
    return pl.pallas_call(
        functools.partial(_outproj_kernel, d=d),
        grid=(t // tm,),
        in_specs=[pl.BlockSpec((tm, d), row),
                  pl.BlockSpec((tm, width), row),
                  pl.BlockSpec((tm, width), row),
                  pl.BlockSpec((2 * width, d), lambda i: (0, 0)),
                  pl.BlockSpec((None, None, 1, 3 * d), lambda i: (layer, i // tiles_per_seq, 0, 0))],
        out_specs=pl.BlockSpec((tm, d), row),
        out_shape=jax.ShapeDtypeStruct((t, d), F32),
        compiler_params=_cparams(("arbitrary",)),
        name="out_projection",
    )(x2, y_ssm, y_att, w_out.astype(BF16), mod)


def _pool_kernel(x_ref, halo_ref, g_ref, mod_ref, pw_ref, ps_ref, o_ref, h_ref, *, d, tiles_per_seq):
    i = pl.program_id(0)
    tm = x_ref.shape[0]
    mod = mod_ref[...]
    shift, scale, gate = mod[:, :d], mod[:, d:2 * d], mod[:, 2 * d:]
    x = x_ref[...]
    first = (i % tiles_per_seq) == 0
    h_ref[POOL_HALO:, :] = _norm_mod(x, g_ref[...], shift, scale)
    halo = _norm_mod(halo_ref[...], g_ref[...], shift, scale)
    h_ref[:POOL_HALO, :] = jnp.where(first, 0.0, halo)
    pos = (i % tiles_per_seq) * tm + lax.broadcasted_iota(jnp.int32, (tm, 1), 0)
    gw = d // len(POOL_WINDOWS)
    ys = []
    for gi, win in enumerate(POOL_WINDOWS):
        cs = slice(gi * gw, (gi + 1) * gw)
        cur = h_ref[POOL_HALO:, cs]
        tot = cur
        for k in range(1, win):
            tot = tot + h_ref[POOL_HALO - k:POOL_HALO - k + tm, cs]
        cnt = jnp.minimum(pos + 1, win).astype(F32)
        pooled = tot / cnt - cur
        ys.append(_dot(pooled.astype(BF16), pw_ref[gi]))
    y = jnp.concatenate(ys, axis=1) * ps_ref[...]
    o_ref[...] = x + gate * y


def _pool_layer(x2, g, mod, layer, pool_w, pool_scale, seq):
    t, d = x2.shape
    tm = 512
    tiles_per_seq = seq // tm
    gw = d // len(POOL_WINDOWS)
    hb = tm // POOL_HALO
    return pl.pallas_call(
        functools.partial(_pool_kernel, d=d, tiles_per_seq=tiles_per_seq),
        grid=(t // tm,),
        in_specs=[pl.BlockSpec((tm, d), lambda i: (i, 0)),
                  pl.BlockSpec((POOL_HALO, d), lambda i: (jnp.maximum(i * hb - 1, 0), 0)),
                  pl.BlockSpec((1, d), lambda i: (0, 0)),
                  pl.BlockSpec((None, None, 1, 3 * d), lambda i: (layer, i // tiles_per_seq, 0, 0)),
                  pl.BlockSpec((len(POOL_WINDOWS), gw, gw), lambda i: (0, 0, 0)),
                  pl.BlockSpec((1, d), lambda i: (0, 0))],
        out_specs=pl.BlockSpec((tm, d), lambda i: (i, 0)),
        out_shape=jax.ShapeDtypeStruct((t, d), F32),
        scratch_shapes=[pltpu.VMEM((tm + POOL_HALO, d), F32)],
        compiler_params=_cparams(("arbitrary",)),
        name="pool_mixer",
    )(x2, x2, g.reshape(1, d), mod, pool_w.astype(BF16), pool_scale.reshape(1, d))


def _first_max(vals, idx, big):
    m = jnp.max(vals, axis=0, keepdims=True)
    first = jnp.min(jnp.where(vals == m, idx, big), axis=0, keepdims=True)
    return m, first


def _pack_bf16_pairs(x):
    n = x.shape[1] // 2
    lo = pltpu.bitcast(x[:, :n].astype(BF16).astype(F32), jnp.int32)
    hi = pltpu.bitcast(x[:, n:].astype(BF16).astype(F32), jnp.int32)
    return hi | ((lo >> 16) & 0xFFFF)


def _unpack_bf16_pairs(w):
    lo = pltpu.bitcast(w << 16, F32)
    hi = pltpu.bitcast(w & jnp.int32(-65536), F32)
    return jnp.concatenate([lo, hi], axis=1)


def _router_kernel(x_ref, g_ref, mod_ref, rw_ref, rb_ref, tri_ref, h_ref, gate_ref, rank_ref, cnt_ref,
                   run_ref, *, d, n_exp):
    mod = mod_ref[...]
    h = _norm_mod(x_ref[...], g_ref[...], mod[:, :d], mod[:, d:2 * d])
    h_ref[...] = _pack_bf16_pairs(h)
    tm = h.shape[0]
    hh, hl = _split_bf16(h)
    rw = rw_ref[...]
    rh, rl = _split_bf16(rw)
    logits = _dot_nt(rh, hh) + (_dot_nt(rh, hl) + _dot_nt(rl, hh))
    scores = jax.nn.sigmoid(logits)
    sel = scores + rb_ref[...]
    gsz = n_exp // N_EXPERT_GROUPS
    neg = -jnp.inf
    shape3 = (N_EXPERT_GROUPS, gsz, tm)
    sel3 = sel.reshape(shape3)
    sub = lax.broadcasted_iota(jnp.int32, shape3, 1)
    m1 = jnp.max(sel3, axis=1, keepdims=True)
    f1 = jnp.min(jnp.where(sel3 == m1, sub, gsz), axis=1, keepdims=True)
    m2 = jnp.max(jnp.where(sub == f1, neg, sel3), axis=1, keepdims=True)
    gscore = jnp.broadcast_to(m1 + m2, shape3).reshape(n_exp, tm)
    eidx = lax.broadcasted_iota(jnp.int32, (n_exp, tm), 0)
    gidx = eidx // gsz
    keep = jnp.zeros((n_exp, tm), F32)
    work = gscore
    for _ in range(TOPK_GROUPS):
        _, first = _first_max(work, gidx, N_EXPERT_GROUPS)
        hit = gidx == first
        keep = jnp.where(hit, 1.0, keep)
        work = jnp.where(hit, neg, work)
    work = jnp.where(keep > 0.0, sel, neg)
    chosen = jnp.zeros((n_exp, tm), F32)
    for _ in range(TOP_K):
        _, first = _first_max(work, eidx, n_exp)
        hit = eidx == first
        chosen = jnp.where(hit, 1.0, chosen)
        work = jnp.where(hit, neg, work)
    picked = chosen * scores
    gate_ref[...] = picked / jnp.sum(picked, axis=0, keepdims=True) * ROUTED_SCALE

    @pl.when(pl.program_id(0) == 0)
    def _():
        run_ref[...] = jnp.zeros(run_ref.shape, F32)

    before = _dot(chosen.astype(BF16), tri_ref[...])
    run = run_ref[...]
    rank_ref[...] = jnp.where(chosen > 0.0, before + run[:, :1], -1.0)
    run = run + jnp.broadcast_to(jnp.sum(chosen, axis=1, keepdims=True), run.shape)
    run_ref[...] = run
    cnt_ref[...] = run


def _ffn_router(x2, g, mod, layer, router_w, router_bias, seq):
    t, d = x2.shape
    n_exp = router_w.shape[1]
    tm = 512
    tiles_per_seq = seq // tm
    tri = (np.arange(tm)[:, None] < np.arange(tm)[None, :]).astype(np.float32)
    return pl.pallas_call(
        functools.partial(_router_kernel, d=d, n_exp=n_exp),
        grid=(t // tm,),
        in_specs=[pl.BlockSpec((tm, d), lambda i: (i, 0)),
                  pl.BlockSpec((1, d), lambda i: (0, 0)),
                  pl.BlockSpec((None, None, 1, 3 * d), lambda i: (layer, i // tiles_per_seq, 0, 0)),
                  pl.BlockSpec((n_exp, d), lambda i: (0, 0)),
                  pl.BlockSpec((n_exp, 1), lambda i: (0, 0)),
                  pl.BlockSpec((tm, tm), lambda i: (0, 0))],
        out_specs=[pl.BlockSpec((tm, d // 2), lambda i: (i, 0)),
                   pl.BlockSpec((n_exp, tm), lambda i: (0, i)),
                   pl.BlockSpec((n_exp, tm), lambda i: (0, i)),
                   pl.BlockSpec((n_exp, 128), lambda i: (0, 0))],
        out_shape=[jax.ShapeDtypeStruct((t, d // 2), jnp.int32),
                   jax.ShapeDtypeStruct((n_exp, t), F32),
                   jax.ShapeDtypeStruct((n_exp, t), F32),
                   jax.ShapeDtypeStruct((n_exp, 128), F32)],
        scratch_shapes=[pltpu.VMEM((n_exp, 128), F32)],
        compiler_params=_cparams(("arbitrary",)),
        name="ffn_router",
    )(x2, g.reshape(1, d), mod, router_w.T, router_bias.reshape(n_exp, 1), jnp.asarray(tri, BF16))


def _assign_kernel(rank_ref, gate_ref, start_ref, pos_ref, w_ref, *, n_exp):
    rank = rank_ref[...]
    gates = gate_ref[...]
    tm = rank.shape[1]
    slot = rank + start_ref[...]
    eidx = lax.broadcasted_iota(jnp.int32, (n_exp, tm), 0).astype(F32)
    alive = jnp.where(rank >= 0.0, eidx, float(n_exp))
    kidx = lax.broadcasted_iota(jnp.int32, (TOP_K, tm), 0)
    pos = jnp.zeros((TOP_K, tm), F32)
    wts = jnp.zeros((TOP_K, tm), F32)
    for k in range(TOP_K):
        first = jnp.min(alive, axis=0, keepdims=True)
        hit = alive == first
        pos_k = jnp.sum(jnp.where(hit, slot, 0.0), axis=0, keepdims=True)
        w_k = jnp.sum(jnp.where(hit, gates, 0.0), axis=0, keepdims=True)
        pos = jnp.where(kidx == k, pos_k, pos)
        wts = jnp.where(kidx == k, w_k, wts)
        alive = jnp.where(hit, float(n_exp), alive)
    pos_ref[...] = pos.astype(jnp.int32)
    w_ref[...] = wts


def _assign_slots(ranks, gates, start):
    n_exp, t = ranks.shape
    tm = 512
    return pl.pallas_call(
        functools.partial(_assign_kernel, n_exp=n_exp),
        grid=(t // tm,),
        in_specs=[pl.BlockSpec((n_exp, tm), lambda i: (0, i)),
                  pl.BlockSpec((n_exp, tm), lambda i: (0, i)),
                  pl.BlockSpec((n_exp, 1), lambda i: (0, 0))],
        out_specs=[pl.BlockSpec((TOP_K, tm), lambda i: (0, i)),
                   pl.BlockSpec((TOP_K, tm), lambda i: (0, i))],
        out_shape=[jax.ShapeDtypeStruct((TOP_K, t), jnp.int32), jax.ShapeDtypeStruct((TOP_K, t), F32)],
        compiler_params=_cparams(("arbitrary",)),
        name="moe_assign",
    )(ranks, gates, start)


SC_CORES = 2
SC_SUBCORES = 16
SC_WINDOW = 128


def _sc_mesh():
    return plsc.VectorSubcoreMesh(core_axis_name="c", subcore_axis_name="s",
                                  num_cores=SC_CORES, num_subcores=SC_SUBCORES)


def _sc_scatter_rows(rows, pos_flat, n_out):
    t, width = rows.shape
    nk = pos_flat.shape[0] // t
    per_worker = t // (SC_CORES * SC_SUBCORES)

    def body(rows_hbm, pos_hbm, out_hbm, idx_v, buf_v):
        wid = lax.axis_index("s") * SC_CORES + lax.axis_index("c")

        @pl.loop(0, per_worker // SC_WINDOW)
        def _(j):
            t0 = wid * per_worker + j * SC_WINDOW
            pltpu.sync_copy(rows_hbm.at[pl.ds(t0, SC_WINDOW)], buf_v)
            for k in range(nk):
                pltpu.sync_copy(pos_hbm.at[pl.ds(k * t + t0, SC_WINDOW)], idx_v)
                pltpu.sync_copy(buf_v, out_hbm.at[idx_v])

    return pl.kernel(
        body, out_type=jax.ShapeDtypeStruct((n_out, width), jnp.int32), mesh=_sc_mesh(),
        scratch_types=[pltpu.VMEM((SC_WINDOW,), jnp.int32), pltpu.VMEM((SC_WINDOW, width), jnp.int32)],
        name="sc_scatter_rows",
    )(rows, pos_flat)


def _sc_gather_rows(table, pos_flat):
    m = pos_flat.shape[0]
    width = table.shape[1]
    per_worker = m // (SC_CORES * SC_SUBCORES)

    def body(table_hbm, pos_hbm, out_hbm, idx_v, buf_v):
        wid = lax.axis_index("s") * SC_CORES + lax.axis_index("c")

        @pl.loop(0, per_worker // SC_WINDOW)
        def _(j):
            n0 = wid * per_worker + j * SC_WINDOW
            pltpu.sync_copy(pos_hbm.at[pl.ds(n0, SC_WINDOW)], idx_v)
            pltpu.sync_copy(table_hbm.at[idx_v], buf_v)
            pltpu.sync_copy(buf_v, out_hbm.at[pl.ds(n0, SC_WINDOW)])

    return pl.kernel(
        body, out_type=jax.ShapeDtypeStruct((m, width), jnp.int32), mesh=_sc_mesh(),
        scratch_types=[pltpu.VMEM((SC_WINDOW,), jnp.int32), pltpu.VMEM((SC_WINDOW, width), jnp.int32)],
        name="sc_gather_rows",
    )(table, pos_flat)


MOE_BLOCK = 256


def _silu_mul(a, b):
    return (a * jax.nn.sigmoid(a)) * b


def _expert_ffn_kernel(be_ref, nv_ref, xs_ref, wg_ref, wu_ref, wd_ref, ys_ref, wg_bf, wu_bf, wd_bf):
    i = pl.program_id(0)
    fresh = jnp.logical_or(i == 0, be_ref[i] != be_ref[jnp.maximum(i - 1, 0)])

    @pl.when(fresh)
    def _():
        wg_bf[...] = wg_ref[...].astype(BF16)
        wu_bf[...] = wu_ref[...].astype(BF16)
        wd_bf[...] = wd_ref[...].astype(BF16)

    nv = nv_ref[i]

    @pl.when(nv > 0)
    def _():
        x = _unpack_bf16_pairs(xs_ref[...])
        row = lax.broadcasted_iota(jnp.int32, (x.shape[0], 1), 0)
        x = jnp.where(row < nv, x, 0.0).astype(BF16)
        mid = _silu_mul(_dot(x, wg_bf[...]), _dot(x, wu_bf[...]))
        ys_ref[...] = _pack_bf16_pairs(_dot(mid.astype(BF16), wd_bf[...]))

    @pl.when(nv <= 0)
    def _():
        ys_ref[...] = jnp.zeros(ys_ref.shape, ys_ref.dtype)


def _expert_ffn(xs, block_expert, block_rows, w_gate, w_up, w_down):
    n_rows, half = xs.shape
    n_exp, d, de = w_gate.shape
    nb = n_rows // MOE_BLOCK
    wmap = lambda i, be, nv: (be[i], 0, 0)
    grid_spec = pltpu.PrefetchScalarGridSpec(
        num_scalar_prefetch=2,
        grid=(nb,),
        in_specs=[pl.BlockSpec((MOE_BLOCK, half), lambda i, be, nv: (i, 0)),
                  pl.BlockSpec((None, d, de), wmap),
                  pl.BlockSpec((None, d, de), wmap),
                  pl.BlockSpec((None, de, d), wmap)],
        out_specs=pl.BlockSpec((MOE_BLOCK, half), lambda i, be, nv: (i, 0)),
        scratch_shapes=[pltpu.VMEM((d, de), BF16), pltpu.VMEM((d, de), BF16), pltpu.VMEM((de, d), BF16)],
    )
    return pl.pallas_call(
        _expert_ffn_kernel,
        grid_spec=grid_spec,
        out_shape=jax.ShapeDtypeStruct((n_rows, half), jnp.int32),
        compiler_params=_cparams(("arbitrary",)),
        name="moe_expert_ffn",
    )(block_expert, block_rows, xs, w_gate, w_up, w_down)


def _combine_kernel(yt_ref, wt_ref, hp_ref, x_ref, mod_ref, sg_ref, su_ref, sd_ref, o_ref, *, d):
    h = _unpack_bf16_pairs(hp_ref[...]).astype(BF16)
    acc = _dot(_silu_mul(_dot(h, sg_ref[...]), _dot(h, su_ref[...])).astype(BF16), sd_ref[...])
    w = wt_ref[...]
    for k in range(TOP_K):
        acc = acc + w[:, k:k + 1] * _unpack_bf16_pairs(yt_ref[k])
    o_ref[...] = x_ref[...] + mod_ref[...][:, 2 * d:] * acc


def _combine(y_tok, w_tok, h_packed, x2, mod, layer, sh_gate, sh_up, sh_down, seq):
    t, d = x2.shape
    ds = sh_gate.shape[1]
    tm = 512
    tiles_per_seq = seq // tm
    row = lambda i: (i, 0)
    const = lambda i: (0, 0)
    return pl.pallas_call(
        functools.partial(_combine_kernel, d=d),
        grid=(t // tm,),
        in_specs=[pl.BlockSpec((TOP_K, tm, d // 2), lambda i: (0, i, 0)),
                  pl.BlockSpec((tm, TOP_K), row),
                  pl.BlockSpec((tm, d // 2), row),
                  pl.BlockSpec((tm, d), row),
                  pl.BlockSpec((None, None, 1, 3 * d), lambda i: (layer, i // tiles_per_seq, 0, 0)),
                  pl.BlockSpec((d, ds), const),
                  pl.BlockSpec((d, ds), const),
                  pl.BlockSpec((ds, d), const)],
        out_specs=pl.BlockSpec((tm, d), row),
        out_shape=jax.ShapeDtypeStruct((t, d), F32),
        compiler_params=_cparams(("arbitrary",), VMEM_LIMIT),
        name="moe_combine",
    )(y_tok, w_tok, h_packed, x2, mod, sh_gate.astype(BF16), sh_up.astype(BF16), sh_down.astype(BF16))


def _moe_layer(x2, h_packed, gates, ranks, counts, mod, layer, w_gate, w_up, w_down, sh_gate, sh_up, sh_down, seq):
    t, d = x2.shape
    n_exp = w_gate.shape[0]
    cnt = counts[:, 0].astype(jnp.int32)
    padded = (cnt + MOE_BLOCK - 1) // MOE_BLOCK * MOE_BLOCK
    seg_end = jnp.cumsum(padded)
    seg_start = seg_end - padded
    nb = t * TOP_K // MOE_BLOCK + n_exp
    first_row = jnp.arange(nb, dtype=jnp.int32) * MOE_BLOCK
    block_expert = jnp.minimum(jnp.searchsorted(seg_end, first_row, side='right'), n_exp - 1).astype(jnp.int32)
    block_rows = jnp.clip(seg_start[block_expert] + cnt[block_expert] - first_row, 0, MOE_BLOCK).astype(jnp.int32)
    pos, w = _assign_slots(ranks, gates, seg_start.astype(F32).reshape(n_exp, 1))
    pos_flat = pos.reshape(TOP_K * t)
    xs = _sc_scatter_rows(h_packed, pos_flat, nb * MOE_BLOCK)
    ys = _expert_ffn(xs, block_expert, block_rows, w_gate, w_up, w_down)
    y_tok = _sc_gather_rows(ys, pos_flat).reshape(TOP_K, t, d // 2)
    return _combine(y_tok, w.T, h_packed, x2, mod, layer, sh_gate, sh_up, sh_down, seq)


def kernel(x, c, positions, mix_norm_g, mix_mod_w, mix_mod_b, ffn_norm_g, ffn_mod_w, ffn_mod_b,
           hyb_w_in, hyb_w_out, s5_lambda_re, s5_lambda_im, s5_log_dt, s5_b_re, s5_b_im,
           s5_c_re, s5_c_im, s5_d, s5_glu_w, s5_glu_b, attn_q_norm_g, attn_k_norm_g,
           pool_w, pool_scale, router_w, router_bias, exp_w_gate, exp_w_up, exp_w_down,
           sh_w_gate, sh_w_up, sh_w_down):
    bsz, seq, d = x.shape
    t = bsz * seq
    depth = mix_norm_g.shape[0]
    x2 = x.reshape(t, d)
    mix_mod = _mod_vectors(c, mix_mod_w, mix_mod_b).reshape(depth, bsz, 1, 3 * d)
    ffn_mod = _mod_vectors(c, ffn_mod_w, ffn_mod_b).reshape(depth, bsz, 1, 3 * d)
    for i in range(depth):
        j = i // 2
        if i % 2 == 0:
            u, q_hm, qi_hm, k, v, ki, wi = _in_projection(
                x2, mix_norm_g[i], mix_mod, i, hyb_w_in[j], positions, attn_q_norm_g[j], attn_k_norm_g[j], bsz, seq)
            nchunk = seq // S5_CHUNK
            groups = u.shape[1] // S5_GROUP
            u2 = u.reshape(bsz * nchunk, S5_CHUNK, groups, S5_GROUP).transpose(0, 2, 1, 3)
            u2 = u2.reshape(bsz * nchunk, groups * S5_CHUNK * S5_GROUP)
            weights = _s5_weights(s5_lambda_re[j], s5_lambda_im[j], s5_log_dt[j], s5_b_re[j], s5_b_im[j],
                                  s5_c_re[j], s5_c_im[j], s5_d[j], s5_glu_w[j], s5_glu_b[j])
            y2 = _s5_mixer(u2, weights, bsz, seq)
            y_ssm = y2.reshape(bsz * nchunk, groups, S5_CHUNK, S5_GROUP).transpose(0, 2, 1, 3).reshape(t, -1)
            y_att = _dsa_attention(q_hm, qi_hm, wi, k, v, ki, bsz, seq)
            x2 = _out_projection(x2, y_ssm, y_att, hyb_w_out[j], mix_mod, i, seq)
        else:
            x2 = _pool_layer(x2, mix_norm_g[i], mix_mod, i, pool_w[j], pool_scale[j], seq)
        h_packed, gates, ranks, counts = _ffn_router(x2, ffn_norm_g[i], ffn_mod, i, router_w[i], router_bias[i], seq)
        x2 = _moe_layer(x2, h_packed, gates, ranks, counts, ffn_mod, i, exp_w_gate[i], exp_w_up[i], exp_w_down[i],
                        sh_w_gate[i], sh_w_up[i], sh_w_down[i], seq)
    return x2.reshape(bsz, seq, d)
```

```python
import functools
import math

import numpy as np
import jax
import jax.numpy as jnp
from jax import lax
from jax.experimental import pallas as pl
from jax.experimental.pallas import tpu as pltpu
from jax.experimental.pallas import tpu_sc as plsc

F32 = jnp.float32
BF16 = jnp.bfloat16

EPS = 1e-6
S5_GROUP = 16
S5_STATE = 64
HEAD_DIM = 64
ATTN_HEADS = 8
IDX_HEADS = 8
ROPE_HALF = 8
ROPE_THETA = 500000.0
TOPK_MAX = 256
POOL_WINDOWS = (2, 4, 8, 16)
POOL_HALO = 16
N_EXPERT_GROUPS = 8
TOPK_GROUPS = 4
TOP_K = 8
ROUTED_SCALE = 2.5
S5_CHUNK = 16
NEG_BIG = -1e30
INT_MIN = -2147483648
VMEM_LIMIT = 56 * 1024 * 1024


def _cparams(sem, vmem=None):
    return pltpu.CompilerParams(dimension_semantics=sem, vmem_limit_bytes=vmem)


def _dot(a, b):
    return jnp.dot(a, b, preferred_element_type=F32)


def _dot_nt(a, b):
    return lax.dot_general(a, b, (((1,), (1,)), ((), ())), preferred_element_type=F32)


def _split_bf16(a):
    hi = a.astype(BF16)
    lo = (a - hi.astype(F32)).astype(BF16)
    return hi, lo


def _dot3(a, b):
    ah, al = _split_bf16(a)
    bh, bl = _split_bf16(b)
    return _dot(ah, bh) + (_dot(ah, bl) + _dot(al, bh))


def _norm_mod(x, g, shift, scale):
    y = x * lax.rsqrt(jnp.mean(x * x, axis=-1, keepdims=True) + EPS)
    return (y * g) * (1.0 + scale) + shift


def _mod_kernel(ct_ref, w_ref, b_ref, o_ref):
    ct = ct_ref[...]
    cs = ct * jax.nn.sigmoid(ct)
    w = w_ref[...]
    rows = [jnp.sum(w * cs[:, b:b + 1], axis=0, keepdims=True) for b in range(ct.shape[1])]
    o_ref[...] = jnp.concatenate(rows, axis=0) + b_ref[...]


def _mod_vectors(c, w, b):
    nl, d, n3 = w.shape
    bsz = c.shape[0]
    tn = 512
    return pl.pallas_call(
        _mod_kernel,
        grid=(nl, n3 // tn),
        in_specs=[pl.BlockSpec((d, bsz), lambda l, j: (0, 0)),
                  pl.BlockSpec((None, d, tn), lambda l, j: (l, 0, j)),
                  pl.BlockSpec((None, 1, tn), lambda l, j: (l, 0, j))],
        out_specs=pl.BlockSpec((None, bsz, tn), lambda l, j: (l, 0, j)),
        out_shape=jax.ShapeDtypeStruct((nl, bsz, n3), F32),
        compiler_params=_cparams(("arbitrary", "arbitrary")),
        name="mod_vectors",
    )(c.T, w, b.reshape(nl, 1, n3))


def _inproj_kernel(x_ref, g_ref, mod_ref, w_ref, pos_ref, inv_ref, qg_ref, kg_ref,
                   u_ref, q_ref, qi_ref, k_ref, v_ref, ki_ref, wi_ref, *, d):
    x = x_ref[...]
    mod = mod_ref[...]
    h = _norm_mod(x, g_ref[...], mod[:, :d], mod[:, d:2 * d])
    proj = _dot(h.astype(BF16), w_ref[...])
    tm = x.shape[0]
    width = ATTN_HEADS * HEAD_DIM

    lane = lax.broadcasted_iota(jnp.int32, (1, 128), 1) & (HEAD_DIM - 1)
    ang = pos_ref[...].astype(F32) * inv_ref[...]
    cos, sin = jnp.cos(ang), jnp.sin(ang)
    t_a = jnp.where(lane < 2 * ROPE_HALF, cos, 1.0)
    t_b = jnp.where(lane < ROPE_HALF, -sin, 0.0)
    t_c = jnp.where((lane >= ROPE_HALF) & (lane < 2 * ROPE_HALF), sin, 0.0)

    def rope(a):
        return a * t_a + pltpu.roll(a, 128 - ROPE_HALF, 1) * t_b + pltpu.roll(a, ROPE_HALF, 1) * t_c

    u_ref[...] = proj[:, :width].astype(u_ref.dtype)
    q = proj[:, width:2 * width]
    qi = proj[:, 2 * width:3 * width]
    small = proj[:, 3 * width:3 * width + 256]
    qsq = q * q
    qg = qg_ref[...]
    for pair in range(ATTN_HEADS // 2):
        sl = slice(pair * 128, (pair + 1) * 128)
        qr = rope(q[:, sl] * qg)
        qir = rope(qi[:, sl])
        for half in range(2):
            hh = 2 * pair + half
            hs = slice(half * HEAD_DIM, (half + 1) * HEAD_DIM)
            ss = jnp.sum(qsq[:, hh * HEAD_DIM:(hh + 1) * HEAD_DIM], axis=-1, keepdims=True)
            r = lax.rsqrt(ss * (1.0 / HEAD_DIM) + EPS) * (HEAD_DIM ** -0.5)
            q_ref[hh] = (qr[:, hs] * r).astype(q_ref.dtype)
            qi_ref[hh] = qir[:, hs].astype(qi_ref.dtype)
    kv = small[:, :128]
    k = kv[:, :HEAD_DIM]
    rk = lax.rsqrt(jnp.mean(k * k, axis=-1, keepdims=True) + EPS)
    kr = rope(kv * kg_ref[...])
    k_ref[...] = (kr[:, :HEAD_DIM] * rk).astype(k_ref.dtype)
    v_ref[...] = kv[:, HEAD_DIM:].astype(v_ref.dtype)
    kiw = small[:, 128:256]
    ki_ref[...] = rope(kiw)[:, :HEAD_DIM].astype(ki_ref.dtype)
    wscale = (IDX_HEADS ** -0.5) * (HEAD_DIM ** -0.5)
    wi_ref[...] = pltpu.roll(kiw, HEAD_DIM, 1) * wscale


def _in_projection(x2, g, mod, layer, w_in, positions, q_norm_g, k_norm_g, bsz, seq):
    t, d = x2.shape
    tm = 512
    width = ATTN_HEADS * HEAD_DIM
    cuts = np.cumsum([width, width, HEAD_DIM, HEAD_DIM, width, HEAD_DIM, IDX_HEADS])
    u_w, q_w, k_w, v_w, qi_w, ki_w, wi_w = jnp.split(w_in, cuts[:-1].tolist(), axis=1)
    pad = jnp.zeros((d, 128 - HEAD_DIM - IDX_HEADS), w_in.dtype)
    w = jnp.concatenate([u_w, q_w, qi_w, k_w, v_w, ki_w, wi_w, pad], axis=1).astype(BF16)
    nw = w.shape[1]
    lane = np.arange(128) % HEAD_DIM
    inv = np.where(lane < 2 * ROPE_HALF,
                   np.power(ROPE_THETA, -2.0 * (lane % ROPE_HALF) / (2 * ROPE_HALF)), 0.0)
    inv = jnp.asarray(inv.reshape(1, 128), F32)
    qg2 = jnp.tile(q_norm_g.reshape(1, HEAD_DIM), (1, 2))
    kg2 = jnp.concatenate([k_norm_g.reshape(1, HEAD_DIM), jnp.ones((1, HEAD_DIM), F32)], axis=1)
    tiles_per_seq = seq // tm
    row = lambda i: (i, 0)
    const = lambda i: (0, 0)
    hm = lambda i: (i // tiles_per_seq, 0, i % tiles_per_seq, 0)
    outs = pl.pallas_call(
        functools.partial(_inproj_kernel, d=d),
        grid=(t // tm,),
        in_specs=[pl.BlockSpec((tm, d), row),
                  pl.BlockSpec((1, d), const),
                  pl.BlockSpec((None, None, 1, 3 * d), lambda i: (layer, i // tiles_per_seq, 0, 0)),
                  pl.BlockSpec((d, nw), const),
                  pl.BlockSpec((tm, 1), row),
                  pl.BlockSpec((1, 128), const),
                  pl.BlockSpec((1, 128), const),
                  pl.BlockSpec((1, 128), const)],
        out_specs=[pl.BlockSpec((tm, width), row),
                   pl.BlockSpec((None, ATTN_HEADS, tm, HEAD_DIM), hm),
                   pl.BlockSpec((None, IDX_HEADS, tm, HEAD_DIM), hm),
                   pl.BlockSpec((tm, HEAD_DIM), row),
                   pl.BlockSpec((tm, HEAD_DIM), row),
                   pl.BlockSpec((tm, HEAD_DIM), row),
                   pl.BlockSpec((tm, 128), row)],
        out_shape=[jax.ShapeDtypeStruct((t, width), BF16),
                   jax.ShapeDtypeStruct((bsz, ATTN_HEADS, seq, HEAD_DIM), BF16),
                   jax.ShapeDtypeStruct((bsz, IDX_HEADS, seq, HEAD_DIM), BF16),
                   jax.ShapeDtypeStruct((t, HEAD_DIM), BF16),
                   jax.ShapeDtypeStruct((t, HEAD_DIM), BF16),
                   jax.ShapeDtypeStruct((t, HEAD_DIM), BF16),
                   jax.ShapeDtypeStruct((t, 128), F32)],
        compiler_params=_cparams(("arbitrary",), VMEM_LIMIT),
        name="in_projection",
    )(x2, g.reshape(1, d), mod, w, positions.reshape(t, 1), inv, qg2, kg2)
    return outs


def _s5_weights(lam_re, lam_im, log_dt, b_re, b_im, c_re, c_im, d_skip, glu_w, glu_b):
    L = S5_CHUNK
    g, p = lam_re.shape
    hch = S5_GROUP
    dt = jnp.exp(log_dt)[:, None]
    lr, li = lam_re, lam_im
    tau = jnp.arange(L + 1, dtype=F32)[:, None, None]
    mag = jnp.exp(lr * dt * tau)
    pw_r, pw_i = mag * jnp.cos(li * dt * tau), mag * jnp.sin(li * dt * tau)
    ar, ai = pw_r[1], pw_i[1]
    nr, ni = ar - 1.0, ai
    den = lr * lr + li * li
    cr, ci = (nr * lr + ni * li) / den, (ni * lr - nr * li) / den
    bb_r = cr[..., None] * b_re - ci[..., None] * b_im
    bb_i = cr[..., None] * b_im + ci[..., None] * b_re
    cl_r = c_re[None] * pw_r[:L, :, None, :] - c_im[None] * pw_i[:L, :, None, :]
    cl_i = c_re[None] * pw_i[:L, :, None, :] + c_im[None] * pw_r[:L, :, None, :]
    taps = (jnp.einsum('tghp,gpk->tghk', cl_r, bb_r) - jnp.einsum('tghp,gpk->tghk', cl_i, bb_i))
    ii = jnp.arange(L)
    lag = ii[None, :] - ii[:, None]
    tap_ji = taps[jnp.clip(lag, 0, L - 1)]
    tap_ji = jnp.where((lag >= 0)[:, :, None, None, None], tap_ji, 0.0)
    m_intra = jnp.transpose(tap_ji, (2, 0, 4, 1, 3)).reshape(g, L * hch, L * hch)
    rev_r, rev_i = pw_r[L - 1 - ii], pw_i[L - 1 - ii]
    z_r = rev_r[..., None] * bb_r[None] - rev_i[..., None] * bb_i[None]
    z_i = rev_r[..., None] * bb_i[None] + rev_i[..., None] * bb_r[None]
    w_z = jnp.concatenate([jnp.transpose(z_r, (1, 0, 3, 2)).reshape(g, L * hch, p),
                           jnp.transpose(z_i, (1, 0, 3, 2)).reshape(g, L * hch, p)], axis=-1)
    w_a = jnp.concatenate([m_intra, w_z], axis=-1).astype(BF16)
    co_r = c_re[None] * pw_r[1:, :, None, :] - c_im[None] * pw_i[1:, :, None, :]
    co_i = c_re[None] * pw_i[1:, :, None, :] + c_im[None] * pw_r[1:, :, None, :]
    w_c = jnp.concatenate([jnp.transpose(co_r, (1, 3, 0, 2)).reshape(g, p, L * hch),
                           -jnp.transpose(co_i, (1, 3, 0, 2)).reshape(g, p, L * hch)], axis=1)
    al_r, al_i = pw_r[L], pw_i[L]
    dec_a = jnp.concatenate([al_r, al_r], axis=-1)[:, None, :]
    dec_b = jnp.concatenate([-al_i, al_i], axis=-1)[:, None, :]
    d_t = jnp.tile(d_skip[:, None, :], (1, L, 1)).reshape(g, 1, L * hch)
    glu_k = jnp.einsum('ij,ghk->gihjk', jnp.eye(L, dtype=F32), glu_w).reshape(g, L * hch, L * hch).astype(BF16)
    glu_bt = jnp.tile(glu_b[:, None, :], (1, L, 1)).reshape(g, 1, L * hch)
    return w_a, w_c, dec_a, dec_b, d_t, glu_k, glu_bt


def _s5_kernel(u_ref, wa_ref, wc_ref, da_ref, db_ref, d_ref, gk_ref, gb_ref, o_ref,
               zp_ref, zq_ref, sp_ref, *, bsz, nchunk):
    u = u_ref[...]
    cols = u.shape[1]
    p2 = zp_ref.shape[1]
    res = _dot(u, wa_ref[...])
    y_intra = res[:, :cols]
    z = res[:, cols:]
    zp_ref[...] = z
    zq_ref[...] = pltpu.roll(z, p2 // 2, 1)
    da = da_ref[...]
    db = db_ref[...]
    dbq = -db

    def step(c, carry):
        new = []
        for b in range(bsz):
            sp, sq = carry[2 * b], carry[2 * b + 1]
            r = b * nchunk + c
            sp_ref[pl.ds(r, 1), :] = sp
            zp = zp_ref[pl.ds(r, 1), :]
            zq = zq_ref[pl.ds(r, 1), :]
            new.append(da * sp + db * sq + zp)
            new.append(da * sq + dbq * sp + zq)
        return tuple(new)

    zero = jnp.zeros((1, p2), F32)
    lax.fori_loop(0, nchunk, step, (zero,) * (2 * bsz))
    s_prev = sp_ref[...]
    sh, sl = _split_bf16(s_prev)
    wc = wc_ref[...]
    wch, wcl = _split_bf16(wc)
    y = y_intra + (_dot(sh, wch) + (_dot(sh, wcl) + _dot(sl, wch))) + d_ref[...] * u.astype(F32)
    y = jax.nn.gelu(y)
    gate = jax.nn.sigmoid(_dot(y.astype(BF16), gk_ref[...]) + gb_ref[...])
    o_ref[...] = (y * gate).astype(o_ref.dtype)


def _s5_mixer(u2, weights, bsz, seq):
    w_a, w_c, dec_a, dec_b, d_t, glu_k, glu_bt = weights
    g = w_a.shape[0]
    r, _ = u2.shape
    cols = S5_CHUNK * S5_GROUP
    p2 = 2 * S5_STATE
    nchunk = seq // S5_CHUNK
    grp = lambda i: (i, 0, 0)
    return pl.pallas_call(
        functools.partial(_s5_kernel, bsz=bsz, nchunk=nchunk),
        grid=(g,),
        in_specs=[pl.BlockSpec((r, cols), lambda i: (0, i)),
                  pl.BlockSpec((None, cols, cols + p2), grp),
                  pl.BlockSpec((None, p2, cols), grp),
                  pl.BlockSpec((None, 1, p2), grp),
                  pl.BlockSpec((None, 1, p2), grp),
                  pl.BlockSpec((None, 1, cols), grp),
                  pl.BlockSpec((None, cols, cols), grp),
                  pl.BlockSpec((None, 1, cols), grp)],
        out_specs=pl.BlockSpec((r, cols), lambda i: (0, i)),
        out_shape=jax.ShapeDtypeStruct(u2.shape, BF16),
        scratch_shapes=[pltpu.VMEM((r, p2), F32), pltpu.VMEM((r, p2), F32), pltpu.VMEM((r, p2), F32)],
        compiler_params=_cparams(("arbitrary",)),
        name="s5_mixer",
    )(u2, w_a, w_c, dec_a, dec_b, d_t, glu_k, glu_bt)


def _dsa_kernel(q_ref, qi_ref, wi_ref, kt_ref, v_ref, kit_ref, o_ref,
                sc_ref, s_ref, wb_ref, m_ref, l_ref, acc_ref, *, tq, ck, k_sel, idx_bits, max_iter):
    i = pl.program_id(1)
    nheads = q_ref.shape[0]
    nj = ck // 128
    n_ck = (i * tq + tq + ck - 1) // ck
    shape = (tq, 128)
    t_pos = i * tq + lax.broadcasted_iota(jnp.int32, shape, 0)
    lane = lax.broadcasted_iota(jnp.int32, shape, 1)
    kf = float(k_sel)
    inf = jnp.inf

    def rep(col):
        return jnp.broadcast_to(col, shape)

    def cols(x):
        return [x[:, j * 128:(j + 1) * 128] for j in range(nj)]

    w = wi_ref[...]
    for h in range(nheads):
        wb_ref[h] = rep(w[:, h:h + 1])

    def score_chunk(c, carry):
        lo, hi = carry
        kit = kit_ref[c]
        accs = [jnp.zeros(shape, F32) for _ in range(nj)]
        for h in range(nheads):
            s = jnp.maximum(_dot(qi_ref[h], kit), 0.0)
            wb = wb_ref[h]
            accs = [a + wb * sj for a, sj in zip(accs, cols(s))]
        for j in range(nj):
            a = accs[j] + 0.0
            vis = (c * ck + j * 128) + lane <= t_pos
            sc_ref[c, :, j * 128:(j + 1) * 128] = jnp.where(vis, a, -inf)
            hi = jnp.maximum(hi, jnp.where(vis, a, -inf))
            lo = jnp.minimum(lo, jnp.where(vis, a, inf))
        return lo, hi

    lo, hi = lax.fori_loop(0, n_ck, score_chunk, (jnp.full(shape, inf, F32), jnp.full(shape, -inf, F32)))
    lo = rep(jnp.min(lo, axis=1, keepdims=True))
    hi = rep(jnp.max(hi, axis=1, keepdims=True))
    few = (t_pos + 1).astype(F32) < kf

    def reduce_rows(fn, init, combine, final):
        def body(c, part):
            for xj in cols(sc_ref[c]):
                part = combine(part, fn(xj))
            return part
        return rep(final(lax.fori_loop(0, n_ck, body, jnp.full(shape, init, F32)), axis=1, keepdims=True))

    def count(pred):
        return reduce_rows(lambda xj: jnp.where(pred(xj), 1.0, 0.0), 0.0, jnp.add, jnp.sum)

    def pivot(it, lo_, hi_, c_lo, c_hi):
        frac = jnp.clip((c_lo - (kf + 0.5)) / (c_lo - c_hi), 0.0625, 0.9375)
        frac = jnp.where(it % 2 == 0, 0.5, frac)
        mid = lo_ + (hi_ - lo_) * frac
        half = lo_ + (hi_ - lo_) * 0.5
        return jnp.where(mid > lo_, jnp.where(mid < hi_, mid, half), half)

    def pending(lo_, hi_, mid, c_lo):
        unresolved = jnp.where(few, 0.0, jnp.where(c_lo == kf, 0.0, 1.0))
        movable = jnp.where(mid > lo_, jnp.where(mid < hi_, 1.0, 0.0), 0.0)
        return jnp.max(unresolved), jnp.max(unresolved * movable)

    def search_cond(st):
        return jnp.logical_and(st[0] < max_iter, st[-1] > 0.5)

    def search_body(st):
        it, lo_, hi_, c_lo, c_hi, mid, _, _ = st
        cnt = count(lambda xj: xj >= mid)
        up = cnt >= kf
        lo2 = jnp.where(up, mid, lo_)
        hi2 = jnp.where(up, hi_, mid)
        c_lo2 = jnp.where(up, cnt, c_lo)
        c_hi2 = jnp.where(up, c_hi, cnt)
        mid2 = pivot(it + 1, lo2, hi2, c_lo2, c_hi2)
        open_, go = pending(lo2, hi2, mid2, c_lo2)
        return it + 1, lo2, hi2, c_lo2, c_hi2, mid2, open_, go

    c_lo0 = (t_pos + 1).astype(F32)
    c_hi0 = jnp.zeros(shape, F32)
    mid0 = pivot(jnp.int32(0), lo, hi, c_lo0, c_hi0)
    open0, go0 = pending(lo, hi, mid0, c_lo0)
    st = lax.while_loop(search_cond, search_body, (jnp.int32(0), lo, hi, c_lo0, c_hi0, mid0, open0, go0))
    lo, open_ = st[1], st[6]

    def resolve():
        def walk(thr_, strict):
            keep = (lambda xj: xj > thr_) if strict else (lambda xj: xj >= thr_)
            return reduce_rows(lambda xj: jnp.where(keep(xj), xj, inf), inf, jnp.minimum, jnp.min)

        def above_equal(thr_):
            return count(lambda xj: xj > thr_), count(lambda xj: xj == thr_)

        def too_low(n_gt):
            return jnp.where(few, 0.0, jnp.where(n_gt >= kf, 1.0, 0.0))

        thr0 = walk(lo, False)
        gt0, eq0 = above_equal(thr0)

        def walk_body(st):
            thr_, n_gt, _ = st
            thr2 = jnp.where(too_low(n_gt) > 0.5, walk(thr_, True), thr_)
            return (thr2,) + above_equal(thr2)

        thr_, n_gt, n_eq = lax.while_loop(lambda st: jnp.max(too_low(st[1])) > 0.5, walk_body, (thr0, gt0, eq0))
        need = kf - n_gt

        def idx_step(b, cur):
            trial = cur | (jnp.int32(1) << (idx_bits - 1 - b))

            def body(c, part):
                for j, xj in enumerate(cols(sc_ref[c])):
                    idx = (c * ck + j * 128) + lane
                    part = part + jnp.where(xj == thr_, jnp.where(idx < trial, 1.0, 0.0), 0.0)
                return part
            cnt = rep(jnp.sum(lax.fori_loop(0, n_ck, body, jnp.zeros(shape, F32)), axis=1, keepdims=True))
            return jnp.where(cnt < need, trial, cur)

        cut_ = lax.fori_loop(0, idx_bits, idx_step, jnp.zeros(shape, jnp.int32))
        return thr_, cut_

    thr, cut = lax.cond(open_ > 0.5, resolve, lambda: (lo, jnp.full(shape, 0x7FFFFFFF, jnp.int32)))

    def bias_chunk(c, _):
        x = sc_ref[c]
        for j, xj in enumerate(cols(x)):
            idx = (c * ck + j * 128) + lane
            at_thr = jnp.where(xj == thr, jnp.where(idx <= cut, 0.0, NEG_BIG), NEG_BIG)
            sc_ref[c, :, j * 128:(j + 1) * 128] = jnp.where(xj > thr, 0.0, at_thr)
        return 0

    lax.fori_loop(0, n_ck, bias_chunk, 0)

    for h in range(nheads):
        m_ref[h] = jnp.full(shape, NEG_BIG, F32)
        l_ref[h] = jnp.zeros(shape, F32)
        acc_ref[h] = jnp.zeros((tq, HEAD_DIM), F32)

    def attn_scores(c, _):
        kt = kt_ref[c]
        bias = cols(sc_ref[c])
        for h in range(nheads):
            s = cols(_dot(q_ref[h], kt))
            part = m_ref[h]
            for j in range(nj):
                sj = s[j] + bias[j]
                s_ref[h, c, :, j * 128:(j + 1) * 128] = sj
                part = jnp.maximum(part, sj)
            m_ref[h] = part
        return 0

    lax.fori_loop(0, n_ck, attn_scores, 0)
    for h in range(nheads):
        m_ref[h] = rep(jnp.max(m_ref[h], axis=1, keepdims=True))

    def attn_values(c, _):
        off = pl.multiple_of(c * ck, ck)
        vc = v_ref[pl.ds(off, ck), :]
        for h in range(nheads):
            m = m_ref[h]
            p = [jnp.exp(sj - m) for sj in cols(s_ref[h, c])]
            l_ref[h] += functools.reduce(jnp.add, p)
            acc_ref[h] += _dot(jnp.concatenate(p, axis=1).astype(BF16), vc)
        return 0

    lax.fori_loop(0, n_ck, attn_values, 0)
    for h in range(nheads):
        l_row = jnp.sum(l_ref[h], axis=1, keepdims=True)
        o_ref[:, h * HEAD_DIM:(h + 1) * HEAD_DIM] = (acc_ref[h] / l_row).astype(o_ref.dtype)


def _dsa_attention(q_hm, qi_hm, wi, k, v, ki, bsz, seq):
    tq = 128
    ck = min(512, seq)
    n_ck = seq // ck
    k_sel = min(TOPK_MAX, seq // 4)
    nq = seq // tq
    width = ATTN_HEADS * HEAD_DIM
    kt = k.reshape(bsz, n_ck, ck, HEAD_DIM).transpose(0, 1, 3, 2)
    kit = ki.reshape(bsz, n_ck, ck, HEAD_DIM).transpose(0, 1, 3, 2)
    hm = lambda b, i: (b, 0, i, 0)
    per_batch = lambda b, i: (b, 0, 0, 0)
    nh = ATTN_HEADS
    return pl.pallas_call(
        functools.partial(_dsa_kernel, tq=tq, ck=ck, k_sel=k_sel, idx_bits=(seq - 1).bit_length(), max_iter=48),
        grid=(bsz, nq),
        in_specs=[pl.BlockSpec((None, ATTN_HEADS, tq, HEAD_DIM), hm),
                  pl.BlockSpec((None, IDX_HEADS, tq, HEAD_DIM), hm),
                  pl.BlockSpec((tq, 128), lambda b, i: (b * nq + i, 0)),
                  pl.BlockSpec((None, n_ck, HEAD_DIM, ck), per_batch),
                  pl.BlockSpec((seq, HEAD_DIM), lambda b, i: (b, 0)),
                  pl.BlockSpec((None, n_ck, HEAD_DIM, ck), per_batch)],
        out_specs=pl.BlockSpec((tq, width), lambda b, i: (b * nq + i, 0)),
        out_shape=jax.ShapeDtypeStruct((bsz * seq, width), BF16),
        scratch_shapes=[pltpu.VMEM((n_ck, tq, ck), F32),
                        pltpu.VMEM((nh, n_ck, tq, ck), F32),
                        pltpu.VMEM((IDX_HEADS, tq, 128), F32),
                        pltpu.VMEM((nh, tq, 128), F32),
                        pltpu.VMEM((nh, tq, 128), F32),
                        pltpu.VMEM((nh, tq, HEAD_DIM), F32)],
        compiler_params=_cparams(("arbitrary", "arbitrary"), VMEM_LIMIT),
        name="dsa_attention",
    )(q_hm, qi_hm, wi, kt, v, kit)


def _outproj_kernel(x_ref, ys_ref, ya_ref, w_ref, mod_ref, o_ref, *, d):
    width = ys_ref.shape[1]
    y = _dot(ys_ref[...], w_ref[:width, :]) + _dot(ya_ref[...], w_ref[width:, :])
    o_ref[...] = x_ref[...] + mod_ref[...][:, 2 * d:] * y


def _out_projection(x2, y_ssm, y_att, w_out, mod, layer, seq):
    t, d = x2.shape
    tm = 512
    width = y_ssm.shape[1]
    tiles_per_seq = seq // tm
    row = lambda i: (i, 0)
    return pl.pallas_call(
        functools.partial(_outproj_kernel, d=d),
        grid=(t // tm,),
        in_specs=[pl.BlockSpec((tm, d), row),
                  pl.BlockSpec((tm, width), row),
                  pl.BlockSpec((tm, width), row),
                  pl.BlockSpec((2 * width, d), lambda i: (0, 0)),
                  pl.BlockSpec((None, None, 1, 3 * d), lambda i: (layer, i // tiles_per_seq, 0, 0))],
        out_specs=pl.BlockSpec((tm, d), row),
        out_shape=jax.ShapeDtypeStruct((t, d), F32),
        compiler_params=_cparams(("arbitrary",)),
        name="out_projection",
    )(x2, y_ssm, y_att, w_out.astype(BF16), mod)


def _pool_kernel(x_ref, halo_ref, g_ref, mod_ref, pw_ref, ps_ref, o_ref, h_ref, *, d, tiles_per_seq):
    i = pl.program_id(0)
    tm = x_ref.shape[0]
    mod = mod_ref[...]
    shift, scale, gate = mod[:, :d], mod[:, d:2 * d], mod[:, 2 * d:]
    x = x_ref[...]
    first = (i % tiles_per_seq) == 0
    h_ref[POOL_HALO:, :] = _norm_mod(x, g_ref[...], shift, scale)
    halo = _norm_mod(halo_ref[...], g_ref[...], shift, scale)
    h_ref[:POOL_HALO, :] = jnp.where(first, 0.0, halo)
    pos = (i % tiles_per_seq) * tm + lax.broadcasted_iota(jnp.int32, (tm, 1), 0)
    gw = d // len(POOL_WINDOWS)
    ys = []
    for gi, win in enumerate(POOL_WINDOWS):
        cs = slice(gi * gw, (gi + 1) * gw)
        cur = h_ref[POOL_HALO:, cs]
        tot = cur
        for k in range(1, win):
            tot = tot + h_ref[POOL_HALO - k:POOL_HALO - k + tm, cs]
        cnt = jnp.minimum(pos + 1, win).astype(F32)
        pooled = tot / cnt - cur
        ys.append(_dot(pooled.astype(BF16), pw_ref[gi]))
    y = jnp.concatenate(ys, axis=1) * ps_ref[...]
    o_ref[...] = x + gate * y


def _pool_layer(x2, g, mod, layer, pool_w, pool_scale, seq):
    t, d = x2.shape
    tm = 512
    tiles_per_seq = seq // tm
    gw = d // len(POOL_WINDOWS)
    hb = tm // POOL_HALO
    return pl.pallas_call(
        functools.partial(_pool_kernel, d=d, tiles_per_seq=tiles_per_seq),
        grid=(t // tm,),
        in_specs=[pl.BlockSpec((tm, d), lambda i: (i, 0)),
                  pl.BlockSpec((POOL_HALO, d), lambda i: (jnp.maximum(i * hb - 1, 0), 0)),
                  pl.BlockSpec((1, d), lambda i: (0, 0)),
                  pl.BlockSpec((None, None, 1, 3 * d), lambda i: (layer, i // tiles_per_seq, 0, 0)),
                  pl.BlockSpec((len(POOL_WINDOWS), gw, gw), lambda i: (0, 0, 0)),
                  pl.BlockSpec((1, d), lambda i: (0, 0))],
        out_specs=pl.BlockSpec((tm, d), lambda i: (i, 0)),
        out_shape=jax.ShapeDtypeStruct((t, d), F32),
        scratch_shapes=[pltpu.VMEM((tm + POOL_HALO, d), F32)],
        compiler_params=_cparams(("arbitrary",)),
        name="pool_mixer",
    )(x2, x2, g.reshape(1, d), mod, pool_w.astype(BF16), pool_scale.reshape(1, d))


def _first_max(vals, idx, big):
    m = jnp.max(vals, axis=0, keepdims=True)
    first = jnp.min(jnp.where(vals == m, idx, big), axis=0, keepdims=True)
    return m, first


def _pack_bf16_pairs(x):
    n = x.shape[1] // 2
    lo = pltpu.bitcast(x[:, :n].astype(BF16).astype(F32), jnp.int32)
    hi = pltpu.bitcast(x[:, n:].astype(BF16).astype(F32), jnp.int32)
    return hi | ((lo >> 16) & 0xFFFF)


def _unpack_bf16_pairs(w):
    lo = pltpu.bitcast(w << 16, F32)
    hi = pltpu.bitcast(w & jnp.int32(-65536), F32)
    return jnp.concatenate([lo, hi], axis=1)


def _router_kernel(x_ref, g_ref, mod_ref, rw_ref, rb_ref, tri_ref, h_ref, gate_ref, rank_ref, cnt_ref,
                   run_ref, *, d, n_exp):
    mod = mod_ref[...]
    h = _norm_mod(x_ref[...], g_ref[...], mod[:, :d], mod[:, d:2 * d])
    h_ref[...] = _pack_bf16_pairs(h)
    tm = h.shape[0]
    hh, hl = _split_bf16(h)
    rw = rw_ref[...]
    rh, rl = _split_bf16(rw)
    logits = _dot_nt(rh, hh) + (_dot_nt(rh, hl) + _dot_nt(rl, hh))
    scores = jax.nn.sigmoid(logits)
    sel = scores + rb_ref[...]
    gsz = n_exp // N_EXPERT_GROUPS
    neg = -jnp.inf
    shape3 = (N_EXPERT_GROUPS, gsz, tm)
    sel3 = sel.reshape(shape3)
    sub = lax.broadcasted_iota(jnp.int32, shape3, 1)
    m1 = jnp.max(sel3, axis=1, keepdims=True)
    f1 = jnp.min(jnp.where(sel3 == m1, sub, gsz), axis=1, keepdims=True)
    m2 = jnp.max(jnp.where(sub == f1, neg, sel3), axis=1, keepdims=True)
    gscore = jnp.broadcast_to(m1 + m2, shape3).reshape(n_exp, tm)
    eidx = lax.broadcasted_iota(jnp.int32, (n_exp, tm), 0)
    gidx = eidx // gsz
    keep = jnp.zeros((n_exp, tm), F32)
    work = gscore
    for _ in range(TOPK_GROUPS):
        _, first = _first_max(work, gidx, N_EXPERT_GROUPS)
        hit = gidx == first
        keep = jnp.where(hit, 1.0, keep)
        work = jnp.where(hit, neg, work)
    work = jnp.where(keep > 0.0, sel, neg)
    chosen = jnp.zeros((n_exp, tm), F32)
    for _ in range(TOP_K):
        _, first = _first_max(work, eidx, n_exp)
        hit = eidx == first
        chosen = jnp.where(hit, 1.0, chosen)
        work = jnp.where(hit, neg, work)
    picked = chosen * scores
    gate_ref[...] = picked / jnp.sum(picked, axis=0, keepdims=True) * ROUTED_SCALE

    @pl.when(pl.program_id(0) == 0)
    def _():
        run_ref[...] = jnp.zeros(run_ref.shape, F32)

    before = _dot(chosen.astype(BF16), tri_ref[...])
    run = run_ref[...]
    rank_ref[...] = jnp.where(chosen > 0.0, before + run[:, :1], -1.0)
    run = run + jnp.broadcast_to(jnp.sum(chosen, axis=1, keepdims=True), run.shape)
    run_ref[...] = run
    cnt_ref[...] = run


def _ffn_router(x2, g, mod, layer, router_w, router_bias, seq):
    t, d = x2.shape
    n_exp = router_w.shape[1]
    tm = 512
    tiles_per_seq = seq // tm
    tri = (np.arange(tm)[:, None] < np.arange(tm)[None, :]).astype(np.float32)
    return pl.pallas_call(
        functools.partial(_router_kernel, d=d, n_exp=n_exp),
        grid=(t // tm,),
        in_specs=[pl.BlockSpec((tm, d), lambda i: (i, 0)),
                  pl.BlockSpec((1, d), lambda i: (0, 0)),
                  pl.BlockSpec((None, None, 1, 3 * d), lambda i: (layer, i // tiles_per_seq, 0, 0)),
                  pl.BlockSpec((n_exp, d), lambda i: (0, 0)),
                  pl.BlockSpec((n_exp, 1), lambda i: (0, 0)),
                  pl.BlockSpec((tm, tm), lambda i: (0, 0))],
        out_specs=[pl.BlockSpec((tm, d // 2), lambda i: (i, 0)),
                   pl.BlockSpec((n_exp, tm), lambda i: (0, i)),
                   pl.BlockSpec((n_exp, tm), lambda i: (0, i)),
                   pl.BlockSpec((n_exp, 128), lambda i: (0, 0))],
        out_shape=[jax.ShapeDtypeStruct((t, d // 2), jnp.int32),
                   jax.ShapeDtypeStruct((n_exp, t), F32),
                   jax.ShapeDtypeStruct((n_exp, t), F32),
                   jax.ShapeDtypeStruct((n_exp, 128), F32)],
        scratch_shapes=[pltpu.VMEM((n_exp, 128), F32)],
        compiler_params=_cparams(("arbitrary",)),
        name="ffn_router",
    )(x2, g.reshape(1, d), mod, router_w.T, router_bias.reshape(n_exp, 1), jnp.asarray(tri, BF16))


def _assign_kernel(rank_ref, gate_ref, start_ref, pos_ref, w_ref, *, n_exp):
    rank = rank_ref[...]
    gates = gate_ref[...]
    tm = rank.shape[1]
    slot = rank + start_ref[...]
    eidx = lax.broadcasted_iota(jnp.int32, (n_exp, tm), 0).astype(F32)
    alive = jnp.where(rank >= 0.0, eidx, float(n_exp))
    kidx = lax.broadcasted_iota(jnp.int32, (TOP_K, tm), 0)
    pos = jnp.zeros((TOP_K, tm), F32)
    wts = jnp.zeros((TOP_K, tm), F32)
    for k in range(TOP_K):
        first = jnp.min(alive, axis=0, keepdims=True)
        hit = alive == first
        pos_k = jnp.sum(jnp.where(hit, slot, 0.0), axis=0, keepdims=True)
        w_k = jnp.sum(jnp.where(hit, gates, 0.0), axis=0, keepdims=True)
        pos = jnp.where(kidx == k, pos_k, pos)
        wts = jnp.where(kidx == k, w_k, wts)
        alive = jnp.where(hit, float(n_exp), alive)
    pos_ref[...] = pos.astype(jnp.int32)
    w_ref[...] = wts


def _assign_slots(ranks, gates, start):
    n_exp, t = ranks.shape
    tm = 512
    return pl.pallas_call(
        functools.partial(_assign_kernel, n_exp=n_exp),
        grid=(t // tm,),
        in_specs=[pl.BlockSpec((n_exp, tm), lambda i: (0, i)),
                  pl.BlockSpec((n_exp, tm), lambda i: (0, i)),
                  pl.BlockSpec((n_exp, 1), lambda i: (0, 0))],
        out_specs=[pl.BlockSpec((TOP_K, tm), lambda i: (0, i)),
                   pl.BlockSpec((TOP_K, tm), lambda i: (0, i))],
        out_shape=[jax.ShapeDtypeStruct((TOP_K, t), jnp.int32), jax.ShapeDtypeStruct((TOP_K, t), F32)],
        compiler_params=_cparams(("arbitrary",)),
        name="moe_assign",
    )(ranks, gates, start)


SC_CORES = 2
SC_SUBCORES = 16
SC_WINDOW = 128


def _sc_mesh():
    return plsc.VectorSubcoreMesh(core_axis_name="c", subcore_axis_name="s",
                                  num_cores=SC_CORES, num_subcores=SC_SUBCORES)


def _sc_scatter_rows(rows, pos, n_out):
    t, width = rows.shape
    nk = pos.shape[0]
    win = SC_WINDOW
    n_win = t // win // (SC_CORES * SC_SUBCORES)
    pos_w = pos.reshape(nk, t // win, win).transpose(1, 0, 2)

    def body(rows_hbm, pos_hbm, out_hbm, idx_v, buf_v, sem):
        wid = lax.axis_index("s") * SC_CORES + lax.axis_index("c")

        @pl.loop(0, n_win)
        def _(j):
            w = wid * n_win + j
            pltpu.sync_copy(rows_hbm.at[pl.ds(w * win, win)], buf_v)
            pltpu.sync_copy(pos_hbm.at[w], idx_v)
            copies = [pltpu.make_async_copy(buf_v, out_hbm.at[idx_v.at[k]], sem) for k in range(nk)]
            for cp in copies:
                cp.start()
            for cp in copies:
                cp.wait()

    return pl.kernel(
        body, out_type=jax.ShapeDtypeStruct((n_out, width), jnp.int32), mesh=_sc_mesh(),
        scratch_types=[pltpu.VMEM((nk, win), jnp.int32), pltpu.VMEM((win, width), jnp.int32),
                       pltpu.SemaphoreType.DMA],
        name="sc_scatter_rows",
    )(rows, pos_w)


SC_GATHER_WINDOW = 64


def _sc_gather_rows(table, pos_flat):
    m = pos_flat.shape[0]
    width = table.shape[1]
    win = SC_GATHER_WINDOW
    workers = SC_CORES * SC_SUBCORES
    n_win = m // win // workers
    pos_w = pos_flat.reshape(workers, n_win, win)

    def body(table_hbm, pos_hbm, out_hbm, idx_v, buf_v, sem_g, sem_w):
        wid = lax.axis_index("s") * SC_CORES + lax.axis_index("c")
        base = wid * (n_win * win)
        pltpu.sync_copy(pos_hbm.at[wid], idx_v)

        def gather(j, b):
            return pltpu.make_async_copy(table_hbm.at[idx_v.at[j]], buf_v.at[b], sem_g.at[b])

        def write(j, b):
            return pltpu.make_async_copy(buf_v.at[b], out_hbm.at[pl.ds(base + j * win, win)], sem_w.at[b])

        gather(0, 0).start()

        @pl.loop(0, n_win, step=2)
        def _(j0):
            for b in range(2):
                j = j0 + b
                gather(j, b).wait()
                write(j, b).start()

                @pl.when(j + 1 < n_win)
                def _():
                    @pl.when(j >= 1)
                    def _():
                        write(j - 1, 1 - b).wait()
                    gather(j + 1, 1 - b).start()

        write(n_win - 2, 0).wait()
        write(n_win - 1, 1).wait()

    return pl.kernel(
        body, out_type=jax.ShapeDtypeStruct((m, width), jnp.int32), mesh=_sc_mesh(),
        scratch_types=[pltpu.VMEM((n_win, win), jnp.int32), pltpu.VMEM((2, win, width), jnp.int32),
                       pltpu.SemaphoreType.DMA((2,)), pltpu.SemaphoreType.DMA((2,))],
        name="sc_gather_rows",
    )(table, pos_w)


MOE_BLOCK = 512


def _silu_mul(a, b):
    return (a * jax.nn.sigmoid(a)) * b


def _expert_ffn_kernel(be_ref, nv_ref, xs_ref, wg_ref, wu_ref, wd_ref, ys_ref, wg_bf, wu_bf, wd_bf):
    i = pl.program_id(0)
    fresh = jnp.logical_or(i == 0, be_ref[i] != be_ref[jnp.maximum(i - 1, 0)])

    @pl.when(fresh)
    def _():
        wg_bf[...] = wg_ref[...].astype(BF16)
        wu_bf[...] = wu_ref[...].astype(BF16)
        wd_bf[...] = wd_ref[...].astype(BF16)

    nv = nv_ref[i]

    @pl.when(nv > 0)
    def _():
        x = _unpack_bf16_pairs(xs_ref[...])
        row = lax.broadcasted_iota(jnp.int32, (x.shape[0], 1), 0)
        x = jnp.where(row < nv, x, 0.0).astype(BF16)
        mid = _silu_mul(_dot(x, wg_bf[...]), _dot(x, wu_bf[...]))
        ys_ref[...] = _pack_bf16_pairs(_dot(mid.astype(BF16), wd_bf[...]))

    @pl.when(nv <= 0)
    def _():
        ys_ref[...] = jnp.zeros(ys_ref.shape, ys_ref.dtype)


def _expert_ffn(xs, block_expert, block_rows, w_gate, w_up, w_down):
    n_rows, half = xs.shape
    n_exp, d, de = w_gate.shape
    nb = n_rows // MOE_BLOCK
    wmap = lambda i, be, nv: (be[i], 0, 0)
    grid_spec = pltpu.PrefetchScalarGridSpec(
        num_scalar_prefetch=2,
        grid=(nb,),
        in_specs=[pl.BlockSpec((MOE_BLOCK, half), lambda i, be, nv: (i, 0)),
                  pl.BlockSpec((None, d, de), wmap),
                  pl.BlockSpec((None, d, de), wmap),
                  pl.BlockSpec((None, de, d), wmap)],
        out_specs=pl.BlockSpec((MOE_BLOCK, half), lambda i, be, nv: (i, 0)),
        scratch_shapes=[pltpu.VMEM((d, de), BF16), pltpu.VMEM((d, de), BF16), pltpu.VMEM((de, d), BF16)],
    )
    return pl.pallas_call(
        _expert_ffn_kernel,
        grid_spec=grid_spec,
        out_shape=jax.ShapeDtypeStruct((n_rows, half), jnp.int32),
        compiler_params=_cparams(("arbitrary",)),
        name="moe_expert_ffn",
    )(block_expert, block_rows, xs, w_gate, w_up, w_down)


def _combine_kernel(yt_ref, wt_ref, hp_ref, x_ref, mod_ref, sg_ref, su_ref, sd_ref, o_ref, *, d):
    h = _unpack_bf16_pairs(hp_ref[...]).astype(BF16)
    acc = _dot(_silu_mul(_dot(h, sg_ref[...]), _dot(h, su_ref[...])).astype(BF16), sd_ref[...])
    w = wt_ref[...]
    for k in range(TOP_K):
        acc = acc + w[:, k:k + 1] * _unpack_bf16_pairs(yt_ref[k])
    o_ref[...] = x_ref[...] + mod_ref[...][:, 2 * d:] * acc


def _combine(y_tok, w_tok, h_packed, x2, mod, layer, sh_gate, sh_up, sh_down, seq):
    t, d = x2.shape
    ds = sh_gate.shape[1]
    tm = 512
    tiles_per_seq = seq // tm
    row = lambda i: (i, 0)
    const = lambda i: (0, 0)
    return pl.pallas_call(
        functools.partial(_combine_kernel, d=d),
        grid=(t // tm,),
        in_specs=[pl.BlockSpec((TOP_K, tm, d // 2), lambda i: (0, i, 0)),
                  pl.BlockSpec((tm, TOP_K), row),
                  pl.BlockSpec((tm, d // 2), row),
                  pl.BlockSpec((tm, d), row),
                  pl.BlockSpec((None, None, 1, 3 * d), lambda i: (layer, i // tiles_per_seq, 0, 0)),
                  pl.BlockSpec((d, ds), const),
                  pl.BlockSpec((d, ds), const),
                  pl.BlockSpec((ds, d), const)],
        out_specs=pl.BlockSpec((tm, d), row),
        out_shape=jax.ShapeDtypeStruct((t, d), F32),
        compiler_params=_cparams(("arbitrary",), VMEM_LIMIT),
        name="moe_combine",
    )(y_tok, w_tok, h_packed, x2, mod, sh_gate.astype(BF16), sh_up.astype(BF16), sh_down.astype(BF16))


def _moe_layer(x2, h_packed, gates, ranks, counts, mod, layer, w_gate, w_up, w_down, sh_gate, sh_up, sh_down, seq):
    t, d = x2.shape
    n_exp = w_gate.shape[0]
    cnt = counts[:, 0].astype(jnp.int32)
    padded = (cnt + MOE_BLOCK - 1) // MOE_BLOCK * MOE_BLOCK
    seg_end = jnp.cumsum(padded)
    seg_start = seg_end - padded
    nb = t * TOP_K // MOE_BLOCK + n_exp
    first_row = jnp.arange(nb, dtype=jnp.int32) * MOE_BLOCK
    block_expert = jnp.sum((seg_end[None, :] <= first_row[:, None]).astype(jnp.int32), axis=1)
    block_expert = jnp.minimum(block_expert, n_exp - 1)
    block_rows = jnp.clip(seg_start[block_expert] + cnt[block_expert] - first_row, 0, MOE_BLOCK).astype(jnp.int32)
    pos, w = _assign_slots(ranks, gates, seg_start.astype(F32).reshape(n_exp, 1))
    xs = _sc_scatter_rows(h_packed, pos, nb * MOE_BLOCK)
    ys = _expert_ffn(xs, block_expert, block_rows, w_gate, w_up, w_down)
    y_tok = _sc_gather_rows(ys, pos.reshape(TOP_K * t)).reshape(TOP_K, t, d // 2)
    return _combine(y_tok, w.T, h_packed, x2, mod, layer, sh_gate, sh_up, sh_down, seq)


def kernel(x, c, positions, mix_norm_g, mix_mod_w, mix_mod_b, ffn_norm_g, ffn_mod_w, ffn_mod_b,
           hyb_w_in, hyb_w_out, s5_lambda_re, s5_lambda_im, s5_log_dt, s5_b_re, s5_b_im,
           s5_c_re, s5_c_im, s5_d, s5_glu_w, s5_glu_b, attn_q_norm_g, attn_k_norm_g,
           pool_w, pool_scale, router_w, router_bias, exp_w_gate, exp_w_up, exp_w_down,
           sh_w_gate, sh_w_up, sh_w_down):
    bsz, seq, d = x.shape
    t = bsz * seq
    depth = mix_norm_g.shape[0]
    x2 = x.reshape(t, d)
    mix_mod = _mod_vectors(c, mix_mod_w, mix_mod_b).reshape(depth, bsz, 1, 3 * d)
    ffn_mod = _mod_vectors(c, ffn_mod_w, ffn_mod_b).reshape(depth, bsz, 1, 3 * d)
    for i in range(depth):
        j = i // 2
        if i % 2 == 0:
            u, q_hm, qi_hm, k, v, ki, wi = _in_projection(
                x2, mix_norm_g[i], mix_mod, i, hyb_w_in[j], positions, attn_q_norm_g[j], attn_k_norm_g[j], bsz, seq)
            nchunk = seq // S5_CHUNK
            groups = u.shape[1] // S5_GROUP
            u2 = u.reshape(bsz * nchunk, S5_CHUNK, groups, S5_GROUP).transpose(0, 2, 1, 3)
            u2 = u2.reshape(bsz * nchunk, groups * S5_CHUNK * S5_GROUP)
            weights = _s5_weights(s5_lambda_re[j], s5_lambda_im[j], s5_log_dt[j], s5_b_re[j], s5_b_im[j],
                                  s5_c_re[j], s5_c_im[j], s5_d[j], s5_glu_w[j], s5_glu_b[j])
            y2 = _s5_mixer(u2, weights, bsz, seq)
            y_ssm = y2.reshape(bsz * nchunk, groups, S5_CHUNK, S5_GROUP).transpose(0, 2, 1, 3).reshape(t, -1)
            y_att = _dsa_attention(q_hm, qi_hm, wi, k, v, ki, bsz, seq)
            x2 = _out_projection(x2, y_ssm, y_att, hyb_w_out[j], mix_mod, i, seq)
        else:
            x2 = _pool_layer(x2, mix_norm_g[i], mix_mod, i, pool_w[j], pool_scale[j], seq)
        h_packed, gates, ranks, counts = _ffn_router(x2, ffn_norm_g[i], ffn_mod, i, router_w[i], router_bias[i], seq)
        x2 = _moe_layer(x2, h_packed, gates, ranks, counts, ffn_mod, i, exp_w_gate[i], exp_w_up[i], exp_w_down[i],
                        sh_w_gate[i], sh_w_up[i], sh_w_down[i], seq)
    return x2.reshape(bsz, seq, d)
```

```python
import functools
import math

import numpy as np
import jax
import jax.numpy as jnp
from jax import lax
from jax.experimental import pallas as pl
from jax.experimental.pallas import tpu as pltpu
from jax.experimental.pallas import tpu_sc as plsc

F32 = jnp.float32
BF16 = jnp.bfloat16

EPS = 1e-6
S5_GROUP = 16
S5_STATE = 64
HEAD_DIM = 64
ATTN_HEADS = 8
IDX_HEADS = 8
ROPE_HALF = 8
ROPE_THETA = 500000.0
TOPK_MAX = 256
POOL_WINDOWS = (2, 4, 8, 16)
POOL_HALO = 16
N_EXPERT_GROUPS = 8
TOPK_GROUPS = 4
TOP_K = 8
ROUTED_SCALE = 2.5
S5_CHUNK = 16
NEG_BIG = -1e30
INT_MIN = -2147483648
VMEM_LIMIT = 56 * 1024 * 1024


def _cparams(sem, vmem=None):
    return pltpu.CompilerParams(dimension_semantics=sem, vmem_limit_bytes=vmem)


def _dot(a, b):
    return jnp.dot(a, b, preferred_element_type=F32)


def _dot_nt(a, b):
    return lax.dot_general(a, b, (((1,), (1,)), ((), ())), preferred_element_type=F32)


def _split_bf16(a):
    hi = a.astype(BF16)
    lo = (a - hi.astype(F32)).astype(BF16)
    return hi, lo


def _dot3(a, b):
    ah, al = _split_bf16(a)
    bh, bl = _split_bf16(b)
    return _dot(ah, bh) + (_dot(ah, bl) + _dot(al, bh))


def _norm_mod(x, g, shift, scale):
    y = x * lax.rsqrt(jnp.mean(x * x, axis=-1, keepdims=True) + EPS)
    return (y * g) * (1.0 + scale) + shift


def _mod_kernel(ct_ref, w_ref, b_ref, o_ref):
    ct = ct_ref[...]
    cs = ct * jax.nn.sigmoid(ct)
    w = w_ref[...]
    rows = [jnp.sum(w * cs[:, b:b + 1], axis=0, keepdims=True) for b in range(ct.shape[1])]
    o_ref[...] = jnp.concatenate(rows, axis=0) + b_ref[...]


def _mod_vectors(c, w, b):
    nl, d, n3 = w.shape
    bsz = c.shape[0]
    tn = 512
    return pl.pallas_call(
        _mod_kernel,
        grid=(nl, n3 // tn),
        in_specs=[pl.BlockSpec((d, bsz), lambda l, j: (0, 0)),
                  pl.BlockSpec((None, d, tn), lambda l, j: (l, 0, j)),
                  pl.BlockSpec((None, 1, tn), lambda l, j: (l, 0, j))],
        out_specs=pl.BlockSpec((None, bsz, tn), lambda l, j: (l, 0, j)),
        out_shape=jax.ShapeDtypeStruct((nl, bsz, n3), F32),
        compiler_params=_cparams(("arbitrary", "arbitrary")),
        name="mod_vectors",
    )(c.T, w, b.reshape(nl, 1, n3))


def _inproj_kernel(x_ref, g_ref, mod_ref, w_ref, pos_ref, inv_ref, qg_ref, kg_ref,
                   u_ref, q_ref, qi_ref, k_ref, v_ref, ki_ref, wi_ref, *, d):
    x = x_ref[...]
    mod = mod_ref[...]
    h = _norm_mod(x, g_ref[...], mod[:, :d], mod[:, d:2 * d])
    proj = _dot(h.astype(BF16), w_ref[...])
    tm = x.shape[0]
    width = ATTN_HEADS * HEAD_DIM

    lane = lax.broadcasted_iota(jnp.int32, (1, 128), 1) & (HEAD_DIM - 1)
    ang = pos_ref[...].astype(F32) * inv_ref[...]
    cos, sin = jnp.cos(ang), jnp.sin(ang)
    t_a = jnp.where(lane < 2 * ROPE_HALF, cos, 1.0)
    t_b = jnp.where(lane < ROPE_HALF, -sin, 0.0)
    t_c = jnp.where((lane >= ROPE_HALF) & (lane < 2 * ROPE_HALF), sin, 0.0)

    def rope(a):
        return a * t_a + pltpu.roll(a, 128 - ROPE_HALF, 1) * t_b + pltpu.roll(a, ROPE_HALF, 1) * t_c

    u_ref[...] = proj[:, :width].astype(u_ref.dtype)
    q = proj[:, width:2 * width]
    qi = proj[:, 2 * width:3 * width]
    small = proj[:, 3 * width:3 * width + 256]
    qsq = q * q
    qg = qg_ref[...]
    lane128 = lax.broadcasted_iota(jnp.int32, (1, 128), 1)
    for pair in range(ATTN_HEADS // 2):
        sl = slice(pair * 128, (pair + 1) * 128)
        r = []
        for half in range(2):
            hh = 2 * pair + half
            ss = jnp.sum(qsq[:, hh * HEAD_DIM:(hh + 1) * HEAD_DIM], axis=-1, keepdims=True)
            r.append(lax.rsqrt(ss * (1.0 / HEAD_DIM) + EPS) * (HEAD_DIM ** -0.5))
        qn = rope(q[:, sl] * qg) * jnp.where(lane128 < HEAD_DIM, r[0], r[1])
        qt = qn.T
        qit = rope(qi[:, sl]).T
        for half in range(2):
            hs = slice(half * HEAD_DIM, (half + 1) * HEAD_DIM)
            q_ref[2 * pair + half] = qt[hs].astype(q_ref.dtype)
            qi_ref[2 * pair + half] = qit[hs].astype(qi_ref.dtype)
    kv = small[:, :128]
    k = kv[:, :HEAD_DIM]
    rk = lax.rsqrt(jnp.mean(k * k, axis=-1, keepdims=True) + EPS)
    kr = rope(kv * kg_ref[...])
    k_ref[...] = (kr[:, :HEAD_DIM] * rk).astype(k_ref.dtype)
    v_ref[...] = kv[:, HEAD_DIM:].astype(v_ref.dtype)
    kiw = small[:, 128:256]
    ki_ref[...] = rope(kiw)[:, :HEAD_DIM].astype(ki_ref.dtype)
    wscale = (IDX_HEADS ** -0.5) * (HEAD_DIM ** -0.5)
    wi_ref[...] = pltpu.roll(kiw, HEAD_DIM, 1) * wscale


def _in_projection(x2, g, mod, layer, w_in, positions, q_norm_g, k_norm_g, bsz, seq):
    t, d = x2.shape
    tm = 512
    width = ATTN_HEADS * HEAD_DIM
    cuts = np.cumsum([width, width, HEAD_DIM, HEAD_DIM, width, HEAD_DIM, IDX_HEADS])
    u_w, q_w, k_w, v_w, qi_w, ki_w, wi_w = jnp.split(w_in, cuts[:-1].tolist(), axis=1)
    pad = jnp.zeros((d, 128 - HEAD_DIM - IDX_HEADS), w_in.dtype)
    w = jnp.concatenate([u_w, q_w, qi_w, k_w, v_w, ki_w, wi_w, pad], axis=1).astype(BF16)
    nw = w.shape[1]
    lane = np.arange(128) % HEAD_DIM
    inv = np.where(lane < 2 * ROPE_HALF,
                   np.power(ROPE_THETA, -2.0 * (lane % ROPE_HALF) / (2 * ROPE_HALF)), 0.0)
    inv = jnp.asarray(inv.reshape(1, 128), F32)
    qg2 = jnp.tile(q_norm_g.reshape(1, HEAD_DIM), (1, 2))
    kg2 = jnp.concatenate([k_norm_g.reshape(1, HEAD_DIM), jnp.ones((1, HEAD_DIM), F32)], axis=1)
    tiles_per_seq = seq // tm
    row = lambda i: (i, 0)
    const = lambda i: (0, 0)
    hm = lambda i: (i // tiles_per_seq, 0, 0, i % tiles_per_seq)
    outs = pl.pallas_call(
        functools.partial(_inproj_kernel, d=d),
        grid=(t // tm,),
        in_specs=[pl.BlockSpec((tm, d), row),
                  pl.BlockSpec((1, d), const),
                  pl.BlockSpec((None, None, 1, 3 * d), lambda i: (layer, i // tiles_per_seq, 0, 0)),
                  pl.BlockSpec((d, nw), const),
                  pl.BlockSpec((tm, 1), row),
                  pl.BlockSpec((1, 128), const),
                  pl.BlockSpec((1, 128), const),
                  pl.BlockSpec((1, 128), const)],
        out_specs=[pl.BlockSpec((tm, width), row),
                   pl.BlockSpec((None, ATTN_HEADS, HEAD_DIM, tm), hm),
                   pl.BlockSpec((None, IDX_HEADS, HEAD_DIM, tm), hm),
                   pl.BlockSpec((tm, HEAD_DIM), row),
                   pl.BlockSpec((tm, HEAD_DIM), row),
                   pl.BlockSpec((tm, HEAD_DIM), row),
                   pl.BlockSpec((tm, 128), row)],
        out_shape=[jax.ShapeDtypeStruct((t, width), BF16),
                   jax.ShapeDtypeStruct((bsz, ATTN_HEADS, HEAD_DIM, seq), BF16),
                   jax.ShapeDtypeStruct((bsz, IDX_HEADS, HEAD_DIM, seq), BF16),
                   jax.ShapeDtypeStruct((t, HEAD_DIM), BF16),
                   jax.ShapeDtypeStruct((t, HEAD_DIM), BF16),
                   jax.ShapeDtypeStruct((t, HEAD_DIM), BF16),
                   jax.ShapeDtypeStruct((t, 128), F32)],
        compiler_params=_cparams(("arbitrary",), VMEM_LIMIT),
        name="in_projection",
    )(x2, g.reshape(1, d), mod, w, positions.reshape(t, 1), inv, qg2, kg2)
    return outs


def _s5_weights(lam_re, lam_im, log_dt, b_re, b_im, c_re, c_im, d_skip, glu_w, glu_b):
    L = S5_CHUNK
    g, p = lam_re.shape
    hch = S5_GROUP
    dt = jnp.exp(log_dt)[:, None]
    lr, li = lam_re, lam_im
    tau = jnp.arange(L + 1, dtype=F32)[:, None, None]
    mag = jnp.exp(lr * dt * tau)
    pw_r, pw_i = mag * jnp.cos(li * dt * tau), mag * jnp.sin(li * dt * tau)
    ar, ai = pw_r[1], pw_i[1]
    nr, ni = ar - 1.0, ai
    den = lr * lr + li * li
    cr, ci = (nr * lr + ni * li) / den, (ni * lr - nr * li) / den
    bb_r = cr[..., None] * b_re - ci[..., None] * b_im
    bb_i = cr[..., None] * b_im + ci[..., None] * b_re
    cl_r = c_re[None] * pw_r[:L, :, None, :] - c_im[None] * pw_i[:L, :, None, :]
    cl_i = c_re[None] * pw_i[:L, :, None, :] + c_im[None] * pw_r[:L, :, None, :]
    taps = (jnp.einsum('tghp,gpk->tghk', cl_r, bb_r) - jnp.einsum('tghp,gpk->tghk', cl_i, bb_i))
    ii = jnp.arange(L)
    lag = ii[None, :] - ii[:, None]
    tap_ji = taps[jnp.clip(lag, 0, L - 1)]
    tap_ji = jnp.where((lag >= 0)[:, :, None, None, None], tap_ji, 0.0)
    m_intra = jnp.transpose(tap_ji, (2, 0, 4, 1, 3)).reshape(g, L * hch, L * hch)
    rev_r, rev_i = pw_r[L - 1 - ii], pw_i[L - 1 - ii]
    z_r = rev_r[..., None] * bb_r[None] - rev_i[..., None] * bb_i[None]
    z_i = rev_r[..., None] * bb_i[None] + rev_i[..., None] * bb_r[None]
    w_z = jnp.concatenate([jnp.transpose(z_r, (1, 0, 3, 2)).reshape(g, L * hch, p),
                           jnp.transpose(z_i, (1, 0, 3, 2)).reshape(g, L * hch, p)], axis=-1)
    w_a = jnp.concatenate([m_intra, w_z], axis=-1).astype(BF16)
    co_r = c_re[None] * pw_r[1:, :, None, :] - c_im[None] * pw_i[1:, :, None, :]
    co_i = c_re[None] * pw_i[1:, :, None, :] + c_im[None] * pw_r[1:, :, None, :]
    w_c = jnp.concatenate([jnp.transpose(co_r, (1, 3, 0, 2)).reshape(g, p, L * hch),
                           -jnp.transpose(co_i, (1, 3, 0, 2)).reshape(g, p, L * hch)], axis=1)
    al_r, al_i = pw_r[L], pw_i[L]
    dec_a = jnp.concatenate([al_r, al_r], axis=-1)[:, None, :]
    dec_b = jnp.concatenate([-al_i, al_i], axis=-1)[:, None, :]
    d_t = jnp.tile(d_skip[:, None, :], (1, L, 1)).reshape(g, 1, L * hch)
    glu_k = jnp.einsum('ij,ghk->gihjk', jnp.eye(L, dtype=F32), glu_w).reshape(g, L * hch, L * hch).astype(BF16)
    glu_bt = jnp.tile(glu_b[:, None, :], (1, L, 1)).reshape(g, 1, L * hch)
    return w_a, w_c, dec_a, dec_b, d_t, glu_k, glu_bt


def _s5_kernel(u_ref, wa_ref, wc_ref, da_ref, db_ref, d_ref, gk_ref, gb_ref, o_ref,
               zp_ref, zq_ref, sp_ref, *, bsz, nchunk):
    u = u_ref[...]
    cols = u.shape[1]
    p2 = zp_ref.shape[1]
    res = _dot(u, wa_ref[...])
    y_intra = res[:, :cols]
    z = res[:, cols:]
    zp_ref[...] = z
    zq_ref[...] = pltpu.roll(z, p2 // 2, 1)
    da = da_ref[...]
    db = db_ref[...]
    dbq = -db

    def step(c, carry):
        new = []
        for b in range(bsz):
            sp, sq = carry[2 * b], carry[2 * b + 1]
            r = b * nchunk + c
            sp_ref[pl.ds(r, 1), :] = sp
            zp = zp_ref[pl.ds(r, 1), :]
            zq = zq_ref[pl.ds(r, 1), :]
            new.append(da * sp + db * sq + zp)
            new.append(da * sq + dbq * sp + zq)
        return tuple(new)

    zero = jnp.zeros((1, p2), F32)
    lax.fori_loop(0, nchunk, step, (zero,) * (2 * bsz))
    s_prev = sp_ref[...]
    sh, sl = _split_bf16(s_prev)
    wc = wc_ref[...]
    wch, wcl = _split_bf16(wc)
    y = y_intra + (_dot(sh, wch) + (_dot(sh, wcl) + _dot(sl, wch))) + d_ref[...] * u.astype(F32)
    y = jax.nn.gelu(y)
    gate = jax.nn.sigmoid(_dot(y.astype(BF16), gk_ref[...]) + gb_ref[...])
    o_ref[...] = (y * gate).astype(o_ref.dtype)


def _s5_mixer(u2, weights, bsz, seq):
    w_a, w_c, dec_a, dec_b, d_t, glu_k, glu_bt = weights
    g = w_a.shape[0]
    r, _ = u2.shape
    cols = S5_CHUNK * S5_GROUP
    p2 = 2 * S5_STATE
    nchunk = seq // S5_CHUNK
    grp = lambda i: (i, 0, 0)
    return pl.pallas_call(
        functools.partial(_s5_kernel, bsz=bsz, nchunk=nchunk),
        grid=(g,),
        in_specs=[pl.BlockSpec((r, cols), lambda i: (0, i)),
                  pl.BlockSpec((None, cols, cols + p2), grp),
                  pl.BlockSpec((None, p2, cols), grp),
                  pl.BlockSpec((None, 1, p2), grp),
                  pl.BlockSpec((None, 1, p2), grp),
                  pl.BlockSpec((None, 1, cols), grp),
                  pl.BlockSpec((None, cols, cols), grp),
                  pl.BlockSpec((None, 1, cols), grp)],
        out_specs=pl.BlockSpec((r, cols), lambda i: (0, i)),
        out_shape=jax.ShapeDtypeStruct(u2.shape, BF16),
        scratch_shapes=[pltpu.VMEM((r, p2), F32), pltpu.VMEM((r, p2), F32), pltpu.VMEM((r, p2), F32)],
        compiler_params=_cparams(("arbitrary",)),
        name="s5_mixer",
    )(u2, w_a, w_c, dec_a, dec_b, d_t, glu_k, glu_bt)


def _dsa_kernel(qt_ref, qit_ref, wt_ref, k_ref, vt_ref, ki_ref, o_ref,
                sc_ref, s_ref, m_ref, l_ref, acc_ref, *, tq, ck, k_sel, idx_bits, max_iter):
    i = pl.program_id(1)
    nheads = qt_ref.shape[0]
    n_ck = (i * tq + tq + ck - 1) // ck
    t_pos = i * tq + lax.broadcasted_iota(jnp.int32, (1, tq), 1)
    krow = lax.broadcasted_iota(jnp.int32, (ck, 1), 0)
    kf = float(k_sel)
    inf = jnp.inf

    fr = 32
    sub = min(256, ck)

    def fold(x, op):
        return op(x.reshape(ck // fr, fr, tq), axis=0)

    wt = wt_ref[...]

    def score_chunk(c, carry):
        lo8, hi8 = carry
        for part in range(ck // sub):
            off = pl.multiple_of(c * ck + part * sub, sub)
            kic = ki_ref[pl.ds(off, sub), :]
            acc = jnp.zeros((sub, tq), F32)
            for h in range(nheads):
                acc = acc + wt[h:h + 1, :] * jnp.maximum(_dot(kic, qit_ref[h]), 0.0)
            acc = acc + 0.0
            vis = off + krow[:sub] <= t_pos
            sc_ref[c, part * sub:(part + 1) * sub, :] = jnp.where(vis, acc, -inf)
            hi8 = jnp.maximum(hi8, jnp.max(jnp.where(vis, acc, -inf).reshape(sub // fr, fr, tq), axis=0))
            lo8 = jnp.minimum(lo8, jnp.min(jnp.where(vis, acc, inf).reshape(sub // fr, fr, tq), axis=0))
        return lo8, hi8

    lo8, hi8 = lax.fori_loop(0, n_ck, score_chunk, (jnp.full((fr, tq), inf, F32), jnp.full((fr, tq), -inf, F32)))
    lo = jnp.min(lo8, axis=0, keepdims=True)
    hi = jnp.max(hi8, axis=0, keepdims=True)
    n_vis = (t_pos + 1).astype(F32)
    few = n_vis < kf

    def reduce_keys(fn, init, combine, op):
        def body(c, part):
            return combine(part, fold(fn(sc_ref[c], c * ck), op))
        return op(lax.fori_loop(0, n_ck, body, jnp.full((fr, tq), init, F32)), axis=0, keepdims=True)

    def count(pred):
        return reduce_keys(lambda x, off: jnp.where(pred(x, off), 1.0, 0.0), 0.0, jnp.add, jnp.sum)

    def bisect(lo_, hi_, c_lo):
        mid = lo_ + (hi_ - lo_) * 0.5
        cnt = count(lambda x, off: x >= mid)
        up = cnt >= kf
        return jnp.where(up, mid, lo_), jnp.where(up, hi_, mid), jnp.where(up, cnt, c_lo)

    def status(lo_, hi_, c_lo):
        mid = lo_ + (hi_ - lo_) * 0.5
        unresolved = jnp.where(few, 0.0, jnp.where(c_lo == kf, 0.0, 1.0))
        movable = jnp.where(mid > lo_, jnp.where(mid < hi_, 1.0, 0.0), 0.0)
        return jnp.max(unresolved), jnp.max(unresolved * movable)

    def search_body(st):
        it, lo_, hi_, c_lo, _, _ = st
        for _ in range(2):
            lo_, hi_, c_lo = bisect(lo_, hi_, c_lo)
        return (it + 2, lo_, hi_, c_lo) + status(lo_, hi_, c_lo)

    st = lax.while_loop(lambda st: jnp.logical_and(st[0] < max_iter, st[5] > 0.5), search_body,
                        (jnp.int32(0), lo, hi, n_vis) + status(lo, hi, n_vis))
    lo, open_ = st[1], st[4]

    def resolve():
        def walk(thr_, strict):
            keep = (lambda x: x > thr_) if strict else (lambda x: x >= thr_)
            return reduce_keys(lambda x, off: jnp.where(keep(x), x, inf), inf, jnp.minimum, jnp.min)

        def above_equal(thr_):
            return count(lambda x, off: x > thr_), count(lambda x, off: x == thr_)

        def too_low(n_gt):
            return jnp.where(few, 0.0, jnp.where(n_gt >= kf, 1.0, 0.0))

        thr0 = walk(lo, False)

        def walk_body(st):
            thr_, n_gt, _ = st
            thr2 = jnp.where(too_low(n_gt) > 0.5, walk(thr_, True), thr_)
            return (thr2,) + above_equal(thr2)

        thr_, n_gt, n_eq = lax.while_loop(lambda st: jnp.max(too_low(st[1])) > 0.5, walk_body,
                                          (thr0,) + above_equal(thr0))
        need = kf - n_gt

        def idx_step(b, cur):
            trial = cur | (jnp.int32(1) << (idx_bits - 1 - b))
            cnt = reduce_keys(lambda x, off: jnp.where(x == thr_, jnp.where(off + krow < trial, 1.0, 0.0), 0.0),
                              0.0, jnp.add, jnp.sum)
            return jnp.where(cnt < need, trial, cur)

        return thr_, lax.fori_loop(0, idx_bits, idx_step, jnp.zeros((1, tq), jnp.int32))

    thr, cut = lax.cond(open_ > 0.5, resolve, lambda: (lo, jnp.full((1, tq), 0x7FFFFFFF, jnp.int32)))

    def bias_chunk(c, _):
        x = sc_ref[c]
        at_thr = jnp.where(x == thr, jnp.where(c * ck + krow <= cut, 0.0, NEG_BIG), NEG_BIG)
        sc_ref[c] = jnp.where(x > thr, 0.0, at_thr)
        return 0

    lax.fori_loop(0, n_ck, bias_chunk, 0)

    for h in range(nheads):
        m_ref[h] = jnp.full((fr, tq), NEG_BIG, F32)
        l_ref[h] = jnp.zeros((fr, tq), F32)
        acc_ref[h] = jnp.zeros((HEAD_DIM, tq), F32)

    def attn_scores(c, _):
        off = pl.multiple_of(c * ck, ck)
        kc = k_ref[pl.ds(off, ck), :]
        bias = sc_ref[c]
        for h in range(nheads):
            s = _dot(kc, qt_ref[h]) + bias
            s_ref[h, c] = s
            m_ref[h] = jnp.maximum(m_ref[h], fold(s, jnp.max))
        return 0

    lax.fori_loop(0, n_ck, attn_scores, 0)
    for h in range(nheads):
        m_ref[h] = jnp.broadcast_to(jnp.max(m_ref[h], axis=0, keepdims=True), (fr, tq))

    def attn_values(c, _):
        vt = vt_ref[c]
        for h in range(nheads):
            p = jnp.exp(s_ref[h, c] - m_ref[h][:1, :])
            l_ref[h] += fold(p, jnp.sum)
            acc_ref[h] += _dot(vt, p.astype(BF16))
        return 0

    lax.fori_loop(0, n_ck, attn_values, 0)
    for h in range(nheads):
        l_row = jnp.sum(l_ref[h], axis=0, keepdims=True)
        o_ref[h * HEAD_DIM:(h + 1) * HEAD_DIM, :] = (acc_ref[h] / l_row).astype(o_ref.dtype)


def _dsa_attention(qt, qit, wi, k, v, ki, bsz, seq):
    tq = 128
    ck = min(512, seq)
    n_ck = seq // ck
    k_sel = min(TOPK_MAX, seq // 4)
    nq = seq // tq
    nh = ATTN_HEADS
    width = nh * HEAD_DIM
    vt = v.reshape(bsz, n_ck, ck, HEAD_DIM).transpose(0, 1, 3, 2)
    wt = wi[:, :IDX_HEADS].T
    qspec = pl.BlockSpec((None, nh, HEAD_DIM, tq), lambda b, i: (b, 0, 0, i))
    out_t = pl.pallas_call(
        functools.partial(_dsa_kernel, tq=tq, ck=ck, k_sel=k_sel, idx_bits=(seq - 1).bit_length(), max_iter=48),
        grid=(bsz, nq),
        in_specs=[qspec, qspec,
                  pl.BlockSpec((IDX_HEADS, tq), lambda b, i: (0, b * nq + i)),
                  pl.BlockSpec((seq, HEAD_DIM), lambda b, i: (b, 0)),
                  pl.BlockSpec((None, n_ck, HEAD_DIM, ck), lambda b, i: (b, 0, 0, 0)),
                  pl.BlockSpec((seq, HEAD_DIM), lambda b, i: (b, 0))],
        out_specs=pl.BlockSpec((None, width, tq), lambda b, i: (b, 0, i)),
        out_shape=jax.ShapeDtypeStruct((bsz, width, seq), BF16),
        scratch_shapes=[pltpu.VMEM((n_ck, ck, tq), F32),
                        pltpu.VMEM((nh, n_ck, ck, tq), F32),
                        pltpu.VMEM((nh, 32, tq), F32),
                        pltpu.VMEM((nh, 32, tq), F32),
                        pltpu.VMEM((nh, HEAD_DIM, tq), F32)],
        compiler_params=_cparams(("arbitrary", "arbitrary"), VMEM_LIMIT),
        name="dsa_attention",
    )(qt, qit, wt, k, vt, ki)
    return out_t.transpose(0, 2, 1).reshape(bsz * seq, width)


def _dsa_kernel_rows(q_ref, qi_ref, wi_ref, kt_ref, v_ref, kit_ref, o_ref,
                     sc_ref, s_ref, wb_ref, m_ref, l_ref, acc_ref, *, tq, ck, k_sel, idx_bits, max_iter):
    i = pl.program_id(1)
    nheads = q_ref.shape[0]
    nj = ck // 128
    n_ck = (i * tq + tq + ck - 1) // ck
    shape = (tq, 128)
    t_pos = i * tq + lax.broadcasted_iota(jnp.int32, shape, 0)
    lane = lax.broadcasted_iota(jnp.int32, shape, 1)
    kf = float(k_sel)
    inf = jnp.inf

    def rep(col):
        return jnp.broadcast_to(col, shape)

    def cols(x):
        return [x[:, j * 128:(j + 1) * 128] for j in range(nj)]

    w = wi_ref[...]
    for h in range(nheads):
        wb_ref[h] = rep(w[:, h:h + 1])

    def score_chunk(c, carry):
        lo, hi = carry
        kit = kit_ref[c]
        accs = [jnp.zeros(shape, F32) for _ in range(nj)]
        for h in range(nheads):
            s = jnp.maximum(_dot(qi_ref[h], kit), 0.0)
            wb = wb_ref[h]
            accs = [a + wb * sj for a, sj in zip(accs, cols(s))]
        for j in range(nj):
            a = accs[j] + 0.0
            vis = (c * ck + j * 128) + lane <= t_pos
            sc_ref[c, :, j * 128:(j + 1) * 128] = jnp.where(vis, a, -inf)
            hi = jnp.maximum(hi, jnp.where(vis, a, -inf))
            lo = jnp.minimum(lo, jnp.where(vis, a, inf))
        return lo, hi

    lo, hi = lax.fori_loop(0, n_ck, score_chunk, (jnp.full(shape, inf, F32), jnp.full(shape, -inf, F32)))
    lo = rep(jnp.min(lo, axis=1, keepdims=True))
    hi = rep(jnp.max(hi, axis=1, keepdims=True))
    few = (t_pos + 1).astype(F32) < kf

    def reduce_rows(fn, init, combine, final):
        def body(c, part):
            for xj in cols(sc_ref[c]):
                part = combine(part, fn(xj))
            return part
        return rep(final(lax.fori_loop(0, n_ck, body, jnp.full(shape, init, F32)), axis=1, keepdims=True))

    def count(pred):
        return reduce_rows(lambda xj: jnp.where(pred(xj), 1.0, 0.0), 0.0, jnp.add, jnp.sum)

    def pivot(it, lo_, hi_, c_lo, c_hi):
        frac = jnp.clip((c_lo - (kf + 0.5)) / (c_lo - c_hi), 0.0625, 0.9375)
        frac = jnp.where(it % 2 == 0, 0.5, frac)
        mid = lo_ + (hi_ - lo_) * frac
        half = lo_ + (hi_ - lo_) * 0.5
        return jnp.where(mid > lo_, jnp.where(mid < hi_, mid, half), half)

    def pending(lo_, hi_, mid, c_lo):
        unresolved = jnp.where(few, 0.0, jnp.where(c_lo == kf, 0.0, 1.0))
        movable = jnp.where(mid > lo_, jnp.where(mid < hi_, 1.0, 0.0), 0.0)
        return jnp.max(unresolved), jnp.max(unresolved * movable)

    def search_cond(st):
        return jnp.logical_and(st[0] < max_iter, st[-1] > 0.5)

    def search_body(st):
        it, lo_, hi_, c_lo, c_hi, mid, _, _ = st
        cnt = count(lambda xj: xj >= mid)
        up = cnt >= kf
        lo2 = jnp.where(up, mid, lo_)
        hi2 = jnp.where(up, hi_, mid)
        c_lo2 = jnp.where(up, cnt, c_lo)
        c_hi2 = jnp.where(up, c_hi, cnt)
        mid2 = pivot(it + 1, lo2, hi2, c_lo2, c_hi2)
        open_, go = pending(lo2, hi2, mid2, c_lo2)
        return it + 1, lo2, hi2, c_lo2, c_hi2, mid2, open_, go

    c_lo0 = (t_pos + 1).astype(F32)
    c_hi0 = jnp.zeros(shape, F32)
    mid0 = pivot(jnp.int32(0), lo, hi, c_lo0, c_hi0)
    open0, go0 = pending(lo, hi, mid0, c_lo0)
    st = lax.while_loop(search_cond, search_body, (jnp.int32(0), lo, hi, c_lo0, c_hi0, mid0, open0, go0))
    lo, open_ = st[1], st[6]

    def resolve():
        def walk(thr_, strict):
            keep = (lambda xj: xj > thr_) if strict else (lambda xj: xj >= thr_)
            return reduce_rows(lambda xj: jnp.where(keep(xj), xj, inf), inf, jnp.minimum, jnp.min)

        def above_equal(thr_):
            return count(lambda xj: xj > thr_), count(lambda xj: xj == thr_)

        def too_low(n_gt):
            return jnp.where(few, 0.0, jnp.where(n_gt >= kf, 1.0, 0.0))

        thr0 = walk(lo, False)
        gt0, eq0 = above_equal(thr0)

        def walk_body(st):
            thr_, n_gt, _ = st
            thr2 = jnp.where(too_low(n_gt) > 0.5, walk(thr_, True), thr_)
            return (thr2,) + above_equal(thr2)

        thr_, n_gt, n_eq = lax.while_loop(lambda st: jnp.max(too_low(st[1])) > 0.5, walk_body, (thr0, gt0, eq0))
        need = kf - n_gt

        def idx_step(b, cur):
            trial = cur | (jnp.int32(1) << (idx_bits - 1 - b))

            def body(c, part):
                for j, xj in enumerate(cols(sc_ref[c])):
                    idx = (c * ck + j * 128) + lane
                    part = part + jnp.where(xj == thr_, jnp.where(idx < trial, 1.0, 0.0), 0.0)
                return part
            cnt = rep(jnp.sum(lax.fori_loop(0, n_ck, body, jnp.zeros(shape, F32)), axis=1, keepdims=True))
            return jnp.where(cnt < need, trial, cur)

        cut_ = lax.fori_loop(0, idx_bits, idx_step, jnp.zeros(shape, jnp.int32))
        return thr_, cut_

    thr, cut = lax.cond(open_ > 0.5, resolve, lambda: (lo, jnp.full(shape, 0x7FFFFFFF, jnp.int32)))

    def bias_chunk(c, _):
        x = sc_ref[c]
        for j, xj in enumerate(cols(x)):
            idx = (c * ck + j * 128) + lane
            at_thr = jnp.where(xj == thr, jnp.where(idx <= cut, 0.0, NEG_BIG), NEG_BIG)
            sc_ref[c, :, j * 128:(j + 1) * 128] = jnp.where(xj > thr, 0.0, at_thr)
        return 0

    lax.fori_loop(0, n_ck, bias_chunk, 0)

    for h in range(nheads):
        m_ref[h] = jnp.full(shape, NEG_BIG, F32)
        l_ref[h] = jnp.zeros(shape, F32)
        acc_ref[h] = jnp.zeros((tq, HEAD_DIM), F32)

    def attn_scores(c, _):
        kt = kt_ref[c]
        bias = cols(sc_ref[c])
        for h in range(nheads):
            s = cols(_dot(q_ref[h], kt))
            part = m_ref[h]
            for j in range(nj):
                sj = s[j] + bias[j]
                s_ref[h, c, :, j * 128:(j + 1) * 128] = sj
                part = jnp.maximum(part, sj)
            m_ref[h] = part
        return 0

    lax.fori_loop(0, n_ck, attn_scores, 0)
    for h in range(nheads):
        m_ref[h] = rep(jnp.max(m_ref[h], axis=1, keepdims=True))

    def attn_values(c, _):
        off = pl.multiple_of(c * ck, ck)
        vc = v_ref[pl.ds(off, ck), :]
        for h in range(nheads):
            m = m_ref[h]
            p = [jnp.exp(sj - m) for sj in cols(s_ref[h, c])]
            l_ref[h] += functools.reduce(jnp.add, p)
            acc_ref[h] += _dot(jnp.concatenate(p, axis=1).astype(BF16), vc)
        return 0

    lax.fori_loop(0, n_ck, attn_values, 0)
    for h in range(nheads):
        l_row = jnp.sum(l_ref[h], axis=1, keepdims=True)
        o_ref[:, h * HEAD_DIM:(h + 1) * HEAD_DIM] = (acc_ref[h] / l_row).astype(o_ref.dtype)


def _dsa_attention_rows(q_hm, qi_hm, wi, k, v, ki, bsz, seq):
    tq = 128
    ck = min(512, seq)
    n_ck = seq // ck
    k_sel = min(TOPK_MAX, seq // 4)
    nq = seq // tq
    width = ATTN_HEADS * HEAD_DIM
    kt = k.reshape(bsz, n_ck, ck, HEAD_DIM).transpose(0, 1, 3, 2)
    kit = ki.reshape(bsz, n_ck, ck, HEAD_DIM).transpose(0, 1, 3, 2)
    hm = lambda b, i: (b, 0, i, 0)
    per_batch = lambda b, i: (b, 0, 0, 0)
    nh = ATTN_HEADS
    return pl.pallas_call(
        functools.partial(_dsa_kernel_rows, tq=tq, ck=ck, k_sel=k_sel, idx_bits=(seq - 1).bit_length(), max_iter=48),
        grid=(bsz, nq),
        in_specs=[pl.BlockSpec((None, ATTN_HEADS, tq, HEAD_DIM), hm),
                  pl.BlockSpec((None, IDX_HEADS, tq, HEAD_DIM), hm),
                  pl.BlockSpec((tq, 128), lambda b, i: (b * nq + i, 0)),
                  pl.BlockSpec((None, n_ck, HEAD_DIM, ck), per_batch),
                  pl.BlockSpec((seq, HEAD_DIM), lambda b, i: (b, 0)),
                  pl.BlockSpec((None, n_ck, HEAD_DIM, ck), per_batch)],
        out_specs=pl.BlockSpec((tq, width), lambda b, i: (b * nq + i, 0)),
        out_shape=jax.ShapeDtypeStruct((bsz * seq, width), BF16),
        scratch_shapes=[pltpu.VMEM((n_ck, tq, ck), F32),
                        pltpu.VMEM((nh, n_ck, tq, ck), F32),
                        pltpu.VMEM((IDX_HEADS, tq, 128), F32),
                        pltpu.VMEM((nh, tq, 128), F32),
                        pltpu.VMEM((nh, tq, 128), F32),
                        pltpu.VMEM((nh, tq, HEAD_DIM), F32)],
        compiler_params=_cparams(("arbitrary", "arbitrary"), VMEM_LIMIT),
        name="dsa_attention",
    )(q_hm, qi_hm, wi, kt, v, kit)


def _outproj_kernel(x_ref, ys_ref, ya_ref, w_ref, mod_ref, o_ref, *, d):
    width = ys_ref.shape[1]
    y = _dot(ys_ref[...], w_ref[:width, :]) + _dot(ya_ref[...], w_ref[width:, :])
    o_ref[...] = x_ref[...] + mod_ref[...][:, 2 * d:] * y


def _out_projection(x2, y_ssm, y_att, w_out, mod, layer, seq):
    t, d = x2.shape
    tm = 512
    width = y_ssm.shape[1]
    tiles_per_seq = seq // tm
    row = lambda i: (i, 0)
    return pl.pallas_call(
        functools.partial(_outproj_kernel, d=d),
        grid=(t // tm,),
        in_specs=[pl.BlockSpec((tm, d), row),
                  pl.BlockSpec((tm, width), row),
                  pl.BlockSpec((tm, width), row),
                  pl.BlockSpec((2 * width, d), lambda i: (0, 0)),
                  pl.BlockSpec((None, None, 1, 3 * d), lambda i: (layer, i // tiles_per_seq, 0, 0))],
        out_specs=pl.BlockSpec((tm, d), row),
        out_shape=jax.ShapeDtypeStruct((t, d), F32),
        compiler_params=_cparams(("arbitrary",)),
        name="out_projection",
    )(x2, y_ssm, y_att, w_out.astype(BF16), mod)


def _pool_kernel(x_ref, halo_ref, g_ref, mod_ref, pw_ref, ps_ref, o_ref, h_ref, *, d, tiles_per_seq):
    i = pl.program_id(0)
    tm = x_ref.shape[0]
    mod = mod_ref[...]
    shift, scale, gate = mod[:, :d], mod[:, d:2 * d], mod[:, 2 * d:]
    x = x_ref[...]
    first = (i % tiles_per_seq) == 0
    h_ref[POOL_HALO:, :] = _norm_mod(x, g_ref[...], shift, scale)
    halo = _norm_mod(halo_ref[...], g_ref[...], shift, scale)
    h_ref[:POOL_HALO, :] = jnp.where(first, 0.0, halo)
    pos = (i % tiles_per_seq) * tm + lax.broadcasted_iota(jnp.int32, (tm, 1), 0)
    gw = d // len(POOL_WINDOWS)
    ys = []
    for gi, win in enumerate(POOL_WINDOWS):
        cs = slice(gi * gw, (gi + 1) * gw)
        cur = h_ref[POOL_HALO:, cs]
        tot = cur
        for k in range(1, win):
            tot = tot + h_ref[POOL_HALO - k:POOL_HALO - k + tm, cs]
        cnt = jnp.minimum(pos + 1, win).astype(F32)
        pooled = tot / cnt - cur
        ys.append(_dot(pooled.astype(BF16), pw_ref[gi]))
    y = jnp.concatenate(ys, axis=1) * ps_ref[...]
    o_ref[...] = x + gate * y


def _pool_layer(x2, g, mod, layer, pool_w, pool_scale, seq):
    t, d = x2.shape
    tm = 512
    tiles_per_seq = seq // tm
    gw = d // len(POOL_WINDOWS)
    hb = tm // POOL_HALO
    return pl.pallas_call(
        functools.partial(_pool_kernel, d=d, tiles_per_seq=tiles_per_seq),
        grid=(t // tm,),
        in_specs=[pl.BlockSpec((tm, d), lambda i: (i, 0)),
                  pl.BlockSpec((POOL_HALO, d), lambda i: (jnp.maximum(i * hb - 1, 0), 0)),
                  pl.BlockSpec((1, d), lambda i: (0, 0)),
                  pl.BlockSpec((None, None, 1, 3 * d), lambda i: (layer, i // tiles_per_seq, 0, 0)),
                  pl.BlockSpec((len(POOL_WINDOWS), gw, gw), lambda i: (0, 0, 0)),
                  pl.BlockSpec((1, d), lambda i: (0, 0))],
        out_specs=pl.BlockSpec((tm, d), lambda i: (i, 0)),
        out_shape=jax.ShapeDtypeStruct((t, d), F32),
        scratch_shapes=[pltpu.VMEM((tm + POOL_HALO, d), F32)],
        compiler_params=_cparams(("arbitrary",)),
        name="pool_mixer",
    )(x2, x2, g.reshape(1, d), mod, pool_w.astype(BF16), pool_scale.reshape(1, d))


def _first_max(vals, idx, big):
    m = jnp.max(vals, axis=0, keepdims=True)
    first = jnp.min(jnp.where(vals == m, idx, big), axis=0, keepdims=True)
    return m, first


def _pack_bf16_pairs(x):
    n = x.shape[1] // 2
    lo = pltpu.bitcast(x[:, :n].astype(BF16).astype(F32), jnp.int32)
    hi = pltpu.bitcast(x[:, n:].astype(BF16).astype(F32), jnp.int32)
    return hi | ((lo >> 16) & 0xFFFF)


def _unpack_bf16_pairs(w):
    lo = pltpu.bitcast(w << 16, F32)
    hi = pltpu.bitcast(w & jnp.int32(-65536), F32)
    return jnp.concatenate([lo, hi], axis=1)


def _router_kernel(x_ref, g_ref, mod_ref, rw_ref, rb_ref, tri_ref, h_ref, gate_ref, rank_ref, cnt_ref,
                   run_ref, *, d, n_exp):
    mod = mod_ref[...]
    h = _norm_mod(x_ref[...], g_ref[...], mod[:, :d], mod[:, d:2 * d])
    h_ref[...] = _pack_bf16_pairs(h)
    tm = h.shape[0]
    hh, hl = _split_bf16(h)
    rw = rw_ref[...]
    rh, rl = _split_bf16(rw)
    logits = _dot_nt(rh, hh) + (_dot_nt(rh, hl) + _dot_nt(rl, hh))
    scores = jax.nn.sigmoid(logits)
    sel = scores + rb_ref[...]
    gsz = n_exp // N_EXPERT_GROUPS
    neg = -jnp.inf
    shape3 = (N_EXPERT_GROUPS, gsz, tm)
    sel3 = sel.reshape(shape3)
    sub = lax.broadcasted_iota(jnp.int32, shape3, 1)
    m1 = jnp.max(sel3, axis=1, keepdims=True)
    f1 = jnp.min(jnp.where(sel3 == m1, sub, gsz), axis=1, keepdims=True)
    m2 = jnp.max(jnp.where(sub == f1, neg, sel3), axis=1, keepdims=True)
    gscore = jnp.broadcast_to(m1 + m2, shape3).reshape(n_exp, tm)
    eidx = lax.broadcasted_iota(jnp.int32, (n_exp, tm), 0)
    gidx = eidx // gsz
    keep = jnp.zeros((n_exp, tm), F32)
    work = gscore
    for _ in range(TOPK_GROUPS):
        _, first = _first_max(work, gidx, N_EXPERT_GROUPS)
        hit = gidx == first
        keep = jnp.where(hit, 1.0, keep)
        work = jnp.where(hit, neg, work)
    work = jnp.where(keep > 0.0, sel, neg)
    chosen = jnp.zeros((n_exp, tm), F32)
    for _ in range(TOP_K):
        _, first = _first_max(work, eidx, n_exp)
        hit = eidx == first
        chosen = jnp.where(hit, 1.0, chosen)
        work = jnp.where(hit, neg, work)
    picked = chosen * scores
    gate_ref[...] = picked / jnp.sum(picked, axis=0, keepdims=True) * ROUTED_SCALE

    @pl.when(pl.program_id(0) == 0)
    def _():
        run_ref[...] = jnp.zeros(run_ref.shape, F32)

    before = _dot(chosen.astype(BF16), tri_ref[...])
    run = run_ref[...]
    rank_ref[...] = jnp.where(chosen > 0.0, before + run[:, :1], -1.0)
    run = run + jnp.broadcast_to(jnp.sum(chosen, axis=1, keepdims=True), run.shape)
    run_ref[...] = run
    cnt_ref[...] = run


def _ffn_router(x2, g, mod, layer, router_w, router_bias, seq):
    t, d = x2.shape
    n_exp = router_w.shape[1]
    tm = 512
    tiles_per_seq = seq // tm
    tri = (np.arange(tm)[:, None] < np.arange(tm)[None, :]).astype(np.float32)
    return pl.pallas_call(
        functools.partial(_router_kernel, d=d, n_exp=n_exp),
        grid=(t // tm,),
        in_specs=[pl.BlockSpec((tm, d), lambda i: (i, 0)),
                  pl.BlockSpec((1, d), lambda i: (0, 0)),
                  pl.BlockSpec((None, None, 1, 3 * d), lambda i: (layer, i // tiles_per_seq, 0, 0)),
                  pl.BlockSpec((n_exp, d), lambda i: (0, 0)),
                  pl.BlockSpec((n_exp, 1), lambda i: (0, 0)),
                  pl.BlockSpec((tm, tm), lambda i: (0, 0))],
        out_specs=[pl.BlockSpec((tm, d // 2), lambda i: (i, 0)),
                   pl.BlockSpec((n_exp, tm), lambda i: (0, i)),
                   pl.BlockSpec((n_exp, tm), lambda i: (0, i)),
                   pl.BlockSpec((n_exp, 128), lambda i: (0, 0))],
        out_shape=[jax.ShapeDtypeStruct((t, d // 2), jnp.int32),
                   jax.ShapeDtypeStruct((n_exp, t), F32),
                   jax.ShapeDtypeStruct((n_exp, t), F32),
                   jax.ShapeDtypeStruct((n_exp, 128), F32)],
        scratch_shapes=[pltpu.VMEM((n_exp, 128), F32)],
        compiler_params=_cparams(("arbitrary",)),
        name="ffn_router",
    )(x2, g.reshape(1, d), mod, router_w.T, router_bias.reshape(n_exp, 1), jnp.asarray(tri, BF16))


def _assign_kernel(rank_ref, gate_ref, start_ref, pos_ref, w_ref, *, n_exp):
    rank = rank_ref[...]
    gates = gate_ref[...]
    tm = rank.shape[1]
    slot = rank + start_ref[...]
    eidx = lax.broadcasted_iota(jnp.int32, (n_exp, tm), 0).astype(F32)
    alive = jnp.where(rank >= 0.0, eidx, float(n_exp))
    kidx = lax.broadcasted_iota(jnp.int32, (TOP_K, tm), 0)
    pos = jnp.zeros((TOP_K, tm), F32)
    wts = jnp.zeros((TOP_K, tm), F32)
    for k in range(TOP_K):
        first = jnp.min(alive, axis=0, keepdims=True)
        hit = alive == first
        pos_k = jnp.sum(jnp.where(hit, slot, 0.0), axis=0, keepdims=True)
        w_k = jnp.sum(jnp.where(hit, gates, 0.0), axis=0, keepdims=True)
        pos = jnp.where(kidx == k, pos_k, pos)
        wts = jnp.where(kidx == k, w_k, wts)
        alive = jnp.where(hit, float(n_exp), alive)
    pos_ref[...] = pos.astype(jnp.int32)
    w_ref[...] = wts


def _assign_slots(ranks, gates, start):
    n_exp, t = ranks.shape
    tm = 512
    return pl.pallas_call(
        functools.partial(_assign_kernel, n_exp=n_exp),
        grid=(t // tm,),
        in_specs=[pl.BlockSpec((n_exp, tm), lambda i: (0, i)),
                  pl.BlockSpec((n_exp, tm), lambda i: (0, i)),
                  pl.BlockSpec((n_exp, 1), lambda i: (0, 0))],
        out_specs=[pl.BlockSpec((TOP_K, tm), lambda i: (0, i)),
                   pl.BlockSpec((TOP_K, tm), lambda i: (0, i))],
        out_shape=[jax.ShapeDtypeStruct((TOP_K, t), jnp.int32), jax.ShapeDtypeStruct((TOP_K, t), F32)],
        compiler_params=_cparams(("arbitrary",)),
        name="moe_assign",
    )(ranks, gates, start)


SC_CORES = 2
SC_SUBCORES = 16
SC_WINDOW = 128


def _sc_mesh():
    return plsc.VectorSubcoreMesh(core_axis_name="c", subcore_axis_name="s",
                                  num_cores=SC_CORES, num_subcores=SC_SUBCORES)


def _sc_scatter_rows(rows, pos, n_out):
    t, width = rows.shape
    nk = pos.shape[0]
    win = SC_WINDOW
    n_win = t // win // (SC_CORES * SC_SUBCORES)
    pos_w = pos.reshape(nk, t // win, win).transpose(1, 0, 2)

    def body(rows_hbm, pos_hbm, out_hbm, idx_v, buf_v, sem):
        wid = lax.axis_index("s") * SC_CORES + lax.axis_index("c")

        @pl.loop(0, n_win)
        def _(j):
            w = wid * n_win + j
            pltpu.sync_copy(rows_hbm.at[pl.ds(w * win, win)], buf_v)
            pltpu.sync_copy(pos_hbm.at[w], idx_v)
            copies = [pltpu.make_async_copy(buf_v, out_hbm.at[idx_v.at[k]], sem) for k in range(nk)]
            for cp in copies:
                cp.start()
            for cp in copies:
                cp.wait()

    return pl.kernel(
        body, out_type=jax.ShapeDtypeStruct((n_out, width), jnp.int32), mesh=_sc_mesh(),
        scratch_types=[pltpu.VMEM((nk, win), jnp.int32), pltpu.VMEM((win, width), jnp.int32),
                       pltpu.SemaphoreType.DMA],
        name="sc_scatter_rows",
    )(rows, pos_w)


SC_GATHER_WINDOW = 64


def _sc_gather_rows(table, pos_flat):
    m = pos_flat.shape[0]
    width = table.shape[1]
    win = SC_GATHER_WINDOW
    workers = SC_CORES * SC_SUBCORES
    n_win = m // win // workers
    pos_w = pos_flat.reshape(workers, n_win, win)

    def body(table_hbm, pos_hbm, out_hbm, idx_v, buf_v, sem_g, sem_w):
        wid = lax.axis_index("s") * SC_CORES + lax.axis_index("c")
        base = wid * (n_win * win)
        pltpu.sync_copy(pos_hbm.at[wid], idx_v)

        def gather(j, b):
            return pltpu.make_async_copy(table_hbm.at[idx_v.at[j]], buf_v.at[b], sem_g.at[b])

        def write(j, b):
            return pltpu.make_async_copy(buf_v.at[b], out_hbm.at[pl.ds(base + j * win, win)], sem_w.at[b])

        gather(0, 0).start()

        @pl.loop(0, n_win, step=2)
        def _(j0):
            for b in range(2):
                j = j0 + b
                gather(j, b).wait()
                write(j, b).start()

                @pl.when(j + 1 < n_win)
                def _():
                    @pl.when(j >= 1)
                    def _():
                        write(j - 1, 1 - b).wait()
                    gather(j + 1, 1 - b).start()

        write(n_win - 2, 0).wait()
        write(n_win - 1, 1).wait()

    return pl.kernel(
        body, out_type=jax.ShapeDtypeStruct((m, width), jnp.int32), mesh=_sc_mesh(),
        scratch_types=[pltpu.VMEM((n_win, win), jnp.int32), pltpu.VMEM((2, win, width), jnp.int32),
                       pltpu.SemaphoreType.DMA((2,)), pltpu.SemaphoreType.DMA((2,))],
        name="sc_gather_rows",
    )(table, pos_w)


MOE_BLOCK = 512


def _silu_mul(a, b):
    return (a * jax.nn.sigmoid(a)) * b


def _expert_ffn_kernel(be_ref, nv_ref, xs_ref, wg_ref, wu_ref, wd_ref, ys_ref, wg_bf, wu_bf, wd_bf):
    i = pl.program_id(0)
    fresh = jnp.logical_or(i == 0, be_ref[i] != be_ref[jnp.maximum(i - 1, 0)])

    @pl.when(fresh)
    def _():
        wg_bf[...] = wg_ref[...].astype(BF16)
        wu_bf[...] = wu_ref[...].astype(BF16)
        wd_bf[...] = wd_ref[...].astype(BF16)

    nv = nv_ref[i]

    @pl.when(nv > 0)
    def _():
        x = _unpack_bf16_pairs(xs_ref[...])
        row = lax.broadcasted_iota(jnp.int32, (x.shape[0], 1), 0)
        x = jnp.where(row < nv, x, 0.0).astype(BF16)
        mid = _silu_mul(_dot(x, wg_bf[...]), _dot(x, wu_bf[...]))
        ys_ref[...] = _pack_bf16_pairs(_dot(mid.astype(BF16), wd_bf[...]))

    @pl.when(nv <= 0)
    def _():
        ys_ref[...] = jnp.zeros(ys_ref.shape, ys_ref.dtype)


def _expert_ffn(xs, block_expert, block_rows, w_gate, w_up, w_down):
    n_rows, half = xs.shape
    n_exp, d, de = w_gate.shape
    nb = n_rows // MOE_BLOCK
    wmap = lambda i, be, nv: (be[i], 0, 0)
    grid_spec = pltpu.PrefetchScalarGridSpec(
        num_scalar_prefetch=2,
        grid=(nb,),
        in_specs=[pl.BlockSpec((MOE_BLOCK, half), lambda i, be, nv: (i, 0)),
                  pl.BlockSpec((None, d, de), wmap),
                  pl.BlockSpec((None, d, de), wmap),
                  pl.BlockSpec((None, de, d), wmap)],
        out_specs=pl.BlockSpec((MOE_BLOCK, half), lambda i, be, nv: (i, 0)),
        scratch_shapes=[pltpu.VMEM((d, de), BF16), pltpu.VMEM((d, de), BF16), pltpu.VMEM((de, d), BF16)],
    )
    return pl.pallas_call(
        _expert_ffn_kernel,
        grid_spec=grid_spec,
        out_shape=jax.ShapeDtypeStruct((n_rows, half), jnp.int32),
        compiler_params=_cparams(("arbitrary",)),
        name="moe_expert_ffn",
    )(block_expert, block_rows, xs, w_gate, w_up, w_down)


def _combine_kernel(yt_ref, wt_ref, hp_ref, x_ref, mod_ref, sg_ref, su_ref, sd_ref, o_ref, *, d):
    h = _unpack_bf16_pairs(hp_ref[...]).astype(BF16)
    acc = _dot(_silu_mul(_dot(h, sg_ref[...]), _dot(h, su_ref[...])).astype(BF16), sd_ref[...])
    w = wt_ref[...]
    for k in range(TOP_K):
        acc = acc + w[:, k:k + 1] * _unpack_bf16_pairs(yt_ref[k])
    o_ref[...] = x_ref[...] + mod_ref[...][:, 2 * d:] * acc


def _combine(y_tok, w_tok, h_packed, x2, mod, layer, sh_gate, sh_up, sh_down, seq):
    t, d = x2.shape
    ds = sh_gate.shape[1]
    tm = 512
    tiles_per_seq = seq // tm
    row = lambda i: (i, 0)
    const = lambda i: (0, 0)
    return pl.pallas_call(
        functools.partial(_combine_kernel, d=d),
        grid=(t // tm,),
        in_specs=[pl.BlockSpec((TOP_K, tm, d // 2), lambda i: (0, i, 0)),
                  pl.BlockSpec((tm, TOP_K), row),
                  pl.BlockSpec((tm, d // 2), row),
                  pl.BlockSpec((tm, d), row),
                  pl.BlockSpec((None, None, 1, 3 * d), lambda i: (layer, i // tiles_per_seq, 0, 0)),
                  pl.BlockSpec((d, ds), const),
                  pl.BlockSpec((d, ds), const),
                  pl.BlockSpec((ds, d), const)],
        out_specs=pl.BlockSpec((tm, d), row),
        out_shape=jax.ShapeDtypeStruct((t, d), F32),
        compiler_params=_cparams(("arbitrary",), VMEM_LIMIT),
        name="moe_combine",
    )(y_tok, w_tok, h_packed, x2, mod, sh_gate.astype(BF16), sh_up.astype(BF16), sh_down.astype(BF16))


def _moe_layer(x2, h_packed, gates, ranks, counts, mod, layer, w_gate, w_up, w_down, sh_gate, sh_up, sh_down, seq):
    t, d = x2.shape
    n_exp = w_gate.shape[0]
    cnt = counts[:, 0].astype(jnp.int32)
    padded = (cnt + MOE_BLOCK - 1) // MOE_BLOCK * MOE_BLOCK
    seg_end = jnp.cumsum(padded)
    seg_start = seg_end - padded
    nb = t * TOP_K // MOE_BLOCK + n_exp
    first_row = jnp.arange(nb, dtype=jnp.int32) * MOE_BLOCK
    block_expert = jnp.sum((seg_end[None, :] <= first_row[:, None]).astype(jnp.int32), axis=1)
    block_expert = jnp.minimum(block_expert, n_exp - 1)
    block_rows = jnp.clip(seg_start[block_expert] + cnt[block_expert] - first_row, 0, MOE_BLOCK).astype(jnp.int32)
    pos, w = _assign_slots(ranks, gates, seg_start.astype(F32).reshape(n_exp, 1))
    xs = _sc_scatter_rows(h_packed, pos, nb * MOE_BLOCK)
    ys = _expert_ffn(xs, block_expert, block_rows, w_gate, w_up, w_down)
    y_tok = _sc_gather_rows(ys, pos.reshape(TOP_K * t)).reshape(TOP_K, t, d // 2)
    return _combine(y_tok, w.T, h_packed, x2, mod, layer, sh_gate, sh_up, sh_down, seq)


def kernel(x, c, positions, mix_norm_g, mix_mod_w, mix_mod_b, ffn_norm_g, ffn_mod_w, ffn_mod_b,
           hyb_w_in, hyb_w_out, s5_lambda_re, s5_lambda_im, s5_log_dt, s5_b_re, s5_b_im,
           s5_c_re, s5_c_im, s5_d, s5_glu_w, s5_glu_b, attn_q_norm_g, attn_k_norm_g,
           pool_w, pool_scale, router_w, router_bias, exp_w_gate, exp_w_up, exp_w_down,
           sh_w_gate, sh_w_up, sh_w_down):
    bsz, seq, d = x.shape
    t = bsz * seq
    depth = mix_norm_g.shape[0]
    x2 = x.reshape(t, d)
    mix_mod = _mod_vectors(c, mix_mod_w, mix_mod_b).reshape(depth, bsz, 1, 3 * d)
    ffn_mod = _mod_vectors(c, ffn_mod_w, ffn_mod_b).reshape(depth, bsz, 1, 3 * d)
    for i in range(depth):
        j = i // 2
        if i % 2 == 0:
            u, q_hm, qi_hm, k, v, ki, wi = _in_projection(
                x2, mix_norm_g[i], mix_mod, i, hyb_w_in[j], positions, attn_q_norm_g[j], attn_k_norm_g[j], bsz, seq)
            nchunk = seq // S5_CHUNK
            groups = u.shape[1] // S5_GROUP
            u2 = u.reshape(bsz * nchunk, S5_CHUNK, groups, S5_GROUP).transpose(0, 2, 1, 3)
            u2 = u2.reshape(bsz * nchunk, groups * S5_CHUNK * S5_GROUP)
            weights = _s5_weights(s5_lambda_re[j], s5_lambda_im[j], s5_log_dt[j], s5_b_re[j], s5_b_im[j],
                                  s5_c_re[j], s5_c_im[j], s5_d[j], s5_glu_w[j], s5_glu_b[j])
            y2 = _s5_mixer(u2, weights, bsz, seq)
            y_ssm = y2.reshape(bsz * nchunk, groups, S5_CHUNK, S5_GROUP).transpose(0, 2, 1, 3).reshape(t, -1)
            y_att = _dsa_attention(q_hm, qi_hm, wi, k, v, ki, bsz, seq)
            x2 = _out_projection(x2, y_ssm, y_att, hyb_w_out[j], mix_mod, i, seq)
        else:
            x2 = _pool_layer(x2, mix_norm_g[i], mix_mod, i, pool_w[j], pool_scale[j], seq)
        h_packed, gates, ranks, counts = _ffn_router(x2, ffn_norm_g[i], ffn_mod, i, router_w[i], router_bias[i], seq)
        x2 = _moe_layer(x2, h_packed, gates, ranks, counts, ffn_mod, i, exp_w_gate[i], exp_w_up[i], exp_w_down[i],
                        sh_w_gate[i], sh_w_up[i], sh_w_down[i], seq)
    return x2.reshape(bsz, seq, d)
```

```python
import functools
import math

import numpy as np
import jax
import jax.numpy as jnp
from jax import lax
from jax.experimental import pallas as pl
from jax.experimental.pallas import tpu as pltpu
from jax.experimental.pallas import tpu_sc as plsc

F32 = jnp.float32
BF16 = jnp.bfloat16

EPS = 1e-6
S5_GROUP = 16
S5_STATE = 64
HEAD_DIM = 64
ATTN_HEADS = 8
IDX_HEADS = 8
ROPE_HALF = 8
ROPE_THETA = 500000.0
TOPK_MAX = 256
POOL_WINDOWS = (2, 4, 8, 16)
POOL_HALO = 16
N_EXPERT_GROUPS = 8
TOPK_GROUPS = 4
TOP_K = 8
ROUTED_SCALE = 2.5
S5_CHUNK = 16
NEG_BIG = -1e30
INT_MIN = -2147483648
VMEM_LIMIT = 56 * 1024 * 1024


def _cparams(sem, vmem=None):
    return pltpu.CompilerParams(dimension_semantics=sem, vmem_limit_bytes=vmem)


def _dot(a, b):
    return jnp.dot(a, b, preferred_element_type=F32)


def _dot_nt(a, b):
    return lax.dot_general(a, b, (((1,), (1,)), ((), ())), preferred_element_type=F32)


def _split_bf16(a):
    hi = a.astype(BF16)
    lo = (a - hi.astype(F32)).astype(BF16)
    return hi, lo


def _dot3(a, b):
    ah, al = _split_bf16(a)
    bh, bl = _split_bf16(b)
    return _dot(ah, bh) + (_dot(ah, bl) + _dot(al, bh))


def _norm_mod(x, g, shift, scale):
    y = x * lax.rsqrt(jnp.mean(x * x, axis=-1, keepdims=True) + EPS)
    return (y * g) * (1.0 + scale) + shift


def _mod_kernel(ct_ref, w_ref, b_ref, o_ref):
    ct = ct_ref[...]
    cs = ct * jax.nn.sigmoid(ct)
    w = w_ref[...]
    rows = [jnp.sum(w * cs[:, b:b + 1], axis=0, keepdims=True) for b in range(ct.shape[1])]
    o_ref[...] = jnp.concatenate(rows, axis=0) + b_ref[...]


def _mod_vectors(c, w, b):
    nl, d, n3 = w.shape
    bsz = c.shape[0]
    tn = 512
    return pl.pallas_call(
        _mod_kernel,
        grid=(nl, n3 // tn),
        in_specs=[pl.BlockSpec((d, bsz), lambda l, j: (0, 0)),
                  pl.BlockSpec((None, d, tn), lambda l, j: (l, 0, j)),
                  pl.BlockSpec((None, 1, tn), lambda l, j: (l, 0, j))],
        out_specs=pl.BlockSpec((None, bsz, tn), lambda l, j: (l, 0, j)),
        out_shape=jax.ShapeDtypeStruct((nl, bsz, n3), F32),
        compiler_params=_cparams(("arbitrary", "arbitrary")),
        name="mod_vectors",
    )(c.T, w, b.reshape(nl, 1, n3))


def _inproj_kernel(x_ref, g_ref, mod_ref, w_ref, pos_ref, inv_ref, qg_ref, kg_ref,
                   u_ref, q_ref, qi_ref, k_ref, v_ref, ki_ref, wi_ref, u_scr, *, d):
    x = x_ref[...]
    mod = mod_ref[...]
    h = _norm_mod(x, g_ref[...], mod[:, :d], mod[:, d:2 * d])
    proj = _dot(h.astype(BF16), w_ref[...])
    tm = x.shape[0]
    width = ATTN_HEADS * HEAD_DIM

    lane = lax.broadcasted_iota(jnp.int32, (1, 128), 1) & (HEAD_DIM - 1)
    ang = pos_ref[...].astype(F32) * inv_ref[...]
    cos, sin = jnp.cos(ang), jnp.sin(ang)
    t_a = jnp.where(lane < 2 * ROPE_HALF, cos, 1.0)
    t_b = jnp.where(lane < ROPE_HALF, -sin, 0.0)
    t_c = jnp.where((lane >= ROPE_HALF) & (lane < 2 * ROPE_HALF), sin, 0.0)

    def rope(a):
        return a * t_a + pltpu.roll(a, 128 - ROPE_HALF, 1) * t_b + pltpu.roll(a, ROPE_HALF, 1) * t_c

    for slab in range(width // 128):
        u_scr[slab] = proj[:, slab * 128:(slab + 1) * 128]
        for j in range(S5_CHUNK):
            piece = u_scr[slab, pl.ds(j, tm // S5_CHUNK, stride=S5_CHUNK), :]
            col = (slab * S5_CHUNK + j) * 128
            u_ref[:, col:col + 128] = piece.astype(u_ref.dtype)
    q = proj[:, width:2 * width]
    qi = proj[:, 2 * width:3 * width]
    small = proj[:, 3 * width:3 * width + 256]
    qsq = q * q
    qg = qg_ref[...]
    lane128 = lax.broadcasted_iota(jnp.int32, (1, 128), 1)
    for pair in range(ATTN_HEADS // 2):
        sl = slice(pair * 128, (pair + 1) * 128)
        r = []
        for half in range(2):
            hh = 2 * pair + half
            ss = jnp.sum(qsq[:, hh * HEAD_DIM:(hh + 1) * HEAD_DIM], axis=-1, keepdims=True)
            r.append(lax.rsqrt(ss * (1.0 / HEAD_DIM) + EPS) * (HEAD_DIM ** -0.5))
        qn = rope(q[:, sl] * qg) * jnp.where(lane128 < HEAD_DIM, r[0], r[1])
        qt = qn.T
        qit = rope(qi[:, sl]).T
        for half in range(2):
            hs = slice(half * HEAD_DIM, (half + 1) * HEAD_DIM)
            q_ref[2 * pair + half] = qt[hs].astype(q_ref.dtype)
            qi_ref[2 * pair + half] = qit[hs].astype(qi_ref.dtype)
    kv = small[:, :128]
    k = kv[:, :HEAD_DIM]
    rk = lax.rsqrt(jnp.mean(k * k, axis=-1, keepdims=True) + EPS)
    kr = rope(kv * kg_ref[...])
    k_ref[...] = (kr[:, :HEAD_DIM] * rk).astype(k_ref.dtype)
    v_ref[...] = kv[:, HEAD_DIM:].astype(v_ref.dtype)
    kiw = small[:, 128:256]
    ki_ref[...] = rope(kiw)[:, :HEAD_DIM].astype(ki_ref.dtype)
    wscale = (IDX_HEADS ** -0.5) * (HEAD_DIM ** -0.5)
    wi_ref[...] = pltpu.roll(kiw, HEAD_DIM, 1) * wscale


def _in_projection(x2, g, mod, layer, w_in, positions, q_norm_g, k_norm_g, bsz, seq):
    t, d = x2.shape
    tm = 512
    width = ATTN_HEADS * HEAD_DIM
    cuts = np.cumsum([width, width, HEAD_DIM, HEAD_DIM, width, HEAD_DIM, IDX_HEADS])
    u_w, q_w, k_w, v_w, qi_w, ki_w, wi_w = jnp.split(w_in, cuts[:-1].tolist(), axis=1)
    pad = jnp.zeros((d, 128 - HEAD_DIM - IDX_HEADS), w_in.dtype)
    w = jnp.concatenate([u_w, q_w, qi_w, k_w, v_w, ki_w, wi_w, pad], axis=1).astype(BF16)
    nw = w.shape[1]
    lane = np.arange(128) % HEAD_DIM
    inv = np.where(lane < 2 * ROPE_HALF,
                   np.power(ROPE_THETA, -2.0 * (lane % ROPE_HALF) / (2 * ROPE_HALF)), 0.0)
    inv = jnp.asarray(inv.reshape(1, 128), F32)
    qg2 = jnp.tile(q_norm_g.reshape(1, HEAD_DIM), (1, 2))
    kg2 = jnp.concatenate([k_norm_g.reshape(1, HEAD_DIM), jnp.ones((1, HEAD_DIM), F32)], axis=1)
    tiles_per_seq = seq // tm
    row = lambda i: (i, 0)
    const = lambda i: (0, 0)
    hm = lambda i: (i // tiles_per_seq, 0, 0, i % tiles_per_seq)
    outs = pl.pallas_call(
        functools.partial(_inproj_kernel, d=d),
        grid=(t // tm,),
        in_specs=[pl.BlockSpec((tm, d), row),
                  pl.BlockSpec((1, d), const),
                  pl.BlockSpec((None, None, 1, 3 * d), lambda i: (layer, i // tiles_per_seq, 0, 0)),
                  pl.BlockSpec((d, nw), const),
                  pl.BlockSpec((tm, 1), row),
                  pl.BlockSpec((1, 128), const),
                  pl.BlockSpec((1, 128), const),
                  pl.BlockSpec((1, 128), const)],
        out_specs=[pl.BlockSpec((tm // S5_CHUNK, width * S5_CHUNK), row),
                   pl.BlockSpec((None, ATTN_HEADS, HEAD_DIM, tm), hm),
                   pl.BlockSpec((None, IDX_HEADS, HEAD_DIM, tm), hm),
                   pl.BlockSpec((tm, HEAD_DIM), row),
                   pl.BlockSpec((tm, HEAD_DIM), row),
                   pl.BlockSpec((tm, HEAD_DIM), row),
                   pl.BlockSpec((tm, 128), row)],
        out_shape=[jax.ShapeDtypeStruct((t // S5_CHUNK, width * S5_CHUNK), BF16),
                   jax.ShapeDtypeStruct((bsz, ATTN_HEADS, HEAD_DIM, seq), BF16),
                   jax.ShapeDtypeStruct((bsz, IDX_HEADS, HEAD_DIM, seq), BF16),
                   jax.ShapeDtypeStruct((t, HEAD_DIM), BF16),
                   jax.ShapeDtypeStruct((t, HEAD_DIM), BF16),
                   jax.ShapeDtypeStruct((t, HEAD_DIM), BF16),
                   jax.ShapeDtypeStruct((t, 128), F32)],
        scratch_shapes=[pltpu.VMEM((width // 128, tm, 128), F32)],
        compiler_params=_cparams(("arbitrary",), VMEM_LIMIT),
        name="in_projection",
    )(x2, g.reshape(1, d), mod, w, positions.reshape(t, 1), inv, qg2, kg2)
    return outs


def _s5_weights(lam_re, lam_im, log_dt, b_re, b_im, c_re, c_im, d_skip, glu_w, glu_b):
    L = S5_CHUNK
    g, p = lam_re.shape
    hch = S5_GROUP
    dt = jnp.exp(log_dt)[:, None]
    lr, li = lam_re, lam_im
    tau = jnp.arange(L + 1, dtype=F32)[:, None, None]
    mag = jnp.exp(lr * dt * tau)
    pw_r, pw_i = mag * jnp.cos(li * dt * tau), mag * jnp.sin(li * dt * tau)
    ar, ai = pw_r[1], pw_i[1]
    nr, ni = ar - 1.0, ai
    den = lr * lr + li * li
    cr, ci = (nr * lr + ni * li) / den, (ni * lr - nr * li) / den
    bb_r = cr[..., None] * b_re - ci[..., None] * b_im
    bb_i = cr[..., None] * b_im + ci[..., None] * b_re
    cl_r = c_re[None] * pw_r[:L, :, None, :] - c_im[None] * pw_i[:L, :, None, :]
    cl_i = c_re[None] * pw_i[:L, :, None, :] + c_im[None] * pw_r[:L, :, None, :]
    taps = (jnp.einsum('tghp,gpk->tghk', cl_r, bb_r) - jnp.einsum('tghp,gpk->tghk', cl_i, bb_i))
    ns = g // 8
    eye = jnp.eye(8, dtype=F32)
    kd = jnp.einsum('tsgoh,gf->tsghfo', taps.reshape(L, ns, 8, hch, hch), eye).reshape(L, ns, 128, 128)
    zero = jnp.zeros_like(kd[0])
    k2 = jnp.stack([jnp.concatenate([jnp.concatenate([kd[2 * dd], kd[2 * dd + 1]], axis=-1),
                                     jnp.concatenate([kd[2 * dd - 1] if dd else zero, kd[2 * dd]], axis=-1)], axis=-2)
                    for dd in range(L // 2)], axis=1).astype(BF16)
    ii = jnp.arange(L)
    rev_r, rev_i = pw_r[L - 1 - ii], pw_i[L - 1 - ii]
    z_r = rev_r[..., None] * bb_r[None] - rev_i[..., None] * bb_i[None]
    z_i = rev_r[..., None] * bb_i[None] + rev_i[..., None] * bb_r[None]
    zc = jnp.concatenate([z_r, z_i], axis=2)
    w_z = jnp.einsum('jsgph,gf->sjghfp', zc.reshape(L, ns, 8, 2 * p, hch), eye)
    w_z = w_z.reshape(ns, L // 2, 2 * 128, 8 * 2 * p).astype(BF16)
    co_r = c_re[None] * pw_r[1:, :, None, :] - c_im[None] * pw_i[1:, :, None, :]
    co_i = c_re[None] * pw_i[1:, :, None, :] + c_im[None] * pw_r[1:, :, None, :]
    cc = jnp.concatenate([co_r, -co_i], axis=-1)
    w_c = jnp.einsum('isghp,gf->sgpifh', cc.reshape(L, ns, 8, hch, 2 * p), eye).reshape(ns, 8 * 2 * p, L * 128)
    w_ch = w_c.astype(BF16)
    w_cl = (w_c - w_ch.astype(F32)).astype(BF16)
    al_r, al_i = pw_r[L], pw_i[L]
    dec_a = jnp.concatenate([al_r, al_r], axis=-1).reshape(ns, 1, 8 * 2 * p)
    dec_b = jnp.concatenate([-al_i, al_i], axis=-1).reshape(ns, 1, 8 * 2 * p)
    d_t = jnp.tile(d_skip.reshape(ns, 1, 128), (1, 1, L))
    gl = jnp.einsum('sghk,gf->sghfk', glu_w.reshape(ns, 8, hch, hch), eye).reshape(ns, 128, 128)
    glu2 = jnp.einsum('ab,shk->sahbk', jnp.eye(2, dtype=F32), gl).reshape(ns, 256, 256).astype(BF16)
    glu_bt = jnp.tile(glu_b.reshape(ns, 1, 128), (1, 1, L))
    return k2, w_z, w_ch, w_cl, dec_a, dec_b, d_t, glu2, glu_bt


def _s5_kernel(u_ref, k2_ref, wz_ref, wch_ref, wcl_ref, da_ref, db_ref, d_ref, g2_ref, gb_ref, o_ref,
               zp_ref, zq_ref, sp_ref, *, nchunk):
    nblk = S5_CHUNK // 2
    ub = [u_ref[:, j * 256:(j + 1) * 256] for j in range(nblk)]
    z = _dot(ub[0], wz_ref[0])
    for j in range(1, nblk):
        z = z + _dot(ub[j], wz_ref[j])
    zp_ref[...] = z
    half = S5_STATE
    zq_ref[...] = jnp.concatenate([pltpu.roll(z[:, s * 2 * half:(s + 1) * 2 * half], half, 1)
                                   for s in range(z.shape[1] // (2 * half))], axis=1)
    da = da_ref[...]
    db = db_ref[...]
    dbq = -db

    def step(c, carry):
        sp, sq = carry
        sp_ref[pl.ds(c, 1), :] = sp
        sp_new = da * sp + db * sq + zp_ref[pl.ds(c, 1), :]
        sq_new = da * sq + dbq * sp + zq_ref[pl.ds(c, 1), :]
        return sp_new, sq_new

    zero = jnp.zeros((1, z.shape[1]), F32)
    lax.fori_loop(0, nchunk, step, (zero, zero))
    sh, sl = _split_bf16(sp_ref[...])
    wch = wch_ref[...]
    y_inter = _dot(sh, wch) + (_dot(sh, wcl_ref[...]) + _dot(sl, wch))
    for i in range(nblk):
        cs = slice(i * 256, (i + 1) * 256)
        acc = y_inter[:, cs] + d_ref[:, cs] * ub[i].astype(F32)
        for j in range(i + 1):
            acc = acc + _dot(ub[j], k2_ref[i - j])
        y = jax.nn.gelu(acc)
        y = y * jax.nn.sigmoid(_dot(y.astype(BF16), g2_ref[...]) + gb_ref[:, cs])
        for a in range(2):
            o_ref[pl.ds(2 * i + a, nchunk, stride=S5_CHUNK), :] = y[:, a * 128:(a + 1) * 128]


def _s5_mixer(u2, weights, bsz, seq):
    k2, w_z, w_ch, w_cl, dec_a, dec_b, d_t, glu2, glu_bt = weights
    ns = k2.shape[0]
    nchunk = seq // S5_CHUNK
    cols = S5_CHUNK * 128
    st = w_z.shape[-1]
    slab3 = lambda b, s: (s, 0, 0)
    slab4 = lambda b, s: (s, 0, 0, 0)
    return pl.pallas_call(
        functools.partial(_s5_kernel, nchunk=nchunk),
        grid=(bsz, ns),
        in_specs=[pl.BlockSpec((nchunk, cols), lambda b, s: (b, s)),
                  pl.BlockSpec((None,) + k2.shape[1:], slab4),
                  pl.BlockSpec((None,) + w_z.shape[1:], slab4),
                  pl.BlockSpec((None, st, cols), slab3),
                  pl.BlockSpec((None, st, cols), slab3),
                  pl.BlockSpec((None, 1, st), slab3),
                  pl.BlockSpec((None, 1, st), slab3),
                  pl.BlockSpec((None, 1, cols), slab3),
                  pl.BlockSpec((None, 256, 256), slab3),
                  pl.BlockSpec((None, 1, cols), slab3)],
        out_specs=pl.BlockSpec((seq, 128), lambda b, s: (b, s)),
        out_shape=jax.ShapeDtypeStruct((bsz * seq, ns * 128), F32),
        scratch_shapes=[pltpu.VMEM((nchunk, st), F32), pltpu.VMEM((nchunk, st), F32), pltpu.VMEM((nchunk, st), F32)],
        compiler_params=_cparams(("arbitrary", "arbitrary"), VMEM_LIMIT),
        name="s5_mixer",
    )(u2, k2, w_z, w_ch, w_cl, dec_a, dec_b, d_t, glu2, glu_bt)


def _dsa_kernel(qt_ref, qit_ref, wt_ref, k_ref, vt_ref, ki_ref, o_ref,
                sc_ref, s_ref, m_ref, l_ref, acc_ref, *, tq, ck, k_sel, idx_bits, max_iter):
    i = pl.program_id(1)
    nheads = qt_ref.shape[0]
    n_ck = (i * tq + tq + ck - 1) // ck
    t_pos = i * tq + lax.broadcasted_iota(jnp.int32, (1, tq), 1)
    krow = lax.broadcasted_iota(jnp.int32, (ck, 1), 0)
    kf = float(k_sel)
    inf = jnp.inf

    fr = 32
    sub = min(256, ck)

    def fold(x, op):
        return op(x.reshape(ck // fr, fr, tq), axis=0)

    wt = wt_ref[...]

    def score_chunk(c, carry):
        lo8, hi8 = carry
        for part in range(ck // sub):
            off = pl.multiple_of(c * ck + part * sub, sub)
            kic = ki_ref[pl.ds(off, sub), :]
            acc = jnp.zeros((sub, tq), F32)
            for h in range(nheads):
                acc = acc + wt[h:h + 1, :] * jnp.maximum(_dot(kic, qit_ref[h]), 0.0)
            acc = acc + 0.0
            vis = off + krow[:sub] <= t_pos
            sc_ref[c, part * sub:(part + 1) * sub, :] = jnp.where(vis, acc, -inf)
            hi8 = jnp.maximum(hi8, jnp.max(jnp.where(vis, acc, -inf).reshape(sub // fr, fr, tq), axis=0))
            lo8 = jnp.minimum(lo8, jnp.min(jnp.where(vis, acc, inf).reshape(sub // fr, fr, tq), axis=0))
        return lo8, hi8

    lo8, hi8 = lax.fori_loop(0, n_ck, score_chunk, (jnp.full((fr, tq), inf, F32), jnp.full((fr, tq), -inf, F32)))
    lo = jnp.min(lo8, axis=0, keepdims=True)
    hi = jnp.max(hi8, axis=0, keepdims=True)
    n_vis = (t_pos + 1).astype(F32)
    few = n_vis < kf

    def reduce_keys(fn, init, combine, op):
        def body(c, part):
            return combine(part, fold(fn(sc_ref[c], c * ck), op))
        return op(lax.fori_loop(0, n_ck, body, jnp.full((fr, tq), init, F32)), axis=0, keepdims=True)

    def count(pred):
        return reduce_keys(lambda x, off: jnp.where(pred(x, off), 1.0, 0.0), 0.0, jnp.add, jnp.sum)

    def bisect(lo_, hi_, c_lo):
        mid = lo_ + (hi_ - lo_) * 0.5
        cnt = count(lambda x, off: x >= mid)
        up = cnt >= kf
        return jnp.where(up, mid, lo_), jnp.where(up, hi_, mid), jnp.where(up, cnt, c_lo)

    def status(lo_, hi_, c_lo):
        mid = lo_ + (hi_ - lo_) * 0.5
        unresolved = jnp.where(few, 0.0, jnp.where(c_lo == kf, 0.0, 1.0))
        movable = jnp.where(mid > lo_, jnp.where(mid < hi_, 1.0, 0.0), 0.0)
        return jnp.max(unresolved), jnp.max(unresolved * movable)

    def search_body(st):
        it, lo_, hi_, c_lo, _, _ = st
        for _ in range(2):
            lo_, hi_, c_lo = bisect(lo_, hi_, c_lo)
        return (it + 2, lo_, hi_, c_lo) + status(lo_, hi_, c_lo)

    st = lax.while_loop(lambda st: jnp.logical_and(st[0] < max_iter, st[5] > 0.5), search_body,
                        (jnp.int32(0), lo, hi, n_vis) + status(lo, hi, n_vis))
    lo, open_ = st[1], st[4]

    def resolve():
        def walk(thr_, strict):
            keep = (lambda x: x > thr_) if strict else (lambda x: x >= thr_)
            return reduce_keys(lambda x, off: jnp.where(keep(x), x, inf), inf, jnp.minimum, jnp.min)

        def above_equal(thr_):
            return count(lambda x, off: x > thr_), count(lambda x, off: x == thr_)

        def too_low(n_gt):
            return jnp.where(few, 0.0, jnp.where(n_gt >= kf, 1.0, 0.0))

        thr0 = walk(lo, False)

        def walk_body(st):
            thr_, n_gt, _ = st
            thr2 = jnp.where(too_low(n_gt) > 0.5, walk(thr_, True), thr_)
            return (thr2,) + above_equal(thr2)

        thr_, n_gt, n_eq = lax.while_loop(lambda st: jnp.max(too_low(st[1])) > 0.5, walk_body,
                                          (thr0,) + above_equal(thr0))
        need = kf - n_gt

        def idx_step(b, cur):
            trial = cur | (jnp.int32(1) << (idx_bits - 1 - b))
            cnt = reduce_keys(lambda x, off: jnp.where(x == thr_, jnp.where(off + krow < trial, 1.0, 0.0), 0.0),
                              0.0, jnp.add, jnp.sum)
            return jnp.where(cnt < need, trial, cur)

        return thr_, lax.fori_loop(0, idx_bits, idx_step, jnp.zeros((1, tq), jnp.int32))

    thr, cut = lax.cond(open_ > 0.5, resolve, lambda: (lo, jnp.full((1, tq), 0x7FFFFFFF, jnp.int32)))

    def bias_chunk(c, _):
        x = sc_ref[c]
        at_thr = jnp.where(x == thr, jnp.where(c * ck + krow <= cut, 0.0, NEG_BIG), NEG_BIG)
        sc_ref[c] = jnp.where(x > thr, 0.0, at_thr)
        return 0

    lax.fori_loop(0, n_ck, bias_chunk, 0)

    for h in range(nheads):
        m_ref[h] = jnp.full((fr, tq), NEG_BIG, F32)
        l_ref[h] = jnp.zeros((fr, tq), F32)
        acc_ref[h] = jnp.zeros((HEAD_DIM, tq), F32)

    def attn_scores(c, _):
        off = pl.multiple_of(c * ck, ck)
        kc = k_ref[pl.ds(off, ck), :]
        bias = sc_ref[c]
        for h in range(nheads):
            s = _dot(kc, qt_ref[h]) + bias
            s_ref[h, c] = s
            m_ref[h] = jnp.maximum(m_ref[h], fold(s, jnp.max))
        return 0

    lax.fori_loop(0, n_ck, attn_scores, 0)
    for h in range(nheads):
        m_ref[h] = jnp.broadcast_to(jnp.max(m_ref[h], axis=0, keepdims=True), (fr, tq))

    def attn_values(c, _):
        vt = vt_ref[c]
        for h in range(nheads):
            p = jnp.exp(s_ref[h, c] - m_ref[h][:1, :])
            l_ref[h] += fold(p, jnp.sum)
            acc_ref[h] += _dot(vt, p.astype(BF16))
        return 0

    lax.fori_loop(0, n_ck, attn_values, 0)
    for h in range(nheads):
        l_row = jnp.sum(l_ref[h], axis=0, keepdims=True)
        o_ref[h * HEAD_DIM:(h + 1) * HEAD_DIM, :] = (acc_ref[h] / l_row).astype(o_ref.dtype)


def _dsa_attention(qt, qit, wi, k, v, ki, bsz, seq):
    tq = 128
    ck = min(512, seq)
    n_ck = seq // ck
    k_sel = min(TOPK_MAX, seq // 4)
    nq = seq // tq
    nh = ATTN_HEADS
    width = nh * HEAD_DIM
    vt = v.reshape(bsz, n_ck, ck, HEAD_DIM).transpose(0, 1, 3, 2)
    wt = wi[:, :IDX_HEADS].T
    qspec = pl.BlockSpec((None, nh, HEAD_DIM, tq), lambda b, i: (b, 0, 0, i))
    out_t = pl.pallas_call(
        functools.partial(_dsa_kernel, tq=tq, ck=ck, k_sel=k_sel, idx_bits=(seq - 1).bit_length(), max_iter=48),
        grid=(bsz, nq),
        in_specs=[qspec, qspec,
                  pl.BlockSpec((IDX_HEADS, tq), lambda b, i: (0, b * nq + i)),
                  pl.BlockSpec((seq, HEAD_DIM), lambda b, i: (b, 0)),
                  pl.BlockSpec((None, n_ck, HEAD_DIM, ck), lambda b, i: (b, 0, 0, 0)),
                  pl.BlockSpec((seq, HEAD_DIM), lambda b, i: (b, 0))],
        out_specs=pl.BlockSpec((None, width, tq), lambda b, i: (b, 0, i)),
        out_shape=jax.ShapeDtypeStruct((bsz, width, seq), BF16),
        scratch_shapes=[pltpu.VMEM((n_ck, ck, tq), F32),
                        pltpu.VMEM((nh, n_ck, ck, tq), F32),
                        pltpu.VMEM((nh, 32, tq), F32),
                        pltpu.VMEM((nh, 32, tq), F32),
                        pltpu.VMEM((nh, HEAD_DIM, tq), F32)],
        compiler_params=_cparams(("arbitrary", "arbitrary"), VMEM_LIMIT),
        name="dsa_attention",
    )(qt, qit, wt, k, vt, ki)
    return out_t.transpose(0, 2, 1).reshape(bsz * seq, width)


def _dsa_kernel_rows(q_ref, qi_ref, wi_ref, kt_ref, v_ref, kit_ref, o_ref,
                     sc_ref, s_ref, wb_ref, m_ref, l_ref, acc_ref, *, tq, ck, k_sel, idx_bits, max_iter):
    i = pl.program_id(1)
    nheads = q_ref.shape[0]
    nj = ck // 128
    n_ck = (i * tq + tq + ck - 1) // ck
    shape = (tq, 128)
    t_pos = i * tq + lax.broadcasted_iota(jnp.int32, shape, 0)
    lane = lax.broadcasted_iota(jnp.int32, shape, 1)
    kf = float(k_sel)
    inf = jnp.inf

    def rep(col):
        return jnp.broadcast_to(col, shape)

    def cols(x):
        return [x[:, j * 128:(j + 1) * 128] for j in range(nj)]

    w = wi_ref[...]
    for h in range(nheads):
        wb_ref[h] = rep(w[:, h:h + 1])

    def score_chunk(c, carry):
        lo, hi = carry
        kit = kit_ref[c]
        accs = [jnp.zeros(shape, F32) for _ in range(nj)]
        for h in range(nheads):
            s = jnp.maximum(_dot(qi_ref[h], kit), 0.0)
            wb = wb_ref[h]
            accs = [a + wb * sj for a, sj in zip(accs, cols(s))]
        for j in range(nj):
            a = accs[j] + 0.0
            vis = (c * ck + j * 128) + lane <= t_pos
            sc_ref[c, :, j * 128:(j + 1) * 128] = jnp.where(vis, a, -inf)
            hi = jnp.maximum(hi, jnp.where(vis, a, -inf))
            lo = jnp.minimum(lo, jnp.where(vis, a, inf))
        return lo, hi

    lo, hi = lax.fori_loop(0, n_ck, score_chunk, (jnp.full(shape, inf, F32), jnp.full(shape, -inf, F32)))
    lo = rep(jnp.min(lo, axis=1, keepdims=True))
    hi = rep(jnp.max(hi, axis=1, keepdims=True))
    few = (t_pos + 1).astype(F32) < kf

    def reduce_rows(fn, init, combine, final):
        def body(c, part):
            for xj in cols(sc_ref[c]):
                part = combine(part, fn(xj))
            return part
        return rep(final(lax.fori_loop(0, n_ck, body, jnp.full(shape, init, F32)), axis=1, keepdims=True))

    def count(pred):
        return reduce_rows(lambda xj: jnp.where(pred(xj), 1.0, 0.0), 0.0, jnp.add, jnp.sum)

    def pivot(it, lo_, hi_, c_lo, c_hi):
        frac = jnp.clip((c_lo - (kf + 0.5)) / (c_lo - c_hi), 0.0625, 0.9375)
        frac = jnp.where(it % 2 == 0, 0.5, frac)
        mid = lo_ + (hi_ - lo_) * frac
        half = lo_ + (hi_ - lo_) * 0.5
        return jnp.where(mid > lo_, jnp.where(mid < hi_, mid, half), half)

    def pending(lo_, hi_, mid, c_lo):
        unresolved = jnp.where(few, 0.0, jnp.where(c_lo == kf, 0.0, 1.0))
        movable = jnp.where(mid > lo_, jnp.where(mid < hi_, 1.0, 0.0), 0.0)
        return jnp.max(unresolved), jnp.max(unresolved * movable)

    def search_cond(st):
        return jnp.logical_and(st[0] < max_iter, st[-1] > 0.5)

    def search_body(st):
        it, lo_, hi_, c_lo, c_hi, mid, _, _ = st
        cnt = count(lambda xj: xj >= mid)
        up = cnt >= kf
        lo2 = jnp.where(up, mid, lo_)
        hi2 = jnp.where(up, hi_, mid)
        c_lo2 = jnp.where(up, cnt, c_lo)
        c_hi2 = jnp.where(up, c_hi, cnt)
        mid2 = pivot(it + 1, lo2, hi2, c_lo2, c_hi2)
        open_, go = pending(lo2, hi2, mid2, c_lo2)
        return it + 1, lo2, hi2, c_lo2, c_hi2, mid2, open_, go

    c_lo0 = (t_pos + 1).astype(F32)
    c_hi0 = jnp.zeros(shape, F32)
    mid0 = pivot(jnp.int32(0), lo, hi, c_lo0, c_hi0)
    open0, go0 = pending(lo, hi, mid0, c_lo0)
    st = lax.while_loop(search_cond, search_body, (jnp.int32(0), lo, hi, c_lo0, c_hi0, mid0, open0, go0))
    lo, open_ = st[1], st[6]

    def resolve():
        def walk(thr_, strict):
            keep = (lambda xj: xj > thr_) if strict else (lambda xj: xj >= thr_)
            return reduce_rows(lambda xj: jnp.where(keep(xj), xj, inf), inf, jnp.minimum, jnp.min)

        def above_equal(thr_):
            return count(lambda xj: xj > thr_), count(lambda xj: xj == thr_)

        def too_low(n_gt):
            return jnp.where(few, 0.0, jnp.where(n_gt >= kf, 1.0, 0.0))

        thr0 = walk(lo, False)
        gt0, eq0 = above_equal(thr0)

        def walk_body(st):
            thr_, n_gt, _ = st
            thr2 = jnp.where(too_low(n_gt) > 0.5, walk(thr_, True), thr_)
            return (thr2,) + above_equal(thr2)

        thr_, n_gt, n_eq = lax.while_loop(lambda st: jnp.max(too_low(st[1])) > 0.5, walk_body, (thr0, gt0, eq0))
        need = kf - n_gt

        def idx_step(b, cur):
            trial = cur | (jnp.int32(1) << (idx_bits - 1 - b))

            def body(c, part):
                for j, xj in enumerate(cols(sc_ref[c])):
                    idx = (c * ck + j * 128) + lane
                    part = part + jnp.where(xj == thr_, jnp.where(idx < trial, 1.0, 0.0), 0.0)
                return part
            cnt = rep(jnp.sum(lax.fori_loop(0, n_ck, body, jnp.zeros(shape, F32)), axis=1, keepdims=True))
            return jnp.where(cnt < need, trial, cur)

        cut_ = lax.fori_loop(0, idx_bits, idx_step, jnp.zeros(shape, jnp.int32))
        return thr_, cut_

    thr, cut = lax.cond(open_ > 0.5, resolve, lambda: (lo, jnp.full(shape, 0x7FFFFFFF, jnp.int32)))

    def bias_chunk(c, _):
        x = sc_ref[c]
        for j, xj in enumerate(cols(x)):
            idx = (c * ck + j * 128) + lane
            at_thr = jnp.where(xj == thr, jnp.where(idx <= cut, 0.0, NEG_BIG), NEG_BIG)
            sc_ref[c, :, j * 128:(j + 1) * 128] = jnp.where(xj > thr, 0.0, at_thr)
        return 0

    lax.fori_loop(0, n_ck, bias_chunk, 0)

    for h in range(nheads):
        m_ref[h] = jnp.full(shape, NEG_BIG, F32)
        l_ref[h] = jnp.zeros(shape, F32)
        acc_ref[h] = jnp.zeros((tq, HEAD_DIM), F32)

    def attn_scores(c, _):
        kt = kt_ref[c]
        bias = cols(sc_ref[c])
        for h in range(nheads):
            s = cols(_dot(q_ref[h], kt))
            part = m_ref[h]
            for j in range(nj):
                sj = s[j] + bias[j]
                s_ref[h, c, :, j * 128:(j + 1) * 128] = sj
                part = jnp.maximum(part, sj)
            m_ref[h] = part
        return 0

    lax.fori_loop(0, n_ck, attn_scores, 0)
    for h in range(nheads):
        m_ref[h] = rep(jnp.max(m_ref[h], axis=1, keepdims=True))

    def attn_values(c, _):
        off = pl.multiple_of(c * ck, ck)
        vc = v_ref[pl.ds(off, ck), :]
        for h in range(nheads):
            m = m_ref[h]
            p = [jnp.exp(sj - m) for sj in cols(s_ref[h, c])]
            l_ref[h] += functools.reduce(jnp.add, p)
            acc_ref[h] += _dot(jnp.concatenate(p, axis=1).astype(BF16), vc)
        return 0

    lax.fori_loop(0, n_ck, attn_values, 0)
    for h in range(nheads):
        l_row = jnp.sum(l_ref[h], axis=1, keepdims=True)
        o_ref[:, h * HEAD_DIM:(h + 1) * HEAD_DIM] = (acc_ref[h] / l_row).astype(o_ref.dtype)


def _dsa_attention_rows(q_hm, qi_hm, wi, k, v, ki, bsz, seq):
    tq = 128
    ck = min(512, seq)
    n_ck = seq // ck
    k_sel = min(TOPK_MAX, seq // 4)
    nq = seq // tq
    width = ATTN_HEADS * HEAD_DIM
    kt = k.reshape(bsz, n_ck, ck, HEAD_DIM).transpose(0, 1, 3, 2)
    kit = ki.reshape(bsz, n_ck, ck, HEAD_DIM).transpose(0, 1, 3, 2)
    hm = lambda b, i: (b, 0, i, 0)
    per_batch = lambda b, i: (b, 0, 0, 0)
    nh = ATTN_HEADS
    return pl.pallas_call(
        functools.partial(_dsa_kernel_rows, tq=tq, ck=ck, k_sel=k_sel, idx_bits=(seq - 1).bit_length(), max_iter=48),
        grid=(bsz, nq),
        in_specs=[pl.BlockSpec((None, ATTN_HEADS, tq, HEAD_DIM), hm),
                  pl.BlockSpec((None, IDX_HEADS, tq, HEAD_DIM), hm),
                  pl.BlockSpec((tq, 128), lambda b, i: (b * nq + i, 0)),
                  pl.BlockSpec((None, n_ck, HEAD_DIM, ck), per_batch),
                  pl.BlockSpec((seq, HEAD_DIM), lambda b, i: (b, 0)),
                  pl.BlockSpec((None, n_ck, HEAD_DIM, ck), per_batch)],
        out_specs=pl.BlockSpec((tq, width), lambda b, i: (b * nq + i, 0)),
        out_shape=jax.ShapeDtypeStruct((bsz * seq, width), BF16),
        scratch_shapes=[pltpu.VMEM((n_ck, tq, ck), F32),
                        pltpu.VMEM((nh, n_ck, tq, ck), F32),
                        pltpu.VMEM((IDX_HEADS, tq, 128), F32),
                        pltpu.VMEM((nh, tq, 128), F32),
                        pltpu.VMEM((nh, tq, 128), F32),
                        pltpu.VMEM((nh, tq, HEAD_DIM), F32)],
        compiler_params=_cparams(("arbitrary", "arbitrary"), VMEM_LIMIT),
        name="dsa_attention",
    )(q_hm, qi_hm, wi, kt, v, kit)


def _outproj_kernel(x_ref, ys_ref, ya_ref, w_ref, mod_ref, o_ref, *, d):
    width = ys_ref.shape[1]
    y = _dot(ys_ref[...].astype(BF16), w_ref[:width, :]) + _dot(ya_ref[...], w_ref[width:, :])
    o_ref[...] = x_ref[...] + mod_ref[...][:, 2 * d:] * y


def _out_projection(x2, y_ssm, y_att, w_out, mod, layer, seq):
    t, d = x2.shape
    tm = 512
    width = y_ssm.shape[1]
    tiles_per_seq = seq // tm
    row = lambda i: (i, 0)
    return pl.pallas_call(
        functools.partial(_outproj_kernel, d=d),
        grid=(t // tm,),
        in_specs=[pl.BlockSpec((tm, d), row),
                  pl.BlockSpec((tm, width), row),
                  pl.BlockSpec((tm, width), row),
                  pl.BlockSpec((2 * width, d), lambda i: (0, 0)),
                  pl.BlockSpec((None, None, 1, 3 * d), lambda i: (layer, i // tiles_per_seq, 0, 0))],
        out_specs=pl.BlockSpec((tm, d), row),
        out_shape=jax.ShapeDtypeStruct((t, d), F32),
        compiler_params=_cparams(("arbitrary",)),
        name="out_projection",
    )(x2, y_ssm, y_att, w_out.astype(BF16), mod)


def _pool_kernel(x_ref, halo_ref, g_ref, mod_ref, pw_ref, ps_ref, o_ref, h_ref, *, d, tiles_per_seq):
    i = pl.program_id(0)
    tm = x_ref.shape[0]
    mod = mod_ref[...]
    shift, scale, gate = mod[:, :d], mod[:, d:2 * d], mod[:, 2 * d:]
    x = x_ref[...]
    first = (i % tiles_per_seq) == 0
    h_ref[POOL_HALO:, :] = _norm_mod(x, g_ref[...], shift, scale)
    halo = _norm_mod(halo_ref[...], g_ref[...], shift, scale)
    h_ref[:POOL_HALO, :] = jnp.where(first, 0.0, halo)
    pos = (i % tiles_per_seq) * tm + lax.broadcasted_iota(jnp.int32, (tm, 1), 0)
    gw = d // len(POOL_WINDOWS)
    ys = []
    for gi, win in enumerate(POOL_WINDOWS):
        cs = slice(gi * gw, (gi + 1) * gw)
        cur = h_ref[POOL_HALO:, cs]
        tot = cur
        for k in range(1, win):
            tot = tot + h_ref[POOL_HALO - k:POOL_HALO - k + tm, cs]
        cnt = jnp.minimum(pos + 1, win).astype(F32)
        pooled = tot / cnt - cur
        ys.append(_dot(pooled.astype(BF16), pw_ref[gi]))
    y = jnp.concatenate(ys, axis=1) * ps_ref[...]
    o_ref[...] = x + gate * y


def _pool_layer(x2, g, mod, layer, pool_w, pool_scale, seq):
    t, d = x2.shape
    tm = 512
    tiles_per_seq = seq // tm
    gw = d // len(POOL_WINDOWS)
    hb = tm // POOL_HALO
    return pl.pallas_call(
        functools.partial(_pool_kernel, d=d, tiles_per_seq=tiles_per_seq),
        grid=(t // tm,),
        in_specs=[pl.BlockSpec((tm, d), lambda i: (i, 0)),
                  pl.BlockSpec((POOL_HALO, d), lambda i: (jnp.maximum(i * hb - 1, 0), 0)),
                  pl.BlockSpec((1, d), lambda i: (0, 0)),
                  pl.BlockSpec((None, None, 1, 3 * d), lambda i: (layer, i // tiles_per_seq, 0, 0)),
                  pl.BlockSpec((len(POOL_WINDOWS), gw, gw), lambda i: (0, 0, 0)),
                  pl.BlockSpec((1, d), lambda i: (0, 0))],
        out_specs=pl.BlockSpec((tm, d), lambda i: (i, 0)),
        out_shape=jax.ShapeDtypeStruct((t, d), F32),
        scratch_shapes=[pltpu.VMEM((tm + POOL_HALO, d), F32)],
        compiler_params=_cparams(("arbitrary",)),
        name="pool_mixer",
    )(x2, x2, g.reshape(1, d), mod, pool_w.astype(BF16), pool_scale.reshape(1, d))


def _first_max(vals, idx, big):
    m = jnp.max(vals, axis=0, keepdims=True)
    first = jnp.min(jnp.where(vals == m, idx, big), axis=0, keepdims=True)
    return m, first


def _pack_bf16_pairs(x):
    n = x.shape[1] // 2
    lo = pltpu.bitcast(x[:, :n].astype(BF16).astype(F32), jnp.int32)
    hi = pltpu.bitcast(x[:, n:].astype(BF16).astype(F32), jnp.int32)
    return hi | ((lo >> 16) & 0xFFFF)


def _unpack_bf16_pairs(w):
    lo = pltpu.bitcast(w << 16, F32)
    hi = pltpu.bitcast(w & jnp.int32(-65536), F32)
    return jnp.concatenate([lo, hi], axis=1)


def _router_kernel(x_ref, g_ref, mod_ref, rw_ref, rb_ref, tri_ref, h_ref, gate_ref, rank_ref, cnt_ref,
                   run_ref, *, d, n_exp):
    mod = mod_ref[...]
    h = _norm_mod(x_ref[...], g_ref[...], mod[:, :d], mod[:, d:2 * d])
    h_ref[...] = _pack_bf16_pairs(h)
    tm = h.shape[0]
    hh, hl = _split_bf16(h)
    rw = rw_ref[...]
    rh, rl = _split_bf16(rw)
    logits = _dot_nt(rh, hh) + (_dot_nt(rh, hl) + _dot_nt(rl, hh))
    scores = jax.nn.sigmoid(logits)
    sel = scores + rb_ref[...]
    gsz = n_exp // N_EXPERT_GROUPS
    neg = -jnp.inf
    shape3 = (N_EXPERT_GROUPS, gsz, tm)
    sel3 = sel.reshape(shape3)
    sub = lax.broadcasted_iota(jnp.int32, shape3, 1)
    m1 = jnp.max(sel3, axis=1, keepdims=True)
    f1 = jnp.min(jnp.where(sel3 == m1, sub, gsz), axis=1, keepdims=True)
    m2 = jnp.max(jnp.where(sub == f1, neg, sel3), axis=1, keepdims=True)
    gscore = jnp.broadcast_to(m1 + m2, shape3).reshape(n_exp, tm)
    eidx = lax.broadcasted_iota(jnp.int32, (n_exp, tm), 0)
    gidx = eidx // gsz
    keep = jnp.zeros((n_exp, tm), F32)
    work = gscore
    for _ in range(TOPK_GROUPS):
        _, first = _first_max(work, gidx, N_EXPERT_GROUPS)
        hit = gidx == first
        keep = jnp.where(hit, 1.0, keep)
        work = jnp.where(hit, neg, work)
    work = jnp.where(keep > 0.0, sel, neg)
    chosen = jnp.zeros((n_exp, tm), F32)
    for _ in range(TOP_K):
        _, first = _first_max(work, eidx, n_exp)
        hit = eidx == first
        chosen = jnp.where(hit, 1.0, chosen)
        work = jnp.where(hit, neg, work)
    picked = chosen * scores
    gate_ref[...] = picked / jnp.sum(picked, axis=0, keepdims=True) * ROUTED_SCALE

    @pl.when(pl.program_id(0) == 0)
    def _():
        run_ref[...] = jnp.zeros(run_ref.shape, F32)

    before = _dot(chosen.astype(BF16), tri_ref[...])
    run = run_ref[...]
    rank_ref[...] = jnp.where(chosen > 0.0, before + run[:, :1], -1.0)
    run = run + jnp.broadcast_to(jnp.sum(chosen, axis=1, keepdims=True), run.shape)
    run_ref[...] = run
    cnt_ref[...] = run


def _ffn_router(x2, g, mod, layer, router_w, router_bias, seq):
    t, d = x2.shape
    n_exp = router_w.shape[1]
    tm = 512
    tiles_per_seq = seq // tm
    tri = (np.arange(tm)[:, None] < np.arange(tm)[None, :]).astype(np.float32)
    return pl.pallas_call(
        functools.partial(_router_kernel, d=d, n_exp=n_exp),
        grid=(t // tm,),
        in_specs=[pl.BlockSpec((tm, d), lambda i: (i, 0)),
                  pl.BlockSpec((1, d), lambda i: (0, 0)),
                  pl.BlockSpec((None, None, 1, 3 * d), lambda i: (layer, i // tiles_per_seq, 0, 0)),
                  pl.BlockSpec((n_exp, d), lambda i: (0, 0)),
                  pl.BlockSpec((n_exp, 1), lambda i: (0, 0)),
                  pl.BlockSpec((tm, tm), lambda i: (0, 0))],
        out_specs=[pl.BlockSpec((tm, d // 2), lambda i: (i, 0)),
                   pl.BlockSpec((n_exp, tm), lambda i: (0, i)),
                   pl.BlockSpec((n_exp, tm), lambda i: (0, i)),
                   pl.BlockSpec((n_exp, 128), lambda i: (0, 0))],
        out_shape=[jax.ShapeDtypeStruct((t, d // 2), jnp.int32),
                   jax.ShapeDtypeStruct((n_exp, t), F32),
                   jax.ShapeDtypeStruct((n_exp, t), F32),
                   jax.ShapeDtypeStruct((n_exp, 128), F32)],
        scratch_shapes=[pltpu.VMEM((n_exp, 128), F32)],
        compiler_params=_cparams(("arbitrary",)),
        name="ffn_router",
    )(x2, g.reshape(1, d), mod, router_w.T, router_bias.reshape(n_exp, 1), jnp.asarray(tri, BF16))


def _assign_kernel(rank_ref, gate_ref, start_ref, pos_ref, w_ref, *, n_exp):
    rank = rank_ref[...]
    gates = gate_ref[...]
    tm = rank.shape[1]
    slot = rank + start_ref[...]
    eidx = lax.broadcasted_iota(jnp.int32, (n_exp, tm), 0).astype(F32)
    alive = jnp.where(rank >= 0.0, eidx, float(n_exp))
    kidx = lax.broadcasted_iota(jnp.int32, (TOP_K, tm), 0)
    pos = jnp.zeros((TOP_K, tm), F32)
    wts = jnp.zeros((TOP_K, tm), F32)
    for k in range(TOP_K):
        first = jnp.min(alive, axis=0, keepdims=True)
        hit = alive == first
        pos_k = jnp.sum(jnp.where(hit, slot, 0.0), axis=0, keepdims=True)
        w_k = jnp.sum(jnp.where(hit, gates, 0.0), axis=0, keepdims=True)
        pos = jnp.where(kidx == k, pos_k, pos)
        wts = jnp.where(kidx == k, w_k, wts)
        alive = jnp.where(hit, float(n_exp), alive)
    pos_ref[...] = pos.astype(jnp.int32)
    w_ref[...] = wts


def _assign_slots(ranks, gates, start):
    n_exp, t = ranks.shape
    tm = 512
    return pl.pallas_call(
        functools.partial(_assign_kernel, n_exp=n_exp),
        grid=(t // tm,),
        in_specs=[pl.BlockSpec((n_exp, tm), lambda i: (0, i)),
                  pl.BlockSpec((n_exp, tm), lambda i: (0, i)),
                  pl.BlockSpec((n_exp, 1), lambda i: (0, 0))],
        out_specs=[pl.BlockSpec((TOP_K, tm), lambda i: (0, i)),
                   pl.BlockSpec((TOP_K, tm), lambda i: (0, i))],
        out_shape=[jax.ShapeDtypeStruct((TOP_K, t), jnp.int32), jax.ShapeDtypeStruct((TOP_K, t), F32)],
        compiler_params=_cparams(("arbitrary",)),
        name="moe_assign",
    )(ranks, gates, start)


SC_CORES = 2
SC_SUBCORES = 16
SC_WINDOW = 128


def _sc_mesh():
    return plsc.VectorSubcoreMesh(core_axis_name="c", subcore_axis_name="s",
                                  num_cores=SC_CORES, num_subcores=SC_SUBCORES)


def _sc_scatter_rows(rows, pos, n_out):
    t, width = rows.shape
    nk = pos.shape[0]
    win = SC_WINDOW
    n_win = t // win // (SC_CORES * SC_SUBCORES)
    pos_w = pos.reshape(nk, t // win, win).transpose(1, 0, 2)

    def body(rows_hbm, pos_hbm, out_hbm, idx_v, buf_v, sem):
        wid = lax.axis_index("s") * SC_CORES + lax.axis_index("c")

        @pl.loop(0, n_win)
        def _(j):
            w = wid * n_win + j
            pltpu.sync_copy(rows_hbm.at[pl.ds(w * win, win)], buf_v)
            pltpu.sync_copy(pos_hbm.at[w], idx_v)
            copies = [pltpu.make_async_copy(buf_v, out_hbm.at[idx_v.at[k]], sem) for k in range(nk)]
            for cp in copies:
                cp.start()
            for cp in copies:
                cp.wait()

    return pl.kernel(
        body, out_type=jax.ShapeDtypeStruct((n_out, width), jnp.int32), mesh=_sc_mesh(),
        scratch_types=[pltpu.VMEM((nk, win), jnp.int32), pltpu.VMEM((win, width), jnp.int32),
                       pltpu.SemaphoreType.DMA],
        name="sc_scatter_rows",
    )(rows, pos_w)


SC_GATHER_WINDOW = 64


def _sc_gather_rows(table, pos_flat):
    m = pos_flat.shape[0]
    width = table.shape[1]
    win = SC_GATHER_WINDOW
    workers = SC_CORES * SC_SUBCORES
    n_win = m // win // workers
    pos_w = pos_flat.reshape(workers, n_win, win)

    def body(table_hbm, pos_hbm, out_hbm, idx_v, buf_v, sem_g, sem_w):
        wid = lax.axis_index("s") * SC_CORES + lax.axis_index("c")
        base = wid * (n_win * win)
        pltpu.sync_copy(pos_hbm.at[wid], idx_v)

        def gather(j, b):
            return pltpu.make_async_copy(table_hbm.at[idx_v.at[j]], buf_v.at[b], sem_g.at[b])

        def write(j, b):
            return pltpu.make_async_copy(buf_v.at[b], out_hbm.at[pl.ds(base + j * win, win)], sem_w.at[b])

        gather(0, 0).start()

        @pl.loop(0, n_win, step=2)
        def _(j0):
            for b in range(2):
                j = j0 + b
                gather(j, b).wait()
                write(j, b).start()

                @pl.when(j + 1 < n_win)
                def _():
                    @pl.when(j >= 1)
                    def _():
                        write(j - 1, 1 - b).wait()
                    gather(j + 1, 1 - b).start()

        write(n_win - 2, 0).wait()
        write(n_win - 1, 1).wait()

    return pl.kernel(
        body, out_type=jax.ShapeDtypeStruct((m, width), jnp.int32), mesh=_sc_mesh(),
        scratch_types=[pltpu.VMEM((n_win, win), jnp.int32), pltpu.VMEM((2, win, width), jnp.int32),
                       pltpu.SemaphoreType.DMA((2,)), pltpu.SemaphoreType.DMA((2,))],
        name="sc_gather_rows",
    )(table, pos_w)


MOE_BLOCK = 512


def _silu_mul(a, b):
    return (a * jax.nn.sigmoid(a)) * b


def _expert_ffn_kernel(be_ref, nv_ref, xs_ref, wg_ref, wu_ref, wd_ref, ys_ref, wg_bf, wu_bf, wd_bf):
    i = pl.program_id(0)
    fresh = jnp.logical_or(i == 0, be_ref[i] != be_ref[jnp.maximum(i - 1, 0)])

    @pl.when(fresh)
    def _():
        wg_bf[...] = wg_ref[...].astype(BF16)
        wu_bf[...] = wu_ref[...].astype(BF16)
        wd_bf[...] = wd_ref[...].astype(BF16)

    nv = nv_ref[i]

    @pl.when(nv > 0)
    def _():
        x = _unpack_bf16_pairs(xs_ref[...])
        row = lax.broadcasted_iota(jnp.int32, (x.shape[0], 1), 0)
        x = jnp.where(row < nv, x, 0.0).astype(BF16)
        mid = _silu_mul(_dot(x, wg_bf[...]), _dot(x, wu_bf[...]))
        ys_ref[...] = _pack_bf16_pairs(_dot(mid.astype(BF16), wd_bf[...]))

    @pl.when(nv <= 0)
    def _():
        ys_ref[...] = jnp.zeros(ys_ref.shape, ys_ref.dtype)


def _expert_ffn(xs, block_expert, block_rows, w_gate, w_up, w_down):
    n_rows, half = xs.shape
    n_exp, d, de = w_gate.shape
    nb = n_rows // MOE_BLOCK
    wmap = lambda i, be, nv: (be[i], 0, 0)
    grid_spec = pltpu.PrefetchScalarGridSpec(
        num_scalar_prefetch=2,
        grid=(nb,),
        in_specs=[pl.BlockSpec((MOE_BLOCK, half), lambda i, be, nv: (i, 0)),
                  pl.BlockSpec((None, d, de), wmap),
                  pl.BlockSpec((None, d, de), wmap),
                  pl.BlockSpec((None, de, d), wmap)],
        out_specs=pl.BlockSpec((MOE_BLOCK, half), lambda i, be, nv: (i, 0)),
        scratch_shapes=[pltpu.VMEM((d, de), BF16), pltpu.VMEM((d, de), BF16), pltpu.VMEM((de, d), BF16)],
    )
    return pl.pallas_call(
        _expert_ffn_kernel,
        grid_spec=grid_spec,
        out_shape=jax.ShapeDtypeStruct((n_rows, half), jnp.int32),
        compiler_params=_cparams(("arbitrary",)),
        name="moe_expert_ffn",
    )(block_expert, block_rows, xs, w_gate, w_up, w_down)


def _combine_kernel(yt_ref, wt_ref, hp_ref, x_ref, mod_ref, sg_ref, su_ref, sd_ref, o_ref, *, d):
    h = _unpack_bf16_pairs(hp_ref[...]).astype(BF16)
    acc = _dot(_silu_mul(_dot(h, sg_ref[...]), _dot(h, su_ref[...])).astype(BF16), sd_ref[...])
    w = wt_ref[...]
    for k in range(TOP_K):
        acc = acc + w[:, k:k + 1] * _unpack_bf16_pairs(yt_ref[k])
    o_ref[...] = x_ref[...] + mod_ref[...][:, 2 * d:] * acc


def _combine(y_tok, w_tok, h_packed, x2, mod, layer, sh_gate, sh_up, sh_down, seq):
    t, d = x2.shape
    ds = sh_gate.shape[1]
    tm = 512
    tiles_per_seq = seq // tm
    row = lambda i: (i, 0)
    const = lambda i: (0, 0)
    return pl.pallas_call(
        functools.partial(_combine_kernel, d=d),
        grid=(t // tm,),
        in_specs=[pl.BlockSpec((TOP_K, tm, d // 2), lambda i: (0, i, 0)),
                  pl.BlockSpec((tm, TOP_K), row),
                  pl.BlockSpec((tm, d // 2), row),
                  pl.BlockSpec((tm, d), row),
                  pl.BlockSpec((None, None, 1, 3 * d), lambda i: (layer, i // tiles_per_seq, 0, 0)),
                  pl.BlockSpec((d, ds), const),
                  pl.BlockSpec((d, ds), const),
                  pl.BlockSpec((ds, d), const)],
        out_specs=pl.BlockSpec((tm, d), row),
        out_shape=jax.ShapeDtypeStruct((t, d), F32),
        compiler_params=_cparams(("arbitrary",), VMEM_LIMIT),
        name="moe_combine",
    )(y_tok, w_tok, h_packed, x2, mod, sh_gate.astype(BF16), sh_up.astype(BF16), sh_down.astype(BF16))


def _moe_layer(x2, h_packed, gates, ranks, counts, mod, layer, w_gate, w_up, w_down, sh_gate, sh_up, sh_down, seq):
    t, d = x2.shape
    n_exp = w_gate.shape[0]
    cnt = counts[:, 0].astype(jnp.int32)
    padded = (cnt + MOE_BLOCK - 1) // MOE_BLOCK * MOE_BLOCK
    seg_end = jnp.cumsum(padded)
    seg_start = seg_end - padded
    nb = t * TOP_K // MOE_BLOCK + n_exp
    first_row = jnp.arange(nb, dtype=jnp.int32) * MOE_BLOCK
    block_expert = jnp.sum((seg_end[None, :] <= first_row[:, None]).astype(jnp.int32), axis=1)
    block_expert = jnp.minimum(block_expert, n_exp - 1)
    block_rows = jnp.clip(seg_start[block_expert] + cnt[block_expert] - first_row, 0, MOE_BLOCK).astype(jnp.int32)
    pos, w = _assign_slots(ranks, gates, seg_start.astype(F32).reshape(n_exp, 1))
    xs = _sc_scatter_rows(h_packed, pos, nb * MOE_BLOCK)
    ys = _expert_ffn(xs, block_expert, block_rows, w_gate, w_up, w_down)
    y_tok = _sc_gather_rows(ys, pos.reshape(TOP_K * t)).reshape(TOP_K, t, d // 2)
    return _combine(y_tok, w.T, h_packed, x2, mod, layer, sh_gate, sh_up, sh_down, seq)


def kernel(x, c, positions, mix_norm_g, mix_mod_w, mix_mod_b, ffn_norm_g, ffn_mod_w, ffn_mod_b,
           hyb_w_in, hyb_w_out, s5_lambda_re, s5_lambda_im, s5_log_dt, s5_b_re, s5_b_im,
           s5_c_re, s5_c_im, s5_d, s5_glu_w, s5_glu_b, attn_q_norm_g, attn_k_norm_g,
           pool_w, pool_scale, router_w, router_bias, exp_w_gate, exp_w_up, exp_w_down,
           sh_w_gate, sh_w_up, sh_w_down):
    bsz, seq, d = x.shape
    t = bsz * seq
    depth = mix_norm_g.shape[0]
    x2 = x.reshape(t, d)
    mix_mod = _mod_vectors(c, mix_mod_w, mix_mod_b).reshape(depth, bsz, 1, 3 * d)
    ffn_mod = _mod_vectors(c, ffn_mod_w, ffn_mod_b).reshape(depth, bsz, 1, 3 * d)
    for i in range(depth):
        j = i // 2
        if i % 2 == 0:
            u, q_hm, qi_hm, k, v, ki, wi = _in_projection(
                x2, mix_norm_g[i], mix_mod, i, hyb_w_in[j], positions, attn_q_norm_g[j], attn_k_norm_g[j], bsz, seq)
            weights = _s5_weights(s5_lambda_re[j], s5_lambda_im[j], s5_log_dt[j], s5_b_re[j], s5_b_im[j],
                                  s5_c_re[j], s5_c_im[j], s5_d[j], s5_glu_w[j], s5_glu_b[j])
            y_ssm = _s5_mixer(u, weights, bsz, seq)
            y_att = _dsa_attention(q_hm, qi_hm, wi, k, v, ki, bsz, seq)
            x2 = _out_projection(x2, y_ssm, y_att, hyb_w_out[j], mix_mod, i, seq)
        else:
            x2 = _pool_layer(x2, mix_norm_g[i], mix_mod, i, pool_w[j], pool_scale[j], seq)
        h_packed, gates, ranks, counts = _ffn_router(x2, ffn_norm_g[i], ffn_mod, i, router_w[i], router_bias[i], seq)
        x2 = _moe_layer(x2, h_packed, gates, ranks, counts, ffn_mod, i, exp_w_gate[i], exp_w_up[i], exp_w_down[i],
                        sh_w_gate[i], sh_w_up[i], sh_w_down[i], seq)
    return x2.reshape(bsz, seq, d)
```

```python
import functools
import math

import numpy as np
import jax
import jax.numpy as jnp
from jax import lax
from jax.experimental import pallas as pl
from jax.experimental.pallas import tpu as pltpu
from jax.experimental.pallas import tpu_sc as plsc

F32 = jnp.float32
BF16 = jnp.bfloat16

EPS = 1e-6
S5_GROUP = 16
S5_STATE = 64
HEAD_DIM = 64
ATTN_HEADS = 8
IDX_HEADS = 8
ROPE_HALF = 8
ROPE_THETA = 500000.0
TOPK_MAX = 256
POOL_WINDOWS = (2, 4, 8, 16)
POOL_HALO = 16
N_EXPERT_GROUPS = 8
TOPK_GROUPS = 4
TOP_K = 8
ROUTED_SCALE = 2.5
S5_CHUNK = 16
NEG_BIG = -1e30
INT_MIN = -2147483648
VMEM_LIMIT = 56 * 1024 * 1024


def _cparams(sem, vmem=None):
    return pltpu.CompilerParams(dimension_semantics=sem, vmem_limit_bytes=vmem)


def _dot(a, b):
    return jnp.dot(a, b, preferred_element_type=F32)


def _dot_nt(a, b):
    return lax.dot_general(a, b, (((1,), (1,)), ((), ())), preferred_element_type=F32)


def _split_bf16(a):
    hi = a.astype(BF16)
    lo = (a - hi.astype(F32)).astype(BF16)
    return hi, lo


def _dot3(a, b):
    ah, al = _split_bf16(a)
    bh, bl = _split_bf16(b)
    return _dot(ah, bh) + (_dot(ah, bl) + _dot(al, bh))


def _norm_mod(x, g, shift, scale):
    y = x * lax.rsqrt(jnp.mean(x * x, axis=-1, keepdims=True) + EPS)
    return (y * g) * (1.0 + scale) + shift


def _mod_kernel(ct_ref, w_ref, b_ref, o_ref):
    ct = ct_ref[...]
    cs = ct * jax.nn.sigmoid(ct)
    w = w_ref[...]
    rows = [jnp.sum(w * cs[:, b:b + 1], axis=0, keepdims=True) for b in range(ct.shape[1])]
    o_ref[...] = jnp.concatenate(rows, axis=0) + b_ref[...]


def _mod_vectors(c, w, b):
    nl, d, n3 = w.shape
    bsz = c.shape[0]
    tn = 512
    return pl.pallas_call(
        _mod_kernel,
        grid=(nl, n3 // tn),
        in_specs=[pl.BlockSpec((d, bsz), lambda l, j: (0, 0)),
                  pl.BlockSpec((None, d, tn), lambda l, j: (l, 0, j)),
                  pl.BlockSpec((None, 1, tn), lambda l, j: (l, 0, j))],
        out_specs=pl.BlockSpec((None, bsz, tn), lambda l, j: (l, 0, j)),
        out_shape=jax.ShapeDtypeStruct((nl, bsz, n3), F32),
        compiler_params=_cparams(("arbitrary", "arbitrary")),
        name="mod_vectors",
    )(c.T, w, b.reshape(nl, 1, n3))


def _inproj_kernel(x_ref, g_ref, mod_ref, w_ref, pos_ref, inv_ref, qg_ref, kg_ref,
                   u_ref, q_ref, qi_ref, k_ref, v_ref, ki_ref, wi_ref, u_scr, *, d):
    x = x_ref[...]
    mod = mod_ref[...]
    h = _norm_mod(x, g_ref[...], mod[:, :d], mod[:, d:2 * d])
    proj = _dot(h.astype(BF16), w_ref[...])
    tm = x.shape[0]
    width = ATTN_HEADS * HEAD_DIM

    lane = lax.broadcasted_iota(jnp.int32, (1, 128), 1) & (HEAD_DIM - 1)
    ang = pos_ref[...].astype(F32) * inv_ref[...]
    cos, sin = jnp.cos(ang), jnp.sin(ang)
    t_a = jnp.where(lane < 2 * ROPE_HALF, cos, 1.0)
    t_b = jnp.where(lane < ROPE_HALF, -sin, 0.0)
    t_c = jnp.where((lane >= ROPE_HALF) & (lane < 2 * ROPE_HALF), sin, 0.0)

    def rope(a):
        return a * t_a + pltpu.roll(a, 128 - ROPE_HALF, 1) * t_b + pltpu.roll(a, ROPE_HALF, 1) * t_c

    for slab in range(width // 128):
        u_scr[slab] = proj[:, slab * 128:(slab + 1) * 128]
        for j in range(S5_CHUNK):
            piece = u_scr[slab, pl.ds(j, tm // S5_CHUNK, stride=S5_CHUNK), :]
            col = (slab * S5_CHUNK + j) * 128
            u_ref[:, col:col + 128] = piece.astype(u_ref.dtype)
    q = proj[:, width:2 * width]
    qi = proj[:, 2 * width:3 * width]
    small = proj[:, 3 * width:3 * width + 256]
    qsq = q * q
    qg = qg_ref[...]
    lane128 = lax.broadcasted_iota(jnp.int32, (1, 128), 1)
    for pair in range(ATTN_HEADS // 2):
        sl = slice(pair * 128, (pair + 1) * 128)
        r = []
        for half in range(2):
            hh = 2 * pair + half
            ss = jnp.sum(qsq[:, hh * HEAD_DIM:(hh + 1) * HEAD_DIM], axis=-1, keepdims=True)
            r.append(lax.rsqrt(ss * (1.0 / HEAD_DIM) + EPS) * (HEAD_DIM ** -0.5))
        qn = rope(q[:, sl] * qg) * jnp.where(lane128 < HEAD_DIM, r[0], r[1])
        qt = qn.T
        qit = rope(qi[:, sl]).T
        for half in range(2):
            hs = slice(half * HEAD_DIM, (half + 1) * HEAD_DIM)
            q_ref[2 * pair + half] = qt[hs].astype(q_ref.dtype)
            qi_ref[2 * pair + half] = qit[hs].astype(qi_ref.dtype)
    kv = small[:, :128]
    k = kv[:, :HEAD_DIM]
    rk = lax.rsqrt(jnp.mean(k * k, axis=-1, keepdims=True) + EPS)
    kr = rope(kv * kg_ref[...])
    k_ref[...] = (kr[:, :HEAD_DIM] * rk).astype(k_ref.dtype)
    v_ref[...] = kv[:, HEAD_DIM:].astype(v_ref.dtype)
    kiw = small[:, 128:256]
    ki_ref[...] = rope(kiw)[:, :HEAD_DIM].astype(ki_ref.dtype)
    wscale = (IDX_HEADS ** -0.5) * (HEAD_DIM ** -0.5)
    wi_ref[...] = pltpu.roll(kiw, HEAD_DIM, 1) * wscale


def _in_projection(x2, g, mod, layer, w_in, positions, q_norm_g, k_norm_g, bsz, seq):
    t, d = x2.shape
    tm = 512
    width = ATTN_HEADS * HEAD_DIM
    cuts = np.cumsum([width, width, HEAD_DIM, HEAD_DIM, width, HEAD_DIM, IDX_HEADS])
    u_w, q_w, k_w, v_w, qi_w, ki_w, wi_w = jnp.split(w_in, cuts[:-1].tolist(), axis=1)
    pad = jnp.zeros((d, 128 - HEAD_DIM - IDX_HEADS), w_in.dtype)
    w = jnp.concatenate([u_w, q_w, qi_w, k_w, v_w, ki_w, wi_w, pad], axis=1).astype(BF16)
    nw = w.shape[1]
    lane = np.arange(128) % HEAD_DIM
    inv = np.where(lane < 2 * ROPE_HALF,
                   np.power(ROPE_THETA, -2.0 * (lane % ROPE_HALF) / (2 * ROPE_HALF)), 0.0)
    inv = jnp.asarray(inv.reshape(1, 128), F32)
    qg2 = jnp.tile(q_norm_g.reshape(1, HEAD_DIM), (1, 2))
    kg2 = jnp.concatenate([k_norm_g.reshape(1, HEAD_DIM), jnp.ones((1, HEAD_DIM), F32)], axis=1)
    tiles_per_seq = seq // tm
    row = lambda i: (i, 0)
    const = lambda i: (0, 0)
    hm = lambda i: (i // tiles_per_seq, 0, 0, i % tiles_per_seq)
    outs = pl.pallas_call(
        functools.partial(_inproj_kernel, d=d),
        grid=(t // tm,),
        in_specs=[pl.BlockSpec((tm, d), row),
                  pl.BlockSpec((1, d), const),
                  pl.BlockSpec((None, None, 1, 3 * d), lambda i: (layer, i // tiles_per_seq, 0, 0)),
                  pl.BlockSpec((d, nw), const),
                  pl.BlockSpec((tm, 1), row),
                  pl.BlockSpec((1, 128), const),
                  pl.BlockSpec((1, 128), const),
                  pl.BlockSpec((1, 128), const)],
        out_specs=[pl.BlockSpec((tm // S5_CHUNK, width * S5_CHUNK), row),
                   pl.BlockSpec((None, ATTN_HEADS, HEAD_DIM, tm), hm),
                   pl.BlockSpec((None, IDX_HEADS, HEAD_DIM, tm), hm),
                   pl.BlockSpec((tm, HEAD_DIM), row),
                   pl.BlockSpec((tm, HEAD_DIM), row),
                   pl.BlockSpec((tm, HEAD_DIM), row),
                   pl.BlockSpec((tm, 128), row)],
        out_shape=[jax.ShapeDtypeStruct((t // S5_CHUNK, width * S5_CHUNK), BF16),
                   jax.ShapeDtypeStruct((bsz, ATTN_HEADS, HEAD_DIM, seq), BF16),
                   jax.ShapeDtypeStruct((bsz, IDX_HEADS, HEAD_DIM, seq), BF16),
                   jax.ShapeDtypeStruct((t, HEAD_DIM), BF16),
                   jax.ShapeDtypeStruct((t, HEAD_DIM), BF16),
                   jax.ShapeDtypeStruct((t, HEAD_DIM), BF16),
                   jax.ShapeDtypeStruct((t, 128), F32)],
        scratch_shapes=[pltpu.VMEM((width // 128, tm, 128), F32)],
        compiler_params=_cparams(("arbitrary",), VMEM_LIMIT),
        name="in_projection",
    )(x2, g.reshape(1, d), mod, w, positions.reshape(t, 1), inv, qg2, kg2)
    return outs


def _s5_weights(lam_re, lam_im, log_dt, b_re, b_im, c_re, c_im, d_skip, glu_w, glu_b):
    L = S5_CHUNK
    g, p = lam_re.shape
    hch = S5_GROUP
    dt = jnp.exp(log_dt)[:, None]
    lr, li = lam_re, lam_im
    tau = jnp.arange(L + 1, dtype=F32)[:, None, None]
    mag = jnp.exp(lr * dt * tau)
    pw_r, pw_i = mag * jnp.cos(li * dt * tau), mag * jnp.sin(li * dt * tau)
    ar, ai = pw_r[1], pw_i[1]
    nr, ni = ar - 1.0, ai
    den = lr * lr + li * li
    cr, ci = (nr * lr + ni * li) / den, (ni * lr - nr * li) / den
    bb_r = cr[..., None] * b_re - ci[..., None] * b_im
    bb_i = cr[..., None] * b_im + ci[..., None] * b_re
    cl_r = c_re[None] * pw_r[:L, :, None, :] - c_im[None] * pw_i[:L, :, None, :]
    cl_i = c_re[None] * pw_i[:L, :, None, :] + c_im[None] * pw_r[:L, :, None, :]
    taps = (jnp.einsum('tghp,gpk->tghk', cl_r, bb_r) - jnp.einsum('tghp,gpk->tghk', cl_i, bb_i))
    ns = g // 8
    eye = jnp.eye(8, dtype=F32)
    kd = jnp.einsum('tsgoh,gf->tsghfo', taps.reshape(L, ns, 8, hch, hch), eye).reshape(L, ns, 128, 128)
    zero = jnp.zeros_like(kd[0])
    k2 = jnp.stack([jnp.concatenate([jnp.concatenate([kd[2 * dd], kd[2 * dd + 1]], axis=-1),
                                     jnp.concatenate([kd[2 * dd - 1] if dd else zero, kd[2 * dd]], axis=-1)], axis=-2)
                    for dd in range(L // 2)], axis=1).astype(BF16)
    ii = jnp.arange(L)
    rev_r, rev_i = pw_r[L - 1 - ii], pw_i[L - 1 - ii]
    z_r = rev_r[..., None] * bb_r[None] - rev_i[..., None] * bb_i[None]
    z_i = rev_r[..., None] * bb_i[None] + rev_i[..., None] * bb_r[None]
    zc = jnp.concatenate([z_r, z_i], axis=2)
    w_z = jnp.einsum('jsgph,gf->sjghfp', zc.reshape(L, ns, 8, 2 * p, hch), eye)
    w_z = w_z.reshape(ns, L // 2, 2 * 128, 8 * 2 * p).astype(BF16)
    co_r = c_re[None] * pw_r[1:, :, None, :] - c_im[None] * pw_i[1:, :, None, :]
    co_i = c_re[None] * pw_i[1:, :, None, :] + c_im[None] * pw_r[1:, :, None, :]
    cc = jnp.concatenate([co_r, -co_i], axis=-1)
    w_c = jnp.einsum('isghp,gf->sgpifh', cc.reshape(L, ns, 8, hch, 2 * p), eye).reshape(ns, 8 * 2 * p, L * 128)
    w_ch = w_c.astype(BF16)
    w_cl = (w_c - w_ch.astype(F32)).astype(BF16)
    al_r, al_i = pw_r[L], pw_i[L]
    dec_a = jnp.concatenate([al_r, al_r], axis=-1).reshape(ns, 1, 8 * 2 * p)
    dec_b = jnp.concatenate([-al_i, al_i], axis=-1).reshape(ns, 1, 8 * 2 * p)
    d_t = jnp.tile(d_skip.reshape(ns, 1, 128), (1, 1, L))
    gl = jnp.einsum('sghk,gf->sghfk', glu_w.reshape(ns, 8, hch, hch), eye).reshape(ns, 128, 128)
    glu2 = jnp.einsum('ab,shk->sahbk', jnp.eye(2, dtype=F32), gl).reshape(ns, 256, 256).astype(BF16)
    glu_bt = jnp.tile(glu_b.reshape(ns, 1, 128), (1, 1, L))
    return k2, w_z, w_ch, w_cl, dec_a, dec_b, d_t, glu2, glu_bt


def _s5_kernel(u_ref, k2_ref, wz_ref, wch_ref, wcl_ref, da_ref, db_ref, d_ref, g2_ref, gb_ref, o_ref,
               zp_ref, zq_ref, sp_ref, *, nchunk):
    nblk = S5_CHUNK // 2
    ub = [u_ref[:, j * 256:(j + 1) * 256] for j in range(nblk)]
    z = _dot(ub[0], wz_ref[0])
    for j in range(1, nblk):
        z = z + _dot(ub[j], wz_ref[j])
    zp_ref[...] = z
    half = S5_STATE
    zq_ref[...] = jnp.concatenate([pltpu.roll(z[:, s * 2 * half:(s + 1) * 2 * half], half, 1)
                                   for s in range(z.shape[1] // (2 * half))], axis=1)
    da = da_ref[...]
    db = db_ref[...]
    dbq = -db

    def step(c, carry):
        sp, sq = carry
        sp_ref[pl.ds(c, 1), :] = sp
        sp_new = da * sp + db * sq + zp_ref[pl.ds(c, 1), :]
        sq_new = da * sq + dbq * sp + zq_ref[pl.ds(c, 1), :]
        return sp_new, sq_new

    zero = jnp.zeros((1, z.shape[1]), F32)
    lax.fori_loop(0, nchunk, step, (zero, zero))
    sh, sl = _split_bf16(sp_ref[...])
    wch = wch_ref[...]
    y_inter = _dot(sh, wch) + (_dot(sh, wcl_ref[...]) + _dot(sl, wch))
    for i in range(nblk):
        cs = slice(i * 256, (i + 1) * 256)
        acc = y_inter[:, cs] + d_ref[:, cs] * ub[i].astype(F32)
        for j in range(i + 1):
            acc = acc + _dot(ub[j], k2_ref[i - j])
        y = jax.nn.gelu(acc)
        y = y * jax.nn.sigmoid(_dot(y.astype(BF16), g2_ref[...]) + gb_ref[:, cs])
        for a in range(2):
            o_ref[pl.ds(2 * i + a, nchunk, stride=S5_CHUNK), :] = y[:, a * 128:(a + 1) * 128]


def _s5_mixer(u2, weights, bsz, seq):
    k2, w_z, w_ch, w_cl, dec_a, dec_b, d_t, glu2, glu_bt = weights
    ns = k2.shape[0]
    nchunk = seq // S5_CHUNK
    cols = S5_CHUNK * 128
    st = w_z.shape[-1]
    slab3 = lambda b, s: (s, 0, 0)
    slab4 = lambda b, s: (s, 0, 0, 0)
    return pl.pallas_call(
        functools.partial(_s5_kernel, nchunk=nchunk),
        grid=(bsz, ns),
        in_specs=[pl.BlockSpec((nchunk, cols), lambda b, s: (b, s)),
                  pl.BlockSpec((None,) + k2.shape[1:], slab4),
                  pl.BlockSpec((None,) + w_z.shape[1:], slab4),
                  pl.BlockSpec((None, st, cols), slab3),
                  pl.BlockSpec((None, st, cols), slab3),
                  pl.BlockSpec((None, 1, st), slab3),
                  pl.BlockSpec((None, 1, st), slab3),
                  pl.BlockSpec((None, 1, cols), slab3),
                  pl.BlockSpec((None, 256, 256), slab3),
                  pl.BlockSpec((None, 1, cols), slab3)],
        out_specs=pl.BlockSpec((seq, 128), lambda b, s: (b, s)),
        out_shape=jax.ShapeDtypeStruct((bsz * seq, ns * 128), F32),
        scratch_shapes=[pltpu.VMEM((nchunk, st), F32), pltpu.VMEM((nchunk, st), F32), pltpu.VMEM((nchunk, st), F32)],
        compiler_params=_cparams(("arbitrary", "arbitrary"), VMEM_LIMIT),
        name="s5_mixer",
    )(u2, k2, w_z, w_ch, w_cl, dec_a, dec_b, d_t, glu2, glu_bt)


def _dsa_kernel(qt_ref, qit_ref, wt_ref, k_ref, vt_ref, ki_ref, o_ref,
                sc_ref, s_ref, m_ref, l_ref, acc_ref, *, tq, ck, k_sel, idx_bits, max_iter):
    i = pl.program_id(1)
    nheads = qt_ref.shape[0]
    n_ck = (i * tq + tq + ck - 1) // ck
    t_pos = i * tq + lax.broadcasted_iota(jnp.int32, (1, tq), 1)
    krow = lax.broadcasted_iota(jnp.int32, (ck, 1), 0)
    kf = float(k_sel)
    inf = jnp.inf

    fr = 32
    sub = min(32768 // tq, ck)

    def fold(x, op):
        return op(x.reshape(ck // fr, fr, tq), axis=0)

    wt = wt_ref[...]

    def score_chunk(c, carry):
        lo8, hi8 = carry
        for part in range(ck // sub):
            off = pl.multiple_of(c * ck + part * sub, sub)
            kic = ki_ref[pl.ds(off, sub), :]
            acc = jnp.zeros((sub, tq), F32)
            for h in range(nheads):
                acc = acc + wt[h:h + 1, :] * jnp.maximum(_dot(kic, qit_ref[h]), 0.0)
            acc = acc + 0.0
            vis = off + krow[:sub] <= t_pos
            sc_ref[c, part * sub:(part + 1) * sub, :] = jnp.where(vis, acc, -inf)
            hi8 = jnp.maximum(hi8, jnp.max(jnp.where(vis, acc, -inf).reshape(sub // fr, fr, tq), axis=0))
            lo8 = jnp.minimum(lo8, jnp.min(jnp.where(vis, acc, inf).reshape(sub // fr, fr, tq), axis=0))
        return lo8, hi8

    lo8, hi8 = lax.fori_loop(0, n_ck, score_chunk, (jnp.full((fr, tq), inf, F32), jnp.full((fr, tq), -inf, F32)))
    lo = jnp.min(lo8, axis=0, keepdims=True)
    hi = jnp.max(hi8, axis=0, keepdims=True)
    n_vis = (t_pos + 1).astype(F32)
    few = n_vis < kf

    def reduce_keys(fn, init, combine, op):
        def body(c, part):
            return combine(part, fold(fn(sc_ref[c], c * ck), op))
        return op(lax.fori_loop(0, n_ck, body, jnp.full((fr, tq), init, F32)), axis=0, keepdims=True)

    def count(pred):
        return reduce_keys(lambda x, off: jnp.where(pred(x, off), 1.0, 0.0), 0.0, jnp.add, jnp.sum)

    def bisect(lo_, hi_, c_lo):
        mid = lo_ + (hi_ - lo_) * 0.5
        cnt = count(lambda x, off: x >= mid)
        up = cnt >= kf
        return jnp.where(up, mid, lo_), jnp.where(up, hi_, mid), jnp.where(up, cnt, c_lo)

    def status(lo_, hi_, c_lo):
        mid = lo_ + (hi_ - lo_) * 0.5
        unresolved = jnp.where(few, 0.0, jnp.where(c_lo == kf, 0.0, 1.0))
        movable = jnp.where(mid > lo_, jnp.where(mid < hi_, 1.0, 0.0), 0.0)
        return jnp.max(unresolved), jnp.max(unresolved * movable)

    def search_body(st):
        it, lo_, hi_, c_lo, _, _ = st
        for _ in range(2):
            lo_, hi_, c_lo = bisect(lo_, hi_, c_lo)
        return (it + 2, lo_, hi_, c_lo) + status(lo_, hi_, c_lo)

    st = lax.while_loop(lambda st: jnp.logical_and(st[0] < max_iter, st[5] > 0.5), search_body,
                        (jnp.int32(0), lo, hi, n_vis) + status(lo, hi, n_vis))
    lo, open_ = st[1], st[4]

    def resolve():
        def walk(thr_, strict):
            keep = (lambda x: x > thr_) if strict else (lambda x: x >= thr_)
            return reduce_keys(lambda x, off: jnp.where(keep(x), x, inf), inf, jnp.minimum, jnp.min)

        def above_equal(thr_):
            return count(lambda x, off: x > thr_), count(lambda x, off: x == thr_)

        def too_low(n_gt):
            return jnp.where(few, 0.0, jnp.where(n_gt >= kf, 1.0, 0.0))

        thr0 = walk(lo, False)

        def walk_body(st):
            thr_, n_gt, _ = st
            thr2 = jnp.where(too_low(n_gt) > 0.5, walk(thr_, True), thr_)
            return (thr2,) + above_equal(thr2)

        thr_, n_gt, n_eq = lax.while_loop(lambda st: jnp.max(too_low(st[1])) > 0.5, walk_body,
                                          (thr0,) + above_equal(thr0))
        need = kf - n_gt

        def idx_step(b, cur):
            trial = cur | (jnp.int32(1) << (idx_bits - 1 - b))
            cnt = reduce_keys(lambda x, off: jnp.where(x == thr_, jnp.where(off + krow < trial, 1.0, 0.0), 0.0),
                              0.0, jnp.add, jnp.sum)
            return jnp.where(cnt < need, trial, cur)

        return thr_, lax.fori_loop(0, idx_bits, idx_step, jnp.zeros((1, tq), jnp.int32))

    thr, cut = lax.cond(open_ > 0.5, resolve, lambda: (lo, jnp.full((1, tq), 0x7FFFFFFF, jnp.int32)))

    def bias_chunk(c, _):
        x = sc_ref[c]
        at_thr = jnp.where(x == thr, jnp.where(c * ck + krow <= cut, 0.0, NEG_BIG), NEG_BIG)
        sc_ref[c] = jnp.where(x > thr, 0.0, at_thr)
        return 0

    lax.fori_loop(0, n_ck, bias_chunk, 0)

    for h in range(nheads):
        m_ref[h] = jnp.full((fr, tq), NEG_BIG, F32)
        l_ref[h] = jnp.zeros((fr, tq), F32)
        acc_ref[h] = jnp.zeros((HEAD_DIM, tq), F32)

    def attn_scores(c, _):
        off = pl.multiple_of(c * ck, ck)
        kc = k_ref[pl.ds(off, ck), :]
        bias = sc_ref[c]
        for h in range(nheads):
            s = _dot(kc, qt_ref[h]) + bias
            s_ref[h, c] = s
            m_ref[h] = jnp.maximum(m_ref[h], fold(s, jnp.max))
        return 0

    lax.fori_loop(0, n_ck, attn_scores, 0)
    for h in range(nheads):
        m_ref[h] = jnp.broadcast_to(jnp.max(m_ref[h], axis=0, keepdims=True), (fr, tq))

    def attn_values(c, _):
        vt = vt_ref[c]
        for h in range(nheads):
            p = jnp.exp(s_ref[h, c] - m_ref[h][:1, :])
            l_ref[h] += fold(p, jnp.sum)
            acc_ref[h] += _dot(vt, p.astype(BF16))
        return 0

    lax.fori_loop(0, n_ck, attn_values, 0)
    for h in range(nheads):
        l_row = jnp.sum(l_ref[h], axis=0, keepdims=True)
        o_ref[h * HEAD_DIM:(h + 1) * HEAD_DIM, :] = (acc_ref[h] / l_row).astype(o_ref.dtype)


def _dsa_attention(qt, qit, wi, k, v, ki, bsz, seq):
    tq = 256
    ck = min(512, seq)
    n_ck = seq // ck
    k_sel = min(TOPK_MAX, seq // 4)
    nq = seq // tq
    nh = ATTN_HEADS
    width = nh * HEAD_DIM
    vt = v.reshape(bsz, n_ck, ck, HEAD_DIM).transpose(0, 1, 3, 2)
    wt = wi[:, :IDX_HEADS].T
    qspec = pl.BlockSpec((None, nh, HEAD_DIM, tq), lambda b, i: (b, 0, 0, i))
    out_t = pl.pallas_call(
        functools.partial(_dsa_kernel, tq=tq, ck=ck, k_sel=k_sel, idx_bits=(seq - 1).bit_length(), max_iter=48),
        grid=(bsz, nq),
        in_specs=[qspec, qspec,
                  pl.BlockSpec((IDX_HEADS, tq), lambda b, i: (0, b * nq + i)),
                  pl.BlockSpec((seq, HEAD_DIM), lambda b, i: (b, 0)),
                  pl.BlockSpec((None, n_ck, HEAD_DIM, ck), lambda b, i: (b, 0, 0, 0)),
                  pl.BlockSpec((seq, HEAD_DIM), lambda b, i: (b, 0))],
        out_specs=pl.BlockSpec((None, width, tq), lambda b, i: (b, 0, i)),
        out_shape=jax.ShapeDtypeStruct((bsz, width, seq), BF16),
        scratch_shapes=[pltpu.VMEM((n_ck, ck, tq), F32),
                        pltpu.VMEM((nh, n_ck, ck, tq), F32),
                        pltpu.VMEM((nh, 32, tq), F32),
                        pltpu.VMEM((nh, 32, tq), F32),
                        pltpu.VMEM((nh, HEAD_DIM, tq), F32)],
        compiler_params=_cparams(("arbitrary", "arbitrary"), VMEM_LIMIT),
        name="dsa_attention",
    )(qt, qit, wt, k, vt, ki)
    return out_t.transpose(0, 2, 1).reshape(bsz * seq, width)


def _dsa_kernel_rows(q_ref, qi_ref, wi_ref, kt_ref, v_ref, kit_ref, o_ref,
                     sc_ref, s_ref, wb_ref, m_ref, l_ref, acc_ref, *, tq, ck, k_sel, idx_bits, max_iter):
    i = pl.program_id(1)
    nheads = q_ref.shape[0]
    nj = ck // 128
    n_ck = (i * tq + tq + ck - 1) // ck
    shape = (tq, 128)
    t_pos = i * tq + lax.broadcasted_iota(jnp.int32, shape, 0)
    lane = lax.broadcasted_iota(jnp.int32, shape, 1)
    kf = float(k_sel)
    inf = jnp.inf

    def rep(col):
        return jnp.broadcast_to(col, shape)

    def cols(x):
        return [x[:, j * 128:(j + 1) * 128] for j in range(nj)]

    w = wi_ref[...]
    for h in range(nheads):
        wb_ref[h] = rep(w[:, h:h + 1])

    def score_chunk(c, carry):
        lo, hi = carry
        kit = kit_ref[c]
        accs = [jnp.zeros(shape, F32) for _ in range(nj)]
        for h in range(nheads):
            s = jnp.maximum(_dot(qi_ref[h], kit), 0.0)
            wb = wb_ref[h]
            accs = [a + wb * sj for a, sj in zip(accs, cols(s))]
        for j in range(nj):
            a = accs[j] + 0.0
            vis = (c * ck + j * 128) + lane <= t_pos
            sc_ref[c, :, j * 128:(j + 1) * 128] = jnp.where(vis, a, -inf)
            hi = jnp.maximum(hi, jnp.where(vis, a, -inf))
            lo = jnp.minimum(lo, jnp.where(vis, a, inf))
        return lo, hi

    lo, hi = lax.fori_loop(0, n_ck, score_chunk, (jnp.full(shape, inf, F32), jnp.full(shape, -inf, F32)))
    lo = rep(jnp.min(lo, axis=1, keepdims=True))
    hi = rep(jnp.max(hi, axis=1, keepdims=True))
    few = (t_pos + 1).astype(F32) < kf

    def reduce_rows(fn, init, combine, final):
        def body(c, part):
            for xj in cols(sc_ref[c]):
                part = combine(part, fn(xj))
            return part
        return rep(final(lax.fori_loop(0, n_ck, body, jnp.full(shape, init, F32)), axis=1, keepdims=True))

    def count(pred):
        return reduce_rows(lambda xj: jnp.where(pred(xj), 1.0, 0.0), 0.0, jnp.add, jnp.sum)

    def pivot(it, lo_, hi_, c_lo, c_hi):
        frac = jnp.clip((c_lo - (kf + 0.5)) / (c_lo - c_hi), 0.0625, 0.9375)
        frac = jnp.where(it % 2 == 0, 0.5, frac)
        mid = lo_ + (hi_ - lo_) * frac
        half = lo_ + (hi_ - lo_) * 0.5
        return jnp.where(mid > lo_, jnp.where(mid < hi_, mid, half), half)

    def pending(lo_, hi_, mid, c_lo):
        unresolved = jnp.where(few, 0.0, jnp.where(c_lo == kf, 0.0, 1.0))
        movable = jnp.where(mid > lo_, jnp.where(mid < hi_, 1.0, 0.0), 0.0)
        return jnp.max(unresolved), jnp.max(unresolved * movable)

    def search_cond(st):
        return jnp.logical_and(st[0] < max_iter, st[-1] > 0.5)

    def search_body(st):
        it, lo_, hi_, c_lo, c_hi, mid, _, _ = st
        cnt = count(lambda xj: xj >= mid)
        up = cnt >= kf
        lo2 = jnp.where(up, mid, lo_)
        hi2 = jnp.where(up, hi_, mid)
        c_lo2 = jnp.where(up, cnt, c_lo)
        c_hi2 = jnp.where(up, c_hi, cnt)
        mid2 = pivot(it + 1, lo2, hi2, c_lo2, c_hi2)
        open_, go = pending(lo2, hi2, mid2, c_lo2)
        return it + 1, lo2, hi2, c_lo2, c_hi2, mid2, open_, go

    c_lo0 = (t_pos + 1).astype(F32)
    c_hi0 = jnp.zeros(shape, F32)
    mid0 = pivot(jnp.int32(0), lo, hi, c_lo0, c_hi0)
    open0, go0 = pending(lo, hi, mid0, c_lo0)
    st = lax.while_loop(search_cond, search_body, (jnp.int32(0), lo, hi, c_lo0, c_hi0, mid0, open0, go0))
    lo, open_ = st[1], st[6]

    def resolve():
        def walk(thr_, strict):
            keep = (lambda xj: xj > thr_) if strict else (lambda xj: xj >= thr_)
            return reduce_rows(lambda xj: jnp.where(keep(xj), xj, inf), inf, jnp.minimum, jnp.min)

        def above_equal(thr_):
            return count(lambda xj: xj > thr_), count(lambda xj: xj == thr_)

        def too_low(n_gt):
            return jnp.where(few, 0.0, jnp.where(n_gt >= kf, 1.0, 0.0))

        thr0 = walk(lo, False)
        gt0, eq0 = above_equal(thr0)

        def walk_body(st):
            thr_, n_gt, _ = st
            thr2 = jnp.where(too_low(n_gt) > 0.5, walk(thr_, True), thr_)
            return (thr2,) + above_equal(thr2)

        thr_, n_gt, n_eq = lax.while_loop(lambda st: jnp.max(too_low(st[1])) > 0.5, walk_body, (thr0, gt0, eq0))
        need = kf - n_gt

        def idx_step(b, cur):
            trial = cur | (jnp.int32(1) << (idx_bits - 1 - b))

            def body(c, part):
                for j, xj in enumerate(cols(sc_ref[c])):
                    idx = (c * ck + j * 128) + lane
                    part = part + jnp.where(xj == thr_, jnp.where(idx < trial, 1.0, 0.0), 0.0)
                return part
            cnt = rep(jnp.sum(lax.fori_loop(0, n_ck, body, jnp.zeros(shape, F32)), axis=1, keepdims=True))
            return jnp.where(cnt < need, trial, cur)

        cut_ = lax.fori_loop(0, idx_bits, idx_step, jnp.zeros(shape, jnp.int32))
        return thr_, cut_

    thr, cut = lax.cond(open_ > 0.5, resolve, lambda: (lo, jnp.full(shape, 0x7FFFFFFF, jnp.int32)))

    def bias_chunk(c, _):
        x = sc_ref[c]
        for j, xj in enumerate(cols(x)):
            idx = (c * ck + j * 128) + lane
            at_thr = jnp.where(xj == thr, jnp.where(idx <= cut, 0.0, NEG_BIG), NEG_BIG)
            sc_ref[c, :, j * 128:(j + 1) * 128] = jnp.where(xj > thr, 0.0, at_thr)
        return 0

    lax.fori_loop(0, n_ck, bias_chunk, 0)

    for h in range(nheads):
        m_ref[h] = jnp.full(shape, NEG_BIG, F32)
        l_ref[h] = jnp.zeros(shape, F32)
        acc_ref[h] = jnp.zeros((tq, HEAD_DIM), F32)

    def attn_scores(c, _):
        kt = kt_ref[c]
        bias = cols(sc_ref[c])
        for h in range(nheads):
            s = cols(_dot(q_ref[h], kt))
            part = m_ref[h]
            for j in range(nj):
                sj = s[j] + bias[j]
                s_ref[h, c, :, j * 128:(j + 1) * 128] = sj
                part = jnp.maximum(part, sj)
            m_ref[h] = part
        return 0

    lax.fori_loop(0, n_ck, attn_scores, 0)
    for h in range(nheads):
        m_ref[h] = rep(jnp.max(m_ref[h], axis=1, keepdims=True))

    def attn_values(c, _):
        off = pl.multiple_of(c * ck, ck)
        vc = v_ref[pl.ds(off, ck), :]
        for h in range(nheads):
            m = m_ref[h]
            p = [jnp.exp(sj - m) for sj in cols(s_ref[h, c])]
            l_ref[h] += functools.reduce(jnp.add, p)
            acc_ref[h] += _dot(jnp.concatenate(p, axis=1).astype(BF16), vc)
        return 0

    lax.fori_loop(0, n_ck, attn_values, 0)
    for h in range(nheads):
        l_row = jnp.sum(l_ref[h], axis=1, keepdims=True)
        o_ref[:, h * HEAD_DIM:(h + 1) * HEAD_DIM] = (acc_ref[h] / l_row).astype(o_ref.dtype)


def _dsa_attention_rows(q_hm, qi_hm, wi, k, v, ki, bsz, seq):
    tq = 128
    ck = min(512, seq)
    n_ck = seq // ck
    k_sel = min(TOPK_MAX, seq // 4)
    nq = seq // tq
    width = ATTN_HEADS * HEAD_DIM
    kt = k.reshape(bsz, n_ck, ck, HEAD_DIM).transpose(0, 1, 3, 2)
    kit = ki.reshape(bsz, n_ck, ck, HEAD_DIM).transpose(0, 1, 3, 2)
    hm = lambda b, i: (b, 0, i, 0)
    per_batch = lambda b, i: (b, 0, 0, 0)
    nh = ATTN_HEADS
    return pl.pallas_call(
        functools.partial(_dsa_kernel_rows, tq=tq, ck=ck, k_sel=k_sel, idx_bits=(seq - 1).bit_length(), max_iter=48),
        grid=(bsz, nq),
        in_specs=[pl.BlockSpec((None, ATTN_HEADS, tq, HEAD_DIM), hm),
                  pl.BlockSpec((None, IDX_HEADS, tq, HEAD_DIM), hm),
                  pl.BlockSpec((tq, 128), lambda b, i: (b * nq + i, 0)),
                  pl.BlockSpec((None, n_ck, HEAD_DIM, ck), per_batch),
                  pl.BlockSpec((seq, HEAD_DIM), lambda b, i: (b, 0)),
                  pl.BlockSpec((None, n_ck, HEAD_DIM, ck), per_batch)],
        out_specs=pl.BlockSpec((tq, width), lambda b, i: (b * nq + i, 0)),
        out_shape=jax.ShapeDtypeStruct((bsz * seq, width), BF16),
        scratch_shapes=[pltpu.VMEM((n_ck, tq, ck), F32),
                        pltpu.VMEM((nh, n_ck, tq, ck), F32),
                        pltpu.VMEM((IDX_HEADS, tq, 128), F32),
                        pltpu.VMEM((nh, tq, 128), F32),
                        pltpu.VMEM((nh, tq, 128), F32),
                        pltpu.VMEM((nh, tq, HEAD_DIM), F32)],
        compiler_params=_cparams(("arbitrary", "arbitrary"), VMEM_LIMIT),
        name="dsa_attention",
    )(q_hm, qi_hm, wi, kt, v, kit)


def _outproj_kernel(x_ref, ys_ref, ya_ref, w_ref, mod_ref, o_ref, *, d):
    width = ys_ref.shape[1]
    y = _dot(ys_ref[...].astype(BF16), w_ref[:width, :]) + _dot(ya_ref[...], w_ref[width:, :])
    o_ref[...] = x_ref[...] + mod_ref[...][:, 2 * d:] * y


def _out_projection(x2, y_ssm, y_att, w_out, mod, layer, seq):
    t, d = x2.shape
    tm = 512
    width = y_ssm.shape[1]
    tiles_per_seq = seq // tm
    row = lambda i: (i, 0)
    return pl.pallas_call(
        functools.partial(_outproj_kernel, d=d),
        grid=(t // tm,),
        in_specs=[pl.BlockSpec((tm, d), row),
                  pl.BlockSpec((tm, width), row),
                  pl.BlockSpec((tm, width), row),
                  pl.BlockSpec((2 * width, d), lambda i: (0, 0)),
                  pl.BlockSpec((None, None, 1, 3 * d), lambda i: (layer, i // tiles_per_seq, 0, 0))],
        out_specs=pl.BlockSpec((tm, d), row),
        out_shape=jax.ShapeDtypeStruct((t, d), F32),
        compiler_params=_cparams(("arbitrary",)),
        name="out_projection",
    )(x2, y_ssm, y_att, w_out.astype(BF16), mod)


def _pool_kernel(x_ref, halo_ref, g_ref, mod_ref, pw_ref, ps_ref, o_ref, h_ref, *, d, tiles_per_seq):
    i = pl.program_id(0)
    tm = x_ref.shape[0]
    mod = mod_ref[...]
    shift, scale, gate = mod[:, :d], mod[:, d:2 * d], mod[:, 2 * d:]
    x = x_ref[...]
    first = (i % tiles_per_seq) == 0
    h_ref[POOL_HALO:, :] = _norm_mod(x, g_ref[...], shift, scale)
    halo = _norm_mod(halo_ref[...], g_ref[...], shift, scale)
    h_ref[:POOL_HALO, :] = jnp.where(first, 0.0, halo)
    pos = (i % tiles_per_seq) * tm + lax.broadcasted_iota(jnp.int32, (tm, 1), 0)
    gw = d // len(POOL_WINDOWS)
    ys = []
    for gi, win in enumerate(POOL_WINDOWS):
        cs = slice(gi * gw, (gi + 1) * gw)
        cur = h_ref[POOL_HALO:, cs]
        tot = cur
        for k in range(1, win):
            tot = tot + h_ref[POOL_HALO - k:POOL_HALO - k + tm, cs]
        cnt = jnp.minimum(pos + 1, win).astype(F32)
        pooled = tot / cnt - cur
        ys.append(_dot(pooled.astype(BF16), pw_ref[gi]))
    y = jnp.concatenate(ys, axis=1) * ps_ref[...]
    o_ref[...] = x + gate * y


def _pool_layer(x2, g, mod, layer, pool_w, pool_scale, seq):
    t, d = x2.shape
    tm = 512
    tiles_per_seq = seq // tm
    gw = d // len(POOL_WINDOWS)
    hb = tm // POOL_HALO
    return pl.pallas_call(
        functools.partial(_pool_kernel, d=d, tiles_per_seq=tiles_per_seq),
        grid=(t // tm,),
        in_specs=[pl.BlockSpec((tm, d), lambda i: (i, 0)),
                  pl.BlockSpec((POOL_HALO, d), lambda i: (jnp.maximum(i * hb - 1, 0), 0)),
                  pl.BlockSpec((1, d), lambda i: (0, 0)),
                  pl.BlockSpec((None, None, 1, 3 * d), lambda i: (layer, i // tiles_per_seq, 0, 0)),
                  pl.BlockSpec((len(POOL_WINDOWS), gw, gw), lambda i: (0, 0, 0)),
                  pl.BlockSpec((1, d), lambda i: (0, 0))],
        out_specs=pl.BlockSpec((tm, d), lambda i: (i, 0)),
        out_shape=jax.ShapeDtypeStruct((t, d), F32),
        scratch_shapes=[pltpu.VMEM((tm + POOL_HALO, d), F32)],
        compiler_params=_cparams(("arbitrary",)),
        name="pool_mixer",
    )(x2, x2, g.reshape(1, d), mod, pool_w.astype(BF16), pool_scale.reshape(1, d))


def _first_max(vals, idx, big):
    m = jnp.max(vals, axis=0, keepdims=True)
    first = jnp.min(jnp.where(vals == m, idx, big), axis=0, keepdims=True)
    return m, first


def _pack_bf16_pairs(x):
    n = x.shape[1] // 2
    lo = pltpu.bitcast(x[:, :n].astype(BF16).astype(F32), jnp.int32)
    hi = pltpu.bitcast(x[:, n:].astype(BF16).astype(F32), jnp.int32)
    return hi | ((lo >> 16) & 0xFFFF)


def _unpack_bf16_pairs(w):
    lo = pltpu.bitcast(w << 16, F32)
    hi = pltpu.bitcast(w & jnp.int32(-65536), F32)
    return jnp.concatenate([lo, hi], axis=1)


def _router_kernel(x_ref, g_ref, mod_ref, rw_ref, rb_ref, tri_ref, h_ref, gate_ref, rank_ref, cnt_ref,
                   run_ref, *, d, n_exp):
    mod = mod_ref[...]
    h = _norm_mod(x_ref[...], g_ref[...], mod[:, :d], mod[:, d:2 * d])
    h_ref[...] = _pack_bf16_pairs(h)
    tm = h.shape[0]
    hh, hl = _split_bf16(h)
    rw = rw_ref[...]
    rh, rl = _split_bf16(rw)
    logits = _dot_nt(rh, hh) + (_dot_nt(rh, hl) + _dot_nt(rl, hh))
    scores = jax.nn.sigmoid(logits)
    sel = scores + rb_ref[...]
    gsz = n_exp // N_EXPERT_GROUPS
    neg = -jnp.inf
    shape3 = (N_EXPERT_GROUPS, gsz, tm)
    sel3 = sel.reshape(shape3)
    sub = lax.broadcasted_iota(jnp.int32, shape3, 1)
    m1 = jnp.max(sel3, axis=1, keepdims=True)
    f1 = jnp.min(jnp.where(sel3 == m1, sub, gsz), axis=1, keepdims=True)
    m2 = jnp.max(jnp.where(sub == f1, neg, sel3), axis=1, keepdims=True)
    gscore = jnp.broadcast_to(m1 + m2, shape3).reshape(n_exp, tm)
    eidx = lax.broadcasted_iota(jnp.int32, (n_exp, tm), 0)
    gidx = eidx // gsz
    keep = jnp.zeros((n_exp, tm), F32)
    work = gscore
    for _ in range(TOPK_GROUPS):
        _, first = _first_max(work, gidx, N_EXPERT_GROUPS)
        hit = gidx == first
        keep = jnp.where(hit, 1.0, keep)
        work = jnp.where(hit, neg, work)
    work = jnp.where(keep > 0.0, sel, neg)
    chosen = jnp.zeros((n_exp, tm), F32)
    for _ in range(TOP_K):
        _, first = _first_max(work, eidx, n_exp)
        hit = eidx == first
        chosen = jnp.where(hit, 1.0, chosen)
        work = jnp.where(hit, neg, work)
    picked = chosen * scores
    gate_ref[...] = picked / jnp.sum(picked, axis=0, keepdims=True) * ROUTED_SCALE

    @pl.when(pl.program_id(0) == 0)
    def _():
        run_ref[...] = jnp.zeros(run_ref.shape, F32)

    before = _dot(chosen.astype(BF16), tri_ref[...])
    run = run_ref[...]
    rank_ref[...] = jnp.where(chosen > 0.0, before + run[:, :1], -1.0)
    run = run + jnp.broadcast_to(jnp.sum(chosen, axis=1, keepdims=True), run.shape)
    run_ref[...] = run
    cnt_ref[...] = run


def _ffn_router(x2, g, mod, layer, router_w, router_bias, seq):
    t, d = x2.shape
    n_exp = router_w.shape[1]
    tm = 512
    tiles_per_seq = seq // tm
    tri = (np.arange(tm)[:, None] < np.arange(tm)[None, :]).astype(np.float32)
    return pl.pallas_call(
        functools.partial(_router_kernel, d=d, n_exp=n_exp),
        grid=(t // tm,),
        in_specs=[pl.BlockSpec((tm, d), lambda i: (i, 0)),
                  pl.BlockSpec((1, d), lambda i: (0, 0)),
                  pl.BlockSpec((None, None, 1, 3 * d), lambda i: (layer, i // tiles_per_seq, 0, 0)),
                  pl.BlockSpec((n_exp, d), lambda i: (0, 0)),
                  pl.BlockSpec((n_exp, 1), lambda i: (0, 0)),
                  pl.BlockSpec((tm, tm), lambda i: (0, 0))],
        out_specs=[pl.BlockSpec((tm, d // 2), lambda i: (i, 0)),
                   pl.BlockSpec((n_exp, tm), lambda i: (0, i)),
                   pl.BlockSpec((n_exp, tm), lambda i: (0, i)),
                   pl.BlockSpec((n_exp, 128), lambda i: (0, 0))],
        out_shape=[jax.ShapeDtypeStruct((t, d // 2), jnp.int32),
                   jax.ShapeDtypeStruct((n_exp, t), F32),
                   jax.ShapeDtypeStruct((n_exp, t), F32),
                   jax.ShapeDtypeStruct((n_exp, 128), F32)],
        scratch_shapes=[pltpu.VMEM((n_exp, 128), F32)],
        compiler_params=_cparams(("arbitrary",)),
        name="ffn_router",
    )(x2, g.reshape(1, d), mod, router_w.T, router_bias.reshape(n_exp, 1), jnp.asarray(tri, BF16))


def _assign_kernel(rank_ref, gate_ref, start_ref, pos_ref, w_ref, *, n_exp):
    rank = rank_ref[...]
    gates = gate_ref[...]
    tm = rank.shape[1]
    slot = rank + start_ref[...]
    eidx = lax.broadcasted_iota(jnp.int32, (n_exp, tm), 0).astype(F32)
    alive = jnp.where(rank >= 0.0, eidx, float(n_exp))
    kidx = lax.broadcasted_iota(jnp.int32, (TOP_K, tm), 0)
    pos = jnp.zeros((TOP_K, tm), F32)
    wts = jnp.zeros((TOP_K, tm), F32)
    for k in range(TOP_K):
        first = jnp.min(alive, axis=0, keepdims=True)
        hit = alive == first
        pos_k = jnp.sum(jnp.where(hit, slot, 0.0), axis=0, keepdims=True)
        w_k = jnp.sum(jnp.where(hit, gates, 0.0), axis=0, keepdims=True)
        pos = jnp.where(kidx == k, pos_k, pos)
        wts = jnp.where(kidx == k, w_k, wts)
        alive = jnp.where(hit, float(n_exp), alive)
    pos_ref[...] = pos.astype(jnp.int32)
    w_ref[...] = wts


def _assign_slots(ranks, gates, start):
    n_exp, t = ranks.shape
    tm = 512
    return pl.pallas_call(
        functools.partial(_assign_kernel, n_exp=n_exp),
        grid=(t // tm,),
        in_specs=[pl.BlockSpec((n_exp, tm), lambda i: (0, i)),
                  pl.BlockSpec((n_exp, tm), lambda i: (0, i)),
                  pl.BlockSpec((n_exp, 1), lambda i: (0, 0))],
        out_specs=[pl.BlockSpec((TOP_K, tm), lambda i: (0, i)),
                   pl.BlockSpec((TOP_K, tm), lambda i: (0, i))],
        out_shape=[jax.ShapeDtypeStruct((TOP_K, t), jnp.int32), jax.ShapeDtypeStruct((TOP_K, t), F32)],
        compiler_params=_cparams(("arbitrary",)),
        name="moe_assign",
    )(ranks, gates, start)


SC_CORES = 2
SC_SUBCORES = 16
SC_WINDOW = 128


def _sc_mesh():
    return plsc.VectorSubcoreMesh(core_axis_name="c", subcore_axis_name="s",
                                  num_cores=SC_CORES, num_subcores=SC_SUBCORES)


def _sc_scatter_rows(rows, pos, n_out):
    t, width = rows.shape
    nk = pos.shape[0]
    win = SC_WINDOW
    n_win = t // win // (SC_CORES * SC_SUBCORES)
    pos_w = pos.reshape(nk, t // win, win).transpose(1, 0, 2)

    def body(rows_hbm, pos_hbm, out_hbm, idx_v, buf_v, sem):
        wid = lax.axis_index("s") * SC_CORES + lax.axis_index("c")

        @pl.loop(0, n_win)
        def _(j):
            w = wid * n_win + j
            pltpu.sync_copy(rows_hbm.at[pl.ds(w * win, win)], buf_v)
            pltpu.sync_copy(pos_hbm.at[w], idx_v)
            copies = [pltpu.make_async_copy(buf_v, out_hbm.at[idx_v.at[k]], sem) for k in range(nk)]
            for cp in copies:
                cp.start()
            for cp in copies:
                cp.wait()

    return pl.kernel(
        body, out_type=jax.ShapeDtypeStruct((n_out, width), jnp.int32), mesh=_sc_mesh(),
        scratch_types=[pltpu.VMEM((nk, win), jnp.int32), pltpu.VMEM((win, width), jnp.int32),
                       pltpu.SemaphoreType.DMA],
        name="sc_scatter_rows",
    )(rows, pos_w)


SC_GATHER_WINDOW = 64


def _sc_gather_rows(table, pos_flat):
    m = pos_flat.shape[0]
    width = table.shape[1]
    win = SC_GATHER_WINDOW
    workers = SC_CORES * SC_SUBCORES
    n_win = m // win // workers
    pos_w = pos_flat.reshape(workers, n_win, win)

    def body(table_hbm, pos_hbm, out_hbm, idx_v, buf_v, sem_g, sem_w):
        wid = lax.axis_index("s") * SC_CORES + lax.axis_index("c")
        base = wid * (n_win * win)
        pltpu.sync_copy(pos_hbm.at[wid], idx_v)

        def gather(j, b):
            return pltpu.make_async_copy(table_hbm.at[idx_v.at[j]], buf_v.at[b], sem_g.at[b])

        def write(j, b):
            return pltpu.make_async_copy(buf_v.at[b], out_hbm.at[pl.ds(base + j * win, win)], sem_w.at[b])

        gather(0, 0).start()

        @pl.loop(0, n_win, step=2)
        def _(j0):
            for b in range(2):
                j = j0 + b
                gather(j, b).wait()
                write(j, b).start()

                @pl.when(j + 1 < n_win)
                def _():
                    @pl.when(j >= 1)
                    def _():
                        write(j - 1, 1 - b).wait()
                    gather(j + 1, 1 - b).start()

        write(n_win - 2, 0).wait()
        write(n_win - 1, 1).wait()

    return pl.kernel(
        body, out_type=jax.ShapeDtypeStruct((m, width), jnp.int32), mesh=_sc_mesh(),
        scratch_types=[pltpu.VMEM((n_win, win), jnp.int32), pltpu.VMEM((2, win, width), jnp.int32),
                       pltpu.SemaphoreType.DMA((2,)), pltpu.SemaphoreType.DMA((2,))],
        name="sc_gather_rows",
    )(table, pos_w)


MOE_BLOCK = 512


def _silu_mul(a, b):
    return (a * jax.nn.sigmoid(a)) * b


def _expert_ffn_kernel(be_ref, nv_ref, xs_ref, wg_ref, wu_ref, wd_ref, ys_ref, wg_bf, wu_bf, wd_bf):
    i = pl.program_id(0)
    fresh = jnp.logical_or(i == 0, be_ref[i] != be_ref[jnp.maximum(i - 1, 0)])

    @pl.when(fresh)
    def _():
        wg_bf[...] = wg_ref[...].astype(BF16)
        wu_bf[...] = wu_ref[...].astype(BF16)
        wd_bf[...] = wd_ref[...].astype(BF16)

    nv = nv_ref[i]

    @pl.when(nv > 0)
    def _():
        x = _unpack_bf16_pairs(xs_ref[...])
        row = lax.broadcasted_iota(jnp.int32, (x.shape[0], 1), 0)
        x = jnp.where(row < nv, x, 0.0).astype(BF16)
        mid = _silu_mul(_dot(x, wg_bf[...]), _dot(x, wu_bf[...]))
        ys_ref[...] = _pack_bf16_pairs(_dot(mid.astype(BF16), wd_bf[...]))

    @pl.when(nv <= 0)
    def _():
        ys_ref[...] = jnp.zeros(ys_ref.shape, ys_ref.dtype)


def _expert_ffn(xs, block_expert, block_rows, layer, w_gate, w_up, w_down):
    n_rows, half = xs.shape
    _, n_exp, d, de = w_gate.shape
    nb = n_rows // MOE_BLOCK
    wmap = lambda i, be, nv: (layer, be[i], 0, 0)
    grid_spec = pltpu.PrefetchScalarGridSpec(
        num_scalar_prefetch=2,
        grid=(nb,),
        in_specs=[pl.BlockSpec((MOE_BLOCK, half), lambda i, be, nv: (i, 0)),
                  pl.BlockSpec((None, None, d, de), wmap),
                  pl.BlockSpec((None, None, d, de), wmap),
                  pl.BlockSpec((None, None, de, d), wmap)],
        out_specs=pl.BlockSpec((MOE_BLOCK, half), lambda i, be, nv: (i, 0)),
        scratch_shapes=[pltpu.VMEM((d, de), BF16), pltpu.VMEM((d, de), BF16), pltpu.VMEM((de, d), BF16)],
    )
    return pl.pallas_call(
        _expert_ffn_kernel,
        grid_spec=grid_spec,
        out_shape=jax.ShapeDtypeStruct((n_rows, half), jnp.int32),
        compiler_params=_cparams(("arbitrary",)),
        name="moe_expert_ffn",
    )(block_expert, block_rows, xs, w_gate, w_up, w_down)


def _combine_kernel(yt_ref, wt_ref, hp_ref, x_ref, mod_ref, sg_ref, su_ref, sd_ref, o_ref, *, d):
    h = _unpack_bf16_pairs(hp_ref[...]).astype(BF16)
    acc = _dot(_silu_mul(_dot(h, sg_ref[...]), _dot(h, su_ref[...])).astype(BF16), sd_ref[...])
    w = wt_ref[...]
    for k in range(TOP_K):
        acc = acc + w[:, k:k + 1] * _unpack_bf16_pairs(yt_ref[k])
    o_ref[...] = x_ref[...] + mod_ref[...][:, 2 * d:] * acc


def _combine(y_tok, w_tok, h_packed, x2, mod, layer, sh_gate, sh_up, sh_down, seq):
    t, d = x2.shape
    ds = sh_gate.shape[1]
    tm = 512
    tiles_per_seq = seq // tm
    row = lambda i: (i, 0)
    const = lambda i: (0, 0)
    return pl.pallas_call(
        functools.partial(_combine_kernel, d=d),
        grid=(t // tm,),
        in_specs=[pl.BlockSpec((TOP_K, tm, d // 2), lambda i: (0, i, 0)),
                  pl.BlockSpec((tm, TOP_K), row),
                  pl.BlockSpec((tm, d // 2), row),
                  pl.BlockSpec((tm, d), row),
                  pl.BlockSpec((None, None, 1, 3 * d), lambda i: (layer, i // tiles_per_seq, 0, 0)),
                  pl.BlockSpec((d, ds), const),
                  pl.BlockSpec((d, ds), const),
                  pl.BlockSpec((ds, d), const)],
        out_specs=pl.BlockSpec((tm, d), row),
        out_shape=jax.ShapeDtypeStruct((t, d), F32),
        compiler_params=_cparams(("arbitrary",), VMEM_LIMIT),
        name="moe_combine",
    )(y_tok, w_tok, h_packed, x2, mod, sh_gate.astype(BF16), sh_up.astype(BF16), sh_down.astype(BF16))


def _moe_layer(x2, h_packed, gates, ranks, counts, mod, layer, w_gate, w_up, w_down, sh_gate, sh_up, sh_down, seq):
    t, d = x2.shape
    n_exp = w_gate.shape[1]
    cnt = counts[:, 0].astype(jnp.int32)
    padded = (cnt + MOE_BLOCK - 1) // MOE_BLOCK * MOE_BLOCK
    eidx = jnp.arange(n_exp, dtype=jnp.int32)
    seg_end = jnp.sum(jnp.where(eidx[None, :] <= eidx[:, None], padded[None, :], 0), axis=1)
    seg_start = seg_end - padded
    nb = t * TOP_K // MOE_BLOCK + n_exp
    first_row = jnp.arange(nb, dtype=jnp.int32) * MOE_BLOCK
    block_expert = jnp.sum((seg_end[None, :] <= first_row[:, None]).astype(jnp.int32), axis=1)
    block_expert = jnp.minimum(block_expert, n_exp - 1)
    last_row = jnp.sum(jnp.where(block_expert[:, None] == eidx[None, :], (seg_start + cnt)[None, :], 0), axis=1)
    block_rows = jnp.clip(last_row - first_row, 0, MOE_BLOCK).astype(jnp.int32)
    pos, w = _assign_slots(ranks, gates, seg_start.astype(F32).reshape(n_exp, 1))
    xs = _sc_scatter_rows(h_packed, pos, nb * MOE_BLOCK)
    ys = _expert_ffn(xs, block_expert, block_rows, layer, w_gate, w_up, w_down)
    y_tok = _sc_gather_rows(ys, pos.reshape(TOP_K * t)).reshape(TOP_K, t, d // 2)
    return _combine(y_tok, w.T, h_packed, x2, mod, layer, sh_gate, sh_up, sh_down, seq)


def kernel(x, c, positions, mix_norm_g, mix_mod_w, mix_mod_b, ffn_norm_g, ffn_mod_w, ffn_mod_b,
           hyb_w_in, hyb_w_out, s5_lambda_re, s5_lambda_im, s5_log_dt, s5_b_re, s5_b_im,
           s5_c_re, s5_c_im, s5_d, s5_glu_w, s5_glu_b, attn_q_norm_g, attn_k_norm_g,
           pool_w, pool_scale, router_w, router_bias, exp_w_gate, exp_w_up, exp_w_down,
           sh_w_gate, sh_w_up, sh_w_down):
    bsz, seq, d = x.shape
    t = bsz * seq
    depth = mix_norm_g.shape[0]
    x2 = x.reshape(t, d)
    mix_mod = _mod_vectors(c, mix_mod_w, mix_mod_b).reshape(depth, bsz, 1, 3 * d)
    ffn_mod = _mod_vectors(c, ffn_mod_w, ffn_mod_b).reshape(depth, bsz, 1, 3 * d)
    for i in range(depth):
        j = i // 2
        if i % 2 == 0:
            u, q_hm, qi_hm, k, v, ki, wi = _in_projection(
                x2, mix_norm_g[i], mix_mod, i, hyb_w_in[j], positions, attn_q_norm_g[j], attn_k_norm_g[j], bsz, seq)
            weights = _s5_weights(s5_lambda_re[j], s5_lambda_im[j], s5_log_dt[j], s5_b_re[j], s5_b_im[j],
                                  s5_c_re[j], s5_c_im[j], s5_d[j], s5_glu_w[j], s5_glu_b[j])
            y_ssm = _s5_mixer(u, weights, bsz, seq)
            y_att = _dsa_attention(q_hm, qi_hm, wi, k, v, ki, bsz, seq)
            x2 = _out_projection(x2, y_ssm, y_att, hyb_w_out[j], mix_mod, i, seq)
        else:
            x2 = _pool_layer(x2, mix_norm_g[i], mix_mod, i, pool_w[j], pool_scale[j], seq)
        h_packed, gates, ranks, counts = _ffn_router(x2, ffn_norm_g[i], ffn_mod, i, router_w[i], router_bias[i], seq)
        x2 = _moe_layer(x2, h_packed, gates, ranks, counts, ffn_mod, i, exp_w_gate, exp_w_up, exp_w_down,
                        sh_w_gate[i], sh_w_up[i], sh_w_down[i], seq)
    return x2.reshape(bsz, seq, d)
```

```python
import functools
import math

import numpy as np
import jax
import jax.numpy as jnp
from jax import lax
from jax.experimental import pallas as pl
from jax.experimental.pallas import tpu as pltpu
from jax.experimental.pallas import tpu_sc as plsc

F32 = jnp.float32
BF16 = jnp.bfloat16

EPS = 1e-6
S5_GROUP = 16
S5_STATE = 64
HEAD_DIM = 64
ATTN_HEADS = 8
IDX_HEADS = 8
ROPE_HALF = 8
ROPE_THETA = 500000.0
TOPK_MAX = 256
POOL_WINDOWS = (2, 4, 8, 16)
POOL_HALO = 16
N_EXPERT_GROUPS = 8
TOPK_GROUPS = 4
TOP_K = 8
ROUTED_SCALE = 2.5
S5_CHUNK = 16
NEG_BIG = -1e30
INT_MIN = -2147483648
VMEM_LIMIT = 56 * 1024 * 1024


def _cparams(sem, vmem=None):
    return pltpu.CompilerParams(dimension_semantics=sem, vmem_limit_bytes=vmem)


def _dot(a, b):
    return jnp.dot(a, b, preferred_element_type=F32)


def _dot_nt(a, b):
    return lax.dot_general(a, b, (((1,), (1,)), ((), ())), preferred_element_type=F32)


def _split_bf16(a):
    hi = a.astype(BF16)
    lo = (a - hi.astype(F32)).astype(BF16)
    return hi, lo


def _dot3(a, b):
    ah, al = _split_bf16(a)
    bh, bl = _split_bf16(b)
    return _dot(ah, bh) + (_dot(ah, bl) + _dot(al, bh))


def _norm_mod(x, g, shift, scale):
    y = x * lax.rsqrt(jnp.mean(x * x, axis=-1, keepdims=True) + EPS)
    return (y * g) * (1.0 + scale) + shift


def _mod_kernel(ct_ref, w_ref, b_ref, o_ref):
    ct = ct_ref[...]
    cs = ct * jax.nn.sigmoid(ct)
    w = w_ref[...]
    rows = [jnp.sum(w * cs[:, b:b + 1], axis=0, keepdims=True) for b in range(ct.shape[1])]
    o_ref[...] = jnp.concatenate(rows, axis=0) + b_ref[...]


def _mod_vectors(c, w, b):
    nl, d, n3 = w.shape
    bsz = c.shape[0]
    tn = 512
    return pl.pallas_call(
        _mod_kernel,
        grid=(nl, n3 // tn),
        in_specs=[pl.BlockSpec((d, bsz), lambda l, j: (0, 0)),
                  pl.BlockSpec((None, d, tn), lambda l, j: (l, 0, j)),
                  pl.BlockSpec((None, 1, tn), lambda l, j: (l, 0, j))],
        out_specs=pl.BlockSpec((None, bsz, tn), lambda l, j: (l, 0, j)),
        out_shape=jax.ShapeDtypeStruct((nl, bsz, n3), F32),
        compiler_params=_cparams(("arbitrary", "arbitrary")),
        name="mod_vectors",
    )(c.T, w, b.reshape(nl, 1, n3))


def _inproj_kernel(x_ref, g_ref, mod_ref, w_ref, pos_ref, inv_ref, qg_ref, kg_ref,
                   u_ref, q_ref, qi_ref, k_ref, v_ref, ki_ref, wi_ref, u_scr, *, d):
    x = x_ref[...]
    mod = mod_ref[...]
    h = _norm_mod(x, g_ref[...], mod[:, :d], mod[:, d:2 * d])
    proj = _dot(h.astype(BF16), w_ref[...])
    tm = x.shape[0]
    width = ATTN_HEADS * HEAD_DIM

    ang = inv_ref[...] * pos_ref[...].astype(F32)
    cos, sin = jnp.cos(ang), jnp.sin(ang)

    def head_t(xt, gain, scale):
        if scale is not None:
            r = lax.rsqrt(jnp.mean(xt * xt, axis=0, keepdims=True) + EPS) * scale
            xt = xt * gain
        x1, x2 = xt[:ROPE_HALF], xt[ROPE_HALF:2 * ROPE_HALF]
        out = jnp.concatenate([x1 * cos - x2 * sin, x1 * sin + x2 * cos, xt[2 * ROPE_HALF:]], axis=0)
        return out if scale is None else out * r

    for slab in range(width // 128):
        u_scr[slab] = proj[:, slab * 128:(slab + 1) * 128]
        for j in range(S5_CHUNK):
            piece = u_scr[slab, pl.ds(j, tm // S5_CHUNK, stride=S5_CHUNK), :]
            col = (slab * S5_CHUNK + j) * 128
            u_ref[:, col:col + 128] = piece.astype(u_ref.dtype)
    q = proj[:, width:2 * width]
    qi = proj[:, 2 * width:3 * width]
    small = proj[:, 3 * width:3 * width + 256]
    qg = qg_ref[...]
    for pair in range(ATTN_HEADS // 2):
        sl = slice(pair * 128, (pair + 1) * 128)
        qt = q[:, sl].T
        qit = qi[:, sl].T
        for half in range(2):
            hs = slice(half * HEAD_DIM, (half + 1) * HEAD_DIM)
            q_ref[2 * pair + half] = head_t(qt[hs], qg, HEAD_DIM ** -0.5).astype(q_ref.dtype)
            qi_ref[2 * pair + half] = head_t(qit[hs], None, None).astype(qi_ref.dtype)
    kvt = small[:, :128].T
    kt = head_t(kvt[:HEAD_DIM], kg_ref[...], 1.0)
    kv = jnp.concatenate([kt, kvt[HEAD_DIM:]], axis=0).T
    k_ref[...] = kv[:, :HEAD_DIM].astype(k_ref.dtype)
    v_ref[...] = kv[:, HEAD_DIM:].astype(v_ref.dtype)
    kiwt = small[:, 128:256].T
    kit = jnp.concatenate([head_t(kiwt[:HEAD_DIM], None, None), kiwt[HEAD_DIM:]], axis=0).T
    ki_ref[...] = kit[:, :HEAD_DIM].astype(ki_ref.dtype)
    wscale = (IDX_HEADS ** -0.5) * (HEAD_DIM ** -0.5)
    wi_ref[...] = kiwt[HEAD_DIM:HEAD_DIM + IDX_HEADS] * wscale


def _in_projection(x2, g, mod, layer, w_in, positions, q_norm_g, k_norm_g, bsz, seq):
    t, d = x2.shape
    tm = 512
    width = ATTN_HEADS * HEAD_DIM
    cuts = np.cumsum([width, width, HEAD_DIM, HEAD_DIM, width, HEAD_DIM, IDX_HEADS])
    u_w, q_w, k_w, v_w, qi_w, ki_w, wi_w = jnp.split(w_in, cuts[:-1].tolist(), axis=1)
    pad = jnp.zeros((d, 128 - HEAD_DIM - IDX_HEADS), w_in.dtype)
    w = jnp.concatenate([u_w, q_w, qi_w, k_w, v_w, ki_w, wi_w, pad], axis=1).astype(BF16)
    nw = w.shape[1]
    inv = np.power(ROPE_THETA, -2.0 * np.arange(ROPE_HALF) / (2 * ROPE_HALF))
    inv = jnp.asarray(inv.reshape(ROPE_HALF, 1), F32)
    tiles_per_seq = seq // tm
    row = lambda i: (i, 0)
    const = lambda i: (0, 0)
    hm = lambda i: (i // tiles_per_seq, 0, 0, i % tiles_per_seq)
    outs = pl.pallas_call(
        functools.partial(_inproj_kernel, d=d),
        grid=(t // tm,),
        in_specs=[pl.BlockSpec((tm, d), row),
                  pl.BlockSpec((1, d), const),
                  pl.BlockSpec((None, None, 1, 3 * d), lambda i: (layer, i // tiles_per_seq, 0, 0)),
                  pl.BlockSpec((d, nw), const),
                  pl.BlockSpec((None, 1, tm), lambda i: (i, 0, 0)),
                  pl.BlockSpec((ROPE_HALF, 1), const),
                  pl.BlockSpec((HEAD_DIM, 1), const),
                  pl.BlockSpec((HEAD_DIM, 1), const)],
        out_specs=[pl.BlockSpec((tm // S5_CHUNK, width * S5_CHUNK), row),
                   pl.BlockSpec((None, ATTN_HEADS, HEAD_DIM, tm), hm),
                   pl.BlockSpec((None, IDX_HEADS, HEAD_DIM, tm), hm),
                   pl.BlockSpec((tm, HEAD_DIM), row),
                   pl.BlockSpec((tm, HEAD_DIM), row),
                   pl.BlockSpec((tm, HEAD_DIM), row),
                   pl.BlockSpec((IDX_HEADS, tm), lambda i: (0, i))],
        out_shape=[jax.ShapeDtypeStruct((t // S5_CHUNK, width * S5_CHUNK), BF16),
                   jax.ShapeDtypeStruct((bsz, ATTN_HEADS, HEAD_DIM, seq), BF16),
                   jax.ShapeDtypeStruct((bsz, IDX_HEADS, HEAD_DIM, seq), BF16),
                   jax.ShapeDtypeStruct((t, HEAD_DIM), BF16),
                   jax.ShapeDtypeStruct((t, HEAD_DIM), BF16),
                   jax.ShapeDtypeStruct((t, HEAD_DIM), BF16),
                   jax.ShapeDtypeStruct((IDX_HEADS, t), F32)],
        scratch_shapes=[pltpu.VMEM((width // 128, tm, 128), F32)],
        compiler_params=_cparams(("arbitrary",), VMEM_LIMIT),
        name="in_projection",
    )(x2, g.reshape(1, d), mod, w, positions.reshape(t // tm, 1, tm), inv,
      q_norm_g.reshape(HEAD_DIM, 1), k_norm_g.reshape(HEAD_DIM, 1))
    return outs


def _s5_weights(lam_re, lam_im, log_dt, b_re, b_im, c_re, c_im, d_skip, glu_w, glu_b):
    L = S5_CHUNK
    g, p = lam_re.shape
    hch = S5_GROUP
    dt = jnp.exp(log_dt)[:, None]
    lr, li = lam_re, lam_im
    tau = jnp.arange(L + 1, dtype=F32)[:, None, None]
    mag = jnp.exp(lr * dt * tau)
    pw_r, pw_i = mag * jnp.cos(li * dt * tau), mag * jnp.sin(li * dt * tau)
    ar, ai = pw_r[1], pw_i[1]
    nr, ni = ar - 1.0, ai
    den = lr * lr + li * li
    cr, ci = (nr * lr + ni * li) / den, (ni * lr - nr * li) / den
    bb_r = cr[..., None] * b_re - ci[..., None] * b_im
    bb_i = cr[..., None] * b_im + ci[..., None] * b_re
    cl_r = c_re[None] * pw_r[:L, :, None, :] - c_im[None] * pw_i[:L, :, None, :]
    cl_i = c_re[None] * pw_i[:L, :, None, :] + c_im[None] * pw_r[:L, :, None, :]
    taps = (jnp.einsum('tghp,gpk->tghk', cl_r, bb_r) - jnp.einsum('tghp,gpk->tghk', cl_i, bb_i))
    ns = g // 8
    eye = jnp.eye(8, dtype=F32)
    kd = jnp.einsum('tsgoh,gf->tsghfo', taps.reshape(L, ns, 8, hch, hch), eye).reshape(L, ns, 128, 128)
    zero = jnp.zeros_like(kd[0])
    k2 = jnp.stack([jnp.concatenate([jnp.concatenate([kd[2 * dd], kd[2 * dd + 1]], axis=-1),
                                     jnp.concatenate([kd[2 * dd - 1] if dd else zero, kd[2 * dd]], axis=-1)], axis=-2)
                    for dd in range(L // 2)], axis=1).astype(BF16)
    ii = jnp.arange(L)
    rev_r, rev_i = pw_r[L - 1 - ii], pw_i[L - 1 - ii]
    z_r = rev_r[..., None] * bb_r[None] - rev_i[..., None] * bb_i[None]
    z_i = rev_r[..., None] * bb_i[None] + rev_i[..., None] * bb_r[None]
    zc = jnp.concatenate([z_r, z_i], axis=2)
    w_z = jnp.einsum('jsgph,gf->sjghfp', zc.reshape(L, ns, 8, 2 * p, hch), eye)
    w_z = w_z.reshape(ns, L // 2, 2 * 128, 8 * 2 * p).astype(BF16)
    co_r = c_re[None] * pw_r[1:, :, None, :] - c_im[None] * pw_i[1:, :, None, :]
    co_i = c_re[None] * pw_i[1:, :, None, :] + c_im[None] * pw_r[1:, :, None, :]
    cc = jnp.concatenate([co_r, -co_i], axis=-1)
    w_c = jnp.einsum('isghp,gf->sgpifh', cc.reshape(L, ns, 8, hch, 2 * p), eye).reshape(ns, 8 * 2 * p, L * 128)
    w_ch = w_c.astype(BF16)
    w_cl = (w_c - w_ch.astype(F32)).astype(BF16)
    al_r, al_i = pw_r[L], pw_i[L]
    dec_a = jnp.concatenate([al_r, al_r], axis=-1).reshape(ns, 1, 8 * 2 * p)
    dec_b = jnp.concatenate([-al_i, al_i], axis=-1).reshape(ns, 1, 8 * 2 * p)
    d_t = jnp.tile(d_skip.reshape(ns, 1, 128), (1, 1, L))
    gl = jnp.einsum('sghk,gf->sghfk', glu_w.reshape(ns, 8, hch, hch), eye).reshape(ns, 128, 128)
    glu2 = jnp.einsum('ab,shk->sahbk', jnp.eye(2, dtype=F32), gl).reshape(ns, 256, 256).astype(BF16)
    glu_bt = jnp.tile(glu_b.reshape(ns, 1, 128), (1, 1, L))
    return k2, w_z, w_ch, w_cl, dec_a, dec_b, d_t, glu2, glu_bt


def _s5_kernel(u_ref, k2_ref, wz_ref, wch_ref, wcl_ref, da_ref, db_ref, d_ref, g2_ref, gb_ref, o_ref,
               zp_ref, zq_ref, sp_ref, *, nchunk):
    nblk = S5_CHUNK // 2
    ub = [u_ref[:, j * 256:(j + 1) * 256] for j in range(nblk)]
    z = _dot(ub[0], wz_ref[0])
    for j in range(1, nblk):
        z = z + _dot(ub[j], wz_ref[j])
    zp_ref[...] = z
    half = S5_STATE
    zq_ref[...] = jnp.concatenate([pltpu.roll(z[:, s * 2 * half:(s + 1) * 2 * half], half, 1)
                                   for s in range(z.shape[1] // (2 * half))], axis=1)
    da = da_ref[...]
    db = db_ref[...]
    dbq = -db

    def step(c, carry):
        sp, sq = carry
        sp_ref[pl.ds(c, 1), :] = sp
        sp_new = da * sp + db * sq + zp_ref[pl.ds(c, 1), :]
        sq_new = da * sq + dbq * sp + zq_ref[pl.ds(c, 1), :]
        return sp_new, sq_new

    zero = jnp.zeros((1, z.shape[1]), F32)
    lax.fori_loop(0, nchunk, step, (zero, zero))
    sh, sl = _split_bf16(sp_ref[...])
    wch = wch_ref[...]
    y_inter = _dot(sh, wch) + (_dot(sh, wcl_ref[...]) + _dot(sl, wch))
    for i in range(nblk):
        cs = slice(i * 256, (i + 1) * 256)
        acc = y_inter[:, cs] + d_ref[:, cs] * ub[i].astype(F32)
        for j in range(i + 1):
            acc = acc + _dot(ub[j], k2_ref[i - j])
        y = jax.nn.gelu(acc)
        y = y * jax.nn.sigmoid(_dot(y.astype(BF16), g2_ref[...]) + gb_ref[:, cs])
        for a in range(2):
            o_ref[pl.ds(2 * i + a, nchunk, stride=S5_CHUNK), :] = y[:, a * 128:(a + 1) * 128]


def _s5_mixer(u2, weights, bsz, seq):
    k2, w_z, w_ch, w_cl, dec_a, dec_b, d_t, glu2, glu_bt = weights
    ns = k2.shape[0]
    nchunk = seq // S5_CHUNK
    cols = S5_CHUNK * 128
    st = w_z.shape[-1]
    slab3 = lambda b, s: (s, 0, 0)
    slab4 = lambda b, s: (s, 0, 0, 0)
    return pl.pallas_call(
        functools.partial(_s5_kernel, nchunk=nchunk),
        grid=(bsz, ns),
        in_specs=[pl.BlockSpec((nchunk, cols), lambda b, s: (b, s)),
                  pl.BlockSpec((None,) + k2.shape[1:], slab4),
                  pl.BlockSpec((None,) + w_z.shape[1:], slab4),
                  pl.BlockSpec((None, st, cols), slab3),
                  pl.BlockSpec((None, st, cols), slab3),
                  pl.BlockSpec((None, 1, st), slab3),
                  pl.BlockSpec((None, 1, st), slab3),
                  pl.BlockSpec((None, 1, cols), slab3),
                  pl.BlockSpec((None, 256, 256), slab3),
                  pl.BlockSpec((None, 1, cols), slab3)],
        out_specs=pl.BlockSpec((seq, 128), lambda b, s: (b, s)),
        out_shape=jax.ShapeDtypeStruct((bsz * seq, ns * 128), F32),
        scratch_shapes=[pltpu.VMEM((nchunk, st), F32), pltpu.VMEM((nchunk, st), F32), pltpu.VMEM((nchunk, st), F32)],
        compiler_params=_cparams(("arbitrary", "arbitrary"), VMEM_LIMIT),
        name="s5_mixer",
    )(u2, k2, w_z, w_ch, w_cl, dec_a, dec_b, d_t, glu2, glu_bt)


def _dsa_kernel(qt_ref, qit_ref, wt_ref, k_ref, vt_ref, ki_ref, o_ref,
                sc_ref, s_ref, m_ref, l_ref, acc_ref, *, tq, ck, k_sel, idx_bits, max_iter):
    i = pl.program_id(1)
    nheads = qt_ref.shape[0]
    n_ck = (i * tq + tq + ck - 1) // ck
    t_pos = i * tq + lax.broadcasted_iota(jnp.int32, (1, tq), 1)
    krow = lax.broadcasted_iota(jnp.int32, (ck, 1), 0)
    kf = float(k_sel)
    inf = jnp.inf

    fr = 32
    sub = min(32768 // tq, ck)

    def fold(x, op):
        return op(x.reshape(ck // fr, fr, tq), axis=0)

    wt = wt_ref[...]

    def score_chunk(c, carry):
        lo8, hi8 = carry
        for part in range(ck // sub):
            off = pl.multiple_of(c * ck + part * sub, sub)
            kic = ki_ref[pl.ds(off, sub), :]
            acc = jnp.zeros((sub, tq), F32)
            for h in range(nheads):
                acc = acc + wt[h:h + 1, :] * jnp.maximum(_dot(kic, qit_ref[h]), 0.0)
            acc = acc + 0.0
            vis = off + krow[:sub] <= t_pos
            sc_ref[c, part * sub:(part + 1) * sub, :] = jnp.where(vis, acc, -inf)
            hi8 = jnp.maximum(hi8, jnp.max(jnp.where(vis, acc, -inf).reshape(sub // fr, fr, tq), axis=0))
            lo8 = jnp.minimum(lo8, jnp.min(jnp.where(vis, acc, inf).reshape(sub // fr, fr, tq), axis=0))
        return lo8, hi8

    lo8, hi8 = lax.fori_loop(0, n_ck, score_chunk, (jnp.full((fr, tq), inf, F32), jnp.full((fr, tq), -inf, F32)))
    lo = jnp.min(lo8, axis=0, keepdims=True)
    hi = jnp.max(hi8, axis=0, keepdims=True)
    n_vis = (t_pos + 1).astype(F32)
    few = n_vis < kf

    def reduce_keys(fn, init, combine, op):
        def body(c, part):
            return combine(part, fold(fn(sc_ref[c], c * ck), op))
        return op(lax.fori_loop(0, n_ck, body, jnp.full((fr, tq), init, F32)), axis=0, keepdims=True)

    def count(pred):
        return reduce_keys(lambda x, off: jnp.where(pred(x, off), 1.0, 0.0), 0.0, jnp.add, jnp.sum)

    def bisect(lo_, hi_, c_lo):
        mid = lo_ + (hi_ - lo_) * 0.5
        cnt = count(lambda x, off: x >= mid)
        up = cnt >= kf
        return jnp.where(up, mid, lo_), jnp.where(up, hi_, mid), jnp.where(up, cnt, c_lo)

    def status(lo_, hi_, c_lo):
        mid = lo_ + (hi_ - lo_) * 0.5
        unresolved = jnp.where(few, 0.0, jnp.where(c_lo == kf, 0.0, 1.0))
        movable = jnp.where(mid > lo_, jnp.where(mid < hi_, 1.0, 0.0), 0.0)
        return jnp.max(unresolved), jnp.max(unresolved * movable)

    def search_body(st):
        it, lo_, hi_, c_lo, _, _ = st
        for _ in range(2):
            lo_, hi_, c_lo = bisect(lo_, hi_, c_lo)
        return (it + 2, lo_, hi_, c_lo) + status(lo_, hi_, c_lo)

    st = lax.while_loop(lambda st: jnp.logical_and(st[0] < max_iter, st[5] > 0.5), search_body,
                        (jnp.int32(0), lo, hi, n_vis) + status(lo, hi, n_vis))
    lo, open_ = st[1], st[4]

    def resolve():
        def walk(thr_, strict):
            keep = (lambda x: x > thr_) if strict else (lambda x: x >= thr_)
            return reduce_keys(lambda x, off: jnp.where(keep(x), x, inf), inf, jnp.minimum, jnp.min)

        def above_equal(thr_):
            return count(lambda x, off: x > thr_), count(lambda x, off: x == thr_)

        def too_low(n_gt):
            return jnp.where(few, 0.0, jnp.where(n_gt >= kf, 1.0, 0.0))

        thr0 = walk(lo, False)

        def walk_body(st):
            thr_, n_gt, _ = st
            thr2 = jnp.where(too_low(n_gt) > 0.5, walk(thr_, True), thr_)
            return (thr2,) + above_equal(thr2)

        thr_, n_gt, n_eq = lax.while_loop(lambda st: jnp.max(too_low(st[1])) > 0.5, walk_body,
                                          (thr0,) + above_equal(thr0))
        need = kf - n_gt

        def idx_step(b, cur):
            trial = cur | (jnp.int32(1) << (idx_bits - 1 - b))
            cnt = reduce_keys(lambda x, off: jnp.where(x == thr_, jnp.where(off + krow < trial, 1.0, 0.0), 0.0),
                              0.0, jnp.add, jnp.sum)
            return jnp.where(cnt < need, trial, cur)

        return thr_, lax.fori_loop(0, idx_bits, idx_step, jnp.zeros((1, tq), jnp.int32))

    thr, cut = lax.cond(open_ > 0.5, resolve, lambda: (lo, jnp.full((1, tq), 0x7FFFFFFF, jnp.int32)))

    def bias_chunk(c, _):
        x = sc_ref[c]
        at_thr = jnp.where(x == thr, jnp.where(c * ck + krow <= cut, 0.0, NEG_BIG), NEG_BIG)
        sc_ref[c] = jnp.where(x > thr, 0.0, at_thr)
        return 0

    lax.fori_loop(0, n_ck, bias_chunk, 0)

    for h in range(nheads):
        m_ref[h] = jnp.full((fr, tq), NEG_BIG, F32)
        l_ref[h] = jnp.zeros((fr, tq), F32)
        acc_ref[h] = jnp.zeros((HEAD_DIM, tq), F32)

    def attn_scores(c, _):
        off = pl.multiple_of(c * ck, ck)
        kc = k_ref[pl.ds(off, ck), :]
        bias = sc_ref[c]
        for h in range(nheads):
            s = _dot(kc, qt_ref[h]) + bias
            s_ref[h, c] = s
            m_ref[h] = jnp.maximum(m_ref[h], fold(s, jnp.max))
        return 0

    lax.fori_loop(0, n_ck, attn_scores, 0)
    for h in range(nheads):
        m_ref[h] = jnp.broadcast_to(jnp.max(m_ref[h], axis=0, keepdims=True), (fr, tq))

    def attn_values(c, _):
        vt = vt_ref[c]
        for h in range(nheads):
            p = jnp.exp(s_ref[h, c] - m_ref[h][:1, :])
            l_ref[h] += fold(p, jnp.sum)
            acc_ref[h] += _dot(vt, p.astype(BF16))
        return 0

    lax.fori_loop(0, n_ck, attn_values, 0)
    for h in range(nheads):
        l_row = jnp.sum(l_ref[h], axis=0, keepdims=True)
        o_ref[h * HEAD_DIM:(h + 1) * HEAD_DIM, :] = (acc_ref[h] / l_row).astype(o_ref.dtype)


def _dsa_attention(qt, qit, wt, k, v, ki, bsz, seq):
    tq = 256
    ck = min(512, seq)
    n_ck = seq // ck
    k_sel = min(TOPK_MAX, seq // 4)
    nq = seq // tq
    nh = ATTN_HEADS
    width = nh * HEAD_DIM
    vt = v.reshape(bsz, n_ck, ck, HEAD_DIM).transpose(0, 1, 3, 2)
    qspec = pl.BlockSpec((None, nh, HEAD_DIM, tq), lambda b, i: (b, 0, 0, i))
    out_t = pl.pallas_call(
        functools.partial(_dsa_kernel, tq=tq, ck=ck, k_sel=k_sel, idx_bits=(seq - 1).bit_length(), max_iter=28),
        grid=(bsz, nq),
        in_specs=[qspec, qspec,
                  pl.BlockSpec((IDX_HEADS, tq), lambda b, i: (0, b * nq + i)),
                  pl.BlockSpec((seq, HEAD_DIM), lambda b, i: (b, 0)),
                  pl.BlockSpec((None, n_ck, HEAD_DIM, ck), lambda b, i: (b, 0, 0, 0)),
                  pl.BlockSpec((seq, HEAD_DIM), lambda b, i: (b, 0))],
        out_specs=pl.BlockSpec((None, width, tq), lambda b, i: (b, 0, i)),
        out_shape=jax.ShapeDtypeStruct((bsz, width, seq), BF16),
        scratch_shapes=[pltpu.VMEM((n_ck, ck, tq), F32),
                        pltpu.VMEM((nh, n_ck, ck, tq), F32),
                        pltpu.VMEM((nh, 32, tq), F32),
                        pltpu.VMEM((nh, 32, tq), F32),
                        pltpu.VMEM((nh, HEAD_DIM, tq), F32)],
        compiler_params=_cparams(("arbitrary", "arbitrary"), VMEM_LIMIT),
        name="dsa_attention",
    )(qt, qit, wt, k, vt, ki)
    return out_t.transpose(0, 2, 1).reshape(bsz * seq, width)


def _dsa_kernel_rows(q_ref, qi_ref, wi_ref, kt_ref, v_ref, kit_ref, o_ref,
                     sc_ref, s_ref, wb_ref, m_ref, l_ref, acc_ref, *, tq, ck, k_sel, idx_bits, max_iter):
    i = pl.program_id(1)
    nheads = q_ref.shape[0]
    nj = ck // 128
    n_ck = (i * tq + tq + ck - 1) // ck
    shape = (tq, 128)
    t_pos = i * tq + lax.broadcasted_iota(jnp.int32, shape, 0)
    lane = lax.broadcasted_iota(jnp.int32, shape, 1)
    kf = float(k_sel)
    inf = jnp.inf

    def rep(col):
        return jnp.broadcast_to(col, shape)

    def cols(x):
        return [x[:, j * 128:(j + 1) * 128] for j in range(nj)]

    w = wi_ref[...]
    for h in range(nheads):
        wb_ref[h] = rep(w[:, h:h + 1])

    def score_chunk(c, carry):
        lo, hi = carry
        kit = kit_ref[c]
        accs = [jnp.zeros(shape, F32) for _ in range(nj)]
        for h in range(nheads):
            s = jnp.maximum(_dot(qi_ref[h], kit), 0.0)
            wb = wb_ref[h]
            accs = [a + wb * sj for a, sj in zip(accs, cols(s))]
        for j in range(nj):
            a = accs[j] + 0.0
            vis = (c * ck + j * 128) + lane <= t_pos
            sc_ref[c, :, j * 128:(j + 1) * 128] = jnp.where(vis, a, -inf)
            hi = jnp.maximum(hi, jnp.where(vis, a, -inf))
            lo = jnp.minimum(lo, jnp.where(vis, a, inf))
        return lo, hi

    lo, hi = lax.fori_loop(0, n_ck, score_chunk, (jnp.full(shape, inf, F32), jnp.full(shape, -inf, F32)))
    lo = rep(jnp.min(lo, axis=1, keepdims=True))
    hi = rep(jnp.max(hi, axis=1, keepdims=True))
    few = (t_pos + 1).astype(F32) < kf

    def reduce_rows(fn, init, combine, final):
        def body(c, part):
            for xj in cols(sc_ref[c]):
                part = combine(part, fn(xj))
            return part
        return rep(final(lax.fori_loop(0, n_ck, body, jnp.full(shape, init, F32)), axis=1, keepdims=True))

    def count(pred):
        return reduce_rows(lambda xj: jnp.where(pred(xj), 1.0, 0.0), 0.0, jnp.add, jnp.sum)

    def pivot(it, lo_, hi_, c_lo, c_hi):
        frac = jnp.clip((c_lo - (kf + 0.5)) / (c_lo - c_hi), 0.0625, 0.9375)
        frac = jnp.where(it % 2 == 0, 0.5, frac)
        mid = lo_ + (hi_ - lo_) * frac
        half = lo_ + (hi_ - lo_) * 0.5
        return jnp.where(mid > lo_, jnp.where(mid < hi_, mid, half), half)

    def pending(lo_, hi_, mid, c_lo):
        unresolved = jnp.where(few, 0.0, jnp.where(c_lo == kf, 0.0, 1.0))
        movable = jnp.where(mid > lo_, jnp.where(mid < hi_, 1.0, 0.0), 0.0)
        return jnp.max(unresolved), jnp.max(unresolved * movable)

    def search_cond(st):
        return jnp.logical_and(st[0] < max_iter, st[-1] > 0.5)

    def search_body(st):
        it, lo_, hi_, c_lo, c_hi, mid, _, _ = st
        cnt = count(lambda xj: xj >= mid)
        up = cnt >= kf
        lo2 = jnp.where(up, mid, lo_)
        hi2 = jnp.where(up, hi_, mid)
        c_lo2 = jnp.where(up, cnt, c_lo)
        c_hi2 = jnp.where(up, c_hi, cnt)
        mid2 = pivot(it + 1, lo2, hi2, c_lo2, c_hi2)
        open_, go = pending(lo2, hi2, mid2, c_lo2)
        return it + 1, lo2, hi2, c_lo2, c_hi2, mid2, open_, go

    c_lo0 = (t_pos + 1).astype(F32)
    c_hi0 = jnp.zeros(shape, F32)
    mid0 = pivot(jnp.int32(0), lo, hi, c_lo0, c_hi0)
    open0, go0 = pending(lo, hi, mid0, c_lo0)
    st = lax.while_loop(search_cond, search_body, (jnp.int32(0), lo, hi, c_lo0, c_hi0, mid0, open0, go0))
    lo, open_ = st[1], st[6]

    def resolve():
        def walk(thr_, strict):
            keep = (lambda xj: xj > thr_) if strict else (lambda xj: xj >= thr_)
            return reduce_rows(lambda xj: jnp.where(keep(xj), xj, inf), inf, jnp.minimum, jnp.min)

        def above_equal(thr_):
            return count(lambda xj: xj > thr_), count(lambda xj: xj == thr_)

        def too_low(n_gt):
            return jnp.where(few, 0.0, jnp.where(n_gt >= kf, 1.0, 0.0))

        thr0 = walk(lo, False)
        gt0, eq0 = above_equal(thr0)

        def walk_body(st):
            thr_, n_gt, _ = st
            thr2 = jnp.where(too_low(n_gt) > 0.5, walk(thr_, True), thr_)
            return (thr2,) + above_equal(thr2)

        thr_, n_gt, n_eq = lax.while_loop(lambda st: jnp.max(too_low(st[1])) > 0.5, walk_body, (thr0, gt0, eq0))
        need = kf - n_gt

        def idx_step(b, cur):
            trial = cur | (jnp.int32(1) << (idx_bits - 1 - b))

            def body(c, part):
                for j, xj in enumerate(cols(sc_ref[c])):
                    idx = (c * ck + j * 128) + lane
                    part = part + jnp.where(xj == thr_, jnp.where(idx < trial, 1.0, 0.0), 0.0)
                return part
            cnt = rep(jnp.sum(lax.fori_loop(0, n_ck, body, jnp.zeros(shape, F32)), axis=1, keepdims=True))
            return jnp.where(cnt < need, trial, cur)

        cut_ = lax.fori_loop(0, idx_bits, idx_step, jnp.zeros(shape, jnp.int32))
        return thr_, cut_

    thr, cut = lax.cond(open_ > 0.5, resolve, lambda: (lo, jnp.full(shape, 0x7FFFFFFF, jnp.int32)))

    def bias_chunk(c, _):
        x = sc_ref[c]
        for j, xj in enumerate(cols(x)):
            idx = (c * ck + j * 128) + lane
            at_thr = jnp.where(xj == thr, jnp.where(idx <= cut, 0.0, NEG_BIG), NEG_BIG)
            sc_ref[c, :, j * 128:(j + 1) * 128] = jnp.where(xj > thr, 0.0, at_thr)
        return 0

    lax.fori_loop(0, n_ck, bias_chunk, 0)

    for h in range(nheads):
        m_ref[h] = jnp.full(shape, NEG_BIG, F32)
        l_ref[h] = jnp.zeros(shape, F32)
        acc_ref[h] = jnp.zeros((tq, HEAD_DIM), F32)

    def attn_scores(c, _):
        kt = kt_ref[c]
        bias = cols(sc_ref[c])
        for h in range(nheads):
            s = cols(_dot(q_ref[h], kt))
            part = m_ref[h]
            for j in range(nj):
                sj = s[j] + bias[j]
                s_ref[h, c, :, j * 128:(j + 1) * 128] = sj
                part = jnp.maximum(part, sj)
            m_ref[h] = part
        return 0

    lax.fori_loop(0, n_ck, attn_scores, 0)
    for h in range(nheads):
        m_ref[h] = rep(jnp.max(m_ref[h], axis=1, keepdims=True))

    def attn_values(c, _):
        off = pl.multiple_of(c * ck, ck)
        vc = v_ref[pl.ds(off, ck), :]
        for h in range(nheads):
            m = m_ref[h]
            p = [jnp.exp(sj - m) for sj in cols(s_ref[h, c])]
            l_ref[h] += functools.reduce(jnp.add, p)
            acc_ref[h] += _dot(jnp.concatenate(p, axis=1).astype(BF16), vc)
        return 0

    lax.fori_loop(0, n_ck, attn_values, 0)
    for h in range(nheads):
        l_row = jnp.sum(l_ref[h], axis=1, keepdims=True)
        o_ref[:, h * HEAD_DIM:(h + 1) * HEAD_DIM] = (acc_ref[h] / l_row).astype(o_ref.dtype)


def _dsa_attention_rows(q_hm, qi_hm, wi, k, v, ki, bsz, seq):
    tq = 128
    ck = min(512, seq)
    n_ck = seq // ck
    k_sel = min(TOPK_MAX, seq // 4)
    nq = seq // tq
    width = ATTN_HEADS * HEAD_DIM
    kt = k.reshape(bsz, n_ck, ck, HEAD_DIM).transpose(0, 1, 3, 2)
    kit = ki.reshape(bsz, n_ck, ck, HEAD_DIM).transpose(0, 1, 3, 2)
    hm = lambda b, i: (b, 0, i, 0)
    per_batch = lambda b, i: (b, 0, 0, 0)
    nh = ATTN_HEADS
    return pl.pallas_call(
        functools.partial(_dsa_kernel_rows, tq=tq, ck=ck, k_sel=k_sel, idx_bits=(seq - 1).bit_length(), max_iter=48),
        grid=(bsz, nq),
        in_specs=[pl.BlockSpec((None, ATTN_HEADS, tq, HEAD_DIM), hm),
                  pl.BlockSpec((None, IDX_HEADS, tq, HEAD_DIM), hm),
                  pl.BlockSpec((tq, 128), lambda b, i: (b * nq + i, 0)),
                  pl.BlockSpec((None, n_ck, HEAD_DIM, ck), per_batch),
                  pl.BlockSpec((seq, HEAD_DIM), lambda b, i: (b, 0)),
                  pl.BlockSpec((None, n_ck, HEAD_DIM, ck), per_batch)],
        out_specs=pl.BlockSpec((tq, width), lambda b, i: (b * nq + i, 0)),
        out_shape=jax.ShapeDtypeStruct((bsz * seq, width), BF16),
        scratch_shapes=[pltpu.VMEM((n_ck, tq, ck), F32),
                        pltpu.VMEM((nh, n_ck, tq, ck), F32),
                        pltpu.VMEM((IDX_HEADS, tq, 128), F32),
                        pltpu.VMEM((nh, tq, 128), F32),
                        pltpu.VMEM((nh, tq, 128), F32),
                        pltpu.VMEM((nh, tq, HEAD_DIM), F32)],
        compiler_params=_cparams(("arbitrary", "arbitrary"), VMEM_LIMIT),
        name="dsa_attention",
    )(q_hm, qi_hm, wi, kt, v, kit)


def _outproj_kernel(x_ref, ys_ref, ya_ref, w_ref, mod_ref, o_ref, *, d):
    width = ys_ref.shape[1]
    y = _dot(ys_ref[...].astype(BF16), w_ref[:width, :]) + _dot(ya_ref[...], w_ref[width:, :])
    o_ref[...] = x_ref[...] + mod_ref[...][:, 2 * d:] * y


def _out_projection(x2, y_ssm, y_att, w_out, mod, layer, seq):
    t, d = x2.shape
    tm = 512
    width = y_ssm.shape[1]
    tiles_per_seq = seq // tm
    row = lambda i: (i, 0)
    return pl.pallas_call(
        functools.partial(_outproj_kernel, d=d),
        grid=(t // tm,),
        in_specs=[pl.BlockSpec((tm, d), row),
                  pl.BlockSpec((tm, width), row),
                  pl.BlockSpec((tm, width), row),
                  pl.BlockSpec((2 * width, d), lambda i: (0, 0)),
                  pl.BlockSpec((None, None, 1, 3 * d), lambda i: (layer, i // tiles_per_seq, 0, 0))],
        out_specs=pl.BlockSpec((tm, d), row),
        out_shape=jax.ShapeDtypeStruct((t, d), F32),
        compiler_params=_cparams(("arbitrary",)),
        name="out_projection",
    )(x2, y_ssm, y_att, w_out.astype(BF16), mod)


def _pool_kernel(x_ref, halo_ref, g_ref, mod_ref, pw_ref, ps_ref, o_ref, h_ref, *, d, tiles_per_seq):
    i = pl.program_id(0)
    tm = x_ref.shape[0]
    mod = mod_ref[...]
    shift, scale, gate = mod[:, :d], mod[:, d:2 * d], mod[:, 2 * d:]
    x = x_ref[...]
    first = (i % tiles_per_seq) == 0
    h_ref[POOL_HALO:, :] = _norm_mod(x, g_ref[...], shift, scale)
    halo = _norm_mod(halo_ref[...], g_ref[...], shift, scale)
    h_ref[:POOL_HALO, :] = jnp.where(first, 0.0, halo)
    pos = (i % tiles_per_seq) * tm + lax.broadcasted_iota(jnp.int32, (tm, 1), 0)
    gw = d // len(POOL_WINDOWS)
    ys = []
    for gi, win in enumerate(POOL_WINDOWS):
        cs = slice(gi * gw, (gi + 1) * gw)
        cur = h_ref[POOL_HALO:, cs]
        tot = cur
        for k in range(1, win):
            tot = tot + h_ref[POOL_HALO - k:POOL_HALO - k + tm, cs]
        cnt = jnp.minimum(pos + 1, win).astype(F32)
        pooled = tot / cnt - cur
        ys.append(_dot(pooled.astype(BF16), pw_ref[gi]))
    y = jnp.concatenate(ys, axis=1) * ps_ref[...]
    o_ref[...] = x + gate * y


def _pool_layer(x2, g, mod, layer, pool_w, pool_scale, seq):
    t, d = x2.shape
    tm = 512
    tiles_per_seq = seq // tm
    gw = d // len(POOL_WINDOWS)
    hb = tm // POOL_HALO
    return pl.pallas_call(
        functools.partial(_pool_kernel, d=d, tiles_per_seq=tiles_per_seq),
        grid=(t // tm,),
        in_specs=[pl.BlockSpec((tm, d), lambda i: (i, 0)),
                  pl.BlockSpec((POOL_HALO, d), lambda i: (jnp.maximum(i * hb - 1, 0), 0)),
                  pl.BlockSpec((1, d), lambda i: (0, 0)),
                  pl.BlockSpec((None, None, 1, 3 * d), lambda i: (layer, i // tiles_per_seq, 0, 0)),
                  pl.BlockSpec((len(POOL_WINDOWS), gw, gw), lambda i: (0, 0, 0)),
                  pl.BlockSpec((1, d), lambda i: (0, 0))],
        out_specs=pl.BlockSpec((tm, d), lambda i: (i, 0)),
        out_shape=jax.ShapeDtypeStruct((t, d), F32),
        scratch_shapes=[pltpu.VMEM((tm + POOL_HALO, d), F32)],
        compiler_params=_cparams(("arbitrary",)),
        name="pool_mixer",
    )(x2, x2, g.reshape(1, d), mod, pool_w.astype(BF16), pool_scale.reshape(1, d))


def _first_max(vals, idx, big):
    m = jnp.max(vals, axis=0, keepdims=True)
    first = jnp.min(jnp.where(vals == m, idx, big), axis=0, keepdims=True)
    return m, first


def _pack_bf16_pairs(x):
    n = x.shape[1] // 2
    lo = pltpu.bitcast(x[:, :n].astype(BF16).astype(F32), jnp.int32)
    hi = pltpu.bitcast(x[:, n:].astype(BF16).astype(F32), jnp.int32)
    return hi | ((lo >> 16) & 0xFFFF)


def _unpack_bf16_pairs(w):
    lo = pltpu.bitcast(w << 16, F32)
    hi = pltpu.bitcast(w & jnp.int32(-65536), F32)
    return jnp.concatenate([lo, hi], axis=1)


def _router_kernel(x_ref, g_ref, mod_ref, rw_ref, rb_ref, tri_ref, h_ref, gate_ref, rank_ref, cnt_ref,
                   run_ref, *, d, n_exp):
    mod = mod_ref[...]
    h = _norm_mod(x_ref[...], g_ref[...], mod[:, :d], mod[:, d:2 * d])
    h_ref[...] = _pack_bf16_pairs(h)
    tm = h.shape[0]
    hh, hl = _split_bf16(h)
    rw = rw_ref[...]
    rh, rl = _split_bf16(rw)
    logits = _dot_nt(rh, hh) + (_dot_nt(rh, hl) + _dot_nt(rl, hh))
    scores = jax.nn.sigmoid(logits)
    sel = scores + rb_ref[...]
    gsz = n_exp // N_EXPERT_GROUPS
    neg = -jnp.inf
    shape3 = (N_EXPERT_GROUPS, gsz, tm)
    sel3 = sel.reshape(shape3)
    sub = lax.broadcasted_iota(jnp.int32, shape3, 1)
    m1 = jnp.max(sel3, axis=1, keepdims=True)
    f1 = jnp.min(jnp.where(sel3 == m1, sub, gsz), axis=1, keepdims=True)
    m2 = jnp.max(jnp.where(sub == f1, neg, sel3), axis=1, keepdims=True)
    gscore = jnp.broadcast_to(m1 + m2, shape3).reshape(n_exp, tm)
    eidx = lax.broadcasted_iota(jnp.int32, (n_exp, tm), 0)
    gidx = eidx // gsz
    keep = jnp.zeros((n_exp, tm), F32)
    work = gscore
    for _ in range(TOPK_GROUPS):
        _, first = _first_max(work, gidx, N_EXPERT_GROUPS)
        hit = gidx == first
        keep = jnp.where(hit, 1.0, keep)
        work = jnp.where(hit, neg, work)
    work = jnp.where(keep > 0.0, sel, neg)
    chosen = jnp.zeros((n_exp, tm), F32)
    for _ in range(TOP_K):
        _, first = _first_max(work, eidx, n_exp)
        hit = eidx == first
        chosen = jnp.where(hit, 1.0, chosen)
        work = jnp.where(hit, neg, work)
    picked = chosen * scores
    gate_ref[...] = picked / jnp.sum(picked, axis=0, keepdims=True) * ROUTED_SCALE

    @pl.when(pl.program_id(0) == 0)
    def _():
        run_ref[...] = jnp.zeros(run_ref.shape, F32)

    before = _dot(chosen.astype(BF16), tri_ref[...])
    run = run_ref[...]
    rank_ref[...] = jnp.where(chosen > 0.0, before + run[:, :1], -1.0)
    run = run + jnp.broadcast_to(jnp.sum(chosen, axis=1, keepdims=True), run.shape)
    run_ref[...] = run
    cnt_ref[...] = run


def _ffn_router(x2, g, mod, layer, router_w, router_bias, seq):
    t, d = x2.shape
    n_exp = router_w.shape[1]
    tm = 512
    tiles_per_seq = seq // tm
    tri = (np.arange(tm)[:, None] < np.arange(tm)[None, :]).astype(np.float32)
    return pl.pallas_call(
        functools.partial(_router_kernel, d=d, n_exp=n_exp),
        grid=(t // tm,),
        in_specs=[pl.BlockSpec((tm, d), lambda i: (i, 0)),
                  pl.BlockSpec((1, d), lambda i: (0, 0)),
                  pl.BlockSpec((None, None, 1, 3 * d), lambda i: (layer, i // tiles_per_seq, 0, 0)),
                  pl.BlockSpec((n_exp, d), lambda i: (0, 0)),
                  pl.BlockSpec((n_exp, 1), lambda i: (0, 0)),
                  pl.BlockSpec((tm, tm), lambda i: (0, 0))],
        out_specs=[pl.BlockSpec((tm, d // 2), lambda i: (i, 0)),
                   pl.BlockSpec((n_exp, tm), lambda i: (0, i)),
                   pl.BlockSpec((n_exp, tm), lambda i: (0, i)),
                   pl.BlockSpec((n_exp, 128), lambda i: (0, 0))],
        out_shape=[jax.ShapeDtypeStruct((t, d // 2), jnp.int32),
                   jax.ShapeDtypeStruct((n_exp, t), F32),
                   jax.ShapeDtypeStruct((n_exp, t), F32),
                   jax.ShapeDtypeStruct((n_exp, 128), F32)],
        scratch_shapes=[pltpu.VMEM((n_exp, 128), F32)],
        compiler_params=_cparams(("arbitrary",)),
        name="ffn_router",
    )(x2, g.reshape(1, d), mod, router_w.T, router_bias.reshape(n_exp, 1), jnp.asarray(tri, BF16))


def _assign_kernel(rank_ref, gate_ref, start_ref, pos_ref, w_ref, *, n_exp):
    rank = rank_ref[...]
    gates = gate_ref[...]
    tm = rank.shape[1]
    slot = rank + start_ref[...]
    eidx = lax.broadcasted_iota(jnp.int32, (n_exp, tm), 0).astype(F32)
    alive = jnp.where(rank >= 0.0, eidx, float(n_exp))
    kidx = lax.broadcasted_iota(jnp.int32, (TOP_K, tm), 0)
    pos = jnp.zeros((TOP_K, tm), F32)
    wts = jnp.zeros((TOP_K, tm), F32)
    for k in range(TOP_K):
        first = jnp.min(alive, axis=0, keepdims=True)
        hit = alive == first
        pos_k = jnp.sum(jnp.where(hit, slot, 0.0), axis=0, keepdims=True)
        w_k = jnp.sum(jnp.where(hit, gates, 0.0), axis=0, keepdims=True)
        pos = jnp.where(kidx == k, pos_k, pos)
        wts = jnp.where(kidx == k, w_k, wts)
        alive = jnp.where(hit, float(n_exp), alive)
    pos_ref[...] = pos.astype(jnp.int32)
    w_ref[...] = wts


def _assign_slots(ranks, gates, start):
    n_exp, t = ranks.shape
    tm = 512
    return pl.pallas_call(
        functools.partial(_assign_kernel, n_exp=n_exp),
        grid=(t // tm,),
        in_specs=[pl.BlockSpec((n_exp, tm), lambda i: (0, i)),
                  pl.BlockSpec((n_exp, tm), lambda i: (0, i)),
                  pl.BlockSpec((n_exp, 1), lambda i: (0, 0))],
        out_specs=[pl.BlockSpec((TOP_K, tm), lambda i: (0, i)),
                   pl.BlockSpec((TOP_K, tm), lambda i: (0, i))],
        out_shape=[jax.ShapeDtypeStruct((TOP_K, t), jnp.int32), jax.ShapeDtypeStruct((TOP_K, t), F32)],
        compiler_params=_cparams(("arbitrary",)),
        name="moe_assign",
    )(ranks, gates, start)


SC_CORES = 2
SC_SUBCORES = 16
SC_WINDOW = 128


def _sc_mesh():
    return plsc.VectorSubcoreMesh(core_axis_name="c", subcore_axis_name="s",
                                  num_cores=SC_CORES, num_subcores=SC_SUBCORES)


def _sc_scatter_rows(rows, pos, n_out):
    t, width = rows.shape
    nk = pos.shape[0]
    win = SC_WINDOW
    n_win = t // win // (SC_CORES * SC_SUBCORES)
    pos_w = pos.reshape(nk, t // win, win).transpose(1, 0, 2)

    def body(rows_hbm, pos_hbm, out_hbm, idx_v, buf_v, sem):
        wid = lax.axis_index("s") * SC_CORES + lax.axis_index("c")

        @pl.loop(0, n_win)
        def _(j):
            w = wid * n_win + j
            pltpu.sync_copy(rows_hbm.at[pl.ds(w * win, win)], buf_v)
            pltpu.sync_copy(pos_hbm.at[w], idx_v)
            copies = [pltpu.make_async_copy(buf_v, out_hbm.at[idx_v.at[k]], sem) for k in range(nk)]
            for cp in copies:
                cp.start()
            for cp in copies:
                cp.wait()

    return pl.kernel(
        body, out_type=jax.ShapeDtypeStruct((n_out, width), jnp.int32), mesh=_sc_mesh(),
        scratch_types=[pltpu.VMEM((nk, win), jnp.int32), pltpu.VMEM((win, width), jnp.int32),
                       pltpu.SemaphoreType.DMA],
        name="sc_scatter_rows",
    )(rows, pos_w)


SC_GATHER_WINDOW = 64


def _sc_gather_rows(table, pos_flat):
    m = pos_flat.shape[0]
    width = table.shape[1]
    win = SC_GATHER_WINDOW
    workers = SC_CORES * SC_SUBCORES
    n_win = m // win // workers
    pos_w = pos_flat.reshape(workers, n_win, win)

    def body(table_hbm, pos_hbm, out_hbm, idx_v, buf_v, sem_g, sem_w):
        wid = lax.axis_index("s") * SC_CORES + lax.axis_index("c")
        base = wid * (n_win * win)
        pltpu.sync_copy(pos_hbm.at[wid], idx_v)

        def gather(j, b):
            return pltpu.make_async_copy(table_hbm.at[idx_v.at[j]], buf_v.at[b], sem_g.at[b])

        def write(j, b):
            return pltpu.make_async_copy(buf_v.at[b], out_hbm.at[pl.ds(base + j * win, win)], sem_w.at[b])

        gather(0, 0).start()

        @pl.loop(0, n_win, step=2)
        def _(j0):
            for b in range(2):
                j = j0 + b
                gather(j, b).wait()
                write(j, b).start()

                @pl.when(j + 1 < n_win)
                def _():
                    @pl.when(j >= 1)
                    def _():
                        write(j - 1, 1 - b).wait()
                    gather(j + 1, 1 - b).start()

        write(n_win - 2, 0).wait()
        write(n_win - 1, 1).wait()

    return pl.kernel(
        body, out_type=jax.ShapeDtypeStruct((m, width), jnp.int32), mesh=_sc_mesh(),
        scratch_types=[pltpu.VMEM((n_win, win), jnp.int32), pltpu.VMEM((2, win, width), jnp.int32),
                       pltpu.SemaphoreType.DMA((2,)), pltpu.SemaphoreType.DMA((2,))],
        name="sc_gather_rows",
    )(table, pos_w)


MOE_BLOCK = 512


def _silu_mul(a, b):
    return (a * jax.nn.sigmoid(a)) * b


def _expert_ffn_kernel(be_ref, nv_ref, xs_ref, wg_ref, wu_ref, wd_ref, ys_ref, wg_bf, wu_bf, wd_bf):
    i = pl.program_id(0)
    fresh = jnp.logical_or(i == 0, be_ref[i] != be_ref[jnp.maximum(i - 1, 0)])

    @pl.when(fresh)
    def _():
        wg_bf[...] = wg_ref[...].astype(BF16)
        wu_bf[...] = wu_ref[...].astype(BF16)
        wd_bf[...] = wd_ref[...].astype(BF16)

    nv = nv_ref[i]

    @pl.when(nv > 0)
    def _():
        x = _unpack_bf16_pairs(xs_ref[...])
        row = lax.broadcasted_iota(jnp.int32, (x.shape[0], 1), 0)
        x = jnp.where(row < nv, x, 0.0).astype(BF16)
        mid = _silu_mul(_dot(x, wg_bf[...]), _dot(x, wu_bf[...]))
        ys_ref[...] = _pack_bf16_pairs(_dot(mid.astype(BF16), wd_bf[...]))

    @pl.when(nv <= 0)
    def _():
        ys_ref[...] = jnp.zeros(ys_ref.shape, ys_ref.dtype)


def _expert_ffn(xs, block_expert, block_rows, layer, w_gate, w_up, w_down):
    n_rows, half = xs.shape
    _, n_exp, d, de = w_gate.shape
    nb = n_rows // MOE_BLOCK
    wmap = lambda i, be, nv: (layer, be[i], 0, 0)
    grid_spec = pltpu.PrefetchScalarGridSpec(
        num_scalar_prefetch=2,
        grid=(nb,),
        in_specs=[pl.BlockSpec((MOE_BLOCK, half), lambda i, be, nv: (i, 0)),
                  pl.BlockSpec((None, None, d, de), wmap),
                  pl.BlockSpec((None, None, d, de), wmap),
                  pl.BlockSpec((None, None, de, d), wmap)],
        out_specs=pl.BlockSpec((MOE_BLOCK, half), lambda i, be, nv: (i, 0)),
        scratch_shapes=[pltpu.VMEM((d, de), BF16), pltpu.VMEM((d, de), BF16), pltpu.VMEM((de, d), BF16)],
    )
    return pl.pallas_call(
        _expert_ffn_kernel,
        grid_spec=grid_spec,
        out_shape=jax.ShapeDtypeStruct((n_rows, half), jnp.int32),
        compiler_params=_cparams(("arbitrary",)),
        name="moe_expert_ffn",
    )(block_expert, block_rows, xs, w_gate, w_up, w_down)


def _combine_kernel(yt_ref, wt_ref, hp_ref, x_ref, mod_ref, sg_ref, su_ref, sd_ref, o_ref, *, d):
    h = _unpack_bf16_pairs(hp_ref[...]).astype(BF16)
    acc = _dot(_silu_mul(_dot(h, sg_ref[...]), _dot(h, su_ref[...])).astype(BF16), sd_ref[...])
    w = wt_ref[...]
    for k in range(TOP_K):
        acc = acc + w[:, k:k + 1] * _unpack_bf16_pairs(yt_ref[k])
    o_ref[...] = x_ref[...] + mod_ref[...][:, 2 * d:] * acc


def _combine(y_tok, w_tok, h_packed, x2, mod, layer, sh_gate, sh_up, sh_down, seq):
    t, d = x2.shape
    ds = sh_gate.shape[1]
    tm = 512
    tiles_per_seq = seq // tm
    row = lambda i: (i, 0)
    const = lambda i: (0, 0)
    return pl.pallas_call(
        functools.partial(_combine_kernel, d=d),
        grid=(t // tm,),
        in_specs=[pl.BlockSpec((TOP_K, tm, d // 2), lambda i: (0, i, 0)),
                  pl.BlockSpec((tm, TOP_K), row),
                  pl.BlockSpec((tm, d // 2), row),
                  pl.BlockSpec((tm, d), row),
                  pl.BlockSpec((None, None, 1, 3 * d), lambda i: (layer, i // tiles_per_seq, 0, 0)),
                  pl.BlockSpec((d, ds), const),
                  pl.BlockSpec((d, ds), const),
                  pl.BlockSpec((ds, d), const)],
        out_specs=pl.BlockSpec((tm, d), row),
        out_shape=jax.ShapeDtypeStruct((t, d), F32),
        compiler_params=_cparams(("arbitrary",), VMEM_LIMIT),
        name="moe_combine",
    )(y_tok, w_tok, h_packed, x2, mod, sh_gate.astype(BF16), sh_up.astype(BF16), sh_down.astype(BF16))


def _moe_layer(x2, h_packed, gates, ranks, counts, mod, layer, w_gate, w_up, w_down, sh_gate, sh_up, sh_down, seq):
    t, d = x2.shape
    n_exp = w_gate.shape[1]
    cnt = counts[:, 0].astype(jnp.int32)
    padded = (cnt + MOE_BLOCK - 1) // MOE_BLOCK * MOE_BLOCK
    eidx = jnp.arange(n_exp, dtype=jnp.int32)
    seg_end = jnp.sum(jnp.where(eidx[None, :] <= eidx[:, None], padded[None, :], 0), axis=1)
    seg_start = seg_end - padded
    nb = t * TOP_K // MOE_BLOCK + n_exp
    first_row = jnp.arange(nb, dtype=jnp.int32) * MOE_BLOCK
    block_expert = jnp.sum((seg_end[None, :] <= first_row[:, None]).astype(jnp.int32), axis=1)
    block_expert = jnp.minimum(block_expert, n_exp - 1)
    last_row = jnp.sum(jnp.where(block_expert[:, None] == eidx[None, :], (seg_start + cnt)[None, :], 0), axis=1)
    block_rows = jnp.clip(last_row - first_row, 0, MOE_BLOCK).astype(jnp.int32)
    pos, w = _assign_slots(ranks, gates, seg_start.astype(F32).reshape(n_exp, 1))
    xs = _sc_scatter_rows(h_packed, pos, nb * MOE_BLOCK)
    ys = _expert_ffn(xs, block_expert, block_rows, layer, w_gate, w_up, w_down)
    y_tok = _sc_gather_rows(ys, pos.reshape(TOP_K * t)).reshape(TOP_K, t, d // 2)
    return _combine(y_tok, w.T, h_packed, x2, mod, layer, sh_gate, sh_up, sh_down, seq)


def kernel(x, c, positions, mix_norm_g, mix_mod_w, mix_mod_b, ffn_norm_g, ffn_mod_w, ffn_mod_b,
           hyb_w_in, hyb_w_out, s5_lambda_re, s5_lambda_im, s5_log_dt, s5_b_re, s5_b_im,
           s5_c_re, s5_c_im, s5_d, s5_glu_w, s5_glu_b, attn_q_norm_g, attn_k_norm_g,
           pool_w, pool_scale, router_w, router_bias, exp_w_gate, exp_w_up, exp_w_down,
           sh_w_gate, sh_w_up, sh_w_down):
    bsz, seq, d = x.shape
    t = bsz * seq
    depth = mix_norm_g.shape[0]
    x2 = x.reshape(t, d)
    mix_mod = _mod_vectors(c, mix_mod_w, mix_mod_b).reshape(depth, bsz, 1, 3 * d)
    ffn_mod = _mod_vectors(c, ffn_mod_w, ffn_mod_b).reshape(depth, bsz, 1, 3 * d)
    for i in range(depth):
        j = i // 2
        if i % 2 == 0:
            u, q_hm, qi_hm, k, v, ki, wi = _in_projection(
                x2, mix_norm_g[i], mix_mod, i, hyb_w_in[j], positions, attn_q_norm_g[j], attn_k_norm_g[j], bsz, seq)
            weights = _s5_weights(s5_lambda_re[j], s5_lambda_im[j], s5_log_dt[j], s5_b_re[j], s5_b_im[j],
                                  s5_c_re[j], s5_c_im[j], s5_d[j], s5_glu_w[j], s5_glu_b[j])
            y_ssm = _s5_mixer(u, weights, bsz, seq)
            y_att = _dsa_attention(q_hm, qi_hm, wi, k, v, ki, bsz, seq)
            x2 = _out_projection(x2, y_ssm, y_att, hyb_w_out[j], mix_mod, i, seq)
        else:
            x2 = _pool_layer(x2, mix_norm_g[i], mix_mod, i, pool_w[j], pool_scale[j], seq)
        h_packed, gates, ranks, counts = _ffn_router(x2, ffn_norm_g[i], ffn_mod, i, router_w[i], router_bias[i], seq)
        x2 = _moe_layer(x2, h_packed, gates, ranks, counts, ffn_mod, i, exp_w_gate, exp_w_up, exp_w_down,
                        sh_w_gate[i], sh_w_up[i], sh_w_down[i], seq)
    return x2.reshape(bsz, seq, d)
```

```python
import functools

import numpy as np
import jax
import jax.numpy as jnp
from jax import lax
from jax.experimental import pallas as pl
from jax.experimental.pallas import tpu as pltpu
from jax.experimental.pallas import tpu_sc as plsc

F32 = jnp.float32
BF16 = jnp.bfloat16

EPS = 1e-6
S5_GROUP = 16
S5_STATE = 64
HEAD_DIM = 64
ATTN_HEADS = 8
IDX_HEADS = 8
ROPE_HALF = 8
ROPE_THETA = 500000.0
TOPK_MAX = 256
POOL_WINDOWS = (2, 4, 8, 16)
POOL_HALO = 16
N_EXPERT_GROUPS = 8
TOPK_GROUPS = 4
TOP_K = 8
ROUTED_SCALE = 2.5
S5_CHUNK = 16
NEG_BIG = -1e30
VMEM_LIMIT = 56 * 1024 * 1024


def _cparams(sem, vmem=None):
    return pltpu.CompilerParams(dimension_semantics=sem, vmem_limit_bytes=vmem)


def _dot(a, b):
    return jnp.dot(a, b, preferred_element_type=F32)


def _dot_nt(a, b):
    return lax.dot_general(a, b, (((1,), (1,)), ((), ())), preferred_element_type=F32)


def _split_bf16(a):
    hi = a.astype(BF16)
    lo = (a - hi.astype(F32)).astype(BF16)
    return hi, lo


def _norm_mod(x, g, shift, scale):
    y = x * lax.rsqrt(jnp.mean(x * x, axis=-1, keepdims=True) + EPS)
    return (y * g) * (1.0 + scale) + shift


def _mod_kernel(ct_ref, w_ref, b_ref, o_ref):
    ct = ct_ref[...]
    cs = ct * jax.nn.sigmoid(ct)
    w = w_ref[...]
    rows = [jnp.sum(w * cs[:, b:b + 1], axis=0, keepdims=True) for b in range(ct.shape[1])]
    o_ref[...] = jnp.concatenate(rows, axis=0) + b_ref[...]


def _mod_vectors(c, w, b):
    nl, d, n3 = w.shape
    bsz = c.shape[0]
    tn = 512
    return pl.pallas_call(
        _mod_kernel,
        grid=(nl, n3 // tn),
        in_specs=[pl.BlockSpec((d, bsz), lambda l, j: (0, 0)),
                  pl.BlockSpec((None, d, tn), lambda l, j: (l, 0, j)),
                  pl.BlockSpec((None, 1, tn), lambda l, j: (l, 0, j))],
        out_specs=pl.BlockSpec((None, bsz, tn), lambda l, j: (l, 0, j)),
        out_shape=jax.ShapeDtypeStruct((nl, bsz, n3), F32),
        compiler_params=_cparams(("arbitrary", "arbitrary")),
        name="mod_vectors",
    )(c.T, w, b.reshape(nl, 1, n3))


def _inproj_kernel(x_ref, g_ref, mod_ref, w_ref, pos_ref, inv_ref, qg_ref, kg_ref,
                   u_ref, q_ref, qi_ref, k_ref, v_ref, ki_ref, wi_ref, u_scr, *, d):
    x = x_ref[...]
    mod = mod_ref[...]
    h = _norm_mod(x, g_ref[...], mod[:, :d], mod[:, d:2 * d])
    proj = _dot(h.astype(BF16), w_ref[...])
    tm = x.shape[0]
    width = ATTN_HEADS * HEAD_DIM

    ang = inv_ref[...] * pos_ref[...].astype(F32)
    cos, sin = jnp.cos(ang), jnp.sin(ang)

    def head_t(xt, gain, scale):
        if scale is not None:
            r = lax.rsqrt(jnp.mean(xt * xt, axis=0, keepdims=True) + EPS) * scale
            xt = xt * gain
        x1, x2 = xt[:ROPE_HALF], xt[ROPE_HALF:2 * ROPE_HALF]
        out = jnp.concatenate([x1 * cos - x2 * sin, x1 * sin + x2 * cos, xt[2 * ROPE_HALF:]], axis=0)
        return out if scale is None else out * r

    for slab in range(width // 128):
        u_scr[slab] = proj[:, slab * 128:(slab + 1) * 128]
        for j in range(S5_CHUNK):
            piece = u_scr[slab, pl.ds(j, tm // S5_CHUNK, stride=S5_CHUNK), :]
            col = (slab * S5_CHUNK + j) * 128
            u_ref[:, col:col + 128] = piece.astype(u_ref.dtype)
    q = proj[:, width:2 * width]
    qi = proj[:, 2 * width:3 * width]
    small = proj[:, 3 * width:3 * width + 256]
    qg = qg_ref[...]
    for pair in range(ATTN_HEADS // 2):
        sl = slice(pair * 128, (pair + 1) * 128)
        qt = q[:, sl].T
        qit = qi[:, sl].T
        for half in range(2):
            hs = slice(half * HEAD_DIM, (half + 1) * HEAD_DIM)
            q_ref[2 * pair + half] = head_t(qt[hs], qg, HEAD_DIM ** -0.5).astype(q_ref.dtype)
            qi_ref[2 * pair + half] = head_t(qit[hs], None, None).astype(qi_ref.dtype)
    kvt = small[:, :128].T
    kt = head_t(kvt[:HEAD_DIM], kg_ref[...], 1.0)
    kv = jnp.concatenate([kt, kvt[HEAD_DIM:]], axis=0).T
    k_ref[...] = kv[:, :HEAD_DIM].astype(k_ref.dtype)
    v_ref[...] = kv[:, HEAD_DIM:].astype(v_ref.dtype)
    kiwt = small[:, 128:256].T
    kit = jnp.concatenate([head_t(kiwt[:HEAD_DIM], None, None), kiwt[HEAD_DIM:]], axis=0).T
    ki_ref[...] = kit[:, :HEAD_DIM].astype(ki_ref.dtype)
    wscale = (IDX_HEADS ** -0.5) * (HEAD_DIM ** -0.5)
    wi_ref[...] = kiwt[HEAD_DIM:HEAD_DIM + IDX_HEADS] * wscale


def _in_projection(x2, g, mod, layer, w_in, positions, q_norm_g, k_norm_g, bsz, seq):
    t, d = x2.shape
    tm = 512
    width = ATTN_HEADS * HEAD_DIM
    cuts = np.cumsum([width, width, HEAD_DIM, HEAD_DIM, width, HEAD_DIM, IDX_HEADS])
    u_w, q_w, k_w, v_w, qi_w, ki_w, wi_w = jnp.split(w_in, cuts[:-1].tolist(), axis=1)
    pad = jnp.zeros((d, 128 - HEAD_DIM - IDX_HEADS), w_in.dtype)
    w = jnp.concatenate([u_w, q_w, qi_w, k_w, v_w, ki_w, wi_w, pad], axis=1).astype(BF16)
    nw = w.shape[1]
    inv = np.power(ROPE_THETA, -2.0 * np.arange(ROPE_HALF) / (2 * ROPE_HALF))
    inv = jnp.asarray(inv.reshape(ROPE_HALF, 1), F32)
    tiles_per_seq = seq // tm
    row = lambda i: (i, 0)
    const = lambda i: (0, 0)
    hm = lambda i: (i // tiles_per_seq, 0, 0, i % tiles_per_seq)
    outs = pl.pallas_call(
        functools.partial(_inproj_kernel, d=d),
        grid=(t // tm,),
        in_specs=[pl.BlockSpec((tm, d), row),
                  pl.BlockSpec((1, d), const),
                  pl.BlockSpec((None, None, 1, 3 * d), lambda i: (layer, i // tiles_per_seq, 0, 0)),
                  pl.BlockSpec((d, nw), const),
                  pl.BlockSpec((None, 1, tm), lambda i: (i, 0, 0)),
                  pl.BlockSpec((ROPE_HALF, 1), const),
                  pl.BlockSpec((HEAD_DIM, 1), const),
                  pl.BlockSpec((HEAD_DIM, 1), const)],
        out_specs=[pl.BlockSpec((tm // S5_CHUNK, width * S5_CHUNK), row),
                   pl.BlockSpec((None, ATTN_HEADS, HEAD_DIM, tm), hm),
                   pl.BlockSpec((None, IDX_HEADS, HEAD_DIM, tm), hm),
                   pl.BlockSpec((tm, HEAD_DIM), row),
                   pl.BlockSpec((tm, HEAD_DIM), row),
                   pl.BlockSpec((tm, HEAD_DIM), row),
                   pl.BlockSpec((IDX_HEADS, tm), lambda i: (0, i))],
        out_shape=[jax.ShapeDtypeStruct((t // S5_CHUNK, width * S5_CHUNK), BF16),
                   jax.ShapeDtypeStruct((bsz, ATTN_HEADS, HEAD_DIM, seq), BF16),
                   jax.ShapeDtypeStruct((bsz, IDX_HEADS, HEAD_DIM, seq), BF16),
                   jax.ShapeDtypeStruct((t, HEAD_DIM), BF16),
                   jax.ShapeDtypeStruct((t, HEAD_DIM), BF16),
                   jax.ShapeDtypeStruct((t, HEAD_DIM), BF16),
                   jax.ShapeDtypeStruct((IDX_HEADS, t), F32)],
        scratch_shapes=[pltpu.VMEM((width // 128, tm, 128), F32)],
        compiler_params=_cparams(("arbitrary",), VMEM_LIMIT),
        name="in_projection",
    )(x2, g.reshape(1, d), mod, w, positions.reshape(t // tm, 1, tm), inv,
      q_norm_g.reshape(HEAD_DIM, 1), k_norm_g.reshape(HEAD_DIM, 1))
    return outs


def _s5_weights(lam_re, lam_im, log_dt, b_re, b_im, c_re, c_im, d_skip, glu_w, glu_b):
    L = S5_CHUNK
    g, p = lam_re.shape
    hch = S5_GROUP
    dt = jnp.exp(log_dt)[:, None]
    lr, li = lam_re, lam_im
    tau = jnp.arange(L + 1, dtype=F32)[:, None, None]
    mag = jnp.exp(lr * dt * tau)
    pw_r, pw_i = mag * jnp.cos(li * dt * tau), mag * jnp.sin(li * dt * tau)
    ar, ai = pw_r[1], pw_i[1]
    nr, ni = ar - 1.0, ai
    den = lr * lr + li * li
    cr, ci = (nr * lr + ni * li) / den, (ni * lr - nr * li) / den
    bb_r = cr[..., None] * b_re - ci[..., None] * b_im
    bb_i = cr[..., None] * b_im + ci[..., None] * b_re
    cl_r = c_re[None] * pw_r[:L, :, None, :] - c_im[None] * pw_i[:L, :, None, :]
    cl_i = c_re[None] * pw_i[:L, :, None, :] + c_im[None] * pw_r[:L, :, None, :]
    taps = (jnp.einsum('tghp,gpk->tghk', cl_r, bb_r) - jnp.einsum('tghp,gpk->tghk', cl_i, bb_i))
    ns = g // 8
    eye = jnp.eye(8, dtype=F32)
    kd = jnp.einsum('tsgoh,gf->tsghfo', taps.reshape(L, ns, 8, hch, hch), eye).reshape(L, ns, 128, 128)
    zero = jnp.zeros_like(kd[0])
    k2 = jnp.stack([jnp.concatenate([jnp.concatenate([kd[2 * dd], kd[2 * dd + 1]], axis=-1),
                                     jnp.concatenate([kd[2 * dd - 1] if dd else zero, kd[2 * dd]], axis=-1)], axis=-2)
                    for dd in range(L // 2)], axis=1).astype(BF16)
    ii = jnp.arange(L)
    rev_r, rev_i = pw_r[L - 1 - ii], pw_i[L - 1 - ii]
    z_r = rev_r[..., None] * bb_r[None] - rev_i[..., None] * bb_i[None]
    z_i = rev_r[..., None] * bb_i[None] + rev_i[..., None] * bb_r[None]
    zc = jnp.concatenate([z_r, z_i], axis=2)
    w_z = jnp.einsum('jsgph,gf->sjghfp', zc.reshape(L, ns, 8, 2 * p, hch), eye)
    w_z = w_z.reshape(ns, L // 2, 2 * 128, 8 * 2 * p).astype(BF16)
    co_r = c_re[None] * pw_r[1:, :, None, :] - c_im[None] * pw_i[1:, :, None, :]
    co_i = c_re[None] * pw_i[1:, :, None, :] + c_im[None] * pw_r[1:, :, None, :]
    cc = jnp.concatenate([co_r, -co_i], axis=-1)
    w_c = jnp.einsum('isghp,gf->sgpifh', cc.reshape(L, ns, 8, hch, 2 * p), eye).reshape(ns, 8 * 2 * p, L * 128)
    w_ch = w_c.astype(BF16)
    w_cl = (w_c - w_ch.astype(F32)).astype(BF16)
    al_r, al_i = pw_r[L], pw_i[L]
    dec_a = jnp.concatenate([al_r, al_r], axis=-1).reshape(ns, 1, 8 * 2 * p)
    dec_b = jnp.concatenate([-al_i, al_i], axis=-1).reshape(ns, 1, 8 * 2 * p)
    d_t = jnp.tile(d_skip.reshape(ns, 1, 128), (1, 1, L))
    gl = jnp.einsum('sghk,gf->sghfk', glu_w.reshape(ns, 8, hch, hch), eye).reshape(ns, 128, 128)
    glu2 = jnp.einsum('ab,shk->sahbk', jnp.eye(2, dtype=F32), gl).reshape(ns, 256, 256).astype(BF16)
    glu_bt = jnp.tile(glu_b.reshape(ns, 1, 128), (1, 1, L))
    return k2, w_z, w_ch, w_cl, dec_a, dec_b, d_t, glu2, glu_bt


def _s5_kernel(u_ref, k2_ref, wz_ref, wch_ref, wcl_ref, da_ref, db_ref, d_ref, g2_ref, gb_ref, o_ref,
               zp_ref, zq_ref, sp_ref, *, nchunk):
    nblk = S5_CHUNK // 2
    ub = [u_ref[:, j * 256:(j + 1) * 256] for j in range(nblk)]
    z = _dot(ub[0], wz_ref[0])
    for j in range(1, nblk):
        z = z + _dot(ub[j], wz_ref[j])
    zp_ref[...] = z
    half = S5_STATE
    zq_ref[...] = jnp.concatenate([pltpu.roll(z[:, s * 2 * half:(s + 1) * 2 * half], half, 1)
                                   for s in range(z.shape[1] // (2 * half))], axis=1)
    da = da_ref[...]
    db = db_ref[...]
    dbq = -db

    def step(c, carry):
        sp, sq = carry
        sp_ref[pl.ds(c, 1), :] = sp
        sp_new = da * sp + db * sq + zp_ref[pl.ds(c, 1), :]
        sq_new = da * sq + dbq * sp + zq_ref[pl.ds(c, 1), :]
        return sp_new, sq_new

    zero = jnp.zeros((1, z.shape[1]), F32)
    lax.fori_loop(0, nchunk, step, (zero, zero))
    sh, sl = _split_bf16(sp_ref[...])
    wch = wch_ref[...]
    y_inter = _dot(sh, wch) + (_dot(sh, wcl_ref[...]) + _dot(sl, wch))
    for i in range(nblk):
        cs = slice(i * 256, (i + 1) * 256)
        acc = y_inter[:, cs] + d_ref[:, cs] * ub[i].astype(F32)
        for j in range(i + 1):
            acc = acc + _dot(ub[j], k2_ref[i - j])
        y = jax.nn.gelu(acc)
        y = y * jax.nn.sigmoid(_dot(y.astype(BF16), g2_ref[...]) + gb_ref[:, cs])
        for a in range(2):
            o_ref[pl.ds(2 * i + a, nchunk, stride=S5_CHUNK), :] = y[:, a * 128:(a + 1) * 128]


def _s5_mixer(u2, weights, bsz, seq):
    k2, w_z, w_ch, w_cl, dec_a, dec_b, d_t, glu2, glu_bt = weights
    ns = k2.shape[0]
    nchunk = seq // S5_CHUNK
    cols = S5_CHUNK * 128
    st = w_z.shape[-1]
    slab3 = lambda b, s: (s, 0, 0)
    slab4 = lambda b, s: (s, 0, 0, 0)
    return pl.pallas_call(
        functools.partial(_s5_kernel, nchunk=nchunk),
        grid=(bsz, ns),
        in_specs=[pl.BlockSpec((nchunk, cols), lambda b, s: (b, s)),
                  pl.BlockSpec((None,) + k2.shape[1:], slab4),
                  pl.BlockSpec((None,) + w_z.shape[1:], slab4),
                  pl.BlockSpec((None, st, cols), slab3),
                  pl.BlockSpec((None, st, cols), slab3),
                  pl.BlockSpec((None, 1, st), slab3),
                  pl.BlockSpec((None, 1, st), slab3),
                  pl.BlockSpec((None, 1, cols), slab3),
                  pl.BlockSpec((None, 256, 256), slab3),
                  pl.BlockSpec((None, 1, cols), slab3)],
        out_specs=pl.BlockSpec((seq, 128), lambda b, s: (b, s)),
        out_shape=jax.ShapeDtypeStruct((bsz * seq, ns * 128), F32),
        scratch_shapes=[pltpu.VMEM((nchunk, st), F32), pltpu.VMEM((nchunk, st), F32), pltpu.VMEM((nchunk, st), F32)],
        compiler_params=_cparams(("arbitrary", "arbitrary"), VMEM_LIMIT),
        name="s5_mixer",
    )(u2, k2, w_z, w_ch, w_cl, dec_a, dec_b, d_t, glu2, glu_bt)


def _dsa_kernel(qt_ref, qit_ref, wt_ref, k_ref, vt_ref, ki_ref, o_ref,
                sc_ref, s_ref, m_ref, l_ref, acc_ref, *, tq, ck, k_sel, idx_bits, max_iter):
    i = pl.program_id(1)
    nheads = qt_ref.shape[0]
    n_ck = (i * tq + tq + ck - 1) // ck
    t_pos = i * tq + lax.broadcasted_iota(jnp.int32, (1, tq), 1)
    krow = lax.broadcasted_iota(jnp.int32, (ck, 1), 0)
    kf = float(k_sel)
    inf = jnp.inf

    fr = 32
    sub = min(32768 // tq, ck)

    def fold(x, op):
        return op(x.reshape(ck // fr, fr, tq), axis=0)

    wt = wt_ref[...]

    def score_chunk(c, carry):
        lo8, hi8 = carry
        for part in range(ck // sub):
            off = pl.multiple_of(c * ck + part * sub, sub)
            kic = ki_ref[pl.ds(off, sub), :]
            acc = jnp.zeros((sub, tq), F32)
            for h in range(nheads):
                acc = acc + wt[h:h + 1, :] * jnp.maximum(_dot(kic, qit_ref[h]), 0.0)
            acc = acc + 0.0
            vis = off + krow[:sub] <= t_pos
            sc_ref[c, part * sub:(part + 1) * sub, :] = jnp.where(vis, acc, -inf)
            hi8 = jnp.maximum(hi8, jnp.max(jnp.where(vis, acc, -inf).reshape(sub // fr, fr, tq), axis=0))
            lo8 = jnp.minimum(lo8, jnp.min(jnp.where(vis, acc, inf).reshape(sub // fr, fr, tq), axis=0))
        return lo8, hi8

    lo8, hi8 = lax.fori_loop(0, n_ck, score_chunk, (jnp.full((fr, tq), inf, F32), jnp.full((fr, tq), -inf, F32)))
    lo = jnp.min(lo8, axis=0, keepdims=True)
    hi = jnp.max(hi8, axis=0, keepdims=True)
    n_vis = (t_pos + 1).astype(F32)
    few = n_vis < kf

    def reduce_keys(fn, init, combine, op):
        def body(c, part):
            return combine(part, fold(fn(sc_ref[c], c * ck), op))
        return op(lax.fori_loop(0, n_ck, body, jnp.full((fr, tq), init, F32)), axis=0, keepdims=True)

    def count(pred):
        return reduce_keys(lambda x, off: jnp.where(pred(x, off), 1.0, 0.0), 0.0, jnp.add, jnp.sum)

    def bisect(lo_, hi_, c_lo):
        mid = lo_ + (hi_ - lo_) * 0.5
        cnt = count(lambda x, off: x >= mid)
        up = cnt >= kf
        return jnp.where(up, mid, lo_), jnp.where(up, hi_, mid), jnp.where(up, cnt, c_lo)

    def status(lo_, hi_, c_lo):
        mid = lo_ + (hi_ - lo_) * 0.5
        unresolved = jnp.where(few, 0.0, jnp.where(c_lo == kf, 0.0, 1.0))
        movable = jnp.where(mid > lo_, jnp.where(mid < hi_, 1.0, 0.0), 0.0)
        return jnp.max(unresolved), jnp.max(unresolved * movable)

    def search_body(st):
        it, lo_, hi_, c_lo, _, _ = st
        for _ in range(2):
            lo_, hi_, c_lo = bisect(lo_, hi_, c_lo)
        return (it + 2, lo_, hi_, c_lo) + status(lo_, hi_, c_lo)

    st = lax.while_loop(lambda st: jnp.logical_and(st[0] < max_iter, st[5] > 0.5), search_body,
                        (jnp.int32(0), lo, hi, n_vis) + status(lo, hi, n_vis))
    lo, open_ = st[1], st[4]

    def resolve():
        def walk(thr_, strict):
            keep = (lambda x: x > thr_) if strict else (lambda x: x >= thr_)
            return reduce_keys(lambda x, off: jnp.where(keep(x), x, inf), inf, jnp.minimum, jnp.min)

        def above_equal(thr_):
            return count(lambda x, off: x > thr_), count(lambda x, off: x == thr_)

        def too_low(n_gt):
            return jnp.where(few, 0.0, jnp.where(n_gt >= kf, 1.0, 0.0))

        thr0 = walk(lo, False)

        def walk_body(st):
            thr_, n_gt, _ = st
            thr2 = jnp.where(too_low(n_gt) > 0.5, walk(thr_, True), thr_)
            return (thr2,) + above_equal(thr2)

        thr_, n_gt, n_eq = lax.while_loop(lambda st: jnp.max(too_low(st[1])) > 0.5, walk_body,
                                          (thr0,) + above_equal(thr0))
        need = kf - n_gt

        def tie_cut():
            def mark_equal(c, _):
                s_ref[0, c] = jnp.where(sc_ref[c] == thr_, 1.0, 0.0)
                return 0

            lax.fori_loop(0, n_ck, mark_equal, 0)

            def idx_step(b, cur):
                trial = cur | (jnp.int32(1) << (idx_bits - 1 - b))

                def body(c, part):
                    return part + fold(jnp.where(c * ck + krow < trial, s_ref[0, c], 0.0), jnp.sum)
                cnt = jnp.sum(lax.fori_loop(0, n_ck, body, jnp.zeros((fr, tq), F32)), axis=0, keepdims=True)
                return jnp.where(cnt < need, trial, cur)

            return lax.fori_loop(0, idx_bits, idx_step, jnp.zeros((1, tq), jnp.int32))

        tied = jnp.max(jnp.where(few, 0.0, jnp.where(n_eq > need, 1.0, 0.0))) > 0.5
        return thr_, lax.cond(tied, tie_cut, lambda: jnp.full((1, tq), 0x7FFFFFFF, jnp.int32))

    thr, cut = lax.cond(open_ > 0.5, resolve, lambda: (lo, jnp.full((1, tq), 0x7FFFFFFF, jnp.int32)))

    for h in range(nheads):
        m_ref[h] = jnp.full((fr, tq), NEG_BIG, F32)
        l_ref[h] = jnp.zeros((fr, tq), F32)
        acc_ref[h] = jnp.zeros((HEAD_DIM, tq), F32)

    def attn_scores(c, _):
        off = pl.multiple_of(c * ck, ck)
        kc = k_ref[pl.ds(off, ck), :]
        x = sc_ref[c]
        at_thr = jnp.where(x == thr, jnp.where(off + krow <= cut, 0.0, NEG_BIG), NEG_BIG)
        bias = jnp.where(x > thr, 0.0, at_thr)
        for h in range(nheads):
            s = _dot(kc, qt_ref[h]) + bias
            s_ref[h, c] = s
            m_ref[h] = jnp.maximum(m_ref[h], fold(s, jnp.max))
        return 0

    lax.fori_loop(0, n_ck, attn_scores, 0)
    for h in range(nheads):
        m_ref[h] = jnp.broadcast_to(jnp.max(m_ref[h], axis=0, keepdims=True), (fr, tq))

    def attn_values(c, _):
        vt = vt_ref[c]
        for h in range(nheads):
            p = jnp.exp(s_ref[h, c] - m_ref[h][:1, :])
            l_ref[h] += fold(p, jnp.sum)
            acc_ref[h] += _dot(vt, p.astype(BF16))
        return 0

    lax.fori_loop(0, n_ck, attn_values, 0)
    for h in range(nheads):
        l_row = jnp.sum(l_ref[h], axis=0, keepdims=True)
        o_ref[h * HEAD_DIM:(h + 1) * HEAD_DIM, :] = (acc_ref[h] / l_row).astype(o_ref.dtype)


def _dsa_attention(qt, qit, wt, k, v, ki, bsz, seq):
    tq = 256
    ck = min(512, seq)
    n_ck = seq // ck
    k_sel = min(TOPK_MAX, seq // 4)
    nq = seq // tq
    nh = ATTN_HEADS
    width = nh * HEAD_DIM
    vt = v.reshape(bsz, n_ck, ck, HEAD_DIM).transpose(0, 1, 3, 2)
    qspec = pl.BlockSpec((None, nh, HEAD_DIM, tq), lambda b, i: (b, 0, 0, i))
    out_t = pl.pallas_call(
        functools.partial(_dsa_kernel, tq=tq, ck=ck, k_sel=k_sel, idx_bits=(seq - 1).bit_length(), max_iter=22),
        grid=(bsz, nq),
        in_specs=[qspec, qspec,
                  pl.BlockSpec((IDX_HEADS, tq), lambda b, i: (0, b * nq + i)),
                  pl.BlockSpec((seq, HEAD_DIM), lambda b, i: (b, 0)),
                  pl.BlockSpec((None, n_ck, HEAD_DIM, ck), lambda b, i: (b, 0, 0, 0)),
                  pl.BlockSpec((seq, HEAD_DIM), lambda b, i: (b, 0))],
        out_specs=pl.BlockSpec((None, width, tq), lambda b, i: (b, 0, i)),
        out_shape=jax.ShapeDtypeStruct((bsz, width, seq), BF16),
        scratch_shapes=[pltpu.VMEM((n_ck, ck, tq), F32),
                        pltpu.VMEM((nh, n_ck, ck, tq), F32),
                        pltpu.VMEM((nh, 32, tq), F32),
                        pltpu.VMEM((nh, 32, tq), F32),
                        pltpu.VMEM((nh, HEAD_DIM, tq), F32)],
        compiler_params=_cparams(("arbitrary", "arbitrary"), VMEM_LIMIT),
        name="dsa_attention",
    )(qt, qit, wt, k, vt, ki)
    return out_t.transpose(0, 2, 1).reshape(bsz * seq, width)


def _outproj_kernel(x_ref, ys_ref, ya_ref, w_ref, mod_ref, o_ref, *, d):
    width = ys_ref.shape[1]
    y = _dot(ys_ref[...].astype(BF16), w_ref[:width, :]) + _dot(ya_ref[...], w_ref[width:, :])
    o_ref[...] = x_ref[...] + mod_ref[...][:, 2 * d:] * y


def _out_projection(x2, y_ssm, y_att, w_out, mod, layer, seq):
    t, d = x2.shape
    tm = 512
    width = y_ssm.shape[1]
    tiles_per_seq = seq // tm
    row = lambda i: (i, 0)
    return pl.pallas_call(
        functools.partial(_outproj_kernel, d=d),
        grid=(t // tm,),
        in_specs=[pl.BlockSpec((tm, d), row),
                  pl.BlockSpec((tm, width), row),
                  pl.BlockSpec((tm, width), row),
                  pl.BlockSpec((2 * width, d), lambda i: (0, 0)),
                  pl.BlockSpec((None, None, 1, 3 * d), lambda i: (layer, i // tiles_per_seq, 0, 0))],
        out_specs=pl.BlockSpec((tm, d), row),
        out_shape=jax.ShapeDtypeStruct((t, d), F32),
        compiler_params=_cparams(("arbitrary",)),
        name="out_projection",
    )(x2, y_ssm, y_att, w_out.astype(BF16), mod)


def _pool_kernel(x_ref, halo_ref, g_ref, mod_ref, pw_ref, ps_ref, o_ref, h_ref, *, d, tiles_per_seq):
    i = pl.program_id(0)
    tm = x_ref.shape[0]
    mod = mod_ref[...]
    shift, scale, gate = mod[:, :d], mod[:, d:2 * d], mod[:, 2 * d:]
    x = x_ref[...]
    first = (i % tiles_per_seq) == 0
    h_ref[POOL_HALO:, :] = _norm_mod(x, g_ref[...], shift, scale)
    halo = _norm_mod(halo_ref[...], g_ref[...], shift, scale)
    h_ref[:POOL_HALO, :] = jnp.where(first, 0.0, halo)
    pos = (i % tiles_per_seq) * tm + lax.broadcasted_iota(jnp.int32, (tm, 1), 0)
    gw = d // len(POOL_WINDOWS)
    ys = []
    for gi, win in enumerate(POOL_WINDOWS):
        cs = slice(gi * gw, (gi + 1) * gw)
        cur = h_ref[POOL_HALO:, cs]
        tot = cur
        for k in range(1, win):
            tot = tot + h_ref[POOL_HALO - k:POOL_HALO - k + tm, cs]
        cnt = jnp.minimum(pos + 1, win).astype(F32)
        pooled = tot / cnt - cur
        ys.append(_dot(pooled.astype(BF16), pw_ref[gi]))
    y = jnp.concatenate(ys, axis=1) * ps_ref[...]
    o_ref[...] = x + gate * y


def _pool_layer(x2, g, mod, layer, pool_w, pool_scale, seq):
    t, d = x2.shape
    tm = 512
    tiles_per_seq = seq // tm
    gw = d // len(POOL_WINDOWS)
    hb = tm // POOL_HALO
    return pl.pallas_call(
        functools.partial(_pool_kernel, d=d, tiles_per_seq=tiles_per_seq),
        grid=(t // tm,),
        in_specs=[pl.BlockSpec((tm, d), lambda i: (i, 0)),
                  pl.BlockSpec((POOL_HALO, d), lambda i: (jnp.maximum(i * hb - 1, 0), 0)),
                  pl.BlockSpec((1, d), lambda i: (0, 0)),
                  pl.BlockSpec((None, None, 1, 3 * d), lambda i: (layer, i // tiles_per_seq, 0, 0)),
                  pl.BlockSpec((len(POOL_WINDOWS), gw, gw), lambda i: (0, 0, 0)),
                  pl.BlockSpec((1, d), lambda i: (0, 0))],
        out_specs=pl.BlockSpec((tm, d), lambda i: (i, 0)),
        out_shape=jax.ShapeDtypeStruct((t, d), F32),
        scratch_shapes=[pltpu.VMEM((tm + POOL_HALO, d), F32)],
        compiler_params=_cparams(("arbitrary",)),
        name="pool_mixer",
    )(x2, x2, g.reshape(1, d), mod, pool_w.astype(BF16), pool_scale.reshape(1, d))


def _first_max(vals, idx, big):
    m = jnp.max(vals, axis=0, keepdims=True)
    first = jnp.min(jnp.where(vals == m, idx, big), axis=0, keepdims=True)
    return m, first


def _pack_bf16_pairs(x):
    n = x.shape[1] // 2
    lo = pltpu.bitcast(x[:, :n].astype(BF16).astype(F32), jnp.int32)
    hi = pltpu.bitcast(x[:, n:].astype(BF16).astype(F32), jnp.int32)
    return hi | ((lo >> 16) & 0xFFFF)


def _unpack_bf16_pairs(w):
    lo = pltpu.bitcast(w << 16, F32)
    hi = pltpu.bitcast(w & jnp.int32(-65536), F32)
    return jnp.concatenate([lo, hi], axis=1)


def _router_kernel(x_ref, g_ref, mod_ref, rw_ref, rb_ref, tri_ref, h_ref, gate_ref, rank_ref, cnt_ref,
                   run_ref, *, d, n_exp):
    mod = mod_ref[...]
    h = _norm_mod(x_ref[...], g_ref[...], mod[:, :d], mod[:, d:2 * d])
    h_ref[...] = _pack_bf16_pairs(h)
    tm = h.shape[0]
    hh, hl = _split_bf16(h)
    rw = rw_ref[...]
    rh, rl = _split_bf16(rw)
    logits = _dot_nt(rh, hh) + (_dot_nt(rh, hl) + _dot_nt(rl, hh))
    scores = jax.nn.sigmoid(logits)
    sel = scores + rb_ref[...]
    gsz = n_exp // N_EXPERT_GROUPS
    neg = -jnp.inf
    shape3 = (N_EXPERT_GROUPS, gsz, tm)
    sel3 = sel.reshape(shape3)
    sub = lax.broadcasted_iota(jnp.int32, shape3, 1)
    m1 = jnp.max(sel3, axis=1, keepdims=True)
    f1 = jnp.min(jnp.where(sel3 == m1, sub, gsz), axis=1, keepdims=True)
    m2 = jnp.max(jnp.where(sub == f1, neg, sel3), axis=1, keepdims=True)
    gscore = jnp.broadcast_to(m1 + m2, shape3).reshape(n_exp, tm)
    eidx = lax.broadcasted_iota(jnp.int32, (n_exp, tm), 0)
    gidx = eidx // gsz
    keep = jnp.zeros((n_exp, tm), F32)
    work = gscore
    for _ in range(TOPK_GROUPS):
        _, first = _first_max(work, gidx, N_EXPERT_GROUPS)
        hit = gidx == first
        keep = jnp.where(hit, 1.0, keep)
        work = jnp.where(hit, neg, work)
    work = jnp.where(keep > 0.0, sel, neg)
    chosen = jnp.zeros((n_exp, tm), F32)
    for _ in range(TOP_K):
        _, first = _first_max(work, eidx, n_exp)
        hit = eidx == first
        chosen = jnp.where(hit, 1.0, chosen)
        work = jnp.where(hit, neg, work)
    picked = chosen * scores
    gate_ref[...] = picked / jnp.sum(picked, axis=0, keepdims=True) * ROUTED_SCALE

    @pl.when(pl.program_id(0) == 0)
    def _():
        run_ref[...] = jnp.zeros(run_ref.shape, F32)

    before = _dot(chosen.astype(BF16), tri_ref[...])
    run = run_ref[...]
    rank_ref[...] = jnp.where(chosen > 0.0, before + run[:, :1], -1.0)
    run = run + jnp.broadcast_to(jnp.sum(chosen, axis=1, keepdims=True), run.shape)
    run_ref[...] = run
    cnt_ref[...] = run


def _ffn_router(x2, g, mod, layer, router_w, router_bias, seq):
    t, d = x2.shape
    n_exp = router_w.shape[1]
    tm = 512
    tiles_per_seq = seq // tm
    tri = (np.arange(tm)[:, None] < np.arange(tm)[None, :]).astype(np.float32)
    return pl.pallas_call(
        functools.partial(_router_kernel, d=d, n_exp=n_exp),
        grid=(t // tm,),
        in_specs=[pl.BlockSpec((tm, d), lambda i: (i, 0)),
                  pl.BlockSpec((1, d), lambda i: (0, 0)),
                  pl.BlockSpec((None, None, 1, 3 * d), lambda i: (layer, i // tiles_per_seq, 0, 0)),
                  pl.BlockSpec((n_exp, d), lambda i: (0, 0)),
                  pl.BlockSpec((n_exp, 1), lambda i: (0, 0)),
                  pl.BlockSpec((tm, tm), lambda i: (0, 0))],
        out_specs=[pl.BlockSpec((tm, d // 2), lambda i: (i, 0)),
                   pl.BlockSpec((n_exp, tm), lambda i: (0, i)),
                   pl.BlockSpec((n_exp, tm), lambda i: (0, i)),
                   pl.BlockSpec((n_exp, 128), lambda i: (0, 0))],
        out_shape=[jax.ShapeDtypeStruct((t, d // 2), jnp.int32),
                   jax.ShapeDtypeStruct((n_exp, t), F32),
                   jax.ShapeDtypeStruct((n_exp, t), F32),
                   jax.ShapeDtypeStruct((n_exp, 128), F32)],
        scratch_shapes=[pltpu.VMEM((n_exp, 128), F32)],
        compiler_params=_cparams(("arbitrary",)),
        name="ffn_router",
    )(x2, g.reshape(1, d), mod, router_w.T, router_bias.reshape(n_exp, 1), jnp.asarray(tri, BF16))


def _assign_kernel(rank_ref, gate_ref, start_ref, pos_ref, w_ref, *, n_exp):
    rank = rank_ref[...]
    gates = gate_ref[...]
    tm = rank.shape[1]
    slot = rank + start_ref[...]
    eidx = lax.broadcasted_iota(jnp.int32, (n_exp, tm), 0).astype(F32)
    alive = jnp.where(rank >= 0.0, eidx, float(n_exp))
    kidx = lax.broadcasted_iota(jnp.int32, (TOP_K, tm), 0)
    pos = jnp.zeros((TOP_K, tm), F32)
    wts = jnp.zeros((TOP_K, tm), F32)
    for k in range(TOP_K):
        first = jnp.min(alive, axis=0, keepdims=True)
        hit = alive == first
        pos_k = jnp.sum(jnp.where(hit, slot, 0.0), axis=0, keepdims=True)
        w_k = jnp.sum(jnp.where(hit, gates, 0.0), axis=0, keepdims=True)
        pos = jnp.where(kidx == k, pos_k, pos)
        wts = jnp.where(kidx == k, w_k, wts)
        alive = jnp.where(hit, float(n_exp), alive)
    pos_ref[...] = pos.astype(jnp.int32)
    w_ref[...] = wts


def _assign_slots(ranks, gates, start):
    n_exp, t = ranks.shape
    tm = 512
    return pl.pallas_call(
        functools.partial(_assign_kernel, n_exp=n_exp),
        grid=(t // tm,),
        in_specs=[pl.BlockSpec((n_exp, tm), lambda i: (0, i)),
                  pl.BlockSpec((n_exp, tm), lambda i: (0, i)),
                  pl.BlockSpec((n_exp, 1), lambda i: (0, 0))],
        out_specs=[pl.BlockSpec((TOP_K, tm), lambda i: (0, i)),
                   pl.BlockSpec((TOP_K, tm), lambda i: (0, i))],
        out_shape=[jax.ShapeDtypeStruct((TOP_K, t), jnp.int32), jax.ShapeDtypeStruct((TOP_K, t), F32)],
        compiler_params=_cparams(("arbitrary",)),
        name="moe_assign",
    )(ranks, gates, start)


SC_CORES = 2
SC_SUBCORES = 16
SC_WINDOW = 128


def _sc_mesh():
    return plsc.VectorSubcoreMesh(core_axis_name="c", subcore_axis_name="s",
                                  num_cores=SC_CORES, num_subcores=SC_SUBCORES)


def _sc_scatter_rows(rows, pos, n_out):
    t, width = rows.shape
    nk = pos.shape[0]
    win = SC_WINDOW
    n_win = t // win // (SC_CORES * SC_SUBCORES)
    pos_w = pos.reshape(nk, t // win, win).transpose(1, 0, 2)

    def body(rows_hbm, pos_hbm, out_hbm, idx_v, buf_v, sem):
        wid = lax.axis_index("s") * SC_CORES + lax.axis_index("c")

        @pl.loop(0, n_win)
        def _(j):
            w = wid * n_win + j
            pltpu.sync_copy(rows_hbm.at[pl.ds(w * win, win)], buf_v)
            pltpu.sync_copy(pos_hbm.at[w], idx_v)
            copies = [pltpu.make_async_copy(buf_v, out_hbm.at[idx_v.at[k]], sem) for k in range(nk)]
            for cp in copies:
                cp.start()
            for cp in copies:
                cp.wait()

    return pl.kernel(
        body, out_type=jax.ShapeDtypeStruct((n_out, width), jnp.int32), mesh=_sc_mesh(),
        scratch_types=[pltpu.VMEM((nk, win), jnp.int32), pltpu.VMEM((win, width), jnp.int32),
                       pltpu.SemaphoreType.DMA],
        name="sc_scatter_rows",
    )(rows, pos_w)


SC_GATHER_WINDOW = 64


def _sc_gather_rows(table, pos_flat):
    m = pos_flat.shape[0]
    width = table.shape[1]
    win = SC_GATHER_WINDOW
    workers = SC_CORES * SC_SUBCORES
    n_win = m // win // workers
    pos_w = pos_flat.reshape(workers, n_win, win)

    def body(table_hbm, pos_hbm, out_hbm, idx_v, buf_v, sem_g, sem_w):
        wid = lax.axis_index("s") * SC_CORES + lax.axis_index("c")
        base = wid * (n_win * win)
        pltpu.sync_copy(pos_hbm.at[wid], idx_v)

        def gather(j, b):
            return pltpu.make_async_copy(table_hbm.at[idx_v.at[j]], buf_v.at[b], sem_g.at[b])

        def write(j, b):
            return pltpu.make_async_copy(buf_v.at[b], out_hbm.at[pl.ds(base + j * win, win)], sem_w.at[b])

        gather(0, 0).start()

        @pl.loop(0, n_win, step=2)
        def _(j0):
            for b in range(2):
                j = j0 + b
                gather(j, b).wait()
                write(j, b).start()

                @pl.when(j + 1 < n_win)
                def _():
                    @pl.when(j >= 1)
                    def _():
                        write(j - 1, 1 - b).wait()
                    gather(j + 1, 1 - b).start()

        write(n_win - 2, 0).wait()
        write(n_win - 1, 1).wait()

    return pl.kernel(
        body, out_type=jax.ShapeDtypeStruct((m, width), jnp.int32), mesh=_sc_mesh(),
        scratch_types=[pltpu.VMEM((n_win, win), jnp.int32), pltpu.VMEM((2, win, width), jnp.int32),
                       pltpu.SemaphoreType.DMA((2,)), pltpu.SemaphoreType.DMA((2,))],
        name="sc_gather_rows",
    )(table, pos_w)


MOE_BLOCK = 512


def _silu_mul(a, b):
    return (a * jax.nn.sigmoid(a)) * b


def _expert_ffn_kernel(be_ref, nv_ref, xs_ref, wg_ref, wu_ref, wd_ref, ys_ref, wg_bf, wu_bf, wd_bf):
    i = pl.program_id(0)
    fresh = jnp.logical_or(i == 0, be_ref[i] != be_ref[jnp.maximum(i - 1, 0)])

    @pl.when(fresh)
    def _():
        wg_bf[...] = wg_ref[...].astype(BF16)
        wu_bf[...] = wu_ref[...].astype(BF16)
        wd_bf[...] = wd_ref[...].astype(BF16)

    nv = nv_ref[i]

    @pl.when(nv > 0)
    def _():
        x = _unpack_bf16_pairs(xs_ref[...])
        row = lax.broadcasted_iota(jnp.int32, (x.shape[0], 1), 0)
        x = jnp.where(row < nv, x, 0.0).astype(BF16)
        mid = _silu_mul(_dot(x, wg_bf[...]), _dot(x, wu_bf[...]))
        ys_ref[...] = _pack_bf16_pairs(_dot(mid.astype(BF16), wd_bf[...]))

    @pl.when(nv <= 0)
    def _():
        ys_ref[...] = jnp.zeros(ys_ref.shape, ys_ref.dtype)


def _expert_ffn(xs, block_expert, block_rows, layer, w_gate, w_up, w_down):
    n_rows, half = xs.shape
    _, n_exp, d, de = w_gate.shape
    nb = n_rows // MOE_BLOCK
    wmap = lambda i, be, nv: (layer, be[i], 0, 0)
    grid_spec = pltpu.PrefetchScalarGridSpec(
        num_scalar_prefetch=2,
        grid=(nb,),
        in_specs=[pl.BlockSpec((MOE_BLOCK, half), lambda i, be, nv: (i, 0)),
                  pl.BlockSpec((None, None, d, de), wmap),
                  pl.BlockSpec((None, None, d, de), wmap),
                  pl.BlockSpec((None, None, de, d), wmap)],
        out_specs=pl.BlockSpec((MOE_BLOCK, half), lambda i, be, nv: (i, 0)),
        scratch_shapes=[pltpu.VMEM((d, de), BF16), pltpu.VMEM((d, de), BF16), pltpu.VMEM((de, d), BF16)],
    )
    return pl.pallas_call(
        _expert_ffn_kernel,
        grid_spec=grid_spec,
        out_shape=jax.ShapeDtypeStruct((n_rows, half), jnp.int32),
        compiler_params=_cparams(("arbitrary",)),
        name="moe_expert_ffn",
    )(block_expert, block_rows, xs, w_gate, w_up, w_down)


def _combine_kernel(yt_ref, wt_ref, hp_ref, x_ref, mod_ref, sg_ref, su_ref, sd_ref, o_ref, *, d):
    h = _unpack_bf16_pairs(hp_ref[...]).astype(BF16)
    acc = _dot(_silu_mul(_dot(h, sg_ref[...]), _dot(h, su_ref[...])).astype(BF16), sd_ref[...])
    w = wt_ref[...]
    for k in range(TOP_K):
        acc = acc + w[:, k:k + 1] * _unpack_bf16_pairs(yt_ref[k])
    o_ref[...] = x_ref[...] + mod_ref[...][:, 2 * d:] * acc


def _combine(y_tok, w_tok, h_packed, x2, mod, layer, sh_gate, sh_up, sh_down, seq):
    t, d = x2.shape
    ds = sh_gate.shape[1]
    tm = 512
    tiles_per_seq = seq // tm
    row = lambda i: (i, 0)
    const = lambda i: (0, 0)
    return pl.pallas_call(
        functools.partial(_combine_kernel, d=d),
        grid=(t // tm,),
        in_specs=[pl.BlockSpec((TOP_K, tm, d // 2), lambda i: (0, i, 0)),
                  pl.BlockSpec((tm, TOP_K), row),
                  pl.BlockSpec((tm, d // 2), row),
                  pl.BlockSpec((tm, d), row),
                  pl.BlockSpec((None, None, 1, 3 * d), lambda i: (layer, i // tiles_per_seq, 0, 0)),
                  pl.BlockSpec((d, ds), const),
                  pl.BlockSpec((d, ds), const),
                  pl.BlockSpec((ds, d), const)],
        out_specs=pl.BlockSpec((tm, d), row),
        out_shape=jax.ShapeDtypeStruct((t, d), F32),
        compiler_params=_cparams(("arbitrary",), VMEM_LIMIT),
        name="moe_combine",
    )(y_tok, w_tok, h_packed, x2, mod, sh_gate.astype(BF16), sh_up.astype(BF16), sh_down.astype(BF16))


def _moe_layer(x2, h_packed, gates, ranks, counts, mod, layer, w_gate, w_up, w_down, sh_gate, sh_up, sh_down, seq):
    t, d = x2.shape
    n_exp = w_gate.shape[1]
    cnt = counts[:, 0].astype(jnp.int32)
    padded = (cnt + MOE_BLOCK - 1) // MOE_BLOCK * MOE_BLOCK
    eidx = jnp.arange(n_exp, dtype=jnp.int32)
    seg_end = jnp.sum(jnp.where(eidx[None, :] <= eidx[:, None], padded[None, :], 0), axis=1)
    seg_start = seg_end - padded
    nb = t * TOP_K // MOE_BLOCK + n_exp
    first_row = jnp.arange(nb, dtype=jnp.int32) * MOE_BLOCK
    block_expert = jnp.sum((seg_end[None, :] <= first_row[:, None]).astype(jnp.int32), axis=1)
    block_expert = jnp.minimum(block_expert, n_exp - 1)
    last_row = jnp.sum(jnp.where(block_expert[:, None] == eidx[None, :], (seg_start + cnt)[None, :], 0), axis=1)
    block_rows = jnp.clip(last_row - first_row, 0, MOE_BLOCK).astype(jnp.int32)
    pos, w = _assign_slots(ranks, gates, seg_start.astype(F32).reshape(n_exp, 1))
    xs = _sc_scatter_rows(h_packed, pos, nb * MOE_BLOCK)
    ys = _expert_ffn(xs, block_expert, block_rows, layer, w_gate, w_up, w_down)
    y_tok = _sc_gather_rows(ys, pos.reshape(TOP_K * t)).reshape(TOP_K, t, d // 2)
    return _combine(y_tok, w.T, h_packed, x2, mod, layer, sh_gate, sh_up, sh_down, seq)


def kernel(x, c, positions, mix_norm_g, mix_mod_w, mix_mod_b, ffn_norm_g, ffn_mod_w, ffn_mod_b,
           hyb_w_in, hyb_w_out, s5_lambda_re, s5_lambda_im, s5_log_dt, s5_b_re, s5_b_im,
           s5_c_re, s5_c_im, s5_d, s5_glu_w, s5_glu_b, attn_q_norm_g, attn_k_norm_g,
           pool_w, pool_scale, router_w, router_bias, exp_w_gate, exp_w_up, exp_w_down,
           sh_w_gate, sh_w_up, sh_w_down):
    bsz, seq, d = x.shape
    t = bsz * seq
    depth = mix_norm_g.shape[0]
    x2 = x.reshape(t, d)
    mix_mod = _mod_vectors(c, mix_mod_w, mix_mod_b).reshape(depth, bsz, 1, 3 * d)
    ffn_mod = _mod_vectors(c, ffn_mod_w, ffn_mod_b).reshape(depth, bsz, 1, 3 * d)
    for i in range(depth):
        j = i // 2
        if i % 2 == 0:
            u, q_hm, qi_hm, k, v, ki, wi = _in_projection(
                x2, mix_norm_g[i], mix_mod, i, hyb_w_in[j], positions, attn_q_norm_g[j], attn_k_norm_g[j], bsz, seq)
            weights = _s5_weights(s5_lambda_re[j], s5_lambda_im[j], s5_log_dt[j], s5_b_re[j], s5_b_im[j],
                                  s5_c_re[j], s5_c_im[j], s5_d[j], s5_glu_w[j], s5_glu_b[j])
            y_ssm = _s5_mixer(u, weights, bsz, seq)
            y_att = _dsa_attention(q_hm, qi_hm, wi, k, v, ki, bsz, seq)
            x2 = _out_projection(x2, y_ssm, y_att, hyb_w_out[j], mix_mod, i, seq)
        else:
            x2 = _pool_layer(x2, mix_norm_g[i], mix_mod, i, pool_w[j], pool_scale[j], seq)
        h_packed, gates, ranks, counts = _ffn_router(x2, ffn_norm_g[i], ffn_mod, i, router_w[i], router_bias[i], seq)
        x2 = _moe_layer(x2, h_packed, gates, ranks, counts, ffn_mod, i, exp_w_gate, exp_w_up, exp_w_down,
                        sh_w_gate[i], sh_w_up[i], sh_w_down[i], seq)
    return x2.reshape(bsz, seq, d)
```

```python
import functools

import numpy as np
import jax
import jax.numpy as jnp
from jax import lax
from jax.experimental import pallas as pl
from jax.experimental.pallas import tpu as pltpu
from jax.experimental.pallas import tpu_sc as plsc

F32 = jnp.float32
BF16 = jnp.bfloat16

EPS = 1e-6
S5_GROUP = 16
S5_STATE = 64
HEAD_DIM = 64
ATTN_HEADS = 8
IDX_HEADS = 8
ROPE_HALF = 8
ROPE_THETA = 500000.0
TOPK_MAX = 256
POOL_WINDOWS = (2, 4, 8, 16)
POOL_HALO = 16
N_EXPERT_GROUPS = 8
TOPK_GROUPS = 4
TOP_K = 8
ROUTED_SCALE = 2.5
S5_CHUNK = 16
NEG_BIG = -1e30
VMEM_LIMIT = 56 * 1024 * 1024


def _cparams(sem, vmem=None):
    return pltpu.CompilerParams(dimension_semantics=sem, vmem_limit_bytes=vmem)


def _dot(a, b):
    return jnp.dot(a, b, preferred_element_type=F32)


def _dot_nt(a, b):
    return lax.dot_general(a, b, (((1,), (1,)), ((), ())), preferred_element_type=F32)


def _split_bf16(a):
    hi = a.astype(BF16)
    lo = (a - hi.astype(F32)).astype(BF16)
    return hi, lo


def _norm_mod(x, g, shift, scale):
    y = x * lax.rsqrt(jnp.mean(x * x, axis=-1, keepdims=True) + EPS)
    return (y * g) * (1.0 + scale) + shift


def _mod_kernel(ct_ref, w_ref, b_ref, o_ref):
    ct = ct_ref[...]
    cs = ct * jax.nn.sigmoid(ct)
    w = w_ref[...]
    rows = [jnp.sum(w * cs[:, b:b + 1], axis=0, keepdims=True) for b in range(ct.shape[1])]
    o_ref[...] = jnp.concatenate(rows, axis=0) + b_ref[...]


def _mod_vectors(c, w, b):
    nl, d, n3 = w.shape
    bsz = c.shape[0]
    tn = 512
    return pl.pallas_call(
        _mod_kernel,
        grid=(nl, n3 // tn),
        in_specs=[pl.BlockSpec((d, bsz), lambda l, j: (0, 0)),
                  pl.BlockSpec((None, d, tn), lambda l, j: (l, 0, j)),
                  pl.BlockSpec((None, 1, tn), lambda l, j: (l, 0, j))],
        out_specs=pl.BlockSpec((None, bsz, tn), lambda l, j: (l, 0, j)),
        out_shape=jax.ShapeDtypeStruct((nl, bsz, n3), F32),
        compiler_params=_cparams(("arbitrary", "arbitrary")),
        name="mod_vectors",
    )(c.T, w, b.reshape(nl, 1, n3))


def _inproj_kernel(x_ref, g_ref, mod_ref, w_ref, pos_ref, inv_ref, qg_ref, kg_ref,
                   u_ref, q_ref, qi_ref, k_ref, v_ref, ki_ref, wi_ref, u_scr, *, d):
    x = x_ref[...]
    mod = mod_ref[...]
    h = _norm_mod(x, g_ref[...], mod[:, :d], mod[:, d:2 * d])
    proj = _dot(h.astype(BF16), w_ref[...])
    tm = x.shape[0]
    width = ATTN_HEADS * HEAD_DIM

    ang = inv_ref[...] * pos_ref[...].astype(F32)
    cos, sin = jnp.cos(ang), jnp.sin(ang)

    def head_t(xt, gain, scale):
        if scale is not None:
            r = lax.rsqrt(jnp.mean(xt * xt, axis=0, keepdims=True) + EPS) * scale
            xt = xt * gain
        x1, x2 = xt[:ROPE_HALF], xt[ROPE_HALF:2 * ROPE_HALF]
        out = jnp.concatenate([x1 * cos - x2 * sin, x1 * sin + x2 * cos, xt[2 * ROPE_HALF:]], axis=0)
        return out if scale is None else out * r

    for slab in range(width // 128):
        u_scr[slab] = proj[:, slab * 128:(slab + 1) * 128]
        for j in range(S5_CHUNK):
            piece = u_scr[slab, pl.ds(j, tm // S5_CHUNK, stride=S5_CHUNK), :]
            col = (slab * S5_CHUNK + j) * 128
            u_ref[:, col:col + 128] = piece.astype(u_ref.dtype)
    q = proj[:, width:2 * width]
    qi = proj[:, 2 * width:3 * width]
    small = proj[:, 3 * width:3 * width + 256]
    qg = qg_ref[...]
    for pair in range(ATTN_HEADS // 2):
        sl = slice(pair * 128, (pair + 1) * 128)
        qt = q[:, sl].T
        qit = qi[:, sl].T
        for half in range(2):
            hs = slice(half * HEAD_DIM, (half + 1) * HEAD_DIM)
            q_ref[2 * pair + half] = head_t(qt[hs], qg, HEAD_DIM ** -0.5).astype(q_ref.dtype)
            qi_ref[2 * pair + half] = head_t(qit[hs], None, None).astype(qi_ref.dtype)
    kvt = small[:, :128].T
    kt = head_t(kvt[:HEAD_DIM], kg_ref[...], 1.0)
    kv = jnp.concatenate([kt, kvt[HEAD_DIM:]], axis=0).T
    k_ref[...] = kv[:, :HEAD_DIM].astype(k_ref.dtype)
    v_ref[...] = kv[:, HEAD_DIM:].astype(v_ref.dtype)
    kiwt = small[:, 128:256].T
    kit = jnp.concatenate([head_t(kiwt[:HEAD_DIM], None, None), kiwt[HEAD_DIM:]], axis=0).T
    ki_ref[...] = kit[:, :HEAD_DIM].astype(ki_ref.dtype)
    wscale = (IDX_HEADS ** -0.5) * (HEAD_DIM ** -0.5)
    wi_ref[...] = kiwt[HEAD_DIM:HEAD_DIM + IDX_HEADS] * wscale


def _in_projection(x2, g, mod, layer, w_in, positions, q_norm_g, k_norm_g, bsz, seq):
    t, d = x2.shape
    tm = 512
    width = ATTN_HEADS * HEAD_DIM
    cuts = np.cumsum([width, width, HEAD_DIM, HEAD_DIM, width, HEAD_DIM, IDX_HEADS])
    u_w, q_w, k_w, v_w, qi_w, ki_w, wi_w = jnp.split(w_in, cuts[:-1].tolist(), axis=1)
    pad = jnp.zeros((d, 128 - HEAD_DIM - IDX_HEADS), w_in.dtype)
    w = jnp.concatenate([u_w, q_w, qi_w, k_w, v_w, ki_w, wi_w, pad], axis=1).astype(BF16)
    nw = w.shape[1]
    inv = np.power(ROPE_THETA, -2.0 * np.arange(ROPE_HALF) / (2 * ROPE_HALF))
    inv = jnp.asarray(inv.reshape(ROPE_HALF, 1), F32)
    tiles_per_seq = seq // tm
    row = lambda i: (i, 0)
    const = lambda i: (0, 0)
    hm = lambda i: (i // tiles_per_seq, 0, 0, i % tiles_per_seq)
    outs = pl.pallas_call(
        functools.partial(_inproj_kernel, d=d),
        grid=(t // tm,),
        in_specs=[pl.BlockSpec((tm, d), row),
                  pl.BlockSpec((1, d), const),
                  pl.BlockSpec((None, None, 1, 3 * d), lambda i: (layer, i // tiles_per_seq, 0, 0)),
                  pl.BlockSpec((d, nw), const),
                  pl.BlockSpec((None, 1, tm), lambda i: (i, 0, 0)),
                  pl.BlockSpec((ROPE_HALF, 1), const),
                  pl.BlockSpec((HEAD_DIM, 1), const),
                  pl.BlockSpec((HEAD_DIM, 1), const)],
        out_specs=[pl.BlockSpec((tm // S5_CHUNK, width * S5_CHUNK), row),
                   pl.BlockSpec((None, ATTN_HEADS, HEAD_DIM, tm), hm),
                   pl.BlockSpec((None, IDX_HEADS, HEAD_DIM, tm), hm),
                   pl.BlockSpec((tm, HEAD_DIM), row),
                   pl.BlockSpec((tm, HEAD_DIM), row),
                   pl.BlockSpec((tm, HEAD_DIM), row),
                   pl.BlockSpec((IDX_HEADS, tm), lambda i: (0, i))],
        out_shape=[jax.ShapeDtypeStruct((t // S5_CHUNK, width * S5_CHUNK), BF16),
                   jax.ShapeDtypeStruct((bsz, ATTN_HEADS, HEAD_DIM, seq), BF16),
                   jax.ShapeDtypeStruct((bsz, IDX_HEADS, HEAD_DIM, seq), BF16),
                   jax.ShapeDtypeStruct((t, HEAD_DIM), BF16),
                   jax.ShapeDtypeStruct((t, HEAD_DIM), BF16),
                   jax.ShapeDtypeStruct((t, HEAD_DIM), BF16),
                   jax.ShapeDtypeStruct((IDX_HEADS, t), F32)],
        scratch_shapes=[pltpu.VMEM((width // 128, tm, 128), F32)],
        compiler_params=_cparams(("arbitrary",), VMEM_LIMIT),
        name="in_projection",
    )(x2, g.reshape(1, d), mod, w, positions.reshape(t // tm, 1, tm), inv,
      q_norm_g.reshape(HEAD_DIM, 1), k_norm_g.reshape(HEAD_DIM, 1))
    return outs


def _s5_weights(lam_re, lam_im, log_dt, b_re, b_im, c_re, c_im, d_skip, glu_w, glu_b):
    L = S5_CHUNK
    g, p = lam_re.shape
    hch = S5_GROUP
    dt = jnp.exp(log_dt)[:, None]
    lr, li = lam_re, lam_im
    tau = jnp.arange(L + 1, dtype=F32)[:, None, None]
    mag = jnp.exp(lr * dt * tau)
    pw_r, pw_i = mag * jnp.cos(li * dt * tau), mag * jnp.sin(li * dt * tau)
    ar, ai = pw_r[1], pw_i[1]
    nr, ni = ar - 1.0, ai
    den = lr * lr + li * li
    cr, ci = (nr * lr + ni * li) / den, (ni * lr - nr * li) / den
    bb_r = cr[..., None] * b_re - ci[..., None] * b_im
    bb_i = cr[..., None] * b_im + ci[..., None] * b_re
    cl_r = c_re[None] * pw_r[:L, :, None, :] - c_im[None] * pw_i[:L, :, None, :]
    cl_i = c_re[None] * pw_i[:L, :, None, :] + c_im[None] * pw_r[:L, :, None, :]
    taps = (jnp.einsum('tghp,gpk->tghk', cl_r, bb_r) - jnp.einsum('tghp,gpk->tghk', cl_i, bb_i))
    ns = g // 8
    eye = jnp.eye(8, dtype=F32)
    kd = jnp.einsum('tsgoh,gf->tsghfo', taps.reshape(L, ns, 8, hch, hch), eye).reshape(L, ns, 128, 128)
    zero = jnp.zeros_like(kd[0])
    k2 = jnp.stack([jnp.concatenate([jnp.concatenate([kd[2 * dd], kd[2 * dd + 1]], axis=-1),
                                     jnp.concatenate([kd[2 * dd - 1] if dd else zero, kd[2 * dd]], axis=-1)], axis=-2)
                    for dd in range(L // 2)], axis=1).astype(BF16)
    ii = jnp.arange(L)
    rev_r, rev_i = pw_r[L - 1 - ii], pw_i[L - 1 - ii]
    z_r = rev_r[..., None] * bb_r[None] - rev_i[..., None] * bb_i[None]
    z_i = rev_r[..., None] * bb_i[None] + rev_i[..., None] * bb_r[None]
    zc = jnp.concatenate([z_r, z_i], axis=2)
    zc = jnp.transpose(zc.reshape(L, ns, 8, 2 * p, hch), (1, 0, 2, 4, 3)).reshape(ns, L * 128, 1, 2 * p)
    row_group = (jnp.arange(L * 128) // hch) % 8
    w_z = jnp.where((row_group[:, None] == jnp.arange(8)[None, :])[None, :, :, None], zc, 0.0)
    w_z = w_z.reshape(ns, L // 2, 2 * 128, 8 * 2 * p).astype(BF16)
    co_r = c_re[None] * pw_r[1:, :, None, :] - c_im[None] * pw_i[1:, :, None, :]
    co_i = c_re[None] * pw_i[1:, :, None, :] + c_im[None] * pw_r[1:, :, None, :]
    cc = jnp.concatenate([co_r, -co_i], axis=-1)
    cc = jnp.transpose(cc.reshape(L, ns, 8, hch, 2 * p), (1, 4, 0, 2, 3)).reshape(ns, 1, 2 * p, L * 128)
    col_group = (jnp.arange(L * 128) // hch) % 8
    w_c = jnp.where((jnp.arange(8)[:, None] == col_group[None, :])[None, :, None, :], cc, 0.0)
    w_c = w_c.reshape(ns, 8 * 2 * p, L * 128)
    w_ch = w_c.astype(BF16)
    w_cl = (w_c - w_ch.astype(F32)).astype(BF16)
    al_r, al_i = pw_r[L], pw_i[L]
    dec_a = jnp.concatenate([al_r, al_r], axis=-1).reshape(ns, 1, 8 * 2 * p)
    dec_b = jnp.concatenate([-al_i, al_i], axis=-1).reshape(ns, 1, 8 * 2 * p)
    d_t = jnp.tile(d_skip.reshape(ns, 1, 128), (1, 1, L))
    gl = jnp.einsum('sghk,gf->sghfk', glu_w.reshape(ns, 8, hch, hch), eye).reshape(ns, 128, 128)
    glu2 = jnp.einsum('ab,shk->sahbk', jnp.eye(2, dtype=F32), gl).reshape(ns, 256, 256).astype(BF16)
    glu_bt = jnp.tile(glu_b.reshape(ns, 1, 128), (1, 1, L))
    return k2, w_z, w_ch, w_cl, dec_a, dec_b, d_t, glu2, glu_bt


def _s5_kernel(u_ref, k2_ref, wz_ref, wch_ref, wcl_ref, da_ref, db_ref, d_ref, g2_ref, gb_ref, o_ref,
               zp_ref, zq_ref, sp_ref, *, nchunk):
    nblk = S5_CHUNK // 2
    ub = [u_ref[:, j * 256:(j + 1) * 256] for j in range(nblk)]
    z = _dot(ub[0], wz_ref[0])
    for j in range(1, nblk):
        z = z + _dot(ub[j], wz_ref[j])
    zp_ref[...] = z
    half = S5_STATE
    zq_ref[...] = jnp.concatenate([pltpu.roll(z[:, s * 2 * half:(s + 1) * 2 * half], half, 1)
                                   for s in range(z.shape[1] // (2 * half))], axis=1)
    da = da_ref[...]
    db = db_ref[...]
    dbq = -db

    def step(c, carry):
        sp, sq = carry
        sp_ref[pl.ds(c, 1), :] = sp
        sp_new = da * sp + db * sq + zp_ref[pl.ds(c, 1), :]
        sq_new = da * sq + dbq * sp + zq_ref[pl.ds(c, 1), :]
        return sp_new, sq_new

    zero = jnp.zeros((1, z.shape[1]), F32)
    lax.fori_loop(0, nchunk, step, (zero, zero))
    sh, sl = _split_bf16(sp_ref[...])
    wch = wch_ref[...]
    y_inter = _dot(sh, wch) + (_dot(sh, wcl_ref[...]) + _dot(sl, wch))
    for i in range(nblk):
        cs = slice(i * 256, (i + 1) * 256)
        acc = y_inter[:, cs] + d_ref[:, cs] * ub[i].astype(F32)
        for j in range(i + 1):
            acc = acc + _dot(ub[j], k2_ref[i - j])
        y = jax.nn.gelu(acc)
        y = y * jax.nn.sigmoid(_dot(y.astype(BF16), g2_ref[...]) + gb_ref[:, cs])
        for a in range(2):
            o_ref[pl.ds(2 * i + a, nchunk, stride=S5_CHUNK), :] = y[:, a * 128:(a + 1) * 128]


def _s5_mixer(u2, weights, bsz, seq):
    k2, w_z, w_ch, w_cl, dec_a, dec_b, d_t, glu2, glu_bt = weights
    ns = k2.shape[0]
    nchunk = seq // S5_CHUNK
    cols = S5_CHUNK * 128
    st = w_z.shape[-1]
    slab3 = lambda b, s: (s, 0, 0)
    slab4 = lambda b, s: (s, 0, 0, 0)
    return pl.pallas_call(
        functools.partial(_s5_kernel, nchunk=nchunk),
        grid=(bsz, ns),
        in_specs=[pl.BlockSpec((nchunk, cols), lambda b, s: (b, s)),
                  pl.BlockSpec((None,) + k2.shape[1:], slab4),
                  pl.BlockSpec((None,) + w_z.shape[1:], slab4),
                  pl.BlockSpec((None, st, cols), slab3),
                  pl.BlockSpec((None, st, cols), slab3),
                  pl.BlockSpec((None, 1, st), slab3),
                  pl.BlockSpec((None, 1, st), slab3),
                  pl.BlockSpec((None, 1, cols), slab3),
                  pl.BlockSpec((None, 256, 256), slab3),
                  pl.BlockSpec((None, 1, cols), slab3)],
        out_specs=pl.BlockSpec((seq, 128), lambda b, s: (b, s)),
        out_shape=jax.ShapeDtypeStruct((bsz * seq, ns * 128), F32),
        scratch_shapes=[pltpu.VMEM((nchunk, st), F32), pltpu.VMEM((nchunk, st), F32), pltpu.VMEM((nchunk, st), F32)],
        compiler_params=_cparams(("arbitrary", "arbitrary"), VMEM_LIMIT),
        name="s5_mixer",
    )(u2, k2, w_z, w_ch, w_cl, dec_a, dec_b, d_t, glu2, glu_bt)


def _dsa_kernel(qt_ref, qit_ref, wt_ref, k_ref, vt_ref, ki_ref, o_ref,
                sc_ref, s_ref, m_ref, l_ref, acc_ref, *, tq, ck, k_sel, idx_bits, max_iter):
    i = pl.program_id(1)
    nheads = qt_ref.shape[0]
    n_ck = (i * tq + tq + ck - 1) // ck
    t_pos = i * tq + lax.broadcasted_iota(jnp.int32, (1, tq), 1)
    krow = lax.broadcasted_iota(jnp.int32, (ck, 1), 0)
    kf = float(k_sel)
    inf = jnp.inf

    fr = 32
    sub = min(32768 // tq, ck)

    def fold(x, op):
        return op(x.reshape(ck // fr, fr, tq), axis=0)

    wt = wt_ref[...]

    def score_chunk(c, carry):
        lo8, hi8 = carry
        for part in range(ck // sub):
            off = pl.multiple_of(c * ck + part * sub, sub)
            kic = ki_ref[pl.ds(off, sub), :]
            acc = jnp.zeros((sub, tq), F32)
            for h in range(nheads):
                acc = acc + wt[h:h + 1, :] * jnp.maximum(_dot(kic, qit_ref[h]), 0.0)
            acc = acc + 0.0
            vis = off + krow[:sub] <= t_pos
            sc_ref[c, part * sub:(part + 1) * sub, :] = jnp.where(vis, acc, -inf)
            hi8 = jnp.maximum(hi8, jnp.max(jnp.where(vis, acc, -inf).reshape(sub // fr, fr, tq), axis=0))
            lo8 = jnp.minimum(lo8, jnp.min(jnp.where(vis, acc, inf).reshape(sub // fr, fr, tq), axis=0))
        return lo8, hi8

    lo8, hi8 = lax.fori_loop(0, n_ck, score_chunk, (jnp.full((fr, tq), inf, F32), jnp.full((fr, tq), -inf, F32)))
    lo = jnp.min(lo8, axis=0, keepdims=True)
    hi = jnp.max(hi8, axis=0, keepdims=True)
    n_vis = (t_pos + 1).astype(F32)
    few = n_vis < kf

    def reduce_keys(fn, init, combine, op):
        def body(c, part):
            return combine(part, fold(fn(sc_ref[c], c * ck), op))
        return op(lax.fori_loop(0, n_ck, body, jnp.full((fr, tq), init, F32)), axis=0, keepdims=True)

    def count(pred):
        return reduce_keys(lambda x, off: jnp.where(pred(x, off), 1.0, 0.0), 0.0, jnp.add, jnp.sum)

    def bisect(lo_, hi_, c_lo):
        mid = lo_ + (hi_ - lo_) * 0.5
        cnt = count(lambda x, off: x >= mid)
        up = cnt >= kf
        return jnp.where(up, mid, lo_), jnp.where(up, hi_, mid), jnp.where(up, cnt, c_lo)

    def status(lo_, hi_, c_lo):
        mid = lo_ + (hi_ - lo_) * 0.5
        unresolved = jnp.where(few, 0.0, jnp.where(c_lo == kf, 0.0, 1.0))
        movable = jnp.where(mid > lo_, jnp.where(mid < hi_, 1.0, 0.0), 0.0)
        return jnp.max(unresolved), jnp.max(unresolved * movable)

    def search_body(st):
        it, lo_, hi_, c_lo, _, _ = st
        for _ in range(2):
            lo_, hi_, c_lo = bisect(lo_, hi_, c_lo)
        return (it + 2, lo_, hi_, c_lo) + status(lo_, hi_, c_lo)

    st = lax.while_loop(lambda st: jnp.logical_and(st[0] < max_iter, st[5] > 0.5), search_body,
                        (jnp.int32(0), lo, hi, n_vis) + status(lo, hi, n_vis))
    lo, open_ = st[1], st[4]

    def resolve():
        def walk(thr_, strict):
            keep = (lambda x: x > thr_) if strict else (lambda x: x >= thr_)
            return reduce_keys(lambda x, off: jnp.where(keep(x), x, inf), inf, jnp.minimum, jnp.min)

        def above_equal(thr_):
            return count(lambda x, off: x > thr_), count(lambda x, off: x == thr_)

        def too_low(n_gt):
            return jnp.where(few, 0.0, jnp.where(n_gt >= kf, 1.0, 0.0))

        thr0 = walk(lo, False)

        def walk_body(st):
            thr_, n_gt, _ = st
            thr2 = jnp.where(too_low(n_gt) > 0.5, walk(thr_, True), thr_)
            return (thr2,) + above_equal(thr2)

        thr_, n_gt, n_eq = lax.while_loop(lambda st: jnp.max(too_low(st[1])) > 0.5, walk_body,
                                          (thr0,) + above_equal(thr0))
        need = kf - n_gt

        def tie_cut():
            def mark_equal(c, _):
                s_ref[0, c] = jnp.where(sc_ref[c] == thr_, 1.0, 0.0)
                return 0

            lax.fori_loop(0, n_ck, mark_equal, 0)

            def idx_step(b, cur):
                trial = cur | (jnp.int32(1) << (idx_bits - 1 - b))

                def body(c, part):
                    return part + fold(jnp.where(c * ck + krow < trial, s_ref[0, c], 0.0), jnp.sum)
                cnt = jnp.sum(lax.fori_loop(0, n_ck, body, jnp.zeros((fr, tq), F32)), axis=0, keepdims=True)
                return jnp.where(cnt < need, trial, cur)

            return lax.fori_loop(0, idx_bits, idx_step, jnp.zeros((1, tq), jnp.int32))

        tied = jnp.max(jnp.where(few, 0.0, jnp.where(n_eq > need, 1.0, 0.0))) > 0.5
        return thr_, lax.cond(tied, tie_cut, lambda: jnp.full((1, tq), 0x7FFFFFFF, jnp.int32))

    thr, cut = lax.cond(open_ > 0.5, resolve, lambda: (lo, jnp.full((1, tq), 0x7FFFFFFF, jnp.int32)))

    for h in range(nheads):
        m_ref[h] = jnp.full((fr, tq), NEG_BIG, F32)
        l_ref[h] = jnp.zeros((fr, tq), F32)
        acc_ref[h] = jnp.zeros((HEAD_DIM, tq), F32)

    def attn_scores(c, _):
        off = pl.multiple_of(c * ck, ck)
        kc = k_ref[pl.ds(off, ck), :]
        x = sc_ref[c]
        at_thr = jnp.where(x == thr, jnp.where(off + krow <= cut, 0.0, NEG_BIG), NEG_BIG)
        bias = jnp.where(x > thr, 0.0, at_thr)
        for h in range(nheads):
            s = _dot(kc, qt_ref[h]) + bias
            s_ref[h, c] = s
            m_ref[h] = jnp.maximum(m_ref[h], fold(s, jnp.max))
        return 0

    lax.fori_loop(0, n_ck, attn_scores, 0)
    for h in range(nheads):
        m_ref[h] = jnp.broadcast_to(jnp.max(m_ref[h], axis=0, keepdims=True), (fr, tq))

    def attn_values(c, _):
        vt = vt_ref[c]
        for h in range(nheads):
            p = jnp.exp(s_ref[h, c] - m_ref[h][:1, :])
            l_ref[h] += fold(p, jnp.sum)
            acc_ref[h] += _dot(vt, p.astype(BF16))
        return 0

    lax.fori_loop(0, n_ck, attn_values, 0)
    for h in range(nheads):
        l_row = jnp.sum(l_ref[h], axis=0, keepdims=True)
        o_ref[h * HEAD_DIM:(h + 1) * HEAD_DIM, :] = (acc_ref[h] / l_row).astype(o_ref.dtype)


def _dsa_attention(qt, qit, wt, k, v, ki, bsz, seq):
    tq = 256
    ck = min(512, seq)
    n_ck = seq // ck
    k_sel = min(TOPK_MAX, seq // 4)
    nq = seq // tq
    nh = ATTN_HEADS
    width = nh * HEAD_DIM
    vt = v.reshape(bsz, n_ck, ck, HEAD_DIM).transpose(0, 1, 3, 2)
    qspec = pl.BlockSpec((None, nh, HEAD_DIM, tq), lambda b, i: (b, 0, 0, i))
    out_t = pl.pallas_call(
        functools.partial(_dsa_kernel, tq=tq, ck=ck, k_sel=k_sel, idx_bits=(seq - 1).bit_length(), max_iter=22),
        grid=(bsz, nq),
        in_specs=[qspec, qspec,
                  pl.BlockSpec((IDX_HEADS, tq), lambda b, i: (0, b * nq + i)),
                  pl.BlockSpec((seq, HEAD_DIM), lambda b, i: (b, 0)),
                  pl.BlockSpec((None, n_ck, HEAD_DIM, ck), lambda b, i: (b, 0, 0, 0)),
                  pl.BlockSpec((seq, HEAD_DIM), lambda b, i: (b, 0))],
        out_specs=pl.BlockSpec((None, width, tq), lambda b, i: (b, 0, i)),
        out_shape=jax.ShapeDtypeStruct((bsz, width, seq), BF16),
        scratch_shapes=[pltpu.VMEM((n_ck, ck, tq), F32),
                        pltpu.VMEM((nh, n_ck, ck, tq), F32),
                        pltpu.VMEM((nh, 32, tq), F32),
                        pltpu.VMEM((nh, 32, tq), F32),
                        pltpu.VMEM((nh, HEAD_DIM, tq), F32)],
        compiler_params=_cparams(("arbitrary", "arbitrary"), VMEM_LIMIT),
        name="dsa_attention",
    )(qt, qit, wt, k, vt, ki)
    return out_t.transpose(0, 2, 1).reshape(bsz * seq, width)


def _outproj_kernel(x_ref, ys_ref, ya_ref, w_ref, mod_ref, o_ref, *, d):
    width = ys_ref.shape[1]
    y = _dot(ys_ref[...].astype(BF16), w_ref[:width, :]) + _dot(ya_ref[...], w_ref[width:, :])
    o_ref[...] = x_ref[...] + mod_ref[...][:, 2 * d:] * y


def _out_projection(x2, y_ssm, y_att, w_out, mod, layer, seq):
    t, d = x2.shape
    tm = 512
    width = y_ssm.shape[1]
    tiles_per_seq = seq // tm
    row = lambda i: (i, 0)
    return pl.pallas_call(
        functools.partial(_outproj_kernel, d=d),
        grid=(t // tm,),
        in_specs=[pl.BlockSpec((tm, d), row),
                  pl.BlockSpec((tm, width), row),
                  pl.BlockSpec((tm, width), row),
                  pl.BlockSpec((2 * width, d), lambda i: (0, 0)),
                  pl.BlockSpec((None, None, 1, 3 * d), lambda i: (layer, i // tiles_per_seq, 0, 0))],
        out_specs=pl.BlockSpec((tm, d), row),
        out_shape=jax.ShapeDtypeStruct((t, d), F32),
        compiler_params=_cparams(("arbitrary",)),
        name="out_projection",
    )(x2, y_ssm, y_att, w_out.astype(BF16), mod)


def _pool_kernel(x_ref, halo_ref, g_ref, mod_ref, pw_ref, ps_ref, o_ref, h_ref, *, d, tiles_per_seq):
    i = pl.program_id(0)
    tm = x_ref.shape[0]
    mod = mod_ref[...]
    shift, scale, gate = mod[:, :d], mod[:, d:2 * d], mod[:, 2 * d:]
    x = x_ref[...]
    first = (i % tiles_per_seq) == 0
    h_ref[POOL_HALO:, :] = _norm_mod(x, g_ref[...], shift, scale)
    halo = _norm_mod(halo_ref[...], g_ref[...], shift, scale)
    h_ref[:POOL_HALO, :] = jnp.where(first, 0.0, halo)
    pos = (i % tiles_per_seq) * tm + lax.broadcasted_iota(jnp.int32, (tm, 1), 0)
    gw = d // len(POOL_WINDOWS)
    ys = []
    for gi, win in enumerate(POOL_WINDOWS):
        cs = slice(gi * gw, (gi + 1) * gw)
        cur = h_ref[POOL_HALO:, cs]
        tot = cur
        for k in range(1, win):
            tot = tot + h_ref[POOL_HALO - k:POOL_HALO - k + tm, cs]
        cnt = jnp.minimum(pos + 1, win).astype(F32)
        pooled = tot / cnt - cur
        ys.append(_dot(pooled.astype(BF16), pw_ref[gi]))
    y = jnp.concatenate(ys, axis=1) * ps_ref[...]
    o_ref[...] = x + gate * y


def _pool_layer(x2, g, mod, layer, pool_w, pool_scale, seq):
    t, d = x2.shape
    tm = 512
    tiles_per_seq = seq // tm
    gw = d // len(POOL_WINDOWS)
    hb = tm // POOL_HALO
    return pl.pallas_call(
        functools.partial(_pool_kernel, d=d, tiles_per_seq=tiles_per_seq),
        grid=(t // tm,),
        in_specs=[pl.BlockSpec((tm, d), lambda i: (i, 0)),
                  pl.BlockSpec((POOL_HALO, d), lambda i: (jnp.maximum(i * hb - 1, 0), 0)),
                  pl.BlockSpec((1, d), lambda i: (0, 0)),
                  pl.BlockSpec((None, None, 1, 3 * d), lambda i: (layer, i // tiles_per_seq, 0, 0)),
                  pl.BlockSpec((len(POOL_WINDOWS), gw, gw), lambda i: (0, 0, 0)),
                  pl.BlockSpec((1, d), lambda i: (0, 0))],
        out_specs=pl.BlockSpec((tm, d), lambda i: (i, 0)),
        out_shape=jax.ShapeDtypeStruct((t, d), F32),
        scratch_shapes=[pltpu.VMEM((tm + POOL_HALO, d), F32)],
        compiler_params=_cparams(("arbitrary",)),
        name="pool_mixer",
    )(x2, x2, g.reshape(1, d), mod, pool_w.astype(BF16), pool_scale.reshape(1, d))


def _first_max(vals, idx, big):
    m = jnp.max(vals, axis=0, keepdims=True)
    first = jnp.min(jnp.where(vals == m, idx, big), axis=0, keepdims=True)
    return m, first


def _pack_bf16_pairs(x):
    n = x.shape[1] // 2
    lo = pltpu.bitcast(x[:, :n].astype(BF16).astype(F32), jnp.int32)
    hi = pltpu.bitcast(x[:, n:].astype(BF16).astype(F32), jnp.int32)
    return hi | ((lo >> 16) & 0xFFFF)


def _unpack_bf16_pairs(w):
    lo = pltpu.bitcast(w << 16, F32)
    hi = pltpu.bitcast(w & jnp.int32(-65536), F32)
    return jnp.concatenate([lo, hi], axis=1)


def _router_kernel(x_ref, g_ref, mod_ref, rw_ref, rb_ref, tri_ref, h_ref, gate_ref, rank_ref, cnt_ref,
                   run_ref, *, d, n_exp):
    mod = mod_ref[...]
    h = _norm_mod(x_ref[...], g_ref[...], mod[:, :d], mod[:, d:2 * d])
    h_ref[...] = _pack_bf16_pairs(h)
    tm = h.shape[0]
    hh, hl = _split_bf16(h)
    rw = rw_ref[...]
    rh, rl = _split_bf16(rw)
    logits = _dot_nt(rh, hh) + (_dot_nt(rh, hl) + _dot_nt(rl, hh))
    scores = jax.nn.sigmoid(logits)
    sel = scores + rb_ref[...]
    gsz = n_exp // N_EXPERT_GROUPS
    neg = -jnp.inf
    shape3 = (N_EXPERT_GROUPS, gsz, tm)
    sel3 = sel.reshape(shape3)
    sub = lax.broadcasted_iota(jnp.int32, shape3, 1)
    m1 = jnp.max(sel3, axis=1, keepdims=True)
    f1 = jnp.min(jnp.where(sel3 == m1, sub, gsz), axis=1, keepdims=True)
    m2 = jnp.max(jnp.where(sub == f1, neg, sel3), axis=1, keepdims=True)
    gscore = jnp.broadcast_to(m1 + m2, shape3).reshape(n_exp, tm)
    eidx = lax.broadcasted_iota(jnp.int32, (n_exp, tm), 0)
    gidx = eidx // gsz
    keep = jnp.zeros((n_exp, tm), F32)
    work = gscore
    for _ in range(TOPK_GROUPS):
        _, first = _first_max(work, gidx, N_EXPERT_GROUPS)
        hit = gidx == first
        keep = jnp.where(hit, 1.0, keep)
        work = jnp.where(hit, neg, work)
    work = jnp.where(keep > 0.0, sel, neg)
    chosen = jnp.zeros((n_exp, tm), F32)
    for _ in range(TOP_K):
        _, first = _first_max(work, eidx, n_exp)
        hit = eidx == first
        chosen = jnp.where(hit, 1.0, chosen)
        work = jnp.where(hit, neg, work)
    picked = chosen * scores
    gate_ref[...] = picked / jnp.sum(picked, axis=0, keepdims=True) * ROUTED_SCALE

    @pl.when(pl.program_id(0) == 0)
    def _():
        run_ref[...] = jnp.zeros(run_ref.shape, F32)

    before = _dot(chosen.astype(BF16), tri_ref[...])
    run = run_ref[...]
    rank_ref[...] = jnp.where(chosen > 0.0, before + run[:, :1], -1.0)
    run = run + jnp.broadcast_to(jnp.sum(chosen, axis=1, keepdims=True), run.shape)
    run_ref[...] = run
    cnt_ref[...] = run


MOE_TILE = 512


def _ffn_router(x2, g, mod, layer, router_w, router_bias, seq, tile0, ntiles):
    d = x2.shape[1]
    n_exp = router_w.shape[1]
    tm = MOE_TILE
    t = ntiles * tm
    tiles_per_seq = seq // tm
    tri = (np.arange(tm)[:, None] < np.arange(tm)[None, :]).astype(np.float32)
    return pl.pallas_call(
        functools.partial(_router_kernel, d=d, n_exp=n_exp),
        grid=(ntiles,),
        in_specs=[pl.BlockSpec((tm, d), lambda i: (i + tile0, 0)),
                  pl.BlockSpec((1, d), lambda i: (0, 0)),
                  pl.BlockSpec((None, None, 1, 3 * d), lambda i: (layer, (i + tile0) // tiles_per_seq, 0, 0)),
                  pl.BlockSpec((n_exp, d), lambda i: (0, 0)),
                  pl.BlockSpec((n_exp, 1), lambda i: (0, 0)),
                  pl.BlockSpec((tm, tm), lambda i: (0, 0))],
        out_specs=[pl.BlockSpec((tm, d // 2), lambda i: (i, 0)),
                   pl.BlockSpec((n_exp, tm), lambda i: (0, i)),
                   pl.BlockSpec((n_exp, tm), lambda i: (0, i)),
                   pl.BlockSpec((n_exp, 128), lambda i: (0, 0))],
        out_shape=[jax.ShapeDtypeStruct((t, d // 2), jnp.int32),
                   jax.ShapeDtypeStruct((n_exp, t), F32),
                   jax.ShapeDtypeStruct((n_exp, t), F32),
                   jax.ShapeDtypeStruct((n_exp, 128), F32)],
        scratch_shapes=[pltpu.VMEM((n_exp, 128), F32)],
        compiler_params=_cparams(("arbitrary",)),
        name="ffn_router",
    )(x2, g.reshape(1, d), mod, router_w.T, router_bias.reshape(n_exp, 1), jnp.asarray(tri, BF16))


def _assign_kernel(rank_ref, gate_ref, start_ref, pos_ref, w_ref, *, n_exp):
    rank = rank_ref[...]
    gates = gate_ref[...]
    tm = rank.shape[1]
    slot = rank + start_ref[...]
    eidx = lax.broadcasted_iota(jnp.int32, (n_exp, tm), 0).astype(F32)
    alive = jnp.where(rank >= 0.0, eidx, float(n_exp))
    kidx = lax.broadcasted_iota(jnp.int32, (TOP_K, tm), 0)
    pos = jnp.zeros((TOP_K, tm), F32)
    wts = jnp.zeros((TOP_K, tm), F32)
    for k in range(TOP_K):
        first = jnp.min(alive, axis=0, keepdims=True)
        hit = alive == first
        pos_k = jnp.sum(jnp.where(hit, slot, 0.0), axis=0, keepdims=True)
        w_k = jnp.sum(jnp.where(hit, gates, 0.0), axis=0, keepdims=True)
        pos = jnp.where(kidx == k, pos_k, pos)
        wts = jnp.where(kidx == k, w_k, wts)
        alive = jnp.where(hit, float(n_exp), alive)
    pos_ref[...] = pos.astype(jnp.int32)
    w_ref[...] = wts


def _assign_slots(ranks, gates, start):
    n_exp, t = ranks.shape
    tm = 512
    return pl.pallas_call(
        functools.partial(_assign_kernel, n_exp=n_exp),
        grid=(t // tm,),
        in_specs=[pl.BlockSpec((n_exp, tm), lambda i: (0, i)),
                  pl.BlockSpec((n_exp, tm), lambda i: (0, i)),
                  pl.BlockSpec((n_exp, 1), lambda i: (0, 0))],
        out_specs=[pl.BlockSpec((TOP_K, tm), lambda i: (0, i)),
                   pl.BlockSpec((TOP_K, tm), lambda i: (0, i))],
        out_shape=[jax.ShapeDtypeStruct((TOP_K, t), jnp.int32), jax.ShapeDtypeStruct((TOP_K, t), F32)],
        compiler_params=_cparams(("arbitrary",)),
        name="moe_assign",
    )(ranks, gates, start)


SC_CORES = 2
SC_SUBCORES = 16
SC_WINDOW = 128


def _sc_mesh():
    return plsc.VectorSubcoreMesh(core_axis_name="c", subcore_axis_name="s",
                                  num_cores=SC_CORES, num_subcores=SC_SUBCORES)


def _sc_scatter_rows(rows, pos, n_out):
    t, width = rows.shape
    nk = pos.shape[0]
    win = SC_WINDOW
    n_win = t // win // (SC_CORES * SC_SUBCORES)
    pos_w = pos.reshape(nk, t // win, win).transpose(1, 0, 2)

    def body(rows_hbm, pos_hbm, out_hbm, idx_v, buf_v, sem):
        wid = lax.axis_index("s") * SC_CORES + lax.axis_index("c")

        @pl.loop(0, n_win)
        def _(j):
            w = wid * n_win + j
            pltpu.sync_copy(rows_hbm.at[pl.ds(w * win, win)], buf_v)
            pltpu.sync_copy(pos_hbm.at[w], idx_v)
            copies = [pltpu.make_async_copy(buf_v, out_hbm.at[idx_v.at[k]], sem) for k in range(nk)]
            for cp in copies:
                cp.start()
            for cp in copies:
                cp.wait()

    return pl.kernel(
        body, out_type=jax.ShapeDtypeStruct((n_out, width), jnp.int32), mesh=_sc_mesh(),
        scratch_types=[pltpu.VMEM((nk, win), jnp.int32), pltpu.VMEM((win, width), jnp.int32),
                       pltpu.SemaphoreType.DMA],
        name="sc_scatter_rows",
    )(rows, pos_w)


SC_GATHER_WINDOW = 64


def _sc_gather_rows(table, pos_flat):
    m = pos_flat.shape[0]
    width = table.shape[1]
    win = SC_GATHER_WINDOW
    workers = SC_CORES * SC_SUBCORES
    n_win = m // win // workers
    pos_w = pos_flat.reshape(workers, n_win, win)

    def body(table_hbm, pos_hbm, out_hbm, idx_v, buf_v, sem_g, sem_w):
        wid = lax.axis_index("s") * SC_CORES + lax.axis_index("c")
        base = wid * (n_win * win)
        pltpu.sync_copy(pos_hbm.at[wid], idx_v)

        def gather(j, b):
            return pltpu.make_async_copy(table_hbm.at[idx_v.at[j]], buf_v.at[b], sem_g.at[b])

        def write(j, b):
            return pltpu.make_async_copy(buf_v.at[b], out_hbm.at[pl.ds(base + j * win, win)], sem_w.at[b])

        gather(0, 0).start()

        @pl.loop(0, n_win, step=2)
        def _(j0):
            for b in range(2):
                j = j0 + b
                gather(j, b).wait()
                write(j, b).start()

                @pl.when(j + 1 < n_win)
                def _():
                    @pl.when(j >= 1)
                    def _():
                        write(j - 1, 1 - b).wait()
                    gather(j + 1, 1 - b).start()

        write(n_win - 2, 0).wait()
        write(n_win - 1, 1).wait()

    return pl.kernel(
        body, out_type=jax.ShapeDtypeStruct((m, width), jnp.int32), mesh=_sc_mesh(),
        scratch_types=[pltpu.VMEM((n_win, win), jnp.int32), pltpu.VMEM((2, win, width), jnp.int32),
                       pltpu.SemaphoreType.DMA((2,)), pltpu.SemaphoreType.DMA((2,))],
        name="sc_gather_rows",
    )(table, pos_w)


MOE_BLOCK = 512


def _silu_mul(a, b):
    return (a * jax.nn.sigmoid(a)) * b


def _expert_ffn_kernel(be_ref, nv_ref, xs_ref, wg_ref, wu_ref, wd_ref, ys_ref, wg_bf, wu_bf, wd_bf):
    i = pl.program_id(0)
    fresh = jnp.logical_or(i == 0, be_ref[i] != be_ref[jnp.maximum(i - 1, 0)])

    @pl.when(fresh)
    def _():
        wg_bf[...] = wg_ref[...].astype(BF16)
        wu_bf[...] = wu_ref[...].astype(BF16)
        wd_bf[...] = wd_ref[...].astype(BF16)

    nv = nv_ref[i]

    @pl.when(nv > 0)
    def _():
        x = _unpack_bf16_pairs(xs_ref[...])
        row = lax.broadcasted_iota(jnp.int32, (x.shape[0], 1), 0)
        x = jnp.where(row < nv, x, 0.0).astype(BF16)
        mid = _silu_mul(_dot(x, wg_bf[...]), _dot(x, wu_bf[...]))
        ys_ref[...] = _pack_bf16_pairs(_dot(mid.astype(BF16), wd_bf[...]))

    @pl.when(nv <= 0)
    def _():
        ys_ref[...] = jnp.zeros(ys_ref.shape, ys_ref.dtype)


def _expert_ffn(xs, block_expert, block_rows, layer, w_gate, w_up, w_down):
    n_rows, half = xs.shape
    _, n_exp, d, de = w_gate.shape
    nb = n_rows // MOE_BLOCK
    wmap = lambda i, be, nv: (layer, be[i], 0, 0)
    grid_spec = pltpu.PrefetchScalarGridSpec(
        num_scalar_prefetch=2,
        grid=(nb,),
        in_specs=[pl.BlockSpec((MOE_BLOCK, half), lambda i, be, nv: (i, 0)),
                  pl.BlockSpec((None, None, d, de), wmap),
                  pl.BlockSpec((None, None, d, de), wmap),
                  pl.BlockSpec((None, None, de, d), wmap)],
        out_specs=pl.BlockSpec((MOE_BLOCK, half), lambda i, be, nv: (i, 0)),
        scratch_shapes=[pltpu.VMEM((d, de), BF16), pltpu.VMEM((d, de), BF16), pltpu.VMEM((de, d), BF16)],
    )
    return pl.pallas_call(
        _expert_ffn_kernel,
        grid_spec=grid_spec,
        out_shape=jax.ShapeDtypeStruct((n_rows, half), jnp.int32),
        compiler_params=_cparams(("arbitrary",)),
        name="moe_expert_ffn",
    )(block_expert, block_rows, xs, w_gate, w_up, w_down)


def _combine_kernel(yt_ref, wt_ref, hp_ref, x_ref, mod_ref, sg_ref, su_ref, sd_ref, *rest, d):
    o_ref = rest[-1]
    h = _unpack_bf16_pairs(hp_ref[...]).astype(BF16)
    acc = _dot(_silu_mul(_dot(h, sg_ref[...]), _dot(h, su_ref[...])).astype(BF16), sd_ref[...])
    w = wt_ref[...]
    for k in range(TOP_K):
        acc = acc + w[:, k:k + 1] * _unpack_bf16_pairs(yt_ref[k])
    o_ref[...] = x_ref[...] + mod_ref[...][:, 2 * d:] * acc


def _combine(y_tok, w_tok, h_packed, x2, out_prev, mod, layer, sh_gate, sh_up, sh_down, seq, tile0):
    t_all, d = x2.shape
    ds = sh_gate.shape[1]
    tm = MOE_TILE
    ntiles = h_packed.shape[0] // tm
    tiles_per_seq = seq // tm
    row = lambda i: (i, 0)
    off = lambda i: (i + tile0, 0)
    const = lambda i: (0, 0)
    in_specs = [pl.BlockSpec((TOP_K, tm, d // 2), lambda i: (0, i, 0)),
                pl.BlockSpec((tm, TOP_K), row),
                pl.BlockSpec((tm, d // 2), row),
                pl.BlockSpec((tm, d), off),
                pl.BlockSpec((None, None, 1, 3 * d), lambda i: (layer, (i + tile0) // tiles_per_seq, 0, 0)),
                pl.BlockSpec((d, ds), const),
                pl.BlockSpec((d, ds), const),
                pl.BlockSpec((ds, d), const)]
    args = [y_tok, w_tok, h_packed, x2, mod, sh_gate.astype(BF16), sh_up.astype(BF16), sh_down.astype(BF16)]
    aliases = {}
    if out_prev is not None:
        in_specs.append(pl.BlockSpec(memory_space=pl.ANY))
        args.append(out_prev)
        aliases = {len(args) - 1: 0}
    return pl.pallas_call(
        functools.partial(_combine_kernel, d=d),
        grid=(ntiles,),
        in_specs=in_specs,
        out_specs=pl.BlockSpec((tm, d), off),
        out_shape=jax.ShapeDtypeStruct((t_all, d), F32),
        input_output_aliases=aliases,
        compiler_params=_cparams(("arbitrary",), VMEM_LIMIT),
        name="moe_combine",
    )(*args)


def _moe_group(x2, out_prev, tile0, h_packed, gates, ranks, counts, mod, layer, w_gate, w_up, w_down,
               sh_gate, sh_up, sh_down, seq):
    t, half = h_packed.shape
    d = 2 * half
    n_exp = w_gate.shape[1]
    cnt = counts[:, 0].astype(jnp.int32)
    padded = (cnt + MOE_BLOCK - 1) // MOE_BLOCK * MOE_BLOCK
    eidx = jnp.arange(n_exp, dtype=jnp.int32)
    seg_end = jnp.sum(jnp.where(eidx[None, :] <= eidx[:, None], padded[None, :], 0), axis=1)
    seg_start = seg_end - padded
    nb = t * TOP_K // MOE_BLOCK + n_exp
    first_row = jnp.arange(nb, dtype=jnp.int32) * MOE_BLOCK
    block_expert = jnp.sum((seg_end[None, :] <= first_row[:, None]).astype(jnp.int32), axis=1)
    block_expert = jnp.minimum(block_expert, n_exp - 1)
    last_row = jnp.sum(jnp.where(block_expert[:, None] == eidx[None, :], (seg_start + cnt)[None, :], 0), axis=1)
    block_rows = jnp.clip(last_row - first_row, 0, MOE_BLOCK).astype(jnp.int32)
    pos, w = _assign_slots(ranks, gates, seg_start.astype(F32).reshape(n_exp, 1))
    xs = _sc_scatter_rows(h_packed, pos, nb * MOE_BLOCK)
    ys = _expert_ffn(xs, block_expert, block_rows, layer, w_gate, w_up, w_down)
    y_tok = _sc_gather_rows(ys, pos.reshape(TOP_K * t)).reshape(TOP_K, t, d // 2)
    return _combine(y_tok, w.T, h_packed, x2, out_prev, mod, layer, sh_gate, sh_up, sh_down, seq, tile0)


MOE_GROUPS = 1


def _moe_layer(x2, g, mod, layer, router_w, router_bias, w_gate, w_up, w_down, sh_gate, sh_up, sh_down, seq):
    ntiles = x2.shape[0] // MOE_TILE // MOE_GROUPS
    routed = [_ffn_router(x2, g, mod, layer, router_w, router_bias, seq, grp * ntiles, ntiles)
              for grp in range(MOE_GROUPS)]
    out = None
    for grp in range(MOE_GROUPS):
        out = _moe_group(x2, out, grp * ntiles, *routed[grp], mod, layer, w_gate, w_up, w_down,
                         sh_gate, sh_up, sh_down, seq)
    return out


def kernel(x, c, positions, mix_norm_g, mix_mod_w, mix_mod_b, ffn_norm_g, ffn_mod_w, ffn_mod_b,
           hyb_w_in, hyb_w_out, s5_lambda_re, s5_lambda_im, s5_log_dt, s5_b_re, s5_b_im,
           s5_c_re, s5_c_im, s5_d, s5_glu_w, s5_glu_b, attn_q_norm_g, attn_k_norm_g,
           pool_w, pool_scale, router_w, router_bias, exp_w_gate, exp_w_up, exp_w_down,
           sh_w_gate, sh_w_up, sh_w_down):
    bsz, seq, d = x.shape
    t = bsz * seq
    depth = mix_norm_g.shape[0]
    x2 = x.reshape(t, d)
    mix_mod = _mod_vectors(c, mix_mod_w, mix_mod_b).reshape(depth, bsz, 1, 3 * d)
    ffn_mod = _mod_vectors(c, ffn_mod_w, ffn_mod_b).reshape(depth, bsz, 1, 3 * d)
    for i in range(depth):
        j = i // 2
        if i % 2 == 0:
            u, q_hm, qi_hm, k, v, ki, wi = _in_projection(
                x2, mix_norm_g[i], mix_mod, i, hyb_w_in[j], positions, attn_q_norm_g[j], attn_k_norm_g[j], bsz, seq)
            weights = _s5_weights(s5_lambda_re[j], s5_lambda_im[j], s5_log_dt[j], s5_b_re[j], s5_b_im[j],
                                  s5_c_re[j], s5_c_im[j], s5_d[j], s5_glu_w[j], s5_glu_b[j])
            y_ssm = _s5_mixer(u, weights, bsz, seq)
            y_att = _dsa_attention(q_hm, qi_hm, wi, k, v, ki, bsz, seq)
            x2 = _out_projection(x2, y_ssm, y_att, hyb_w_out[j], mix_mod, i, seq)
        else:
            x2 = _pool_layer(x2, mix_norm_g[i], mix_mod, i, pool_w[j], pool_scale[j], seq)
        x2 = _moe_layer(x2, ffn_norm_g[i], ffn_mod, i, router_w[i], router_bias[i], exp_w_gate, exp_w_up, exp_w_down,
                        sh_w_gate[i], sh_w_up[i], sh_w_down[i], seq)
    return x2.reshape(bsz, seq, d)
```

```python
import functools

import numpy as np
import jax
import jax.numpy as jnp
from jax import lax
from jax.experimental import pallas as pl
from jax.experimental.pallas import tpu as pltpu
from jax.experimental.pallas import tpu_sc as plsc

F32 = jnp.float32
BF16 = jnp.bfloat16

EPS = 1e-6
S5_GROUP = 16
S5_STATE = 64
HEAD_DIM = 64
ATTN_HEADS = 8
IDX_HEADS = 8
ROPE_HALF = 8
ROPE_THETA = 500000.0
TOPK_MAX = 256
POOL_WINDOWS = (2, 4, 8, 16)
POOL_HALO = 16
N_EXPERT_GROUPS = 8
TOPK_GROUPS = 4
TOP_K = 8
ROUTED_SCALE = 2.5
S5_CHUNK = 16
NEG_BIG = -1e30
VMEM_LIMIT = 56 * 1024 * 1024


def _cparams(sem, vmem=None):
    return pltpu.CompilerParams(dimension_semantics=sem, vmem_limit_bytes=vmem)


def _dot(a, b):
    return jnp.dot(a, b, preferred_element_type=F32)


def _dot_nt(a, b):
    return lax.dot_general(a, b, (((1,), (1,)), ((), ())), preferred_element_type=F32)


def _split_bf16(a):
    hi = a.astype(BF16)
    lo = (a - hi.astype(F32)).astype(BF16)
    return hi, lo


def _norm_mod(x, g, shift, scale):
    y = x * lax.rsqrt(jnp.mean(x * x, axis=-1, keepdims=True) + EPS)
    return (y * g) * (1.0 + scale) + shift


def _mod_kernel(ct_ref, w_ref, b_ref, o_ref):
    ct = ct_ref[...]
    cs = ct * jax.nn.sigmoid(ct)
    w = w_ref[...]
    rows = [jnp.sum(w * cs[:, b:b + 1], axis=0, keepdims=True) for b in range(ct.shape[1])]
    o_ref[...] = jnp.concatenate(rows, axis=0) + b_ref[...]


def _mod_vectors(c, w, b):
    nl, d, n3 = w.shape
    bsz = c.shape[0]
    tn = 512
    return pl.pallas_call(
        _mod_kernel,
        grid=(nl, n3 // tn),
        in_specs=[pl.BlockSpec((d, bsz), lambda l, j: (0, 0)),
                  pl.BlockSpec((None, d, tn), lambda l, j: (l, 0, j)),
                  pl.BlockSpec((None, 1, tn), lambda l, j: (l, 0, j))],
        out_specs=pl.BlockSpec((None, bsz, tn), lambda l, j: (l, 0, j)),
        out_shape=jax.ShapeDtypeStruct((nl, bsz, n3), F32),
        compiler_params=_cparams(("arbitrary", "arbitrary")),
        name="mod_vectors",
    )(c.T, w, b.reshape(nl, 1, n3))


def _inproj_kernel(x_ref, g_ref, mod_ref, w_ref, pos_ref, inv_ref, qg_ref, kg_ref,
                   u_ref, q_ref, qi_ref, k_ref, v_ref, ki_ref, wi_ref, u_scr, *, d):
    x = x_ref[...]
    mod = mod_ref[...]
    h = _norm_mod(x, g_ref[...], mod[:, :d], mod[:, d:2 * d])
    proj = _dot(h.astype(BF16), w_ref[...])
    tm = x.shape[0]
    width = ATTN_HEADS * HEAD_DIM

    ang = inv_ref[...] * pos_ref[...].astype(F32)
    cos, sin = jnp.cos(ang), jnp.sin(ang)

    def head_t(xt, gain, scale):
        if scale is not None:
            r = lax.rsqrt(jnp.mean(xt * xt, axis=0, keepdims=True) + EPS) * scale
            xt = xt * gain
        x1, x2 = xt[:ROPE_HALF], xt[ROPE_HALF:2 * ROPE_HALF]
        out = jnp.concatenate([x1 * cos - x2 * sin, x1 * sin + x2 * cos, xt[2 * ROPE_HALF:]], axis=0)
        return out if scale is None else out * r

    for slab in range(width // 128):
        u_scr[slab] = proj[:, slab * 128:(slab + 1) * 128]
        for j in range(S5_CHUNK):
            piece = u_scr[slab, pl.ds(j, tm // S5_CHUNK, stride=S5_CHUNK), :]
            col = (slab * S5_CHUNK + j) * 128
            u_ref[:, col:col + 128] = piece.astype(u_ref.dtype)
    q = proj[:, width:2 * width]
    qi = proj[:, 2 * width:3 * width]
    small = proj[:, 3 * width:3 * width + 256]
    qg = qg_ref[...]
    for pair in range(ATTN_HEADS // 2):
        sl = slice(pair * 128, (pair + 1) * 128)
        qt = q[:, sl].T
        qit = qi[:, sl].T
        for half in range(2):
            hs = slice(half * HEAD_DIM, (half + 1) * HEAD_DIM)
            q_ref[2 * pair + half] = head_t(qt[hs], qg, HEAD_DIM ** -0.5).astype(q_ref.dtype)
            qi_ref[2 * pair + half] = head_t(qit[hs], None, None).astype(qi_ref.dtype)
    kvt = small[:, :128].T
    kt = head_t(kvt[:HEAD_DIM], kg_ref[...], 1.0)
    kv = jnp.concatenate([kt, kvt[HEAD_DIM:]], axis=0).T
    k_ref[...] = kv[:, :HEAD_DIM].astype(k_ref.dtype)
    v_ref[...] = kv[:, HEAD_DIM:].astype(v_ref.dtype)
    kiwt = small[:, 128:256].T
    kit = jnp.concatenate([head_t(kiwt[:HEAD_DIM], None, None), kiwt[HEAD_DIM:]], axis=0).T
    ki_ref[...] = kit[:, :HEAD_DIM].astype(ki_ref.dtype)
    wscale = (IDX_HEADS ** -0.5) * (HEAD_DIM ** -0.5)
    wi_ref[...] = kiwt[HEAD_DIM:HEAD_DIM + IDX_HEADS] * wscale


def _in_projection(x2, g, mod, layer, w_in, positions, q_norm_g, k_norm_g, bsz, seq):
    t, d = x2.shape
    tm = 512
    width = ATTN_HEADS * HEAD_DIM
    cuts = np.cumsum([width, width, HEAD_DIM, HEAD_DIM, width, HEAD_DIM, IDX_HEADS])
    u_w, q_w, k_w, v_w, qi_w, ki_w, wi_w = jnp.split(w_in, cuts[:-1].tolist(), axis=1)
    pad = jnp.zeros((d, 128 - HEAD_DIM - IDX_HEADS), w_in.dtype)
    w = jnp.concatenate([u_w, q_w, qi_w, k_w, v_w, ki_w, wi_w, pad], axis=1).astype(BF16)
    nw = w.shape[1]
    inv = np.power(ROPE_THETA, -2.0 * np.arange(ROPE_HALF) / (2 * ROPE_HALF))
    inv = jnp.asarray(inv.reshape(ROPE_HALF, 1), F32)
    tiles_per_seq = seq // tm
    row = lambda i: (i, 0)
    const = lambda i: (0, 0)
    hm = lambda i: (i // tiles_per_seq, 0, 0, i % tiles_per_seq)
    outs = pl.pallas_call(
        functools.partial(_inproj_kernel, d=d),
        grid=(t // tm,),
        in_specs=[pl.BlockSpec((tm, d), row),
                  pl.BlockSpec((1, d), const),
                  pl.BlockSpec((None, None, 1, 3 * d), lambda i: (layer, i // tiles_per_seq, 0, 0)),
                  pl.BlockSpec((d, nw), const),
                  pl.BlockSpec((None, 1, tm), lambda i: (i, 0, 0)),
                  pl.BlockSpec((ROPE_HALF, 1), const),
                  pl.BlockSpec((HEAD_DIM, 1), const),
                  pl.BlockSpec((HEAD_DIM, 1), const)],
        out_specs=[pl.BlockSpec((tm // S5_CHUNK, width * S5_CHUNK), row),
                   pl.BlockSpec((None, ATTN_HEADS, HEAD_DIM, tm), hm),
                   pl.BlockSpec((None, IDX_HEADS, HEAD_DIM, tm), hm),
                   pl.BlockSpec((tm, HEAD_DIM), row),
                   pl.BlockSpec((tm, HEAD_DIM), row),
                   pl.BlockSpec((tm, HEAD_DIM), row),
                   pl.BlockSpec((IDX_HEADS, tm), lambda i: (0, i))],
        out_shape=[jax.ShapeDtypeStruct((t // S5_CHUNK, width * S5_CHUNK), BF16),
                   jax.ShapeDtypeStruct((bsz, ATTN_HEADS, HEAD_DIM, seq), BF16),
                   jax.ShapeDtypeStruct((bsz, IDX_HEADS, HEAD_DIM, seq), BF16),
                   jax.ShapeDtypeStruct((t, HEAD_DIM), BF16),
                   jax.ShapeDtypeStruct((t, HEAD_DIM), BF16),
                   jax.ShapeDtypeStruct((t, HEAD_DIM), BF16),
                   jax.ShapeDtypeStruct((IDX_HEADS, t), F32)],
        scratch_shapes=[pltpu.VMEM((width // 128, tm, 128), F32)],
        compiler_params=_cparams(("arbitrary",), VMEM_LIMIT),
        name="in_projection",
    )(x2, g.reshape(1, d), mod, w, positions.reshape(t // tm, 1, tm), inv,
      q_norm_g.reshape(HEAD_DIM, 1), k_norm_g.reshape(HEAD_DIM, 1))
    return outs


def _s5_weights(lam_re, lam_im, log_dt, b_re, b_im, c_re, c_im, d_skip, glu_w, glu_b):
    L = S5_CHUNK
    g, p = lam_re.shape
    hch = S5_GROUP
    dt = jnp.exp(log_dt)[:, None]
    lr, li = lam_re, lam_im
    tau = jnp.arange(L + 1, dtype=F32)[:, None, None]
    mag = jnp.exp(lr * dt * tau)
    pw_r, pw_i = mag * jnp.cos(li * dt * tau), mag * jnp.sin(li * dt * tau)
    ar, ai = pw_r[1], pw_i[1]
    nr, ni = ar - 1.0, ai
    den = lr * lr + li * li
    cr, ci = (nr * lr + ni * li) / den, (ni * lr - nr * li) / den
    bb_r = cr[..., None] * b_re - ci[..., None] * b_im
    bb_i = cr[..., None] * b_im + ci[..., None] * b_re
    cl_r = c_re[None] * pw_r[:L, :, None, :] - c_im[None] * pw_i[:L, :, None, :]
    cl_i = c_re[None] * pw_i[:L, :, None, :] + c_im[None] * pw_r[:L, :, None, :]
    taps = (jnp.einsum('tghp,gpk->tghk', cl_r, bb_r) - jnp.einsum('tghp,gpk->tghk', cl_i, bb_i))
    ns = g // 8
    eye = jnp.eye(8, dtype=F32)
    kd = jnp.einsum('tsgoh,gf->tsghfo', taps.reshape(L, ns, 8, hch, hch), eye).reshape(L, ns, 128, 128)
    zero = jnp.zeros_like(kd[0])
    k2 = jnp.stack([jnp.concatenate([jnp.concatenate([kd[2 * dd], kd[2 * dd + 1]], axis=-1),
                                     jnp.concatenate([kd[2 * dd - 1] if dd else zero, kd[2 * dd]], axis=-1)], axis=-2)
                    for dd in range(L // 2)], axis=1).astype(BF16)
    ii = jnp.arange(L)
    rev_r, rev_i = pw_r[L - 1 - ii], pw_i[L - 1 - ii]
    z_r = rev_r[..., None] * bb_r[None] - rev_i[..., None] * bb_i[None]
    z_i = rev_r[..., None] * bb_i[None] + rev_i[..., None] * bb_r[None]
    zc = jnp.concatenate([z_r, z_i], axis=2)
    zc = jnp.transpose(zc.reshape(L, ns, 8, 2 * p, hch), (1, 0, 2, 4, 3)).reshape(ns, L * 128, 1, 2 * p)
    row_group = (jnp.arange(L * 128) // hch) % 8
    w_z = jnp.where((row_group[:, None] == jnp.arange(8)[None, :])[None, :, :, None], zc, 0.0)
    w_z = w_z.reshape(ns, L // 2, 2 * 128, 8 * 2 * p).astype(BF16)
    co_r = c_re[None] * pw_r[1:, :, None, :] - c_im[None] * pw_i[1:, :, None, :]
    co_i = c_re[None] * pw_i[1:, :, None, :] + c_im[None] * pw_r[1:, :, None, :]
    cc = jnp.concatenate([co_r, -co_i], axis=-1)
    cc = jnp.transpose(cc.reshape(L, ns, 8, hch, 2 * p), (1, 4, 0, 2, 3)).reshape(ns, 1, 2 * p, L * 128)
    col_group = (jnp.arange(L * 128) // hch) % 8
    w_c = jnp.where((jnp.arange(8)[:, None] == col_group[None, :])[None, :, None, :], cc, 0.0)
    w_c = w_c.reshape(ns, 8 * 2 * p, L * 128)
    w_ch = w_c.astype(BF16)
    w_cl = (w_c - w_ch.astype(F32)).astype(BF16)
    al_r, al_i = pw_r[L], pw_i[L]
    dec_a = jnp.concatenate([al_r, al_r], axis=-1).reshape(ns, 1, 8 * 2 * p)
    dec_b = jnp.concatenate([-al_i, al_i], axis=-1).reshape(ns, 1, 8 * 2 * p)
    d_t = jnp.tile(d_skip.reshape(ns, 1, 128), (1, 1, L))
    gl = jnp.einsum('sghk,gf->sghfk', glu_w.reshape(ns, 8, hch, hch), eye).reshape(ns, 128, 128)
    glu2 = jnp.einsum('ab,shk->sahbk', jnp.eye(2, dtype=F32), gl).reshape(ns, 256, 256).astype(BF16)
    glu_bt = jnp.tile(glu_b.reshape(ns, 1, 128), (1, 1, L))
    return k2, w_z, w_ch, w_cl, dec_a, dec_b, d_t, glu2, glu_bt


def _s5_kernel(u_ref, k2_ref, wz_ref, wch_ref, wcl_ref, da_ref, db_ref, d_ref, g2_ref, gb_ref, o_ref,
               zp_ref, zq_ref, sp_ref, *, nchunk):
    nblk = S5_CHUNK // 2
    ub = [u_ref[:, j * 256:(j + 1) * 256] for j in range(nblk)]
    z = _dot(ub[0], wz_ref[0])
    for j in range(1, nblk):
        z = z + _dot(ub[j], wz_ref[j])
    zp_ref[...] = z
    half = S5_STATE
    zq_ref[...] = jnp.concatenate([pltpu.roll(z[:, s * 2 * half:(s + 1) * 2 * half], half, 1)
                                   for s in range(z.shape[1] // (2 * half))], axis=1)
    da = da_ref[...]
    db = db_ref[...]
    dbq = -db

    def step(c, carry):
        sp, sq = carry
        sp_ref[pl.ds(c, 1), :] = sp
        sp_new = da * sp + db * sq + zp_ref[pl.ds(c, 1), :]
        sq_new = da * sq + dbq * sp + zq_ref[pl.ds(c, 1), :]
        return sp_new, sq_new

    zero = jnp.zeros((1, z.shape[1]), F32)
    lax.fori_loop(0, nchunk, step, (zero, zero))
    sh, sl = _split_bf16(sp_ref[...])
    wch = wch_ref[...]
    y_inter = _dot(sh, wch) + (_dot(sh, wcl_ref[...]) + _dot(sl, wch))
    for i in range(nblk):
        cs = slice(i * 256, (i + 1) * 256)
        acc = y_inter[:, cs] + d_ref[:, cs] * ub[i].astype(F32)
        for j in range(i + 1):
            acc = acc + _dot(ub[j], k2_ref[i - j])
        y = jax.nn.gelu(acc)
        y = y * jax.nn.sigmoid(_dot(y.astype(BF16), g2_ref[...]) + gb_ref[:, cs])
        for a in range(2):
            o_ref[pl.ds(2 * i + a, nchunk, stride=S5_CHUNK), :] = y[:, a * 128:(a + 1) * 128]


def _s5_mixer(u2, weights, bsz, seq):
    k2, w_z, w_ch, w_cl, dec_a, dec_b, d_t, glu2, glu_bt = weights
    ns = k2.shape[0]
    nchunk = seq // S5_CHUNK
    cols = S5_CHUNK * 128
    st = w_z.shape[-1]
    slab3 = lambda b, s: (s, 0, 0)
    slab4 = lambda b, s: (s, 0, 0, 0)
    return pl.pallas_call(
        functools.partial(_s5_kernel, nchunk=nchunk),
        grid=(bsz, ns),
        in_specs=[pl.BlockSpec((nchunk, cols), lambda b, s: (b, s)),
                  pl.BlockSpec((None,) + k2.shape[1:], slab4),
                  pl.BlockSpec((None,) + w_z.shape[1:], slab4),
                  pl.BlockSpec((None, st, cols), slab3),
                  pl.BlockSpec((None, st, cols), slab3),
                  pl.BlockSpec((None, 1, st), slab3),
                  pl.BlockSpec((None, 1, st), slab3),
                  pl.BlockSpec((None, 1, cols), slab3),
                  pl.BlockSpec((None, 256, 256), slab3),
                  pl.BlockSpec((None, 1, cols), slab3)],
        out_specs=pl.BlockSpec((seq, 128), lambda b, s: (b, s)),
        out_shape=jax.ShapeDtypeStruct((bsz * seq, ns * 128), F32),
        scratch_shapes=[pltpu.VMEM((nchunk, st), F32), pltpu.VMEM((nchunk, st), F32), pltpu.VMEM((nchunk, st), F32)],
        compiler_params=_cparams(("arbitrary", "arbitrary"), VMEM_LIMIT),
        name="s5_mixer",
    )(u2, k2, w_z, w_ch, w_cl, dec_a, dec_b, d_t, glu2, glu_bt)


def _dsa_kernel(qt_ref, qit_ref, wt_ref, k_ref, vt_ref, ki_ref, o_ref,
                sc_ref, s_ref, m_ref, l_ref, acc_ref, *, tq, ck, k_sel, idx_bits, max_iter):
    i = pl.program_id(1)
    nheads = qt_ref.shape[0]
    n_ck = (i * tq + tq + ck - 1) // ck
    t_pos = i * tq + lax.broadcasted_iota(jnp.int32, (1, tq), 1)
    krow = lax.broadcasted_iota(jnp.int32, (ck, 1), 0)
    kf = float(k_sel)
    inf = jnp.inf

    fr = 32
    sub = min(32768 // tq, ck)

    def fold(x, op):
        return op(x.reshape(ck // fr, fr, tq), axis=0)

    wt = wt_ref[...]

    def score_chunk(c, carry):
        lo8, hi8 = carry
        for part in range(ck // sub):
            off = pl.multiple_of(c * ck + part * sub, sub)
            kic = ki_ref[pl.ds(off, sub), :]
            acc = jnp.zeros((sub, tq), F32)
            for h in range(nheads):
                acc = acc + wt[h:h + 1, :] * jnp.maximum(_dot(kic, qit_ref[h]), 0.0)
            acc = acc + 0.0
            vis = off + krow[:sub] <= t_pos
            sc_ref[c, part * sub:(part + 1) * sub, :] = jnp.where(vis, acc, -inf)
            hi8 = jnp.maximum(hi8, jnp.max(jnp.where(vis, acc, -inf).reshape(sub // fr, fr, tq), axis=0))
            lo8 = jnp.minimum(lo8, jnp.min(jnp.where(vis, acc, inf).reshape(sub // fr, fr, tq), axis=0))
        return lo8, hi8

    lo8, hi8 = lax.fori_loop(0, n_ck, score_chunk, (jnp.full((fr, tq), inf, F32), jnp.full((fr, tq), -inf, F32)))
    lo = jnp.min(lo8, axis=0, keepdims=True)
    hi = jnp.max(hi8, axis=0, keepdims=True)
    n_vis = (t_pos + 1).astype(F32)
    few = n_vis < kf

    def reduce_keys(fn, init, combine, op):
        def body(c, part):
            return combine(part, fold(fn(sc_ref[c], c * ck), op))
        return op(lax.fori_loop(0, n_ck, body, jnp.full((fr, tq), init, F32)), axis=0, keepdims=True)

    def count(pred):
        return reduce_keys(lambda x, off: jnp.where(pred(x, off), 1.0, 0.0), 0.0, jnp.add, jnp.sum)

    def bisect(lo_, hi_, c_lo):
        mid = lo_ + (hi_ - lo_) * 0.5
        cnt = count(lambda x, off: x >= mid)
        up = cnt >= kf
        return jnp.where(up, mid, lo_), jnp.where(up, hi_, mid), jnp.where(up, cnt, c_lo)

    def status(lo_, hi_, c_lo):
        mid = lo_ + (hi_ - lo_) * 0.5
        unresolved = jnp.where(few, 0.0, jnp.where(c_lo == kf, 0.0, 1.0))
        movable = jnp.where(mid > lo_, jnp.where(mid < hi_, 1.0, 0.0), 0.0)
        return jnp.max(unresolved), jnp.max(unresolved * movable)

    def search_body(st):
        it, lo_, hi_, c_lo, _, _ = st
        for _ in range(2):
            lo_, hi_, c_lo = bisect(lo_, hi_, c_lo)
        return (it + 2, lo_, hi_, c_lo) + status(lo_, hi_, c_lo)

    st = lax.while_loop(lambda st: jnp.logical_and(st[0] < max_iter, st[5] > 0.5), search_body,
                        (jnp.int32(0), lo, hi, n_vis) + status(lo, hi, n_vis))
    lo, open_ = st[1], st[4]

    def resolve():
        def walk(thr_, strict):
            keep = (lambda x: x > thr_) if strict else (lambda x: x >= thr_)
            return reduce_keys(lambda x, off: jnp.where(keep(x), x, inf), inf, jnp.minimum, jnp.min)

        def above_equal(thr_):
            return count(lambda x, off: x > thr_), count(lambda x, off: x == thr_)

        def too_low(n_gt):
            return jnp.where(few, 0.0, jnp.where(n_gt >= kf, 1.0, 0.0))

        thr0 = walk(lo, False)

        def walk_body(st):
            thr_, n_gt, _ = st
            thr2 = jnp.where(too_low(n_gt) > 0.5, walk(thr_, True), thr_)
            return (thr2,) + above_equal(thr2)

        thr_, n_gt, n_eq = lax.while_loop(lambda st: jnp.max(too_low(st[1])) > 0.5, walk_body,
                                          (thr0,) + above_equal(thr0))
        need = kf - n_gt

        def tie_cut():
            def mark_equal(c, _):
                s_ref[0, c] = jnp.where(sc_ref[c] == thr_, 1.0, 0.0)
                return 0

            lax.fori_loop(0, n_ck, mark_equal, 0)

            def idx_step(b, cur):
                trial = cur | (jnp.int32(1) << (idx_bits - 1 - b))

                def body(c, part):
                    return part + fold(jnp.where(c * ck + krow < trial, s_ref[0, c], 0.0), jnp.sum)
                cnt = jnp.sum(lax.fori_loop(0, n_ck, body, jnp.zeros((fr, tq), F32)), axis=0, keepdims=True)
                return jnp.where(cnt < need, trial, cur)

            return lax.fori_loop(0, idx_bits, idx_step, jnp.zeros((1, tq), jnp.int32))

        tied = jnp.max(jnp.where(few, 0.0, jnp.where(n_eq > need, 1.0, 0.0))) > 0.5
        return thr_, lax.cond(tied, tie_cut, lambda: jnp.full((1, tq), 0x7FFFFFFF, jnp.int32))

    thr, cut = lax.cond(open_ > 0.5, resolve, lambda: (lo, jnp.full((1, tq), 0x7FFFFFFF, jnp.int32)))

    for h in range(nheads):
        m_ref[h] = jnp.full((fr, tq), NEG_BIG, F32)
        l_ref[h] = jnp.zeros((fr, tq), F32)
        acc_ref[h] = jnp.zeros((HEAD_DIM, tq), F32)

    def attn_scores(c, _):
        off = pl.multiple_of(c * ck, ck)
        kc = k_ref[pl.ds(off, ck), :]
        x = sc_ref[c]
        at_thr = jnp.where(x == thr, jnp.where(off + krow <= cut, 0.0, NEG_BIG), NEG_BIG)
        bias = jnp.where(x > thr, 0.0, at_thr)
        for h in range(nheads):
            s = _dot(kc, qt_ref[h]) + bias
            s_ref[h, c] = s
            m_ref[h] = jnp.maximum(m_ref[h], fold(s, jnp.max))
        return 0

    lax.fori_loop(0, n_ck, attn_scores, 0)
    for h in range(nheads):
        m_ref[h] = jnp.broadcast_to(jnp.max(m_ref[h], axis=0, keepdims=True), (fr, tq))

    def attn_values(c, _):
        vt = vt_ref[c]
        for h in range(nheads):
            p = jnp.exp(s_ref[h, c] - m_ref[h][:1, :])
            l_ref[h] += fold(p, jnp.sum)
            acc_ref[h] += _dot(vt, p.astype(BF16))
        return 0

    lax.fori_loop(0, n_ck, attn_values, 0)
    for h in range(nheads):
        l_row = jnp.sum(l_ref[h], axis=0, keepdims=True)
        o_ref[h * HEAD_DIM:(h + 1) * HEAD_DIM, :] = (acc_ref[h] / l_row).astype(o_ref.dtype)


def _dsa_attention(qt, qit, wt, k, v, ki, bsz, seq):
    tq = 256
    ck = min(512, seq)
    n_ck = seq // ck
    k_sel = min(TOPK_MAX, seq // 4)
    nq = seq // tq
    nh = ATTN_HEADS
    width = nh * HEAD_DIM
    vt = v.reshape(bsz, n_ck, ck, HEAD_DIM).transpose(0, 1, 3, 2)
    qspec = pl.BlockSpec((None, nh, HEAD_DIM, tq), lambda b, i: (b, 0, 0, i))
    out_t = pl.pallas_call(
        functools.partial(_dsa_kernel, tq=tq, ck=ck, k_sel=k_sel, idx_bits=(seq - 1).bit_length(), max_iter=22),
        grid=(bsz, nq),
        in_specs=[qspec, qspec,
                  pl.BlockSpec((IDX_HEADS, tq), lambda b, i: (0, b * nq + i)),
                  pl.BlockSpec((seq, HEAD_DIM), lambda b, i: (b, 0)),
                  pl.BlockSpec((None, n_ck, HEAD_DIM, ck), lambda b, i: (b, 0, 0, 0)),
                  pl.BlockSpec((seq, HEAD_DIM), lambda b, i: (b, 0))],
        out_specs=pl.BlockSpec((None, width, tq), lambda b, i: (b, 0, i)),
        out_shape=jax.ShapeDtypeStruct((bsz, width, seq), BF16),
        scratch_shapes=[pltpu.VMEM((n_ck, ck, tq), F32),
                        pltpu.VMEM((nh, n_ck, ck, tq), F32),
                        pltpu.VMEM((nh, 32, tq), F32),
                        pltpu.VMEM((nh, 32, tq), F32),
                        pltpu.VMEM((nh, HEAD_DIM, tq), F32)],
        compiler_params=_cparams(("arbitrary", "arbitrary"), VMEM_LIMIT),
        name="dsa_attention",
    )(qt, qit, wt, k, vt, ki)
    return out_t.transpose(0, 2, 1).reshape(bsz * seq, width)


def _outproj_kernel(x_ref, ys_ref, ya_ref, w_ref, mod_ref, o_ref, *, d):
    width = ys_ref.shape[1]
    y = _dot(ys_ref[...].astype(BF16), w_ref[:width, :]) + _dot(ya_ref[...], w_ref[width:, :])
    o_ref[...] = x_ref[...] + mod_ref[...][:, 2 * d:] * y


def _out_projection(x2, y_ssm, y_att, w_out, mod, layer, seq):
    t, d = x2.shape
    tm = 512
    width = y_ssm.shape[1]
    tiles_per_seq = seq // tm
    row = lambda i: (i, 0)
    return pl.pallas_call(
        functools.partial(_outproj_kernel, d=d),
        grid=(t // tm,),
        in_specs=[pl.BlockSpec((tm, d), row),
                  pl.BlockSpec((tm, width), row),
                  pl.BlockSpec((tm, width), row),
                  pl.BlockSpec((2 * width, d), lambda i: (0, 0)),
                  pl.BlockSpec((None, None, 1, 3 * d), lambda i: (layer, i // tiles_per_seq, 0, 0))],
        out_specs=pl.BlockSpec((tm, d), row),
        out_shape=jax.ShapeDtypeStruct((t, d), F32),
        compiler_params=_cparams(("arbitrary",)),
        name="out_projection",
    )(x2, y_ssm, y_att, w_out.astype(BF16), mod)


def _pool_kernel(x_ref, halo_ref, g_ref, mod_ref, pw_ref, ps_ref, o_ref, h_ref, *, d, tiles_per_seq):
    i = pl.program_id(0)
    tm = x_ref.shape[0]
    mod = mod_ref[...]
    shift, scale, gate = mod[:, :d], mod[:, d:2 * d], mod[:, 2 * d:]
    x = x_ref[...]
    first = (i % tiles_per_seq) == 0
    h_ref[POOL_HALO:, :] = _norm_mod(x, g_ref[...], shift, scale)
    halo = _norm_mod(halo_ref[...], g_ref[...], shift, scale)
    h_ref[:POOL_HALO, :] = jnp.where(first, 0.0, halo)
    pos = (i % tiles_per_seq) * tm + lax.broadcasted_iota(jnp.int32, (tm, 1), 0)
    gw = d // len(POOL_WINDOWS)
    ys = []
    for gi, win in enumerate(POOL_WINDOWS):
        cs = slice(gi * gw, (gi + 1) * gw)
        cur = h_ref[POOL_HALO:, cs]
        tot = cur
        for k in range(1, win):
            tot = tot + h_ref[POOL_HALO - k:POOL_HALO - k + tm, cs]
        cnt = jnp.minimum(pos + 1, win).astype(F32)
        pooled = tot / cnt - cur
        ys.append(_dot(pooled.astype(BF16), pw_ref[gi]))
    y = jnp.concatenate(ys, axis=1) * ps_ref[...]
    o_ref[...] = x + gate * y


def _pool_layer(x2, g, mod, layer, pool_w, pool_scale, seq):
    t, d = x2.shape
    tm = 512
    tiles_per_seq = seq // tm
    gw = d // len(POOL_WINDOWS)
    hb = tm // POOL_HALO
    return pl.pallas_call(
        functools.partial(_pool_kernel, d=d, tiles_per_seq=tiles_per_seq),
        grid=(t // tm,),
        in_specs=[pl.BlockSpec((tm, d), lambda i: (i, 0)),
                  pl.BlockSpec((POOL_HALO, d), lambda i: (jnp.maximum(i * hb - 1, 0), 0)),
                  pl.BlockSpec((1, d), lambda i: (0, 0)),
                  pl.BlockSpec((None, None, 1, 3 * d), lambda i: (layer, i // tiles_per_seq, 0, 0)),
                  pl.BlockSpec((len(POOL_WINDOWS), gw, gw), lambda i: (0, 0, 0)),
                  pl.BlockSpec((1, d), lambda i: (0, 0))],
        out_specs=pl.BlockSpec((tm, d), lambda i: (i, 0)),
        out_shape=jax.ShapeDtypeStruct((t, d), F32),
        scratch_shapes=[pltpu.VMEM((tm + POOL_HALO, d), F32)],
        compiler_params=_cparams(("arbitrary",)),
        name="pool_mixer",
    )(x2, x2, g.reshape(1, d), mod, pool_w.astype(BF16), pool_scale.reshape(1, d))


def _first_max(vals, idx, big):
    m = jnp.max(vals, axis=0, keepdims=True)
    first = jnp.min(jnp.where(vals == m, idx, big), axis=0, keepdims=True)
    return m, first


def _pack_bf16_pairs(x):
    n = x.shape[1] // 2
    lo = pltpu.bitcast(x[:, :n].astype(BF16).astype(F32), jnp.int32)
    hi = pltpu.bitcast(x[:, n:].astype(BF16).astype(F32), jnp.int32)
    return hi | ((lo >> 16) & 0xFFFF)


def _unpack_bf16_pairs(w):
    lo = pltpu.bitcast(w << 16, F32)
    hi = pltpu.bitcast(w & jnp.int32(-65536), F32)
    return jnp.concatenate([lo, hi], axis=1)


def _router_kernel(x_ref, g_ref, mod_ref, rw_ref, rb_ref, tri_ref, h_ref, gate_ref, rank_ref, cnt_ref,
                   run_ref, *, d, n_exp):
    mod = mod_ref[...]
    h = _norm_mod(x_ref[...], g_ref[...], mod[:, :d], mod[:, d:2 * d])
    h_ref[...] = _pack_bf16_pairs(h)
    tm = h.shape[0]
    hh, hl = _split_bf16(h)
    rw = rw_ref[...]
    rh, rl = _split_bf16(rw)
    logits = _dot_nt(rh, hh) + (_dot_nt(rh, hl) + _dot_nt(rl, hh))
    scores = jax.nn.sigmoid(logits)
    sel = scores + rb_ref[...]
    gsz = n_exp // N_EXPERT_GROUPS
    neg = -jnp.inf
    shape3 = (N_EXPERT_GROUPS, gsz, tm)
    sel3 = sel.reshape(shape3)
    sub = lax.broadcasted_iota(jnp.int32, shape3, 1)
    m1 = jnp.max(sel3, axis=1, keepdims=True)
    f1 = jnp.min(jnp.where(sel3 == m1, sub, gsz), axis=1, keepdims=True)
    m2 = jnp.max(jnp.where(sub == f1, neg, sel3), axis=1, keepdims=True)
    gscore = jnp.broadcast_to(m1 + m2, shape3).reshape(n_exp, tm)
    eidx = lax.broadcasted_iota(jnp.int32, (n_exp, tm), 0)
    gidx = eidx // gsz
    keep = jnp.zeros((n_exp, tm), F32)
    work = gscore
    for _ in range(TOPK_GROUPS):
        _, first = _first_max(work, gidx, N_EXPERT_GROUPS)
        hit = gidx == first
        keep = jnp.where(hit, 1.0, keep)
        work = jnp.where(hit, neg, work)
    work = jnp.where(keep > 0.0, sel, neg)
    chosen = jnp.zeros((n_exp, tm), F32)
    for _ in range(TOP_K):
        _, first = _first_max(work, eidx, n_exp)
        hit = eidx == first
        chosen = jnp.where(hit, 1.0, chosen)
        work = jnp.where(hit, neg, work)
    picked = chosen * scores
    gate_ref[...] = picked / jnp.sum(picked, axis=0, keepdims=True) * ROUTED_SCALE

    @pl.when(pl.program_id(0) == 0)
    def _():
        run_ref[...] = jnp.zeros(run_ref.shape, F32)

    before = _dot(chosen.astype(BF16), tri_ref[...])
    run = run_ref[...]
    rank_ref[...] = jnp.where(chosen > 0.0, before + run[:, :1], -1.0)
    run = run + jnp.broadcast_to(jnp.sum(chosen, axis=1, keepdims=True), run.shape)
    run_ref[...] = run
    cnt_ref[...] = run


MOE_TILE = 512


def _ffn_router(x2, g, mod, layer, router_w, router_bias, seq, tile0, ntiles):
    d = x2.shape[1]
    n_exp = router_w.shape[1]
    tm = MOE_TILE
    t = ntiles * tm
    tiles_per_seq = seq // tm
    tri = (np.arange(tm)[:, None] < np.arange(tm)[None, :]).astype(np.float32)
    return pl.pallas_call(
        functools.partial(_router_kernel, d=d, n_exp=n_exp),
        grid=(ntiles,),
        in_specs=[pl.BlockSpec((tm, d), lambda i: (i + tile0, 0)),
                  pl.BlockSpec((1, d), lambda i: (0, 0)),
                  pl.BlockSpec((None, None, 1, 3 * d), lambda i: (layer, (i + tile0) // tiles_per_seq, 0, 0)),
                  pl.BlockSpec((n_exp, d), lambda i: (0, 0)),
                  pl.BlockSpec((n_exp, 1), lambda i: (0, 0)),
                  pl.BlockSpec((tm, tm), lambda i: (0, 0))],
        out_specs=[pl.BlockSpec((tm, d // 2), lambda i: (i, 0)),
                   pl.BlockSpec((n_exp, tm), lambda i: (0, i)),
                   pl.BlockSpec((n_exp, tm), lambda i: (0, i)),
                   pl.BlockSpec((n_exp, 128), lambda i: (0, 0))],
        out_shape=[jax.ShapeDtypeStruct((t, d // 2), jnp.int32),
                   jax.ShapeDtypeStruct((n_exp, t), F32),
                   jax.ShapeDtypeStruct((n_exp, t), F32),
                   jax.ShapeDtypeStruct((n_exp, 128), F32)],
        scratch_shapes=[pltpu.VMEM((n_exp, 128), F32)],
        compiler_params=_cparams(("arbitrary",)),
        name="ffn_router",
    )(x2, g.reshape(1, d), mod, router_w.T, router_bias.reshape(n_exp, 1), jnp.asarray(tri, BF16))


def _assign_kernel(rank_ref, gate_ref, start_ref, pos_ref, w_ref, *, n_exp):
    rank = rank_ref[...]
    gates = gate_ref[...]
    tm = rank.shape[1]
    slot = rank + start_ref[...]
    eidx = lax.broadcasted_iota(jnp.int32, (n_exp, tm), 0).astype(F32)
    alive = jnp.where(rank >= 0.0, eidx, float(n_exp))
    kidx = lax.broadcasted_iota(jnp.int32, (TOP_K, tm), 0)
    pos = jnp.zeros((TOP_K, tm), F32)
    wts = jnp.zeros((TOP_K, tm), F32)
    for k in range(TOP_K):
        first = jnp.min(alive, axis=0, keepdims=True)
        hit = alive == first
        pos_k = jnp.sum(jnp.where(hit, slot, 0.0), axis=0, keepdims=True)
        w_k = jnp.sum(jnp.where(hit, gates, 0.0), axis=0, keepdims=True)
        pos = jnp.where(kidx == k, pos_k, pos)
        wts = jnp.where(kidx == k, w_k, wts)
        alive = jnp.where(hit, float(n_exp), alive)
    pos_ref[...] = pos.astype(jnp.int32)
    w_ref[...] = wts


def _assign_slots(ranks, gates, start):
    n_exp, t = ranks.shape
    tm = 512
    return pl.pallas_call(
        functools.partial(_assign_kernel, n_exp=n_exp),
        grid=(t // tm,),
        in_specs=[pl.BlockSpec((n_exp, tm), lambda i: (0, i)),
                  pl.BlockSpec((n_exp, tm), lambda i: (0, i)),
                  pl.BlockSpec((n_exp, 1), lambda i: (0, 0))],
        out_specs=[pl.BlockSpec((TOP_K, tm), lambda i: (0, i)),
                   pl.BlockSpec((TOP_K, tm), lambda i: (0, i))],
        out_shape=[jax.ShapeDtypeStruct((TOP_K, t), jnp.int32), jax.ShapeDtypeStruct((TOP_K, t), F32)],
        compiler_params=_cparams(("arbitrary",)),
        name="moe_assign",
    )(ranks, gates, start)


SC_CORES = 2
SC_SUBCORES = 16
SC_WINDOW = 128


def _sc_mesh():
    return plsc.VectorSubcoreMesh(core_axis_name="c", subcore_axis_name="s",
                                  num_cores=SC_CORES, num_subcores=SC_SUBCORES)


def _sc_scatter_rows(rows, pos, n_out):
    t, width = rows.shape
    nk = pos.shape[0]
    win = SC_WINDOW
    n_win = t // win // (SC_CORES * SC_SUBCORES)
    pos_w = pos.reshape(nk, t // win, win).transpose(1, 0, 2)

    def body(rows_hbm, pos_hbm, out_hbm, idx_v, buf_v, sem):
        wid = lax.axis_index("s") * SC_CORES + lax.axis_index("c")

        @pl.loop(0, n_win)
        def _(j):
            w = wid * n_win + j
            pltpu.sync_copy(rows_hbm.at[pl.ds(w * win, win)], buf_v)
            pltpu.sync_copy(pos_hbm.at[w], idx_v)
            copies = [pltpu.make_async_copy(buf_v, out_hbm.at[idx_v.at[k]], sem) for k in range(nk)]
            for cp in copies:
                cp.start()
            for cp in copies:
                cp.wait()

    return pl.kernel(
        body, out_type=jax.ShapeDtypeStruct((n_out, width), jnp.int32), mesh=_sc_mesh(),
        scratch_types=[pltpu.VMEM((nk, win), jnp.int32), pltpu.VMEM((win, width), jnp.int32),
                       pltpu.SemaphoreType.DMA],
        name="sc_scatter_rows",
    )(rows, pos_w)


SC_GATHER_WINDOW = 64


def _sc_gather_rows(table, pos_flat):
    m = pos_flat.shape[0]
    width = table.shape[1]
    win = SC_GATHER_WINDOW
    workers = SC_CORES * SC_SUBCORES
    n_win = m // win // workers
    pos_w = pos_flat.reshape(workers, n_win, win)

    def body(table_hbm, pos_hbm, out_hbm, idx_v, buf_v, sem_g, sem_w):
        wid = lax.axis_index("s") * SC_CORES + lax.axis_index("c")
        base = wid * (n_win * win)
        pltpu.sync_copy(pos_hbm.at[wid], idx_v)

        def gather(j, b):
            return pltpu.make_async_copy(table_hbm.at[idx_v.at[j]], buf_v.at[b], sem_g.at[b])

        def write(j, b):
            return pltpu.make_async_copy(buf_v.at[b], out_hbm.at[pl.ds(base + j * win, win)], sem_w.at[b])

        gather(0, 0).start()

        @pl.loop(0, n_win, step=2)
        def _(j0):
            for b in range(2):
                j = j0 + b
                gather(j, b).wait()
                write(j, b).start()

                @pl.when(j + 1 < n_win)
                def _():
                    @pl.when(j >= 1)
                    def _():
                        write(j - 1, 1 - b).wait()
                    gather(j + 1, 1 - b).start()

        write(n_win - 2, 0).wait()
        write(n_win - 1, 1).wait()

    return pl.kernel(
        body, out_type=jax.ShapeDtypeStruct((m, width), jnp.int32), mesh=_sc_mesh(),
        scratch_types=[pltpu.VMEM((n_win, win), jnp.int32), pltpu.VMEM((2, win, width), jnp.int32),
                       pltpu.SemaphoreType.DMA((2,)), pltpu.SemaphoreType.DMA((2,))],
        name="sc_gather_rows",
    )(table, pos_w)


MOE_BLOCK = 1024


def _silu_mul(a, b):
    return (a * jax.nn.sigmoid(a)) * b


def _expert_ffn_kernel(be_ref, nv_ref, xs_ref, wg_ref, wu_ref, wd_ref, ys_ref, wg_bf, wu_bf, wd_bf):
    i = pl.program_id(0)
    fresh = jnp.logical_or(i == 0, be_ref[i] != be_ref[jnp.maximum(i - 1, 0)])

    @pl.when(fresh)
    def _():
        wg_bf[...] = wg_ref[...].astype(BF16)
        wu_bf[...] = wu_ref[...].astype(BF16)
        wd_bf[...] = wd_ref[...].astype(BF16)

    nv = nv_ref[i]

    @pl.when(nv > 0)
    def _():
        x = _unpack_bf16_pairs(xs_ref[...])
        row = lax.broadcasted_iota(jnp.int32, (x.shape[0], 1), 0)
        x = jnp.where(row < nv, x, 0.0).astype(BF16)
        mid = _silu_mul(_dot(x, wg_bf[...]), _dot(x, wu_bf[...]))
        ys_ref[...] = _pack_bf16_pairs(_dot(mid.astype(BF16), wd_bf[...]))

    @pl.when(nv <= 0)
    def _():
        ys_ref[...] = jnp.zeros(ys_ref.shape, ys_ref.dtype)


def _expert_ffn(xs, block_expert, block_rows, layer, w_gate, w_up, w_down):
    n_rows, half = xs.shape
    _, n_exp, d, de = w_gate.shape
    nb = n_rows // MOE_BLOCK
    wmap = lambda i, be, nv: (layer, be[i], 0, 0)
    grid_spec = pltpu.PrefetchScalarGridSpec(
        num_scalar_prefetch=2,
        grid=(nb,),
        in_specs=[pl.BlockSpec((MOE_BLOCK, half), lambda i, be, nv: (i, 0)),
                  pl.BlockSpec((None, None, d, de), wmap),
                  pl.BlockSpec((None, None, d, de), wmap),
                  pl.BlockSpec((None, None, de, d), wmap)],
        out_specs=pl.BlockSpec((MOE_BLOCK, half), lambda i, be, nv: (i, 0)),
        scratch_shapes=[pltpu.VMEM((d, de), BF16), pltpu.VMEM((d, de), BF16), pltpu.VMEM((de, d), BF16)],
    )
    return pl.pallas_call(
        _expert_ffn_kernel,
        grid_spec=grid_spec,
        out_shape=jax.ShapeDtypeStruct((n_rows, half), jnp.int32),
        compiler_params=_cparams(("arbitrary",), VMEM_LIMIT),
        name="moe_expert_ffn",
    )(block_expert, block_rows, xs, w_gate, w_up, w_down)


def _combine_kernel(yt_ref, wt_ref, hp_ref, x_ref, mod_ref, sg_ref, su_ref, sd_ref, *rest, d):
    o_ref = rest[-1]
    h = _unpack_bf16_pairs(hp_ref[...]).astype(BF16)
    acc = _dot(_silu_mul(_dot(h, sg_ref[...]), _dot(h, su_ref[...])).astype(BF16), sd_ref[...])
    w = wt_ref[...]
    for k in range(TOP_K):
        acc = acc + w[:, k:k + 1] * _unpack_bf16_pairs(yt_ref[k])
    o_ref[...] = x_ref[...] + mod_ref[...][:, 2 * d:] * acc


def _combine(y_tok, w_tok, h_packed, x2, out_prev, mod, layer, sh_gate, sh_up, sh_down, seq, tile0):
    t_all, d = x2.shape
    ds = sh_gate.shape[1]
    tm = MOE_TILE
    ntiles = h_packed.shape[0] // tm
    tiles_per_seq = seq // tm
    row = lambda i: (i, 0)
    off = lambda i: (i + tile0, 0)
    const = lambda i: (0, 0)
    in_specs = [pl.BlockSpec((TOP_K, tm, d // 2), lambda i: (0, i, 0)),
                pl.BlockSpec((tm, TOP_K), row),
                pl.BlockSpec((tm, d // 2), row),
                pl.BlockSpec((tm, d), off),
                pl.BlockSpec((None, None, 1, 3 * d), lambda i: (layer, (i + tile0) // tiles_per_seq, 0, 0)),
                pl.BlockSpec((d, ds), const),
                pl.BlockSpec((d, ds), const),
                pl.BlockSpec((ds, d), const)]
    args = [y_tok, w_tok, h_packed, x2, mod, sh_gate.astype(BF16), sh_up.astype(BF16), sh_down.astype(BF16)]
    aliases = {}
    if out_prev is not None:
        in_specs.append(pl.BlockSpec(memory_space=pl.ANY))
        args.append(out_prev)
        aliases = {len(args) - 1: 0}
    return pl.pallas_call(
        functools.partial(_combine_kernel, d=d),
        grid=(ntiles,),
        in_specs=in_specs,
        out_specs=pl.BlockSpec((tm, d), off),
        out_shape=jax.ShapeDtypeStruct((t_all, d), F32),
        input_output_aliases=aliases,
        compiler_params=_cparams(("arbitrary",), VMEM_LIMIT),
        name="moe_combine",
    )(*args)


def _moe_group(x2, out_prev, tile0, h_packed, gates, ranks, counts, mod, layer, w_gate, w_up, w_down,
               sh_gate, sh_up, sh_down, seq):
    t, half = h_packed.shape
    d = 2 * half
    n_exp = w_gate.shape[1]
    cnt = counts[:, 0].astype(jnp.int32)
    padded = (cnt + MOE_BLOCK - 1) // MOE_BLOCK * MOE_BLOCK
    eidx = jnp.arange(n_exp, dtype=jnp.int32)
    seg_end = jnp.sum(jnp.where(eidx[None, :] <= eidx[:, None], padded[None, :], 0), axis=1)
    seg_start = seg_end - padded
    nb = t * TOP_K // MOE_BLOCK + n_exp
    first_row = jnp.arange(nb, dtype=jnp.int32) * MOE_BLOCK
    block_expert = jnp.sum((seg_end[None, :] <= first_row[:, None]).astype(jnp.int32), axis=1)
    block_expert = jnp.minimum(block_expert, n_exp - 1)
    last_row = jnp.sum(jnp.where(block_expert[:, None] == eidx[None, :], (seg_start + cnt)[None, :], 0), axis=1)
    block_rows = jnp.clip(last_row - first_row, 0, MOE_BLOCK).astype(jnp.int32)
    pos, w = _assign_slots(ranks, gates, seg_start.astype(F32).reshape(n_exp, 1))
    xs = _sc_scatter_rows(h_packed, pos, nb * MOE_BLOCK)
    ys = _expert_ffn(xs, block_expert, block_rows, layer, w_gate, w_up, w_down)
    y_tok = _sc_gather_rows(ys, pos.reshape(TOP_K * t)).reshape(TOP_K, t, d // 2)
    return _combine(y_tok, w.T, h_packed, x2, out_prev, mod, layer, sh_gate, sh_up, sh_down, seq, tile0)


MOE_GROUPS = 1


def _moe_layer(x2, g, mod, layer, router_w, router_bias, w_gate, w_up, w_down, sh_gate, sh_up, sh_down, seq):
    ntiles = x2.shape[0] // MOE_TILE // MOE_GROUPS
    routed = [_ffn_router(x2, g, mod, layer, router_w, router_bias, seq, grp * ntiles, ntiles)
              for grp in range(MOE_GROUPS)]
    out = None
    for grp in range(MOE_GROUPS):
        out = _moe_group(x2, out, grp * ntiles, *routed[grp], mod, layer, w_gate, w_up, w_down,
                         sh_gate, sh_up, sh_down, seq)
    return out


def kernel(x, c, positions, mix_norm_g, mix_mod_w, mix_mod_b, ffn_norm_g, ffn_mod_w, ffn_mod_b,
           hyb_w_in, hyb_w_out, s5_lambda_re, s5_lambda_im, s5_log_dt, s5_b_re, s5_b_im,
           s5_c_re, s5_c_im, s5_d, s5_glu_w, s5_glu_b, attn_q_norm_g, attn_k_norm_g,
           pool_w, pool_scale, router_w, router_bias, exp_w_gate, exp_w_up, exp_w_down,
           sh_w_gate, sh_w_up, sh_w_down):
    bsz, seq, d = x.shape
    t = bsz * seq
    depth = mix_norm_g.shape[0]
    x2 = x.reshape(t, d)
    mix_mod = _mod_vectors(c, mix_mod_w, mix_mod_b).reshape(depth, bsz, 1, 3 * d)
    ffn_mod = _mod_vectors(c, ffn_mod_w, ffn_mod_b).reshape(depth, bsz, 1, 3 * d)
    for i in range(depth):
        j = i // 2
        if i % 2 == 0:
            u, q_hm, qi_hm, k, v, ki, wi = _in_projection(
                x2, mix_norm_g[i], mix_mod, i, hyb_w_in[j], positions, attn_q_norm_g[j], attn_k_norm_g[j], bsz, seq)
            weights = _s5_weights(s5_lambda_re[j], s5_lambda_im[j], s5_log_dt[j], s5_b_re[j], s5_b_im[j],
                                  s5_c_re[j], s5_c_im[j], s5_d[j], s5_glu_w[j], s5_glu_b[j])
            y_ssm = _s5_mixer(u, weights, bsz, seq)
            y_att = _dsa_attention(q_hm, qi_hm, wi, k, v, ki, bsz, seq)
            x2 = _out_projection(x2, y_ssm, y_att, hyb_w_out[j], mix_mod, i, seq)
        else:
            x2 = _pool_layer(x2, mix_norm_g[i], mix_mod, i, pool_w[j], pool_scale[j], seq)
        x2 = _moe_layer(x2, ffn_norm_g[i], ffn_mod, i, router_w[i], router_bias[i], exp_w_gate, exp_w_up, exp_w_down,
                        sh_w_gate[i], sh_w_up[i], sh_w_down[i], seq)
    return x2.reshape(bsz, seq, d)
```

```python
import functools

import numpy as np
import jax
import jax.numpy as jnp
from jax import lax
from jax.experimental import pallas as pl
from jax.experimental.pallas import tpu as pltpu
from jax.experimental.pallas import tpu_sc as plsc

F32 = jnp.float32
BF16 = jnp.bfloat16

EPS = 1e-6
S5_GROUP = 16
S5_STATE = 64
HEAD_DIM = 64
ATTN_HEADS = 8
IDX_HEADS = 8
ROPE_HALF = 8
ROPE_THETA = 500000.0
TOPK_MAX = 256
POOL_WINDOWS = (2, 4, 8, 16)
POOL_HALO = 16
N_EXPERT_GROUPS = 8
TOPK_GROUPS = 4
TOP_K = 8
ROUTED_SCALE = 2.5
S5_CHUNK = 16
NEG_BIG = -1e30
VMEM_LIMIT = 56 * 1024 * 1024


def _cparams(sem, vmem=None):
    return pltpu.CompilerParams(dimension_semantics=sem, vmem_limit_bytes=vmem)


def _dot(a, b):
    return jnp.dot(a, b, preferred_element_type=F32)


def _dot_nt(a, b):
    return lax.dot_general(a, b, (((1,), (1,)), ((), ())), preferred_element_type=F32)


def _split_bf16(a):
    hi = a.astype(BF16)
    lo = (a - hi.astype(F32)).astype(BF16)
    return hi, lo


def _norm_mod(x, g, shift, scale):
    y = x * lax.rsqrt(jnp.mean(x * x, axis=-1, keepdims=True) + EPS)
    return (y * g) * (1.0 + scale) + shift


def _mod_kernel(ct_ref, w_ref, b_ref, o_ref):
    ct = ct_ref[...]
    cs = ct * jax.nn.sigmoid(ct)
    w = w_ref[...]
    rows = [jnp.sum(w * cs[:, b:b + 1], axis=0, keepdims=True) for b in range(ct.shape[1])]
    o_ref[...] = jnp.concatenate(rows, axis=0) + b_ref[...]


def _mod_vectors(c, w, b):
    nl, d, n3 = w.shape
    bsz = c.shape[0]
    tn = 512
    return pl.pallas_call(
        _mod_kernel,
        grid=(nl, n3 // tn),
        in_specs=[pl.BlockSpec((d, bsz), lambda l, j: (0, 0)),
                  pl.BlockSpec((None, d, tn), lambda l, j: (l, 0, j)),
                  pl.BlockSpec((None, 1, tn), lambda l, j: (l, 0, j))],
        out_specs=pl.BlockSpec((None, bsz, tn), lambda l, j: (l, 0, j)),
        out_shape=jax.ShapeDtypeStruct((nl, bsz, n3), F32),
        compiler_params=_cparams(("arbitrary", "arbitrary")),
        name="mod_vectors",
    )(c.T, w, b.reshape(nl, 1, n3))


def _inproj_kernel(x_ref, g_ref, mod_ref, w_ref, pos_ref, inv_ref, qg_ref, kg_ref,
                   u_ref, q_ref, qi_ref, k_ref, v_ref, ki_ref, wi_ref, u_scr, *, d):
    x = x_ref[...]
    mod = mod_ref[...]
    h = _norm_mod(x, g_ref[...], mod[:, :d], mod[:, d:2 * d])
    proj = _dot(h.astype(BF16), w_ref[...])
    tm = x.shape[0]
    width = ATTN_HEADS * HEAD_DIM

    ang = inv_ref[...] * pos_ref[...].astype(F32)
    cos, sin = jnp.cos(ang), jnp.sin(ang)

    def head_t(xt, gain, scale):
        if scale is not None:
            r = lax.rsqrt(jnp.mean(xt * xt, axis=0, keepdims=True) + EPS) * scale
            xt = xt * gain
        x1, x2 = xt[:ROPE_HALF], xt[ROPE_HALF:2 * ROPE_HALF]
        out = jnp.concatenate([x1 * cos - x2 * sin, x1 * sin + x2 * cos, xt[2 * ROPE_HALF:]], axis=0)
        return out if scale is None else out * r

    for slab in range(width // 128):
        u_scr[slab] = proj[:, slab * 128:(slab + 1) * 128]
        for j in range(S5_CHUNK):
            piece = u_scr[slab, pl.ds(j, tm // S5_CHUNK, stride=S5_CHUNK), :]
            col = (slab * S5_CHUNK + j) * 128
            u_ref[:, col:col + 128] = piece.astype(u_ref.dtype)
    q = proj[:, width:2 * width]
    qi = proj[:, 2 * width:3 * width]
    small = proj[:, 3 * width:3 * width + 256]
    qg = qg_ref[...]
    for pair in range(ATTN_HEADS // 2):
        sl = slice(pair * 128, (pair + 1) * 128)
        qt = q[:, sl].T
        qit = qi[:, sl].T
        for half in range(2):
            hs = slice(half * HEAD_DIM, (half + 1) * HEAD_DIM)
            q_ref[2 * pair + half] = head_t(qt[hs], qg, HEAD_DIM ** -0.5).astype(q_ref.dtype)
            qi_ref[2 * pair + half] = head_t(qit[hs], None, None).astype(qi_ref.dtype)
    kvt = small[:, :128].T
    kt = head_t(kvt[:HEAD_DIM], kg_ref[...], 1.0)
    kv = jnp.concatenate([kt, kvt[HEAD_DIM:]], axis=0).T
    k_ref[...] = kv[:, :HEAD_DIM].astype(k_ref.dtype)
    v_ref[...] = kv[:, HEAD_DIM:].astype(v_ref.dtype)
    kiwt = small[:, 128:256].T
    kit = jnp.concatenate([head_t(kiwt[:HEAD_DIM], None, None), kiwt[HEAD_DIM:]], axis=0).T
    ki_ref[...] = kit[:, :HEAD_DIM].astype(ki_ref.dtype)
    wscale = (IDX_HEADS ** -0.5) * (HEAD_DIM ** -0.5)
    wi_ref[...] = kiwt[HEAD_DIM:HEAD_DIM + IDX_HEADS] * wscale


def _in_projection(x2, g, mod, layer, w_in, positions, q_norm_g, k_norm_g, bsz, seq):
    t, d = x2.shape
    tm = 512
    width = ATTN_HEADS * HEAD_DIM
    cuts = np.cumsum([width, width, HEAD_DIM, HEAD_DIM, width, HEAD_DIM, IDX_HEADS])
    u_w, q_w, k_w, v_w, qi_w, ki_w, wi_w = jnp.split(w_in, cuts[:-1].tolist(), axis=1)
    pad = jnp.zeros((d, 128 - HEAD_DIM - IDX_HEADS), w_in.dtype)
    w = jnp.concatenate([u_w, q_w, qi_w, k_w, v_w, ki_w, wi_w, pad], axis=1).astype(BF16)
    nw = w.shape[1]
    inv = np.power(ROPE_THETA, -2.0 * np.arange(ROPE_HALF) / (2 * ROPE_HALF))
    inv = jnp.asarray(inv.reshape(ROPE_HALF, 1), F32)
    tiles_per_seq = seq // tm
    row = lambda i: (i, 0)
    const = lambda i: (0, 0)
    hm = lambda i: (i // tiles_per_seq, 0, 0, i % tiles_per_seq)
    outs = pl.pallas_call(
        functools.partial(_inproj_kernel, d=d),
        grid=(t // tm,),
        in_specs=[pl.BlockSpec((tm, d), row),
                  pl.BlockSpec((1, d), const),
                  pl.BlockSpec((None, None, 1, 3 * d), lambda i: (layer, i // tiles_per_seq, 0, 0)),
                  pl.BlockSpec((d, nw), const),
                  pl.BlockSpec((None, 1, tm), lambda i: (i, 0, 0)),
                  pl.BlockSpec((ROPE_HALF, 1), const),
                  pl.BlockSpec((HEAD_DIM, 1), const),
                  pl.BlockSpec((HEAD_DIM, 1), const)],
        out_specs=[pl.BlockSpec((tm // S5_CHUNK, width * S5_CHUNK), row),
                   pl.BlockSpec((None, ATTN_HEADS, HEAD_DIM, tm), hm),
                   pl.BlockSpec((None, IDX_HEADS, HEAD_DIM, tm), hm),
                   pl.BlockSpec((tm, HEAD_DIM), row),
                   pl.BlockSpec((tm, HEAD_DIM), row),
                   pl.BlockSpec((tm, HEAD_DIM), row),
                   pl.BlockSpec((IDX_HEADS, tm), lambda i: (0, i))],
        out_shape=[jax.ShapeDtypeStruct((t // S5_CHUNK, width * S5_CHUNK), BF16),
                   jax.ShapeDtypeStruct((bsz, ATTN_HEADS, HEAD_DIM, seq), BF16),
                   jax.ShapeDtypeStruct((bsz, IDX_HEADS, HEAD_DIM, seq), BF16),
                   jax.ShapeDtypeStruct((t, HEAD_DIM), BF16),
                   jax.ShapeDtypeStruct((t, HEAD_DIM), BF16),
                   jax.ShapeDtypeStruct((t, HEAD_DIM), BF16),
                   jax.ShapeDtypeStruct((IDX_HEADS, t), F32)],
        scratch_shapes=[pltpu.VMEM((width // 128, tm, 128), F32)],
        compiler_params=_cparams(("arbitrary",), VMEM_LIMIT),
        name="in_projection",
    )(x2, g.reshape(1, d), mod, w, positions.reshape(t // tm, 1, tm), inv,
      q_norm_g.reshape(HEAD_DIM, 1), k_norm_g.reshape(HEAD_DIM, 1))
    return outs


def _s5_weights(lam_re, lam_im, log_dt, b_re, b_im, c_re, c_im, d_skip, glu_w, glu_b):
    L = S5_CHUNK
    g, p = lam_re.shape
    hch = S5_GROUP
    dt = jnp.exp(log_dt)[:, None]
    lr, li = lam_re, lam_im
    tau = jnp.arange(L + 1, dtype=F32)[:, None, None]
    mag = jnp.exp(lr * dt * tau)
    pw_r, pw_i = mag * jnp.cos(li * dt * tau), mag * jnp.sin(li * dt * tau)
    ar, ai = pw_r[1], pw_i[1]
    nr, ni = ar - 1.0, ai
    den = lr * lr + li * li
    cr, ci = (nr * lr + ni * li) / den, (ni * lr - nr * li) / den
    bb_r = cr[..., None] * b_re - ci[..., None] * b_im
    bb_i = cr[..., None] * b_im + ci[..., None] * b_re
    cl_r = c_re[None] * pw_r[:L, :, None, :] - c_im[None] * pw_i[:L, :, None, :]
    cl_i = c_re[None] * pw_i[:L, :, None, :] + c_im[None] * pw_r[:L, :, None, :]
    taps = (jnp.einsum('tghp,gpk->tghk', cl_r, bb_r) - jnp.einsum('tghp,gpk->tghk', cl_i, bb_i))
    ns = g // 8
    eye = jnp.eye(8, dtype=F32)
    kd = jnp.einsum('tsgoh,gf->tsghfo', taps.reshape(L, ns, 8, hch, hch), eye).reshape(L, ns, 128, 128)
    zero = jnp.zeros_like(kd[0])
    k2 = jnp.stack([jnp.concatenate([jnp.concatenate([kd[2 * dd], kd[2 * dd + 1]], axis=-1),
                                     jnp.concatenate([kd[2 * dd - 1] if dd else zero, kd[2 * dd]], axis=-1)], axis=-2)
                    for dd in range(L // 2)], axis=1).astype(BF16)
    ii = jnp.arange(L)
    rev_r, rev_i = pw_r[L - 1 - ii], pw_i[L - 1 - ii]
    z_r = rev_r[..., None] * bb_r[None] - rev_i[..., None] * bb_i[None]
    z_i = rev_r[..., None] * bb_i[None] + rev_i[..., None] * bb_r[None]
    zc = jnp.concatenate([z_r, z_i], axis=2)
    zc = jnp.transpose(zc.reshape(L, ns, 8, 2 * p, hch), (1, 0, 2, 4, 3)).reshape(ns, L * 128, 1, 2 * p)
    row_group = (jnp.arange(L * 128) // hch) % 8
    w_z = jnp.where((row_group[:, None] == jnp.arange(8)[None, :])[None, :, :, None], zc, 0.0)
    w_z = w_z.reshape(ns, L // 2, 2 * 128, 8 * 2 * p).astype(BF16)
    co_r = c_re[None] * pw_r[1:, :, None, :] - c_im[None] * pw_i[1:, :, None, :]
    co_i = c_re[None] * pw_i[1:, :, None, :] + c_im[None] * pw_r[1:, :, None, :]
    cc = jnp.concatenate([co_r, -co_i], axis=-1)
    cc = jnp.transpose(cc.reshape(L, ns, 8, hch, 2 * p), (1, 4, 0, 2, 3)).reshape(ns, 1, 2 * p, L * 128)
    col_group = (jnp.arange(L * 128) // hch) % 8
    w_c = jnp.where((jnp.arange(8)[:, None] == col_group[None, :])[None, :, None, :], cc, 0.0)
    w_c = w_c.reshape(ns, 8 * 2 * p, L * 128)
    w_ch = w_c.astype(BF16)
    w_cl = (w_c - w_ch.astype(F32)).astype(BF16)
    al_r, al_i = pw_r[L], pw_i[L]
    dec_a = jnp.concatenate([al_r, al_r], axis=-1).reshape(ns, 1, 8 * 2 * p)
    dec_b = jnp.concatenate([-al_i, al_i], axis=-1).reshape(ns, 1, 8 * 2 * p)
    d_t = jnp.tile(d_skip.reshape(ns, 1, 128), (1, 1, L))
    gl = jnp.einsum('sghk,gf->sghfk', glu_w.reshape(ns, 8, hch, hch), eye).reshape(ns, 128, 128)
    glu2 = jnp.einsum('ab,shk->sahbk', jnp.eye(2, dtype=F32), gl).reshape(ns, 256, 256).astype(BF16)
    glu_bt = jnp.tile(glu_b.reshape(ns, 1, 128), (1, 1, L))
    return k2, w_z, w_ch, w_cl, dec_a, dec_b, d_t, glu2, glu_bt


def _s5_kernel(u_ref, k2_ref, wz_ref, wch_ref, wcl_ref, da_ref, db_ref, d_ref, g2_ref, gb_ref, o_ref,
               zp_ref, zq_ref, sp_ref, *, nchunk):
    nblk = S5_CHUNK // 2
    ub = [u_ref[:, j * 256:(j + 1) * 256] for j in range(nblk)]
    z = _dot(ub[0], wz_ref[0])
    for j in range(1, nblk):
        z = z + _dot(ub[j], wz_ref[j])
    zp_ref[...] = z
    half = S5_STATE
    zq_ref[...] = jnp.concatenate([pltpu.roll(z[:, s * 2 * half:(s + 1) * 2 * half], half, 1)
                                   for s in range(z.shape[1] // (2 * half))], axis=1)
    da = da_ref[...]
    db = db_ref[...]
    dbq = -db

    def step(c, carry):
        sp, sq = carry
        sp_ref[pl.ds(c, 1), :] = sp
        sp_new = da * sp + db * sq + zp_ref[pl.ds(c, 1), :]
        sq_new = da * sq + dbq * sp + zq_ref[pl.ds(c, 1), :]
        return sp_new, sq_new

    zero = jnp.zeros((1, z.shape[1]), F32)
    lax.fori_loop(0, nchunk, step, (zero, zero))
    sh, sl = _split_bf16(sp_ref[...])
    wch = wch_ref[...]
    y_inter = _dot(sh, wch) + (_dot(sh, wcl_ref[...]) + _dot(sl, wch))
    for i in range(nblk):
        cs = slice(i * 256, (i + 1) * 256)
        acc = y_inter[:, cs] + d_ref[:, cs] * ub[i].astype(F32)
        for j in range(i + 1):
            acc = acc + _dot(ub[j], k2_ref[i - j])
        y = jax.nn.gelu(acc)
        y = y * jax.nn.sigmoid(_dot(y.astype(BF16), g2_ref[...]) + gb_ref[:, cs])
        for a in range(2):
            o_ref[pl.ds(2 * i + a, nchunk, stride=S5_CHUNK), :] = y[:, a * 128:(a + 1) * 128]


def _s5_mixer(u2, weights, bsz, seq):
    k2, w_z, w_ch, w_cl, dec_a, dec_b, d_t, glu2, glu_bt = weights
    ns = k2.shape[0]
    nchunk = seq // S5_CHUNK
    cols = S5_CHUNK * 128
    st = w_z.shape[-1]
    slab3 = lambda b, s: (s, 0, 0)
    slab4 = lambda b, s: (s, 0, 0, 0)
    return pl.pallas_call(
        functools.partial(_s5_kernel, nchunk=nchunk),
        grid=(bsz, ns),
        in_specs=[pl.BlockSpec((nchunk, cols), lambda b, s: (b, s)),
                  pl.BlockSpec((None,) + k2.shape[1:], slab4),
                  pl.BlockSpec((None,) + w_z.shape[1:], slab4),
                  pl.BlockSpec((None, st, cols), slab3),
                  pl.BlockSpec((None, st, cols), slab3),
                  pl.BlockSpec((None, 1, st), slab3),
                  pl.BlockSpec((None, 1, st), slab3),
                  pl.BlockSpec((None, 1, cols), slab3),
                  pl.BlockSpec((None, 256, 256), slab3),
                  pl.BlockSpec((None, 1, cols), slab3)],
        out_specs=pl.BlockSpec((seq, 128), lambda b, s: (b, s)),
        out_shape=jax.ShapeDtypeStruct((bsz * seq, ns * 128), F32),
        scratch_shapes=[pltpu.VMEM((nchunk, st), F32), pltpu.VMEM((nchunk, st), F32), pltpu.VMEM((nchunk, st), F32)],
        compiler_params=_cparams(("arbitrary", "arbitrary"), VMEM_LIMIT),
        name="s5_mixer",
    )(u2, k2, w_z, w_ch, w_cl, dec_a, dec_b, d_t, glu2, glu_bt)


def _dsa_kernel(qt_ref, qit_ref, wt_ref, k_ref, vt_ref, ki_ref, o_ref,
                sc_ref, s_ref, m_ref, l_ref, acc_ref, *, tq, ck, k_sel, idx_bits, max_iter):
    i = pl.program_id(1)
    nheads = qt_ref.shape[0]
    n_ck = (i * tq + tq + ck - 1) // ck
    t_pos = i * tq + lax.broadcasted_iota(jnp.int32, (1, tq), 1)
    krow = lax.broadcasted_iota(jnp.int32, (ck, 1), 0)
    kf = float(k_sel)
    inf = jnp.inf

    fr = 32
    sub = min(32768 // tq, ck)

    def fold(x, op):
        return op(x.reshape(ck // fr, fr, tq), axis=0)

    wt = wt_ref[...]

    def score_chunk(c, carry):
        lo8, hi8 = carry
        for part in range(ck // sub):
            off = pl.multiple_of(c * ck + part * sub, sub)
            kic = ki_ref[pl.ds(off, sub), :]
            acc = jnp.zeros((sub, tq), F32)
            for h in range(nheads):
                acc = acc + wt[h:h + 1, :] * jnp.maximum(_dot(kic, qit_ref[h]), 0.0)
            acc = acc + 0.0
            vis = off + krow[:sub] <= t_pos
            sc_ref[c, part * sub:(part + 1) * sub, :] = jnp.where(vis, acc, -inf)
            hi8 = jnp.maximum(hi8, jnp.max(jnp.where(vis, acc, -inf).reshape(sub // fr, fr, tq), axis=0))
            lo8 = jnp.minimum(lo8, jnp.min(jnp.where(vis, acc, inf).reshape(sub // fr, fr, tq), axis=0))
        return lo8, hi8

    lo8, hi8 = lax.fori_loop(0, n_ck, score_chunk, (jnp.full((fr, tq), inf, F32), jnp.full((fr, tq), -inf, F32)))
    lo = jnp.min(lo8, axis=0, keepdims=True)
    hi = jnp.max(hi8, axis=0, keepdims=True)
    n_vis = (t_pos + 1).astype(F32)
    few = n_vis < kf

    def reduce_keys(fn, init, combine, op):
        def body(c, part):
            return combine(part, fold(fn(sc_ref[c], c * ck), op))
        return op(lax.fori_loop(0, n_ck, body, jnp.full((fr, tq), init, F32)), axis=0, keepdims=True)

    def count(pred):
        return reduce_keys(lambda x, off: jnp.where(pred(x, off), 1.0, 0.0), 0.0, jnp.add, jnp.sum)

    def bisect(lo_, hi_, c_lo):
        mid = lo_ + (hi_ - lo_) * 0.5
        cnt = count(lambda x, off: x >= mid)
        up = cnt >= kf
        return jnp.where(up, mid, lo_), jnp.where(up, hi_, mid), jnp.where(up, cnt, c_lo)

    def status(lo_, hi_, c_lo):
        mid = lo_ + (hi_ - lo_) * 0.5
        unresolved = jnp.where(few, 0.0, jnp.where(c_lo == kf, 0.0, 1.0))
        movable = jnp.where(mid > lo_, jnp.where(mid < hi_, 1.0, 0.0), 0.0)
        return jnp.max(unresolved), jnp.max(unresolved * movable)

    def search_body(st):
        it, lo_, hi_, c_lo, _, _ = st
        for _ in range(2):
            lo_, hi_, c_lo = bisect(lo_, hi_, c_lo)
        return (it + 2, lo_, hi_, c_lo) + status(lo_, hi_, c_lo)

    st = lax.while_loop(lambda st: jnp.logical_and(st[0] < max_iter, st[5] > 0.5), search_body,
                        (jnp.int32(0), lo, hi, n_vis) + status(lo, hi, n_vis))
    lo, open_ = st[1], st[4]

    def resolve():
        def walk(thr_, strict):
            keep = (lambda x: x > thr_) if strict else (lambda x: x >= thr_)
            return reduce_keys(lambda x, off: jnp.where(keep(x), x, inf), inf, jnp.minimum, jnp.min)

        def above_equal(thr_):
            return count(lambda x, off: x > thr_), count(lambda x, off: x == thr_)

        def too_low(n_gt):
            return jnp.where(few, 0.0, jnp.where(n_gt >= kf, 1.0, 0.0))

        thr0 = walk(lo, False)

        def walk_body(st):
            thr_, n_gt, _ = st
            thr2 = jnp.where(too_low(n_gt) > 0.5, walk(thr_, True), thr_)
            return (thr2,) + above_equal(thr2)

        thr_, n_gt, n_eq = lax.while_loop(lambda st: jnp.max(too_low(st[1])) > 0.5, walk_body,
                                          (thr0,) + above_equal(thr0))
        need = kf - n_gt

        def tie_cut():
            def mark_equal(c, _):
                s_ref[0, c] = jnp.where(sc_ref[c] == thr_, 1.0, 0.0)
                return 0

            lax.fori_loop(0, n_ck, mark_equal, 0)

            def idx_step(b, cur):
                trial = cur | (jnp.int32(1) << (idx_bits - 1 - b))

                def body(c, part):
                    return part + fold(jnp.where(c * ck + krow < trial, s_ref[0, c], 0.0), jnp.sum)
                cnt = jnp.sum(lax.fori_loop(0, n_ck, body, jnp.zeros((fr, tq), F32)), axis=0, keepdims=True)
                return jnp.where(cnt < need, trial, cur)

            return lax.fori_loop(0, idx_bits, idx_step, jnp.zeros((1, tq), jnp.int32))

        tied = jnp.max(jnp.where(few, 0.0, jnp.where(n_eq > need, 1.0, 0.0))) > 0.5
        return thr_, lax.cond(tied, tie_cut, lambda: jnp.full((1, tq), 0x7FFFFFFF, jnp.int32))

    thr, cut = lax.cond(open_ > 0.5, resolve, lambda: (lo, jnp.full((1, tq), 0x7FFFFFFF, jnp.int32)))

    for h in range(nheads):
        m_ref[h] = jnp.full((fr, tq), NEG_BIG, F32)
        l_ref[h] = jnp.zeros((fr, tq), F32)
        acc_ref[h] = jnp.zeros((HEAD_DIM, tq), F32)

    def attn_scores(c, _):
        off = pl.multiple_of(c * ck, ck)
        kc = k_ref[pl.ds(off, ck), :]
        x = sc_ref[c]
        at_thr = jnp.where(x == thr, jnp.where(off + krow <= cut, 0.0, NEG_BIG), NEG_BIG)
        bias = jnp.where(x > thr, 0.0, at_thr)
        for h in range(nheads):
            s = _dot(kc, qt_ref[h]) + bias
            s_ref[h, c] = s
            m_ref[h] = jnp.maximum(m_ref[h], fold(s, jnp.max))
        return 0

    lax.fori_loop(0, n_ck, attn_scores, 0)
    for h in range(nheads):
        m_ref[h] = jnp.broadcast_to(jnp.max(m_ref[h], axis=0, keepdims=True), (fr, tq))

    def attn_values(c, _):
        vt = vt_ref[c]
        for h in range(nheads):
            p = jnp.exp(s_ref[h, c] - m_ref[h][:1, :])
            l_ref[h] += fold(p, jnp.sum)
            acc_ref[h] += _dot(vt, p.astype(BF16))
        return 0

    lax.fori_loop(0, n_ck, attn_values, 0)
    for h in range(nheads):
        l_row = jnp.sum(l_ref[h], axis=0, keepdims=True)
        o_ref[h * HEAD_DIM:(h + 1) * HEAD_DIM, :] = (acc_ref[h] / l_row).astype(o_ref.dtype)


def _dsa_attention(qt, qit, wt, k, v, ki, bsz, seq):
    tq = 256
    ck = min(512, seq)
    n_ck = seq // ck
    k_sel = min(TOPK_MAX, seq // 4)
    nq = seq // tq
    nh = ATTN_HEADS
    width = nh * HEAD_DIM
    vt = v.reshape(bsz, n_ck, ck, HEAD_DIM).transpose(0, 1, 3, 2)
    qspec = pl.BlockSpec((None, nh, HEAD_DIM, tq), lambda b, i: (b, 0, 0, i))
    out_t = pl.pallas_call(
        functools.partial(_dsa_kernel, tq=tq, ck=ck, k_sel=k_sel, idx_bits=(seq - 1).bit_length(), max_iter=22),
        grid=(bsz, nq),
        in_specs=[qspec, qspec,
                  pl.BlockSpec((IDX_HEADS, tq), lambda b, i: (0, b * nq + i)),
                  pl.BlockSpec((seq, HEAD_DIM), lambda b, i: (b, 0)),
                  pl.BlockSpec((None, n_ck, HEAD_DIM, ck), lambda b, i: (b, 0, 0, 0)),
                  pl.BlockSpec((seq, HEAD_DIM), lambda b, i: (b, 0))],
        out_specs=pl.BlockSpec((None, width, tq), lambda b, i: (b, 0, i)),
        out_shape=jax.ShapeDtypeStruct((bsz, width, seq), BF16),
        scratch_shapes=[pltpu.VMEM((n_ck, ck, tq), F32),
                        pltpu.VMEM((nh, n_ck, ck, tq), F32),
                        pltpu.VMEM((nh, 32, tq), F32),
                        pltpu.VMEM((nh, 32, tq), F32),
                        pltpu.VMEM((nh, HEAD_DIM, tq), F32)],
        compiler_params=_cparams(("arbitrary", "arbitrary"), VMEM_LIMIT),
        name="dsa_attention",
    )(qt, qit, wt, k, vt, ki)
    return out_t.transpose(0, 2, 1).reshape(bsz * seq, width)


def _outproj_kernel(x_ref, ys_ref, ya_ref, w_ref, mod_ref, o_ref, *, d):
    width = ys_ref.shape[1]
    y = _dot(ys_ref[...].astype(BF16), w_ref[:width, :]) + _dot(ya_ref[...], w_ref[width:, :])
    o_ref[...] = x_ref[...] + mod_ref[...][:, 2 * d:] * y


def _out_projection(x2, y_ssm, y_att, w_out, mod, layer, seq):
    t, d = x2.shape
    tm = 512
    width = y_ssm.shape[1]
    tiles_per_seq = seq // tm
    row = lambda i: (i, 0)
    return pl.pallas_call(
        functools.partial(_outproj_kernel, d=d),
        grid=(t // tm,),
        in_specs=[pl.BlockSpec((tm, d), row),
                  pl.BlockSpec((tm, width), row),
                  pl.BlockSpec((tm, width), row),
                  pl.BlockSpec((2 * width, d), lambda i: (0, 0)),
                  pl.BlockSpec((None, None, 1, 3 * d), lambda i: (layer, i // tiles_per_seq, 0, 0))],
        out_specs=pl.BlockSpec((tm, d), row),
        out_shape=jax.ShapeDtypeStruct((t, d), F32),
        compiler_params=_cparams(("arbitrary",)),
        name="out_projection",
    )(x2, y_ssm, y_att, w_out.astype(BF16), mod)


def _pool_kernel(x_ref, halo_ref, g_ref, mod_ref, pw_ref, ps_ref, o_ref, h_ref, *, d, tiles_per_seq):
    i = pl.program_id(0)
    tm = x_ref.shape[0]
    mod = mod_ref[...]
    shift, scale, gate = mod[:, :d], mod[:, d:2 * d], mod[:, 2 * d:]
    x = x_ref[...]
    first = (i % tiles_per_seq) == 0
    h_ref[POOL_HALO:, :] = _norm_mod(x, g_ref[...], shift, scale)
    halo = _norm_mod(halo_ref[...], g_ref[...], shift, scale)
    h_ref[:POOL_HALO, :] = jnp.where(first, 0.0, halo)
    pos = (i % tiles_per_seq) * tm + lax.broadcasted_iota(jnp.int32, (tm, 1), 0)
    gw = d // len(POOL_WINDOWS)
    ys = []
    for gi, win in enumerate(POOL_WINDOWS):
        cs = slice(gi * gw, (gi + 1) * gw)
        cur = h_ref[POOL_HALO:, cs]
        tot = cur
        for k in range(1, win):
            tot = tot + h_ref[POOL_HALO - k:POOL_HALO - k + tm, cs]
        cnt = jnp.minimum(pos + 1, win).astype(F32)
        pooled = tot / cnt - cur
        ys.append(_dot(pooled.astype(BF16), pw_ref[gi]))
    y = jnp.concatenate(ys, axis=1) * ps_ref[...]
    o_ref[...] = x + gate * y


def _pool_layer(x2, g, mod, layer, pool_w, pool_scale, seq):
    t, d = x2.shape
    tm = 512
    tiles_per_seq = seq // tm
    gw = d // len(POOL_WINDOWS)
    hb = tm // POOL_HALO
    return pl.pallas_call(
        functools.partial(_pool_kernel, d=d, tiles_per_seq=tiles_per_seq),
        grid=(t // tm,),
        in_specs=[pl.BlockSpec((tm, d), lambda i: (i, 0)),
                  pl.BlockSpec((POOL_HALO, d), lambda i: (jnp.maximum(i * hb - 1, 0), 0)),
                  pl.BlockSpec((1, d), lambda i: (0, 0)),
                  pl.BlockSpec((None, None, 1, 3 * d), lambda i: (layer, i // tiles_per_seq, 0, 0)),
                  pl.BlockSpec((len(POOL_WINDOWS), gw, gw), lambda i: (0, 0, 0)),
                  pl.BlockSpec((1, d), lambda i: (0, 0))],
        out_specs=pl.BlockSpec((tm, d), lambda i: (i, 0)),
        out_shape=jax.ShapeDtypeStruct((t, d), F32),
        scratch_shapes=[pltpu.VMEM((tm + POOL_HALO, d), F32)],
        compiler_params=_cparams(("arbitrary",)),
        name="pool_mixer",
    )(x2, x2, g.reshape(1, d), mod, pool_w.astype(BF16), pool_scale.reshape(1, d))


def _first_max(vals, idx, big):
    m = jnp.max(vals, axis=0, keepdims=True)
    first = jnp.min(jnp.where(vals == m, idx, big), axis=0, keepdims=True)
    return m, first


def _pack_bf16_pairs(x):
    n = x.shape[1] // 2
    lo = pltpu.bitcast(x[:, :n].astype(BF16).astype(F32), jnp.int32)
    hi = pltpu.bitcast(x[:, n:].astype(BF16).astype(F32), jnp.int32)
    return hi | ((lo >> 16) & 0xFFFF)


def _unpack_bf16_pairs(w):
    lo = pltpu.bitcast(w << 16, F32)
    hi = pltpu.bitcast(w & jnp.int32(-65536), F32)
    return jnp.concatenate([lo, hi], axis=1)


def _router_kernel(x_ref, g_ref, mod_ref, rw_ref, rb_ref, tri_ref, h_ref, gate_ref, rank_ref, cnt_ref,
                   run_ref, *, d, n_exp):
    mod = mod_ref[...]
    h = _norm_mod(x_ref[...], g_ref[...], mod[:, :d], mod[:, d:2 * d])
    h_ref[...] = _pack_bf16_pairs(h)
    tm = h.shape[0]
    hh, hl = _split_bf16(h)
    rw = rw_ref[...]
    rh, rl = _split_bf16(rw)
    logits = _dot_nt(rh, hh) + (_dot_nt(rh, hl) + _dot_nt(rl, hh))
    scores = jax.nn.sigmoid(logits)
    sel = scores + rb_ref[...]
    gsz = n_exp // N_EXPERT_GROUPS
    neg = -jnp.inf
    shape3 = (N_EXPERT_GROUPS, gsz, tm)
    sel3 = sel.reshape(shape3)
    sub = lax.broadcasted_iota(jnp.int32, shape3, 1)
    m1 = jnp.max(sel3, axis=1, keepdims=True)
    f1 = jnp.min(jnp.where(sel3 == m1, sub, gsz), axis=1, keepdims=True)
    m2 = jnp.max(jnp.where(sub == f1, neg, sel3), axis=1, keepdims=True)
    gscore = jnp.broadcast_to(m1 + m2, shape3).reshape(n_exp, tm)
    eidx = lax.broadcasted_iota(jnp.int32, (n_exp, tm), 0)
    gidx = eidx // gsz
    keep = jnp.zeros((n_exp, tm), F32)
    work = gscore
    for _ in range(TOPK_GROUPS):
        _, first = _first_max(work, gidx, N_EXPERT_GROUPS)
        hit = gidx == first
        keep = jnp.where(hit, 1.0, keep)
        work = jnp.where(hit, neg, work)
    work = jnp.where(keep > 0.0, sel, neg)
    chosen = jnp.zeros((n_exp, tm), F32)
    for _ in range(TOP_K):
        _, first = _first_max(work, eidx, n_exp)
        hit = eidx == first
        chosen = jnp.where(hit, 1.0, chosen)
        work = jnp.where(hit, neg, work)
    picked = chosen * scores
    gate_ref[...] = picked / jnp.sum(picked, axis=0, keepdims=True) * ROUTED_SCALE

    @pl.when(pl.program_id(0) == 0)
    def _():
        run_ref[...] = jnp.zeros(run_ref.shape, F32)

    before = _dot(chosen.astype(BF16), tri_ref[...])
    run = run_ref[...]
    rank_ref[...] = jnp.where(chosen > 0.0, before + run[:, :1], -1.0)
    run = run + jnp.broadcast_to(jnp.sum(chosen, axis=1, keepdims=True), run.shape)
    run_ref[...] = run
    cnt_ref[...] = run


MOE_TILE = 512
ROUTER_TILE = 1024


def _ffn_router(x2, g, mod, layer, router_w, router_bias, seq, tile0, ntiles):
    d = x2.shape[1]
    n_exp = router_w.shape[1]
    tm = ROUTER_TILE
    t = ntiles * tm
    tiles_per_seq = seq // tm
    tri = (np.arange(tm)[:, None] < np.arange(tm)[None, :]).astype(np.float32)
    return pl.pallas_call(
        functools.partial(_router_kernel, d=d, n_exp=n_exp),
        grid=(ntiles,),
        in_specs=[pl.BlockSpec((tm, d), lambda i: (i + tile0, 0)),
                  pl.BlockSpec((1, d), lambda i: (0, 0)),
                  pl.BlockSpec((None, None, 1, 3 * d), lambda i: (layer, (i + tile0) // tiles_per_seq, 0, 0)),
                  pl.BlockSpec((n_exp, d), lambda i: (0, 0)),
                  pl.BlockSpec((n_exp, 1), lambda i: (0, 0)),
                  pl.BlockSpec((tm, tm), lambda i: (0, 0))],
        out_specs=[pl.BlockSpec((tm, d // 2), lambda i: (i, 0)),
                   pl.BlockSpec((n_exp, tm), lambda i: (0, i)),
                   pl.BlockSpec((n_exp, tm), lambda i: (0, i)),
                   pl.BlockSpec((n_exp, 128), lambda i: (0, 0))],
        out_shape=[jax.ShapeDtypeStruct((t, d // 2), jnp.int32),
                   jax.ShapeDtypeStruct((n_exp, t), F32),
                   jax.ShapeDtypeStruct((n_exp, t), F32),
                   jax.ShapeDtypeStruct((n_exp, 128), F32)],
        scratch_shapes=[pltpu.VMEM((n_exp, 128), F32)],
        compiler_params=_cparams(("arbitrary",)),
        name="ffn_router",
    )(x2, g.reshape(1, d), mod, router_w.T, router_bias.reshape(n_exp, 1), jnp.asarray(tri, BF16))


def _assign_kernel(rank_ref, gate_ref, start_ref, pos_ref, w_ref, *, n_exp):
    rank = rank_ref[...]
    gates = gate_ref[...]
    tm = rank.shape[1]
    slot = rank + start_ref[...]
    eidx = lax.broadcasted_iota(jnp.int32, (n_exp, tm), 0).astype(F32)
    alive = jnp.where(rank >= 0.0, eidx, float(n_exp))
    kidx = lax.broadcasted_iota(jnp.int32, (TOP_K, tm), 0)
    pos = jnp.zeros((TOP_K, tm), F32)
    wts = jnp.zeros((TOP_K, tm), F32)
    for k in range(TOP_K):
        first = jnp.min(alive, axis=0, keepdims=True)
        hit = alive == first
        pos_k = jnp.sum(jnp.where(hit, slot, 0.0), axis=0, keepdims=True)
        w_k = jnp.sum(jnp.where(hit, gates, 0.0), axis=0, keepdims=True)
        pos = jnp.where(kidx == k, pos_k, pos)
        wts = jnp.where(kidx == k, w_k, wts)
        alive = jnp.where(hit, float(n_exp), alive)
    pos_ref[...] = pos.astype(jnp.int32)
    w_ref[...] = wts


def _assign_slots(ranks, gates, start):
    n_exp, t = ranks.shape
    tm = 512
    return pl.pallas_call(
        functools.partial(_assign_kernel, n_exp=n_exp),
        grid=(t // tm,),
        in_specs=[pl.BlockSpec((n_exp, tm), lambda i: (0, i)),
                  pl.BlockSpec((n_exp, tm), lambda i: (0, i)),
                  pl.BlockSpec((n_exp, 1), lambda i: (0, 0))],
        out_specs=[pl.BlockSpec((TOP_K, tm), lambda i: (0, i)),
                   pl.BlockSpec((TOP_K, tm), lambda i: (0, i))],
        out_shape=[jax.ShapeDtypeStruct((TOP_K, t), jnp.int32), jax.ShapeDtypeStruct((TOP_K, t), F32)],
        compiler_params=_cparams(("arbitrary",)),
        name="moe_assign",
    )(ranks, gates, start)


SC_CORES = 2
SC_SUBCORES = 16
SC_WINDOW = 128


def _sc_mesh():
    return plsc.VectorSubcoreMesh(core_axis_name="c", subcore_axis_name="s",
                                  num_cores=SC_CORES, num_subcores=SC_SUBCORES)


def _sc_scatter_rows(rows, pos, n_out):
    t, width = rows.shape
    nk = pos.shape[0]
    win = SC_WINDOW
    n_win = t // win // (SC_CORES * SC_SUBCORES)
    pos_w = pos.reshape(nk, t // win, win).transpose(1, 0, 2)

    def body(rows_hbm, pos_hbm, out_hbm, idx_v, buf_v, sem):
        wid = lax.axis_index("s") * SC_CORES + lax.axis_index("c")

        @pl.loop(0, n_win)
        def _(j):
            w = wid * n_win + j
            pltpu.sync_copy(rows_hbm.at[pl.ds(w * win, win)], buf_v)
            pltpu.sync_copy(pos_hbm.at[w], idx_v)
            copies = [pltpu.make_async_copy(buf_v, out_hbm.at[idx_v.at[k]], sem) for k in range(nk)]
            for cp in copies:
                cp.start()
            for cp in copies:
                cp.wait()

    return pl.kernel(
        body, out_type=jax.ShapeDtypeStruct((n_out, width), jnp.int32), mesh=_sc_mesh(),
        scratch_types=[pltpu.VMEM((nk, win), jnp.int32), pltpu.VMEM((win, width), jnp.int32),
                       pltpu.SemaphoreType.DMA],
        name="sc_scatter_rows",
    )(rows, pos_w)


SC_GATHER_WINDOW = 64


def _sc_gather_rows(table, pos_flat):
    m = pos_flat.shape[0]
    width = table.shape[1]
    win = SC_GATHER_WINDOW
    workers = SC_CORES * SC_SUBCORES
    n_win = m // win // workers
    pos_w = pos_flat.reshape(workers, n_win, win)

    def body(table_hbm, pos_hbm, out_hbm, idx_v, buf_v, sem_g, sem_w):
        wid = lax.axis_index("s") * SC_CORES + lax.axis_index("c")
        base = wid * (n_win * win)
        pltpu.sync_copy(pos_hbm.at[wid], idx_v)

        def gather(j, b):
            return pltpu.make_async_copy(table_hbm.at[idx_v.at[j]], buf_v.at[b], sem_g.at[b])

        def write(j, b):
            return pltpu.make_async_copy(buf_v.at[b], out_hbm.at[pl.ds(base + j * win, win)], sem_w.at[b])

        gather(0, 0).start()

        @pl.loop(0, n_win, step=2)
        def _(j0):
            for b in range(2):
                j = j0 + b
                gather(j, b).wait()
                write(j, b).start()

                @pl.when(j + 1 < n_win)
                def _():
                    @pl.when(j >= 1)
                    def _():
                        write(j - 1, 1 - b).wait()
                    gather(j + 1, 1 - b).start()

        write(n_win - 2, 0).wait()
        write(n_win - 1, 1).wait()

    return pl.kernel(
        body, out_type=jax.ShapeDtypeStruct((m, width), jnp.int32), mesh=_sc_mesh(),
        scratch_types=[pltpu.VMEM((n_win, win), jnp.int32), pltpu.VMEM((2, win, width), jnp.int32),
                       pltpu.SemaphoreType.DMA((2,)), pltpu.SemaphoreType.DMA((2,))],
        name="sc_gather_rows",
    )(table, pos_w)


MOE_BLOCK = 1024


def _silu_mul(a, b):
    return (a * jax.nn.sigmoid(a)) * b


def _expert_ffn_kernel(be_ref, nv_ref, xs_ref, wg_ref, wu_ref, wd_ref, ys_ref, wg_bf, wu_bf, wd_bf):
    i = pl.program_id(0)
    fresh = jnp.logical_or(i == 0, be_ref[i] != be_ref[jnp.maximum(i - 1, 0)])

    @pl.when(fresh)
    def _():
        wg_bf[...] = wg_ref[...].astype(BF16)
        wu_bf[...] = wu_ref[...].astype(BF16)
        wd_bf[...] = wd_ref[...].astype(BF16)

    nv = nv_ref[i]

    @pl.when(nv > 0)
    def _():
        x = _unpack_bf16_pairs(xs_ref[...])
        row = lax.broadcasted_iota(jnp.int32, (x.shape[0], 1), 0)
        x = jnp.where(row < nv, x, 0.0).astype(BF16)
        mid = _silu_mul(_dot(x, wg_bf[...]), _dot(x, wu_bf[...]))
        ys_ref[...] = _pack_bf16_pairs(_dot(mid.astype(BF16), wd_bf[...]))

    @pl.when(nv <= 0)
    def _():
        ys_ref[...] = jnp.zeros(ys_ref.shape, ys_ref.dtype)


def _expert_ffn(xs, block_expert, block_rows, layer, w_gate, w_up, w_down):
    n_rows, half = xs.shape
    _, n_exp, d, de = w_gate.shape
    nb = n_rows // MOE_BLOCK
    wmap = lambda i, be, nv: (layer, be[i], 0, 0)
    grid_spec = pltpu.PrefetchScalarGridSpec(
        num_scalar_prefetch=2,
        grid=(nb,),
        in_specs=[pl.BlockSpec((MOE_BLOCK, half), lambda i, be, nv: (jnp.where(nv[i] > 0, i, 0), 0)),
                  pl.BlockSpec((None, None, d, de), wmap),
                  pl.BlockSpec((None, None, d, de), wmap),
                  pl.BlockSpec((None, None, de, d), wmap)],
        out_specs=pl.BlockSpec((MOE_BLOCK, half), lambda i, be, nv: (i, 0)),
        scratch_shapes=[pltpu.VMEM((d, de), BF16), pltpu.VMEM((d, de), BF16), pltpu.VMEM((de, d), BF16)],
    )
    return pl.pallas_call(
        _expert_ffn_kernel,
        grid_spec=grid_spec,
        out_shape=jax.ShapeDtypeStruct((n_rows, half), jnp.int32),
        compiler_params=_cparams(("arbitrary",), VMEM_LIMIT),
        name="moe_expert_ffn",
    )(block_expert, block_rows, xs, w_gate, w_up, w_down)


def _combine_kernel(yt_ref, wt_ref, hp_ref, x_ref, mod_ref, sg_ref, su_ref, sd_ref, *rest, d):
    o_ref = rest[-1]
    h = _unpack_bf16_pairs(hp_ref[...]).astype(BF16)
    acc = _dot(_silu_mul(_dot(h, sg_ref[...]), _dot(h, su_ref[...])).astype(BF16), sd_ref[...])
    w = wt_ref[...]
    for k in range(TOP_K):
        acc = acc + w[:, k:k + 1] * _unpack_bf16_pairs(yt_ref[k])
    o_ref[...] = x_ref[...] + mod_ref[...][:, 2 * d:] * acc


def _combine(y_tok, w_tok, h_packed, x2, out_prev, mod, layer, sh_gate, sh_up, sh_down, seq, tile0):
    t_all, d = x2.shape
    ds = sh_gate.shape[1]
    tm = MOE_TILE
    ntiles = h_packed.shape[0] // tm
    tiles_per_seq = seq // tm
    row = lambda i: (i, 0)
    off = lambda i: (i + tile0, 0)
    const = lambda i: (0, 0)
    in_specs = [pl.BlockSpec((TOP_K, tm, d // 2), lambda i: (0, i, 0)),
                pl.BlockSpec((tm, TOP_K), row),
                pl.BlockSpec((tm, d // 2), row),
                pl.BlockSpec((tm, d), off),
                pl.BlockSpec((None, None, 1, 3 * d), lambda i: (layer, (i + tile0) // tiles_per_seq, 0, 0)),
                pl.BlockSpec((d, ds), const),
                pl.BlockSpec((d, ds), const),
                pl.BlockSpec((ds, d), const)]
    args = [y_tok, w_tok, h_packed, x2, mod, sh_gate.astype(BF16), sh_up.astype(BF16), sh_down.astype(BF16)]
    aliases = {}
    if out_prev is not None:
        in_specs.append(pl.BlockSpec(memory_space=pl.ANY))
        args.append(out_prev)
        aliases = {len(args) - 1: 0}
    return pl.pallas_call(
        functools.partial(_combine_kernel, d=d),
        grid=(ntiles,),
        in_specs=in_specs,
        out_specs=pl.BlockSpec((tm, d), off),
        out_shape=jax.ShapeDtypeStruct((t_all, d), F32),
        input_output_aliases=aliases,
        compiler_params=_cparams(("arbitrary",), VMEM_LIMIT),
        name="moe_combine",
    )(*args)


def _moe_group(x2, out_prev, tile0, h_packed, gates, ranks, counts, mod, layer, w_gate, w_up, w_down,
               sh_gate, sh_up, sh_down, seq):
    t, half = h_packed.shape
    d = 2 * half
    n_exp = w_gate.shape[1]
    cnt = counts[:, 0].astype(jnp.int32)
    padded = (cnt + MOE_BLOCK - 1) // MOE_BLOCK * MOE_BLOCK
    eidx = jnp.arange(n_exp, dtype=jnp.int32)
    seg_end = jnp.sum(jnp.where(eidx[None, :] <= eidx[:, None], padded[None, :], 0), axis=1)
    seg_start = seg_end - padded
    nb = t * TOP_K // MOE_BLOCK + n_exp
    first_row = jnp.arange(nb, dtype=jnp.int32) * MOE_BLOCK
    block_expert = jnp.sum((seg_end[None, :] <= first_row[:, None]).astype(jnp.int32), axis=1)
    block_expert = jnp.minimum(block_expert, n_exp - 1)
    last_row = jnp.sum(jnp.where(block_expert[:, None] == eidx[None, :], (seg_start + cnt)[None, :], 0), axis=1)
    block_rows = jnp.clip(last_row - first_row, 0, MOE_BLOCK).astype(jnp.int32)
    pos, w = _assign_slots(ranks, gates, seg_start.astype(F32).reshape(n_exp, 1))
    xs = _sc_scatter_rows(h_packed, pos, nb * MOE_BLOCK)
    ys = _expert_ffn(xs, block_expert, block_rows, layer, w_gate, w_up, w_down)
    y_tok = _sc_gather_rows(ys, pos.reshape(TOP_K * t)).reshape(TOP_K, t, d // 2)
    return _combine(y_tok, w.T, h_packed, x2, out_prev, mod, layer, sh_gate, sh_up, sh_down, seq, tile0)


MOE_GROUPS = 1


def _moe_layer(x2, g, mod, layer, router_w, router_bias, w_gate, w_up, w_down, sh_gate, sh_up, sh_down, seq):
    ntiles = x2.shape[0] // MOE_TILE // MOE_GROUPS
    rtiles = ntiles * MOE_TILE // ROUTER_TILE
    routed = [_ffn_router(x2, g, mod, layer, router_w, router_bias, seq, grp * rtiles, rtiles)
              for grp in range(MOE_GROUPS)]
    out = None
    for grp in range(MOE_GROUPS):
        out = _moe_group(x2, out, grp * ntiles, *routed[grp], mod, layer, w_gate, w_up, w_down,
                         sh_gate, sh_up, sh_down, seq)
    return out


def kernel(x, c, positions, mix_norm_g, mix_mod_w, mix_mod_b, ffn_norm_g, ffn_mod_w, ffn_mod_b,
           hyb_w_in, hyb_w_out, s5_lambda_re, s5_lambda_im, s5_log_dt, s5_b_re, s5_b_im,
           s5_c_re, s5_c_im, s5_d, s5_glu_w, s5_glu_b, attn_q_norm_g, attn_k_norm_g,
           pool_w, pool_scale, router_w, router_bias, exp_w_gate, exp_w_up, exp_w_down,
           sh_w_gate, sh_w_up, sh_w_down):
    bsz, seq, d = x.shape
    t = bsz * seq
    depth = mix_norm_g.shape[0]
    x2 = x.reshape(t, d)
    mix_mod = _mod_vectors(c, mix_mod_w, mix_mod_b).reshape(depth, bsz, 1, 3 * d)
    ffn_mod = _mod_vectors(c, ffn_mod_w, ffn_mod_b).reshape(depth, bsz, 1, 3 * d)
    for i in range(depth):
        j = i // 2
        if i % 2 == 0:
            u, q_hm, qi_hm, k, v, ki, wi = _in_projection(
                x2, mix_norm_g[i], mix_mod, i, hyb_w_in[j], positions, attn_q_norm_g[j], attn_k_norm_g[j], bsz, seq)
            weights = _s5_weights(s5_lambda_re[j], s5_lambda_im[j], s5_log_dt[j], s5_b_re[j], s5_b_im[j],
                                  s5_c_re[j], s5_c_im[j], s5_d[j], s5_glu_w[j], s5_glu_b[j])
            y_ssm = _s5_mixer(u, weights, bsz, seq)
            y_att = _dsa_attention(q_hm, qi_hm, wi, k, v, ki, bsz, seq)
            x2 = _out_projection(x2, y_ssm, y_att, hyb_w_out[j], mix_mod, i, seq)
        else:
            x2 = _pool_layer(x2, mix_norm_g[i], mix_mod, i, pool_w[j], pool_scale[j], seq)
        x2 = _moe_layer(x2, ffn_norm_g[i], ffn_mod, i, router_w[i], router_bias[i], exp_w_gate, exp_w_up, exp_w_down,
                        sh_w_gate[i], sh_w_up[i], sh_w_down[i], seq)
    return x2.reshape(bsz, seq, d)
```

```python
import functools

import numpy as np
import jax
import jax.numpy as jnp
from jax import lax
from jax.experimental import pallas as pl
from jax.experimental.pallas import tpu as pltpu
from jax.experimental.pallas import tpu_sc as plsc

F32 = jnp.float32
BF16 = jnp.bfloat16

EPS = 1e-6
S5_GROUP = 16
S5_STATE = 64
HEAD_DIM = 64
ATTN_HEADS = 8
IDX_HEADS = 8
ROPE_HALF = 8
ROPE_THETA = 500000.0
TOPK_MAX = 256
POOL_WINDOWS = (2, 4, 8, 16)
POOL_HALO = 16
N_EXPERT_GROUPS = 8
TOPK_GROUPS = 4
TOP_K = 8
ROUTED_SCALE = 2.5
S5_CHUNK = 16
NEG_BIG = -1e30
VMEM_LIMIT = 56 * 1024 * 1024


def _cparams(sem, vmem=None):
    return pltpu.CompilerParams(dimension_semantics=sem, vmem_limit_bytes=vmem)


def _dot(a, b):
    return jnp.dot(a, b, preferred_element_type=F32)


def _dot_nt(a, b):
    return lax.dot_general(a, b, (((1,), (1,)), ((), ())), preferred_element_type=F32)


def _split_bf16(a):
    hi = a.astype(BF16)
    lo = (a - hi.astype(F32)).astype(BF16)
    return hi, lo


def _norm_mod(x, g, shift, scale):
    y = x * lax.rsqrt(jnp.mean(x * x, axis=-1, keepdims=True) + EPS)
    return (y * g) * (1.0 + scale) + shift


def _mod_kernel(ct_ref, w_ref, b_ref, o_ref):
    ct = ct_ref[...]
    cs = ct * jax.nn.sigmoid(ct)
    w = w_ref[...]
    rows = [jnp.sum(w * cs[:, b:b + 1], axis=0, keepdims=True) for b in range(ct.shape[1])]
    o_ref[...] = jnp.concatenate(rows, axis=0) + b_ref[...]


def _mod_vectors(c, w, b):
    nl, d, n3 = w.shape
    bsz = c.shape[0]
    tn = 512
    return pl.pallas_call(
        _mod_kernel,
        grid=(nl, n3 // tn),
        in_specs=[pl.BlockSpec((d, bsz), lambda l, j: (0, 0)),
                  pl.BlockSpec((None, d, tn), lambda l, j: (l, 0, j)),
                  pl.BlockSpec((None, 1, tn), lambda l, j: (l, 0, j))],
        out_specs=pl.BlockSpec((None, bsz, tn), lambda l, j: (l, 0, j)),
        out_shape=jax.ShapeDtypeStruct((nl, bsz, n3), F32),
        compiler_params=_cparams(("arbitrary", "arbitrary")),
        name="mod_vectors",
    )(c.T, w, b.reshape(nl, 1, n3))


def _inproj_kernel(x_ref, g_ref, mod_ref, w_ref, pos_ref, inv_ref, qg_ref, kg_ref,
                   u_ref, q_ref, qi_ref, k_ref, v_ref, ki_ref, wi_ref, u_scr, *, d):
    x = x_ref[...]
    mod = mod_ref[...]
    h = _norm_mod(x, g_ref[...], mod[:, :d], mod[:, d:2 * d])
    proj = _dot(h.astype(BF16), w_ref[...])
    tm = x.shape[0]
    width = ATTN_HEADS * HEAD_DIM

    ang = inv_ref[...] * pos_ref[...].astype(F32)
    cos, sin = jnp.cos(ang), jnp.sin(ang)

    def head_t(xt, gain, scale):
        if scale is not None:
            r = lax.rsqrt(jnp.mean(xt * xt, axis=0, keepdims=True) + EPS) * scale
            xt = xt * gain
        x1, x2 = xt[:ROPE_HALF], xt[ROPE_HALF:2 * ROPE_HALF]
        out = jnp.concatenate([x1 * cos - x2 * sin, x1 * sin + x2 * cos, xt[2 * ROPE_HALF:]], axis=0)
        return out if scale is None else out * r

    for slab in range(width // 128):
        u_scr[slab] = proj[:, slab * 128:(slab + 1) * 128]
        for j in range(S5_CHUNK):
            piece = u_scr[slab, pl.ds(j, tm // S5_CHUNK, stride=S5_CHUNK), :]
            col = (slab * S5_CHUNK + j) * 128
            u_ref[:, col:col + 128] = piece.astype(u_ref.dtype)
    q = proj[:, width:2 * width]
    qi = proj[:, 2 * width:3 * width]
    small = proj[:, 3 * width:3 * width + 256]
    qg = qg_ref[...]
    for pair in range(ATTN_HEADS // 2):
        sl = slice(pair * 128, (pair + 1) * 128)
        qt = q[:, sl].T
        qit = qi[:, sl].T
        for half in range(2):
            hs = slice(half * HEAD_DIM, (half + 1) * HEAD_DIM)
            q_ref[2 * pair + half] = head_t(qt[hs], qg, HEAD_DIM ** -0.5).astype(q_ref.dtype)
            qi_ref[2 * pair + half] = head_t(qit[hs], None, None).astype(qi_ref.dtype)
    kvt = small[:, :128].T
    kt = head_t(kvt[:HEAD_DIM], kg_ref[...], 1.0)
    kv = jnp.concatenate([kt, kvt[HEAD_DIM:]], axis=0).T
    k_ref[...] = kv[:, :HEAD_DIM].astype(k_ref.dtype)
    v_ref[...] = kv[:, HEAD_DIM:].astype(v_ref.dtype)
    kiwt = small[:, 128:256].T
    kit = jnp.concatenate([head_t(kiwt[:HEAD_DIM], None, None), kiwt[HEAD_DIM:]], axis=0).T
    ki_ref[...] = kit[:, :HEAD_DIM].astype(ki_ref.dtype)
    wscale = (IDX_HEADS ** -0.5) * (HEAD_DIM ** -0.5)
    wi_ref[...] = kiwt[HEAD_DIM:HEAD_DIM + IDX_HEADS] * wscale


def _in_projection(x2, g, mod, layer, w_in, positions, q_norm_g, k_norm_g, bsz, seq):
    t, d = x2.shape
    tm = 512
    width = ATTN_HEADS * HEAD_DIM
    cuts = np.cumsum([width, width, HEAD_DIM, HEAD_DIM, width, HEAD_DIM, IDX_HEADS])
    u_w, q_w, k_w, v_w, qi_w, ki_w, wi_w = jnp.split(w_in, cuts[:-1].tolist(), axis=1)
    pad = jnp.zeros((d, 128 - HEAD_DIM - IDX_HEADS), w_in.dtype)
    w = jnp.concatenate([u_w, q_w, qi_w, k_w, v_w, ki_w, wi_w, pad], axis=1).astype(BF16)
    nw = w.shape[1]
    inv = np.power(ROPE_THETA, -2.0 * np.arange(ROPE_HALF) / (2 * ROPE_HALF))
    inv = jnp.asarray(inv.reshape(ROPE_HALF, 1), F32)
    tiles_per_seq = seq // tm
    row = lambda i: (i, 0)
    const = lambda i: (0, 0)
    hm = lambda i: (i // tiles_per_seq, 0, 0, i % tiles_per_seq)
    outs = pl.pallas_call(
        functools.partial(_inproj_kernel, d=d),
        grid=(t // tm,),
        in_specs=[pl.BlockSpec((tm, d), row),
                  pl.BlockSpec((1, d), const),
                  pl.BlockSpec((None, None, 1, 3 * d), lambda i: (layer, i // tiles_per_seq, 0, 0)),
                  pl.BlockSpec((d, nw), const),
                  pl.BlockSpec((None, 1, tm), lambda i: (i, 0, 0)),
                  pl.BlockSpec((ROPE_HALF, 1), const),
                  pl.BlockSpec((HEAD_DIM, 1), const),
                  pl.BlockSpec((HEAD_DIM, 1), const)],
        out_specs=[pl.BlockSpec((tm // S5_CHUNK, width * S5_CHUNK), row),
                   pl.BlockSpec((None, ATTN_HEADS, HEAD_DIM, tm), hm),
                   pl.BlockSpec((None, IDX_HEADS, HEAD_DIM, tm), hm),
                   pl.BlockSpec((tm, HEAD_DIM), row),
                   pl.BlockSpec((tm, HEAD_DIM), row),
                   pl.BlockSpec((tm, HEAD_DIM), row),
                   pl.BlockSpec((IDX_HEADS, tm), lambda i: (0, i))],
        out_shape=[jax.ShapeDtypeStruct((t // S5_CHUNK, width * S5_CHUNK), BF16),
                   jax.ShapeDtypeStruct((bsz, ATTN_HEADS, HEAD_DIM, seq), BF16),
                   jax.ShapeDtypeStruct((bsz, IDX_HEADS, HEAD_DIM, seq), BF16),
                   jax.ShapeDtypeStruct((t, HEAD_DIM), BF16),
                   jax.ShapeDtypeStruct((t, HEAD_DIM), BF16),
                   jax.ShapeDtypeStruct((t, HEAD_DIM), BF16),
                   jax.ShapeDtypeStruct((IDX_HEADS, t), F32)],
        scratch_shapes=[pltpu.VMEM((width // 128, tm, 128), F32)],
        compiler_params=_cparams(("arbitrary",), VMEM_LIMIT),
        name="in_projection",
    )(x2, g.reshape(1, d), mod, w, positions.reshape(t // tm, 1, tm), inv,
      q_norm_g.reshape(HEAD_DIM, 1), k_norm_g.reshape(HEAD_DIM, 1))
    return outs


def _s5_weights(lam_re, lam_im, log_dt, b_re, b_im, c_re, c_im, d_skip, glu_w, glu_b):
    L = S5_CHUNK
    g, p = lam_re.shape
    hch = S5_GROUP
    dt = jnp.exp(log_dt)[:, None]
    lr, li = lam_re, lam_im
    tau = jnp.arange(L + 1, dtype=F32)[:, None, None]
    mag = jnp.exp(lr * dt * tau)
    pw_r, pw_i = mag * jnp.cos(li * dt * tau), mag * jnp.sin(li * dt * tau)
    ar, ai = pw_r[1], pw_i[1]
    nr, ni = ar - 1.0, ai
    den = lr * lr + li * li
    cr, ci = (nr * lr + ni * li) / den, (ni * lr - nr * li) / den
    bb_r = cr[..., None] * b_re - ci[..., None] * b_im
    bb_i = cr[..., None] * b_im + ci[..., None] * b_re
    cl_r = c_re[None] * pw_r[:L, :, None, :] - c_im[None] * pw_i[:L, :, None, :]
    cl_i = c_re[None] * pw_i[:L, :, None, :] + c_im[None] * pw_r[:L, :, None, :]
    taps = (jnp.einsum('tghp,gpk->tghk', cl_r, bb_r) - jnp.einsum('tghp,gpk->tghk', cl_i, bb_i))
    ns = g // 8
    eye = jnp.eye(8, dtype=F32)
    kd = jnp.einsum('tsgoh,gf->tsghfo', taps.reshape(L, ns, 8, hch, hch), eye).reshape(L, ns, 128, 128)
    zero = jnp.zeros_like(kd[0])
    k2 = jnp.stack([jnp.concatenate([jnp.concatenate([kd[2 * dd], kd[2 * dd + 1]], axis=-1),
                                     jnp.concatenate([kd[2 * dd - 1] if dd else zero, kd[2 * dd]], axis=-1)], axis=-2)
                    for dd in range(L // 2)], axis=1).astype(BF16)
    ii = jnp.arange(L)
    rev_r, rev_i = pw_r[L - 1 - ii], pw_i[L - 1 - ii]
    z_r = rev_r[..., None] * bb_r[None] - rev_i[..., None] * bb_i[None]
    z_i = rev_r[..., None] * bb_i[None] + rev_i[..., None] * bb_r[None]
    zc = jnp.concatenate([z_r, z_i], axis=2)
    zc = jnp.transpose(zc.reshape(L, ns, 8, 2 * p, hch), (1, 0, 2, 4, 3)).reshape(ns, L * 128, 1, 2 * p)
    row_group = (jnp.arange(L * 128) // hch) % 8
    w_z = jnp.where((row_group[:, None] == jnp.arange(8)[None, :])[None, :, :, None], zc, 0.0)
    w_z = w_z.reshape(ns, L // 2, 2 * 128, 8 * 2 * p).astype(BF16)
    co_r = c_re[None] * pw_r[1:, :, None, :] - c_im[None] * pw_i[1:, :, None, :]
    co_i = c_re[None] * pw_i[1:, :, None, :] + c_im[None] * pw_r[1:, :, None, :]
    cc = jnp.concatenate([co_r, -co_i], axis=-1)
    cc = jnp.transpose(cc.reshape(L, ns, 8, hch, 2 * p), (1, 4, 0, 2, 3)).reshape(ns, 1, 2 * p, L * 128)
    col_group = (jnp.arange(L * 128) // hch) % 8
    w_c = jnp.where((jnp.arange(8)[:, None] == col_group[None, :])[None, :, None, :], cc, 0.0)
    w_c = w_c.reshape(ns, 8 * 2 * p, L * 128)
    w_c = w_c.astype(BF16)
    al_r, al_i = pw_r[L], pw_i[L]
    dec_a = jnp.concatenate([al_r, al_r], axis=-1).reshape(ns, 1, 8 * 2 * p)
    dec_b = jnp.concatenate([-al_i, al_i], axis=-1).reshape(ns, 1, 8 * 2 * p)
    d_t = jnp.tile(d_skip.reshape(ns, 1, 128), (1, 1, L))
    gl = jnp.einsum('sghk,gf->sghfk', glu_w.reshape(ns, 8, hch, hch), eye).reshape(ns, 128, 128)
    glu2 = jnp.einsum('ab,shk->sahbk', jnp.eye(2, dtype=F32), gl).reshape(ns, 256, 256).astype(BF16)
    glu_bt = jnp.tile(glu_b.reshape(ns, 1, 128), (1, 1, L))
    return k2, w_z, w_c, dec_a, dec_b, d_t, glu2, glu_bt


def _s5_kernel(u_ref, k2_ref, wz_ref, wc_ref, da_ref, db_ref, d_ref, g2_ref, gb_ref, o_ref,
               zp_ref, zq_ref, sp_ref, *, nchunk):
    nblk = S5_CHUNK // 2
    ub = [u_ref[:, j * 256:(j + 1) * 256] for j in range(nblk)]
    z = _dot(ub[0], wz_ref[0])
    for j in range(1, nblk):
        z = z + _dot(ub[j], wz_ref[j])
    zp_ref[...] = z
    half = S5_STATE
    zq_ref[...] = jnp.concatenate([pltpu.roll(z[:, s * 2 * half:(s + 1) * 2 * half], half, 1)
                                   for s in range(z.shape[1] // (2 * half))], axis=1)
    da = da_ref[...]
    db = db_ref[...]
    dbq = -db

    def step(c, carry):
        sp, sq = carry
        sp_ref[pl.ds(c, 1), :] = sp
        sp_new = da * sp + db * sq + zp_ref[pl.ds(c, 1), :]
        sq_new = da * sq + dbq * sp + zq_ref[pl.ds(c, 1), :]
        return sp_new, sq_new

    zero = jnp.zeros((1, z.shape[1]), F32)
    lax.fori_loop(0, nchunk, step, (zero, zero))
    sh, sl = _split_bf16(sp_ref[...])
    wc = wc_ref[...]
    y_inter = _dot(sh, wc) + _dot(sl, wc)
    for i in range(nblk):
        cs = slice(i * 256, (i + 1) * 256)
        acc = y_inter[:, cs] + d_ref[:, cs] * ub[i].astype(F32)
        for j in range(i + 1):
            acc = acc + _dot(ub[j], k2_ref[i - j])
        y = jax.nn.gelu(acc)
        y = y * jax.nn.sigmoid(_dot(y.astype(BF16), g2_ref[...]) + gb_ref[:, cs])
        for a in range(2):
            o_ref[pl.ds(2 * i + a, nchunk, stride=S5_CHUNK), :] = y[:, a * 128:(a + 1) * 128]


def _s5_mixer(u2, weights, bsz, seq):
    k2, w_z, w_c, dec_a, dec_b, d_t, glu2, glu_bt = weights
    ns = k2.shape[0]
    nchunk = seq // S5_CHUNK
    cols = S5_CHUNK * 128
    st = w_z.shape[-1]
    slab3 = lambda b, s: (s, 0, 0)
    slab4 = lambda b, s: (s, 0, 0, 0)
    return pl.pallas_call(
        functools.partial(_s5_kernel, nchunk=nchunk),
        grid=(bsz, ns),
        in_specs=[pl.BlockSpec((nchunk, cols), lambda b, s: (b, s)),
                  pl.BlockSpec((None,) + k2.shape[1:], slab4),
                  pl.BlockSpec((None,) + w_z.shape[1:], slab4),
                  pl.BlockSpec((None, st, cols), slab3),
                  pl.BlockSpec((None, 1, st), slab3),
                  pl.BlockSpec((None, 1, st), slab3),
                  pl.BlockSpec((None, 1, cols), slab3),
                  pl.BlockSpec((None, 256, 256), slab3),
                  pl.BlockSpec((None, 1, cols), slab3)],
        out_specs=pl.BlockSpec((seq, 128), lambda b, s: (b, s)),
        out_shape=jax.ShapeDtypeStruct((bsz * seq, ns * 128), F32),
        scratch_shapes=[pltpu.VMEM((nchunk, st), F32), pltpu.VMEM((nchunk, st), F32), pltpu.VMEM((nchunk, st), F32)],
        compiler_params=_cparams(("arbitrary", "arbitrary"), VMEM_LIMIT),
        name="s5_mixer",
    )(u2, k2, w_z, w_c, dec_a, dec_b, d_t, glu2, glu_bt)


def _dsa_kernel(qt_ref, qit_ref, wt_ref, k_ref, vt_ref, ki_ref, o_ref,
                sc_ref, s_ref, m_ref, l_ref, acc_ref, *, tq, ck, k_sel, idx_bits, max_iter):
    i = pl.program_id(1)
    nheads = qt_ref.shape[0]
    n_ck = (i * tq + tq + ck - 1) // ck
    t_pos = i * tq + lax.broadcasted_iota(jnp.int32, (1, tq), 1)
    krow = lax.broadcasted_iota(jnp.int32, (ck, 1), 0)
    kf = float(k_sel)
    inf = jnp.inf

    fr = 32
    sub = min(32768 // tq, ck)

    def fold(x, op):
        return op(x.reshape(ck // fr, fr, tq), axis=0)

    wt = wt_ref[...]

    def score_chunk(c, carry):
        lo8, hi8 = carry
        for part in range(ck // sub):
            off = pl.multiple_of(c * ck + part * sub, sub)
            kic = ki_ref[pl.ds(off, sub), :]
            acc = jnp.zeros((sub, tq), F32)
            for h in range(nheads):
                acc = acc + wt[h:h + 1, :] * jnp.maximum(_dot(kic, qit_ref[h]), 0.0)
            acc = acc + 0.0
            vis = off + krow[:sub] <= t_pos
            sc_ref[c, part * sub:(part + 1) * sub, :] = jnp.where(vis, acc, -inf)
            hi8 = jnp.maximum(hi8, jnp.max(jnp.where(vis, acc, -inf).reshape(sub // fr, fr, tq), axis=0))
            lo8 = jnp.minimum(lo8, jnp.min(jnp.where(vis, acc, inf).reshape(sub // fr, fr, tq), axis=0))
        return lo8, hi8

    lo8, hi8 = lax.fori_loop(0, n_ck, score_chunk, (jnp.full((fr, tq), inf, F32), jnp.full((fr, tq), -inf, F32)))
    lo = jnp.min(lo8, axis=0, keepdims=True)
    hi = jnp.max(hi8, axis=0, keepdims=True)
    n_vis = (t_pos + 1).astype(F32)
    few = n_vis < kf

    def reduce_keys(fn, init, combine, op):
        def body(c, part):
            return combine(part, fold(fn(sc_ref[c], c * ck), op))
        return op(lax.fori_loop(0, n_ck, body, jnp.full((fr, tq), init, F32)), axis=0, keepdims=True)

    def count(pred):
        return reduce_keys(lambda x, off: jnp.where(pred(x, off), 1.0, 0.0), 0.0, jnp.add, jnp.sum)

    def bisect(lo_, hi_, c_lo):
        mid = lo_ + (hi_ - lo_) * 0.5
        cnt = count(lambda x, off: x >= mid)
        up = cnt >= kf
        return jnp.where(up, mid, lo_), jnp.where(up, hi_, mid), jnp.where(up, cnt, c_lo)

    def status(lo_, hi_, c_lo):
        mid = lo_ + (hi_ - lo_) * 0.5
        unresolved = jnp.where(few, 0.0, jnp.where(c_lo == kf, 0.0, 1.0))
        movable = jnp.where(mid > lo_, jnp.where(mid < hi_, 1.0, 0.0), 0.0)
        return jnp.max(unresolved), jnp.max(unresolved * movable)

    def search_body(st):
        it, lo_, hi_, c_lo, _, _ = st
        for _ in range(2):
            lo_, hi_, c_lo = bisect(lo_, hi_, c_lo)
        return (it + 2, lo_, hi_, c_lo) + status(lo_, hi_, c_lo)

    st = lax.while_loop(lambda st: jnp.logical_and(st[0] < max_iter, st[5] > 0.5), search_body,
                        (jnp.int32(0), lo, hi, n_vis) + status(lo, hi, n_vis))
    lo, open_ = st[1], st[4]

    def resolve():
        def walk(thr_, strict):
            keep = (lambda x: x > thr_) if strict else (lambda x: x >= thr_)
            return reduce_keys(lambda x, off: jnp.where(keep(x), x, inf), inf, jnp.minimum, jnp.min)

        def above_equal(thr_):
            return count(lambda x, off: x > thr_), count(lambda x, off: x == thr_)

        def too_low(n_gt):
            return jnp.where(few, 0.0, jnp.where(n_gt >= kf, 1.0, 0.0))

        thr0 = walk(lo, False)

        def walk_body(st):
            thr_, n_gt, _ = st
            thr2 = jnp.where(too_low(n_gt) > 0.5, walk(thr_, True), thr_)
            return (thr2,) + above_equal(thr2)

        thr_, n_gt, n_eq = lax.while_loop(lambda st: jnp.max(too_low(st[1])) > 0.5, walk_body,
                                          (thr0,) + above_equal(thr0))
        need = kf - n_gt

        def tie_cut():
            def mark_equal(c, _):
                s_ref[0, c] = jnp.where(sc_ref[c] == thr_, 1.0, 0.0)
                return 0

            lax.fori_loop(0, n_ck, mark_equal, 0)

            def idx_step(b, cur):
                trial = cur | (jnp.int32(1) << (idx_bits - 1 - b))

                def body(c, part):
                    return part + fold(jnp.where(c * ck + krow < trial, s_ref[0, c], 0.0), jnp.sum)
                cnt = jnp.sum(lax.fori_loop(0, n_ck, body, jnp.zeros((fr, tq), F32)), axis=0, keepdims=True)
                return jnp.where(cnt < need, trial, cur)

            return lax.fori_loop(0, idx_bits, idx_step, jnp.zeros((1, tq), jnp.int32))

        tied = jnp.max(jnp.where(few, 0.0, jnp.where(n_eq > need, 1.0, 0.0))) > 0.5
        return thr_, lax.cond(tied, tie_cut, lambda: jnp.full((1, tq), 0x7FFFFFFF, jnp.int32))

    thr, cut = lax.cond(open_ > 0.5, resolve, lambda: (lo, jnp.full((1, tq), 0x7FFFFFFF, jnp.int32)))

    for h in range(nheads):
        m_ref[h] = jnp.full((fr, tq), NEG_BIG, F32)
        l_ref[h] = jnp.zeros((fr, tq), F32)
        acc_ref[h] = jnp.zeros((HEAD_DIM, tq), F32)

    def attn_scores(c, _):
        off = pl.multiple_of(c * ck, ck)
        kc = k_ref[pl.ds(off, ck), :]
        x = sc_ref[c]
        at_thr = jnp.where(x == thr, jnp.where(off + krow <= cut, 0.0, NEG_BIG), NEG_BIG)
        bias = jnp.where(x > thr, 0.0, at_thr)
        for h in range(nheads):
            s = _dot(kc, qt_ref[h]) + bias
            s_ref[h, c] = s
            m_ref[h] = jnp.maximum(m_ref[h], fold(s, jnp.max))
        return 0

    lax.fori_loop(0, n_ck, attn_scores, 0)
    for h in range(nheads):
        m_ref[h] = jnp.broadcast_to(jnp.max(m_ref[h], axis=0, keepdims=True), (fr, tq))

    def attn_values(c, _):
        vt = vt_ref[c]
        for h in range(nheads):
            p = jnp.exp(s_ref[h, c] - m_ref[h][:1, :])
            l_ref[h] += fold(p, jnp.sum)
            acc_ref[h] += _dot(vt, p.astype(BF16))
        return 0

    lax.fori_loop(0, n_ck, attn_values, 0)
    for h in range(nheads):
        l_row = jnp.sum(l_ref[h], axis=0, keepdims=True)
        o_ref[h * HEAD_DIM:(h + 1) * HEAD_DIM, :] = (acc_ref[h] / l_row).astype(o_ref.dtype)


def _dsa_attention(qt, qit, wt, k, v, ki, bsz, seq):
    tq = 256
    ck = min(512, seq)
    n_ck = seq // ck
    k_sel = min(TOPK_MAX, seq // 4)
    nq = seq // tq
    nh = ATTN_HEADS
    width = nh * HEAD_DIM
    vt = v.reshape(bsz, n_ck, ck, HEAD_DIM).transpose(0, 1, 3, 2)
    qspec = pl.BlockSpec((None, nh, HEAD_DIM, tq), lambda b, i: (b, 0, 0, i))
    out_t = pl.pallas_call(
        functools.partial(_dsa_kernel, tq=tq, ck=ck, k_sel=k_sel, idx_bits=(seq - 1).bit_length(), max_iter=22),
        grid=(bsz, nq),
        in_specs=[qspec, qspec,
                  pl.BlockSpec((IDX_HEADS, tq), lambda b, i: (0, b * nq + i)),
                  pl.BlockSpec((seq, HEAD_DIM), lambda b, i: (b, 0)),
                  pl.BlockSpec((None, n_ck, HEAD_DIM, ck), lambda b, i: (b, 0, 0, 0)),
                  pl.BlockSpec((seq, HEAD_DIM), lambda b, i: (b, 0))],
        out_specs=pl.BlockSpec((None, width, tq), lambda b, i: (b, 0, i)),
        out_shape=jax.ShapeDtypeStruct((bsz, width, seq), BF16),
        scratch_shapes=[pltpu.VMEM((n_ck, ck, tq), F32),
                        pltpu.VMEM((nh, n_ck, ck, tq), F32),
                        pltpu.VMEM((nh, 32, tq), F32),
                        pltpu.VMEM((nh, 32, tq), F32),
                        pltpu.VMEM((nh, HEAD_DIM, tq), F32)],
        compiler_params=_cparams(("arbitrary", "arbitrary"), VMEM_LIMIT),
        name="dsa_attention",
    )(qt, qit, wt, k, vt, ki)
    return out_t.transpose(0, 2, 1).reshape(bsz * seq, width)


def _outproj_kernel(x_ref, ys_ref, ya_ref, w_ref, mod_ref, o_ref, *, d):
    width = ys_ref.shape[1]
    y = _dot(ys_ref[...].astype(BF16), w_ref[:width, :]) + _dot(ya_ref[...], w_ref[width:, :])
    o_ref[...] = x_ref[...] + mod_ref[...][:, 2 * d:] * y


def _out_projection(x2, y_ssm, y_att, w_out, mod, layer, seq):
    t, d = x2.shape
    tm = 512
    width = y_ssm.shape[1]
    tiles_per_seq = seq // tm
    row = lambda i: (i, 0)
    return pl.pallas_call(
        functools.partial(_outproj_kernel, d=d),
        grid=(t // tm,),
        in_specs=[pl.BlockSpec((tm, d), row),
                  pl.BlockSpec((tm, width), row),
                  pl.BlockSpec((tm, width), row),
                  pl.BlockSpec((2 * width, d), lambda i: (0, 0)),
                  pl.BlockSpec((None, None, 1, 3 * d), lambda i: (layer, i // tiles_per_seq, 0, 0))],
        out_specs=pl.BlockSpec((tm, d), row),
        out_shape=jax.ShapeDtypeStruct((t, d), F32),
        compiler_params=_cparams(("arbitrary",)),
        name="out_projection",
    )(x2, y_ssm, y_att, w_out.astype(BF16), mod)


def _pool_kernel(x_ref, halo_ref, g_ref, mod_ref, pw_ref, ps_ref, o_ref, h_ref, fa_ref, fb_ref,
                 *, d, tiles_per_seq):
    i = pl.program_id(0)
    tm = x_ref.shape[0]
    pad, rows = POOL_HALO, POOL_HALO + tm
    mod = mod_ref[...]
    shift, scale, gate = mod[:, :d], mod[:, d:2 * d], mod[:, 2 * d:]
    x = x_ref[...]
    first = (i % tiles_per_seq) == 0
    gw = d // len(POOL_WINDOWS)
    h_ref[:pad, :] = jnp.zeros((pad, d), F32)
    fa_ref[:pad, :] = jnp.zeros((pad, gw), F32)
    fb_ref[:pad, :] = jnp.zeros((pad, gw), F32)
    h_ref[2 * pad:, :] = _norm_mod(x, g_ref[...], shift, scale)
    halo = _norm_mod(halo_ref[...], g_ref[...], shift, scale)
    h_ref[pad:2 * pad, :] = jnp.where(first, 0.0, halo)
    pos = (i % tiles_per_seq) * tm + lax.broadcasted_iota(jnp.int32, (tm, 1), 0)
    ys = []
    for gi, win in enumerate(POOL_WINDOWS):
        cs = slice(gi * gw, (gi + 1) * gw)
        cur = h_ref[2 * pad:, cs]
        src, width, bufs = None, 1, (fa_ref, fb_ref)
        while width < win:
            dst = bufs[0] if src is not bufs[0] else bufs[1]
            if src is None:
                dst[pad:, :] = h_ref[pad:, cs] + h_ref[pad - width:pad - width + rows, cs]
            else:
                dst[pad:, :] = src[pad:, :] + src[pad - width:pad - width + rows, :]
            src, width = dst, 2 * width
        tot = src[2 * pad:, :]
        cnt = jnp.minimum(pos + 1, win).astype(F32)
        pooled = tot / cnt - cur
        ys.append(_dot(pooled.astype(BF16), pw_ref[gi]))
    y = jnp.concatenate(ys, axis=1) * ps_ref[...]
    o_ref[...] = x + gate * y


def _pool_layer(x2, g, mod, layer, pool_w, pool_scale, seq):
    t, d = x2.shape
    tm = 512
    tiles_per_seq = seq // tm
    gw = d // len(POOL_WINDOWS)
    hb = tm // POOL_HALO
    return pl.pallas_call(
        functools.partial(_pool_kernel, d=d, tiles_per_seq=tiles_per_seq),
        grid=(t // tm,),
        in_specs=[pl.BlockSpec((tm, d), lambda i: (i, 0)),
                  pl.BlockSpec((POOL_HALO, d), lambda i: (jnp.maximum(i * hb - 1, 0), 0)),
                  pl.BlockSpec((1, d), lambda i: (0, 0)),
                  pl.BlockSpec((None, None, 1, 3 * d), lambda i: (layer, i // tiles_per_seq, 0, 0)),
                  pl.BlockSpec((len(POOL_WINDOWS), gw, gw), lambda i: (0, 0, 0)),
                  pl.BlockSpec((1, d), lambda i: (0, 0))],
        out_specs=pl.BlockSpec((tm, d), lambda i: (i, 0)),
        out_shape=jax.ShapeDtypeStruct((t, d), F32),
        scratch_shapes=[pltpu.VMEM((tm + 2 * POOL_HALO, d), F32),
                        pltpu.VMEM((tm + 2 * POOL_HALO, gw), F32),
                        pltpu.VMEM((tm + 2 * POOL_HALO, gw), F32)],
        compiler_params=_cparams(("arbitrary",)),
        name="pool_mixer",
    )(x2, x2, g.reshape(1, d), mod, pool_w.astype(BF16), pool_scale.reshape(1, d))


def _first_max(vals, idx, big):
    m = jnp.max(vals, axis=0, keepdims=True)
    first = jnp.min(jnp.where(vals == m, idx, big), axis=0, keepdims=True)
    return m, first


def _pack_bf16_pairs(x):
    n = x.shape[1] // 2
    lo = pltpu.bitcast(x[:, :n].astype(BF16).astype(F32), jnp.int32)
    hi = pltpu.bitcast(x[:, n:].astype(BF16).astype(F32), jnp.int32)
    return hi | ((lo >> 16) & 0xFFFF)


def _unpack_bf16_pairs(w):
    lo = pltpu.bitcast(w << 16, F32)
    hi = pltpu.bitcast(w & jnp.int32(-65536), F32)
    return jnp.concatenate([lo, hi], axis=1)


def _router_kernel(x_ref, g_ref, mod_ref, rw_ref, rb_ref, tri_ref, h_ref, gate_ref, rank_ref, cnt_ref,
                   run_ref, *, d, n_exp):
    mod = mod_ref[...]
    h = _norm_mod(x_ref[...], g_ref[...], mod[:, :d], mod[:, d:2 * d])
    h_ref[...] = _pack_bf16_pairs(h)
    tm = h.shape[0]
    hh, hl = _split_bf16(h)
    rw = rw_ref[...]
    rh, rl = _split_bf16(rw)
    logits = _dot_nt(rh, hh) + (_dot_nt(rh, hl) + _dot_nt(rl, hh))
    scores = jax.nn.sigmoid(logits)
    sel = scores + rb_ref[...]
    gsz = n_exp // N_EXPERT_GROUPS
    neg = -jnp.inf
    shape3 = (N_EXPERT_GROUPS, gsz, tm)
    sel3 = sel.reshape(shape3)
    sub = lax.broadcasted_iota(jnp.int32, shape3, 1)
    m1 = jnp.max(sel3, axis=1, keepdims=True)
    f1 = jnp.min(jnp.where(sel3 == m1, sub, gsz), axis=1, keepdims=True)
    m2 = jnp.max(jnp.where(sub == f1, neg, sel3), axis=1, keepdims=True)
    gscore = jnp.broadcast_to(m1 + m2, shape3).reshape(n_exp, tm)
    eidx = lax.broadcasted_iota(jnp.int32, (n_exp, tm), 0)
    gidx = eidx // gsz
    keep = jnp.zeros((n_exp, tm), F32)
    work = gscore
    for _ in range(TOPK_GROUPS):
        _, first = _first_max(work, gidx, N_EXPERT_GROUPS)
        hit = gidx == first
        keep = jnp.where(hit, 1.0, keep)
        work = jnp.where(hit, neg, work)
    work = jnp.where(keep > 0.0, sel, neg)
    chosen = jnp.zeros((n_exp, tm), F32)
    for _ in range(TOP_K):
        _, first = _first_max(work, eidx, n_exp)
        hit = eidx == first
        chosen = jnp.where(hit, 1.0, chosen)
        work = jnp.where(hit, neg, work)
    picked = chosen * scores
    gate_ref[...] = picked / jnp.sum(picked, axis=0, keepdims=True) * ROUTED_SCALE

    @pl.when(pl.program_id(0) == 0)
    def _():
        run_ref[...] = jnp.zeros(run_ref.shape, F32)

    before = _dot(chosen.astype(BF16), tri_ref[...])
    run = run_ref[...]
    rank_ref[...] = jnp.where(chosen > 0.0, before + run[:, :1], -1.0)
    run = run + jnp.broadcast_to(jnp.sum(chosen, axis=1, keepdims=True), run.shape)
    run_ref[...] = run
    cnt_ref[...] = run


MOE_TILE = 512
ROUTER_TILE = 1024


def _ffn_router(x2, g, mod, layer, router_w, router_bias, seq, tile0, ntiles):
    d = x2.shape[1]
    n_exp = router_w.shape[1]
    tm = ROUTER_TILE
    t = ntiles * tm
    tiles_per_seq = seq // tm
    tri = (np.arange(tm)[:, None] < np.arange(tm)[None, :]).astype(np.float32)
    return pl.pallas_call(
        functools.partial(_router_kernel, d=d, n_exp=n_exp),
        grid=(ntiles,),
        in_specs=[pl.BlockSpec((tm, d), lambda i: (i + tile0, 0)),
                  pl.BlockSpec((1, d), lambda i: (0, 0)),
                  pl.BlockSpec((None, None, 1, 3 * d), lambda i: (layer, (i + tile0) // tiles_per_seq, 0, 0)),
                  pl.BlockSpec((n_exp, d), lambda i: (0, 0)),
                  pl.BlockSpec((n_exp, 1), lambda i: (0, 0)),
                  pl.BlockSpec((tm, tm), lambda i: (0, 0))],
        out_specs=[pl.BlockSpec((tm, d // 2), lambda i: (i, 0)),
                   pl.BlockSpec((n_exp, tm), lambda i: (0, i)),
                   pl.BlockSpec((n_exp, tm), lambda i: (0, i)),
                   pl.BlockSpec((n_exp, 128), lambda i: (0, 0))],
        out_shape=[jax.ShapeDtypeStruct((t, d // 2), jnp.int32),
                   jax.ShapeDtypeStruct((n_exp, t), F32),
                   jax.ShapeDtypeStruct((n_exp, t), F32),
                   jax.ShapeDtypeStruct((n_exp, 128), F32)],
        scratch_shapes=[pltpu.VMEM((n_exp, 128), F32)],
        compiler_params=_cparams(("arbitrary",)),
        name="ffn_router",
    )(x2, g.reshape(1, d), mod, router_w.T, router_bias.reshape(n_exp, 1), jnp.asarray(tri, BF16))


def _assign_kernel(rank_ref, gate_ref, start_ref, pos_ref, w_ref, *, n_exp):
    rank = rank_ref[...]
    gates = gate_ref[...]
    tm = rank.shape[1]
    slot = rank + start_ref[...]
    eidx = lax.broadcasted_iota(jnp.int32, (n_exp, tm), 0).astype(F32)
    alive = jnp.where(rank >= 0.0, eidx, float(n_exp))
    kidx = lax.broadcasted_iota(jnp.int32, (TOP_K, tm), 0)
    pos = jnp.zeros((TOP_K, tm), F32)
    wts = jnp.zeros((TOP_K, tm), F32)
    for k in range(TOP_K):
        first = jnp.min(alive, axis=0, keepdims=True)
        hit = alive == first
        pos_k = jnp.sum(jnp.where(hit, slot, 0.0), axis=0, keepdims=True)
        w_k = jnp.sum(jnp.where(hit, gates, 0.0), axis=0, keepdims=True)
        pos = jnp.where(kidx == k, pos_k, pos)
        wts = jnp.where(kidx == k, w_k, wts)
        alive = jnp.where(hit, float(n_exp), alive)
    pos_ref[...] = pos.astype(jnp.int32)
    w_ref[...] = wts


def _assign_slots(ranks, gates, start):
    n_exp, t = ranks.shape
    tm = 512
    return pl.pallas_call(
        functools.partial(_assign_kernel, n_exp=n_exp),
        grid=(t // tm,),
        in_specs=[pl.BlockSpec((n_exp, tm), lambda i: (0, i)),
                  pl.BlockSpec((n_exp, tm), lambda i: (0, i)),
                  pl.BlockSpec((n_exp, 1), lambda i: (0, 0))],
        out_specs=[pl.BlockSpec((TOP_K, tm), lambda i: (0, i)),
                   pl.BlockSpec((TOP_K, tm), lambda i: (0, i))],
        out_shape=[jax.ShapeDtypeStruct((TOP_K, t), jnp.int32), jax.ShapeDtypeStruct((TOP_K, t), F32)],
        compiler_params=_cparams(("arbitrary",)),
        name="moe_assign",
    )(ranks, gates, start)


SC_CORES = 2
SC_SUBCORES = 16
SC_WINDOW = 128


def _sc_mesh():
    return plsc.VectorSubcoreMesh(core_axis_name="c", subcore_axis_name="s",
                                  num_cores=SC_CORES, num_subcores=SC_SUBCORES)


def _sc_scatter_rows(rows, pos, n_out):
    t, width = rows.shape
    nk = pos.shape[0]
    win = SC_WINDOW
    n_win = t // win // (SC_CORES * SC_SUBCORES)
    pos_w = pos.reshape(nk, t // win, win).transpose(1, 0, 2)

    def body(rows_hbm, pos_hbm, out_hbm, idx_v, buf_v, sem):
        wid = lax.axis_index("s") * SC_CORES + lax.axis_index("c")

        @pl.loop(0, n_win)
        def _(j):
            w = wid * n_win + j
            pltpu.sync_copy(rows_hbm.at[pl.ds(w * win, win)], buf_v)
            pltpu.sync_copy(pos_hbm.at[w], idx_v)
            copies = [pltpu.make_async_copy(buf_v, out_hbm.at[idx_v.at[k]], sem) for k in range(nk)]
            for cp in copies:
                cp.start()
            for cp in copies:
                cp.wait()

    return pl.kernel(
        body, out_type=jax.ShapeDtypeStruct((n_out, width), jnp.int32), mesh=_sc_mesh(),
        scratch_types=[pltpu.VMEM((nk, win), jnp.int32), pltpu.VMEM((win, width), jnp.int32),
                       pltpu.SemaphoreType.DMA],
        name="sc_scatter_rows",
    )(rows, pos_w)


SC_GATHER_WINDOW = 64


def _sc_gather_rows(table, pos_flat):
    m = pos_flat.shape[0]
    width = table.shape[1]
    win = SC_GATHER_WINDOW
    workers = SC_CORES * SC_SUBCORES
    n_win = m // win // workers
    pos_w = pos_flat.reshape(workers, n_win, win)

    def body(table_hbm, pos_hbm, out_hbm, idx_v, buf_v, sem_g, sem_w):
        wid = lax.axis_index("s") * SC_CORES + lax.axis_index("c")
        base = wid * (n_win * win)
        pltpu.sync_copy(pos_hbm.at[wid], idx_v)

        def gather(j, b):
            return pltpu.make_async_copy(table_hbm.at[idx_v.at[j]], buf_v.at[b], sem_g.at[b])

        def write(j, b):
            return pltpu.make_async_copy(buf_v.at[b], out_hbm.at[pl.ds(base + j * win, win)], sem_w.at[b])

        gather(0, 0).start()

        @pl.loop(0, n_win, step=2)
        def _(j0):
            for b in range(2):
                j = j0 + b
                gather(j, b).wait()
                write(j, b).start()

                @pl.when(j + 1 < n_win)
                def _():
                    @pl.when(j >= 1)
                    def _():
                        write(j - 1, 1 - b).wait()
                    gather(j + 1, 1 - b).start()

        write(n_win - 2, 0).wait()
        write(n_win - 1, 1).wait()

    return pl.kernel(
        body, out_type=jax.ShapeDtypeStruct((m, width), jnp.int32), mesh=_sc_mesh(),
        scratch_types=[pltpu.VMEM((n_win, win), jnp.int32), pltpu.VMEM((2, win, width), jnp.int32),
                       pltpu.SemaphoreType.DMA((2,)), pltpu.SemaphoreType.DMA((2,))],
        name="sc_gather_rows",
    )(table, pos_w)


MOE_BLOCK = 1024


def _silu_mul(a, b):
    return (a * jax.nn.sigmoid(a)) * b


def _expert_ffn_kernel(be_ref, nv_ref, xs_ref, wg_ref, wu_ref, wd_ref, ys_ref, wg_bf, wu_bf, wd_bf):
    i = pl.program_id(0)
    fresh = jnp.logical_or(i == 0, be_ref[i] != be_ref[jnp.maximum(i - 1, 0)])

    @pl.when(fresh)
    def _():
        wg_bf[...] = wg_ref[...].astype(BF16)
        wu_bf[...] = wu_ref[...].astype(BF16)
        wd_bf[...] = wd_ref[...].astype(BF16)

    nv = nv_ref[i]

    @pl.when(nv > 0)
    def _():
        x = _unpack_bf16_pairs(xs_ref[...])
        row = lax.broadcasted_iota(jnp.int32, (x.shape[0], 1), 0)
        x = jnp.where(row < nv, x, 0.0).astype(BF16)
        mid = _silu_mul(_dot(x, wg_bf[...]), _dot(x, wu_bf[...]))
        ys_ref[...] = _pack_bf16_pairs(_dot(mid.astype(BF16), wd_bf[...]))

    @pl.when(nv <= 0)
    def _():
        ys_ref[...] = jnp.zeros(ys_ref.shape, ys_ref.dtype)


def _expert_ffn(xs, block_expert, block_rows, layer, w_gate, w_up, w_down):
    n_rows, half = xs.shape
    _, n_exp, d, de = w_gate.shape
    nb = n_rows // MOE_BLOCK
    wmap = lambda i, be, nv: (layer, be[i], 0, 0)
    grid_spec = pltpu.PrefetchScalarGridSpec(
        num_scalar_prefetch=2,
        grid=(nb,),
        in_specs=[pl.BlockSpec((MOE_BLOCK, half), lambda i, be, nv: (jnp.where(nv[i] > 0, i, 0), 0)),
                  pl.BlockSpec((None, None, d, de), wmap),
                  pl.BlockSpec((None, None, d, de), wmap),
                  pl.BlockSpec((None, None, de, d), wmap)],
        out_specs=pl.BlockSpec((MOE_BLOCK, half), lambda i, be, nv: (i, 0)),
        scratch_shapes=[pltpu.VMEM((d, de), BF16), pltpu.VMEM((d, de), BF16), pltpu.VMEM((de, d), BF16)],
    )
    return pl.pallas_call(
        _expert_ffn_kernel,
        grid_spec=grid_spec,
        out_shape=jax.ShapeDtypeStruct((n_rows, half), jnp.int32),
        compiler_params=_cparams(("arbitrary",), VMEM_LIMIT),
        name="moe_expert_ffn",
    )(block_expert, block_rows, xs, w_gate, w_up, w_down)


def _combine_kernel(yt_ref, wt_ref, hp_ref, x_ref, mod_ref, sg_ref, su_ref, sd_ref, *rest, d):
    o_ref = rest[-1]
    h = _unpack_bf16_pairs(hp_ref[...]).astype(BF16)
    acc = _dot(_silu_mul(_dot(h, sg_ref[...]), _dot(h, su_ref[...])).astype(BF16), sd_ref[...])
    w = wt_ref[...]
    for k in range(TOP_K):
        acc = acc + w[:, k:k + 1] * _unpack_bf16_pairs(yt_ref[k])
    o_ref[...] = x_ref[...] + mod_ref[...][:, 2 * d:] * acc


def _combine(y_tok, w_tok, h_packed, x2, out_prev, mod, layer, sh_gate, sh_up, sh_down, seq, tile0):
    t_all, d = x2.shape
    ds = sh_gate.shape[1]
    tm = MOE_TILE
    ntiles = h_packed.shape[0] // tm
    tiles_per_seq = seq // tm
    row = lambda i: (i, 0)
    off = lambda i: (i + tile0, 0)
    const = lambda i: (0, 0)
    in_specs = [pl.BlockSpec((TOP_K, tm, d // 2), lambda i: (0, i, 0)),
                pl.BlockSpec((tm, TOP_K), row),
                pl.BlockSpec((tm, d // 2), row),
                pl.BlockSpec((tm, d), off),
                pl.BlockSpec((None, None, 1, 3 * d), lambda i: (layer, (i + tile0) // tiles_per_seq, 0, 0)),
                pl.BlockSpec((d, ds), const),
                pl.BlockSpec((d, ds), const),
                pl.BlockSpec((ds, d), const)]
    args = [y_tok, w_tok, h_packed, x2, mod, sh_gate.astype(BF16), sh_up.astype(BF16), sh_down.astype(BF16)]
    aliases = {}
    if out_prev is not None:
        in_specs.append(pl.BlockSpec(memory_space=pl.ANY))
        args.append(out_prev)
        aliases = {len(args) - 1: 0}
    return pl.pallas_call(
        functools.partial(_combine_kernel, d=d),
        grid=(ntiles,),
        in_specs=in_specs,
        out_specs=pl.BlockSpec((tm, d), off),
        out_shape=jax.ShapeDtypeStruct((t_all, d), F32),
        input_output_aliases=aliases,
        compiler_params=_cparams(("arbitrary",), VMEM_LIMIT),
        name="moe_combine",
    )(*args)


def _moe_group(x2, out_prev, tile0, h_packed, gates, ranks, counts, mod, layer, w_gate, w_up, w_down,
               sh_gate, sh_up, sh_down, seq):
    t, half = h_packed.shape
    d = 2 * half
    n_exp = w_gate.shape[1]
    cnt = counts[:, 0].astype(jnp.int32)
    padded = (cnt + MOE_BLOCK - 1) // MOE_BLOCK * MOE_BLOCK
    eidx = jnp.arange(n_exp, dtype=jnp.int32)
    seg_end = jnp.sum(jnp.where(eidx[None, :] <= eidx[:, None], padded[None, :], 0), axis=1)
    seg_start = seg_end - padded
    nb = t * TOP_K // MOE_BLOCK + n_exp
    first_row = jnp.arange(nb, dtype=jnp.int32) * MOE_BLOCK
    block_expert = jnp.sum((seg_end[None, :] <= first_row[:, None]).astype(jnp.int32), axis=1)
    block_expert = jnp.minimum(block_expert, n_exp - 1)
    last_row = jnp.sum(jnp.where(block_expert[:, None] == eidx[None, :], (seg_start + cnt)[None, :], 0), axis=1)
    block_rows = jnp.clip(last_row - first_row, 0, MOE_BLOCK).astype(jnp.int32)
    pos, w = _assign_slots(ranks, gates, seg_start.astype(F32).reshape(n_exp, 1))
    xs = _sc_scatter_rows(h_packed, pos, nb * MOE_BLOCK)
    ys = _expert_ffn(xs, block_expert, block_rows, layer, w_gate, w_up, w_down)
    y_tok = _sc_gather_rows(ys, pos.reshape(TOP_K * t)).reshape(TOP_K, t, d // 2)
    return _combine(y_tok, w.T, h_packed, x2, out_prev, mod, layer, sh_gate, sh_up, sh_down, seq, tile0)


MOE_GROUPS = 1


def _moe_layer(x2, g, mod, layer, router_w, router_bias, w_gate, w_up, w_down, sh_gate, sh_up, sh_down, seq):
    ntiles = x2.shape[0] // MOE_TILE // MOE_GROUPS
    rtiles = ntiles * MOE_TILE // ROUTER_TILE
    routed = [_ffn_router(x2, g, mod, layer, router_w, router_bias, seq, grp * rtiles, rtiles)
              for grp in range(MOE_GROUPS)]
    out = None
    for grp in range(MOE_GROUPS):
        out = _moe_group(x2, out, grp * ntiles, *routed[grp], mod, layer, w_gate, w_up, w_down,
                         sh_gate, sh_up, sh_down, seq)
    return out


def kernel(x, c, positions, mix_norm_g, mix_mod_w, mix_mod_b, ffn_norm_g, ffn_mod_w, ffn_mod_b,
           hyb_w_in, hyb_w_out, s5_lambda_re, s5_lambda_im, s5_log_dt, s5_b_re, s5_b_im,
           s5_c_re, s5_c_im, s5_d, s5_glu_w, s5_glu_b, attn_q_norm_g, attn_k_norm_g,
           pool_w, pool_scale, router_w, router_bias, exp_w_gate, exp_w_up, exp_w_down,
           sh_w_gate, sh_w_up, sh_w_down):
    bsz, seq, d = x.shape
    t = bsz * seq
    depth = mix_norm_g.shape[0]
    x2 = x.reshape(t, d)
    mix_mod = _mod_vectors(c, mix_mod_w, mix_mod_b).reshape(depth, bsz, 1, 3 * d)
    ffn_mod = _mod_vectors(c, ffn_mod_w, ffn_mod_b).reshape(depth, bsz, 1, 3 * d)
    for i in range(depth):
        j = i // 2
        if i % 2 == 0:
            u, q_hm, qi_hm, k, v, ki, wi = _in_projection(
                x2, mix_norm_g[i], mix_mod, i, hyb_w_in[j], positions, attn_q_norm_g[j], attn_k_norm_g[j], bsz, seq)
            weights = _s5_weights(s5_lambda_re[j], s5_lambda_im[j], s5_log_dt[j], s5_b_re[j], s5_b_im[j],
                                  s5_c_re[j], s5_c_im[j], s5_d[j], s5_glu_w[j], s5_glu_b[j])
            y_ssm = _s5_mixer(u, weights, bsz, seq)
            y_att = _dsa_attention(q_hm, qi_hm, wi, k, v, ki, bsz, seq)
            x2 = _out_projection(x2, y_ssm, y_att, hyb_w_out[j], mix_mod, i, seq)
        else:
            x2 = _pool_layer(x2, mix_norm_g[i], mix_mod, i, pool_w[j], pool_scale[j], seq)
        x2 = _moe_layer(x2, ffn_norm_g[i], ffn_mod, i, router_w[i], router_bias[i], exp_w_gate, exp_w_up, exp_w_down,
                        sh_w_gate[i], sh_w_up[i], sh_w_down[i], seq)
    return x2.reshape(bsz, seq, d)
```

```python
import functools

import numpy as np
import jax
import jax.numpy as jnp
from jax import lax
from jax.experimental import pallas as pl
from jax.experimental.pallas import tpu as pltpu
from jax.experimental.pallas import tpu_sc as plsc

F32 = jnp.float32
BF16 = jnp.bfloat16

EPS = 1e-6
S5_GROUP = 16
S5_STATE = 64
HEAD_DIM = 64
ATTN_HEADS = 8
IDX_HEADS = 8
ROPE_HALF = 8
ROPE_THETA = 500000.0
TOPK_MAX = 256
POOL_WINDOWS = (2, 4, 8, 16)
POOL_HALO = 16
N_EXPERT_GROUPS = 8
TOPK_GROUPS = 4
TOP_K = 8
ROUTED_SCALE = 2.5
S5_CHUNK = 16
NEG_BIG = -1e30
VMEM_LIMIT = 56 * 1024 * 1024


def _cparams(sem, vmem=None):
    return pltpu.CompilerParams(dimension_semantics=sem, vmem_limit_bytes=vmem)


def _dot(a, b):
    return jnp.dot(a, b, preferred_element_type=F32)


def _dot_nt(a, b):
    return lax.dot_general(a, b, (((1,), (1,)), ((), ())), preferred_element_type=F32)


def _split_bf16(a):
    hi = a.astype(BF16)
    lo = (a - hi.astype(F32)).astype(BF16)
    return hi, lo


def _norm_mod(x, g, shift, scale):
    y = x * lax.rsqrt(jnp.mean(x * x, axis=-1, keepdims=True) + EPS)
    return (y * g) * (1.0 + scale) + shift


def _mod_kernel(ct_ref, w_ref, b_ref, o_ref):
    ct = ct_ref[...]
    cs = ct * jax.nn.sigmoid(ct)
    w = w_ref[...]
    rows = [jnp.sum(w * cs[:, b:b + 1], axis=0, keepdims=True) for b in range(ct.shape[1])]
    o_ref[...] = jnp.concatenate(rows, axis=0) + b_ref[...]


def _mod_vectors(c, w, b):
    nl, d, n3 = w.shape
    bsz = c.shape[0]
    tn = 512
    return pl.pallas_call(
        _mod_kernel,
        grid=(nl, n3 // tn),
        in_specs=[pl.BlockSpec((d, bsz), lambda l, j: (0, 0)),
                  pl.BlockSpec((None, d, tn), lambda l, j: (l, 0, j)),
                  pl.BlockSpec((None, 1, tn), lambda l, j: (l, 0, j))],
        out_specs=pl.BlockSpec((None, bsz, tn), lambda l, j: (l, 0, j)),
        out_shape=jax.ShapeDtypeStruct((nl, bsz, n3), F32),
        compiler_params=_cparams(("arbitrary", "arbitrary")),
        name="mod_vectors",
    )(c.T, w, b.reshape(nl, 1, n3))


def _inproj_kernel(x_ref, g_ref, mod_ref, w_ref, pos_ref, inv_ref, qg_ref, kg_ref,
                   u_ref, q_ref, qi_ref, k_ref, v_ref, ki_ref, wi_ref, u_scr, *, d):
    x = x_ref[...]
    mod = mod_ref[...]
    h = _norm_mod(x, g_ref[...], mod[:, :d], mod[:, d:2 * d])
    proj = _dot(h.astype(BF16), w_ref[...])
    tm = x.shape[0]
    width = ATTN_HEADS * HEAD_DIM

    ang = inv_ref[...] * pos_ref[...].astype(F32)
    cos, sin = jnp.cos(ang), jnp.sin(ang)

    def head_t(xt, gain, scale):
        if scale is not None:
            r = lax.rsqrt(jnp.mean(xt * xt, axis=0, keepdims=True) + EPS) * scale
            xt = xt * gain
        x1, x2 = xt[:ROPE_HALF], xt[ROPE_HALF:2 * ROPE_HALF]
        out = jnp.concatenate([x1 * cos - x2 * sin, x1 * sin + x2 * cos, xt[2 * ROPE_HALF:]], axis=0)
        return out if scale is None else out * r

    for slab in range(width // 128):
        u_scr[slab] = proj[:, slab * 128:(slab + 1) * 128]
        for j in range(S5_CHUNK):
            piece = u_scr[slab, pl.ds(j, tm // S5_CHUNK, stride=S5_CHUNK), :]
            col = (slab * S5_CHUNK + j) * 128
            u_ref[:, col:col + 128] = piece.astype(u_ref.dtype)
    q = proj[:, width:2 * width]
    qi = proj[:, 2 * width:3 * width]
    small = proj[:, 3 * width:3 * width + 256]
    qg = qg_ref[...]
    for pair in range(ATTN_HEADS // 2):
        sl = slice(pair * 128, (pair + 1) * 128)
        qt = q[:, sl].T
        qit = qi[:, sl].T
        for half in range(2):
            hs = slice(half * HEAD_DIM, (half + 1) * HEAD_DIM)
            q_ref[2 * pair + half] = head_t(qt[hs], qg, HEAD_DIM ** -0.5).astype(q_ref.dtype)
            qi_ref[2 * pair + half] = head_t(qit[hs], None, None).astype(qi_ref.dtype)
    kvt = small[:, :128].T
    kt = head_t(kvt[:HEAD_DIM], kg_ref[...], 1.0)
    kv = jnp.concatenate([kt, kvt[HEAD_DIM:]], axis=0).T
    k_ref[...] = kv[:, :HEAD_DIM].astype(k_ref.dtype)
    v_ref[...] = kv[:, HEAD_DIM:].astype(v_ref.dtype)
    kiwt = small[:, 128:256].T
    kit = jnp.concatenate([head_t(kiwt[:HEAD_DIM], None, None), kiwt[HEAD_DIM:]], axis=0).T
    ki_ref[...] = kit[:, :HEAD_DIM].astype(ki_ref.dtype)
    wscale = (IDX_HEADS ** -0.5) * (HEAD_DIM ** -0.5)
    wi_ref[...] = kiwt[HEAD_DIM:HEAD_DIM + IDX_HEADS] * wscale


def _in_projection(x2, g, mod, layer, w_in, positions, q_norm_g, k_norm_g, bsz, seq):
    t, d = x2.shape
    tm = 512
    width = ATTN_HEADS * HEAD_DIM
    cuts = np.cumsum([width, width, HEAD_DIM, HEAD_DIM, width, HEAD_DIM, IDX_HEADS])
    u_w, q_w, k_w, v_w, qi_w, ki_w, wi_w = jnp.split(w_in, cuts[:-1].tolist(), axis=1)
    pad = jnp.zeros((d, 128 - HEAD_DIM - IDX_HEADS), w_in.dtype)
    w = jnp.concatenate([u_w, q_w, qi_w, k_w, v_w, ki_w, wi_w, pad], axis=1).astype(BF16)
    nw = w.shape[1]
    inv = np.power(ROPE_THETA, -2.0 * np.arange(ROPE_HALF) / (2 * ROPE_HALF))
    inv = jnp.asarray(inv.reshape(ROPE_HALF, 1), F32)
    tiles_per_seq = seq // tm
    row = lambda i: (i, 0)
    const = lambda i: (0, 0)
    hm = lambda i: (i // tiles_per_seq, 0, 0, i % tiles_per_seq)
    outs = pl.pallas_call(
        functools.partial(_inproj_kernel, d=d),
        grid=(t // tm,),
        in_specs=[pl.BlockSpec((tm, d), row),
                  pl.BlockSpec((1, d), const),
                  pl.BlockSpec((None, None, 1, 3 * d), lambda i: (layer, i // tiles_per_seq, 0, 0)),
                  pl.BlockSpec((d, nw), const),
                  pl.BlockSpec((None, 1, tm), lambda i: (i, 0, 0)),
                  pl.BlockSpec((ROPE_HALF, 1), const),
                  pl.BlockSpec((HEAD_DIM, 1), const),
                  pl.BlockSpec((HEAD_DIM, 1), const)],
        out_specs=[pl.BlockSpec((tm // S5_CHUNK, width * S5_CHUNK), row),
                   pl.BlockSpec((None, ATTN_HEADS, HEAD_DIM, tm), hm),
                   pl.BlockSpec((None, IDX_HEADS, HEAD_DIM, tm), hm),
                   pl.BlockSpec((tm, HEAD_DIM), row),
                   pl.BlockSpec((tm, HEAD_DIM), row),
                   pl.BlockSpec((tm, HEAD_DIM), row),
                   pl.BlockSpec((IDX_HEADS, tm), lambda i: (0, i))],
        out_shape=[jax.ShapeDtypeStruct((t // S5_CHUNK, width * S5_CHUNK), BF16),
                   jax.ShapeDtypeStruct((bsz, ATTN_HEADS, HEAD_DIM, seq), BF16),
                   jax.ShapeDtypeStruct((bsz, IDX_HEADS, HEAD_DIM, seq), BF16),
                   jax.ShapeDtypeStruct((t, HEAD_DIM), BF16),
                   jax.ShapeDtypeStruct((t, HEAD_DIM), BF16),
                   jax.ShapeDtypeStruct((t, HEAD_DIM), BF16),
                   jax.ShapeDtypeStruct((IDX_HEADS, t), F32)],
        scratch_shapes=[pltpu.VMEM((width // 128, tm, 128), F32)],
        compiler_params=_cparams(("arbitrary",), VMEM_LIMIT),
        name="in_projection",
    )(x2, g.reshape(1, d), mod, w, positions.reshape(t // tm, 1, tm), inv,
      q_norm_g.reshape(HEAD_DIM, 1), k_norm_g.reshape(HEAD_DIM, 1))
    return outs


def _s5_weights(lam_re, lam_im, log_dt, b_re, b_im, c_re, c_im, d_skip, glu_w, glu_b):
    L = S5_CHUNK
    g, p = lam_re.shape
    hch = S5_GROUP
    dt = jnp.exp(log_dt)[:, None]
    lr, li = lam_re, lam_im
    tau = jnp.arange(L + 1, dtype=F32)[:, None, None]
    mag = jnp.exp(lr * dt * tau)
    pw_r, pw_i = mag * jnp.cos(li * dt * tau), mag * jnp.sin(li * dt * tau)
    ar, ai = pw_r[1], pw_i[1]
    nr, ni = ar - 1.0, ai
    den = lr * lr + li * li
    cr, ci = (nr * lr + ni * li) / den, (ni * lr - nr * li) / den
    bb_r = cr[..., None] * b_re - ci[..., None] * b_im
    bb_i = cr[..., None] * b_im + ci[..., None] * b_re
    cl_r = c_re[None] * pw_r[:L, :, None, :] - c_im[None] * pw_i[:L, :, None, :]
    cl_i = c_re[None] * pw_i[:L, :, None, :] + c_im[None] * pw_r[:L, :, None, :]
    taps = (jnp.einsum('tghp,gpk->tghk', cl_r, bb_r) - jnp.einsum('tghp,gpk->tghk', cl_i, bb_i))
    ns = g // 8
    eye = jnp.eye(8, dtype=F32)
    kd = jnp.einsum('tsgoh,gf->tsghfo', taps.reshape(L, ns, 8, hch, hch), eye).reshape(L, ns, 128, 128)
    zero = jnp.zeros_like(kd[0])
    k2 = jnp.stack([jnp.concatenate([jnp.concatenate([kd[2 * dd], kd[2 * dd + 1]], axis=-1),
                                     jnp.concatenate([kd[2 * dd - 1] if dd else zero, kd[2 * dd]], axis=-1)], axis=-2)
                    for dd in range(L // 2)], axis=1).astype(BF16)
    ii = jnp.arange(L)
    rev_r, rev_i = pw_r[L - 1 - ii], pw_i[L - 1 - ii]
    z_r = rev_r[..., None] * bb_r[None] - rev_i[..., None] * bb_i[None]
    z_i = rev_r[..., None] * bb_i[None] + rev_i[..., None] * bb_r[None]
    zc = jnp.concatenate([z_r, z_i], axis=2)
    zc = jnp.transpose(zc.reshape(L, ns, 8, 2 * p, hch), (1, 0, 2, 4, 3)).reshape(ns, L * 128, 2 * p)
    row_group = (jnp.arange(L * 128) // hch) % 8
    state_group = jnp.arange(8 * 2 * p) // (2 * p)
    w_z = jnp.where((row_group[:, None] == state_group[None, :])[None], jnp.concatenate([zc] * 8, axis=-1), 0.0)
    w_z = w_z.reshape(ns, L // 2, 2 * 128, 8 * 2 * p).astype(BF16)
    co_r = c_re[None] * pw_r[1:, :, None, :] - c_im[None] * pw_i[1:, :, None, :]
    co_i = c_re[None] * pw_i[1:, :, None, :] + c_im[None] * pw_r[1:, :, None, :]
    cc = jnp.concatenate([co_r, -co_i], axis=-1)
    cc = jnp.transpose(cc.reshape(L, ns, 8, hch, 2 * p), (1, 4, 0, 2, 3)).reshape(ns, 1, 2 * p, L * 128)
    col_group = (jnp.arange(L * 128) // hch) % 8
    w_c = jnp.where((jnp.arange(8)[:, None] == col_group[None, :])[None, :, None, :], cc, 0.0)
    w_c = w_c.reshape(ns, 8 * 2 * p, L * 128)
    w_c = w_c.astype(BF16)
    al_r, al_i = pw_r[L], pw_i[L]
    dec_a = jnp.concatenate([al_r, al_r], axis=-1).reshape(ns, 1, 8 * 2 * p)
    dec_b = jnp.concatenate([-al_i, al_i], axis=-1).reshape(ns, 1, 8 * 2 * p)
    d_t = jnp.tile(d_skip.reshape(ns, 1, 128), (1, 1, L))
    gl = jnp.einsum('sghk,gf->sghfk', glu_w.reshape(ns, 8, hch, hch), eye).reshape(ns, 128, 128)
    glu2 = jnp.einsum('ab,shk->sahbk', jnp.eye(2, dtype=F32), gl).reshape(ns, 256, 256).astype(BF16)
    glu_bt = jnp.tile(glu_b.reshape(ns, 1, 128), (1, 1, L))
    return k2, w_z, w_c, dec_a, dec_b, d_t, glu2, glu_bt


def _s5_kernel(u_ref, k2_ref, wz_ref, wc_ref, da_ref, db_ref, d_ref, g2_ref, gb_ref, o_ref,
               zp_ref, zq_ref, sp_ref, *, nchunk):
    nblk = S5_CHUNK // 2
    ub = [u_ref[:, j * 256:(j + 1) * 256] for j in range(nblk)]
    z = _dot(ub[0], wz_ref[0])
    for j in range(1, nblk):
        z = z + _dot(ub[j], wz_ref[j])
    zp_ref[...] = z
    half = S5_STATE
    zq_ref[...] = jnp.concatenate([pltpu.roll(z[:, s * 2 * half:(s + 1) * 2 * half], half, 1)
                                   for s in range(z.shape[1] // (2 * half))], axis=1)
    da = da_ref[...]
    db = db_ref[...]
    dbq = -db

    def step(c, carry):
        sp, sq = carry
        sp_ref[pl.ds(c, 1), :] = sp
        sp_new = da * sp + db * sq + zp_ref[pl.ds(c, 1), :]
        sq_new = da * sq + dbq * sp + zq_ref[pl.ds(c, 1), :]
        return sp_new, sq_new

    zero = jnp.zeros((1, z.shape[1]), F32)
    lax.fori_loop(0, nchunk, step, (zero, zero))
    sh, sl = _split_bf16(sp_ref[...])
    wc = wc_ref[...]
    y_inter = _dot(sh, wc) + _dot(sl, wc)
    for i in range(nblk):
        cs = slice(i * 256, (i + 1) * 256)
        acc = y_inter[:, cs] + d_ref[:, cs] * ub[i].astype(F32)
        for j in range(i + 1):
            acc = acc + _dot(ub[j], k2_ref[i - j])
        y = jax.nn.gelu(acc)
        y = y * jax.nn.sigmoid(_dot(y.astype(BF16), g2_ref[...]) + gb_ref[:, cs])
        for a in range(2):
            o_ref[pl.ds(2 * i + a, nchunk, stride=S5_CHUNK), :] = y[:, a * 128:(a + 1) * 128]


def _s5_mixer(u2, weights, bsz, seq):
    k2, w_z, w_c, dec_a, dec_b, d_t, glu2, glu_bt = weights
    ns = k2.shape[0]
    nchunk = seq // S5_CHUNK
    cols = S5_CHUNK * 128
    st = w_z.shape[-1]
    slab3 = lambda b, s: (s, 0, 0)
    slab4 = lambda b, s: (s, 0, 0, 0)
    return pl.pallas_call(
        functools.partial(_s5_kernel, nchunk=nchunk),
        grid=(bsz, ns),
        in_specs=[pl.BlockSpec((nchunk, cols), lambda b, s: (b, s)),
                  pl.BlockSpec((None,) + k2.shape[1:], slab4),
                  pl.BlockSpec((None,) + w_z.shape[1:], slab4),
                  pl.BlockSpec((None, st, cols), slab3),
                  pl.BlockSpec((None, 1, st), slab3),
                  pl.BlockSpec((None, 1, st), slab3),
                  pl.BlockSpec((None, 1, cols), slab3),
                  pl.BlockSpec((None, 256, 256), slab3),
                  pl.BlockSpec((None, 1, cols), slab3)],
        out_specs=pl.BlockSpec((seq, 128), lambda b, s: (b, s)),
        out_shape=jax.ShapeDtypeStruct((bsz * seq, ns * 128), F32),
        scratch_shapes=[pltpu.VMEM((nchunk, st), F32), pltpu.VMEM((nchunk, st), F32), pltpu.VMEM((nchunk, st), F32)],
        compiler_params=_cparams(("arbitrary", "arbitrary"), VMEM_LIMIT),
        name="s5_mixer",
    )(u2, k2, w_z, w_c, dec_a, dec_b, d_t, glu2, glu_bt)


def _dsa_kernel(qt_ref, qit_ref, wt_ref, k_ref, vt_ref, ki_ref, o_ref,
                sc_ref, s_ref, m_ref, l_ref, acc_ref, *, tq, ck, k_sel, idx_bits, max_iter):
    i = pl.program_id(1)
    nheads = qt_ref.shape[0]
    n_ck = (i * tq + tq + ck - 1) // ck
    t_pos = i * tq + lax.broadcasted_iota(jnp.int32, (1, tq), 1)
    krow = lax.broadcasted_iota(jnp.int32, (ck, 1), 0)
    kf = float(k_sel)
    inf = jnp.inf

    fr = 32
    sub = min(32768 // tq, ck)

    def fold(x, op):
        return op(x.reshape(ck // fr, fr, tq), axis=0)

    wt = wt_ref[...]

    def score_chunk(c, carry):
        lo8, hi8 = carry
        for part in range(ck // sub):
            off = pl.multiple_of(c * ck + part * sub, sub)
            kic = ki_ref[pl.ds(off, sub), :]
            acc = jnp.zeros((sub, tq), F32)
            for h in range(nheads):
                acc = acc + wt[h:h + 1, :] * jnp.maximum(_dot(kic, qit_ref[h]), 0.0)
            acc = acc + 0.0
            vis = off + krow[:sub] <= t_pos
            sc_ref[c, part * sub:(part + 1) * sub, :] = jnp.where(vis, acc, -inf)
            hi8 = jnp.maximum(hi8, jnp.max(jnp.where(vis, acc, -inf).reshape(sub // fr, fr, tq), axis=0))
            lo8 = jnp.minimum(lo8, jnp.min(jnp.where(vis, acc, inf).reshape(sub // fr, fr, tq), axis=0))
        return lo8, hi8

    lo8, hi8 = lax.fori_loop(0, n_ck, score_chunk, (jnp.full((fr, tq), inf, F32), jnp.full((fr, tq), -inf, F32)))
    lo = jnp.min(lo8, axis=0, keepdims=True)
    hi = jnp.max(hi8, axis=0, keepdims=True)
    n_vis = (t_pos + 1).astype(F32)
    few = n_vis < kf

    def reduce_keys(fn, init, combine, op):
        def body(c, part):
            return combine(part, fold(fn(sc_ref[c], c * ck), op))
        return op(lax.fori_loop(0, n_ck, body, jnp.full((fr, tq), init, F32)), axis=0, keepdims=True)

    def count(pred):
        return reduce_keys(lambda x, off: jnp.where(pred(x, off), 1.0, 0.0), 0.0, jnp.add, jnp.sum)

    def bisect(lo_, hi_, c_lo):
        mid = lo_ + (hi_ - lo_) * 0.5
        cnt = count(lambda x, off: x >= mid)
        up = cnt >= kf
        return jnp.where(up, mid, lo_), jnp.where(up, hi_, mid), jnp.where(up, cnt, c_lo)

    def status(lo_, hi_, c_lo):
        mid = lo_ + (hi_ - lo_) * 0.5
        unresolved = jnp.where(few, 0.0, jnp.where(c_lo == kf, 0.0, 1.0))
        movable = jnp.where(mid > lo_, jnp.where(mid < hi_, 1.0, 0.0), 0.0)
        return jnp.max(unresolved), jnp.max(unresolved * movable)

    def search_body(st):
        it, lo_, hi_, c_lo, _, _ = st
        for _ in range(2):
            lo_, hi_, c_lo = bisect(lo_, hi_, c_lo)
        return (it + 2, lo_, hi_, c_lo) + status(lo_, hi_, c_lo)

    st = lax.while_loop(lambda st: jnp.logical_and(st[0] < max_iter, st[5] > 0.5), search_body,
                        (jnp.int32(0), lo, hi, n_vis) + status(lo, hi, n_vis))
    lo, open_ = st[1], st[4]

    def resolve():
        def walk(thr_, strict):
            keep = (lambda x: x > thr_) if strict else (lambda x: x >= thr_)
            return reduce_keys(lambda x, off: jnp.where(keep(x), x, inf), inf, jnp.minimum, jnp.min)

        def above_equal(thr_):
            return count(lambda x, off: x > thr_), count(lambda x, off: x == thr_)

        def too_low(n_gt):
            return jnp.where(few, 0.0, jnp.where(n_gt >= kf, 1.0, 0.0))

        thr0 = walk(lo, False)

        def walk_body(st):
            thr_, n_gt, _ = st
            thr2 = jnp.where(too_low(n_gt) > 0.5, walk(thr_, True), thr_)
            return (thr2,) + above_equal(thr2)

        thr_, n_gt, n_eq = lax.while_loop(lambda st: jnp.max(too_low(st[1])) > 0.5, walk_body,
                                          (thr0,) + above_equal(thr0))
        need = kf - n_gt

        def tie_cut():
            def mark_equal(c, _):
                s_ref[0, c] = jnp.where(sc_ref[c] == thr_, 1.0, 0.0)
                return 0

            lax.fori_loop(0, n_ck, mark_equal, 0)

            def idx_step(b, cur):
                trial = cur | (jnp.int32(1) << (idx_bits - 1 - b))

                def body(c, part):
                    return part + fold(jnp.where(c * ck + krow < trial, s_ref[0, c], 0.0), jnp.sum)
                cnt = jnp.sum(lax.fori_loop(0, n_ck, body, jnp.zeros((fr, tq), F32)), axis=0, keepdims=True)
                return jnp.where(cnt < need, trial, cur)

            return lax.fori_loop(0, idx_bits, idx_step, jnp.zeros((1, tq), jnp.int32))

        tied = jnp.max(jnp.where(few, 0.0, jnp.where(n_eq > need, 1.0, 0.0))) > 0.5
        return thr_, lax.cond(tied, tie_cut, lambda: jnp.full((1, tq), 0x7FFFFFFF, jnp.int32))

    thr, cut = lax.cond(open_ > 0.5, resolve, lambda: (lo, jnp.full((1, tq), 0x7FFFFFFF, jnp.int32)))

    for h in range(nheads):
        m_ref[h] = jnp.full((fr, tq), NEG_BIG, F32)
        l_ref[h] = jnp.zeros((fr, tq), F32)
        acc_ref[h] = jnp.zeros((HEAD_DIM, tq), F32)

    def attn_scores(c, _):
        off = pl.multiple_of(c * ck, ck)
        kc = k_ref[pl.ds(off, ck), :]
        x = sc_ref[c]
        at_thr = jnp.where(x == thr, jnp.where(off + krow <= cut, 0.0, NEG_BIG), NEG_BIG)
        bias = jnp.where(x > thr, 0.0, at_thr)
        for h in range(nheads):
            s = _dot(kc, qt_ref[h]) + bias
            s_ref[h, c] = s
            m_ref[h] = jnp.maximum(m_ref[h], fold(s, jnp.max))
        return 0

    lax.fori_loop(0, n_ck, attn_scores, 0)
    for h in range(nheads):
        m_ref[h] = jnp.broadcast_to(jnp.max(m_ref[h], axis=0, keepdims=True), (fr, tq))

    def attn_values(c, _):
        vt = vt_ref[c]
        for h in range(nheads):
            p = jnp.exp(s_ref[h, c] - m_ref[h][:1, :])
            l_ref[h] += fold(p, jnp.sum)
            acc_ref[h] += _dot(vt, p.astype(BF16))
        return 0

    lax.fori_loop(0, n_ck, attn_values, 0)
    for h in range(nheads):
        l_row = jnp.sum(l_ref[h], axis=0, keepdims=True)
        o_ref[h * HEAD_DIM:(h + 1) * HEAD_DIM, :] = (acc_ref[h] / l_row).astype(o_ref.dtype)


def _dsa_attention(qt, qit, wt, k, v, ki, bsz, seq):
    tq = 256
    ck = min(512, seq)
    n_ck = seq // ck
    k_sel = min(TOPK_MAX, seq // 4)
    nq = seq // tq
    nh = ATTN_HEADS
    width = nh * HEAD_DIM
    vt = v.reshape(bsz, n_ck, ck, HEAD_DIM).transpose(0, 1, 3, 2)
    qspec = pl.BlockSpec((None, nh, HEAD_DIM, tq), lambda b, i: (b, 0, 0, i))
    out_t = pl.pallas_call(
        functools.partial(_dsa_kernel, tq=tq, ck=ck, k_sel=k_sel, idx_bits=(seq - 1).bit_length(), max_iter=22),
        grid=(bsz, nq),
        in_specs=[qspec, qspec,
                  pl.BlockSpec((IDX_HEADS, tq), lambda b, i: (0, b * nq + i)),
                  pl.BlockSpec((seq, HEAD_DIM), lambda b, i: (b, 0)),
                  pl.BlockSpec((None, n_ck, HEAD_DIM, ck), lambda b, i: (b, 0, 0, 0)),
                  pl.BlockSpec((seq, HEAD_DIM), lambda b, i: (b, 0))],
        out_specs=pl.BlockSpec((None, width, tq), lambda b, i: (b, 0, i)),
        out_shape=jax.ShapeDtypeStruct((bsz, width, seq), BF16),
        scratch_shapes=[pltpu.VMEM((n_ck, ck, tq), F32),
                        pltpu.VMEM((nh, n_ck, ck, tq), F32),
                        pltpu.VMEM((nh, 32, tq), F32),
                        pltpu.VMEM((nh, 32, tq), F32),
                        pltpu.VMEM((nh, HEAD_DIM, tq), F32)],
        compiler_params=_cparams(("arbitrary", "arbitrary"), VMEM_LIMIT),
        name="dsa_attention",
    )(qt, qit, wt, k, vt, ki)
    return out_t.transpose(0, 2, 1).reshape(bsz * seq, width)


def _outproj_kernel(x_ref, ys_ref, ya_ref, w_ref, mod_ref, o_ref, *, d):
    width = ys_ref.shape[1]
    y = _dot(ys_ref[...].astype(BF16), w_ref[:width, :]) + _dot(ya_ref[...], w_ref[width:, :])
    o_ref[...] = x_ref[...] + mod_ref[...][:, 2 * d:] * y


def _out_projection(x2, y_ssm, y_att, w_out, mod, layer, seq):
    t, d = x2.shape
    tm = min(1024, seq)
    width = y_ssm.shape[1]
    tiles_per_seq = seq // tm
    row = lambda i: (i, 0)
    return pl.pallas_call(
        functools.partial(_outproj_kernel, d=d),
        grid=(t // tm,),
        in_specs=[pl.BlockSpec((tm, d), row),
                  pl.BlockSpec((tm, width), row),
                  pl.BlockSpec((tm, width), row),
                  pl.BlockSpec((2 * width, d), lambda i: (0, 0)),
                  pl.BlockSpec((None, None, 1, 3 * d), lambda i: (layer, i // tiles_per_seq, 0, 0))],
        out_specs=pl.BlockSpec((tm, d), row),
        out_shape=jax.ShapeDtypeStruct((t, d), F32),
        compiler_params=_cparams(("arbitrary",), VMEM_LIMIT),
        name="out_projection",
    )(x2, y_ssm, y_att, w_out.astype(BF16), mod)


def _pool_kernel(x_ref, halo_ref, g_ref, mod_ref, pw_ref, ps_ref, o_ref, h_ref, fa_ref, fb_ref,
                 *, d, tiles_per_seq):
    i = pl.program_id(0)
    tm = x_ref.shape[0]
    pad, rows = POOL_HALO, POOL_HALO + tm
    mod = mod_ref[...]
    shift, scale, gate = mod[:, :d], mod[:, d:2 * d], mod[:, 2 * d:]
    x = x_ref[...]
    first = (i % tiles_per_seq) == 0
    gw = d // len(POOL_WINDOWS)
    h_ref[:pad, :] = jnp.zeros((pad, d), F32)
    fa_ref[:pad, :] = jnp.zeros((pad, gw), F32)
    fb_ref[:pad, :] = jnp.zeros((pad, gw), F32)
    h_ref[2 * pad:, :] = _norm_mod(x, g_ref[...], shift, scale)
    halo = _norm_mod(halo_ref[...], g_ref[...], shift, scale)
    h_ref[pad:2 * pad, :] = jnp.where(first, 0.0, halo)
    pos = (i % tiles_per_seq) * tm + lax.broadcasted_iota(jnp.int32, (tm, 1), 0)
    ys = []
    for gi, win in enumerate(POOL_WINDOWS):
        cs = slice(gi * gw, (gi + 1) * gw)
        cur = h_ref[2 * pad:, cs]
        src, width, bufs = None, 1, (fa_ref, fb_ref)
        while width < win:
            dst = bufs[0] if src is not bufs[0] else bufs[1]
            if src is None:
                dst[pad:, :] = h_ref[pad:, cs] + h_ref[pad - width:pad - width + rows, cs]
            else:
                dst[pad:, :] = src[pad:, :] + src[pad - width:pad - width + rows, :]
            src, width = dst, 2 * width
        tot = src[2 * pad:, :]
        cnt = jnp.minimum(pos + 1, win).astype(F32)
        pooled = tot / cnt - cur
        ys.append(_dot(pooled.astype(BF16), pw_ref[gi]))
    y = jnp.concatenate(ys, axis=1) * ps_ref[...]
    o_ref[...] = x + gate * y


def _pool_layer(x2, g, mod, layer, pool_w, pool_scale, seq):
    t, d = x2.shape
    tm = 512
    tiles_per_seq = seq // tm
    gw = d // len(POOL_WINDOWS)
    hb = tm // POOL_HALO
    return pl.pallas_call(
        functools.partial(_pool_kernel, d=d, tiles_per_seq=tiles_per_seq),
        grid=(t // tm,),
        in_specs=[pl.BlockSpec((tm, d), lambda i: (i, 0)),
                  pl.BlockSpec((POOL_HALO, d), lambda i: (jnp.maximum(i * hb - 1, 0), 0)),
                  pl.BlockSpec((1, d), lambda i: (0, 0)),
                  pl.BlockSpec((None, None, 1, 3 * d), lambda i: (layer, i // tiles_per_seq, 0, 0)),
                  pl.BlockSpec((len(POOL_WINDOWS), gw, gw), lambda i: (0, 0, 0)),
                  pl.BlockSpec((1, d), lambda i: (0, 0))],
        out_specs=pl.BlockSpec((tm, d), lambda i: (i, 0)),
        out_shape=jax.ShapeDtypeStruct((t, d), F32),
        scratch_shapes=[pltpu.VMEM((tm + 2 * POOL_HALO, d), F32),
                        pltpu.VMEM((tm + 2 * POOL_HALO, gw), F32),
                        pltpu.VMEM((tm + 2 * POOL_HALO, gw), F32)],
        compiler_params=_cparams(("arbitrary",)),
        name="pool_mixer",
    )(x2, x2, g.reshape(1, d), mod, pool_w.astype(BF16), pool_scale.reshape(1, d))


def _first_max(vals, idx, big):
    m = jnp.max(vals, axis=0, keepdims=True)
    first = jnp.min(jnp.where(vals == m, idx, big), axis=0, keepdims=True)
    return m, first


def _pack_bf16_pairs(x):
    n = x.shape[1] // 2
    lo = pltpu.bitcast(x[:, :n].astype(BF16).astype(F32), jnp.int32)
    hi = pltpu.bitcast(x[:, n:].astype(BF16).astype(F32), jnp.int32)
    return hi | ((lo >> 16) & 0xFFFF)


def _unpack_bf16_pairs(w):
    lo = pltpu.bitcast(w << 16, F32)
    hi = pltpu.bitcast(w & jnp.int32(-65536), F32)
    return jnp.concatenate([lo, hi], axis=1)


def _router_kernel(x_ref, g_ref, mod_ref, rw_ref, rb_ref, tri_ref, h_ref, gate_ref, rank_ref, cnt_ref,
                   run_ref, *, d, n_exp):
    mod = mod_ref[...]
    h = _norm_mod(x_ref[...], g_ref[...], mod[:, :d], mod[:, d:2 * d])
    h_ref[...] = _pack_bf16_pairs(h)
    tm = h.shape[0]
    hh, hl = _split_bf16(h)
    rw = rw_ref[...]
    rh, rl = _split_bf16(rw)
    logits = _dot_nt(rh, hh) + (_dot_nt(rh, hl) + _dot_nt(rl, hh))
    scores = jax.nn.sigmoid(logits)
    sel = scores + rb_ref[...]
    gsz = n_exp // N_EXPERT_GROUPS
    neg = -jnp.inf
    shape3 = (N_EXPERT_GROUPS, gsz, tm)
    sel3 = sel.reshape(shape3)
    sub = lax.broadcasted_iota(jnp.int32, shape3, 1)
    m1 = jnp.max(sel3, axis=1, keepdims=True)
    f1 = jnp.min(jnp.where(sel3 == m1, sub, gsz), axis=1, keepdims=True)
    m2 = jnp.max(jnp.where(sub == f1, neg, sel3), axis=1, keepdims=True)
    gscore = jnp.broadcast_to(m1 + m2, shape3).reshape(n_exp, tm)
    eidx = lax.broadcasted_iota(jnp.int32, (n_exp, tm), 0)
    gidx = eidx // gsz
    keep = jnp.zeros((n_exp, tm), F32)
    work = gscore
    for _ in range(TOPK_GROUPS):
        _, first = _first_max(work, gidx, N_EXPERT_GROUPS)
        hit = gidx == first
        keep = jnp.where(hit, 1.0, keep)
        work = jnp.where(hit, neg, work)
    work = jnp.where(keep > 0.0, sel, neg)
    chosen = jnp.zeros((n_exp, tm), F32)
    for _ in range(TOP_K):
        _, first = _first_max(work, eidx, n_exp)
        hit = eidx == first
        chosen = jnp.where(hit, 1.0, chosen)
        work = jnp.where(hit, neg, work)
    picked = chosen * scores
    gate_ref[...] = picked / jnp.sum(picked, axis=0, keepdims=True) * ROUTED_SCALE

    @pl.when(pl.program_id(0) == 0)
    def _():
        run_ref[...] = jnp.zeros(run_ref.shape, F32)

    before = _dot(chosen.astype(BF16), tri_ref[...])
    run = run_ref[...]
    rank_ref[...] = jnp.where(chosen > 0.0, before + run[:, :1], -1.0)
    run = run + jnp.broadcast_to(jnp.sum(chosen, axis=1, keepdims=True), run.shape)
    run_ref[...] = run
    cnt_ref[...] = run


MOE_TILE = 512
ROUTER_TILE = 1024


def _ffn_router(x2, g, mod, layer, router_w, router_bias, seq, tile0, ntiles):
    d = x2.shape[1]
    n_exp = router_w.shape[1]
    tm = ROUTER_TILE
    t = ntiles * tm
    tiles_per_seq = seq // tm
    tri = (np.arange(tm)[:, None] < np.arange(tm)[None, :]).astype(np.float32)
    return pl.pallas_call(
        functools.partial(_router_kernel, d=d, n_exp=n_exp),
        grid=(ntiles,),
        in_specs=[pl.BlockSpec((tm, d), lambda i: (i + tile0, 0)),
                  pl.BlockSpec((1, d), lambda i: (0, 0)),
                  pl.BlockSpec((None, None, 1, 3 * d), lambda i: (layer, (i + tile0) // tiles_per_seq, 0, 0)),
                  pl.BlockSpec((n_exp, d), lambda i: (0, 0)),
                  pl.BlockSpec((n_exp, 1), lambda i: (0, 0)),
                  pl.BlockSpec((tm, tm), lambda i: (0, 0))],
        out_specs=[pl.BlockSpec((tm, d // 2), lambda i: (i, 0)),
                   pl.BlockSpec((n_exp, tm), lambda i: (0, i)),
                   pl.BlockSpec((n_exp, tm), lambda i: (0, i)),
                   pl.BlockSpec((n_exp, 128), lambda i: (0, 0))],
        out_shape=[jax.ShapeDtypeStruct((t, d // 2), jnp.int32),
                   jax.ShapeDtypeStruct((n_exp, t), F32),
                   jax.ShapeDtypeStruct((n_exp, t), F32),
                   jax.ShapeDtypeStruct((n_exp, 128), F32)],
        scratch_shapes=[pltpu.VMEM((n_exp, 128), F32)],
        compiler_params=_cparams(("arbitrary",)),
        name="ffn_router",
    )(x2, g.reshape(1, d), mod, router_w.T, router_bias.reshape(n_exp, 1), jnp.asarray(tri, BF16))


def _assign_kernel(rank_ref, gate_ref, start_ref, pos_ref, w_ref, *, n_exp):
    rank = rank_ref[...]
    gates = gate_ref[...]
    tm = rank.shape[1]
    slot = rank + start_ref[...]
    eidx = lax.broadcasted_iota(jnp.int32, (n_exp, tm), 0).astype(F32)
    alive = jnp.where(rank >= 0.0, eidx, float(n_exp))
    kidx = lax.broadcasted_iota(jnp.int32, (TOP_K, tm), 0)
    pos = jnp.zeros((TOP_K, tm), F32)
    wts = jnp.zeros((TOP_K, tm), F32)
    for k in range(TOP_K):
        first = jnp.min(alive, axis=0, keepdims=True)
        hit = alive == first
        pos_k = jnp.sum(jnp.where(hit, slot, 0.0), axis=0, keepdims=True)
        w_k = jnp.sum(jnp.where(hit, gates, 0.0), axis=0, keepdims=True)
        pos = jnp.where(kidx == k, pos_k, pos)
        wts = jnp.where(kidx == k, w_k, wts)
        alive = jnp.where(hit, float(n_exp), alive)
    pos_ref[...] = pos.astype(jnp.int32)
    w_ref[...] = wts


def _assign_slots(ranks, gates, start):
    n_exp, t = ranks.shape
    tm = 512
    return pl.pallas_call(
        functools.partial(_assign_kernel, n_exp=n_exp),
        grid=(t // tm,),
        in_specs=[pl.BlockSpec((n_exp, tm), lambda i: (0, i)),
                  pl.BlockSpec((n_exp, tm), lambda i: (0, i)),
                  pl.BlockSpec((n_exp, 1), lambda i: (0, 0))],
        out_specs=[pl.BlockSpec((TOP_K, tm), lambda i: (0, i)),
                   pl.BlockSpec((TOP_K, tm), lambda i: (0, i))],
        out_shape=[jax.ShapeDtypeStruct((TOP_K, t), jnp.int32), jax.ShapeDtypeStruct((TOP_K, t), F32)],
        compiler_params=_cparams(("arbitrary",)),
        name="moe_assign",
    )(ranks, gates, start)


SC_CORES = 2
SC_SUBCORES = 16
SC_WINDOW = 128


def _sc_mesh():
    return plsc.VectorSubcoreMesh(core_axis_name="c", subcore_axis_name="s",
                                  num_cores=SC_CORES, num_subcores=SC_SUBCORES)


def _sc_scatter_rows(rows, pos, n_out):
    t, width = rows.shape
    nk = pos.shape[0]
    win = SC_WINDOW
    n_win = t // win // (SC_CORES * SC_SUBCORES)
    pos_w = pos.reshape(nk, t // win, win).transpose(1, 0, 2)

    def body(rows_hbm, pos_hbm, out_hbm, idx_v, buf_v, sem):
        wid = lax.axis_index("s") * SC_CORES + lax.axis_index("c")

        @pl.loop(0, n_win)
        def _(j):
            w = wid * n_win + j
            pltpu.sync_copy(rows_hbm.at[pl.ds(w * win, win)], buf_v)
            pltpu.sync_copy(pos_hbm.at[w], idx_v)
            copies = [pltpu.make_async_copy(buf_v, out_hbm.at[idx_v.at[k]], sem) for k in range(nk)]
            for cp in copies:
                cp.start()
            for cp in copies:
                cp.wait()

    return pl.kernel(
        body, out_type=jax.ShapeDtypeStruct((n_out, width), jnp.int32), mesh=_sc_mesh(),
        scratch_types=[pltpu.VMEM((nk, win), jnp.int32), pltpu.VMEM((win, width), jnp.int32),
                       pltpu.SemaphoreType.DMA],
        name="sc_scatter_rows",
    )(rows, pos_w)


SC_GATHER_WINDOW = 64


def _sc_gather_rows(table, pos_flat):
    m = pos_flat.shape[0]
    width = table.shape[1]
    win = SC_GATHER_WINDOW
    workers = SC_CORES * SC_SUBCORES
    n_win = m // win // workers
    pos_w = pos_flat.reshape(workers, n_win, win)

    def body(table_hbm, pos_hbm, out_hbm, idx_v, buf_v, sem_g, sem_w):
        wid = lax.axis_index("s") * SC_CORES + lax.axis_index("c")
        base = wid * (n_win * win)
        pltpu.sync_copy(pos_hbm.at[wid], idx_v)

        def gather(j, b):
            return pltpu.make_async_copy(table_hbm.at[idx_v.at[j]], buf_v.at[b], sem_g.at[b])

        def write(j, b):
            return pltpu.make_async_copy(buf_v.at[b], out_hbm.at[pl.ds(base + j * win, win)], sem_w.at[b])

        gather(0, 0).start()

        @pl.loop(0, n_win, step=2)
        def _(j0):
            for b in range(2):
                j = j0 + b
                gather(j, b).wait()
                write(j, b).start()

                @pl.when(j + 1 < n_win)
                def _():
                    @pl.when(j >= 1)
                    def _():
                        write(j - 1, 1 - b).wait()
                    gather(j + 1, 1 - b).start()

        write(n_win - 2, 0).wait()
        write(n_win - 1, 1).wait()

    return pl.kernel(
        body, out_type=jax.ShapeDtypeStruct((m, width), jnp.int32), mesh=_sc_mesh(),
        scratch_types=[pltpu.VMEM((n_win, win), jnp.int32), pltpu.VMEM((2, win, width), jnp.int32),
                       pltpu.SemaphoreType.DMA((2,)), pltpu.SemaphoreType.DMA((2,))],
        name="sc_gather_rows",
    )(table, pos_w)


MOE_BLOCK = 1024


def _silu_mul(a, b):
    return (a * jax.nn.sigmoid(a)) * b


def _expert_ffn_kernel(be_ref, nv_ref, xs_ref, wg_ref, wu_ref, wd_ref, ys_ref, wg_bf, wu_bf, wd_bf):
    i = pl.program_id(0)
    fresh = jnp.logical_or(i == 0, be_ref[i] != be_ref[jnp.maximum(i - 1, 0)])

    @pl.when(fresh)
    def _():
        wg_bf[...] = wg_ref[...].astype(BF16)
        wu_bf[...] = wu_ref[...].astype(BF16)
        wd_bf[...] = wd_ref[...].astype(BF16)

    nv = nv_ref[i]

    @pl.when(nv > 0)
    def _():
        x = _unpack_bf16_pairs(xs_ref[...])
        row = lax.broadcasted_iota(jnp.int32, (x.shape[0], 1), 0)
        x = jnp.where(row < nv, x, 0.0).astype(BF16)
        mid = _silu_mul(_dot(x, wg_bf[...]), _dot(x, wu_bf[...]))
        ys_ref[...] = _pack_bf16_pairs(_dot(mid.astype(BF16), wd_bf[...]))

    @pl.when(nv <= 0)
    def _():
        ys_ref[...] = jnp.zeros(ys_ref.shape, ys_ref.dtype)


def _expert_ffn(xs, block_expert, block_rows, layer, w_gate, w_up, w_down):
    n_rows, half = xs.shape
    _, n_exp, d, de = w_gate.shape
    nb = n_rows // MOE_BLOCK
    wmap = lambda i, be, nv: (layer, be[i], 0, 0)
    grid_spec = pltpu.PrefetchScalarGridSpec(
        num_scalar_prefetch=2,
        grid=(nb,),
        in_specs=[pl.BlockSpec((MOE_BLOCK, half), lambda i, be, nv: (jnp.where(nv[i] > 0, i, 0), 0)),
                  pl.BlockSpec((None, None, d, de), wmap),
                  pl.BlockSpec((None, None, d, de), wmap),
                  pl.BlockSpec((None, None, de, d), wmap)],
        out_specs=pl.BlockSpec((MOE_BLOCK, half), lambda i, be, nv: (i, 0)),
        scratch_shapes=[pltpu.VMEM((d, de), BF16), pltpu.VMEM((d, de), BF16), pltpu.VMEM((de, d), BF16)],
    )
    return pl.pallas_call(
        _expert_ffn_kernel,
        grid_spec=grid_spec,
        out_shape=jax.ShapeDtypeStruct((n_rows, half), jnp.int32),
        compiler_params=_cparams(("arbitrary",), VMEM_LIMIT),
        name="moe_expert_ffn",
    )(block_expert, block_rows, xs, w_gate, w_up, w_down)


def _combine_kernel(yt_ref, wt_ref, hp_ref, x_ref, mod_ref, sg_ref, su_ref, sd_ref, *rest, d):
    o_ref = rest[-1]
    h = _unpack_bf16_pairs(hp_ref[...]).astype(BF16)
    acc = _dot(_silu_mul(_dot(h, sg_ref[...]), _dot(h, su_ref[...])).astype(BF16), sd_ref[...])
    w = wt_ref[...]
    for k in range(TOP_K):
        acc = acc + w[:, k:k + 1] * _unpack_bf16_pairs(yt_ref[k])
    o_ref[...] = x_ref[...] + mod_ref[...][:, 2 * d:] * acc


def _combine(y_tok, w_tok, h_packed, x2, out_prev, mod, layer, sh_gate, sh_up, sh_down, seq, tile0):
    t_all, d = x2.shape
    ds = sh_gate.shape[1]
    tm = MOE_TILE
    ntiles = h_packed.shape[0] // tm
    tiles_per_seq = seq // tm
    row = lambda i: (i, 0)
    off = lambda i: (i + tile0, 0)
    const = lambda i: (0, 0)
    in_specs = [pl.BlockSpec((TOP_K, tm, d // 2), lambda i: (0, i, 0)),
                pl.BlockSpec((tm, TOP_K), row),
                pl.BlockSpec((tm, d // 2), row),
                pl.BlockSpec((tm, d), off),
                pl.BlockSpec((None, None, 1, 3 * d), lambda i: (layer, (i + tile0) // tiles_per_seq, 0, 0)),
                pl.BlockSpec((d, ds), const),
                pl.BlockSpec((d, ds), const),
                pl.BlockSpec((ds, d), const)]
    args = [y_tok, w_tok, h_packed, x2, mod, sh_gate.astype(BF16), sh_up.astype(BF16), sh_down.astype(BF16)]
    aliases = {}
    if out_prev is not None:
        in_specs.append(pl.BlockSpec(memory_space=pl.ANY))
        args.append(out_prev)
        aliases = {len(args) - 1: 0}
    return pl.pallas_call(
        functools.partial(_combine_kernel, d=d),
        grid=(ntiles,),
        in_specs=in_specs,
        out_specs=pl.BlockSpec((tm, d), off),
        out_shape=jax.ShapeDtypeStruct((t_all, d), F32),
        input_output_aliases=aliases,
        compiler_params=_cparams(("arbitrary",), VMEM_LIMIT),
        name="moe_combine",
    )(*args)


def _moe_group(x2, out_prev, tile0, h_packed, gates, ranks, counts, mod, layer, w_gate, w_up, w_down,
               sh_gate, sh_up, sh_down, seq):
    t, half = h_packed.shape
    d = 2 * half
    n_exp = w_gate.shape[1]
    cnt = counts[:, 0].astype(jnp.int32)
    padded = (cnt + MOE_BLOCK - 1) // MOE_BLOCK * MOE_BLOCK
    eidx = jnp.arange(n_exp, dtype=jnp.int32)
    seg_end = jnp.sum(jnp.where(eidx[None, :] <= eidx[:, None], padded[None, :], 0), axis=1)
    seg_start = seg_end - padded
    nb = t * TOP_K // MOE_BLOCK + n_exp
    first_row = jnp.arange(nb, dtype=jnp.int32) * MOE_BLOCK
    block_expert = jnp.sum((seg_end[None, :] <= first_row[:, None]).astype(jnp.int32), axis=1)
    block_expert = jnp.minimum(block_expert, n_exp - 1)
    last_row = jnp.sum(jnp.where(block_expert[:, None] == eidx[None, :], (seg_start + cnt)[None, :], 0), axis=1)
    block_rows = jnp.clip(last_row - first_row, 0, MOE_BLOCK).astype(jnp.int32)
    pos, w = _assign_slots(ranks, gates, seg_start.astype(F32).reshape(n_exp, 1))
    xs = _sc_scatter_rows(h_packed, pos, nb * MOE_BLOCK)
    ys = _expert_ffn(xs, block_expert, block_rows, layer, w_gate, w_up, w_down)
    y_tok = _sc_gather_rows(ys, pos.reshape(TOP_K * t)).reshape(TOP_K, t, d // 2)
    return _combine(y_tok, w.T, h_packed, x2, out_prev, mod, layer, sh_gate, sh_up, sh_down, seq, tile0)


MOE_GROUPS = 1


def _moe_layer(x2, g, mod, layer, router_w, router_bias, w_gate, w_up, w_down, sh_gate, sh_up, sh_down, seq):
    ntiles = x2.shape[0] // MOE_TILE // MOE_GROUPS
    rtiles = ntiles * MOE_TILE // ROUTER_TILE
    routed = [_ffn_router(x2, g, mod, layer, router_w, router_bias, seq, grp * rtiles, rtiles)
              for grp in range(MOE_GROUPS)]
    out = None
    for grp in range(MOE_GROUPS):
        out = _moe_group(x2, out, grp * ntiles, *routed[grp], mod, layer, w_gate, w_up, w_down,
                         sh_gate, sh_up, sh_down, seq)
    return out


def kernel(x, c, positions, mix_norm_g, mix_mod_w, mix_mod_b, ffn_norm_g, ffn_mod_w, ffn_mod_b,
           hyb_w_in, hyb_w_out, s5_lambda_re, s5_lambda_im, s5_log_dt, s5_b_re, s5_b_im,
           s5_c_re, s5_c_im, s5_d, s5_glu_w, s5_glu_b, attn_q_norm_g, attn_k_norm_g,
           pool_w, pool_scale, router_w, router_bias, exp_w_gate, exp_w_up, exp_w_down,
           sh_w_gate, sh_w_up, sh_w_down):
    bsz, seq, d = x.shape
    t = bsz * seq
    depth = mix_norm_g.shape[0]
    x2 = x.reshape(t, d)
    mix_mod = _mod_vectors(c, mix_mod_w, mix_mod_b).reshape(depth, bsz, 1, 3 * d)
    ffn_mod = _mod_vectors(c, ffn_mod_w, ffn_mod_b).reshape(depth, bsz, 1, 3 * d)
    for i in range(depth):
        j = i // 2
        if i % 2 == 0:
            u, q_hm, qi_hm, k, v, ki, wi = _in_projection(
                x2, mix_norm_g[i], mix_mod, i, hyb_w_in[j], positions, attn_q_norm_g[j], attn_k_norm_g[j], bsz, seq)
            weights = _s5_weights(s5_lambda_re[j], s5_lambda_im[j], s5_log_dt[j], s5_b_re[j], s5_b_im[j],
                                  s5_c_re[j], s5_c_im[j], s5_d[j], s5_glu_w[j], s5_glu_b[j])
            y_ssm = _s5_mixer(u, weights, bsz, seq)
            y_att = _dsa_attention(q_hm, qi_hm, wi, k, v, ki, bsz, seq)
            x2 = _out_projection(x2, y_ssm, y_att, hyb_w_out[j], mix_mod, i, seq)
        else:
            x2 = _pool_layer(x2, mix_norm_g[i], mix_mod, i, pool_w[j], pool_scale[j], seq)
        x2 = _moe_layer(x2, ffn_norm_g[i], ffn_mod, i, router_w[i], router_bias[i], exp_w_gate, exp_w_up, exp_w_down,
                        sh_w_gate[i], sh_w_up[i], sh_w_down[i], seq)
    return x2.reshape(bsz, seq, d)
```

```python
import functools

import numpy as np
import jax
import jax.numpy as jnp
from jax import lax
from jax.experimental import pallas as pl
from jax.experimental.pallas import tpu as pltpu
from jax.experimental.pallas import tpu_sc as plsc

F32 = jnp.float32
BF16 = jnp.bfloat16

EPS = 1e-6
S5_GROUP = 16
S5_STATE = 64
HEAD_DIM = 64
ATTN_HEADS = 8
IDX_HEADS = 8
ROPE_HALF = 8
ROPE_THETA = 500000.0
TOPK_MAX = 256
POOL_WINDOWS = (2, 4, 8, 16)
POOL_HALO = 16
N_EXPERT_GROUPS = 8
TOPK_GROUPS = 4
TOP_K = 8
ROUTED_SCALE = 2.5
S5_CHUNK = 16
NEG_BIG = -1e30
VMEM_LIMIT = 56 * 1024 * 1024


def _cparams(sem, vmem=None):
    return pltpu.CompilerParams(dimension_semantics=sem, vmem_limit_bytes=vmem)


def _dot(a, b):
    return jnp.dot(a, b, preferred_element_type=F32)


def _dot_nt(a, b):
    return lax.dot_general(a, b, (((1,), (1,)), ((), ())), preferred_element_type=F32)


def _split_bf16(a):
    hi = a.astype(BF16)
    lo = (a - hi.astype(F32)).astype(BF16)
    return hi, lo


def _norm_mod(x, g, shift, scale):
    y = x * lax.rsqrt(jnp.mean(x * x, axis=-1, keepdims=True) + EPS)
    return (y * g) * (1.0 + scale) + shift


def _mod_kernel(ct_ref, w_ref, b_ref, o_ref):
    ct = ct_ref[...]
    cs = ct * jax.nn.sigmoid(ct)
    w = w_ref[...]
    rows = [jnp.sum(w * cs[:, b:b + 1], axis=0, keepdims=True) for b in range(ct.shape[1])]
    o_ref[...] = jnp.concatenate(rows, axis=0) + b_ref[...]


def _mod_vectors(c, w, b):
    nl, d, n3 = w.shape
    bsz = c.shape[0]
    tn = 512
    return pl.pallas_call(
        _mod_kernel,
        grid=(nl, n3 // tn),
        in_specs=[pl.BlockSpec((d, bsz), lambda l, j: (0, 0)),
                  pl.BlockSpec((None, d, tn), lambda l, j: (l, 0, j)),
                  pl.BlockSpec((None, 1, tn), lambda l, j: (l, 0, j))],
        out_specs=pl.BlockSpec((None, bsz, tn), lambda l, j: (l, 0, j)),
        out_shape=jax.ShapeDtypeStruct((nl, bsz, n3), F32),
        compiler_params=_cparams(("arbitrary", "arbitrary")),
        name="mod_vectors",
    )(c.T, w, b.reshape(nl, 1, n3))


def _inproj_kernel(x_ref, g_ref, mod_ref, w_ref, pos_ref, inv_ref, qg_ref, kg_ref,
                   u_ref, q_ref, qi_ref, k_ref, v_ref, ki_ref, wi_ref, u_scr, *, d):
    x = x_ref[...]
    mod = mod_ref[...]
    h = _norm_mod(x, g_ref[...], mod[:, :d], mod[:, d:2 * d])
    proj = _dot(h.astype(BF16), w_ref[...])
    tm = x.shape[0]
    width = ATTN_HEADS * HEAD_DIM

    ang = inv_ref[...] * pos_ref[...].astype(F32)
    cos, sin = jnp.cos(ang), jnp.sin(ang)

    def head_t(xt, gain, scale):
        if scale is not None:
            r = lax.rsqrt(jnp.mean(xt * xt, axis=0, keepdims=True) + EPS) * scale
            xt = xt * gain
        x1, x2 = xt[:ROPE_HALF], xt[ROPE_HALF:2 * ROPE_HALF]
        out = jnp.concatenate([x1 * cos - x2 * sin, x1 * sin + x2 * cos, xt[2 * ROPE_HALF:]], axis=0)
        return out if scale is None else out * r

    for slab in range(width // 128):
        u_scr[slab] = proj[:, slab * 128:(slab + 1) * 128]
        for j in range(S5_CHUNK):
            piece = u_scr[slab, pl.ds(j, tm // S5_CHUNK, stride=S5_CHUNK), :]
            col = (slab * S5_CHUNK + j) * 128
            u_ref[:, col:col + 128] = piece.astype(u_ref.dtype)
    q = proj[:, width:2 * width]
    qi = proj[:, 2 * width:3 * width]
    small = proj[:, 3 * width:3 * width + 256]
    qg = qg_ref[...]
    for pair in range(ATTN_HEADS // 2):
        sl = slice(pair * 128, (pair + 1) * 128)
        qt = q[:, sl].T
        qit = qi[:, sl].T
        for half in range(2):
            hs = slice(half * HEAD_DIM, (half + 1) * HEAD_DIM)
            q_ref[2 * pair + half] = head_t(qt[hs], qg, HEAD_DIM ** -0.5).astype(q_ref.dtype)
            qi_ref[2 * pair + half] = head_t(qit[hs], None, None).astype(qi_ref.dtype)
    kvt = small[:, :128].T
    kt = head_t(kvt[:HEAD_DIM], kg_ref[...], 1.0)
    kv = jnp.concatenate([kt, kvt[HEAD_DIM:]], axis=0).T
    k_ref[...] = kv[:, :HEAD_DIM].astype(k_ref.dtype)
    v_ref[...] = kv[:, HEAD_DIM:].astype(v_ref.dtype)
    kiwt = small[:, 128:256].T
    kit = jnp.concatenate([head_t(kiwt[:HEAD_DIM], None, None), kiwt[HEAD_DIM:]], axis=0).T
    ki_ref[...] = kit[:, :HEAD_DIM].astype(ki_ref.dtype)
    wscale = (IDX_HEADS ** -0.5) * (HEAD_DIM ** -0.5)
    wi_ref[...] = kiwt[HEAD_DIM:HEAD_DIM + IDX_HEADS] * wscale


def _in_projection(x2, g, mod, layer, w_in, positions, q_norm_g, k_norm_g, bsz, seq):
    t, d = x2.shape
    tm = 512
    width = ATTN_HEADS * HEAD_DIM
    cuts = np.cumsum([width, width, HEAD_DIM, HEAD_DIM, width, HEAD_DIM, IDX_HEADS])
    u_w, q_w, k_w, v_w, qi_w, ki_w, wi_w = jnp.split(w_in, cuts[:-1].tolist(), axis=1)
    pad = jnp.zeros((d, 128 - HEAD_DIM - IDX_HEADS), w_in.dtype)
    w = jnp.concatenate([u_w, q_w, qi_w, k_w, v_w, ki_w, wi_w, pad], axis=1).astype(BF16)
    nw = w.shape[1]
    inv = np.power(ROPE_THETA, -2.0 * np.arange(ROPE_HALF) / (2 * ROPE_HALF))
    inv = jnp.asarray(inv.reshape(ROPE_HALF, 1), F32)
    tiles_per_seq = seq // tm
    row = lambda i: (i, 0)
    const = lambda i: (0, 0)
    hm = lambda i: (i // tiles_per_seq, 0, 0, i % tiles_per_seq)
    outs = pl.pallas_call(
        functools.partial(_inproj_kernel, d=d),
        grid=(t // tm,),
        in_specs=[pl.BlockSpec((tm, d), row),
                  pl.BlockSpec((1, d), const),
                  pl.BlockSpec((None, None, 1, 3 * d), lambda i: (layer, i // tiles_per_seq, 0, 0)),
                  pl.BlockSpec((d, nw), const),
                  pl.BlockSpec((None, 1, tm), lambda i: (i, 0, 0)),
                  pl.BlockSpec((ROPE_HALF, 1), const),
                  pl.BlockSpec((HEAD_DIM, 1), const),
                  pl.BlockSpec((HEAD_DIM, 1), const)],
        out_specs=[pl.BlockSpec((tm // S5_CHUNK, width * S5_CHUNK), row),
                   pl.BlockSpec((None, ATTN_HEADS, HEAD_DIM, tm), hm),
                   pl.BlockSpec((None, IDX_HEADS, HEAD_DIM, tm), hm),
                   pl.BlockSpec((tm, HEAD_DIM), row),
                   pl.BlockSpec((tm, HEAD_DIM), row),
                   pl.BlockSpec((tm, HEAD_DIM), row),
                   pl.BlockSpec((IDX_HEADS, tm), lambda i: (0, i))],
        out_shape=[jax.ShapeDtypeStruct((t // S5_CHUNK, width * S5_CHUNK), BF16),
                   jax.ShapeDtypeStruct((bsz, ATTN_HEADS, HEAD_DIM, seq), BF16),
                   jax.ShapeDtypeStruct((bsz, IDX_HEADS, HEAD_DIM, seq), BF16),
                   jax.ShapeDtypeStruct((t, HEAD_DIM), BF16),
                   jax.ShapeDtypeStruct((t, HEAD_DIM), BF16),
                   jax.ShapeDtypeStruct((t, HEAD_DIM), BF16),
                   jax.ShapeDtypeStruct((IDX_HEADS, t), F32)],
        scratch_shapes=[pltpu.VMEM((width // 128, tm, 128), F32)],
        compiler_params=_cparams(("arbitrary",), VMEM_LIMIT),
        name="in_projection",
    )(x2, g.reshape(1, d), mod, w, positions.reshape(t // tm, 1, tm), inv,
      q_norm_g.reshape(HEAD_DIM, 1), k_norm_g.reshape(HEAD_DIM, 1))
    return outs


def _s5_weights(lam_re, lam_im, log_dt, b_re, b_im, c_re, c_im, d_skip, glu_w, glu_b):
    L = S5_CHUNK
    g, p = lam_re.shape
    hch = S5_GROUP
    dt = jnp.exp(log_dt)[:, None]
    lr, li = lam_re, lam_im
    tau = jnp.arange(L + 1, dtype=F32)[:, None, None]
    mag = jnp.exp(lr * dt * tau)
    pw_r, pw_i = mag * jnp.cos(li * dt * tau), mag * jnp.sin(li * dt * tau)
    ar, ai = pw_r[1], pw_i[1]
    nr, ni = ar - 1.0, ai
    den = lr * lr + li * li
    cr, ci = (nr * lr + ni * li) / den, (ni * lr - nr * li) / den
    bb_r = cr[..., None] * b_re - ci[..., None] * b_im
    bb_i = cr[..., None] * b_im + ci[..., None] * b_re
    cl_r = c_re[None] * pw_r[:L, :, None, :] - c_im[None] * pw_i[:L, :, None, :]
    cl_i = c_re[None] * pw_i[:L, :, None, :] + c_im[None] * pw_r[:L, :, None, :]
    taps = (jnp.einsum('tghp,gpk->tghk', cl_r, bb_r) - jnp.einsum('tghp,gpk->tghk', cl_i, bb_i))
    ns = g // 8
    eye = jnp.eye(8, dtype=F32)
    kd = jnp.einsum('tsgoh,gf->tsghfo', taps.reshape(L, ns, 8, hch, hch), eye).reshape(L, ns, 128, 128)
    zero = jnp.zeros_like(kd[0])
    k2 = jnp.stack([jnp.concatenate([jnp.concatenate([kd[2 * dd], kd[2 * dd + 1]], axis=-1),
                                     jnp.concatenate([kd[2 * dd - 1] if dd else zero, kd[2 * dd]], axis=-1)], axis=-2)
                    for dd in range(L // 2)], axis=1).astype(BF16)
    ii = jnp.arange(L)
    rev_r, rev_i = pw_r[L - 1 - ii], pw_i[L - 1 - ii]
    z_r = rev_r[..., None] * bb_r[None] - rev_i[..., None] * bb_i[None]
    z_i = rev_r[..., None] * bb_i[None] + rev_i[..., None] * bb_r[None]
    zc = jnp.concatenate([z_r, z_i], axis=2)
    zc = jnp.transpose(zc.reshape(L, ns, 8, 2 * p, hch), (1, 0, 2, 4, 3)).reshape(ns, L * 128, 2 * p)
    row_group = (jnp.arange(L * 128) // hch) % 8
    state_group = jnp.arange(8 * 2 * p) // (2 * p)
    w_z = jnp.where((row_group[:, None] == state_group[None, :])[None], jnp.concatenate([zc] * 8, axis=-1), 0.0)
    w_z = w_z.reshape(ns, L // 2, 2 * 128, 8 * 2 * p).astype(BF16)
    co_r = c_re[None] * pw_r[1:, :, None, :] - c_im[None] * pw_i[1:, :, None, :]
    co_i = c_re[None] * pw_i[1:, :, None, :] + c_im[None] * pw_r[1:, :, None, :]
    cc = jnp.concatenate([co_r, -co_i], axis=-1)
    cc = jnp.transpose(cc.reshape(L, ns, 8, hch, 2 * p), (1, 4, 0, 2, 3)).reshape(ns, 1, 2 * p, L * 128)
    col_group = (jnp.arange(L * 128) // hch) % 8
    w_c = jnp.where((jnp.arange(8)[:, None] == col_group[None, :])[None, :, None, :], cc, 0.0)
    w_c = w_c.reshape(ns, 8 * 2 * p, L * 128)
    w_c = w_c.astype(BF16)
    al_r, al_i = pw_r[L], pw_i[L]
    dec_a = jnp.concatenate([al_r, al_r], axis=-1).reshape(ns, 1, 8 * 2 * p)
    dec_b = jnp.concatenate([-al_i, al_i], axis=-1).reshape(ns, 1, 8 * 2 * p)
    d_t = jnp.tile(d_skip.reshape(ns, 1, 128), (1, 1, L))
    gl = jnp.einsum('sghk,gf->sghfk', glu_w.reshape(ns, 8, hch, hch), eye).reshape(ns, 128, 128)
    glu2 = jnp.einsum('ab,shk->sahbk', jnp.eye(2, dtype=F32), gl).reshape(ns, 256, 256).astype(BF16)
    glu_bt = jnp.tile(glu_b.reshape(ns, 1, 128), (1, 1, L))
    return k2, w_z, w_c, dec_a, dec_b, d_t, glu2, glu_bt


def _s5_kernel(u_ref, k2_ref, wz_ref, wc_ref, da_ref, db_ref, d_ref, g2_ref, gb_ref, o_ref,
               zp_ref, zq_ref, sp_ref, *, nchunk):
    nblk = S5_CHUNK // 2
    ub = [u_ref[:, j * 256:(j + 1) * 256] for j in range(nblk)]
    z = _dot(ub[0], wz_ref[0])
    for j in range(1, nblk):
        z = z + _dot(ub[j], wz_ref[j])
    zp_ref[...] = z
    half = S5_STATE
    zq_ref[...] = jnp.concatenate([pltpu.roll(z[:, s * 2 * half:(s + 1) * 2 * half], half, 1)
                                   for s in range(z.shape[1] // (2 * half))], axis=1)
    da = da_ref[...]
    db = db_ref[...]
    dbq = -db

    def step(c, carry):
        sp, sq = carry
        sp_ref[pl.ds(c, 1), :] = sp
        sp_new = da * sp + db * sq + zp_ref[pl.ds(c, 1), :]
        sq_new = da * sq + dbq * sp + zq_ref[pl.ds(c, 1), :]
        return sp_new, sq_new

    zero = jnp.zeros((1, z.shape[1]), F32)
    lax.fori_loop(0, nchunk, step, (zero, zero))
    sh, sl = _split_bf16(sp_ref[...])
    wc = wc_ref[...]
    y_inter = _dot(sh, wc) + _dot(sl, wc)
    for i in range(nblk):
        cs = slice(i * 256, (i + 1) * 256)
        acc = y_inter[:, cs] + d_ref[:, cs] * ub[i].astype(F32)
        for j in range(i + 1):
            acc = acc + _dot(ub[j], k2_ref[i - j])
        y = jax.nn.gelu(acc)
        y = y * jax.nn.sigmoid(_dot(y.astype(BF16), g2_ref[...]) + gb_ref[:, cs])
        for a in range(2):
            o_ref[pl.ds(2 * i + a, nchunk, stride=S5_CHUNK), :] = y[:, a * 128:(a + 1) * 128]


def _s5_mixer(u2, weights, bsz, seq):
    k2, w_z, w_c, dec_a, dec_b, d_t, glu2, glu_bt = weights
    ns = k2.shape[0]
    nchunk = seq // S5_CHUNK
    cols = S5_CHUNK * 128
    st = w_z.shape[-1]
    slab3 = lambda b, s: (s, 0, 0)
    slab4 = lambda b, s: (s, 0, 0, 0)
    return pl.pallas_call(
        functools.partial(_s5_kernel, nchunk=nchunk),
        grid=(bsz, ns),
        in_specs=[pl.BlockSpec((nchunk, cols), lambda b, s: (b, s)),
                  pl.BlockSpec((None,) + k2.shape[1:], slab4),
                  pl.BlockSpec((None,) + w_z.shape[1:], slab4),
                  pl.BlockSpec((None, st, cols), slab3),
                  pl.BlockSpec((None, 1, st), slab3),
                  pl.BlockSpec((None, 1, st), slab3),
                  pl.BlockSpec((None, 1, cols), slab3),
                  pl.BlockSpec((None, 256, 256), slab3),
                  pl.BlockSpec((None, 1, cols), slab3)],
        out_specs=pl.BlockSpec((seq, 128), lambda b, s: (b, s)),
        out_shape=jax.ShapeDtypeStruct((bsz * seq, ns * 128), F32),
        scratch_shapes=[pltpu.VMEM((nchunk, st), F32), pltpu.VMEM((nchunk, st), F32), pltpu.VMEM((nchunk, st), F32)],
        compiler_params=_cparams(("arbitrary", "arbitrary"), VMEM_LIMIT),
        name="s5_mixer",
    )(u2, k2, w_z, w_c, dec_a, dec_b, d_t, glu2, glu_bt)


def _dsa_kernel(qt_ref, qit_ref, wt_ref, k_ref, vt_ref, ki_ref, o_ref,
                sc_ref, s_ref, m_ref, l_ref, acc_ref, *, tq, ck, k_sel, idx_bits, max_iter):
    i = pl.program_id(1)
    nheads = qt_ref.shape[0]
    n_ck = (i * tq + tq + ck - 1) // ck
    t_pos = i * tq + lax.broadcasted_iota(jnp.int32, (1, tq), 1)
    krow = lax.broadcasted_iota(jnp.int32, (ck, 1), 0)
    kf = float(k_sel)
    inf = jnp.inf

    fr = 32
    sub = min(32768 // tq, ck)

    def fold(x, op):
        return op(x.reshape(ck // fr, fr, tq), axis=0)

    wt = wt_ref[...]

    def score_chunk(c, carry):
        lo8, hi8 = carry
        for part in range(ck // sub):
            off = pl.multiple_of(c * ck + part * sub, sub)
            kic = ki_ref[pl.ds(off, sub), :]
            acc = jnp.zeros((sub, tq), F32)
            for h in range(nheads):
                acc = acc + wt[h:h + 1, :] * jnp.maximum(_dot(kic, qit_ref[h]), 0.0)
            acc = acc + 0.0
            vis = off + krow[:sub] <= t_pos
            sc_ref[c, part * sub:(part + 1) * sub, :] = jnp.where(vis, acc, -inf)
            hi8 = jnp.maximum(hi8, jnp.max(jnp.where(vis, acc, -inf).reshape(sub // fr, fr, tq), axis=0))
            lo8 = jnp.minimum(lo8, jnp.min(jnp.where(vis, acc, inf).reshape(sub // fr, fr, tq), axis=0))
        return lo8, hi8

    lo8, hi8 = lax.fori_loop(0, n_ck, score_chunk, (jnp.full((fr, tq), inf, F32), jnp.full((fr, tq), -inf, F32)))
    lo = jnp.min(lo8, axis=0, keepdims=True)
    hi = jnp.max(hi8, axis=0, keepdims=True)
    n_vis = (t_pos + 1).astype(F32)
    few = n_vis < kf

    def reduce_keys(fn, init, combine, op):
        def body(c, part):
            return combine(part, fold(fn(sc_ref[c], c * ck), op))
        return op(lax.fori_loop(0, n_ck, body, jnp.full((fr, tq), init, F32)), axis=0, keepdims=True)

    def count(pred):
        return reduce_keys(lambda x, off: jnp.where(pred(x, off), 1.0, 0.0), 0.0, jnp.add, jnp.sum)

    def bisect(lo_, hi_, c_lo):
        mid = lo_ + (hi_ - lo_) * 0.5
        cnt = count(lambda x, off: x >= mid)
        up = cnt >= kf
        return jnp.where(up, mid, lo_), jnp.where(up, hi_, mid), jnp.where(up, cnt, c_lo)

    def status(lo_, hi_, c_lo):
        mid = lo_ + (hi_ - lo_) * 0.5
        unresolved = jnp.where(few, 0.0, jnp.where(c_lo == kf, 0.0, 1.0))
        movable = jnp.where(mid > lo_, jnp.where(mid < hi_, 1.0, 0.0), 0.0)
        return jnp.max(unresolved), jnp.max(unresolved * movable)

    def search_body(st):
        it, lo_, hi_, c_lo, _, _ = st
        for _ in range(2):
            lo_, hi_, c_lo = bisect(lo_, hi_, c_lo)
        return (it + 2, lo_, hi_, c_lo) + status(lo_, hi_, c_lo)

    st = lax.while_loop(lambda st: jnp.logical_and(st[0] < max_iter, st[5] > 0.5), search_body,
                        (jnp.int32(0), lo, hi, n_vis) + status(lo, hi, n_vis))
    lo, open_ = st[1], st[4]

    def resolve():
        def walk(thr_, strict):
            keep = (lambda x: x > thr_) if strict else (lambda x: x >= thr_)
            return reduce_keys(lambda x, off: jnp.where(keep(x), x, inf), inf, jnp.minimum, jnp.min)

        def above_equal(thr_):
            return count(lambda x, off: x > thr_), count(lambda x, off: x == thr_)

        def too_low(n_gt):
            return jnp.where(few, 0.0, jnp.where(n_gt >= kf, 1.0, 0.0))

        thr0 = walk(lo, False)

        def walk_body(st):
            thr_, n_gt, _ = st
            thr2 = jnp.where(too_low(n_gt) > 0.5, walk(thr_, True), thr_)
            return (thr2,) + above_equal(thr2)

        thr_, n_gt, n_eq = lax.while_loop(lambda st: jnp.max(too_low(st[1])) > 0.5, walk_body,
                                          (thr0,) + above_equal(thr0))
        need = kf - n_gt

        def tie_cut():
            def mark_equal(c, _):
                s_ref[0, c] = jnp.where(sc_ref[c] == thr_, 1.0, 0.0)
                return 0

            lax.fori_loop(0, n_ck, mark_equal, 0)

            def idx_step(b, cur):
                trial = cur | (jnp.int32(1) << (idx_bits - 1 - b))

                def body(c, part):
                    return part + fold(jnp.where(c * ck + krow < trial, s_ref[0, c], 0.0), jnp.sum)
                cnt = jnp.sum(lax.fori_loop(0, n_ck, body, jnp.zeros((fr, tq), F32)), axis=0, keepdims=True)
                return jnp.where(cnt < need, trial, cur)

            return lax.fori_loop(0, idx_bits, idx_step, jnp.zeros((1, tq), jnp.int32))

        tied = jnp.max(jnp.where(few, 0.0, jnp.where(n_eq > need, 1.0, 0.0))) > 0.5
        return thr_, lax.cond(tied, tie_cut, lambda: jnp.full((1, tq), 0x7FFFFFFF, jnp.int32))

    thr, cut = lax.cond(open_ > 0.5, resolve, lambda: (lo, jnp.full((1, tq), 0x7FFFFFFF, jnp.int32)))

    for h in range(nheads):
        m_ref[h] = jnp.full((fr, tq), NEG_BIG, F32)
        l_ref[h] = jnp.zeros((fr, tq), F32)
        acc_ref[h] = jnp.zeros((HEAD_DIM, tq), F32)

    def attn_scores(c, _):
        off = pl.multiple_of(c * ck, ck)
        kc = k_ref[pl.ds(off, ck), :]
        x = sc_ref[c]
        at_thr = jnp.where(x == thr, jnp.where(off + krow <= cut, 0.0, NEG_BIG), NEG_BIG)
        bias = jnp.where(x > thr, 0.0, at_thr)
        for h in range(nheads):
            s = _dot(kc, qt_ref[h]) + bias
            s_ref[h, c] = s
            m_ref[h] = jnp.maximum(m_ref[h], fold(s, jnp.max))
        return 0

    lax.fori_loop(0, n_ck, attn_scores, 0)
    for h in range(nheads):
        m_ref[h] = jnp.broadcast_to(jnp.max(m_ref[h], axis=0, keepdims=True), (fr, tq))

    def attn_values(c, _):
        vt = vt_ref[c]
        for h in range(nheads):
            p = jnp.exp(s_ref[h, c] - m_ref[h][:1, :])
            l_ref[h] += fold(p, jnp.sum)
            acc_ref[h] += _dot(vt, p.astype(BF16))
        return 0

    lax.fori_loop(0, n_ck, attn_values, 0)
    for h in range(nheads):
        l_row = jnp.sum(l_ref[h], axis=0, keepdims=True)
        o_ref[h * HEAD_DIM:(h + 1) * HEAD_DIM, :] = (acc_ref[h] / l_row).astype(o_ref.dtype)


def _dsa_attention(qt, qit, wt, k, v, ki, bsz, seq):
    tq = 256
    ck = min(512, seq)
    n_ck = seq // ck
    k_sel = min(TOPK_MAX, seq // 4)
    nq = seq // tq
    nh = ATTN_HEADS
    width = nh * HEAD_DIM
    vt = v.reshape(bsz, n_ck, ck, HEAD_DIM).transpose(0, 1, 3, 2)
    qspec = pl.BlockSpec((None, nh, HEAD_DIM, tq), lambda b, i: (b, 0, 0, i))
    out_t = pl.pallas_call(
        functools.partial(_dsa_kernel, tq=tq, ck=ck, k_sel=k_sel, idx_bits=(seq - 1).bit_length(), max_iter=22),
        grid=(bsz, nq),
        in_specs=[qspec, qspec,
                  pl.BlockSpec((IDX_HEADS, tq), lambda b, i: (0, b * nq + i)),
                  pl.BlockSpec((seq, HEAD_DIM), lambda b, i: (b, 0)),
                  pl.BlockSpec((None, n_ck, HEAD_DIM, ck), lambda b, i: (b, 0, 0, 0)),
                  pl.BlockSpec((seq, HEAD_DIM), lambda b, i: (b, 0))],
        out_specs=pl.BlockSpec((None, width, tq), lambda b, i: (b, 0, i)),
        out_shape=jax.ShapeDtypeStruct((bsz, width, seq), BF16),
        scratch_shapes=[pltpu.VMEM((n_ck, ck, tq), F32),
                        pltpu.VMEM((nh, n_ck, ck, tq), F32),
                        pltpu.VMEM((nh, 32, tq), F32),
                        pltpu.VMEM((nh, 32, tq), F32),
                        pltpu.VMEM((nh, HEAD_DIM, tq), F32)],
        compiler_params=_cparams(("arbitrary", "arbitrary"), VMEM_LIMIT),
        name="dsa_attention",
    )(qt, qit, wt, k, vt, ki)
    return out_t.transpose(0, 2, 1).reshape(bsz * seq, width)


def _outproj_kernel(x_ref, ys_ref, ya_ref, w_ref, mod_ref, o_ref, *, d):
    width = ys_ref.shape[1]
    y = _dot(ys_ref[...].astype(BF16), w_ref[:width, :]) + _dot(ya_ref[...], w_ref[width:, :])
    o_ref[...] = x_ref[...] + mod_ref[...][:, 2 * d:] * y


def _out_projection(x2, y_ssm, y_att, w_out, mod, layer, seq):
    t, d = x2.shape
    tm = min(1024, seq)
    width = y_ssm.shape[1]
    tiles_per_seq = seq // tm
    row = lambda i: (i, 0)
    return pl.pallas_call(
        functools.partial(_outproj_kernel, d=d),
        grid=(t // tm,),
        in_specs=[pl.BlockSpec((tm, d), row),
                  pl.BlockSpec((tm, width), row),
                  pl.BlockSpec((tm, width), row),
                  pl.BlockSpec((2 * width, d), lambda i: (0, 0)),
                  pl.BlockSpec((None, None, 1, 3 * d), lambda i: (layer, i // tiles_per_seq, 0, 0))],
        out_specs=pl.BlockSpec((tm, d), row),
        out_shape=jax.ShapeDtypeStruct((t, d), F32),
        compiler_params=_cparams(("arbitrary",), VMEM_LIMIT),
        name="out_projection",
    )(x2, y_ssm, y_att, w_out.astype(BF16), mod)


def _pool_kernel(x_ref, halo_ref, g_ref, mod_ref, pw_ref, ps_ref, o_ref, h_ref, fa_ref, fb_ref,
                 *, d, tiles_per_seq):
    i = pl.program_id(0)
    tm = x_ref.shape[0]
    pad, rows = POOL_HALO, POOL_HALO + tm
    mod = mod_ref[...]
    shift, scale, gate = mod[:, :d], mod[:, d:2 * d], mod[:, 2 * d:]
    x = x_ref[...]
    first = (i % tiles_per_seq) == 0
    gw = d // len(POOL_WINDOWS)
    h_ref[:pad, :] = jnp.zeros((pad, d), F32)
    fa_ref[:pad, :] = jnp.zeros((pad, gw), F32)
    fb_ref[:pad, :] = jnp.zeros((pad, gw), F32)
    h_ref[2 * pad:, :] = _norm_mod(x, g_ref[...], shift, scale)
    halo = _norm_mod(halo_ref[...], g_ref[...], shift, scale)
    h_ref[pad:2 * pad, :] = jnp.where(first, 0.0, halo)
    pos = (i % tiles_per_seq) * tm + lax.broadcasted_iota(jnp.int32, (tm, 1), 0)
    ys = []
    for gi, win in enumerate(POOL_WINDOWS):
        cs = slice(gi * gw, (gi + 1) * gw)
        cur = h_ref[2 * pad:, cs]
        src, width, bufs = None, 1, (fa_ref, fb_ref)
        while width < win:
            dst = bufs[0] if src is not bufs[0] else bufs[1]
            if src is None:
                dst[pad:, :] = h_ref[pad:, cs] + h_ref[pad - width:pad - width + rows, cs]
            else:
                dst[pad:, :] = src[pad:, :] + src[pad - width:pad - width + rows, :]
            src, width = dst, 2 * width
        tot = src[2 * pad:, :]
        cnt = jnp.minimum(pos + 1, win).astype(F32)
        pooled = tot / cnt - cur
        ys.append(_dot(pooled.astype(BF16), pw_ref[gi]))
    y = jnp.concatenate(ys, axis=1) * ps_ref[...]
    o_ref[...] = x + gate * y


def _pool_layer(x2, g, mod, layer, pool_w, pool_scale, seq):
    t, d = x2.shape
    tm = min(1024, seq)
    tiles_per_seq = seq // tm
    gw = d // len(POOL_WINDOWS)
    hb = tm // POOL_HALO
    return pl.pallas_call(
        functools.partial(_pool_kernel, d=d, tiles_per_seq=tiles_per_seq),
        grid=(t // tm,),
        in_specs=[pl.BlockSpec((tm, d), lambda i: (i, 0)),
                  pl.BlockSpec((POOL_HALO, d), lambda i: (jnp.maximum(i * hb - 1, 0), 0)),
                  pl.BlockSpec((1, d), lambda i: (0, 0)),
                  pl.BlockSpec((None, None, 1, 3 * d), lambda i: (layer, i // tiles_per_seq, 0, 0)),
                  pl.BlockSpec((len(POOL_WINDOWS), gw, gw), lambda i: (0, 0, 0)),
                  pl.BlockSpec((1, d), lambda i: (0, 0))],
        out_specs=pl.BlockSpec((tm, d), lambda i: (i, 0)),
        out_shape=jax.ShapeDtypeStruct((t, d), F32),
        scratch_shapes=[pltpu.VMEM((tm + 2 * POOL_HALO, d), F32),
                        pltpu.VMEM((tm + 2 * POOL_HALO, gw), F32),
                        pltpu.VMEM((tm + 2 * POOL_HALO, gw), F32)],
        compiler_params=_cparams(("arbitrary",), VMEM_LIMIT),
        name="pool_mixer",
    )(x2, x2, g.reshape(1, d), mod, pool_w.astype(BF16), pool_scale.reshape(1, d))


def _first_max(vals, idx, big):
    m = jnp.max(vals, axis=0, keepdims=True)
    first = jnp.min(jnp.where(vals == m, idx, big), axis=0, keepdims=True)
    return m, first


def _pack_bf16_pairs(x):
    n = x.shape[1] // 2
    lo = pltpu.bitcast(x[:, :n].astype(BF16).astype(F32), jnp.int32)
    hi = pltpu.bitcast(x[:, n:].astype(BF16).astype(F32), jnp.int32)
    return hi | ((lo >> 16) & 0xFFFF)


def _unpack_bf16_pairs(w):
    lo = pltpu.bitcast(w << 16, F32)
    hi = pltpu.bitcast(w & jnp.int32(-65536), F32)
    return jnp.concatenate([lo, hi], axis=1)


def _router_kernel(x_ref, g_ref, mod_ref, rw_ref, rb_ref, tri_ref, h_ref, gate_ref, rank_ref, cnt_ref,
                   run_ref, *, d, n_exp):
    mod = mod_ref[...]
    h = _norm_mod(x_ref[...], g_ref[...], mod[:, :d], mod[:, d:2 * d])
    h_ref[...] = _pack_bf16_pairs(h)
    tm = h.shape[0]
    hh, hl = _split_bf16(h)
    rw = rw_ref[...]
    rh, rl = _split_bf16(rw)
    logits = _dot_nt(rh, hh) + (_dot_nt(rh, hl) + _dot_nt(rl, hh))
    scores = jax.nn.sigmoid(logits)
    sel = scores + rb_ref[...]
    gsz = n_exp // N_EXPERT_GROUPS
    neg = -jnp.inf
    shape3 = (N_EXPERT_GROUPS, gsz, tm)
    sel3 = sel.reshape(shape3)
    sub = lax.broadcasted_iota(jnp.int32, shape3, 1)
    m1 = jnp.max(sel3, axis=1, keepdims=True)
    f1 = jnp.min(jnp.where(sel3 == m1, sub, gsz), axis=1, keepdims=True)
    m2 = jnp.max(jnp.where(sub == f1, neg, sel3), axis=1, keepdims=True)
    gscore = jnp.broadcast_to(m1 + m2, shape3).reshape(n_exp, tm)
    eidx = lax.broadcasted_iota(jnp.int32, (n_exp, tm), 0)
    gidx = eidx // gsz
    keep = jnp.zeros((n_exp, tm), F32)
    work = gscore
    for _ in range(TOPK_GROUPS):
        _, first = _first_max(work, gidx, N_EXPERT_GROUPS)
        hit = gidx == first
        keep = jnp.where(hit, 1.0, keep)
        work = jnp.where(hit, neg, work)
    work = jnp.where(keep > 0.0, sel, neg)
    chosen = jnp.zeros((n_exp, tm), F32)
    for _ in range(TOP_K):
        _, first = _first_max(work, eidx, n_exp)
        hit = eidx == first
        chosen = jnp.where(hit, 1.0, chosen)
        work = jnp.where(hit, neg, work)
    picked = chosen * scores
    gate_ref[...] = picked / jnp.sum(picked, axis=0, keepdims=True) * ROUTED_SCALE

    @pl.when(pl.program_id(0) == 0)
    def _():
        run_ref[...] = jnp.zeros(run_ref.shape, F32)

    before = _dot(chosen.astype(BF16), tri_ref[...])
    run = run_ref[...]
    rank_ref[...] = jnp.where(chosen > 0.0, before + run[:, :1], -1.0)
    run = run + jnp.broadcast_to(jnp.sum(chosen, axis=1, keepdims=True), run.shape)
    run_ref[...] = run
    cnt_ref[...] = run


MOE_TILE = 512
ROUTER_TILE = 1024


def _ffn_router(x2, g, mod, layer, router_w, router_bias, seq, tile0, ntiles):
    d = x2.shape[1]
    n_exp = router_w.shape[1]
    tm = ROUTER_TILE
    t = ntiles * tm
    tiles_per_seq = seq // tm
    tri = (np.arange(tm)[:, None] < np.arange(tm)[None, :]).astype(np.float32)
    return pl.pallas_call(
        functools.partial(_router_kernel, d=d, n_exp=n_exp),
        grid=(ntiles,),
        in_specs=[pl.BlockSpec((tm, d), lambda i: (i + tile0, 0)),
                  pl.BlockSpec((1, d), lambda i: (0, 0)),
                  pl.BlockSpec((None, None, 1, 3 * d), lambda i: (layer, (i + tile0) // tiles_per_seq, 0, 0)),
                  pl.BlockSpec((n_exp, d), lambda i: (0, 0)),
                  pl.BlockSpec((n_exp, 1), lambda i: (0, 0)),
                  pl.BlockSpec((tm, tm), lambda i: (0, 0))],
        out_specs=[pl.BlockSpec((tm, d // 2), lambda i: (i, 0)),
                   pl.BlockSpec((n_exp, tm), lambda i: (0, i)),
                   pl.BlockSpec((n_exp, tm), lambda i: (0, i)),
                   pl.BlockSpec((n_exp, 128), lambda i: (0, 0))],
        out_shape=[jax.ShapeDtypeStruct((t, d // 2), jnp.int32),
                   jax.ShapeDtypeStruct((n_exp, t), F32),
                   jax.ShapeDtypeStruct((n_exp, t), F32),
                   jax.ShapeDtypeStruct((n_exp, 128), F32)],
        scratch_shapes=[pltpu.VMEM((n_exp, 128), F32)],
        compiler_params=_cparams(("arbitrary",)),
        name="ffn_router",
    )(x2, g.reshape(1, d), mod, router_w.T, router_bias.reshape(n_exp, 1), jnp.asarray(tri, BF16))


def _assign_kernel(rank_ref, gate_ref, start_ref, pos_ref, w_ref, *, n_exp):
    rank = rank_ref[...]
    gates = gate_ref[...]
    tm = rank.shape[1]
    slot = rank + start_ref[...]
    eidx = lax.broadcasted_iota(jnp.int32, (n_exp, tm), 0).astype(F32)
    alive = jnp.where(rank >= 0.0, eidx, float(n_exp))
    kidx = lax.broadcasted_iota(jnp.int32, (TOP_K, tm), 0)
    pos = jnp.zeros((TOP_K, tm), F32)
    wts = jnp.zeros((TOP_K, tm), F32)
    for k in range(TOP_K):
        first = jnp.min(alive, axis=0, keepdims=True)
        hit = alive == first
        pos_k = jnp.sum(jnp.where(hit, slot, 0.0), axis=0, keepdims=True)
        w_k = jnp.sum(jnp.where(hit, gates, 0.0), axis=0, keepdims=True)
        pos = jnp.where(kidx == k, pos_k, pos)
        wts = jnp.where(kidx == k, w_k, wts)
        alive = jnp.where(hit, float(n_exp), alive)
    pos_ref[...] = pos.astype(jnp.int32)
    w_ref[...] = wts


def _assign_slots(ranks, gates, start):
    n_exp, t = ranks.shape
    tm = 512
    return pl.pallas_call(
        functools.partial(_assign_kernel, n_exp=n_exp),
        grid=(t // tm,),
        in_specs=[pl.BlockSpec((n_exp, tm), lambda i: (0, i)),
                  pl.BlockSpec((n_exp, tm), lambda i: (0, i)),
                  pl.BlockSpec((n_exp, 1), lambda i: (0, 0))],
        out_specs=[pl.BlockSpec((TOP_K, tm), lambda i: (0, i)),
                   pl.BlockSpec((TOP_K, tm), lambda i: (0, i))],
        out_shape=[jax.ShapeDtypeStruct((TOP_K, t), jnp.int32), jax.ShapeDtypeStruct((TOP_K, t), F32)],
        compiler_params=_cparams(("arbitrary",)),
        name="moe_assign",
    )(ranks, gates, start)


SC_CORES = 2
SC_SUBCORES = 16
SC_WINDOW = 128


def _sc_mesh():
    return plsc.VectorSubcoreMesh(core_axis_name="c", subcore_axis_name="s",
                                  num_cores=SC_CORES, num_subcores=SC_SUBCORES)


def _sc_scatter_rows(rows, pos, n_out):
    t, width = rows.shape
    nk = pos.shape[0]
    win = SC_WINDOW
    n_win = t // win // (SC_CORES * SC_SUBCORES)
    pos_w = pos.reshape(nk, t // win, win).transpose(1, 0, 2)

    def body(rows_hbm, pos_hbm, out_hbm, idx_v, buf_v, sem):
        wid = lax.axis_index("s") * SC_CORES + lax.axis_index("c")

        @pl.loop(0, n_win)
        def _(j):
            w = wid * n_win + j
            pltpu.sync_copy(rows_hbm.at[pl.ds(w * win, win)], buf_v)
            pltpu.sync_copy(pos_hbm.at[w], idx_v)
            copies = [pltpu.make_async_copy(buf_v, out_hbm.at[idx_v.at[k]], sem) for k in range(nk)]
            for cp in copies:
                cp.start()
            for cp in copies:
                cp.wait()

    return pl.kernel(
        body, out_type=jax.ShapeDtypeStruct((n_out, width), jnp.int32), mesh=_sc_mesh(),
        scratch_types=[pltpu.VMEM((nk, win), jnp.int32), pltpu.VMEM((win, width), jnp.int32),
                       pltpu.SemaphoreType.DMA],
        name="sc_scatter_rows",
    )(rows, pos_w)


SC_GATHER_WINDOW = 64


def _sc_gather_rows(table, pos_flat):
    m = pos_flat.shape[0]
    width = table.shape[1]
    win = SC_GATHER_WINDOW
    workers = SC_CORES * SC_SUBCORES
    n_win = m // win // workers
    pos_w = pos_flat.reshape(workers, n_win, win)

    def body(table_hbm, pos_hbm, out_hbm, idx_v, buf_v, sem_g, sem_w):
        wid = lax.axis_index("s") * SC_CORES + lax.axis_index("c")
        base = wid * (n_win * win)
        pltpu.sync_copy(pos_hbm.at[wid], idx_v)

        def gather(j, b):
            return pltpu.make_async_copy(table_hbm.at[idx_v.at[j]], buf_v.at[b], sem_g.at[b])

        def write(j, b):
            return pltpu.make_async_copy(buf_v.at[b], out_hbm.at[pl.ds(base + j * win, win)], sem_w.at[b])

        gather(0, 0).start()

        @pl.loop(0, n_win, step=2)
        def _(j0):
            for b in range(2):
                j = j0 + b
                gather(j, b).wait()
                write(j, b).start()

                @pl.when(j + 1 < n_win)
                def _():
                    @pl.when(j >= 1)
                    def _():
                        write(j - 1, 1 - b).wait()
                    gather(j + 1, 1 - b).start()

        write(n_win - 2, 0).wait()
        write(n_win - 1, 1).wait()

    return pl.kernel(
        body, out_type=jax.ShapeDtypeStruct((m, width), jnp.int32), mesh=_sc_mesh(),
        scratch_types=[pltpu.VMEM((n_win, win), jnp.int32), pltpu.VMEM((2, win, width), jnp.int32),
                       pltpu.SemaphoreType.DMA((2,)), pltpu.SemaphoreType.DMA((2,))],
        name="sc_gather_rows",
    )(table, pos_w)


MOE_BLOCK = 1024


def _silu_mul(a, b):
    return (a * jax.nn.sigmoid(a)) * b


def _expert_ffn_kernel(be_ref, nv_ref, xs_ref, wg_ref, wu_ref, wd_ref, ys_ref, wg_bf, wu_bf, wd_bf):
    i = pl.program_id(0)
    fresh = jnp.logical_or(i == 0, be_ref[i] != be_ref[jnp.maximum(i - 1, 0)])

    @pl.when(fresh)
    def _():
        wg_bf[...] = wg_ref[...].astype(BF16)
        wu_bf[...] = wu_ref[...].astype(BF16)
        wd_bf[...] = wd_ref[...].astype(BF16)

    nv = nv_ref[i]

    @pl.when(nv > 0)
    def _():
        x = _unpack_bf16_pairs(xs_ref[...])
        row = lax.broadcasted_iota(jnp.int32, (x.shape[0], 1), 0)
        x = jnp.where(row < nv, x, 0.0).astype(BF16)
        mid = _silu_mul(_dot(x, wg_bf[...]), _dot(x, wu_bf[...]))
        ys_ref[...] = _pack_bf16_pairs(_dot(mid.astype(BF16), wd_bf[...]))

    @pl.when(nv <= 0)
    def _():
        ys_ref[...] = jnp.zeros(ys_ref.shape, ys_ref.dtype)


def _expert_ffn(xs, block_expert, block_rows, layer, w_gate, w_up, w_down):
    n_rows, half = xs.shape
    _, n_exp, d, de = w_gate.shape
    nb = n_rows // MOE_BLOCK
    wmap = lambda i, be, nv: (layer, be[i], 0, 0)
    grid_spec = pltpu.PrefetchScalarGridSpec(
        num_scalar_prefetch=2,
        grid=(nb,),
        in_specs=[pl.BlockSpec((MOE_BLOCK, half), lambda i, be, nv: (jnp.where(nv[i] > 0, i, 0), 0)),
                  pl.BlockSpec((None, None, d, de), wmap),
                  pl.BlockSpec((None, None, d, de), wmap),
                  pl.BlockSpec((None, None, de, d), wmap)],
        out_specs=pl.BlockSpec((MOE_BLOCK, half), lambda i, be, nv: (i, 0)),
        scratch_shapes=[pltpu.VMEM((d, de), BF16), pltpu.VMEM((d, de), BF16), pltpu.VMEM((de, d), BF16)],
    )
    return pl.pallas_call(
        _expert_ffn_kernel,
        grid_spec=grid_spec,
        out_shape=jax.ShapeDtypeStruct((n_rows, half), jnp.int32),
        compiler_params=_cparams(("arbitrary",), VMEM_LIMIT),
        name="moe_expert_ffn",
    )(block_expert, block_rows, xs, w_gate, w_up, w_down)


def _combine_kernel(yt_ref, wt_ref, hp_ref, x_ref, mod_ref, sg_ref, su_ref, sd_ref, *rest, d):
    o_ref = rest[-1]
    h = _unpack_bf16_pairs(hp_ref[...]).astype(BF16)
    acc = _dot(_silu_mul(_dot(h, sg_ref[...]), _dot(h, su_ref[...])).astype(BF16), sd_ref[...])
    w = wt_ref[...]
    for k in range(TOP_K):
        acc = acc + w[:, k:k + 1] * _unpack_bf16_pairs(yt_ref[k])
    o_ref[...] = x_ref[...] + mod_ref[...][:, 2 * d:] * acc


def _combine(y_tok, w_tok, h_packed, x2, out_prev, mod, layer, sh_gate, sh_up, sh_down, seq, tile0):
    t_all, d = x2.shape
    ds = sh_gate.shape[1]
    tm = MOE_TILE
    ntiles = h_packed.shape[0] // tm
    tiles_per_seq = seq // tm
    row = lambda i: (i, 0)
    off = lambda i: (i + tile0, 0)
    const = lambda i: (0, 0)
    in_specs = [pl.BlockSpec((TOP_K, tm, d // 2), lambda i: (0, i, 0)),
                pl.BlockSpec((tm, TOP_K), row),
                pl.BlockSpec((tm, d // 2), row),
                pl.BlockSpec((tm, d), off),
                pl.BlockSpec((None, None, 1, 3 * d), lambda i: (layer, (i + tile0) // tiles_per_seq, 0, 0)),
                pl.BlockSpec((d, ds), const),
                pl.BlockSpec((d, ds), const),
                pl.BlockSpec((ds, d), const)]
    args = [y_tok, w_tok, h_packed, x2, mod, sh_gate.astype(BF16), sh_up.astype(BF16), sh_down.astype(BF16)]
    aliases = {}
    if out_prev is not None:
        in_specs.append(pl.BlockSpec(memory_space=pl.ANY))
        args.append(out_prev)
        aliases = {len(args) - 1: 0}
    return pl.pallas_call(
        functools.partial(_combine_kernel, d=d),
        grid=(ntiles,),
        in_specs=in_specs,
        out_specs=pl.BlockSpec((tm, d), off),
        out_shape=jax.ShapeDtypeStruct((t_all, d), F32),
        input_output_aliases=aliases,
        compiler_params=_cparams(("arbitrary",), VMEM_LIMIT),
        name="moe_combine",
    )(*args)


def _moe_group(x2, out_prev, tile0, h_packed, gates, ranks, counts, mod, layer, w_gate, w_up, w_down,
               sh_gate, sh_up, sh_down, seq):
    t, half = h_packed.shape
    d = 2 * half
    n_exp = w_gate.shape[1]
    cnt = counts[:, 0].astype(jnp.int32)
    padded = (cnt + MOE_BLOCK - 1) // MOE_BLOCK * MOE_BLOCK
    eidx = jnp.arange(n_exp, dtype=jnp.int32)
    seg_end = jnp.sum(jnp.where(eidx[None, :] <= eidx[:, None], padded[None, :], 0), axis=1)
    seg_start = seg_end - padded
    nb = t * TOP_K // MOE_BLOCK + n_exp
    first_row = jnp.arange(nb, dtype=jnp.int32) * MOE_BLOCK
    block_expert = jnp.sum((seg_end[None, :] <= first_row[:, None]).astype(jnp.int32), axis=1)
    block_expert = jnp.minimum(block_expert, n_exp - 1)
    last_row = jnp.sum(jnp.where(block_expert[:, None] == eidx[None, :], (seg_start + cnt)[None, :], 0), axis=1)
    block_rows = jnp.clip(last_row - first_row, 0, MOE_BLOCK).astype(jnp.int32)
    pos, w = _assign_slots(ranks, gates, seg_start.astype(F32).reshape(n_exp, 1))
    xs = _sc_scatter_rows(h_packed, pos, nb * MOE_BLOCK)
    ys = _expert_ffn(xs, block_expert, block_rows, layer, w_gate, w_up, w_down)
    y_tok = _sc_gather_rows(ys, pos.reshape(TOP_K * t)).reshape(TOP_K, t, d // 2)
    return _combine(y_tok, w.T, h_packed, x2, out_prev, mod, layer, sh_gate, sh_up, sh_down, seq, tile0)


MOE_GROUPS = 1


def _moe_layer(x2, g, mod, layer, router_w, router_bias, w_gate, w_up, w_down, sh_gate, sh_up, sh_down, seq):
    ntiles = x2.shape[0] // MOE_TILE // MOE_GROUPS
    rtiles = ntiles * MOE_TILE // ROUTER_TILE
    routed = [_ffn_router(x2, g, mod, layer, router_w, router_bias, seq, grp * rtiles, rtiles)
              for grp in range(MOE_GROUPS)]
    out = None
    for grp in range(MOE_GROUPS):
        out = _moe_group(x2, out, grp * ntiles, *routed[grp], mod, layer, w_gate, w_up, w_down,
                         sh_gate, sh_up, sh_down, seq)
    return out


def kernel(x, c, positions, mix_norm_g, mix_mod_w, mix_mod_b, ffn_norm_g, ffn_mod_w, ffn_mod_b,
           hyb_w_in, hyb_w_out, s5_lambda_re, s5_lambda_im, s5_log_dt, s5_b_re, s5_b_im,
           s5_c_re, s5_c_im, s5_d, s5_glu_w, s5_glu_b, attn_q_norm_g, attn_k_norm_g,
           pool_w, pool_scale, router_w, router_bias, exp_w_gate, exp_w_up, exp_w_down,
           sh_w_gate, sh_w_up, sh_w_down):
    bsz, seq, d = x.shape
    t = bsz * seq
    depth = mix_norm_g.shape[0]
    x2 = x.reshape(t, d)
    mix_mod = _mod_vectors(c, mix_mod_w, mix_mod_b).reshape(depth, bsz, 1, 3 * d)
    ffn_mod = _mod_vectors(c, ffn_mod_w, ffn_mod_b).reshape(depth, bsz, 1, 3 * d)
    for i in range(depth):
        j = i // 2
        if i % 2 == 0:
            u, q_hm, qi_hm, k, v, ki, wi = _in_projection(
                x2, mix_norm_g[i], mix_mod, i, hyb_w_in[j], positions, attn_q_norm_g[j], attn_k_norm_g[j], bsz, seq)
            weights = _s5_weights(s5_lambda_re[j], s5_lambda_im[j], s5_log_dt[j], s5_b_re[j], s5_b_im[j],
                                  s5_c_re[j], s5_c_im[j], s5_d[j], s5_glu_w[j], s5_glu_b[j])
            y_ssm = _s5_mixer(u, weights, bsz, seq)
            y_att = _dsa_attention(q_hm, qi_hm, wi, k, v, ki, bsz, seq)
            x2 = _out_projection(x2, y_ssm, y_att, hyb_w_out[j], mix_mod, i, seq)
        else:
            x2 = _pool_layer(x2, mix_norm_g[i], mix_mod, i, pool_w[j], pool_scale[j], seq)
        x2 = _moe_layer(x2, ffn_norm_g[i], ffn_mod, i, router_w[i], router_bias[i], exp_w_gate, exp_w_up, exp_w_down,
                        sh_w_gate[i], sh_w_up[i], sh_w_down[i], seq)
    return x2.reshape(bsz, seq, d)
```

```python
import functools

import numpy as np
import jax
import jax.numpy as jnp
from jax import lax
from jax.experimental import pallas as pl
from jax.experimental.pallas import tpu as pltpu
from jax.experimental.pallas import tpu_sc as plsc

F32 = jnp.float32
BF16 = jnp.bfloat16

EPS = 1e-6
S5_GROUP = 16
S5_STATE = 64
HEAD_DIM = 64
ATTN_HEADS = 8
IDX_HEADS = 8
ROPE_HALF = 8
ROPE_THETA = 500000.0
TOPK_MAX = 256
POOL_WINDOWS = (2, 4, 8, 16)
POOL_HALO = 16
N_EXPERT_GROUPS = 8
TOPK_GROUPS = 4
TOP_K = 8
ROUTED_SCALE = 2.5
S5_CHUNK = 16
NEG_BIG = -1e30
VMEM_LIMIT = 56 * 1024 * 1024


def _cparams(sem, vmem=None):
    return pltpu.CompilerParams(dimension_semantics=sem, vmem_limit_bytes=vmem)


def _dot(a, b):
    return jnp.dot(a, b, preferred_element_type=F32)


def _dot_nt(a, b):
    return lax.dot_general(a, b, (((1,), (1,)), ((), ())), preferred_element_type=F32)


def _split_bf16(a):
    hi = a.astype(BF16)
    lo = (a - hi.astype(F32)).astype(BF16)
    return hi, lo


def _norm_mod(x, g, shift, scale):
    y = x * lax.rsqrt(jnp.mean(x * x, axis=-1, keepdims=True) + EPS)
    return (y * g) * (1.0 + scale) + shift


def _mod_kernel(ct_ref, w_ref, b_ref, o_ref):
    ct = ct_ref[...]
    cs = ct * jax.nn.sigmoid(ct)
    w = w_ref[...]
    rows = [jnp.sum(w * cs[:, b:b + 1], axis=0, keepdims=True) for b in range(ct.shape[1])]
    o_ref[...] = jnp.concatenate(rows, axis=0) + b_ref[...]


def _mod_vectors(c, w, b):
    nl, d, n3 = w.shape
    bsz = c.shape[0]
    tn = 512
    return pl.pallas_call(
        _mod_kernel,
        grid=(nl, n3 // tn),
        in_specs=[pl.BlockSpec((d, bsz), lambda l, j: (0, 0)),
                  pl.BlockSpec((None, d, tn), lambda l, j: (l, 0, j)),
                  pl.BlockSpec((None, 1, tn), lambda l, j: (l, 0, j))],
        out_specs=pl.BlockSpec((None, bsz, tn), lambda l, j: (l, 0, j)),
        out_shape=jax.ShapeDtypeStruct((nl, bsz, n3), F32),
        compiler_params=_cparams(("arbitrary", "arbitrary")),
        name="mod_vectors",
    )(c.T, w, b.reshape(nl, 1, n3))


def _inproj_kernel(x_ref, g_ref, mod_ref, w_ref, pos_ref, inv_ref, qg_ref, kg_ref,
                   u_ref, q_ref, qi_ref, k_ref, v_ref, ki_ref, wi_ref, u_scr, *, d):
    x = x_ref[...]
    mod = mod_ref[...]
    h = _norm_mod(x, g_ref[...], mod[:, :d], mod[:, d:2 * d])
    proj = _dot(h.astype(BF16), w_ref[...])
    tm = x.shape[0]
    width = ATTN_HEADS * HEAD_DIM

    ang = inv_ref[...] * pos_ref[...].astype(F32)
    cos, sin = jnp.cos(ang), jnp.sin(ang)

    def head_t(xt, gain, scale):
        if scale is not None:
            r = lax.rsqrt(jnp.mean(xt * xt, axis=0, keepdims=True) + EPS) * scale
            xt = xt * gain
        x1, x2 = xt[:ROPE_HALF], xt[ROPE_HALF:2 * ROPE_HALF]
        out = jnp.concatenate([x1 * cos - x2 * sin, x1 * sin + x2 * cos, xt[2 * ROPE_HALF:]], axis=0)
        return out if scale is None else out * r

    for slab in range(width // 128):
        u_scr[slab] = proj[:, slab * 128:(slab + 1) * 128]
        for j in range(S5_CHUNK):
            piece = u_scr[slab, pl.ds(j, tm // S5_CHUNK, stride=S5_CHUNK), :]
            col = (slab * S5_CHUNK + j) * 128
            u_ref[:, col:col + 128] = piece.astype(u_ref.dtype)
    q = proj[:, width:2 * width]
    qi = proj[:, 2 * width:3 * width]
    small = proj[:, 3 * width:3 * width + 256]
    qg = qg_ref[...]
    for pair in range(ATTN_HEADS // 2):
        sl = slice(pair * 128, (pair + 1) * 128)
        qt = q[:, sl].T
        qit = qi[:, sl].T
        for half in range(2):
            hs = slice(half * HEAD_DIM, (half + 1) * HEAD_DIM)
            q_ref[2 * pair + half] = head_t(qt[hs], qg, HEAD_DIM ** -0.5).astype(q_ref.dtype)
            qi_ref[2 * pair + half] = head_t(qit[hs], None, None).astype(qi_ref.dtype)
    kvt = small[:, :128].T
    kt = head_t(kvt[:HEAD_DIM], kg_ref[...], 1.0)
    kv = jnp.concatenate([kt, kvt[HEAD_DIM:]], axis=0).T
    k_ref[...] = kv[:, :HEAD_DIM].astype(k_ref.dtype)
    v_ref[...] = kv[:, HEAD_DIM:].astype(v_ref.dtype)
    kiwt = small[:, 128:256].T
    kit = jnp.concatenate([head_t(kiwt[:HEAD_DIM], None, None), kiwt[HEAD_DIM:]], axis=0).T
    ki_ref[...] = kit[:, :HEAD_DIM].astype(ki_ref.dtype)
    wscale = (IDX_HEADS ** -0.5) * (HEAD_DIM ** -0.5)
    wi_ref[...] = kiwt[HEAD_DIM:HEAD_DIM + IDX_HEADS] * wscale


def _in_projection(x2, g, mod, layer, w_in, positions, q_norm_g, k_norm_g, bsz, seq):
    t, d = x2.shape
    tm = 512
    width = ATTN_HEADS * HEAD_DIM
    cuts = np.cumsum([width, width, HEAD_DIM, HEAD_DIM, width, HEAD_DIM, IDX_HEADS])
    u_w, q_w, k_w, v_w, qi_w, ki_w, wi_w = jnp.split(w_in, cuts[:-1].tolist(), axis=1)
    pad = jnp.zeros((d, 128 - HEAD_DIM - IDX_HEADS), w_in.dtype)
    w = jnp.concatenate([u_w, q_w, qi_w, k_w, v_w, ki_w, wi_w, pad], axis=1).astype(BF16)
    nw = w.shape[1]
    inv = np.power(ROPE_THETA, -2.0 * np.arange(ROPE_HALF) / (2 * ROPE_HALF))
    inv = jnp.asarray(inv.reshape(ROPE_HALF, 1), F32)
    tiles_per_seq = seq // tm
    row = lambda i: (i, 0)
    const = lambda i: (0, 0)
    hm = lambda i: (i // tiles_per_seq, 0, 0, i % tiles_per_seq)
    outs = pl.pallas_call(
        functools.partial(_inproj_kernel, d=d),
        grid=(t // tm,),
        in_specs=[pl.BlockSpec((tm, d), row),
                  pl.BlockSpec((1, d), const),
                  pl.BlockSpec((None, None, 1, 3 * d), lambda i: (layer, i // tiles_per_seq, 0, 0)),
                  pl.BlockSpec((d, nw), const),
                  pl.BlockSpec((None, 1, tm), lambda i: (i, 0, 0)),
                  pl.BlockSpec((ROPE_HALF, 1), const),
                  pl.BlockSpec((HEAD_DIM, 1), const),
                  pl.BlockSpec((HEAD_DIM, 1), const)],
        out_specs=[pl.BlockSpec((tm // S5_CHUNK, width * S5_CHUNK), row),
                   pl.BlockSpec((None, ATTN_HEADS, HEAD_DIM, tm), hm),
                   pl.BlockSpec((None, IDX_HEADS, HEAD_DIM, tm), hm),
                   pl.BlockSpec((tm, HEAD_DIM), row),
                   pl.BlockSpec((tm, HEAD_DIM), row),
                   pl.BlockSpec((tm, HEAD_DIM), row),
                   pl.BlockSpec((IDX_HEADS, tm), lambda i: (0, i))],
        out_shape=[jax.ShapeDtypeStruct((t // S5_CHUNK, width * S5_CHUNK), BF16),
                   jax.ShapeDtypeStruct((bsz, ATTN_HEADS, HEAD_DIM, seq), BF16),
                   jax.ShapeDtypeStruct((bsz, IDX_HEADS, HEAD_DIM, seq), BF16),
                   jax.ShapeDtypeStruct((t, HEAD_DIM), BF16),
                   jax.ShapeDtypeStruct((t, HEAD_DIM), BF16),
                   jax.ShapeDtypeStruct((t, HEAD_DIM), BF16),
                   jax.ShapeDtypeStruct((IDX_HEADS, t), F32)],
        scratch_shapes=[pltpu.VMEM((width // 128, tm, 128), F32)],
        compiler_params=_cparams(("arbitrary",), VMEM_LIMIT),
        name="in_projection",
    )(x2, g.reshape(1, d), mod, w, positions.reshape(t // tm, 1, tm), inv,
      q_norm_g.reshape(HEAD_DIM, 1), k_norm_g.reshape(HEAD_DIM, 1))
    return outs


def _s5_weights(lam_re, lam_im, log_dt, b_re, b_im, c_re, c_im, d_skip, glu_w, glu_b):
    L = S5_CHUNK
    g, p = lam_re.shape
    hch = S5_GROUP
    dt = jnp.exp(log_dt)[:, None]
    lr, li = lam_re, lam_im
    tau = jnp.arange(L + 1, dtype=F32)[:, None, None]
    mag = jnp.exp(lr * dt * tau)
    pw_r, pw_i = mag * jnp.cos(li * dt * tau), mag * jnp.sin(li * dt * tau)
    ar, ai = pw_r[1], pw_i[1]
    nr, ni = ar - 1.0, ai
    den = lr * lr + li * li
    cr, ci = (nr * lr + ni * li) / den, (ni * lr - nr * li) / den
    bb_r = cr[..., None] * b_re - ci[..., None] * b_im
    bb_i = cr[..., None] * b_im + ci[..., None] * b_re
    cl_r = c_re[None] * pw_r[:L, :, None, :] - c_im[None] * pw_i[:L, :, None, :]
    cl_i = c_re[None] * pw_i[:L, :, None, :] + c_im[None] * pw_r[:L, :, None, :]
    taps = (jnp.einsum('tghp,gpk->tghk', cl_r, bb_r) - jnp.einsum('tghp,gpk->tghk', cl_i, bb_i))
    ns = g // 8
    eye = jnp.eye(8, dtype=F32)
    kd = jnp.einsum('tsgoh,gf->tsghfo', taps.reshape(L, ns, 8, hch, hch), eye).reshape(L, ns, 128, 128)
    zero = jnp.zeros_like(kd[0])
    k2 = jnp.stack([jnp.concatenate([jnp.concatenate([kd[2 * dd], kd[2 * dd + 1]], axis=-1),
                                     jnp.concatenate([kd[2 * dd - 1] if dd else zero, kd[2 * dd]], axis=-1)], axis=-2)
                    for dd in range(L // 2)], axis=1).astype(BF16)
    ii = jnp.arange(L)
    rev_r, rev_i = pw_r[L - 1 - ii], pw_i[L - 1 - ii]
    z_r = rev_r[..., None] * bb_r[None] - rev_i[..., None] * bb_i[None]
    z_i = rev_r[..., None] * bb_i[None] + rev_i[..., None] * bb_r[None]
    zc = jnp.concatenate([z_r, z_i], axis=2)
    zc = jnp.transpose(zc.reshape(L, ns, 8, 2 * p, hch), (1, 0, 2, 4, 3)).reshape(ns, L * 128, 2 * p)
    row_group = (jnp.arange(L * 128) // hch) % 8
    state_group = jnp.arange(8 * 2 * p) // (2 * p)
    w_z = jnp.where((row_group[:, None] == state_group[None, :])[None], jnp.concatenate([zc] * 8, axis=-1), 0.0)
    w_z = w_z.reshape(ns, L // 2, 2 * 128, 8 * 2 * p).astype(BF16)
    co_r = c_re[None] * pw_r[1:, :, None, :] - c_im[None] * pw_i[1:, :, None, :]
    co_i = c_re[None] * pw_i[1:, :, None, :] + c_im[None] * pw_r[1:, :, None, :]
    cc = jnp.concatenate([co_r, -co_i], axis=-1)
    cc = jnp.transpose(cc.reshape(L, ns, 8, hch, 2 * p), (1, 4, 0, 2, 3)).reshape(ns, 1, 2 * p, L * 128)
    col_group = (jnp.arange(L * 128) // hch) % 8
    w_c = jnp.where((jnp.arange(8)[:, None] == col_group[None, :])[None, :, None, :], cc, 0.0)
    w_c = w_c.reshape(ns, 8 * 2 * p, L * 128)
    w_c = w_c.astype(BF16)
    al_r, al_i = pw_r[L], pw_i[L]
    dec_a = jnp.concatenate([al_r, al_r], axis=-1).reshape(ns, 1, 8 * 2 * p)
    dec_b = jnp.concatenate([-al_i, al_i], axis=-1).reshape(ns, 1, 8 * 2 * p)
    d_t = jnp.tile(d_skip.reshape(ns, 1, 128), (1, 1, L))
    gl = jnp.einsum('sghk,gf->sghfk', glu_w.reshape(ns, 8, hch, hch), eye).reshape(ns, 128, 128)
    glu2 = jnp.einsum('ab,shk->sahbk', jnp.eye(2, dtype=F32), gl).reshape(ns, 256, 256).astype(BF16)
    glu_bt = jnp.tile(glu_b.reshape(ns, 1, 128), (1, 1, L))
    return k2, w_z, w_c, dec_a, dec_b, d_t, glu2, glu_bt


def _s5_kernel(u_ref, k2_ref, wz_ref, wc_ref, da_ref, db_ref, d_ref, g2_ref, gb_ref, o_ref,
               zp_ref, zq_ref, sp_ref, *, nchunk):
    nblk = S5_CHUNK // 2
    ub = [u_ref[:, j * 256:(j + 1) * 256] for j in range(nblk)]
    z = _dot(ub[0], wz_ref[0])
    for j in range(1, nblk):
        z = z + _dot(ub[j], wz_ref[j])
    zp_ref[...] = z
    half = S5_STATE
    zq_ref[...] = jnp.concatenate([pltpu.roll(z[:, s * 2 * half:(s + 1) * 2 * half], half, 1)
                                   for s in range(z.shape[1] // (2 * half))], axis=1)
    da = da_ref[...]
    db = db_ref[...]
    dbq = -db

    def step(c, carry):
        sp, sq = carry
        sp_ref[pl.ds(c, 1), :] = sp
        sp_new = da * sp + db * sq + zp_ref[pl.ds(c, 1), :]
        sq_new = da * sq + dbq * sp + zq_ref[pl.ds(c, 1), :]
        return sp_new, sq_new

    zero = jnp.zeros((1, z.shape[1]), F32)
    lax.fori_loop(0, nchunk, step, (zero, zero))
    sh, sl = _split_bf16(sp_ref[...])
    wc = wc_ref[...]
    y_inter = _dot(sh, wc) + _dot(sl, wc)
    for i in range(nblk):
        cs = slice(i * 256, (i + 1) * 256)
        acc = y_inter[:, cs] + d_ref[:, cs] * ub[i].astype(F32)
        for j in range(i + 1):
            acc = acc + _dot(ub[j], k2_ref[i - j])
        y = jax.nn.gelu(acc)
        y = y * jax.nn.sigmoid(_dot(y.astype(BF16), g2_ref[...]) + gb_ref[:, cs])
        for a in range(2):
            o_ref[pl.ds(2 * i + a, nchunk, stride=S5_CHUNK), :] = y[:, a * 128:(a + 1) * 128]


def _s5_mixer(u2, weights, bsz, seq):
    k2, w_z, w_c, dec_a, dec_b, d_t, glu2, glu_bt = weights
    ns = k2.shape[0]
    nchunk = seq // S5_CHUNK
    cols = S5_CHUNK * 128
    st = w_z.shape[-1]
    slab3 = lambda b, s: (s, 0, 0)
    slab4 = lambda b, s: (s, 0, 0, 0)
    return pl.pallas_call(
        functools.partial(_s5_kernel, nchunk=nchunk),
        grid=(bsz, ns),
        in_specs=[pl.BlockSpec((nchunk, cols), lambda b, s: (b, s)),
                  pl.BlockSpec((None,) + k2.shape[1:], slab4),
                  pl.BlockSpec((None,) + w_z.shape[1:], slab4),
                  pl.BlockSpec((None, st, cols), slab3),
                  pl.BlockSpec((None, 1, st), slab3),
                  pl.BlockSpec((None, 1, st), slab3),
                  pl.BlockSpec((None, 1, cols), slab3),
                  pl.BlockSpec((None, 256, 256), slab3),
                  pl.BlockSpec((None, 1, cols), slab3)],
        out_specs=pl.BlockSpec((seq, 128), lambda b, s: (b, s)),
        out_shape=jax.ShapeDtypeStruct((bsz * seq, ns * 128), F32),
        scratch_shapes=[pltpu.VMEM((nchunk, st), F32), pltpu.VMEM((nchunk, st), F32), pltpu.VMEM((nchunk, st), F32)],
        compiler_params=_cparams(("arbitrary", "arbitrary"), VMEM_LIMIT),
        name="s5_mixer",
    )(u2, k2, w_z, w_c, dec_a, dec_b, d_t, glu2, glu_bt)


def _dsa_kernel(qt_ref, qit_ref, wt_ref, k_ref, vt_ref, ki_ref, o_ref,
                sc_ref, s_ref, m_ref, l_ref, acc_ref, *, tq, ck, k_sel, idx_bits, max_iter):
    i = pl.program_id(1)
    nheads = qt_ref.shape[0]
    n_ck = (i * tq + tq + ck - 1) // ck
    t_pos = i * tq + lax.broadcasted_iota(jnp.int32, (1, tq), 1)
    krow = lax.broadcasted_iota(jnp.int32, (ck, 1), 0)
    kf = float(k_sel)
    inf = jnp.inf

    fr = 32
    sub = min(32768 // tq, ck)

    def fold(x, op):
        return op(x.reshape(ck // fr, fr, tq), axis=0)

    wt = wt_ref[...]

    def score_chunk(c, carry):
        lo8, hi8 = carry
        for part in range(ck // sub):
            off = pl.multiple_of(c * ck + part * sub, sub)
            kic = ki_ref[pl.ds(off, sub), :]
            acc = jnp.zeros((sub, tq), F32)
            for h in range(nheads):
                acc = acc + wt[h:h + 1, :] * jnp.maximum(_dot(kic, qit_ref[h]), 0.0)
            acc = acc + 0.0
            vis = off + krow[:sub] <= t_pos
            sc_ref[c, part * sub:(part + 1) * sub, :] = jnp.where(vis, acc, -inf)
            hi8 = jnp.maximum(hi8, jnp.max(jnp.where(vis, acc, -inf).reshape(sub // fr, fr, tq), axis=0))
            lo8 = jnp.minimum(lo8, jnp.min(jnp.where(vis, acc, inf).reshape(sub // fr, fr, tq), axis=0))
        return lo8, hi8

    lo8, hi8 = lax.fori_loop(0, n_ck, score_chunk, (jnp.full((fr, tq), inf, F32), jnp.full((fr, tq), -inf, F32)))
    lo = jnp.min(lo8, axis=0, keepdims=True)
    hi = jnp.max(hi8, axis=0, keepdims=True)
    n_vis = (t_pos + 1).astype(F32)
    few = n_vis < kf

    def reduce_keys(fn, init, combine, op):
        def body(c, part):
            return combine(part, fold(fn(sc_ref[c], c * ck), op))
        return op(lax.fori_loop(0, n_ck, body, jnp.full((fr, tq), init, F32)), axis=0, keepdims=True)

    def count(pred):
        return reduce_keys(lambda x, off: jnp.where(pred(x, off), 1.0, 0.0), 0.0, jnp.add, jnp.sum)

    def bisect(lo_, hi_, c_lo):
        mid = lo_ + (hi_ - lo_) * 0.5
        cnt = count(lambda x, off: x >= mid)
        up = cnt >= kf
        return jnp.where(up, mid, lo_), jnp.where(up, hi_, mid), jnp.where(up, cnt, c_lo)

    def status(lo_, hi_, c_lo):
        mid = lo_ + (hi_ - lo_) * 0.5
        unresolved = jnp.where(few, 0.0, jnp.where(c_lo == kf, 0.0, 1.0))
        movable = jnp.where(mid > lo_, jnp.where(mid < hi_, 1.0, 0.0), 0.0)
        return jnp.max(unresolved), jnp.max(unresolved * movable)

    def search_body(st):
        it, lo_, hi_, c_lo, _, _ = st
        for _ in range(2):
            lo_, hi_, c_lo = bisect(lo_, hi_, c_lo)
        return (it + 2, lo_, hi_, c_lo) + status(lo_, hi_, c_lo)

    st = lax.while_loop(lambda st: jnp.logical_and(st[0] < max_iter, st[5] > 0.5), search_body,
                        (jnp.int32(0), lo, hi, n_vis) + status(lo, hi, n_vis))
    lo, open_ = st[1], st[4]

    def resolve():
        def walk(thr_, strict):
            keep = (lambda x: x > thr_) if strict else (lambda x: x >= thr_)
            return reduce_keys(lambda x, off: jnp.where(keep(x), x, inf), inf, jnp.minimum, jnp.min)

        def above_equal(thr_):
            return count(lambda x, off: x > thr_), count(lambda x, off: x == thr_)

        def too_low(n_gt):
            return jnp.where(few, 0.0, jnp.where(n_gt >= kf, 1.0, 0.0))

        thr0 = walk(lo, False)

        def walk_body(st):
            thr_, n_gt, _ = st
            thr2 = jnp.where(too_low(n_gt) > 0.5, walk(thr_, True), thr_)
            return (thr2,) + above_equal(thr2)

        thr_, n_gt, n_eq = lax.while_loop(lambda st: jnp.max(too_low(st[1])) > 0.5, walk_body,
                                          (thr0,) + above_equal(thr0))
        need = kf - n_gt

        def tie_cut():
            def mark_equal(c, _):
                s_ref[0, c] = jnp.where(sc_ref[c] == thr_, 1.0, 0.0)
                return 0

            lax.fori_loop(0, n_ck, mark_equal, 0)

            def idx_step(b, cur):
                trial = cur | (jnp.int32(1) << (idx_bits - 1 - b))

                def body(c, part):
                    return part + fold(jnp.where(c * ck + krow < trial, s_ref[0, c], 0.0), jnp.sum)
                cnt = jnp.sum(lax.fori_loop(0, n_ck, body, jnp.zeros((fr, tq), F32)), axis=0, keepdims=True)
                return jnp.where(cnt < need, trial, cur)

            return lax.fori_loop(0, idx_bits, idx_step, jnp.zeros((1, tq), jnp.int32))

        tied = jnp.max(jnp.where(few, 0.0, jnp.where(n_eq > need, 1.0, 0.0))) > 0.5
        return thr_, lax.cond(tied, tie_cut, lambda: jnp.full((1, tq), 0x7FFFFFFF, jnp.int32))

    thr, cut = lax.cond(open_ > 0.5, resolve, lambda: (lo, jnp.full((1, tq), 0x7FFFFFFF, jnp.int32)))

    for h in range(nheads):
        m_ref[h] = jnp.full((fr, tq), NEG_BIG, F32)
        l_ref[h] = jnp.zeros((fr, tq), F32)
        acc_ref[h] = jnp.zeros((HEAD_DIM, tq), F32)

    def attn_scores(c, _):
        off = pl.multiple_of(c * ck, ck)
        kc = k_ref[pl.ds(off, ck), :]
        x = sc_ref[c]
        at_thr = jnp.where(x == thr, jnp.where(off + krow <= cut, 0.0, NEG_BIG), NEG_BIG)
        bias = jnp.where(x > thr, 0.0, at_thr)
        for h in range(nheads):
            s = _dot(kc, qt_ref[h]) + bias
            s_ref[h, c] = s
            m_ref[h] = jnp.maximum(m_ref[h], fold(s, jnp.max))
        return 0

    lax.fori_loop(0, n_ck, attn_scores, 0)
    for h in range(nheads):
        m_ref[h] = jnp.broadcast_to(jnp.max(m_ref[h], axis=0, keepdims=True), (fr, tq))

    def attn_values(c, _):
        vt = vt_ref[c]
        for h in range(nheads):
            p = jnp.exp(s_ref[h, c] - m_ref[h][:1, :])
            l_ref[h] += fold(p, jnp.sum)
            acc_ref[h] += _dot(vt, p.astype(BF16))
        return 0

    lax.fori_loop(0, n_ck, attn_values, 0)
    for h in range(nheads):
        l_row = jnp.sum(l_ref[h], axis=0, keepdims=True)
        o_ref[h * HEAD_DIM:(h + 1) * HEAD_DIM, :] = (acc_ref[h] / l_row).astype(o_ref.dtype)


def _dsa_attention(qt, qit, wt, k, v, ki, bsz, seq):
    tq = 256
    ck = min(512, seq)
    n_ck = seq // ck
    k_sel = min(TOPK_MAX, seq // 4)
    nq = seq // tq
    nh = ATTN_HEADS
    width = nh * HEAD_DIM
    vt = v.reshape(bsz, n_ck, ck, HEAD_DIM).transpose(0, 1, 3, 2)
    qspec = pl.BlockSpec((None, nh, HEAD_DIM, tq), lambda b, i: (b, 0, 0, i))
    out_t = pl.pallas_call(
        functools.partial(_dsa_kernel, tq=tq, ck=ck, k_sel=k_sel, idx_bits=(seq - 1).bit_length(), max_iter=22),
        grid=(bsz, nq),
        in_specs=[qspec, qspec,
                  pl.BlockSpec((IDX_HEADS, tq), lambda b, i: (0, b * nq + i)),
                  pl.BlockSpec((seq, HEAD_DIM), lambda b, i: (b, 0)),
                  pl.BlockSpec((None, n_ck, HEAD_DIM, ck), lambda b, i: (b, 0, 0, 0)),
                  pl.BlockSpec((seq, HEAD_DIM), lambda b, i: (b, 0))],
        out_specs=pl.BlockSpec((None, width, tq), lambda b, i: (b, 0, i)),
        out_shape=jax.ShapeDtypeStruct((bsz, width, seq), BF16),
        scratch_shapes=[pltpu.VMEM((n_ck, ck, tq), F32),
                        pltpu.VMEM((nh, n_ck, ck, tq), F32),
                        pltpu.VMEM((nh, 32, tq), F32),
                        pltpu.VMEM((nh, 32, tq), F32),
                        pltpu.VMEM((nh, HEAD_DIM, tq), F32)],
        compiler_params=_cparams(("arbitrary", "arbitrary"), VMEM_LIMIT),
        name="dsa_attention",
    )(qt, qit, wt, k, vt, ki)
    return out_t.transpose(0, 2, 1).reshape(bsz * seq, width)


def _outproj_kernel(x_ref, ys_ref, ya_ref, w_ref, mod_ref, o_ref, *, d):
    width = ys_ref.shape[1]
    y = _dot(ys_ref[...].astype(BF16), w_ref[:width, :]) + _dot(ya_ref[...], w_ref[width:, :])
    o_ref[...] = x_ref[...] + mod_ref[...][:, 2 * d:] * y


def _out_projection(x2, y_ssm, y_att, w_out, mod, layer, seq):
    t, d = x2.shape
    tm = min(1024, seq)
    width = y_ssm.shape[1]
    tiles_per_seq = seq // tm
    row = lambda i: (i, 0)
    return pl.pallas_call(
        functools.partial(_outproj_kernel, d=d),
        grid=(t // tm,),
        in_specs=[pl.BlockSpec((tm, d), row),
                  pl.BlockSpec((tm, width), row),
                  pl.BlockSpec((tm, width), row),
                  pl.BlockSpec((2 * width, d), lambda i: (0, 0)),
                  pl.BlockSpec((None, None, 1, 3 * d), lambda i: (layer, i // tiles_per_seq, 0, 0))],
        out_specs=pl.BlockSpec((tm, d), row),
        out_shape=jax.ShapeDtypeStruct((t, d), F32),
        compiler_params=_cparams(("arbitrary",), VMEM_LIMIT),
        name="out_projection",
    )(x2, y_ssm, y_att, w_out.astype(BF16), mod)


def _pool_kernel(x_ref, halo_ref, g_ref, mod_ref, pw_ref, ps_ref, o_ref, h_ref, fa_ref, fb_ref,
                 *, d, tiles_per_seq):
    i = pl.program_id(0)
    tm = x_ref.shape[0]
    pad, rows = POOL_HALO, POOL_HALO + tm
    mod = mod_ref[...]
    shift, scale, gate = mod[:, :d], mod[:, d:2 * d], mod[:, 2 * d:]
    x = x_ref[...]
    first = (i % tiles_per_seq) == 0
    gw = d // len(POOL_WINDOWS)
    h_ref[:pad, :] = jnp.zeros((pad, d), F32)
    fa_ref[:pad, :] = jnp.zeros((pad, gw), F32)
    fb_ref[:pad, :] = jnp.zeros((pad, gw), F32)
    h_ref[2 * pad:, :] = _norm_mod(x, g_ref[...], shift, scale)
    halo = _norm_mod(halo_ref[...], g_ref[...], shift, scale)
    h_ref[pad:2 * pad, :] = jnp.where(first, 0.0, halo)
    pos = (i % tiles_per_seq) * tm + lax.broadcasted_iota(jnp.int32, (tm, 1), 0)
    ys = []
    for gi, win in enumerate(POOL_WINDOWS):
        cs = slice(gi * gw, (gi + 1) * gw)
        cur = h_ref[2 * pad:, cs]
        src, width, bufs = None, 1, (fa_ref, fb_ref)
        while width < win:
            dst = bufs[0] if src is not bufs[0] else bufs[1]
            if src is None:
                dst[pad:, :] = h_ref[pad:, cs] + h_ref[pad - width:pad - width + rows, cs]
            else:
                dst[pad:, :] = src[pad:, :] + src[pad - width:pad - width + rows, :]
            src, width = dst, 2 * width
        tot = src[2 * pad:, :]
        cnt = jnp.minimum(pos + 1, win).astype(F32)
        pooled = tot / cnt - cur
        ys.append(_dot(pooled.astype(BF16), pw_ref[gi]))
    y = jnp.concatenate(ys, axis=1) * ps_ref[...]
    o_ref[...] = x + gate * y


def _pool_layer(x2, g, mod, layer, pool_w, pool_scale, seq):
    t, d = x2.shape
    tm = min(1024, seq)
    tiles_per_seq = seq // tm
    gw = d // len(POOL_WINDOWS)
    hb = tm // POOL_HALO
    return pl.pallas_call(
        functools.partial(_pool_kernel, d=d, tiles_per_seq=tiles_per_seq),
        grid=(t // tm,),
        in_specs=[pl.BlockSpec((tm, d), lambda i: (i, 0)),
                  pl.BlockSpec((POOL_HALO, d), lambda i: (jnp.maximum(i * hb - 1, 0), 0)),
                  pl.BlockSpec((1, d), lambda i: (0, 0)),
                  pl.BlockSpec((None, None, 1, 3 * d), lambda i: (layer, i // tiles_per_seq, 0, 0)),
                  pl.BlockSpec((len(POOL_WINDOWS), gw, gw), lambda i: (0, 0, 0)),
                  pl.BlockSpec((1, d), lambda i: (0, 0))],
        out_specs=pl.BlockSpec((tm, d), lambda i: (i, 0)),
        out_shape=jax.ShapeDtypeStruct((t, d), F32),
        scratch_shapes=[pltpu.VMEM((tm + 2 * POOL_HALO, d), F32),
                        pltpu.VMEM((tm + 2 * POOL_HALO, gw), F32),
                        pltpu.VMEM((tm + 2 * POOL_HALO, gw), F32)],
        compiler_params=_cparams(("arbitrary",), VMEM_LIMIT),
        name="pool_mixer",
    )(x2, x2, g.reshape(1, d), mod, pool_w.astype(BF16), pool_scale.reshape(1, d))


def _first_max(vals, idx, big):
    m = jnp.max(vals, axis=0, keepdims=True)
    first = jnp.min(jnp.where(vals == m, idx, big), axis=0, keepdims=True)
    return m, first


def _pack_bf16_pairs(x):
    n = x.shape[1] // 2
    lo = pltpu.bitcast(x[:, :n].astype(BF16).astype(F32), jnp.int32)
    hi = pltpu.bitcast(x[:, n:].astype(BF16).astype(F32), jnp.int32)
    return hi | ((lo >> 16) & 0xFFFF)


def _unpack_bf16_pairs(w):
    lo = pltpu.bitcast(w << 16, F32)
    hi = pltpu.bitcast(w & jnp.int32(-65536), F32)
    return jnp.concatenate([lo, hi], axis=1)


def _router_kernel(x_ref, g_ref, mod_ref, rw_ref, rb_ref, tri_ref, h_ref, gate_ref, rank_ref, cnt_ref,
                   run_ref, *, d, n_exp):
    mod = mod_ref[...]
    h = _norm_mod(x_ref[...], g_ref[...], mod[:, :d], mod[:, d:2 * d])
    h_ref[...] = _pack_bf16_pairs(h)
    tm = h.shape[0]
    hh, hl = _split_bf16(h)
    rw = rw_ref[...]
    rh, rl = _split_bf16(rw)
    logits = _dot_nt(rh, hh) + (_dot_nt(rh, hl) + _dot_nt(rl, hh))
    scores = jax.nn.sigmoid(logits)
    sel = scores + rb_ref[...]
    gsz = n_exp // N_EXPERT_GROUPS
    neg = -jnp.inf
    shape3 = (N_EXPERT_GROUPS, gsz, tm)
    sel3 = sel.reshape(shape3)
    sub = lax.broadcasted_iota(jnp.int32, shape3, 1)
    m1 = jnp.max(sel3, axis=1, keepdims=True)
    f1 = jnp.min(jnp.where(sel3 == m1, sub, gsz), axis=1, keepdims=True)
    m2 = jnp.max(jnp.where(sub == f1, neg, sel3), axis=1, keepdims=True)
    gscore = jnp.broadcast_to(m1 + m2, shape3).reshape(n_exp, tm)
    eidx = lax.broadcasted_iota(jnp.int32, (n_exp, tm), 0)
    gidx = eidx // gsz
    keep = jnp.zeros((n_exp, tm), F32)
    work = gscore
    for _ in range(TOPK_GROUPS):
        _, first = _first_max(work, gidx, N_EXPERT_GROUPS)
        hit = gidx == first
        keep = jnp.where(hit, 1.0, keep)
        work = jnp.where(hit, neg, work)
    work = jnp.where(keep > 0.0, sel, neg)
    chosen = jnp.zeros((n_exp, tm), F32)
    for _ in range(TOP_K):
        _, first = _first_max(work, eidx, n_exp)
        hit = eidx == first
        chosen = jnp.where(hit, 1.0, chosen)
        work = jnp.where(hit, neg, work)
    picked = chosen * scores
    gate_ref[...] = picked / jnp.sum(picked, axis=0, keepdims=True) * ROUTED_SCALE

    @pl.when(pl.program_id(0) == 0)
    def _():
        run_ref[...] = jnp.zeros(run_ref.shape, F32)

    before = _dot(chosen.astype(BF16), tri_ref[...])
    run = run_ref[...]
    rank_ref[...] = jnp.where(chosen > 0.0, before + run[:, :1], -1.0)
    run = run + jnp.broadcast_to(jnp.sum(chosen, axis=1, keepdims=True), run.shape)
    run_ref[...] = run
    cnt_ref[...] = run


MOE_TILE = 512
ROUTER_TILE = 1024


def _ffn_router(x2, g, mod, layer, router_w, router_bias, seq, tile0, ntiles):
    d = x2.shape[1]
    n_exp = router_w.shape[1]
    tm = ROUTER_TILE
    t = ntiles * tm
    tiles_per_seq = seq // tm
    tri = (np.arange(tm)[:, None] < np.arange(tm)[None, :]).astype(np.float32)
    return pl.pallas_call(
        functools.partial(_router_kernel, d=d, n_exp=n_exp),
        grid=(ntiles,),
        in_specs=[pl.BlockSpec((tm, d), lambda i: (i + tile0, 0)),
                  pl.BlockSpec((1, d), lambda i: (0, 0)),
                  pl.BlockSpec((None, None, 1, 3 * d), lambda i: (layer, (i + tile0) // tiles_per_seq, 0, 0)),
                  pl.BlockSpec((n_exp, d), lambda i: (0, 0)),
                  pl.BlockSpec((n_exp, 1), lambda i: (0, 0)),
                  pl.BlockSpec((tm, tm), lambda i: (0, 0))],
        out_specs=[pl.BlockSpec((tm, d // 2), lambda i: (i, 0)),
                   pl.BlockSpec((n_exp, tm), lambda i: (0, i)),
                   pl.BlockSpec((n_exp, tm), lambda i: (0, i)),
                   pl.BlockSpec((n_exp, 128), lambda i: (0, 0))],
        out_shape=[jax.ShapeDtypeStruct((t, d // 2), jnp.int32),
                   jax.ShapeDtypeStruct((n_exp, t), F32),
                   jax.ShapeDtypeStruct((n_exp, t), F32),
                   jax.ShapeDtypeStruct((n_exp, 128), F32)],
        scratch_shapes=[pltpu.VMEM((n_exp, 128), F32)],
        compiler_params=_cparams(("arbitrary",)),
        name="ffn_router",
    )(x2, g.reshape(1, d), mod, router_w.T, router_bias.reshape(n_exp, 1), jnp.asarray(tri, BF16))


def _assign_kernel(rank_ref, gate_ref, start_ref, pos_ref, w_ref, *, n_exp):
    rank = rank_ref[...]
    gates = gate_ref[...]
    tm = rank.shape[1]
    slot = rank + start_ref[...]
    eidx = lax.broadcasted_iota(jnp.int32, (n_exp, tm), 0).astype(F32)
    alive = jnp.where(rank >= 0.0, eidx, float(n_exp))
    kidx = lax.broadcasted_iota(jnp.int32, (TOP_K, tm), 0)
    pos = jnp.zeros((TOP_K, tm), F32)
    wts = jnp.zeros((TOP_K, tm), F32)
    for k in range(TOP_K):
        first = jnp.min(alive, axis=0, keepdims=True)
        hit = alive == first
        pos_k = jnp.sum(jnp.where(hit, slot, 0.0), axis=0, keepdims=True)
        w_k = jnp.sum(jnp.where(hit, gates, 0.0), axis=0, keepdims=True)
        pos = jnp.where(kidx == k, pos_k, pos)
        wts = jnp.where(kidx == k, w_k, wts)
        alive = jnp.where(hit, float(n_exp), alive)
    pos_ref[...] = pos.astype(jnp.int32)
    w_ref[...] = wts


def _assign_slots(ranks, gates, start):
    n_exp, t = ranks.shape
    tm = min(2048, t)
    return pl.pallas_call(
        functools.partial(_assign_kernel, n_exp=n_exp),
        grid=(t // tm,),
        in_specs=[pl.BlockSpec((n_exp, tm), lambda i: (0, i)),
                  pl.BlockSpec((n_exp, tm), lambda i: (0, i)),
                  pl.BlockSpec((n_exp, 1), lambda i: (0, 0))],
        out_specs=[pl.BlockSpec((TOP_K, tm), lambda i: (0, i)),
                   pl.BlockSpec((TOP_K, tm), lambda i: (0, i))],
        out_shape=[jax.ShapeDtypeStruct((TOP_K, t), jnp.int32), jax.ShapeDtypeStruct((TOP_K, t), F32)],
        compiler_params=_cparams(("arbitrary",)),
        name="moe_assign",
    )(ranks, gates, start)


SC_CORES = 2
SC_SUBCORES = 16
SC_WINDOW = 128


def _sc_mesh():
    return plsc.VectorSubcoreMesh(core_axis_name="c", subcore_axis_name="s",
                                  num_cores=SC_CORES, num_subcores=SC_SUBCORES)


def _sc_scatter_rows(rows, pos, n_out):
    t, width = rows.shape
    nk = pos.shape[0]
    win = SC_WINDOW
    n_win = t // win // (SC_CORES * SC_SUBCORES)
    pos_w = pos.reshape(nk, t // win, win).transpose(1, 0, 2)

    def body(rows_hbm, pos_hbm, out_hbm, idx_v, buf_v, sem):
        wid = lax.axis_index("s") * SC_CORES + lax.axis_index("c")

        @pl.loop(0, n_win)
        def _(j):
            w = wid * n_win + j
            pltpu.sync_copy(rows_hbm.at[pl.ds(w * win, win)], buf_v)
            pltpu.sync_copy(pos_hbm.at[w], idx_v)
            copies = [pltpu.make_async_copy(buf_v, out_hbm.at[idx_v.at[k]], sem) for k in range(nk)]
            for cp in copies:
                cp.start()
            for cp in copies:
                cp.wait()

    return pl.kernel(
        body, out_type=jax.ShapeDtypeStruct((n_out, width), jnp.int32), mesh=_sc_mesh(),
        scratch_types=[pltpu.VMEM((nk, win), jnp.int32), pltpu.VMEM((win, width), jnp.int32),
                       pltpu.SemaphoreType.DMA],
        name="sc_scatter_rows",
    )(rows, pos_w)


SC_GATHER_WINDOW = 64


def _sc_gather_rows(table, pos_flat):
    m = pos_flat.shape[0]
    width = table.shape[1]
    win = SC_GATHER_WINDOW
    workers = SC_CORES * SC_SUBCORES
    n_win = m // win // workers
    pos_w = pos_flat.reshape(workers, n_win, win)

    def body(table_hbm, pos_hbm, out_hbm, idx_v, buf_v, sem_g, sem_w):
        wid = lax.axis_index("s") * SC_CORES + lax.axis_index("c")
        base = wid * (n_win * win)
        pltpu.sync_copy(pos_hbm.at[wid], idx_v)

        def gather(j, b):
            return pltpu.make_async_copy(table_hbm.at[idx_v.at[j]], buf_v.at[b], sem_g.at[b])

        def write(j, b):
            return pltpu.make_async_copy(buf_v.at[b], out_hbm.at[pl.ds(base + j * win, win)], sem_w.at[b])

        gather(0, 0).start()

        @pl.loop(0, n_win, step=2)
        def _(j0):
            for b in range(2):
                j = j0 + b
                gather(j, b).wait()
                write(j, b).start()

                @pl.when(j + 1 < n_win)
                def _():
                    @pl.when(j >= 1)
                    def _():
                        write(j - 1, 1 - b).wait()
                    gather(j + 1, 1 - b).start()

        write(n_win - 2, 0).wait()
        write(n_win - 1, 1).wait()

    return pl.kernel(
        body, out_type=jax.ShapeDtypeStruct((m, width), jnp.int32), mesh=_sc_mesh(),
        scratch_types=[pltpu.VMEM((n_win, win), jnp.int32), pltpu.VMEM((2, win, width), jnp.int32),
                       pltpu.SemaphoreType.DMA((2,)), pltpu.SemaphoreType.DMA((2,))],
        name="sc_gather_rows",
    )(table, pos_w)


MOE_BLOCK = 1024


def _silu_mul(a, b):
    return (a * jax.nn.sigmoid(a)) * b


def _expert_ffn_kernel(be_ref, nv_ref, xs_ref, wg_ref, wu_ref, wd_ref, ys_ref, wg_bf, wu_bf, wd_bf):
    i = pl.program_id(0)
    fresh = jnp.logical_or(i == 0, be_ref[i] != be_ref[jnp.maximum(i - 1, 0)])

    @pl.when(fresh)
    def _():
        wg_bf[...] = wg_ref[...].astype(BF16)
        wu_bf[...] = wu_ref[...].astype(BF16)
        wd_bf[...] = wd_ref[...].astype(BF16)

    nv = nv_ref[i]

    @pl.when(nv > 0)
    def _():
        x = _unpack_bf16_pairs(xs_ref[...])
        row = lax.broadcasted_iota(jnp.int32, (x.shape[0], 1), 0)
        x = jnp.where(row < nv, x, 0.0).astype(BF16)
        mid = _silu_mul(_dot(x, wg_bf[...]), _dot(x, wu_bf[...]))
        ys_ref[...] = _pack_bf16_pairs(_dot(mid.astype(BF16), wd_bf[...]))

    @pl.when(nv <= 0)
    def _():
        ys_ref[...] = jnp.zeros(ys_ref.shape, ys_ref.dtype)


def _expert_ffn(xs, block_expert, block_rows, layer, w_gate, w_up, w_down):
    n_rows, half = xs.shape
    _, n_exp, d, de = w_gate.shape
    nb = n_rows // MOE_BLOCK
    wmap = lambda i, be, nv: (layer, be[i], 0, 0)
    grid_spec = pltpu.PrefetchScalarGridSpec(
        num_scalar_prefetch=2,
        grid=(nb,),
        in_specs=[pl.BlockSpec((MOE_BLOCK, half), lambda i, be, nv: (jnp.where(nv[i] > 0, i, 0), 0)),
                  pl.BlockSpec((None, None, d, de), wmap),
                  pl.BlockSpec((None, None, d, de), wmap),
                  pl.BlockSpec((None, None, de, d), wmap)],
        out_specs=pl.BlockSpec((MOE_BLOCK, half), lambda i, be, nv: (i, 0)),
        scratch_shapes=[pltpu.VMEM((d, de), BF16), pltpu.VMEM((d, de), BF16), pltpu.VMEM((de, d), BF16)],
    )
    return pl.pallas_call(
        _expert_ffn_kernel,
        grid_spec=grid_spec,
        out_shape=jax.ShapeDtypeStruct((n_rows, half), jnp.int32),
        compiler_params=_cparams(("arbitrary",), VMEM_LIMIT),
        name="moe_expert_ffn",
    )(block_expert, block_rows, xs, w_gate, w_up, w_down)


def _combine_kernel(yt_ref, wt_ref, hp_ref, x_ref, mod_ref, sg_ref, su_ref, sd_ref, *rest, d):
    o_ref = rest[-1]
    h = _unpack_bf16_pairs(hp_ref[...]).astype(BF16)
    acc = _dot(_silu_mul(_dot(h, sg_ref[...]), _dot(h, su_ref[...])).astype(BF16), sd_ref[...])
    w = wt_ref[...]
    for k in range(TOP_K):
        acc = acc + w[:, k:k + 1] * _unpack_bf16_pairs(yt_ref[k])
    o_ref[...] = x_ref[...] + mod_ref[...][:, 2 * d:] * acc


def _combine(y_tok, w_tok, h_packed, x2, out_prev, mod, layer, sh_gate, sh_up, sh_down, seq, tile0):
    t_all, d = x2.shape
    ds = sh_gate.shape[1]
    tm = MOE_TILE
    ntiles = h_packed.shape[0] // tm
    tiles_per_seq = seq // tm
    row = lambda i: (i, 0)
    off = lambda i: (i + tile0, 0)
    const = lambda i: (0, 0)
    in_specs = [pl.BlockSpec((TOP_K, tm, d // 2), lambda i: (0, i, 0)),
                pl.BlockSpec((tm, TOP_K), row),
                pl.BlockSpec((tm, d // 2), row),
                pl.BlockSpec((tm, d), off),
                pl.BlockSpec((None, None, 1, 3 * d), lambda i: (layer, (i + tile0) // tiles_per_seq, 0, 0)),
                pl.BlockSpec((d, ds), const),
                pl.BlockSpec((d, ds), const),
                pl.BlockSpec((ds, d), const)]
    args = [y_tok, w_tok, h_packed, x2, mod, sh_gate.astype(BF16), sh_up.astype(BF16), sh_down.astype(BF16)]
    aliases = {}
    if out_prev is not None:
        in_specs.append(pl.BlockSpec(memory_space=pl.ANY))
        args.append(out_prev)
        aliases = {len(args) - 1: 0}
    return pl.pallas_call(
        functools.partial(_combine_kernel, d=d),
        grid=(ntiles,),
        in_specs=in_specs,
        out_specs=pl.BlockSpec((tm, d), off),
        out_shape=jax.ShapeDtypeStruct((t_all, d), F32),
        input_output_aliases=aliases,
        compiler_params=_cparams(("arbitrary",), VMEM_LIMIT),
        name="moe_combine",
    )(*args)


def _moe_group(x2, out_prev, tile0, h_packed, gates, ranks, counts, mod, layer, w_gate, w_up, w_down,
               sh_gate, sh_up, sh_down, seq):
    t, half = h_packed.shape
    d = 2 * half
    n_exp = w_gate.shape[1]
    cnt = counts[:, 0].astype(jnp.int32)
    padded = (cnt + MOE_BLOCK - 1) // MOE_BLOCK * MOE_BLOCK
    eidx = jnp.arange(n_exp, dtype=jnp.int32)
    seg_end = jnp.sum(jnp.where(eidx[None, :] <= eidx[:, None], padded[None, :], 0), axis=1)
    seg_start = seg_end - padded
    nb = t * TOP_K // MOE_BLOCK + n_exp
    first_row = jnp.arange(nb, dtype=jnp.int32) * MOE_BLOCK
    block_expert = jnp.sum((seg_end[None, :] <= first_row[:, None]).astype(jnp.int32), axis=1)
    block_expert = jnp.minimum(block_expert, n_exp - 1)
    last_row = jnp.sum(jnp.where(block_expert[:, None] == eidx[None, :], (seg_start + cnt)[None, :], 0), axis=1)
    block_rows = jnp.clip(last_row - first_row, 0, MOE_BLOCK).astype(jnp.int32)
    pos, w = _assign_slots(ranks, gates, seg_start.astype(F32).reshape(n_exp, 1))
    xs = _sc_scatter_rows(h_packed, pos, nb * MOE_BLOCK)
    ys = _expert_ffn(xs, block_expert, block_rows, layer, w_gate, w_up, w_down)
    y_tok = _sc_gather_rows(ys, pos.reshape(TOP_K * t)).reshape(TOP_K, t, d // 2)
    return _combine(y_tok, w.T, h_packed, x2, out_prev, mod, layer, sh_gate, sh_up, sh_down, seq, tile0)


MOE_GROUPS = 1


def _moe_layer(x2, g, mod, layer, router_w, router_bias, w_gate, w_up, w_down, sh_gate, sh_up, sh_down, seq):
    ntiles = x2.shape[0] // MOE_TILE // MOE_GROUPS
    rtiles = ntiles * MOE_TILE // ROUTER_TILE
    routed = [_ffn_router(x2, g, mod, layer, router_w, router_bias, seq, grp * rtiles, rtiles)
              for grp in range(MOE_GROUPS)]
    out = None
    for grp in range(MOE_GROUPS):
        out = _moe_group(x2, out, grp * ntiles, *routed[grp], mod, layer, w_gate, w_up, w_down,
                         sh_gate, sh_up, sh_down, seq)
    return out


def kernel(x, c, positions, mix_norm_g, mix_mod_w, mix_mod_b, ffn_norm_g, ffn_mod_w, ffn_mod_b,
           hyb_w_in, hyb_w_out, s5_lambda_re, s5_lambda_im, s5_log_dt, s5_b_re, s5_b_im,
           s5_c_re, s5_c_im, s5_d, s5_glu_w, s5_glu_b, attn_q_norm_g, attn_k_norm_g,
           pool_w, pool_scale, router_w, router_bias, exp_w_gate, exp_w_up, exp_w_down,
           sh_w_gate, sh_w_up, sh_w_down):
    bsz, seq, d = x.shape
    t = bsz * seq
    depth = mix_norm_g.shape[0]
    x2 = x.reshape(t, d)
    mix_mod = _mod_vectors(c, mix_mod_w, mix_mod_b).reshape(depth, bsz, 1, 3 * d)
    ffn_mod = _mod_vectors(c, ffn_mod_w, ffn_mod_b).reshape(depth, bsz, 1, 3 * d)
    for i in range(depth):
        j = i // 2
        if i % 2 == 0:
            u, q_hm, qi_hm, k, v, ki, wi = _in_projection(
                x2, mix_norm_g[i], mix_mod, i, hyb_w_in[j], positions, attn_q_norm_g[j], attn_k_norm_g[j], bsz, seq)
            weights = _s5_weights(s5_lambda_re[j], s5_lambda_im[j], s5_log_dt[j], s5_b_re[j], s5_b_im[j],
                                  s5_c_re[j], s5_c_im[j], s5_d[j], s5_glu_w[j], s5_glu_b[j])
            y_ssm = _s5_mixer(u, weights, bsz, seq)
            y_att = _dsa_attention(q_hm, qi_hm, wi, k, v, ki, bsz, seq)
            x2 = _out_projection(x2, y_ssm, y_att, hyb_w_out[j], mix_mod, i, seq)
        else:
            x2 = _pool_layer(x2, mix_norm_g[i], mix_mod, i, pool_w[j], pool_scale[j], seq)
        x2 = _moe_layer(x2, ffn_norm_g[i], ffn_mod, i, router_w[i], router_bias[i], exp_w_gate, exp_w_up, exp_w_down,
                        sh_w_gate[i], sh_w_up[i], sh_w_down[i], seq)
    return x2.reshape(bsz, seq, d)
```

```python
import functools

import numpy as np
import jax
import jax.numpy as jnp
from jax import lax
from jax.experimental import pallas as pl
from jax.experimental.pallas import tpu as pltpu
from jax.experimental.pallas import tpu_sc as plsc

F32 = jnp.float32
BF16 = jnp.bfloat16

EPS = 1e-6
S5_GROUP = 16
S5_STATE = 64
HEAD_DIM = 64
ATTN_HEADS = 8
IDX_HEADS = 8
ROPE_HALF = 8
ROPE_THETA = 500000.0
TOPK_MAX = 256
POOL_WINDOWS = (2, 4, 8, 16)
POOL_HALO = 16
N_EXPERT_GROUPS = 8
TOPK_GROUPS = 4
TOP_K = 8
ROUTED_SCALE = 2.5
S5_CHUNK = 16
NEG_BIG = -1e30
VMEM_LIMIT = 56 * 1024 * 1024


def _cparams(sem, vmem=None):
    return pltpu.CompilerParams(dimension_semantics=sem, vmem_limit_bytes=vmem)


def _dot(a, b):
    return jnp.dot(a, b, preferred_element_type=F32)


def _dot_nt(a, b):
    return lax.dot_general(a, b, (((1,), (1,)), ((), ())), preferred_element_type=F32)


def _split_bf16(a):
    hi = a.astype(BF16)
    lo = (a - hi.astype(F32)).astype(BF16)
    return hi, lo


def _norm_mod(x, g, shift, scale):
    y = x * lax.rsqrt(jnp.mean(x * x, axis=-1, keepdims=True) + EPS)
    return (y * g) * (1.0 + scale) + shift


def _mod_kernel(ct_ref, w_ref, b_ref, o_ref):
    ct = ct_ref[...]
    cs = ct * jax.nn.sigmoid(ct)
    w = w_ref[...]
    rows = [jnp.sum(w * cs[:, b:b + 1], axis=0, keepdims=True) for b in range(ct.shape[1])]
    o_ref[...] = jnp.concatenate(rows, axis=0) + b_ref[...]


def _mod_vectors(c, w, b):
    nl, d, n3 = w.shape
    bsz = c.shape[0]
    tn = 512
    return pl.pallas_call(
        _mod_kernel,
        grid=(nl, n3 // tn),
        in_specs=[pl.BlockSpec((d, bsz), lambda l, j: (0, 0)),
                  pl.BlockSpec((None, d, tn), lambda l, j: (l, 0, j)),
                  pl.BlockSpec((None, 1, tn), lambda l, j: (l, 0, j))],
        out_specs=pl.BlockSpec((None, bsz, tn), lambda l, j: (l, 0, j)),
        out_shape=jax.ShapeDtypeStruct((nl, bsz, n3), F32),
        compiler_params=_cparams(("arbitrary", "arbitrary")),
        name="mod_vectors",
    )(c.T, w, b.reshape(nl, 1, n3))


def _inproj_kernel(x_ref, g_ref, mod_ref, w_ref, pos_ref, inv_ref, qg_ref, kg_ref,
                   u_ref, q_ref, qi_ref, k_ref, v_ref, ki_ref, wi_ref, u_scr, *, d):
    x = x_ref[...]
    mod = mod_ref[...]
    h = _norm_mod(x, g_ref[...], mod[:, :d], mod[:, d:2 * d])
    proj = _dot(h.astype(BF16), w_ref[...])
    tm = x.shape[0]
    width = ATTN_HEADS * HEAD_DIM

    ang = inv_ref[...] * pos_ref[...].astype(F32)
    cos, sin = jnp.cos(ang), jnp.sin(ang)

    def head_t(xt, gain, scale):
        if scale is not None:
            r = lax.rsqrt(jnp.mean(xt * xt, axis=0, keepdims=True) + EPS) * scale
            xt = xt * gain
        x1, x2 = xt[:ROPE_HALF], xt[ROPE_HALF:2 * ROPE_HALF]
        out = jnp.concatenate([x1 * cos - x2 * sin, x1 * sin + x2 * cos, xt[2 * ROPE_HALF:]], axis=0)
        return out if scale is None else out * r

    for slab in range(width // 128):
        u_scr[slab] = proj[:, slab * 128:(slab + 1) * 128]
        for j in range(S5_CHUNK):
            piece = u_scr[slab, pl.ds(j, tm // S5_CHUNK, stride=S5_CHUNK), :]
            col = (slab * S5_CHUNK + j) * 128
            u_ref[:, col:col + 128] = piece.astype(u_ref.dtype)
    q = proj[:, width:2 * width]
    qi = proj[:, 2 * width:3 * width]
    small = proj[:, 3 * width:3 * width + 256]
    qg = qg_ref[...]
    for pair in range(ATTN_HEADS // 2):
        sl = slice(pair * 128, (pair + 1) * 128)
        qt = q[:, sl].T
        qit = qi[:, sl].T
        for half in range(2):
            hs = slice(half * HEAD_DIM, (half + 1) * HEAD_DIM)
            q_ref[2 * pair + half] = head_t(qt[hs], qg, HEAD_DIM ** -0.5).astype(q_ref.dtype)
            qi_ref[2 * pair + half] = head_t(qit[hs], None, None).astype(qi_ref.dtype)
    kvt = small[:, :128].T
    kt = head_t(kvt[:HEAD_DIM], kg_ref[...], 1.0)
    kv = jnp.concatenate([kt, kvt[HEAD_DIM:]], axis=0).T
    k_ref[...] = kv[:, :HEAD_DIM].astype(k_ref.dtype)
    v_ref[...] = kv[:, HEAD_DIM:].astype(v_ref.dtype)
    kiwt = small[:, 128:256].T
    kit = jnp.concatenate([head_t(kiwt[:HEAD_DIM], None, None), kiwt[HEAD_DIM:]], axis=0).T
    ki_ref[...] = kit[:, :HEAD_DIM].astype(ki_ref.dtype)
    wscale = (IDX_HEADS ** -0.5) * (HEAD_DIM ** -0.5)
    wi_ref[...] = kiwt[HEAD_DIM:HEAD_DIM + IDX_HEADS] * wscale


def _in_projection(x2, g, mod, layer, w_in, positions, q_norm_g, k_norm_g, bsz, seq):
    t, d = x2.shape
    tm = min(1024, seq)
    width = ATTN_HEADS * HEAD_DIM
    cuts = np.cumsum([width, width, HEAD_DIM, HEAD_DIM, width, HEAD_DIM, IDX_HEADS])
    u_w, q_w, k_w, v_w, qi_w, ki_w, wi_w = jnp.split(w_in, cuts[:-1].tolist(), axis=1)
    pad = jnp.zeros((d, 128 - HEAD_DIM - IDX_HEADS), w_in.dtype)
    w = jnp.concatenate([u_w, q_w, qi_w, k_w, v_w, ki_w, wi_w, pad], axis=1).astype(BF16)
    nw = w.shape[1]
    inv = np.power(ROPE_THETA, -2.0 * np.arange(ROPE_HALF) / (2 * ROPE_HALF))
    inv = jnp.asarray(inv.reshape(ROPE_HALF, 1), F32)
    tiles_per_seq = seq // tm
    row = lambda i: (i, 0)
    const = lambda i: (0, 0)
    hm = lambda i: (i // tiles_per_seq, 0, 0, i % tiles_per_seq)
    outs = pl.pallas_call(
        functools.partial(_inproj_kernel, d=d),
        grid=(t // tm,),
        in_specs=[pl.BlockSpec((tm, d), row),
                  pl.BlockSpec((1, d), const),
                  pl.BlockSpec((None, None, 1, 3 * d), lambda i: (layer, i // tiles_per_seq, 0, 0)),
                  pl.BlockSpec((d, nw), const),
                  pl.BlockSpec((None, 1, tm), lambda i: (i, 0, 0)),
                  pl.BlockSpec((ROPE_HALF, 1), const),
                  pl.BlockSpec((HEAD_DIM, 1), const),
                  pl.BlockSpec((HEAD_DIM, 1), const)],
        out_specs=[pl.BlockSpec((tm // S5_CHUNK, width * S5_CHUNK), row),
                   pl.BlockSpec((None, ATTN_HEADS, HEAD_DIM, tm), hm),
                   pl.BlockSpec((None, IDX_HEADS, HEAD_DIM, tm), hm),
                   pl.BlockSpec((tm, HEAD_DIM), row),
                   pl.BlockSpec((tm, HEAD_DIM), row),
                   pl.BlockSpec((tm, HEAD_DIM), row),
                   pl.BlockSpec((IDX_HEADS, tm), lambda i: (0, i))],
        out_shape=[jax.ShapeDtypeStruct((t // S5_CHUNK, width * S5_CHUNK), BF16),
                   jax.ShapeDtypeStruct((bsz, ATTN_HEADS, HEAD_DIM, seq), BF16),
                   jax.ShapeDtypeStruct((bsz, IDX_HEADS, HEAD_DIM, seq), BF16),
                   jax.ShapeDtypeStruct((t, HEAD_DIM), BF16),
                   jax.ShapeDtypeStruct((t, HEAD_DIM), BF16),
                   jax.ShapeDtypeStruct((t, HEAD_DIM), BF16),
                   jax.ShapeDtypeStruct((IDX_HEADS, t), F32)],
        scratch_shapes=[pltpu.VMEM((width // 128, tm, 128), F32)],
        compiler_params=_cparams(("arbitrary",), VMEM_LIMIT),
        name="in_projection",
    )(x2, g.reshape(1, d), mod, w, positions.reshape(t // tm, 1, tm), inv,
      q_norm_g.reshape(HEAD_DIM, 1), k_norm_g.reshape(HEAD_DIM, 1))
    return outs


def _s5_weights(lam_re, lam_im, log_dt, b_re, b_im, c_re, c_im, d_skip, glu_w, glu_b):
    L = S5_CHUNK
    g, p = lam_re.shape
    hch = S5_GROUP
    dt = jnp.exp(log_dt)[:, None]
    lr, li = lam_re, lam_im
    tau = jnp.arange(L + 1, dtype=F32)[:, None, None]
    mag = jnp.exp(lr * dt * tau)
    pw_r, pw_i = mag * jnp.cos(li * dt * tau), mag * jnp.sin(li * dt * tau)
    ar, ai = pw_r[1], pw_i[1]
    nr, ni = ar - 1.0, ai
    den = lr * lr + li * li
    cr, ci = (nr * lr + ni * li) / den, (ni * lr - nr * li) / den
    bb_r = cr[..., None] * b_re - ci[..., None] * b_im
    bb_i = cr[..., None] * b_im + ci[..., None] * b_re
    cl_r = c_re[None] * pw_r[:L, :, None, :] - c_im[None] * pw_i[:L, :, None, :]
    cl_i = c_re[None] * pw_i[:L, :, None, :] + c_im[None] * pw_r[:L, :, None, :]
    taps = (jnp.einsum('tghp,gpk->tghk', cl_r, bb_r) - jnp.einsum('tghp,gpk->tghk', cl_i, bb_i))
    ns = g // 8
    eye = jnp.eye(8, dtype=F32)
    kd = jnp.einsum('tsgoh,gf->tsghfo', taps.reshape(L, ns, 8, hch, hch), eye).reshape(L, ns, 128, 128)
    zero = jnp.zeros_like(kd[0])
    k2 = jnp.stack([jnp.concatenate([jnp.concatenate([kd[2 * dd], kd[2 * dd + 1]], axis=-1),
                                     jnp.concatenate([kd[2 * dd - 1] if dd else zero, kd[2 * dd]], axis=-1)], axis=-2)
                    for dd in range(L // 2)], axis=1).astype(BF16)
    ii = jnp.arange(L)
    rev_r, rev_i = pw_r[L - 1 - ii], pw_i[L - 1 - ii]
    z_r = rev_r[..., None] * bb_r[None] - rev_i[..., None] * bb_i[None]
    z_i = rev_r[..., None] * bb_i[None] + rev_i[..., None] * bb_r[None]
    zc = jnp.concatenate([z_r, z_i], axis=2)
    zc = jnp.transpose(zc.reshape(L, ns, 8, 2 * p, hch), (1, 0, 2, 4, 3)).reshape(ns, L * 128, 2 * p)
    row_group = (jnp.arange(L * 128) // hch) % 8
    state_group = jnp.arange(8 * 2 * p) // (2 * p)
    w_z = jnp.where((row_group[:, None] == state_group[None, :])[None], jnp.concatenate([zc] * 8, axis=-1), 0.0)
    w_z = w_z.reshape(ns, L // 2, 2 * 128, 8 * 2 * p).astype(BF16)
    co_r = c_re[None] * pw_r[1:, :, None, :] - c_im[None] * pw_i[1:, :, None, :]
    co_i = c_re[None] * pw_i[1:, :, None, :] + c_im[None] * pw_r[1:, :, None, :]
    cc = jnp.concatenate([co_r, -co_i], axis=-1)
    cc = jnp.transpose(cc.reshape(L, ns, 8, hch, 2 * p), (1, 4, 0, 2, 3)).reshape(ns, 1, 2 * p, L * 128)
    col_group = (jnp.arange(L * 128) // hch) % 8
    w_c = jnp.where((jnp.arange(8)[:, None] == col_group[None, :])[None, :, None, :], cc, 0.0)
    w_c = w_c.reshape(ns, 8 * 2 * p, L * 128)
    w_c = w_c.astype(BF16)
    al_r, al_i = pw_r[L], pw_i[L]
    dec_a = jnp.concatenate([al_r, al_r], axis=-1).reshape(ns, 1, 8 * 2 * p)
    dec_b = jnp.concatenate([-al_i, al_i], axis=-1).reshape(ns, 1, 8 * 2 * p)
    d_t = jnp.tile(d_skip.reshape(ns, 1, 128), (1, 1, L))
    gl = jnp.einsum('sghk,gf->sghfk', glu_w.reshape(ns, 8, hch, hch), eye).reshape(ns, 128, 128)
    glu2 = jnp.einsum('ab,shk->sahbk', jnp.eye(2, dtype=F32), gl).reshape(ns, 256, 256).astype(BF16)
    glu_bt = jnp.tile(glu_b.reshape(ns, 1, 128), (1, 1, L))
    return k2, w_z, w_c, dec_a, dec_b, d_t, glu2, glu_bt


def _s5_kernel(u_ref, k2_ref, wz_ref, wc_ref, da_ref, db_ref, d_ref, g2_ref, gb_ref, o_ref,
               zp_ref, zq_ref, sp_ref, *, nchunk):
    nblk = S5_CHUNK // 2
    ub = [u_ref[:, j * 256:(j + 1) * 256] for j in range(nblk)]
    z = _dot(ub[0], wz_ref[0])
    for j in range(1, nblk):
        z = z + _dot(ub[j], wz_ref[j])
    zp_ref[...] = z
    half = S5_STATE
    zq_ref[...] = jnp.concatenate([pltpu.roll(z[:, s * 2 * half:(s + 1) * 2 * half], half, 1)
                                   for s in range(z.shape[1] // (2 * half))], axis=1)
    da = da_ref[...]
    db = db_ref[...]
    dbq = -db

    def step(c, carry):
        sp, sq = carry
        sp_ref[pl.ds(c, 1), :] = sp
        sp_new = da * sp + db * sq + zp_ref[pl.ds(c, 1), :]
        sq_new = da * sq + dbq * sp + zq_ref[pl.ds(c, 1), :]
        return sp_new, sq_new

    zero = jnp.zeros((1, z.shape[1]), F32)
    lax.fori_loop(0, nchunk, step, (zero, zero))
    sh, sl = _split_bf16(sp_ref[...])
    wc = wc_ref[...]
    y_inter = _dot(sh, wc) + _dot(sl, wc)
    for i in range(nblk):
        cs = slice(i * 256, (i + 1) * 256)
        acc = y_inter[:, cs] + d_ref[:, cs] * ub[i].astype(F32)
        for j in range(i + 1):
            acc = acc + _dot(ub[j], k2_ref[i - j])
        y = jax.nn.gelu(acc)
        y = y * jax.nn.sigmoid(_dot(y.astype(BF16), g2_ref[...]) + gb_ref[:, cs])
        for a in range(2):
            o_ref[pl.ds(2 * i + a, nchunk, stride=S5_CHUNK), :] = y[:, a * 128:(a + 1) * 128]


def _s5_mixer(u2, weights, bsz, seq):
    k2, w_z, w_c, dec_a, dec_b, d_t, glu2, glu_bt = weights
    ns = k2.shape[0]
    nchunk = seq // S5_CHUNK
    cols = S5_CHUNK * 128
    st = w_z.shape[-1]
    slab3 = lambda b, s: (s, 0, 0)
    slab4 = lambda b, s: (s, 0, 0, 0)
    return pl.pallas_call(
        functools.partial(_s5_kernel, nchunk=nchunk),
        grid=(bsz, ns),
        in_specs=[pl.BlockSpec((nchunk, cols), lambda b, s: (b, s)),
                  pl.BlockSpec((None,) + k2.shape[1:], slab4),
                  pl.BlockSpec((None,) + w_z.shape[1:], slab4),
                  pl.BlockSpec((None, st, cols), slab3),
                  pl.BlockSpec((None, 1, st), slab3),
                  pl.BlockSpec((None, 1, st), slab3),
                  pl.BlockSpec((None, 1, cols), slab3),
                  pl.BlockSpec((None, 256, 256), slab3),
                  pl.BlockSpec((None, 1, cols), slab3)],
        out_specs=pl.BlockSpec((seq, 128), lambda b, s: (b, s)),
        out_shape=jax.ShapeDtypeStruct((bsz * seq, ns * 128), F32),
        scratch_shapes=[pltpu.VMEM((nchunk, st), F32), pltpu.VMEM((nchunk, st), F32), pltpu.VMEM((nchunk, st), F32)],
        compiler_params=_cparams(("arbitrary", "arbitrary"), VMEM_LIMIT),
        name="s5_mixer",
    )(u2, k2, w_z, w_c, dec_a, dec_b, d_t, glu2, glu_bt)


def _dsa_kernel(qt_ref, qit_ref, wt_ref, k_ref, vt_ref, ki_ref, o_ref,
                sc_ref, s_ref, m_ref, l_ref, acc_ref, *, tq, ck, k_sel, idx_bits, max_iter):
    i = pl.program_id(1)
    nheads = qt_ref.shape[0]
    n_ck = (i * tq + tq + ck - 1) // ck
    t_pos = i * tq + lax.broadcasted_iota(jnp.int32, (1, tq), 1)
    krow = lax.broadcasted_iota(jnp.int32, (ck, 1), 0)
    kf = float(k_sel)
    inf = jnp.inf

    fr = 32
    sub = min(32768 // tq, ck)

    def fold(x, op):
        return op(x.reshape(ck // fr, fr, tq), axis=0)

    wt = wt_ref[...]

    def score_chunk(c, carry):
        lo8, hi8 = carry
        for part in range(ck // sub):
            off = pl.multiple_of(c * ck + part * sub, sub)
            kic = ki_ref[pl.ds(off, sub), :]
            acc = jnp.zeros((sub, tq), F32)
            for h in range(nheads):
                acc = acc + wt[h:h + 1, :] * jnp.maximum(_dot(kic, qit_ref[h]), 0.0)
            acc = acc + 0.0
            vis = off + krow[:sub] <= t_pos
            sc_ref[c, part * sub:(part + 1) * sub, :] = jnp.where(vis, acc, -inf)
            hi8 = jnp.maximum(hi8, jnp.max(jnp.where(vis, acc, -inf).reshape(sub // fr, fr, tq), axis=0))
            lo8 = jnp.minimum(lo8, jnp.min(jnp.where(vis, acc, inf).reshape(sub // fr, fr, tq), axis=0))
        return lo8, hi8

    lo8, hi8 = lax.fori_loop(0, n_ck, score_chunk, (jnp.full((fr, tq), inf, F32), jnp.full((fr, tq), -inf, F32)))
    lo = jnp.min(lo8, axis=0, keepdims=True)
    hi = jnp.max(hi8, axis=0, keepdims=True)
    n_vis = (t_pos + 1).astype(F32)
    few = n_vis < kf

    def reduce_keys(fn, init, combine, op):
        def body(c, part):
            return combine(part, fold(fn(sc_ref[c], c * ck), op))
        return op(lax.fori_loop(0, n_ck, body, jnp.full((fr, tq), init, F32)), axis=0, keepdims=True)

    def count(pred):
        return reduce_keys(lambda x, off: jnp.where(pred(x, off), 1.0, 0.0), 0.0, jnp.add, jnp.sum)

    def bisect(lo_, hi_, c_lo):
        mid = lo_ + (hi_ - lo_) * 0.5
        cnt = count(lambda x, off: x >= mid)
        up = cnt >= kf
        return jnp.where(up, mid, lo_), jnp.where(up, hi_, mid), jnp.where(up, cnt, c_lo)

    def status(lo_, hi_, c_lo):
        mid = lo_ + (hi_ - lo_) * 0.5
        unresolved = jnp.where(few, 0.0, jnp.where(c_lo == kf, 0.0, 1.0))
        movable = jnp.where(mid > lo_, jnp.where(mid < hi_, 1.0, 0.0), 0.0)
        return jnp.max(unresolved), jnp.max(unresolved * movable)

    def search_body(st):
        it, lo_, hi_, c_lo, _, _ = st
        for _ in range(2):
            lo_, hi_, c_lo = bisect(lo_, hi_, c_lo)
        return (it + 2, lo_, hi_, c_lo) + status(lo_, hi_, c_lo)

    st = lax.while_loop(lambda st: jnp.logical_and(st[0] < max_iter, st[5] > 0.5), search_body,
                        (jnp.int32(0), lo, hi, n_vis) + status(lo, hi, n_vis))
    lo, open_ = st[1], st[4]

    def resolve():
        def walk(thr_, strict):
            keep = (lambda x: x > thr_) if strict else (lambda x: x >= thr_)
            return reduce_keys(lambda x, off: jnp.where(keep(x), x, inf), inf, jnp.minimum, jnp.min)

        def above_equal(thr_):
            return count(lambda x, off: x > thr_), count(lambda x, off: x == thr_)

        def too_low(n_gt):
            return jnp.where(few, 0.0, jnp.where(n_gt >= kf, 1.0, 0.0))

        thr0 = walk(lo, False)

        def walk_body(st):
            thr_, n_gt, _ = st
            thr2 = jnp.where(too_low(n_gt) > 0.5, walk(thr_, True), thr_)
            return (thr2,) + above_equal(thr2)

        thr_, n_gt, n_eq = lax.while_loop(lambda st: jnp.max(too_low(st[1])) > 0.5, walk_body,
                                          (thr0,) + above_equal(thr0))
        need = kf - n_gt

        def tie_cut():
            def mark_equal(c, _):
                s_ref[0, c] = jnp.where(sc_ref[c] == thr_, 1.0, 0.0)
                return 0

            lax.fori_loop(0, n_ck, mark_equal, 0)

            def idx_step(b, cur):
                trial = cur | (jnp.int32(1) << (idx_bits - 1 - b))

                def body(c, part):
                    return part + fold(jnp.where(c * ck + krow < trial, s_ref[0, c], 0.0), jnp.sum)
                cnt = jnp.sum(lax.fori_loop(0, n_ck, body, jnp.zeros((fr, tq), F32)), axis=0, keepdims=True)
                return jnp.where(cnt < need, trial, cur)

            return lax.fori_loop(0, idx_bits, idx_step, jnp.zeros((1, tq), jnp.int32))

        tied = jnp.max(jnp.where(few, 0.0, jnp.where(n_eq > need, 1.0, 0.0))) > 0.5
        return thr_, lax.cond(tied, tie_cut, lambda: jnp.full((1, tq), 0x7FFFFFFF, jnp.int32))

    thr, cut = lax.cond(open_ > 0.5, resolve, lambda: (lo, jnp.full((1, tq), 0x7FFFFFFF, jnp.int32)))

    for h in range(nheads):
        m_ref[h] = jnp.full((fr, tq), NEG_BIG, F32)
        l_ref[h] = jnp.zeros((fr, tq), F32)
        acc_ref[h] = jnp.zeros((HEAD_DIM, tq), F32)

    def attn_scores(c, _):
        off = pl.multiple_of(c * ck, ck)
        kc = k_ref[pl.ds(off, ck), :]
        x = sc_ref[c]
        at_thr = jnp.where(x == thr, jnp.where(off + krow <= cut, 0.0, NEG_BIG), NEG_BIG)
        bias = jnp.where(x > thr, 0.0, at_thr)
        for h in range(nheads):
            s = _dot(kc, qt_ref[h]) + bias
            s_ref[h, c] = s
            m_ref[h] = jnp.maximum(m_ref[h], fold(s, jnp.max))
        return 0

    lax.fori_loop(0, n_ck, attn_scores, 0)
    for h in range(nheads):
        m_ref[h] = jnp.broadcast_to(jnp.max(m_ref[h], axis=0, keepdims=True), (fr, tq))

    def attn_values(c, _):
        vt = vt_ref[c]
        for h in range(nheads):
            p = jnp.exp(s_ref[h, c] - m_ref[h][:1, :])
            l_ref[h] += fold(p, jnp.sum)
            acc_ref[h] += _dot(vt, p.astype(BF16))
        return 0

    lax.fori_loop(0, n_ck, attn_values, 0)
    for h in range(nheads):
        l_row = jnp.sum(l_ref[h], axis=0, keepdims=True)
        o_ref[h * HEAD_DIM:(h + 1) * HEAD_DIM, :] = (acc_ref[h] / l_row).astype(o_ref.dtype)


def _dsa_attention(qt, qit, wt, k, v, ki, bsz, seq):
    tq = 256
    ck = min(512, seq)
    n_ck = seq // ck
    k_sel = min(TOPK_MAX, seq // 4)
    nq = seq // tq
    nh = ATTN_HEADS
    width = nh * HEAD_DIM
    vt = v.reshape(bsz, n_ck, ck, HEAD_DIM).transpose(0, 1, 3, 2)
    qspec = pl.BlockSpec((None, nh, HEAD_DIM, tq), lambda b, i: (b, 0, 0, i))
    out_t = pl.pallas_call(
        functools.partial(_dsa_kernel, tq=tq, ck=ck, k_sel=k_sel, idx_bits=(seq - 1).bit_length(), max_iter=22),
        grid=(bsz, nq),
        in_specs=[qspec, qspec,
                  pl.BlockSpec((IDX_HEADS, tq), lambda b, i: (0, b * nq + i)),
                  pl.BlockSpec((seq, HEAD_DIM), lambda b, i: (b, 0)),
                  pl.BlockSpec((None, n_ck, HEAD_DIM, ck), lambda b, i: (b, 0, 0, 0)),
                  pl.BlockSpec((seq, HEAD_DIM), lambda b, i: (b, 0))],
        out_specs=pl.BlockSpec((None, width, tq), lambda b, i: (b, 0, i)),
        out_shape=jax.ShapeDtypeStruct((bsz, width, seq), BF16),
        scratch_shapes=[pltpu.VMEM((n_ck, ck, tq), F32),
                        pltpu.VMEM((nh, n_ck, ck, tq), F32),
                        pltpu.VMEM((nh, 32, tq), F32),
                        pltpu.VMEM((nh, 32, tq), F32),
                        pltpu.VMEM((nh, HEAD_DIM, tq), F32)],
        compiler_params=_cparams(("arbitrary", "arbitrary"), VMEM_LIMIT),
        name="dsa_attention",
    )(qt, qit, wt, k, vt, ki)
    return out_t.transpose(0, 2, 1).reshape(bsz * seq, width)


def _outproj_kernel(x_ref, ys_ref, ya_ref, w_ref, mod_ref, o_ref, *, d):
    width = ys_ref.shape[1]
    y = _dot(ys_ref[...].astype(BF16), w_ref[:width, :]) + _dot(ya_ref[...], w_ref[width:, :])
    o_ref[...] = x_ref[...] + mod_ref[...][:, 2 * d:] * y


def _out_projection(x2, y_ssm, y_att, w_out, mod, layer, seq):
    t, d = x2.shape
    tm = min(1024, seq)
    width = y_ssm.shape[1]
    tiles_per_seq = seq // tm
    row = lambda i: (i, 0)
    return pl.pallas_call(
        functools.partial(_outproj_kernel, d=d),
        grid=(t // tm,),
        in_specs=[pl.BlockSpec((tm, d), row),
                  pl.BlockSpec((tm, width), row),
                  pl.BlockSpec((tm, width), row),
                  pl.BlockSpec((2 * width, d), lambda i: (0, 0)),
                  pl.BlockSpec((None, None, 1, 3 * d), lambda i: (layer, i // tiles_per_seq, 0, 0))],
        out_specs=pl.BlockSpec((tm, d), row),
        out_shape=jax.ShapeDtypeStruct((t, d), F32),
        compiler_params=_cparams(("arbitrary",), VMEM_LIMIT),
        name="out_projection",
    )(x2, y_ssm, y_att, w_out.astype(BF16), mod)


def _pool_kernel(x_ref, halo_ref, g_ref, mod_ref, pw_ref, ps_ref, o_ref, h_ref, fa_ref, fb_ref,
                 *, d, tiles_per_seq):
    i = pl.program_id(0)
    tm = x_ref.shape[0]
    pad, rows = POOL_HALO, POOL_HALO + tm
    mod = mod_ref[...]
    shift, scale, gate = mod[:, :d], mod[:, d:2 * d], mod[:, 2 * d:]
    x = x_ref[...]
    first = (i % tiles_per_seq) == 0
    gw = d // len(POOL_WINDOWS)
    h_ref[:pad, :] = jnp.zeros((pad, d), F32)
    fa_ref[:pad, :] = jnp.zeros((pad, gw), F32)
    fb_ref[:pad, :] = jnp.zeros((pad, gw), F32)
    h_ref[2 * pad:, :] = _norm_mod(x, g_ref[...], shift, scale)
    halo = _norm_mod(halo_ref[...], g_ref[...], shift, scale)
    h_ref[pad:2 * pad, :] = jnp.where(first, 0.0, halo)
    pos = (i % tiles_per_seq) * tm + lax.broadcasted_iota(jnp.int32, (tm, 1), 0)
    ys = []
    for gi, win in enumerate(POOL_WINDOWS):
        cs = slice(gi * gw, (gi + 1) * gw)
        cur = h_ref[2 * pad:, cs]
        src, width, bufs = None, 1, (fa_ref, fb_ref)
        while width < win:
            dst = bufs[0] if src is not bufs[0] else bufs[1]
            if src is None:
                dst[pad:, :] = h_ref[pad:, cs] + h_ref[pad - width:pad - width + rows, cs]
            else:
                dst[pad:, :] = src[pad:, :] + src[pad - width:pad - width + rows, :]
            src, width = dst, 2 * width
        tot = src[2 * pad:, :]
        cnt = jnp.minimum(pos + 1, win).astype(F32)
        pooled = tot / cnt - cur
        ys.append(_dot(pooled.astype(BF16), pw_ref[gi]))
    y = jnp.concatenate(ys, axis=1) * ps_ref[...]
    o_ref[...] = x + gate * y


def _pool_layer(x2, g, mod, layer, pool_w, pool_scale, seq):
    t, d = x2.shape
    tm = min(1024, seq)
    tiles_per_seq = seq // tm
    gw = d // len(POOL_WINDOWS)
    hb = tm // POOL_HALO
    return pl.pallas_call(
        functools.partial(_pool_kernel, d=d, tiles_per_seq=tiles_per_seq),
        grid=(t // tm,),
        in_specs=[pl.BlockSpec((tm, d), lambda i: (i, 0)),
                  pl.BlockSpec((POOL_HALO, d), lambda i: (jnp.maximum(i * hb - 1, 0), 0)),
                  pl.BlockSpec((1, d), lambda i: (0, 0)),
                  pl.BlockSpec((None, None, 1, 3 * d), lambda i: (layer, i // tiles_per_seq, 0, 0)),
                  pl.BlockSpec((len(POOL_WINDOWS), gw, gw), lambda i: (0, 0, 0)),
                  pl.BlockSpec((1, d), lambda i: (0, 0))],
        out_specs=pl.BlockSpec((tm, d), lambda i: (i, 0)),
        out_shape=jax.ShapeDtypeStruct((t, d), F32),
        scratch_shapes=[pltpu.VMEM((tm + 2 * POOL_HALO, d), F32),
                        pltpu.VMEM((tm + 2 * POOL_HALO, gw), F32),
                        pltpu.VMEM((tm + 2 * POOL_HALO, gw), F32)],
        compiler_params=_cparams(("arbitrary",), VMEM_LIMIT),
        name="pool_mixer",
    )(x2, x2, g.reshape(1, d), mod, pool_w.astype(BF16), pool_scale.reshape(1, d))


def _first_max(vals, idx, big):
    m = jnp.max(vals, axis=0, keepdims=True)
    first = jnp.min(jnp.where(vals == m, idx, big), axis=0, keepdims=True)
    return m, first


def _pack_bf16_pairs(x):
    n = x.shape[1] // 2
    lo = pltpu.bitcast(x[:, :n].astype(BF16).astype(F32), jnp.int32)
    hi = pltpu.bitcast(x[:, n:].astype(BF16).astype(F32), jnp.int32)
    return hi | ((lo >> 16) & 0xFFFF)


def _unpack_bf16_pairs(w):
    lo = pltpu.bitcast(w << 16, F32)
    hi = pltpu.bitcast(w & jnp.int32(-65536), F32)
    return jnp.concatenate([lo, hi], axis=1)


def _router_kernel(x_ref, g_ref, mod_ref, rw_ref, rb_ref, tri_ref, h_ref, gate_ref, rank_ref, cnt_ref,
                   run_ref, *, d, n_exp):
    mod = mod_ref[...]
    h = _norm_mod(x_ref[...], g_ref[...], mod[:, :d], mod[:, d:2 * d])
    h_ref[...] = _pack_bf16_pairs(h)
    tm = h.shape[0]
    hh, hl = _split_bf16(h)
    rw = rw_ref[...]
    rh, rl = _split_bf16(rw)
    logits = _dot_nt(rh, hh) + (_dot_nt(rh, hl) + _dot_nt(rl, hh))
    scores = jax.nn.sigmoid(logits)
    sel = scores + rb_ref[...]
    gsz = n_exp // N_EXPERT_GROUPS
    neg = -jnp.inf
    shape3 = (N_EXPERT_GROUPS, gsz, tm)
    sel3 = sel.reshape(shape3)
    sub = lax.broadcasted_iota(jnp.int32, shape3, 1)
    m1 = jnp.max(sel3, axis=1, keepdims=True)
    f1 = jnp.min(jnp.where(sel3 == m1, sub, gsz), axis=1, keepdims=True)
    m2 = jnp.max(jnp.where(sub == f1, neg, sel3), axis=1, keepdims=True)
    gscore = jnp.broadcast_to(m1 + m2, shape3).reshape(n_exp, tm)
    eidx = lax.broadcasted_iota(jnp.int32, (n_exp, tm), 0)
    gidx = eidx // gsz
    keep = jnp.zeros((n_exp, tm), F32)
    work = gscore
    for _ in range(TOPK_GROUPS):
        _, first = _first_max(work, gidx, N_EXPERT_GROUPS)
        hit = gidx == first
        keep = jnp.where(hit, 1.0, keep)
        work = jnp.where(hit, neg, work)
    work = jnp.where(keep > 0.0, sel, neg)
    chosen = jnp.zeros((n_exp, tm), F32)
    for _ in range(TOP_K):
        _, first = _first_max(work, eidx, n_exp)
        hit = eidx == first
        chosen = jnp.where(hit, 1.0, chosen)
        work = jnp.where(hit, neg, work)
    picked = chosen * scores
    gate_ref[...] = picked / jnp.sum(picked, axis=0, keepdims=True) * ROUTED_SCALE

    @pl.when(pl.program_id(0) == 0)
    def _():
        run_ref[...] = jnp.zeros(run_ref.shape, F32)

    before = _dot(chosen.astype(BF16), tri_ref[...])
    run = run_ref[...]
    rank_ref[...] = jnp.where(chosen > 0.0, before + run[:, :1], -1.0)
    run = run + jnp.broadcast_to(jnp.sum(chosen, axis=1, keepdims=True), run.shape)
    run_ref[...] = run
    cnt_ref[...] = run


MOE_TILE = 512
ROUTER_TILE = 1024


def _ffn_router(x2, g, mod, layer, router_w, router_bias, seq, tile0, ntiles):
    d = x2.shape[1]
    n_exp = router_w.shape[1]
    tm = ROUTER_TILE
    t = ntiles * tm
    tiles_per_seq = seq // tm
    tri = (np.arange(tm)[:, None] < np.arange(tm)[None, :]).astype(np.float32)
    return pl.pallas_call(
        functools.partial(_router_kernel, d=d, n_exp=n_exp),
        grid=(ntiles,),
        in_specs=[pl.BlockSpec((tm, d), lambda i: (i + tile0, 0)),
                  pl.BlockSpec((1, d), lambda i: (0, 0)),
                  pl.BlockSpec((None, None, 1, 3 * d), lambda i: (layer, (i + tile0) // tiles_per_seq, 0, 0)),
                  pl.BlockSpec((n_exp, d), lambda i: (0, 0)),
                  pl.BlockSpec((n_exp, 1), lambda i: (0, 0)),
                  pl.BlockSpec((tm, tm), lambda i: (0, 0))],
        out_specs=[pl.BlockSpec((tm, d // 2), lambda i: (i, 0)),
                   pl.BlockSpec((n_exp, tm), lambda i: (0, i)),
                   pl.BlockSpec((n_exp, tm), lambda i: (0, i)),
                   pl.BlockSpec((n_exp, 128), lambda i: (0, 0))],
        out_shape=[jax.ShapeDtypeStruct((t, d // 2), jnp.int32),
                   jax.ShapeDtypeStruct((n_exp, t), F32),
                   jax.ShapeDtypeStruct((n_exp, t), F32),
                   jax.ShapeDtypeStruct((n_exp, 128), F32)],
        scratch_shapes=[pltpu.VMEM((n_exp, 128), F32)],
        compiler_params=_cparams(("arbitrary",)),
        name="ffn_router",
    )(x2, g.reshape(1, d), mod, router_w.T, router_bias.reshape(n_exp, 1), jnp.asarray(tri, BF16))


def _assign_kernel(rank_ref, gate_ref, start_ref, pos_ref, w_ref, *, n_exp):
    rank = rank_ref[...]
    gates = gate_ref[...]
    tm = rank.shape[1]
    slot = rank + start_ref[...]
    eidx = lax.broadcasted_iota(jnp.int32, (n_exp, tm), 0).astype(F32)
    alive = jnp.where(rank >= 0.0, eidx, float(n_exp))
    kidx = lax.broadcasted_iota(jnp.int32, (TOP_K, tm), 0)
    pos = jnp.zeros((TOP_K, tm), F32)
    wts = jnp.zeros((TOP_K, tm), F32)
    for k in range(TOP_K):
        first = jnp.min(alive, axis=0, keepdims=True)
        hit = alive == first
        pos_k = jnp.sum(jnp.where(hit, slot, 0.0), axis=0, keepdims=True)
        w_k = jnp.sum(jnp.where(hit, gates, 0.0), axis=0, keepdims=True)
        pos = jnp.where(kidx == k, pos_k, pos)
        wts = jnp.where(kidx == k, w_k, wts)
        alive = jnp.where(hit, float(n_exp), alive)
    pos_ref[...] = pos.astype(jnp.int32)
    w_ref[...] = wts


def _assign_slots(ranks, gates, start):
    n_exp, t = ranks.shape
    tm = min(2048, t)
    return pl.pallas_call(
        functools.partial(_assign_kernel, n_exp=n_exp),
        grid=(t // tm,),
        in_specs=[pl.BlockSpec((n_exp, tm), lambda i: (0, i)),
                  pl.BlockSpec((n_exp, tm), lambda i: (0, i)),
                  pl.BlockSpec((n_exp, 1), lambda i: (0, 0))],
        out_specs=[pl.BlockSpec((TOP_K, tm), lambda i: (0, i)),
                   pl.BlockSpec((TOP_K, tm), lambda i: (0, i))],
        out_shape=[jax.ShapeDtypeStruct((TOP_K, t), jnp.int32), jax.ShapeDtypeStruct((TOP_K, t), F32)],
        compiler_params=_cparams(("arbitrary",)),
        name="moe_assign",
    )(ranks, gates, start)


SC_CORES = 2
SC_SUBCORES = 16
SC_WINDOW = 128


def _sc_mesh():
    return plsc.VectorSubcoreMesh(core_axis_name="c", subcore_axis_name="s",
                                  num_cores=SC_CORES, num_subcores=SC_SUBCORES)


def _sc_scatter_rows(rows, pos, n_out):
    t, width = rows.shape
    nk = pos.shape[0]
    win = SC_WINDOW
    n_win = t // win // (SC_CORES * SC_SUBCORES)
    pos_w = pos.reshape(nk, t // win, win).transpose(1, 0, 2)

    def body(rows_hbm, pos_hbm, out_hbm, idx_v, buf_v, sem):
        wid = lax.axis_index("s") * SC_CORES + lax.axis_index("c")

        @pl.loop(0, n_win)
        def _(j):
            w = wid * n_win + j
            pltpu.sync_copy(rows_hbm.at[pl.ds(w * win, win)], buf_v)
            pltpu.sync_copy(pos_hbm.at[w], idx_v)
            copies = [pltpu.make_async_copy(buf_v, out_hbm.at[idx_v.at[k]], sem) for k in range(nk)]
            for cp in copies:
                cp.start()
            for cp in copies:
                cp.wait()

    return pl.kernel(
        body, out_type=jax.ShapeDtypeStruct((n_out, width), jnp.int32), mesh=_sc_mesh(),
        scratch_types=[pltpu.VMEM((nk, win), jnp.int32), pltpu.VMEM((win, width), jnp.int32),
                       pltpu.SemaphoreType.DMA],
        name="sc_scatter_rows",
    )(rows, pos_w)


SC_GATHER_WINDOW = 64


def _sc_gather_rows(table, pos_flat):
    m = pos_flat.shape[0]
    width = table.shape[1]
    win = SC_GATHER_WINDOW
    workers = SC_CORES * SC_SUBCORES
    n_win = m // win // workers
    pos_w = pos_flat.reshape(workers, n_win, win)

    def body(table_hbm, pos_hbm, out_hbm, idx_v, buf_v, sem_g, sem_w):
        wid = lax.axis_index("s") * SC_CORES + lax.axis_index("c")
        base = wid * (n_win * win)
        pltpu.sync_copy(pos_hbm.at[wid], idx_v)

        def gather(j, b):
            return pltpu.make_async_copy(table_hbm.at[idx_v.at[j]], buf_v.at[b], sem_g.at[b])

        def write(j, b):
            return pltpu.make_async_copy(buf_v.at[b], out_hbm.at[pl.ds(base + j * win, win)], sem_w.at[b])

        gather(0, 0).start()

        @pl.loop(0, n_win, step=2)
        def _(j0):
            for b in range(2):
                j = j0 + b
                gather(j, b).wait()
                write(j, b).start()

                @pl.when(j + 1 < n_win)
                def _():
                    @pl.when(j >= 1)
                    def _():
                        write(j - 1, 1 - b).wait()
                    gather(j + 1, 1 - b).start()

        write(n_win - 2, 0).wait()
        write(n_win - 1, 1).wait()

    return pl.kernel(
        body, out_type=jax.ShapeDtypeStruct((m, width), jnp.int32), mesh=_sc_mesh(),
        scratch_types=[pltpu.VMEM((n_win, win), jnp.int32), pltpu.VMEM((2, win, width), jnp.int32),
                       pltpu.SemaphoreType.DMA((2,)), pltpu.SemaphoreType.DMA((2,))],
        name="sc_gather_rows",
    )(table, pos_w)


MOE_BLOCK = 1024


def _silu_mul(a, b):
    return (a * jax.nn.sigmoid(a)) * b


def _expert_ffn_kernel(be_ref, nv_ref, xs_ref, wg_ref, wu_ref, wd_ref, ys_ref, wg_bf, wu_bf, wd_bf):
    i = pl.program_id(0)
    fresh = jnp.logical_or(i == 0, be_ref[i] != be_ref[jnp.maximum(i - 1, 0)])

    @pl.when(fresh)
    def _():
        wg_bf[...] = wg_ref[...].astype(BF16)
        wu_bf[...] = wu_ref[...].astype(BF16)
        wd_bf[...] = wd_ref[...].astype(BF16)

    nv = nv_ref[i]

    @pl.when(nv > 0)
    def _():
        x = _unpack_bf16_pairs(xs_ref[...])
        row = lax.broadcasted_iota(jnp.int32, (x.shape[0], 1), 0)
        x = jnp.where(row < nv, x, 0.0).astype(BF16)
        mid = _silu_mul(_dot(x, wg_bf[...]), _dot(x, wu_bf[...]))
        ys_ref[...] = _pack_bf16_pairs(_dot(mid.astype(BF16), wd_bf[...]))

    @pl.when(nv <= 0)
    def _():
        ys_ref[...] = jnp.zeros(ys_ref.shape, ys_ref.dtype)


def _expert_ffn(xs, block_expert, block_rows, layer, w_gate, w_up, w_down):
    n_rows, half = xs.shape
    _, n_exp, d, de = w_gate.shape
    nb = n_rows // MOE_BLOCK
    wmap = lambda i, be, nv: (layer, be[i], 0, 0)
    grid_spec = pltpu.PrefetchScalarGridSpec(
        num_scalar_prefetch=2,
        grid=(nb,),
        in_specs=[pl.BlockSpec((MOE_BLOCK, half), lambda i, be, nv: (jnp.where(nv[i] > 0, i, 0), 0)),
                  pl.BlockSpec((None, None, d, de), wmap),
                  pl.BlockSpec((None, None, d, de), wmap),
                  pl.BlockSpec((None, None, de, d), wmap)],
        out_specs=pl.BlockSpec((MOE_BLOCK, half), lambda i, be, nv: (i, 0)),
        scratch_shapes=[pltpu.VMEM((d, de), BF16), pltpu.VMEM((d, de), BF16), pltpu.VMEM((de, d), BF16)],
    )
    return pl.pallas_call(
        _expert_ffn_kernel,
        grid_spec=grid_spec,
        out_shape=jax.ShapeDtypeStruct((n_rows, half), jnp.int32),
        compiler_params=_cparams(("arbitrary",), VMEM_LIMIT),
        name="moe_expert_ffn",
    )(block_expert, block_rows, xs, w_gate, w_up, w_down)


def _combine_kernel(yt_ref, wt_ref, hp_ref, x_ref, mod_ref, sg_ref, su_ref, sd_ref, *rest, d):
    o_ref = rest[-1]
    h = _unpack_bf16_pairs(hp_ref[...]).astype(BF16)
    acc = _dot(_silu_mul(_dot(h, sg_ref[...]), _dot(h, su_ref[...])).astype(BF16), sd_ref[...])
    w = wt_ref[...]
    for k in range(TOP_K):
        acc = acc + w[:, k:k + 1] * _unpack_bf16_pairs(yt_ref[k])
    o_ref[...] = x_ref[...] + mod_ref[...][:, 2 * d:] * acc


def _combine(y_tok, w_tok, h_packed, x2, out_prev, mod, layer, sh_gate, sh_up, sh_down, seq, tile0):
    t_all, d = x2.shape
    ds = sh_gate.shape[1]
    tm = MOE_TILE
    ntiles = h_packed.shape[0] // tm
    tiles_per_seq = seq // tm
    row = lambda i: (i, 0)
    off = lambda i: (i + tile0, 0)
    const = lambda i: (0, 0)
    in_specs = [pl.BlockSpec((TOP_K, tm, d // 2), lambda i: (0, i, 0)),
                pl.BlockSpec((tm, TOP_K), row),
                pl.BlockSpec((tm, d // 2), row),
                pl.BlockSpec((tm, d), off),
                pl.BlockSpec((None, None, 1, 3 * d), lambda i: (layer, (i + tile0) // tiles_per_seq, 0, 0)),
                pl.BlockSpec((d, ds), const),
                pl.BlockSpec((d, ds), const),
                pl.BlockSpec((ds, d), const)]
    args = [y_tok, w_tok, h_packed, x2, mod, sh_gate.astype(BF16), sh_up.astype(BF16), sh_down.astype(BF16)]
    aliases = {}
    if out_prev is not None:
        in_specs.append(pl.BlockSpec(memory_space=pl.ANY))
        args.append(out_prev)
        aliases = {len(args) - 1: 0}
    return pl.pallas_call(
        functools.partial(_combine_kernel, d=d),
        grid=(ntiles,),
        in_specs=in_specs,
        out_specs=pl.BlockSpec((tm, d), off),
        out_shape=jax.ShapeDtypeStruct((t_all, d), F32),
        input_output_aliases=aliases,
        compiler_params=_cparams(("arbitrary",), VMEM_LIMIT),
        name="moe_combine",
    )(*args)


def _moe_group(x2, out_prev, tile0, h_packed, gates, ranks, counts, mod, layer, w_gate, w_up, w_down,
               sh_gate, sh_up, sh_down, seq):
    t, half = h_packed.shape
    d = 2 * half
    n_exp = w_gate.shape[1]
    cnt = counts[:, 0].astype(jnp.int32)
    padded = (cnt + MOE_BLOCK - 1) // MOE_BLOCK * MOE_BLOCK
    eidx = jnp.arange(n_exp, dtype=jnp.int32)
    seg_end = jnp.sum(jnp.where(eidx[None, :] <= eidx[:, None], padded[None, :], 0), axis=1)
    seg_start = seg_end - padded
    nb = t * TOP_K // MOE_BLOCK + n_exp
    first_row = jnp.arange(nb, dtype=jnp.int32) * MOE_BLOCK
    block_expert = jnp.sum((seg_end[None, :] <= first_row[:, None]).astype(jnp.int32), axis=1)
    block_expert = jnp.minimum(block_expert, n_exp - 1)
    last_row = jnp.sum(jnp.where(block_expert[:, None] == eidx[None, :], (seg_start + cnt)[None, :], 0), axis=1)
    block_rows = jnp.clip(last_row - first_row, 0, MOE_BLOCK).astype(jnp.int32)
    pos, w = _assign_slots(ranks, gates, seg_start.astype(F32).reshape(n_exp, 1))
    xs = _sc_scatter_rows(h_packed, pos, nb * MOE_BLOCK)
    ys = _expert_ffn(xs, block_expert, block_rows, layer, w_gate, w_up, w_down)
    y_tok = _sc_gather_rows(ys, pos.reshape(TOP_K * t)).reshape(TOP_K, t, d // 2)
    return _combine(y_tok, w.T, h_packed, x2, out_prev, mod, layer, sh_gate, sh_up, sh_down, seq, tile0)


MOE_GROUPS = 1


def _moe_layer(x2, g, mod, layer, router_w, router_bias, w_gate, w_up, w_down, sh_gate, sh_up, sh_down, seq):
    ntiles = x2.shape[0] // MOE_TILE // MOE_GROUPS
    rtiles = ntiles * MOE_TILE // ROUTER_TILE
    routed = [_ffn_router(x2, g, mod, layer, router_w, router_bias, seq, grp * rtiles, rtiles)
              for grp in range(MOE_GROUPS)]
    out = None
    for grp in range(MOE_GROUPS):
        out = _moe_group(x2, out, grp * ntiles, *routed[grp], mod, layer, w_gate, w_up, w_down,
                         sh_gate, sh_up, sh_down, seq)
    return out


def kernel(x, c, positions, mix_norm_g, mix_mod_w, mix_mod_b, ffn_norm_g, ffn_mod_w, ffn_mod_b,
           hyb_w_in, hyb_w_out, s5_lambda_re, s5_lambda_im, s5_log_dt, s5_b_re, s5_b_im,
           s5_c_re, s5_c_im, s5_d, s5_glu_w, s5_glu_b, attn_q_norm_g, attn_k_norm_g,
           pool_w, pool_scale, router_w, router_bias, exp_w_gate, exp_w_up, exp_w_down,
           sh_w_gate, sh_w_up, sh_w_down):
    bsz, seq, d = x.shape
    t = bsz * seq
    depth = mix_norm_g.shape[0]
    x2 = x.reshape(t, d)
    mix_mod = _mod_vectors(c, mix_mod_w, mix_mod_b).reshape(depth, bsz, 1, 3 * d)
    ffn_mod = _mod_vectors(c, ffn_mod_w, ffn_mod_b).reshape(depth, bsz, 1, 3 * d)
    for i in range(depth):
        j = i // 2
        if i % 2 == 0:
            u, q_hm, qi_hm, k, v, ki, wi = _in_projection(
                x2, mix_norm_g[i], mix_mod, i, hyb_w_in[j], positions, attn_q_norm_g[j], attn_k_norm_g[j], bsz, seq)
            weights = _s5_weights(s5_lambda_re[j], s5_lambda_im[j], s5_log_dt[j], s5_b_re[j], s5_b_im[j],
                                  s5_c_re[j], s5_c_im[j], s5_d[j], s5_glu_w[j], s5_glu_b[j])
            y_ssm = _s5_mixer(u, weights, bsz, seq)
            y_att = _dsa_attention(q_hm, qi_hm, wi, k, v, ki, bsz, seq)
            x2 = _out_projection(x2, y_ssm, y_att, hyb_w_out[j], mix_mod, i, seq)
        else:
            x2 = _pool_layer(x2, mix_norm_g[i], mix_mod, i, pool_w[j], pool_scale[j], seq)
        x2 = _moe_layer(x2, ffn_norm_g[i], ffn_mod, i, router_w[i], router_bias[i], exp_w_gate, exp_w_up, exp_w_down,
                        sh_w_gate[i], sh_w_up[i], sh_w_down[i], seq)
    return x2.reshape(bsz, seq, d)
```

```python
import functools

import numpy as np
import jax
import jax.numpy as jnp
from jax import lax
from jax.experimental import pallas as pl
from jax.experimental.pallas import tpu as pltpu
from jax.experimental.pallas import tpu_sc as plsc

F32 = jnp.float32
BF16 = jnp.bfloat16

EPS = 1e-6
S5_GROUP = 16
S5_STATE = 64
HEAD_DIM = 64
ATTN_HEADS = 8
IDX_HEADS = 8
ROPE_HALF = 8
ROPE_THETA = 500000.0
TOPK_MAX = 256
POOL_WINDOWS = (2, 4, 8, 16)
POOL_HALO = 16
N_EXPERT_GROUPS = 8
TOPK_GROUPS = 4
TOP_K = 8
ROUTED_SCALE = 2.5
S5_CHUNK = 16
NEG_BIG = -1e30
VMEM_LIMIT = 56 * 1024 * 1024


def _cparams(sem, vmem=None):
    return pltpu.CompilerParams(dimension_semantics=sem, vmem_limit_bytes=vmem)


def _dot(a, b):
    return jnp.dot(a, b, preferred_element_type=F32)


def _dot_nt(a, b):
    return lax.dot_general(a, b, (((1,), (1,)), ((), ())), preferred_element_type=F32)


def _split_bf16(a):
    hi = a.astype(BF16)
    lo = (a - hi.astype(F32)).astype(BF16)
    return hi, lo


def _norm_mod(x, g, shift, scale):
    y = x * lax.rsqrt(jnp.mean(x * x, axis=-1, keepdims=True) + EPS)
    return (y * g) * (1.0 + scale) + shift


def _mod_kernel(ct_ref, w_ref, b_ref, o_ref):
    ct = ct_ref[...]
    cs = ct * jax.nn.sigmoid(ct)
    w = w_ref[...]
    rows = [jnp.sum(w * cs[:, b:b + 1], axis=0, keepdims=True) for b in range(ct.shape[1])]
    o_ref[...] = jnp.concatenate(rows, axis=0) + b_ref[...]


def _mod_vectors(c, w, b):
    nl, d, n3 = w.shape
    bsz = c.shape[0]
    tn = 1536
    return pl.pallas_call(
        _mod_kernel,
        grid=(nl, n3 // tn),
        in_specs=[pl.BlockSpec((d, bsz), lambda l, j: (0, 0)),
                  pl.BlockSpec((None, d, tn), lambda l, j: (l, 0, j)),
                  pl.BlockSpec((None, 1, tn), lambda l, j: (l, 0, j))],
        out_specs=pl.BlockSpec((None, bsz, tn), lambda l, j: (l, 0, j)),
        out_shape=jax.ShapeDtypeStruct((nl, bsz, n3), F32),
        compiler_params=_cparams(("arbitrary", "arbitrary")),
        name="mod_vectors",
    )(c.T, w, b.reshape(nl, 1, n3))


def _inproj_kernel(x_ref, g_ref, mod_ref, w_ref, pos_ref, inv_ref, qg_ref, kg_ref,
                   u_ref, q_ref, qi_ref, k_ref, v_ref, ki_ref, wi_ref, u_scr, *, d):
    x = x_ref[...]
    mod = mod_ref[...]
    h = _norm_mod(x, g_ref[...], mod[:, :d], mod[:, d:2 * d])
    proj = _dot(h.astype(BF16), w_ref[...])
    tm = x.shape[0]
    width = ATTN_HEADS * HEAD_DIM

    ang = inv_ref[...] * pos_ref[...].astype(F32)
    cos, sin = jnp.cos(ang), jnp.sin(ang)

    def head_t(xt, gain, scale):
        if scale is not None:
            r = lax.rsqrt(jnp.mean(xt * xt, axis=0, keepdims=True) + EPS) * scale
            xt = xt * gain
        x1, x2 = xt[:ROPE_HALF], xt[ROPE_HALF:2 * ROPE_HALF]
        out = jnp.concatenate([x1 * cos - x2 * sin, x1 * sin + x2 * cos, xt[2 * ROPE_HALF:]], axis=0)
        return out if scale is None else out * r

    for slab in range(width // 128):
        u_scr[slab] = proj[:, slab * 128:(slab + 1) * 128]
        for j in range(S5_CHUNK):
            piece = u_scr[slab, pl.ds(j, tm // S5_CHUNK, stride=S5_CHUNK), :]
            col = (slab * S5_CHUNK + j) * 128
            u_ref[:, col:col + 128] = piece.astype(u_ref.dtype)
    q = proj[:, width:2 * width]
    qi = proj[:, 2 * width:3 * width]
    small = proj[:, 3 * width:3 * width + 256]
    qg = qg_ref[...]
    for pair in range(ATTN_HEADS // 2):
        sl = slice(pair * 128, (pair + 1) * 128)
        qt = q[:, sl].T
        qit = qi[:, sl].T
        for half in range(2):
            hs = slice(half * HEAD_DIM, (half + 1) * HEAD_DIM)
            q_ref[2 * pair + half] = head_t(qt[hs], qg, HEAD_DIM ** -0.5).astype(q_ref.dtype)
            qi_ref[2 * pair + half] = head_t(qit[hs], None, None).astype(qi_ref.dtype)
    kvt = small[:, :128].T
    kt = head_t(kvt[:HEAD_DIM], kg_ref[...], 1.0)
    kv = jnp.concatenate([kt, kvt[HEAD_DIM:]], axis=0).T
    k_ref[...] = kv[:, :HEAD_DIM].astype(k_ref.dtype)
    v_ref[...] = kv[:, HEAD_DIM:].astype(v_ref.dtype)
    kiwt = small[:, 128:256].T
    kit = jnp.concatenate([head_t(kiwt[:HEAD_DIM], None, None), kiwt[HEAD_DIM:]], axis=0).T
    ki_ref[...] = kit[:, :HEAD_DIM].astype(ki_ref.dtype)
    wscale = (IDX_HEADS ** -0.5) * (HEAD_DIM ** -0.5)
    wi_ref[...] = kiwt[HEAD_DIM:HEAD_DIM + IDX_HEADS] * wscale


def _in_projection(x2, g, mod, layer, w_in, positions, q_norm_g, k_norm_g, bsz, seq):
    t, d = x2.shape
    tm = min(1024, seq)
    width = ATTN_HEADS * HEAD_DIM
    cuts = np.cumsum([width, width, HEAD_DIM, HEAD_DIM, width, HEAD_DIM, IDX_HEADS])
    u_w, q_w, k_w, v_w, qi_w, ki_w, wi_w = jnp.split(w_in, cuts[:-1].tolist(), axis=1)
    pad = jnp.zeros((d, 128 - HEAD_DIM - IDX_HEADS), w_in.dtype)
    w = jnp.concatenate([u_w, q_w, qi_w, k_w, v_w, ki_w, wi_w, pad], axis=1).astype(BF16)
    nw = w.shape[1]
    inv = np.power(ROPE_THETA, -2.0 * np.arange(ROPE_HALF) / (2 * ROPE_HALF))
    inv = jnp.asarray(inv.reshape(ROPE_HALF, 1), F32)
    tiles_per_seq = seq // tm
    row = lambda i: (i, 0)
    const = lambda i: (0, 0)
    hm = lambda i: (i // tiles_per_seq, 0, 0, i % tiles_per_seq)
    outs = pl.pallas_call(
        functools.partial(_inproj_kernel, d=d),
        grid=(t // tm,),
        in_specs=[pl.BlockSpec((tm, d), row),
                  pl.BlockSpec((1, d), const),
                  pl.BlockSpec((None, None, 1, 3 * d), lambda i: (layer, i // tiles_per_seq, 0, 0)),
                  pl.BlockSpec((d, nw), const),
                  pl.BlockSpec((None, 1, tm), lambda i: (i, 0, 0)),
                  pl.BlockSpec((ROPE_HALF, 1), const),
                  pl.BlockSpec((HEAD_DIM, 1), const),
                  pl.BlockSpec((HEAD_DIM, 1), const)],
        out_specs=[pl.BlockSpec((tm // S5_CHUNK, width * S5_CHUNK), row),
                   pl.BlockSpec((None, ATTN_HEADS, HEAD_DIM, tm), hm),
                   pl.BlockSpec((None, IDX_HEADS, HEAD_DIM, tm), hm),
                   pl.BlockSpec((tm, HEAD_DIM), row),
                   pl.BlockSpec((tm, HEAD_DIM), row),
                   pl.BlockSpec((tm, HEAD_DIM), row),
                   pl.BlockSpec((IDX_HEADS, tm), lambda i: (0, i))],
        out_shape=[jax.ShapeDtypeStruct((t // S5_CHUNK, width * S5_CHUNK), BF16),
                   jax.ShapeDtypeStruct((bsz, ATTN_HEADS, HEAD_DIM, seq), BF16),
                   jax.ShapeDtypeStruct((bsz, IDX_HEADS, HEAD_DIM, seq), BF16),
                   jax.ShapeDtypeStruct((t, HEAD_DIM), BF16),
                   jax.ShapeDtypeStruct((t, HEAD_DIM), BF16),
                   jax.ShapeDtypeStruct((t, HEAD_DIM), BF16),
                   jax.ShapeDtypeStruct((IDX_HEADS, t), F32)],
        scratch_shapes=[pltpu.VMEM((width // 128, tm, 128), F32)],
        compiler_params=_cparams(("arbitrary",), VMEM_LIMIT),
        name="in_projection",
    )(x2, g.reshape(1, d), mod, w, positions.reshape(t // tm, 1, tm), inv,
      q_norm_g.reshape(HEAD_DIM, 1), k_norm_g.reshape(HEAD_DIM, 1))
    return outs


def _s5_weights(lam_re, lam_im, log_dt, b_re, b_im, c_re, c_im, d_skip, glu_w, glu_b):
    L = S5_CHUNK
    g, p = lam_re.shape
    hch = S5_GROUP
    dt = jnp.exp(log_dt)[:, None]
    lr, li = lam_re, lam_im
    tau = jnp.arange(L + 1, dtype=F32)[:, None, None]
    mag = jnp.exp(lr * dt * tau)
    pw_r, pw_i = mag * jnp.cos(li * dt * tau), mag * jnp.sin(li * dt * tau)
    ar, ai = pw_r[1], pw_i[1]
    nr, ni = ar - 1.0, ai
    den = lr * lr + li * li
    cr, ci = (nr * lr + ni * li) / den, (ni * lr - nr * li) / den
    bb_r = cr[..., None] * b_re - ci[..., None] * b_im
    bb_i = cr[..., None] * b_im + ci[..., None] * b_re
    cl_r = c_re[None] * pw_r[:L, :, None, :] - c_im[None] * pw_i[:L, :, None, :]
    cl_i = c_re[None] * pw_i[:L, :, None, :] + c_im[None] * pw_r[:L, :, None, :]
    taps = (jnp.einsum('tghp,gpk->tghk', cl_r, bb_r) - jnp.einsum('tghp,gpk->tghk', cl_i, bb_i))
    ns = g // 8
    eye = jnp.eye(8, dtype=F32)
    kd = jnp.einsum('tsgoh,gf->tsghfo', taps.reshape(L, ns, 8, hch, hch), eye).reshape(L, ns, 128, 128)
    zero = jnp.zeros_like(kd[0])
    k2 = jnp.stack([jnp.concatenate([jnp.concatenate([kd[2 * dd], kd[2 * dd + 1]], axis=-1),
                                     jnp.concatenate([kd[2 * dd - 1] if dd else zero, kd[2 * dd]], axis=-1)], axis=-2)
                    for dd in range(L // 2)], axis=1).astype(BF16)
    ii = jnp.arange(L)
    rev_r, rev_i = pw_r[L - 1 - ii], pw_i[L - 1 - ii]
    z_r = rev_r[..., None] * bb_r[None] - rev_i[..., None] * bb_i[None]
    z_i = rev_r[..., None] * bb_i[None] + rev_i[..., None] * bb_r[None]
    zc = jnp.concatenate([z_r, z_i], axis=2)
    zc = jnp.transpose(zc.reshape(L, ns, 8, 2 * p, hch), (1, 0, 2, 4, 3)).reshape(ns, L * 128, 2 * p)
    row_group = (jnp.arange(L * 128) // hch) % 8
    state_group = jnp.arange(8 * 2 * p) // (2 * p)
    w_z = jnp.where((row_group[:, None] == state_group[None, :])[None], jnp.concatenate([zc] * 8, axis=-1), 0.0)
    w_z = w_z.reshape(ns, L // 2, 2 * 128, 8 * 2 * p).astype(BF16)
    co_r = c_re[None] * pw_r[1:, :, None, :] - c_im[None] * pw_i[1:, :, None, :]
    co_i = c_re[None] * pw_i[1:, :, None, :] + c_im[None] * pw_r[1:, :, None, :]
    cc = jnp.concatenate([co_r, -co_i], axis=-1)
    cc = jnp.transpose(cc.reshape(L, ns, 8, hch, 2 * p), (1, 4, 0, 2, 3)).reshape(ns, 1, 2 * p, L * 128)
    col_group = (jnp.arange(L * 128) // hch) % 8
    w_c = jnp.where((jnp.arange(8)[:, None] == col_group[None, :])[None, :, None, :], cc, 0.0)
    w_c = w_c.reshape(ns, 8 * 2 * p, L * 128)
    w_c = w_c.astype(BF16)
    al_r, al_i = pw_r[L], pw_i[L]
    dec_a = jnp.concatenate([al_r, al_r], axis=-1).reshape(ns, 1, 8 * 2 * p)
    dec_b = jnp.concatenate([-al_i, al_i], axis=-1).reshape(ns, 1, 8 * 2 * p)
    d_t = jnp.tile(d_skip.reshape(ns, 1, 128), (1, 1, L))
    gl = jnp.einsum('sghk,gf->sghfk', glu_w.reshape(ns, 8, hch, hch), eye).reshape(ns, 128, 128)
    glu2 = jnp.einsum('ab,shk->sahbk', jnp.eye(2, dtype=F32), gl).reshape(ns, 256, 256).astype(BF16)
    glu_bt = jnp.tile(glu_b.reshape(ns, 1, 128), (1, 1, L))
    return k2, w_z, w_c, dec_a, dec_b, d_t, glu2, glu_bt


def _s5_kernel(u_ref, k2_ref, wz_ref, wc_ref, da_ref, db_ref, d_ref, g2_ref, gb_ref, o_ref,
               zp_ref, zq_ref, sp_ref, *, nchunk):
    nblk = S5_CHUNK // 2
    ub = [u_ref[:, j * 256:(j + 1) * 256] for j in range(nblk)]
    z = _dot(ub[0], wz_ref[0])
    for j in range(1, nblk):
        z = z + _dot(ub[j], wz_ref[j])
    zp_ref[...] = z
    half = S5_STATE
    zq_ref[...] = jnp.concatenate([pltpu.roll(z[:, s * 2 * half:(s + 1) * 2 * half], half, 1)
                                   for s in range(z.shape[1] // (2 * half))], axis=1)
    da = da_ref[...]
    db = db_ref[...]
    dbq = -db

    def step(c, carry):
        sp, sq = carry
        sp_ref[pl.ds(c, 1), :] = sp
        sp_new = da * sp + db * sq + zp_ref[pl.ds(c, 1), :]
        sq_new = da * sq + dbq * sp + zq_ref[pl.ds(c, 1), :]
        return sp_new, sq_new

    zero = jnp.zeros((1, z.shape[1]), F32)
    lax.fori_loop(0, nchunk, step, (zero, zero))
    sh, sl = _split_bf16(sp_ref[...])
    wc = wc_ref[...]
    y_inter = _dot(sh, wc) + _dot(sl, wc)
    for i in range(nblk):
        cs = slice(i * 256, (i + 1) * 256)
        acc = y_inter[:, cs] + d_ref[:, cs] * ub[i].astype(F32)
        for j in range(i + 1):
            acc = acc + _dot(ub[j], k2_ref[i - j])
        y = jax.nn.gelu(acc)
        y = y * jax.nn.sigmoid(_dot(y.astype(BF16), g2_ref[...]) + gb_ref[:, cs])
        for a in range(2):
            o_ref[pl.ds(2 * i + a, nchunk, stride=S5_CHUNK), :] = y[:, a * 128:(a + 1) * 128]


def _s5_mixer(u2, weights, bsz, seq):
    k2, w_z, w_c, dec_a, dec_b, d_t, glu2, glu_bt = weights
    ns = k2.shape[0]
    nchunk = seq // S5_CHUNK
    cols = S5_CHUNK * 128
    st = w_z.shape[-1]
    slab3 = lambda b, s: (s, 0, 0)
    slab4 = lambda b, s: (s, 0, 0, 0)
    return pl.pallas_call(
        functools.partial(_s5_kernel, nchunk=nchunk),
        grid=(bsz, ns),
        in_specs=[pl.BlockSpec((nchunk, cols), lambda b, s: (b, s)),
                  pl.BlockSpec((None,) + k2.shape[1:], slab4),
                  pl.BlockSpec((None,) + w_z.shape[1:], slab4),
                  pl.BlockSpec((None, st, cols), slab3),
                  pl.BlockSpec((None, 1, st), slab3),
                  pl.BlockSpec((None, 1, st), slab3),
                  pl.BlockSpec((None, 1, cols), slab3),
                  pl.BlockSpec((None, 256, 256), slab3),
                  pl.BlockSpec((None, 1, cols), slab3)],
        out_specs=pl.BlockSpec((seq, 128), lambda b, s: (b, s)),
        out_shape=jax.ShapeDtypeStruct((bsz * seq, ns * 128), F32),
        scratch_shapes=[pltpu.VMEM((nchunk, st), F32), pltpu.VMEM((nchunk, st), F32), pltpu.VMEM((nchunk, st), F32)],
        compiler_params=_cparams(("arbitrary", "arbitrary"), VMEM_LIMIT),
        name="s5_mixer",
    )(u2, k2, w_z, w_c, dec_a, dec_b, d_t, glu2, glu_bt)


def _dsa_kernel(qt_ref, qit_ref, wt_ref, k_ref, vt_ref, ki_ref, o_ref,
                sc_ref, s_ref, m_ref, l_ref, acc_ref, *, tq, ck, k_sel, idx_bits, max_iter):
    i = pl.program_id(1)
    nheads = qt_ref.shape[0]
    n_ck = (i * tq + tq + ck - 1) // ck
    t_pos = i * tq + lax.broadcasted_iota(jnp.int32, (1, tq), 1)
    krow = lax.broadcasted_iota(jnp.int32, (ck, 1), 0)
    kf = float(k_sel)
    inf = jnp.inf

    fr = 32
    sub = min(32768 // tq, ck)

    def fold(x, op):
        return op(x.reshape(ck // fr, fr, tq), axis=0)

    wt = wt_ref[...]

    def score_chunk(c, carry):
        lo8, hi8 = carry
        for part in range(ck // sub):
            off = pl.multiple_of(c * ck + part * sub, sub)
            kic = ki_ref[pl.ds(off, sub), :]
            acc = jnp.zeros((sub, tq), F32)
            for h in range(nheads):
                acc = acc + wt[h:h + 1, :] * jnp.maximum(_dot(kic, qit_ref[h]), 0.0)
            acc = acc + 0.0
            vis = off + krow[:sub] <= t_pos
            sc_ref[c, part * sub:(part + 1) * sub, :] = jnp.where(vis, acc, -inf)
            hi8 = jnp.maximum(hi8, jnp.max(jnp.where(vis, acc, -inf).reshape(sub // fr, fr, tq), axis=0))
            lo8 = jnp.minimum(lo8, jnp.min(jnp.where(vis, acc, inf).reshape(sub // fr, fr, tq), axis=0))
        return lo8, hi8

    lo8, hi8 = lax.fori_loop(0, n_ck, score_chunk, (jnp.full((fr, tq), inf, F32), jnp.full((fr, tq), -inf, F32)))
    lo = jnp.min(lo8, axis=0, keepdims=True)
    hi = jnp.max(hi8, axis=0, keepdims=True)
    n_vis = (t_pos + 1).astype(F32)
    few = n_vis < kf

    def reduce_keys(fn, init, combine, op):
        def body(c, part):
            return combine(part, fold(fn(sc_ref[c], c * ck), op))
        return op(lax.fori_loop(0, n_ck, body, jnp.full((fr, tq), init, F32)), axis=0, keepdims=True)

    def count(pred):
        return reduce_keys(lambda x, off: jnp.where(pred(x, off), 1.0, 0.0), 0.0, jnp.add, jnp.sum)

    def bisect(lo_, hi_, c_lo):
        mid = lo_ + (hi_ - lo_) * 0.5
        cnt = count(lambda x, off: x >= mid)
        up = cnt >= kf
        return jnp.where(up, mid, lo_), jnp.where(up, hi_, mid), jnp.where(up, cnt, c_lo)

    def status(lo_, hi_, c_lo):
        mid = lo_ + (hi_ - lo_) * 0.5
        unresolved = jnp.where(few, 0.0, jnp.where(c_lo == kf, 0.0, 1.0))
        movable = jnp.where(mid > lo_, jnp.where(mid < hi_, 1.0, 0.0), 0.0)
        return jnp.max(unresolved), jnp.max(unresolved * movable)

    def search_body(st):
        it, lo_, hi_, c_lo, _, _ = st
        for _ in range(2):
            lo_, hi_, c_lo = bisect(lo_, hi_, c_lo)
        return (it + 2, lo_, hi_, c_lo) + status(lo_, hi_, c_lo)

    st = lax.while_loop(lambda st: jnp.logical_and(st[0] < max_iter, st[5] > 0.5), search_body,
                        (jnp.int32(0), lo, hi, n_vis) + status(lo, hi, n_vis))
    lo, open_ = st[1], st[4]

    def resolve():
        def walk(thr_, strict):
            keep = (lambda x: x > thr_) if strict else (lambda x: x >= thr_)
            return reduce_keys(lambda x, off: jnp.where(keep(x), x, inf), inf, jnp.minimum, jnp.min)

        def above_equal(thr_):
            return count(lambda x, off: x > thr_), count(lambda x, off: x == thr_)

        def too_low(n_gt):
            return jnp.where(few, 0.0, jnp.where(n_gt >= kf, 1.0, 0.0))

        thr0 = walk(lo, False)

        def walk_body(st):
            thr_, n_gt, _ = st
            thr2 = jnp.where(too_low(n_gt) > 0.5, walk(thr_, True), thr_)
            return (thr2,) + above_equal(thr2)

        thr_, n_gt, n_eq = lax.while_loop(lambda st: jnp.max(too_low(st[1])) > 0.5, walk_body,
                                          (thr0,) + above_equal(thr0))
        need = kf - n_gt

        def tie_cut():
            def mark_equal(c, _):
                s_ref[0, c] = jnp.where(sc_ref[c] == thr_, 1.0, 0.0)
                return 0

            lax.fori_loop(0, n_ck, mark_equal, 0)

            def idx_step(b, cur):
                trial = cur | (jnp.int32(1) << (idx_bits - 1 - b))

                def body(c, part):
                    return part + fold(jnp.where(c * ck + krow < trial, s_ref[0, c], 0.0), jnp.sum)
                cnt = jnp.sum(lax.fori_loop(0, n_ck, body, jnp.zeros((fr, tq), F32)), axis=0, keepdims=True)
                return jnp.where(cnt < need, trial, cur)

            return lax.fori_loop(0, idx_bits, idx_step, jnp.zeros((1, tq), jnp.int32))

        tied = jnp.max(jnp.where(few, 0.0, jnp.where(n_eq > need, 1.0, 0.0))) > 0.5
        return thr_, lax.cond(tied, tie_cut, lambda: jnp.full((1, tq), 0x7FFFFFFF, jnp.int32))

    thr, cut = lax.cond(open_ > 0.5, resolve, lambda: (lo, jnp.full((1, tq), 0x7FFFFFFF, jnp.int32)))

    for h in range(nheads):
        m_ref[h] = jnp.full((fr, tq), NEG_BIG, F32)
        l_ref[h] = jnp.zeros((fr, tq), F32)
        acc_ref[h] = jnp.zeros((HEAD_DIM, tq), F32)

    def attn_scores(c, _):
        off = pl.multiple_of(c * ck, ck)
        kc = k_ref[pl.ds(off, ck), :]
        x = sc_ref[c]
        at_thr = jnp.where(x == thr, jnp.where(off + krow <= cut, 0.0, NEG_BIG), NEG_BIG)
        bias = jnp.where(x > thr, 0.0, at_thr)
        for h in range(nheads):
            s = _dot(kc, qt_ref[h]) + bias
            s_ref[h, c] = s
            m_ref[h] = jnp.maximum(m_ref[h], fold(s, jnp.max))
        return 0

    lax.fori_loop(0, n_ck, attn_scores, 0)
    for h in range(nheads):
        m_ref[h] = jnp.broadcast_to(jnp.max(m_ref[h], axis=0, keepdims=True), (fr, tq))

    def attn_values(c, _):
        vt = vt_ref[c]
        for h in range(nheads):
            p = jnp.exp(s_ref[h, c] - m_ref[h][:1, :])
            l_ref[h] += fold(p, jnp.sum)
            acc_ref[h] += _dot(vt, p.astype(BF16))
        return 0

    lax.fori_loop(0, n_ck, attn_values, 0)
    for h in range(nheads):
        l_row = jnp.sum(l_ref[h], axis=0, keepdims=True)
        o_ref[h * HEAD_DIM:(h + 1) * HEAD_DIM, :] = (acc_ref[h] / l_row).astype(o_ref.dtype)


def _dsa_attention(qt, qit, wt, k, v, ki, bsz, seq):
    tq = 256
    ck = min(512, seq)
    n_ck = seq // ck
    k_sel = min(TOPK_MAX, seq // 4)
    nq = seq // tq
    nh = ATTN_HEADS
    width = nh * HEAD_DIM
    vt = v.reshape(bsz, n_ck, ck, HEAD_DIM).transpose(0, 1, 3, 2)
    qspec = pl.BlockSpec((None, nh, HEAD_DIM, tq), lambda b, i: (b, 0, 0, i))
    out_t = pl.pallas_call(
        functools.partial(_dsa_kernel, tq=tq, ck=ck, k_sel=k_sel, idx_bits=(seq - 1).bit_length(), max_iter=22),
        grid=(bsz, nq),
        in_specs=[qspec, qspec,
                  pl.BlockSpec((IDX_HEADS, tq), lambda b, i: (0, b * nq + i)),
                  pl.BlockSpec((seq, HEAD_DIM), lambda b, i: (b, 0)),
                  pl.BlockSpec((None, n_ck, HEAD_DIM, ck), lambda b, i: (b, 0, 0, 0)),
                  pl.BlockSpec((seq, HEAD_DIM), lambda b, i: (b, 0))],
        out_specs=pl.BlockSpec((None, width, tq), lambda b, i: (b, 0, i)),
        out_shape=jax.ShapeDtypeStruct((bsz, width, seq), BF16),
        scratch_shapes=[pltpu.VMEM((n_ck, ck, tq), F32),
                        pltpu.VMEM((nh, n_ck, ck, tq), F32),
                        pltpu.VMEM((nh, 32, tq), F32),
                        pltpu.VMEM((nh, 32, tq), F32),
                        pltpu.VMEM((nh, HEAD_DIM, tq), F32)],
        compiler_params=_cparams(("arbitrary", "arbitrary"), VMEM_LIMIT),
        name="dsa_attention",
    )(qt, qit, wt, k, vt, ki)
    return out_t.transpose(0, 2, 1).reshape(bsz * seq, width)


def _outproj_kernel(x_ref, ys_ref, ya_ref, w_ref, mod_ref, o_ref, *, d):
    width = ys_ref.shape[1]
    y = _dot(ys_ref[...].astype(BF16), w_ref[:width, :]) + _dot(ya_ref[...], w_ref[width:, :])
    o_ref[...] = x_ref[...] + mod_ref[...][:, 2 * d:] * y


def _out_projection(x2, y_ssm, y_att, w_out, mod, layer, seq):
    t, d = x2.shape
    tm = min(1024, seq)
    width = y_ssm.shape[1]
    tiles_per_seq = seq // tm
    row = lambda i: (i, 0)
    return pl.pallas_call(
        functools.partial(_outproj_kernel, d=d),
        grid=(t // tm,),
        in_specs=[pl.BlockSpec((tm, d), row),
                  pl.BlockSpec((tm, width), row),
                  pl.BlockSpec((tm, width), row),
                  pl.BlockSpec((2 * width, d), lambda i: (0, 0)),
                  pl.BlockSpec((None, None, 1, 3 * d), lambda i: (layer, i // tiles_per_seq, 0, 0))],
        out_specs=pl.BlockSpec((tm, d), row),
        out_shape=jax.ShapeDtypeStruct((t, d), F32),
        compiler_params=_cparams(("arbitrary",), VMEM_LIMIT),
        name="out_projection",
    )(x2, y_ssm, y_att, w_out.astype(BF16), mod)


def _pool_kernel(x_ref, halo_ref, g_ref, mod_ref, pw_ref, ps_ref, o_ref, h_ref, fa_ref, fb_ref,
                 *, d, tiles_per_seq):
    i = pl.program_id(0)
    tm = x_ref.shape[0]
    pad, rows = POOL_HALO, POOL_HALO + tm
    mod = mod_ref[...]
    shift, scale, gate = mod[:, :d], mod[:, d:2 * d], mod[:, 2 * d:]
    x = x_ref[...]
    first = (i % tiles_per_seq) == 0
    gw = d // len(POOL_WINDOWS)
    h_ref[:pad, :] = jnp.zeros((pad, d), F32)
    fa_ref[:pad, :] = jnp.zeros((pad, gw), F32)
    fb_ref[:pad, :] = jnp.zeros((pad, gw), F32)
    h_ref[2 * pad:, :] = _norm_mod(x, g_ref[...], shift, scale)
    halo = _norm_mod(halo_ref[...], g_ref[...], shift, scale)
    h_ref[pad:2 * pad, :] = jnp.where(first, 0.0, halo)
    pos = (i % tiles_per_seq) * tm + lax.broadcasted_iota(jnp.int32, (tm, 1), 0)
    ys = []
    for gi, win in enumerate(POOL_WINDOWS):
        cs = slice(gi * gw, (gi + 1) * gw)
        cur = h_ref[2 * pad:, cs]
        src, width, bufs = None, 1, (fa_ref, fb_ref)
        while width < win:
            dst = bufs[0] if src is not bufs[0] else bufs[1]
            if src is None:
                dst[pad:, :] = h_ref[pad:, cs] + h_ref[pad - width:pad - width + rows, cs]
            else:
                dst[pad:, :] = src[pad:, :] + src[pad - width:pad - width + rows, :]
            src, width = dst, 2 * width
        tot = src[2 * pad:, :]
        cnt = jnp.minimum(pos + 1, win).astype(F32)
        pooled = tot / cnt - cur
        ys.append(_dot(pooled.astype(BF16), pw_ref[gi]))
    y = jnp.concatenate(ys, axis=1) * ps_ref[...]
    o_ref[...] = x + gate * y


def _pool_layer(x2, g, mod, layer, pool_w, pool_scale, seq):
    t, d = x2.shape
    tm = min(1024, seq)
    tiles_per_seq = seq // tm
    gw = d // len(POOL_WINDOWS)
    hb = tm // POOL_HALO
    return pl.pallas_call(
        functools.partial(_pool_kernel, d=d, tiles_per_seq=tiles_per_seq),
        grid=(t // tm,),
        in_specs=[pl.BlockSpec((tm, d), lambda i: (i, 0)),
                  pl.BlockSpec((POOL_HALO, d), lambda i: (jnp.maximum(i * hb - 1, 0), 0)),
                  pl.BlockSpec((1, d), lambda i: (0, 0)),
                  pl.BlockSpec((None, None, 1, 3 * d), lambda i: (layer, i // tiles_per_seq, 0, 0)),
                  pl.BlockSpec((len(POOL_WINDOWS), gw, gw), lambda i: (0, 0, 0)),
                  pl.BlockSpec((1, d), lambda i: (0, 0))],
        out_specs=pl.BlockSpec((tm, d), lambda i: (i, 0)),
        out_shape=jax.ShapeDtypeStruct((t, d), F32),
        scratch_shapes=[pltpu.VMEM((tm + 2 * POOL_HALO, d), F32),
                        pltpu.VMEM((tm + 2 * POOL_HALO, gw), F32),
                        pltpu.VMEM((tm + 2 * POOL_HALO, gw), F32)],
        compiler_params=_cparams(("arbitrary",), VMEM_LIMIT),
        name="pool_mixer",
    )(x2, x2, g.reshape(1, d), mod, pool_w.astype(BF16), pool_scale.reshape(1, d))


def _first_max(vals, idx, big):
    m = jnp.max(vals, axis=0, keepdims=True)
    first = jnp.min(jnp.where(vals == m, idx, big), axis=0, keepdims=True)
    return m, first


def _pack_bf16_pairs(x):
    n = x.shape[1] // 2
    lo = pltpu.bitcast(x[:, :n].astype(BF16).astype(F32), jnp.int32)
    hi = pltpu.bitcast(x[:, n:].astype(BF16).astype(F32), jnp.int32)
    return hi | ((lo >> 16) & 0xFFFF)


def _unpack_bf16_pairs(w):
    lo = pltpu.bitcast(w << 16, F32)
    hi = pltpu.bitcast(w & jnp.int32(-65536), F32)
    return jnp.concatenate([lo, hi], axis=1)


def _router_kernel(x_ref, g_ref, mod_ref, rw_ref, rb_ref, tri_ref, h_ref, gate_ref, rank_ref, cnt_ref,
                   run_ref, *, d, n_exp):
    mod = mod_ref[...]
    h = _norm_mod(x_ref[...], g_ref[...], mod[:, :d], mod[:, d:2 * d])
    h_ref[...] = _pack_bf16_pairs(h)
    tm = h.shape[0]
    hh, hl = _split_bf16(h)
    rw = rw_ref[...]
    rh, rl = _split_bf16(rw)
    logits = _dot_nt(rh, hh) + (_dot_nt(rh, hl) + _dot_nt(rl, hh))
    scores = jax.nn.sigmoid(logits)
    sel = scores + rb_ref[...]
    gsz = n_exp // N_EXPERT_GROUPS
    neg = -jnp.inf
    shape3 = (N_EXPERT_GROUPS, gsz, tm)
    sel3 = sel.reshape(shape3)
    sub = lax.broadcasted_iota(jnp.int32, shape3, 1)
    m1 = jnp.max(sel3, axis=1, keepdims=True)
    f1 = jnp.min(jnp.where(sel3 == m1, sub, gsz), axis=1, keepdims=True)
    m2 = jnp.max(jnp.where(sub == f1, neg, sel3), axis=1, keepdims=True)
    gscore = jnp.broadcast_to(m1 + m2, shape3).reshape(n_exp, tm)
    eidx = lax.broadcasted_iota(jnp.int32, (n_exp, tm), 0)
    gidx = eidx // gsz
    keep = jnp.zeros((n_exp, tm), F32)
    work = gscore
    for _ in range(TOPK_GROUPS):
        _, first = _first_max(work, gidx, N_EXPERT_GROUPS)
        hit = gidx == first
        keep = jnp.where(hit, 1.0, keep)
        work = jnp.where(hit, neg, work)
    work = jnp.where(keep > 0.0, sel, neg)
    chosen = jnp.zeros((n_exp, tm), F32)
    for _ in range(TOP_K):
        _, first = _first_max(work, eidx, n_exp)
        hit = eidx == first
        chosen = jnp.where(hit, 1.0, chosen)
        work = jnp.where(hit, neg, work)
    picked = chosen * scores
    gate_ref[...] = picked / jnp.sum(picked, axis=0, keepdims=True) * ROUTED_SCALE

    @pl.when(pl.program_id(0) == 0)
    def _():
        run_ref[...] = jnp.zeros(run_ref.shape, F32)

    before = _dot(chosen.astype(BF16), tri_ref[...])
    run = run_ref[...]
    rank_ref[...] = jnp.where(chosen > 0.0, before + run[:, :1], -1.0)
    run = run + jnp.broadcast_to(jnp.sum(chosen, axis=1, keepdims=True), run.shape)
    run_ref[...] = run
    cnt_ref[...] = run


MOE_TILE = 512
ROUTER_TILE = 1024


def _ffn_router(x2, g, mod, layer, router_w, router_bias, seq, tile0, ntiles):
    d = x2.shape[1]
    n_exp = router_w.shape[1]
    tm = ROUTER_TILE
    t = ntiles * tm
    tiles_per_seq = seq // tm
    tri = (np.arange(tm)[:, None] < np.arange(tm)[None, :]).astype(np.float32)
    return pl.pallas_call(
        functools.partial(_router_kernel, d=d, n_exp=n_exp),
        grid=(ntiles,),
        in_specs=[pl.BlockSpec((tm, d), lambda i: (i + tile0, 0)),
                  pl.BlockSpec((1, d), lambda i: (0, 0)),
                  pl.BlockSpec((None, None, 1, 3 * d), lambda i: (layer, (i + tile0) // tiles_per_seq, 0, 0)),
                  pl.BlockSpec((n_exp, d), lambda i: (0, 0)),
                  pl.BlockSpec((n_exp, 1), lambda i: (0, 0)),
                  pl.BlockSpec((tm, tm), lambda i: (0, 0))],
        out_specs=[pl.BlockSpec((tm, d // 2), lambda i: (i, 0)),
                   pl.BlockSpec((n_exp, tm), lambda i: (0, i)),
                   pl.BlockSpec((n_exp, tm), lambda i: (0, i)),
                   pl.BlockSpec((n_exp, 128), lambda i: (0, 0))],
        out_shape=[jax.ShapeDtypeStruct((t, d // 2), jnp.int32),
                   jax.ShapeDtypeStruct((n_exp, t), F32),
                   jax.ShapeDtypeStruct((n_exp, t), F32),
                   jax.ShapeDtypeStruct((n_exp, 128), F32)],
        scratch_shapes=[pltpu.VMEM((n_exp, 128), F32)],
        compiler_params=_cparams(("arbitrary",)),
        name="ffn_router",
    )(x2, g.reshape(1, d), mod, router_w.T, router_bias.reshape(n_exp, 1), jnp.asarray(tri, BF16))


def _assign_kernel(rank_ref, gate_ref, start_ref, pos_ref, w_ref, *, n_exp):
    rank = rank_ref[...]
    gates = gate_ref[...]
    tm = rank.shape[1]
    slot = rank + start_ref[...]
    eidx = lax.broadcasted_iota(jnp.int32, (n_exp, tm), 0).astype(F32)
    alive = jnp.where(rank >= 0.0, eidx, float(n_exp))
    kidx = lax.broadcasted_iota(jnp.int32, (TOP_K, tm), 0)
    pos = jnp.zeros((TOP_K, tm), F32)
    wts = jnp.zeros((TOP_K, tm), F32)
    for k in range(TOP_K):
        first = jnp.min(alive, axis=0, keepdims=True)
        hit = alive == first
        pos_k = jnp.sum(jnp.where(hit, slot, 0.0), axis=0, keepdims=True)
        w_k = jnp.sum(jnp.where(hit, gates, 0.0), axis=0, keepdims=True)
        pos = jnp.where(kidx == k, pos_k, pos)
        wts = jnp.where(kidx == k, w_k, wts)
        alive = jnp.where(hit, float(n_exp), alive)
    pos_ref[...] = pos.astype(jnp.int32)
    w_ref[...] = wts


def _assign_slots(ranks, gates, start):
    n_exp, t = ranks.shape
    tm = min(2048, t)
    return pl.pallas_call(
        functools.partial(_assign_kernel, n_exp=n_exp),
        grid=(t // tm,),
        in_specs=[pl.BlockSpec((n_exp, tm), lambda i: (0, i)),
                  pl.BlockSpec((n_exp, tm), lambda i: (0, i)),
                  pl.BlockSpec((n_exp, 1), lambda i: (0, 0))],
        out_specs=[pl.BlockSpec((TOP_K, tm), lambda i: (0, i)),
                   pl.BlockSpec((TOP_K, tm), lambda i: (0, i))],
        out_shape=[jax.ShapeDtypeStruct((TOP_K, t), jnp.int32), jax.ShapeDtypeStruct((TOP_K, t), F32)],
        compiler_params=_cparams(("arbitrary",)),
        name="moe_assign",
    )(ranks, gates, start)


SC_CORES = 2
SC_SUBCORES = 16
SC_WINDOW = 128


def _sc_mesh():
    return plsc.VectorSubcoreMesh(core_axis_name="c", subcore_axis_name="s",
                                  num_cores=SC_CORES, num_subcores=SC_SUBCORES)


def _sc_scatter_rows(rows, pos, n_out):
    t, width = rows.shape
    nk = pos.shape[0]
    win = SC_WINDOW
    n_win = t // win // (SC_CORES * SC_SUBCORES)
    pos_w = pos.reshape(nk, t // win, win).transpose(1, 0, 2)

    def body(rows_hbm, pos_hbm, out_hbm, idx_v, buf_v, sem):
        wid = lax.axis_index("s") * SC_CORES + lax.axis_index("c")

        @pl.loop(0, n_win)
        def _(j):
            w = wid * n_win + j
            pltpu.sync_copy(rows_hbm.at[pl.ds(w * win, win)], buf_v)
            pltpu.sync_copy(pos_hbm.at[w], idx_v)
            copies = [pltpu.make_async_copy(buf_v, out_hbm.at[idx_v.at[k]], sem) for k in range(nk)]
            for cp in copies:
                cp.start()
            for cp in copies:
                cp.wait()

    return pl.kernel(
        body, out_type=jax.ShapeDtypeStruct((n_out, width), jnp.int32), mesh=_sc_mesh(),
        scratch_types=[pltpu.VMEM((nk, win), jnp.int32), pltpu.VMEM((win, width), jnp.int32),
                       pltpu.SemaphoreType.DMA],
        name="sc_scatter_rows",
    )(rows, pos_w)


SC_GATHER_WINDOW = 64


def _sc_gather_rows(table, pos_flat):
    m = pos_flat.shape[0]
    width = table.shape[1]
    win = SC_GATHER_WINDOW
    workers = SC_CORES * SC_SUBCORES
    n_win = m // win // workers
    pos_w = pos_flat.reshape(workers, n_win, win)

    def body(table_hbm, pos_hbm, out_hbm, idx_v, buf_v, sem_g, sem_w):
        wid = lax.axis_index("s") * SC_CORES + lax.axis_index("c")
        base = wid * (n_win * win)
        pltpu.sync_copy(pos_hbm.at[wid], idx_v)

        def gather(j, b):
            return pltpu.make_async_copy(table_hbm.at[idx_v.at[j]], buf_v.at[b], sem_g.at[b])

        def write(j, b):
            return pltpu.make_async_copy(buf_v.at[b], out_hbm.at[pl.ds(base + j * win, win)], sem_w.at[b])

        gather(0, 0).start()

        @pl.loop(0, n_win, step=2)
        def _(j0):
            for b in range(2):
                j = j0 + b
                gather(j, b).wait()
                write(j, b).start()

                @pl.when(j + 1 < n_win)
                def _():
                    @pl.when(j >= 1)
                    def _():
                        write(j - 1, 1 - b).wait()
                    gather(j + 1, 1 - b).start()

        write(n_win - 2, 0).wait()
        write(n_win - 1, 1).wait()

    return pl.kernel(
        body, out_type=jax.ShapeDtypeStruct((m, width), jnp.int32), mesh=_sc_mesh(),
        scratch_types=[pltpu.VMEM((n_win, win), jnp.int32), pltpu.VMEM((2, win, width), jnp.int32),
                       pltpu.SemaphoreType.DMA((2,)), pltpu.SemaphoreType.DMA((2,))],
        name="sc_gather_rows",
    )(table, pos_w)


MOE_BLOCK = 1024


def _silu_mul(a, b):
    return (a * jax.nn.sigmoid(a)) * b


def _expert_ffn_kernel(be_ref, nv_ref, xs_ref, wg_ref, wu_ref, wd_ref, ys_ref, wg_bf, wu_bf, wd_bf):
    i = pl.program_id(0)
    fresh = jnp.logical_or(i == 0, be_ref[i] != be_ref[jnp.maximum(i - 1, 0)])

    @pl.when(fresh)
    def _():
        wg_bf[...] = wg_ref[...].astype(BF16)
        wu_bf[...] = wu_ref[...].astype(BF16)
        wd_bf[...] = wd_ref[...].astype(BF16)

    nv = nv_ref[i]

    @pl.when(nv > 0)
    def _():
        x = _unpack_bf16_pairs(xs_ref[...])
        row = lax.broadcasted_iota(jnp.int32, (x.shape[0], 1), 0)
        x = jnp.where(row < nv, x, 0.0).astype(BF16)
        mid = _silu_mul(_dot(x, wg_bf[...]), _dot(x, wu_bf[...]))
        ys_ref[...] = _pack_bf16_pairs(_dot(mid.astype(BF16), wd_bf[...]))

    @pl.when(nv <= 0)
    def _():
        ys_ref[...] = jnp.zeros(ys_ref.shape, ys_ref.dtype)


def _expert_ffn(xs, block_expert, block_rows, layer, w_gate, w_up, w_down):
    n_rows, half = xs.shape
    _, n_exp, d, de = w_gate.shape
    nb = n_rows // MOE_BLOCK
    wmap = lambda i, be, nv: (layer, be[i], 0, 0)
    grid_spec = pltpu.PrefetchScalarGridSpec(
        num_scalar_prefetch=2,
        grid=(nb,),
        in_specs=[pl.BlockSpec((MOE_BLOCK, half), lambda i, be, nv: (jnp.where(nv[i] > 0, i, 0), 0)),
                  pl.BlockSpec((None, None, d, de), wmap),
                  pl.BlockSpec((None, None, d, de), wmap),
                  pl.BlockSpec((None, None, de, d), wmap)],
        out_specs=pl.BlockSpec((MOE_BLOCK, half), lambda i, be, nv: (i, 0)),
        scratch_shapes=[pltpu.VMEM((d, de), BF16), pltpu.VMEM((d, de), BF16), pltpu.VMEM((de, d), BF16)],
    )
    return pl.pallas_call(
        _expert_ffn_kernel,
        grid_spec=grid_spec,
        out_shape=jax.ShapeDtypeStruct((n_rows, half), jnp.int32),
        compiler_params=_cparams(("arbitrary",), VMEM_LIMIT),
        name="moe_expert_ffn",
    )(block_expert, block_rows, xs, w_gate, w_up, w_down)


def _combine_kernel(yt_ref, wt_ref, hp_ref, x_ref, mod_ref, sg_ref, su_ref, sd_ref, *rest, d):
    o_ref = rest[-1]
    h = _unpack_bf16_pairs(hp_ref[...]).astype(BF16)
    acc = _dot(_silu_mul(_dot(h, sg_ref[...]), _dot(h, su_ref[...])).astype(BF16), sd_ref[...])
    w = wt_ref[...]
    for k in range(TOP_K):
        acc = acc + w[:, k:k + 1] * _unpack_bf16_pairs(yt_ref[k])
    o_ref[...] = x_ref[...] + mod_ref[...][:, 2 * d:] * acc


def _combine(y_tok, w_tok, h_packed, x2, out_prev, mod, layer, sh_gate, sh_up, sh_down, seq, tile0):
    t_all, d = x2.shape
    ds = sh_gate.shape[1]
    tm = MOE_TILE
    ntiles = h_packed.shape[0] // tm
    tiles_per_seq = seq // tm
    row = lambda i: (i, 0)
    off = lambda i: (i + tile0, 0)
    const = lambda i: (0, 0)
    in_specs = [pl.BlockSpec((TOP_K, tm, d // 2), lambda i: (0, i, 0)),
                pl.BlockSpec((tm, TOP_K), row),
                pl.BlockSpec((tm, d // 2), row),
                pl.BlockSpec((tm, d), off),
                pl.BlockSpec((None, None, 1, 3 * d), lambda i: (layer, (i + tile0) // tiles_per_seq, 0, 0)),
                pl.BlockSpec((d, ds), const),
                pl.BlockSpec((d, ds), const),
                pl.BlockSpec((ds, d), const)]
    args = [y_tok, w_tok, h_packed, x2, mod, sh_gate.astype(BF16), sh_up.astype(BF16), sh_down.astype(BF16)]
    aliases = {}
    if out_prev is not None:
        in_specs.append(pl.BlockSpec(memory_space=pl.ANY))
        args.append(out_prev)
        aliases = {len(args) - 1: 0}
    return pl.pallas_call(
        functools.partial(_combine_kernel, d=d),
        grid=(ntiles,),
        in_specs=in_specs,
        out_specs=pl.BlockSpec((tm, d), off),
        out_shape=jax.ShapeDtypeStruct((t_all, d), F32),
        input_output_aliases=aliases,
        compiler_params=_cparams(("arbitrary",), VMEM_LIMIT),
        name="moe_combine",
    )(*args)


def _moe_group(x2, out_prev, tile0, h_packed, gates, ranks, counts, mod, layer, w_gate, w_up, w_down,
               sh_gate, sh_up, sh_down, seq):
    t, half = h_packed.shape
    d = 2 * half
    n_exp = w_gate.shape[1]
    cnt = counts[:, 0].astype(jnp.int32)
    padded = (cnt + MOE_BLOCK - 1) // MOE_BLOCK * MOE_BLOCK
    eidx = jnp.arange(n_exp, dtype=jnp.int32)
    seg_end = jnp.sum(jnp.where(eidx[None, :] <= eidx[:, None], padded[None, :], 0), axis=1)
    seg_start = seg_end - padded
    nb = t * TOP_K // MOE_BLOCK + n_exp
    first_row = jnp.arange(nb, dtype=jnp.int32) * MOE_BLOCK
    block_expert = jnp.sum((seg_end[None, :] <= first_row[:, None]).astype(jnp.int32), axis=1)
    block_expert = jnp.minimum(block_expert, n_exp - 1)
    last_row = jnp.sum(jnp.where(block_expert[:, None] == eidx[None, :], (seg_start + cnt)[None, :], 0), axis=1)
    block_rows = jnp.clip(last_row - first_row, 0, MOE_BLOCK).astype(jnp.int32)
    pos, w = _assign_slots(ranks, gates, seg_start.astype(F32).reshape(n_exp, 1))
    xs = _sc_scatter_rows(h_packed, pos, nb * MOE_BLOCK)
    ys = _expert_ffn(xs, block_expert, block_rows, layer, w_gate, w_up, w_down)
    y_tok = _sc_gather_rows(ys, pos.reshape(TOP_K * t)).reshape(TOP_K, t, d // 2)
    return _combine(y_tok, w.T, h_packed, x2, out_prev, mod, layer, sh_gate, sh_up, sh_down, seq, tile0)


MOE_GROUPS = 1


def _moe_layer(x2, g, mod, layer, router_w, router_bias, w_gate, w_up, w_down, sh_gate, sh_up, sh_down, seq):
    ntiles = x2.shape[0] // MOE_TILE // MOE_GROUPS
    rtiles = ntiles * MOE_TILE // ROUTER_TILE
    routed = [_ffn_router(x2, g, mod, layer, router_w, router_bias, seq, grp * rtiles, rtiles)
              for grp in range(MOE_GROUPS)]
    out = None
    for grp in range(MOE_GROUPS):
        out = _moe_group(x2, out, grp * ntiles, *routed[grp], mod, layer, w_gate, w_up, w_down,
                         sh_gate, sh_up, sh_down, seq)
    return out


def kernel(x, c, positions, mix_norm_g, mix_mod_w, mix_mod_b, ffn_norm_g, ffn_mod_w, ffn_mod_b,
           hyb_w_in, hyb_w_out, s5_lambda_re, s5_lambda_im, s5_log_dt, s5_b_re, s5_b_im,
           s5_c_re, s5_c_im, s5_d, s5_glu_w, s5_glu_b, attn_q_norm_g, attn_k_norm_g,
           pool_w, pool_scale, router_w, router_bias, exp_w_gate, exp_w_up, exp_w_down,
           sh_w_gate, sh_w_up, sh_w_down):
    bsz, seq, d = x.shape
    t = bsz * seq
    depth = mix_norm_g.shape[0]
    x2 = x.reshape(t, d)
    mix_mod = _mod_vectors(c, mix_mod_w, mix_mod_b).reshape(depth, bsz, 1, 3 * d)
    ffn_mod = _mod_vectors(c, ffn_mod_w, ffn_mod_b).reshape(depth, bsz, 1, 3 * d)
    for i in range(depth):
        j = i // 2
        if i % 2 == 0:
            u, q_hm, qi_hm, k, v, ki, wi = _in_projection(
                x2, mix_norm_g[i], mix_mod, i, hyb_w_in[j], positions, attn_q_norm_g[j], attn_k_norm_g[j], bsz, seq)
            weights = _s5_weights(s5_lambda_re[j], s5_lambda_im[j], s5_log_dt[j], s5_b_re[j], s5_b_im[j],
                                  s5_c_re[j], s5_c_im[j], s5_d[j], s5_glu_w[j], s5_glu_b[j])
            y_ssm = _s5_mixer(u, weights, bsz, seq)
            y_att = _dsa_attention(q_hm, qi_hm, wi, k, v, ki, bsz, seq)
            x2 = _out_projection(x2, y_ssm, y_att, hyb_w_out[j], mix_mod, i, seq)
        else:
            x2 = _pool_layer(x2, mix_norm_g[i], mix_mod, i, pool_w[j], pool_scale[j], seq)
        x2 = _moe_layer(x2, ffn_norm_g[i], ffn_mod, i, router_w[i], router_bias[i], exp_w_gate, exp_w_up, exp_w_down,
                        sh_w_gate[i], sh_w_up[i], sh_w_down[i], seq)
    return x2.reshape(bsz, seq, d)
```
